```python
import jax, jax.numpy as jnp
from jax import lax
import numpy as np

D_MODEL = 1024
BATCH = 8
SEQ = 4096
DEPTH = 1

MIX_WIDTH = D_MODEL
CONV_WIDTH = MIX_WIDTH // 2
CONV_GROUPS = 8
CONV_K = 3
GMLP_WIDTH = MIX_WIDTH - CONV_WIDTH
GMLP_HEADS = 8
GMLP_HEAD_DIM = GMLP_WIDTH // GMLP_HEADS
CHUNK = 128
IN_COLS = 3 * CONV_WIDTH + 2 * GMLP_WIDTH
MEM_LEN = 256
XA_HEADS = 4
XA_HEAD_DIM = D_MODEL // XA_HEADS
N_GROUPS = 4
EXPERTS_PER_GROUP = 8
N_EXPERTS = N_GROUPS * EXPERTS_PER_GROUP
TOP_K = 2
D_EXPERT = D_MODEL // 2
MOE_BLOCK = 128
EPS = 1e-6

kernel_name = "hymba_conv_gmlp_hmoe_layer"


def rmsnorm(x, g):
    xf = x.astype(jnp.float32)
    y = xf * lax.rsqrt(jnp.mean(xf * xf, axis=-1, keepdims=True) + EPS)
    return (y * g.astype(jnp.float32)).astype(x.dtype)


def hybrid_mixer(h, w_in, conv_w, g_v, w_s, b_s, g_out_conv, g_out_gmlp, w_out):
    B_, S_, _ = h.shape
    proj = h @ w_in
    gate_b, gate_c, xc, u, v = jnp.split(
        proj, [CONV_WIDTH, 2 * CONV_WIDTH, 3 * CONV_WIDTH, 3 * CONV_WIDTH + GMLP_WIDTH], axis=-1)

    z = gate_c * xc
    zc = lax.conv_general_dilated(
        z, conv_w[:, None, :].astype(z.dtype), window_strides=(1,),
        padding=[(CONV_K - 1, 0)], dimension_numbers=('NWC', 'WIO', 'NWC'),
        feature_group_count=CONV_WIDTH)
    y_a = gate_b * zc

    u = jax.nn.gelu(u)
    v = jax.nn.gelu(v).reshape(B_, S_, GMLP_HEADS, GMLP_HEAD_DIM)
    v = rmsnorm(v, g_v.reshape(GMLP_HEADS, GMLP_HEAD_DIM))
    v = v.reshape(B_, S_ // CHUNK, CHUNK, GMLP_HEADS, GMLP_HEAD_DIM)
    causal = jnp.tril(jnp.ones((CHUNK, CHUNK), dtype=bool))
    ws = jnp.where(causal[None], w_s, 0.0).astype(v.dtype)
    s = jnp.einsum('hts,bcshd->bcthd', ws, v) + b_s.T.astype(v.dtype)[:, :, None]
    y_b = u * s.reshape(B_, S_, GMLP_WIDTH)

    y = jnp.concatenate([rmsnorm(y_a, g_out_conv), rmsnorm(y_b, g_out_gmlp)], axis=-1)
    return y @ w_out


def memory_cross_attention(h, mem_n, w_q, w_k, w_v, w_o):
    B_, S_, _ = h.shape
    M_ = mem_n.shape[1]
    q = (h @ w_q).reshape(B_, S_, XA_HEADS, XA_HEAD_DIM)
    k = (mem_n @ w_k).reshape(B_, M_, XA_HEADS, XA_HEAD_DIM)
    v = (mem_n @ w_v).reshape(B_, M_, XA_HEADS, XA_HEAD_DIM)
    scores = jnp.einsum('bshd,bmhd->bhsm', q, k).astype(jnp.float32) * (XA_HEAD_DIM ** -0.5)
    p = jax.nn.softmax(scores, axis=-1).astype(v.dtype)
    o = jnp.einsum('bhsm,bmhd->bshd', p, v).reshape(B_, S_, D_MODEL)
    return o @ w_o


def hierarchical_moe(h, w_grp, b_grp, w_rt, b_rt, w_gate, w_up, w_down):
    B_, S_, D_ = h.shape
    T = B_ * S_
    xt = h.reshape(T, D_)
    grp_prob = jax.nn.softmax((xt @ w_grp).astype(jnp.float32) + b_grp.astype(jnp.float32), axis=-1)
    grp_p, grp_idx = lax.top_k(grp_prob, 1)
    exp_logits = ((xt @ w_rt).astype(jnp.float32) + b_rt.astype(jnp.float32)).reshape(
        T, N_GROUPS, EXPERTS_PER_GROUP)
    in_grp = jnp.take_along_axis(exp_logits, grp_idx[:, :, None], axis=1)[:, 0]
    top_logit, top_local = lax.top_k(in_grp, TOP_K)
    gates = jax.nn.softmax(top_logit, axis=-1) * grp_p
    expert = grp_idx * EXPERTS_PER_GROUP + top_local

    A = T * TOP_K
    e_flat = expert.reshape(A)
    g_flat = gates.reshape(A)
    tok = jnp.repeat(jnp.arange(T, dtype=jnp.int32), TOP_K)
    order = jnp.argsort(e_flat)
    e_sorted = e_flat[order]
    tok_sorted = tok[order]
    g_sorted = g_flat[order]
    counts = jnp.bincount(e_flat, length=N_EXPERTS)
    starts = jnp.cumsum(counts) - counts
    padded = ((counts + MOE_BLOCK - 1) // MOE_BLOCK) * MOE_BLOCK
    pad_ends = jnp.cumsum(padded)
    pad_starts = pad_ends - padded
    dest = pad_starts[e_sorted] + jnp.arange(A, dtype=jnp.int32) - starts[e_sorted]
    n_blocks = (A + MOE_BLOCK - 1) // MOE_BLOCK + N_EXPERTS
    cap = n_blocks * MOE_BLOCK
    buf = jnp.zeros((cap, D_), xt.dtype).at[dest].set(xt[tok_sorted])
    block_start = jnp.arange(n_blocks, dtype=jnp.int32) * MOE_BLOCK
    block_expert = jnp.minimum(jnp.searchsorted(pad_ends, block_start, side='right'), N_EXPERTS - 1)

    def expert_block(args):
        xb, e = args
        return (jax.nn.silu(xb @ w_gate[e]) * (xb @ w_up[e])) @ w_down[e]

    y_buf = lax.map(expert_block, (buf.reshape(n_blocks, MOE_BLOCK, D_), block_expert)).reshape(cap, D_)
    y_assign = y_buf[dest] * g_sorted[:, None].astype(y_buf.dtype)
    out = jnp.zeros((T, D_), y_buf.dtype).at[tok_sorted].add(y_assign)
    return out.reshape(B_, S_, D_)


def setup_inputs(seed: int = 0) -> dict:
    key = jax.random.key(seed)
    ks = jax.random.split(key, 32)
    f32 = jnp.float32

    def nrm(k, shape, fan_in):
        return jax.random.normal(k, shape, f32) * (fan_in ** -0.5)

    def gain(k, shape):
        return 1.0 + 0.01 * jax.random.normal(k, shape, f32)

    L = DEPTH
    return {
        "x": jax.random.normal(ks[0], (BATCH, SEQ, D_MODEL), f32),
        "mem": jax.random.normal(ks[1], (BATCH, MEM_LEN, D_MODEL), f32),
        "g_mix": gain(ks[2], (L, D_MODEL)),
        "w_in": nrm(ks[3], (L, D_MODEL, IN_COLS), D_MODEL),
        "conv_w": nrm(ks[4], (L, CONV_K, CONV_WIDTH), CONV_K),
        "g_v": gain(ks[5], (L, GMLP_WIDTH)),
        "w_s": nrm(ks[6], (L, GMLP_HEADS, CHUNK, CHUNK), CHUNK),
        "b_s": gain(ks[7], (L, GMLP_HEADS, CHUNK)),
        "g_out_conv": gain(ks[8], (L, CONV_WIDTH)),
        "g_out_gmlp": gain(ks[9], (L, GMLP_WIDTH)),
        "w_out": nrm(ks[10], (L, MIX_WIDTH, D_MODEL), MIX_WIDTH),
        "g_xattn": gain(ks[11], (L, D_MODEL)),
        "g_mem": gain(ks[12], (L, D_MODEL)),
        "w_q": nrm(ks[13], (L, D_MODEL, D_MODEL), D_MODEL),
        "w_k": nrm(ks[14], (L, D_MODEL, D_MODEL), D_MODEL),
        "w_v": nrm(ks[15], (L, D_MODEL, D_MODEL), D_MODEL),
        "w_o": nrm(ks[16], (L, D_MODEL, D_MODEL), D_MODEL),
        "g_ffn": gain(ks[17], (L, D_MODEL)),
        "w_grp": nrm(ks[18], (L, D_MODEL, N_GROUPS), D_MODEL),
        "b_grp": 0.01 * jax.random.normal(ks[19], (L, N_GROUPS), f32),
        "w_rt": nrm(ks[20], (L, D_MODEL, N_EXPERTS), D_MODEL),
        "b_rt": 0.01 * jax.random.normal(ks[21], (L, N_EXPERTS), f32),
        "w_gate": nrm(ks[22], (L, N_EXPERTS, D_MODEL, D_EXPERT), D_MODEL),
        "w_up": nrm(ks[23], (L, N_EXPERTS, D_MODEL, D_EXPERT), D_MODEL),
        "w_down": nrm(ks[24], (L, N_EXPERTS, D_EXPERT, D_MODEL), D_EXPERT),
        "g_final": gain(ks[25], (D_MODEL,)),
    }


def reference(x, mem, g_mix, w_in, conv_w, g_v, w_s, b_s, g_out_conv, g_out_gmlp, w_out,
              g_xattn, g_mem, w_q, w_k, w_v, w_o, g_ffn, w_grp, b_grp, w_rt, b_rt,
              w_gate, w_up, w_down, g_final):
    for l in range(DEPTH):
        x = x + hybrid_mixer(rmsnorm(x, g_mix[l]), w_in[l], conv_w[l], g_v[l], w_s[l], b_s[l],
                             g_out_conv[l], g_out_gmlp[l], w_out[l])
        x = x + memory_cross_attention(rmsnorm(x, g_xattn[l]), rmsnorm(mem, g_mem[l]),
                                       w_q[l], w_k[l], w_v[l], w_o[l])
        x = x + hierarchical_moe(rmsnorm(x, g_ffn[l]), w_grp[l], b_grp[l], w_rt[l], b_rt[l],
                                 w_gate[l], w_up[l], w_down[l])
    return rmsnorm(x, g_final)
```

```python
import functools

import jax
import jax.numpy as jnp
from jax import lax
from jax.experimental import pallas as pl
from jax.experimental.pallas import tpu as pltpu

EPS = 1e-6
CONV_GROUP_WIDTH = 512
GMLP_HEADS = 8
GMLP_HEAD_DIM = 64
CHUNK = 128
CONV_K = 3
XA_HEADS = 4
N_GROUPS = 4
EXPERTS_PER_GROUP = 8
N_EXPERTS = N_GROUPS * EXPERTS_PER_GROUP
TOP_K = 2
ROUTE_LANES = 128
EXPERT_LANE0 = N_GROUPS

TOKEN_TILE = 512
ROW_BLOCK = 256
COMBINE_TILE = 256
VMEM_LIMIT_BYTES = 56 * 1024 * 1024

_NEG = -1e30


def _rms(x, g):
    return x * lax.rsqrt(jnp.mean(x * x, axis=-1, keepdims=True) + EPS) * g


def _gelu_tanh(x):
    return 0.5 * x * (1.0 + jnp.tanh(0.7978845608028654 * (x + 0.044715 * (x * x * x))))


def _bdot(a, b):
    return jnp.dot(a.astype(jnp.bfloat16), b.astype(jnp.bfloat16),
                   preferred_element_type=jnp.float32)


def _kv_kernel(mem_ref, g_ref, wk_ref, wv_ref, kt_ref, v_ref):
    m = _rms(mem_ref[0], g_ref[...]).astype(jnp.bfloat16)
    k = jnp.dot(m, wk_ref[...], preferred_element_type=jnp.float32)
    v = jnp.dot(m, wv_ref[...], preferred_element_type=jnp.float32)
    kt_ref[0] = k.T.astype(jnp.bfloat16)
    v_ref[0] = v.astype(jnp.bfloat16)


def _kv_proj(mem, g_mem, w_k, w_v):
    b, m, d = mem.shape
    const = lambda i: (0, 0)
    return pl.pallas_call(
        _kv_kernel,
        grid=(b,),
        in_specs=[
            pl.BlockSpec((1, m, d), lambda i: (i, 0, 0)),
            pl.BlockSpec((1, d), const),
            pl.BlockSpec((d, d), const),
            pl.BlockSpec((d, d), const),
        ],
        out_specs=[
            pl.BlockSpec((1, d, m), lambda i: (i, 0, 0)),
            pl.BlockSpec((1, m, d), lambda i: (i, 0, 0)),
        ],
        out_shape=[
            jax.ShapeDtypeStruct((b, d, m), jnp.bfloat16),
            jax.ShapeDtypeStruct((b, m, d), jnp.bfloat16),
        ],
        compiler_params=pltpu.CompilerParams(
            dimension_semantics=("arbitrary",), vmem_limit_bytes=VMEM_LIMIT_BYTES),
        name="kv_proj",
    )(mem, g_mem, w_k, w_v)


def _mixer_kernel(x_ref, gmix_ref, win_ref, convw_ref, gv_ref, ws_ref, bias_ref,
                  goc_ref, gog_ref, wout_ref, hsum_ref, o_ref, zbuf_ref):
    ts = x_ref.shape[1]
    w = CONV_GROUP_WIDTH
    xt = x_ref[0]
    h = _rms(xt, gmix_ref[...]).astype(jnp.bfloat16)

    def proj(i):
        return jnp.dot(h, win_ref[:, i * w:(i + 1) * w], preferred_element_type=jnp.float32)

    z = proj(1) * proj(2)

    @pl.when(pl.program_id(1) == 0)
    def _():
        zbuf_ref[0:8, :] = jnp.zeros((8, w), jnp.float32)

    @pl.when(pl.program_id(1) != 0)
    def _():
        zbuf_ref[0:8, :] = zbuf_ref[ts:ts + 8, :]

    zbuf_ref[8:8 + ts, :] = z
    zc = (convw_ref[0:1, :] * zbuf_ref[6:6 + ts, :]
          + convw_ref[1:2, :] * zbuf_ref[7:7 + ts, :]
          + convw_ref[2:3, :] * z)
    ya = _rms(proj(0) * zc, goc_ref[...]).astype(jnp.bfloat16)

    u = _gelu_tanh(proj(3))
    v = _gelu_tanh(proj(4))
    v2 = v * v
    v2_hi = v2.astype(jnp.bfloat16)
    v2_lo = (v2 - v2_hi.astype(jnp.float32)).astype(jnp.bfloat16)
    ss = (jnp.dot(v2_hi, hsum_ref[...], preferred_element_type=jnp.float32)
          + jnp.dot(v2_lo, hsum_ref[...], preferred_element_type=jnp.float32))
    vn = (v * lax.rsqrt(ss * (1.0 / GMLP_HEAD_DIM) + EPS) * gv_ref[...]).astype(jnp.bfloat16)

    row = lax.broadcasted_iota(jnp.int32, (CHUNK, CHUNK), 0)
    col = lax.broadcasted_iota(jnp.int32, (CHUNK, CHUNK), 1)
    causal = row >= col
    ws = [jnp.where(causal, ws_ref[hd], 0.0).astype(jnp.bfloat16) for hd in range(GMLP_HEADS)]
    chunks = []
    for c in range(ts // CHUNK):
        vc = vn[c * CHUNK:(c + 1) * CHUNK, :]
        heads = [jnp.dot(ws[hd], vc[:, hd * GMLP_HEAD_DIM:(hd + 1) * GMLP_HEAD_DIM],
                         preferred_element_type=jnp.float32) for hd in range(GMLP_HEADS)]
        chunks.append(jnp.concatenate(heads, axis=1) + bias_ref[...])
    s = jnp.concatenate(chunks, axis=0)
    yb = _rms(u * s, gog_ref[...]).astype(jnp.bfloat16)

    o_ref[0] = (xt
                + jnp.dot(ya, wout_ref[0:w, :], preferred_element_type=jnp.float32)
                + jnp.dot(yb, wout_ref[w:2 * w, :], preferred_element_type=jnp.float32))


def _mixer(x, g_mix, w_in, conv_w, g_v, w_s, bias, g_oc, g_og, w_out, hsum):
    b, s, d = x.shape
    ts = min(TOKEN_TILE, s)
    const2 = lambda i, j: (0, 0)
    const3 = lambda i, j: (0, 0, 0)
    return pl.pallas_call(
        _mixer_kernel,
        grid=(b, s // ts),
        in_specs=[
            pl.BlockSpec((1, ts, d), lambda i, j: (i, j, 0)),
            pl.BlockSpec(g_mix.shape, const2),
            pl.BlockSpec(w_in.shape, const2),
            pl.BlockSpec(conv_w.shape, const2),
            pl.BlockSpec(g_v.shape, const2),
            pl.BlockSpec(w_s.shape, const3),
            pl.BlockSpec(bias.shape, const2),
            pl.BlockSpec(g_oc.shape, const2),
            pl.BlockSpec(g_og.shape, const2),
            pl.BlockSpec(w_out.shape, const2),
            pl.BlockSpec(hsum.shape, const2),
        ],
        out_specs=pl.BlockSpec((1, ts, d), lambda i, j: (i, j, 0)),
        out_shape=jax.ShapeDtypeStruct((b, s, d), jnp.float32),
        scratch_shapes=[pltpu.VMEM((ts + 8, CONV_GROUP_WIDTH), jnp.float32)],
        compiler_params=pltpu.CompilerParams(
            dimension_semantics=("arbitrary", "arbitrary"), vmem_limit_bytes=VMEM_LIMIT_BYTES),
        name="mixer",
    )(x, g_mix, w_in, conv_w, g_v, w_s, bias, g_oc, g_og, w_out, hsum)


def _attn_route_kernel(x_ref, kt_ref, v_ref, gx_ref, wq_ref, wo_ref, gf_ref, wr_ref, br_ref,
                       x2_ref, h3_ref, slab_ref, counts_ref, carry_ref):
    ts = x_ref.shape[1]
    d = x_ref.shape[2]
    hd = d // XA_HEADS
    first = (pl.program_id(0) == 0) & (pl.program_id(1) == 0)

    @pl.when(first)
    def _():
        carry_ref[...] = jnp.zeros_like(carry_ref)

    x1 = x_ref[0]
    h2 = _rms(x1, gx_ref[...]).astype(jnp.bfloat16)
    q = jnp.dot(h2, wq_ref[...], preferred_element_type=jnp.float32).astype(jnp.bfloat16)
    heads = []
    for a in range(XA_HEADS):
        sc = jnp.dot(q[:, a * hd:(a + 1) * hd], kt_ref[0, a * hd:(a + 1) * hd, :],
                     preferred_element_type=jnp.float32) * (hd ** -0.5)
        p = jnp.exp(sc - jnp.max(sc, axis=-1, keepdims=True))
        l = jnp.sum(p, axis=-1, keepdims=True)
        o = jnp.dot(p.astype(jnp.bfloat16), v_ref[0, :, a * hd:(a + 1) * hd],
                    preferred_element_type=jnp.float32)
        heads.append((o / l).astype(jnp.bfloat16))
    o = jnp.concatenate(heads, axis=1)
    x2 = x1 + jnp.dot(o, wo_ref[...], preferred_element_type=jnp.float32)
    x2_ref[0] = x2

    h3 = _rms(x2, gf_ref[...])
    h3_ref[0] = h3
    lg = jnp.dot(h3.astype(jnp.bfloat16), wr_ref[...],
                 preferred_element_type=jnp.float32) + br_ref[...]
    lane = lax.broadcasted_iota(jnp.int32, (ts, ROUTE_LANES), 1).astype(jnp.float32)
    big = float(ROUTE_LANES)
    is_g = lane < N_GROUPS
    gmax = jnp.max(jnp.where(is_g, lg, _NEG), axis=-1, keepdims=True)
    den = jnp.sum(jnp.where(is_g, jnp.exp(jnp.where(is_g, lg, _NEG) - gmax), 0.0),
                  axis=-1, keepdims=True)
    grp_p = 1.0 / den
    gidx = jnp.min(jnp.where(is_g & (lg == gmax), lane, big), axis=-1, keepdims=True)
    lo = EXPERT_LANE0 + EXPERTS_PER_GROUP * gidx
    in_g = (lane >= lo) & (lane < lo + EXPERTS_PER_GROUP)
    t1 = jnp.max(jnp.where(in_g, lg, _NEG), axis=-1, keepdims=True)
    i1 = jnp.min(jnp.where(in_g & (lg == t1), lane, big), axis=-1, keepdims=True)
    rest = in_g & (lane != i1)
    t2 = jnp.max(jnp.where(rest, lg, _NEG), axis=-1, keepdims=True)
    i2 = jnp.min(jnp.where(rest & (lg == t2), lane, big), axis=-1, keepdims=True)
    e21 = jnp.exp(t2 - t1)
    g1 = grp_p / (1.0 + e21)
    g2 = grp_p * e21 / (1.0 + e21)

    oh1 = lane == i1
    oh2 = lane == i2
    oh = jnp.where(oh1 | oh2, 1.0, 0.0)
    r = lax.broadcasted_iota(jnp.int32, (ts, ts), 0)
    c = lax.broadcasted_iota(jnp.int32, (ts, ts), 1)
    lower = jnp.where(r > c, 1.0, 0.0).astype(jnp.bfloat16)
    before = jnp.dot(lower, oh.astype(jnp.bfloat16),
                     preferred_element_type=jnp.float32) + carry_ref[...]
    r1 = jnp.sum(jnp.where(oh1, before, 0.0), axis=-1, keepdims=True)
    r2 = jnp.sum(jnp.where(oh2, before, 0.0), axis=-1, keepdims=True)
    carry_ref[...] = carry_ref[...] + jnp.sum(oh, axis=0, keepdims=True)
    counts_ref[...] = carry_ref[...]

    slab = jnp.where(lane == 0, i1 - EXPERT_LANE0, 0.0)
    for k, val in enumerate((i2 - EXPERT_LANE0, g1, g2, r1, r2), start=1):
        slab = jnp.where(lane == k, val, slab)
    slab_ref[...] = slab


def _attn_route(x1, kt, v, g_x, w_q, w_o, g_f, w_r, b_r):
    b, s, d = x1.shape
    m = v.shape[1]
    ts = min(TOKEN_TILE, s)
    nt = s // ts
    const2 = lambda i, j: (0, 0)
    return pl.pallas_call(
        _attn_route_kernel,
        grid=(b, nt),
        in_specs=[
            pl.BlockSpec((1, ts, d), lambda i, j: (i, j, 0)),
            pl.BlockSpec((1, d, m), lambda i, j: (i, 0, 0)),
            pl.BlockSpec((1, m, d), lambda i, j: (i, 0, 0)),
            pl.BlockSpec(g_x.shape, const2),
            pl.BlockSpec(w_q.shape, const2),
            pl.BlockSpec(w_o.shape, const2),
            pl.BlockSpec(g_f.shape, const2),
            pl.BlockSpec(w_r.shape, const2),
            pl.BlockSpec(b_r.shape, const2),
        ],
        out_specs=[
            pl.BlockSpec((1, ts, d), lambda i, j: (i, j, 0)),
            pl.BlockSpec((1, ts, d), lambda i, j: (i, j, 0)),
            pl.BlockSpec((ts, ROUTE_LANES), lambda i, j: (i * nt + j, 0)),
            pl.BlockSpec((1, ROUTE_LANES), const2),
        ],
        out_shape=[
            jax.ShapeDtypeStruct((b, s, d), jnp.float32),
            jax.ShapeDtypeStruct((b, s, d), jnp.float32),
            jax.ShapeDtypeStruct((b * s, ROUTE_LANES), jnp.float32),
            jax.ShapeDtypeStruct((1, ROUTE_LANES), jnp.float32),
        ],
        scratch_shapes=[pltpu.VMEM((1, ROUTE_LANES), jnp.float32)],
        compiler_params=pltpu.CompilerParams(
            dimension_semantics=("arbitrary", "arbitrary"), vmem_limit_bytes=VMEM_LIMIT_BYTES),
        name="attn_route",
    )(x1, kt, v, g_x, w_q, w_o, g_f, w_r, b_r)


def _dest_kernel(slab_ref, starts_ref, o_ref):
    slab = slab_ref[...]
    lane = lax.broadcasted_iota(jnp.int32, slab.shape, 1).astype(jnp.float32)
    starts = starts_ref[...]
    d1 = jnp.sum(jnp.where(lane == slab[:, 0:1] + EXPERT_LANE0, starts, 0.0),
                 axis=-1, keepdims=True) + slab[:, 4:5]
    d2 = jnp.sum(jnp.where(lane == slab[:, 1:2] + EXPERT_LANE0, starts, 0.0),
                 axis=-1, keepdims=True) + slab[:, 5:6]
    o_ref[...] = jnp.where(lane == 0, d1, jnp.where(lane == 1, d2, 0.0)).astype(jnp.int32)


def _dest_rows(slab, starts_row):
    t = slab.shape[0]
    tt = min(2048, t)
    return pl.pallas_call(
        _dest_kernel,
        grid=(t // tt,),
        in_specs=[pl.BlockSpec((tt, ROUTE_LANES), lambda i: (i, 0)),
                  pl.BlockSpec((1, ROUTE_LANES), lambda i: (0, 0))],
        out_specs=pl.BlockSpec((tt, ROUTE_LANES), lambda i: (i, 0)),
        out_shape=jax.ShapeDtypeStruct((t, ROUTE_LANES), jnp.int32),
        compiler_params=pltpu.CompilerParams(dimension_semantics=("arbitrary",)),
        name="dest_rows",
    )(slab, starts_row)


def _dispatch_kernel(zero_block_ref, dest_ref, h_ref, xs_ref, zero_ref, sem, zsem):
    ts = h_ref.shape[0]
    nb = zero_block_ref.shape[0]

    def block_copy(j):
        start = pl.multiple_of(j * ROW_BLOCK, ROW_BLOCK)
        return pltpu.make_async_copy(zero_ref, xs_ref.at[pl.ds(start, ROW_BLOCK)], zsem)

    @pl.when(pl.program_id(0) == 0)
    def _():
        zero_ref[...] = jnp.zeros_like(zero_ref)

        @pl.loop(0, nb)
        def _(j):
            @pl.when(zero_block_ref[j] > 0)
            def _():
                block_copy(j).start()

        @pl.loop(0, nb)
        def _(j):
            @pl.when(zero_block_ref[j] > 0)
            def _():
                block_copy(j).wait()

    def row_copy(r, k):
        return pltpu.make_async_copy(h_ref.at[pl.ds(r, 1)],
                                     xs_ref.at[pl.ds(dest_ref[TOP_K * r + k], 1)], sem)

    def issue(r, carry):
        for k in range(TOP_K):
            row_copy(r, k).start()
        return carry

    lax.fori_loop(0, ts, issue, 0, unroll=8)
    for _ in range(TOP_K):
        pltpu.make_async_copy(h_ref, xs_ref.at[pl.ds(0, ts)], sem).wait()


def _dispatch(h3, dest_flat, zero_block):
    t, d = h3.shape
    ts = min(TOKEN_TILE, t)
    cap = zero_block.shape[0] * ROW_BLOCK
    grid_spec = pltpu.PrefetchScalarGridSpec(
        num_scalar_prefetch=1,
        grid=(t // ts,),
        in_specs=[
            pl.BlockSpec((TOP_K * ts,), lambda i, zb: (i,), memory_space=pltpu.SMEM),
            pl.BlockSpec((ts, d), lambda i, zb: (i, 0)),
        ],
        out_specs=pl.BlockSpec(memory_space=pl.ANY),
        scratch_shapes=[pltpu.VMEM((ROW_BLOCK, d), jnp.float32),
                        pltpu.SemaphoreType.DMA, pltpu.SemaphoreType.DMA],
    )
    return pl.pallas_call(
        _dispatch_kernel,
        grid_spec=grid_spec,
        out_shape=jax.ShapeDtypeStruct((cap, d), jnp.float32),
        compiler_params=pltpu.CompilerParams(
            dimension_semantics=("arbitrary",), has_side_effects=True),
        name="dispatch",
    )(zero_block, dest_flat, h3)


def _expert_kernel(be_ref, nact_ref, xs_ref, wg_ref, wu_ref, wd_ref, y_ref, wgb, wub, wdb):
    j = pl.program_id(0)
    prev = be_ref[jnp.maximum(j - 1, 0)]
    new_expert = (j == 0) | (be_ref[j] != prev)

    @pl.when(new_expert)
    def _():
        wgb[...] = wg_ref[0].astype(jnp.bfloat16)
        wub[...] = wu_ref[0].astype(jnp.bfloat16)
        wdb[...] = wd_ref[0].astype(jnp.bfloat16)

    @pl.when(j < nact_ref[0])
    def _():
        xb = xs_ref[...].astype(jnp.bfloat16)
        g = jnp.dot(xb, wgb[...], preferred_element_type=jnp.float32)
        u = jnp.dot(xb, wub[...], preferred_element_type=jnp.float32)
        a = (g * (1.0 / (1.0 + jnp.exp(-g))) * u).astype(jnp.bfloat16)
        y_ref[...] = jnp.dot(a, wdb[...], preferred_element_type=jnp.float32)

    @pl.when(j >= nact_ref[0])
    def _():
        y_ref[...] = jnp.zeros_like(y_ref)


def _experts(xs, block_expert, nact, w_gate, w_up, w_down):
    cap, d = xs.shape
    de = w_gate.shape[2]
    nb = cap // ROW_BLOCK
    row_map = lambda j, be, na: (jnp.minimum(j, na[0] - 1), 0)
    grid_spec = pltpu.PrefetchScalarGridSpec(
        num_scalar_prefetch=2,
        grid=(nb,),
        in_specs=[
            pl.BlockSpec((ROW_BLOCK, d), row_map),
            pl.BlockSpec((1, d, de), lambda j, be, na: (be[j], 0, 0)),
            pl.BlockSpec((1, d, de), lambda j, be, na: (be[j], 0, 0)),
            pl.BlockSpec((1, de, d), lambda j, be, na: (be[j], 0, 0)),
        ],
        out_specs=pl.BlockSpec((ROW_BLOCK, d), lambda j, be, na: (j, 0)),
        scratch_shapes=[pltpu.VMEM((d, de), jnp.bfloat16),
                        pltpu.VMEM((d, de), jnp.bfloat16),
                        pltpu.VMEM((de, d), jnp.bfloat16)],
    )
    return pl.pallas_call(
        _expert_kernel,
        grid_spec=grid_spec,
        out_shape=jax.ShapeDtypeStruct((cap, d), jnp.float32),
        compiler_params=pltpu.CompilerParams(
            dimension_semantics=("arbitrary",), vmem_limit_bytes=VMEM_LIMIT_BYTES),
        name="experts",
    )(block_expert, nact, xs, w_gate, w_up, w_down)


def _combine_kernel(dest_ref, x_ref, slab_ref, g_ref, y_ref, o_ref, buf_ref, sem):
    tc = x_ref.shape[0]

    def issue(r, carry):
        for k in range(TOP_K):
            pltpu.make_async_copy(y_ref.at[pl.ds(dest_ref[TOP_K * r + k], 1)],
                                  buf_ref.at[k, pl.ds(r, 1)], sem).start()
        return carry

    lax.fori_loop(0, tc, issue, 0, unroll=8)
    for k in range(TOP_K):
        pltpu.make_async_copy(y_ref.at[pl.ds(0, tc)], buf_ref.at[k], sem).wait()
    slab = slab_ref[...]
    x3 = x_ref[...] + slab[:, 2:3] * buf_ref[0] + slab[:, 3:4] * buf_ref[1]
    o_ref[...] = _rms(x3, g_ref[...])


def _combine(x2, slab, dest_flat, y, g_final):
    t, d = x2.shape
    tc = min(COMBINE_TILE, t)
    grid_spec = pltpu.PrefetchScalarGridSpec(
        num_scalar_prefetch=0,
        grid=(t // tc,),
        in_specs=[
            pl.BlockSpec((TOP_K * tc,), lambda i: (i,), memory_space=pltpu.SMEM),
            pl.BlockSpec((tc, d), lambda i: (i, 0)),
            pl.BlockSpec((tc, ROUTE_LANES), lambda i: (i, 0)),
            pl.BlockSpec((1, d), lambda i: (0, 0)),
            pl.BlockSpec(memory_space=pl.ANY),
        ],
        out_specs=pl.BlockSpec((tc, d), lambda i: (i, 0)),
        scratch_shapes=[pltpu.VMEM((TOP_K, tc, d), jnp.float32), pltpu.SemaphoreType.DMA],
    )
    return pl.pallas_call(
        _combine_kernel,
        grid_spec=grid_spec,
        out_shape=jax.ShapeDtypeStruct((t, d), jnp.float32),
        compiler_params=pltpu.CompilerParams(dimension_semantics=("arbitrary",)),
        name="combine",
    )(dest_flat, x2, slab, g_final, y)


def _layer(x, kv, p):
    b, s, d = x.shape
    t = b * s
    bf = jnp.bfloat16
    kt, v = kv

    bias = jnp.repeat(p["b_s"].T, GMLP_HEAD_DIM, axis=1)
    head_of_lane = jnp.arange(GMLP_HEADS * GMLP_HEAD_DIM) // GMLP_HEAD_DIM
    hsum = (head_of_lane[:, None] == head_of_lane[None, :]).astype(bf)
    x1 = _mixer(x, p["g_mix"][None], p["w_in"].astype(bf), p["conv_w"], p["g_v"][None],
                p["w_s"], bias, p["g_out_conv"][None], p["g_out_gmlp"][None],
                p["w_out"].astype(bf), hsum)

    pad = ROUTE_LANES - N_GROUPS - N_EXPERTS
    w_r = jnp.concatenate([p["w_grp"], p["w_rt"], jnp.zeros((d, pad), jnp.float32)], axis=1)
    b_r = jnp.concatenate([p["b_grp"], p["b_rt"], jnp.zeros((pad,), jnp.float32)])[None]
    x2, h3, slab, counts_row = _attn_route(
        x1, kt, v, p["g_xattn"][None], p["w_q"].astype(bf), p["w_o"].astype(bf),
        p["g_ffn"][None], w_r.astype(bf), b_r)

    a = t * TOP_K
    nb = a // ROW_BLOCK + N_EXPERTS
    cap = nb * ROW_BLOCK
    counts = counts_row[0, EXPERT_LANE0:EXPERT_LANE0 + N_EXPERTS].astype(jnp.int32)
    padded = ((counts + ROW_BLOCK - 1) // ROW_BLOCK) * ROW_BLOCK
    pad_ends = jnp.cumsum(padded)
    pad_starts = pad_ends - padded
    nact = (pad_ends[-1] // ROW_BLOCK).astype(jnp.int32)
    block_start = jnp.arange(nb, dtype=jnp.int32) * ROW_BLOCK
    block_expert = jnp.minimum(
        jnp.sum((pad_ends[None, :] <= block_start[:, None]).astype(jnp.int32), axis=1),
        N_EXPERTS - 1)
    block_expert = jnp.where(jnp.arange(nb) < nact, block_expert, block_expert[nact - 1])
    starts_row = jnp.zeros((1, ROUTE_LANES), jnp.float32).at[
        0, EXPERT_LANE0:EXPERT_LANE0 + N_EXPERTS].set(pad_starts.astype(jnp.float32))
    last_of_expert = (block_start + ROW_BLOCK) == pad_ends[block_expert]
    zero_block = ((jnp.arange(nb) >= nact) | last_of_expert).astype(jnp.int32)

    dest = _dest_rows(slab, starts_row)
    dest_flat = dest[:, :TOP_K].reshape(a)

    xs = _dispatch(h3.reshape(t, d), dest_flat, zero_block)
    y = _experts(xs, block_expert, nact[None], p["w_gate"], p["w_up"], p["w_down"])
    return x2.reshape(t, d), slab, dest_flat, y


def kernel(x, mem, g_mix, w_in, conv_w, g_v, w_s, b_s, g_out_conv, g_out_gmlp, w_out, g_xattn,
           g_mem, w_q, w_k, w_v, w_o, g_ffn, w_grp, b_grp, w_rt, b_rt, w_gate, w_up, w_down,
           g_final):
    b, s, d = x.shape
    assert g_mix.shape[0] == 1, "the final norm is fused into the single layer's combine"
    bf = jnp.bfloat16
    p = dict(g_mix=g_mix[0], w_in=w_in[0], conv_w=conv_w[0], g_v=g_v[0], w_s=w_s[0],
             b_s=b_s[0], g_out_conv=g_out_conv[0], g_out_gmlp=g_out_gmlp[0],
             w_out=w_out[0], g_xattn=g_xattn[0], w_q=w_q[0], w_o=w_o[0], g_ffn=g_ffn[0],
             w_grp=w_grp[0], b_grp=b_grp[0], w_rt=w_rt[0], b_rt=b_rt[0],
             w_gate=w_gate[0], w_up=w_up[0], w_down=w_down[0])
    kv = _kv_proj(mem, g_mem[0][None], w_k[0].astype(bf), w_v[0].astype(bf))
    x2, slab, dest_flat, y = _layer(x, kv, p)
    return _combine(x2, slab, dest_flat, y, g_final[None]).reshape(b, s, d)
```

```python
import jax
import jax.numpy as jnp
from jax import lax
from jax.experimental import pallas as pl
from jax.experimental.pallas import tpu as pltpu

EPS = 1e-6
CONV_GROUP_WIDTH = 512
GMLP_HEADS = 8
GMLP_HEAD_DIM = 64
CHUNK = 128
CONV_K = 3
XA_HEADS = 4
N_GROUPS = 4
EXPERTS_PER_GROUP = 8
N_EXPERTS = N_GROUPS * EXPERTS_PER_GROUP
TOP_K = 2
ROUTE_LANES = 128
EXPERT_LANE0 = N_GROUPS

N_PAIRS = EXPERTS_PER_GROUP * (EXPERTS_PER_GROUP - 1) // 2
N_CLASSES = N_GROUPS * N_PAIRS
PAIRS = [(a, b) for a in range(EXPERTS_PER_GROUP) for b in range(a + 1, EXPERTS_PER_GROUP)]

TOKEN_TILE = 512
ROW_BLOCK = 128
COPY_TILE = 512
VMEM_LIMIT_BYTES = 56 * 1024 * 1024

_NEG = -1e30


def _rms(x, g):
    return x * lax.rsqrt(jnp.mean(x * x, axis=-1, keepdims=True) + EPS) * g


def _gelu_tanh(x):
    return 0.5 * x * (1.0 + jnp.tanh(0.7978845608028654 * (x + 0.044715 * (x * x * x))))


def _kv_kernel(mem_ref, g_ref, wk_ref, wv_ref, kt_ref, v_ref):
    m = _rms(mem_ref[0], g_ref[...]).astype(jnp.bfloat16)
    k = jnp.dot(m, wk_ref[...], preferred_element_type=jnp.float32)
    v = jnp.dot(m, wv_ref[...], preferred_element_type=jnp.float32)
    kt_ref[0] = k.T.astype(jnp.bfloat16)
    v_ref[0] = v.astype(jnp.bfloat16)


def _kv_proj(mem, g_mem, w_k, w_v):
    b, m, d = mem.shape
    const = lambda i: (0, 0)
    return pl.pallas_call(
        _kv_kernel,
        grid=(b,),
        in_specs=[
            pl.BlockSpec((1, m, d), lambda i: (i, 0, 0)),
            pl.BlockSpec((1, d), const),
            pl.BlockSpec((d, d), const),
            pl.BlockSpec((d, d), const),
        ],
        out_specs=[
            pl.BlockSpec((1, d, m), lambda i: (i, 0, 0)),
            pl.BlockSpec((1, m, d), lambda i: (i, 0, 0)),
        ],
        out_shape=[
            jax.ShapeDtypeStruct((b, d, m), jnp.bfloat16),
            jax.ShapeDtypeStruct((b, m, d), jnp.bfloat16),
        ],
        compiler_params=pltpu.CompilerParams(
            dimension_semantics=("arbitrary",), vmem_limit_bytes=VMEM_LIMIT_BYTES),
        name="kv_proj",
    )(mem, g_mem, w_k, w_v)


def _mixer_kernel(x_ref, gmix_ref, win_ref, convw_ref, gv_ref, ws_ref, bias_ref,
                  goc_ref, gog_ref, wout_ref, hsum_ref, o_ref, zbuf_ref):
    ts = x_ref.shape[1]
    w = CONV_GROUP_WIDTH
    xt = x_ref[0]
    h = _rms(xt, gmix_ref[...]).astype(jnp.bfloat16)

    def proj(i):
        return jnp.dot(h, win_ref[:, i * w:(i + 1) * w], preferred_element_type=jnp.float32)

    z = proj(1) * proj(2)

    @pl.when(pl.program_id(1) == 0)
    def _():
        zbuf_ref[0:8, :] = jnp.zeros((8, w), jnp.float32)

    @pl.when(pl.program_id(1) != 0)
    def _():
        zbuf_ref[0:8, :] = zbuf_ref[ts:ts + 8, :]

    zbuf_ref[8:8 + ts, :] = z
    zc = (convw_ref[0:1, :] * zbuf_ref[6:6 + ts, :]
          + convw_ref[1:2, :] * zbuf_ref[7:7 + ts, :]
          + convw_ref[2:3, :] * z)
    ya = _rms(proj(0) * zc, goc_ref[...]).astype(jnp.bfloat16)

    u = _gelu_tanh(proj(3))
    v = _gelu_tanh(proj(4))
    v2 = v * v
    v2_hi = v2.astype(jnp.bfloat16)
    v2_lo = (v2 - v2_hi.astype(jnp.float32)).astype(jnp.bfloat16)
    ss = (jnp.dot(v2_hi, hsum_ref[...], preferred_element_type=jnp.float32)
          + jnp.dot(v2_lo, hsum_ref[...], preferred_element_type=jnp.float32))
    vn = (v * lax.rsqrt(ss * (1.0 / GMLP_HEAD_DIM) + EPS) * gv_ref[...]).astype(jnp.bfloat16)

    row = lax.broadcasted_iota(jnp.int32, (CHUNK, CHUNK), 0)
    col = lax.broadcasted_iota(jnp.int32, (CHUNK, CHUNK), 1)
    causal = row >= col
    ws = [jnp.where(causal, ws_ref[hd], 0.0).astype(jnp.bfloat16) for hd in range(GMLP_HEADS)]
    chunks = []
    for c in range(ts // CHUNK):
        vc = vn[c * CHUNK:(c + 1) * CHUNK, :]
        heads = [jnp.dot(ws[hd], vc[:, hd * GMLP_HEAD_DIM:(hd + 1) * GMLP_HEAD_DIM],
                         preferred_element_type=jnp.float32) for hd in range(GMLP_HEADS)]
        chunks.append(jnp.concatenate(heads, axis=1) + bias_ref[...])
    s = jnp.concatenate(chunks, axis=0)
    yb = _rms(u * s, gog_ref[...]).astype(jnp.bfloat16)

    o_ref[0] = (xt
                + jnp.dot(ya, wout_ref[0:w, :], preferred_element_type=jnp.float32)
                + jnp.dot(yb, wout_ref[w:2 * w, :], preferred_element_type=jnp.float32))


def _mixer(x, g_mix, w_in, conv_w, g_v, w_s, bias, g_oc, g_og, w_out, hsum):
    b, s, d = x.shape
    ts = min(TOKEN_TILE, s)
    const2 = lambda i, j: (0, 0)
    const3 = lambda i, j: (0, 0, 0)
    return pl.pallas_call(
        _mixer_kernel,
        grid=(b, s // ts),
        in_specs=[
            pl.BlockSpec((1, ts, d), lambda i, j: (i, j, 0)),
            pl.BlockSpec(g_mix.shape, const2),
            pl.BlockSpec(w_in.shape, const2),
            pl.BlockSpec(conv_w.shape, const2),
            pl.BlockSpec(g_v.shape, const2),
            pl.BlockSpec(w_s.shape, const3),
            pl.BlockSpec(bias.shape, const2),
            pl.BlockSpec(g_oc.shape, const2),
            pl.BlockSpec(g_og.shape, const2),
            pl.BlockSpec(w_out.shape, const2),
            pl.BlockSpec(hsum.shape, const2),
        ],
        out_specs=pl.BlockSpec((1, ts, d), lambda i, j: (i, j, 0)),
        out_shape=jax.ShapeDtypeStruct((b, s, d), jnp.float32),
        scratch_shapes=[pltpu.VMEM((ts + 8, CONV_GROUP_WIDTH), jnp.float32)],
        compiler_params=pltpu.CompilerParams(
            dimension_semantics=("arbitrary", "arbitrary"), vmem_limit_bytes=VMEM_LIMIT_BYTES),
        name="mixer",
    )(x, g_mix, w_in, conv_w, g_v, w_s, bias, g_oc, g_og, w_out, hsum)


def _attn_route_kernel(x_ref, kt_ref, v_ref, gx_ref, wq_ref, wo_ref, gf_ref, wr_ref, br_ref,
                       x2_ref, slab_ref, counts_ref, carry_ref):
    ts = x_ref.shape[1]
    d = x_ref.shape[2]
    hd = d // XA_HEADS
    first = (pl.program_id(0) == 0) & (pl.program_id(1) == 0)

    @pl.when(first)
    def _():
        carry_ref[...] = jnp.zeros_like(carry_ref)

    x1 = x_ref[0]
    h2 = _rms(x1, gx_ref[...]).astype(jnp.bfloat16)
    q = jnp.dot(h2, wq_ref[...], preferred_element_type=jnp.float32).astype(jnp.bfloat16)
    heads = []
    for a in range(XA_HEADS):
        sc = jnp.dot(q[:, a * hd:(a + 1) * hd], kt_ref[0, a * hd:(a + 1) * hd, :],
                     preferred_element_type=jnp.float32) * (hd ** -0.5)
        p = jnp.exp(sc - jnp.max(sc, axis=-1, keepdims=True))
        l = jnp.sum(p, axis=-1, keepdims=True)
        o = jnp.dot(p.astype(jnp.bfloat16), v_ref[0, :, a * hd:(a + 1) * hd],
                    preferred_element_type=jnp.float32)
        heads.append((o / l).astype(jnp.bfloat16))
    o = jnp.concatenate(heads, axis=1)
    x2 = x1 + jnp.dot(o, wo_ref[...], preferred_element_type=jnp.float32)
    x2_ref[0] = x2

    h3 = _rms(x2, gf_ref[...]).astype(jnp.bfloat16)
    lg = jnp.dot(h3, wr_ref[...], preferred_element_type=jnp.float32) + br_ref[...]
    lane = lax.broadcasted_iota(jnp.int32, (ts, ROUTE_LANES), 1).astype(jnp.float32)
    big = float(ROUTE_LANES)
    is_g = lane < N_GROUPS
    gmax = jnp.max(jnp.where(is_g, lg, _NEG), axis=-1, keepdims=True)
    gidx = jnp.min(jnp.where(is_g & (lg == gmax), lane, big), axis=-1, keepdims=True)
    lo = EXPERT_LANE0 + EXPERTS_PER_GROUP * gidx
    in_g = (lane >= lo) & (lane < lo + EXPERTS_PER_GROUP)
    t1 = jnp.max(jnp.where(in_g, lg, _NEG), axis=-1, keepdims=True)
    i1 = jnp.min(jnp.where(in_g & (lg == t1), lane, big), axis=-1, keepdims=True)
    rest = in_g & (lane != i1)
    t2 = jnp.max(jnp.where(rest, lg, _NEG), axis=-1, keepdims=True)
    i2 = jnp.min(jnp.where(rest & (lg == t2), lane, big), axis=-1, keepdims=True)

    a = jnp.minimum(i1, i2) - lo
    b = jnp.maximum(i1, i2) - lo
    pair = a * (2 * EXPERTS_PER_GROUP - 1 - a) * 0.5 + (b - a - 1.0)
    cls = gidx * N_PAIRS + pair

    oh = lane == cls
    ohf = jnp.where(oh, 1.0, 0.0)
    r = lax.broadcasted_iota(jnp.int32, (ts, ts), 0)
    c = lax.broadcasted_iota(jnp.int32, (ts, ts), 1)
    lower = jnp.where(r > c, 1.0, 0.0).astype(jnp.bfloat16)
    before = jnp.dot(lower, ohf.astype(jnp.bfloat16),
                     preferred_element_type=jnp.float32) + carry_ref[...]
    rank = jnp.sum(jnp.where(oh, before, 0.0), axis=-1, keepdims=True)
    carry_ref[...] = carry_ref[...] + jnp.sum(ohf, axis=0, keepdims=True)
    counts_ref[...] = carry_ref[...]
    slab_ref[...] = jnp.where(lane == 0, cls, jnp.where(lane == 1, rank, 0.0))


def _attn_route(x1, kt, v, g_x, w_q, w_o, g_f, w_r, b_r):
    b, s, d = x1.shape
    m = v.shape[1]
    ts = min(TOKEN_TILE, s)
    nt = s // ts
    const2 = lambda i, j: (0, 0)
    return pl.pallas_call(
        _attn_route_kernel,
        grid=(b, nt),
        in_specs=[
            pl.BlockSpec((1, ts, d), lambda i, j: (i, j, 0)),
            pl.BlockSpec((1, d, m), lambda i, j: (i, 0, 0)),
            pl.BlockSpec((1, m, d), lambda i, j: (i, 0, 0)),
            pl.BlockSpec(g_x.shape, const2),
            pl.BlockSpec(w_q.shape, const2),
            pl.BlockSpec(w_o.shape, const2),
            pl.BlockSpec(g_f.shape, const2),
            pl.BlockSpec(w_r.shape, const2),
            pl.BlockSpec(b_r.shape, const2),
        ],
        out_specs=[
            pl.BlockSpec((1, ts, d), lambda i, j: (i, j, 0)),
            pl.BlockSpec((ts, ROUTE_LANES), lambda i, j: (i * nt + j, 0)),
            pl.BlockSpec((1, ROUTE_LANES), const2),
        ],
        out_shape=[
            jax.ShapeDtypeStruct((b, s, d), jnp.float32),
            jax.ShapeDtypeStruct((b * s, ROUTE_LANES), jnp.float32),
            jax.ShapeDtypeStruct((1, ROUTE_LANES), jnp.float32),
        ],
        scratch_shapes=[pltpu.VMEM((1, ROUTE_LANES), jnp.float32)],
        compiler_params=pltpu.CompilerParams(
            dimension_semantics=("arbitrary", "arbitrary"), vmem_limit_bytes=VMEM_LIMIT_BYTES),
        name="attn_route",
    )(x1, kt, v, g_x, w_q, w_o, g_f, w_r, b_r)


def _dest_kernel(slab_ref, starts_ref, o_ref):
    slab = slab_ref[...]
    lane = lax.broadcasted_iota(jnp.int32, slab.shape, 1).astype(jnp.float32)
    row = jnp.sum(jnp.where(lane == slab[:, 0:1], starts_ref[...], 0.0),
                  axis=-1, keepdims=True) + slab[:, 1:2]
    o_ref[...] = jnp.where(lane == 0, row, 0.0).astype(jnp.int32)


def _dest_rows(slab, starts_row):
    t = slab.shape[0]
    tt = min(2048, t)
    return pl.pallas_call(
        _dest_kernel,
        grid=(t // tt,),
        in_specs=[pl.BlockSpec((tt, ROUTE_LANES), lambda i: (i, 0)),
                  pl.BlockSpec((1, ROUTE_LANES), lambda i: (0, 0))],
        out_specs=pl.BlockSpec((tt, ROUTE_LANES), lambda i: (i, 0)),
        out_shape=jax.ShapeDtypeStruct((t, ROUTE_LANES), jnp.int32),
        compiler_params=pltpu.CompilerParams(dimension_semantics=("arbitrary",)),
        name="dest_rows",
    )(slab, starts_row)


def _wait_rows(src_ref, dst_ref, sem, n):
    pltpu.make_async_copy(src_ref.at[pl.ds(0, n)], dst_ref.at[pl.ds(0, n)], sem).wait()


def _dispatch_kernel(zero_block_ref, dest_ref, x_ref, xs_ref, zero_ref, sem, zsem):
    tc = dest_ref.shape[0]
    nb = zero_block_ref.shape[0]
    i = pl.program_id(0)

    def block_copy(j):
        start = pl.multiple_of(j * ROW_BLOCK, ROW_BLOCK)
        return pltpu.make_async_copy(zero_ref, xs_ref.at[pl.ds(start, ROW_BLOCK)], zsem)

    @pl.when(i == 0)
    def _():
        zero_ref[...] = jnp.zeros_like(zero_ref)

        @pl.loop(0, nb)
        def _(j):
            @pl.when(zero_block_ref[j] > 0)
            def _():
                block_copy(j).start()

        @pl.loop(0, nb)
        def _(j):
            @pl.when(zero_block_ref[j] > 0)
            def _():
                block_copy(j).wait()

    base = i * tc

    def issue(r, carry):
        pltpu.make_async_copy(x_ref.at[pl.ds(base + r, 1)],
                              xs_ref.at[pl.ds(dest_ref[r], 1)], sem).start()
        return carry

    lax.fori_loop(0, tc, issue, 0, unroll=8)

    @pl.when(i > 0)
    def _():
        _wait_rows(x_ref, xs_ref, sem, tc)

    @pl.when(i == pl.num_programs(0) - 1)
    def _():
        _wait_rows(x_ref, xs_ref, sem, tc)


def _dispatch(x2, dest, zero_block):
    t, d = x2.shape
    tc = min(COPY_TILE, t)
    cap = zero_block.shape[0] * ROW_BLOCK
    grid_spec = pltpu.PrefetchScalarGridSpec(
        num_scalar_prefetch=1,
        grid=(t // tc,),
        in_specs=[
            pl.BlockSpec((tc,), lambda i, zb: (i,), memory_space=pltpu.SMEM),
            pl.BlockSpec(memory_space=pl.ANY),
        ],
        out_specs=pl.BlockSpec(memory_space=pl.ANY),
        scratch_shapes=[pltpu.VMEM((ROW_BLOCK, d), jnp.float32),
                        pltpu.SemaphoreType.DMA, pltpu.SemaphoreType.DMA],
    )
    return pl.pallas_call(
        _dispatch_kernel,
        grid_spec=grid_spec,
        out_shape=jax.ShapeDtypeStruct((cap, d), jnp.float32),
        compiler_params=pltpu.CompilerParams(
            dimension_semantics=("arbitrary",), has_side_effects=True),
        name="dispatch",
    )(zero_block, dest, x2)


def _unsort_kernel(dest_ref, ys_ref, o_ref, sem):
    tc = dest_ref.shape[0]
    i = pl.program_id(0)
    base = i * tc

    def issue(r, carry):
        pltpu.make_async_copy(ys_ref.at[pl.ds(dest_ref[r], 1)],
                              o_ref.at[pl.ds(base + r, 1)], sem).start()
        return carry

    lax.fori_loop(0, tc, issue, 0, unroll=8)

    @pl.when(i > 0)
    def _():
        _wait_rows(ys_ref, o_ref, sem, tc)

    @pl.when(i == pl.num_programs(0) - 1)
    def _():
        _wait_rows(ys_ref, o_ref, sem, tc)


def _unsort(ys, dest, t):
    d = ys.shape[1]
    tc = min(COPY_TILE, t)
    return pl.pallas_call(
        _unsort_kernel,
        grid=(t // tc,),
        in_specs=[
            pl.BlockSpec((tc,), lambda i: (i,), memory_space=pltpu.SMEM),
            pl.BlockSpec(memory_space=pl.ANY),
        ],
        out_specs=pl.BlockSpec(memory_space=pl.ANY),
        out_shape=jax.ShapeDtypeStruct((t, d), jnp.float32),
        scratch_shapes=[pltpu.SemaphoreType.DMA],
        compiler_params=pltpu.CompilerParams(
            dimension_semantics=("arbitrary",), has_side_effects=True),
        name="unsort",
    )(dest, ys)


def _expert_kernel(grp_ref, ea_ref, eb_ref, nact_ref,
                   xs_ref, gf_ref, wr_ref, br_ref, gfin_ref, wg_hbm, wu_hbm, wd_hbm,
                   o_ref, wgb, wub, wdb, sg, su, sd, sem):
    j = pl.program_id(0)
    g = grp_ref[j]
    new_group = (j == 0) | (g != grp_ref[jnp.maximum(j - 1, 0)])

    def stage(k, slot):
        e = g * EXPERTS_PER_GROUP + k
        return (pltpu.make_async_copy(wg_hbm.at[e], sg.at[slot], sem.at[slot, 0]),
                pltpu.make_async_copy(wu_hbm.at[e], su.at[slot], sem.at[slot, 1]),
                pltpu.make_async_copy(wd_hbm.at[e], sd.at[slot], sem.at[slot, 2]))

    @pl.when(new_group & (j < nact_ref[0]))
    def _():
        for cp in stage(0, 0):
            cp.start()

        @pl.loop(0, EXPERTS_PER_GROUP)
        def _(k):
            slot = k % 2

            @pl.when(k + 1 < EXPERTS_PER_GROUP)
            def _():
                for cp in stage(k + 1, 1 - slot):
                    cp.start()

            for cp in stage(k, slot):
                cp.wait()
            wgb[k] = sg[slot].astype(jnp.bfloat16)
            wub[k] = su[slot].astype(jnp.bfloat16)
            wdb[k] = sd[slot].astype(jnp.bfloat16)

    @pl.when(j < nact_ref[0])
    def _():
        x2 = xs_ref[...]
        rows = x2.shape[0]
        h3 = _rms(x2, gf_ref[...]).astype(jnp.bfloat16)
        lg = jnp.dot(h3, wr_ref[...], preferred_element_type=jnp.float32) + br_ref[...]
        lane = lax.broadcasted_iota(jnp.int32, (rows, ROUTE_LANES), 1)
        is_g = lane < N_GROUPS
        gmax = jnp.max(jnp.where(is_g, lg, _NEG), axis=-1, keepdims=True)
        den = jnp.sum(jnp.where(is_g, jnp.exp(jnp.where(is_g, lg, _NEG) - gmax), 0.0),
                      axis=-1, keepdims=True)

        def pick(idx):
            return jnp.sum(jnp.where(lane == idx, lg, 0.0), axis=-1, keepdims=True)

        grp_p = jnp.exp(pick(g) - gmax) / den
        lo = EXPERT_LANE0 + EXPERTS_PER_GROUP * g
        la = pick(lo + ea_ref[j])
        lb = pick(lo + eb_ref[j])
        m = jnp.maximum(la, lb)
        pa = jnp.exp(la - m)
        pb = jnp.exp(lb - m)
        gate = (grp_p * pa / (pa + pb), grp_p * pb / (pa + pb))

        y = x2
        for k, gt in zip((ea_ref[j], eb_ref[j]), gate):
            gg = jnp.dot(h3, wgb[k], preferred_element_type=jnp.float32)
            uu = jnp.dot(h3, wub[k], preferred_element_type=jnp.float32)
            act = (gg * (1.0 / (1.0 + jnp.exp(-gg))) * uu * gt).astype(jnp.bfloat16)
            y = y + jnp.dot(act, wdb[k], preferred_element_type=jnp.float32)
        o_ref[...] = _rms(y, gfin_ref[...])

    @pl.when(j >= nact_ref[0])
    def _():
        o_ref[...] = jnp.zeros_like(o_ref)


def _experts(xs, blk_grp, blk_a, blk_b, nact, g_ffn, w_r, b_r, g_final, w_gate, w_up, w_down):
    cap, d = xs.shape
    de = w_gate.shape[2]
    nb = cap // ROW_BLOCK
    pre = lambda f: (lambda j, gr, ea, eb, na: f(j, na))
    const2 = pre(lambda j, na: (0, 0))
    hbm = pl.BlockSpec(memory_space=pl.ANY)
    grid_spec = pltpu.PrefetchScalarGridSpec(
        num_scalar_prefetch=4,
        grid=(nb,),
        in_specs=[
            pl.BlockSpec((ROW_BLOCK, d), pre(lambda j, na: (jnp.minimum(j, na[0] - 1), 0))),
            pl.BlockSpec(g_ffn.shape, const2),
            pl.BlockSpec(w_r.shape, const2),
            pl.BlockSpec(b_r.shape, const2),
            pl.BlockSpec(g_final.shape, const2),
            hbm, hbm, hbm,
        ],
        out_specs=pl.BlockSpec((ROW_BLOCK, d), pre(lambda j, na: (j, 0))),
        scratch_shapes=[
            pltpu.VMEM((EXPERTS_PER_GROUP, d, de), jnp.bfloat16),
            pltpu.VMEM((EXPERTS_PER_GROUP, d, de), jnp.bfloat16),
            pltpu.VMEM((EXPERTS_PER_GROUP, de, d), jnp.bfloat16),
            pltpu.VMEM((2, d, de), jnp.float32),
            pltpu.VMEM((2, d, de), jnp.float32),
            pltpu.VMEM((2, de, d), jnp.float32),
            pltpu.SemaphoreType.DMA((2, 3)),
        ],
    )
    return pl.pallas_call(
        _expert_kernel,
        grid_spec=grid_spec,
        out_shape=jax.ShapeDtypeStruct((cap, d), jnp.float32),
        compiler_params=pltpu.CompilerParams(
            dimension_semantics=("arbitrary",), vmem_limit_bytes=VMEM_LIMIT_BYTES),
        name="experts",
    )(blk_grp, blk_a, blk_b, nact, xs, g_ffn, w_r, b_r, g_final, w_gate, w_up, w_down)


def _moe_final(x2, slab, counts_row, g_ffn, w_r, b_r, g_final, w_gate, w_up, w_down):
    t, d = x2.shape
    nb = t // ROW_BLOCK + N_CLASSES
    counts = counts_row[0, :N_CLASSES].astype(jnp.int32)
    nblk = (counts + ROW_BLOCK - 1) // ROW_BLOCK
    blk_end = jnp.cumsum(nblk)
    blk_start = blk_end - nblk
    nact = blk_end[-1]
    j = jnp.arange(nb, dtype=jnp.int32)
    blk_class = jnp.minimum(
        jnp.sum((blk_end[None, :] <= j[:, None]).astype(jnp.int32), axis=1), N_CLASSES - 1)
    blk_class = jnp.where(j < nact, blk_class, blk_class[nact - 1])
    zero_block = ((j + 1 == blk_end[blk_class]) | (j >= nact)).astype(jnp.int32)
    pair_a = jnp.array([p[0] for p in PAIRS], jnp.int32)
    pair_b = jnp.array([p[1] for p in PAIRS], jnp.int32)
    blk_grp = blk_class // N_PAIRS
    blk_a = pair_a[blk_class % N_PAIRS]
    blk_b = pair_b[blk_class % N_PAIRS]
    starts_row = jnp.zeros((1, ROUTE_LANES), jnp.float32).at[0, :N_CLASSES].set(
        (blk_start * ROW_BLOCK).astype(jnp.float32))

    dest = _dest_rows(slab, starts_row)[:, 0]
    xs = _dispatch(x2, dest, zero_block)
    ys = _experts(xs, blk_grp, blk_a, blk_b, nact[None], g_ffn, w_r, b_r, g_final,
                  w_gate, w_up, w_down)
    return _unsort(ys, dest, t)


def kernel(x, mem, g_mix, w_in, conv_w, g_v, w_s, b_s, g_out_conv, g_out_gmlp, w_out, g_xattn,
           g_mem, w_q, w_k, w_v, w_o, g_ffn, w_grp, b_grp, w_rt, b_rt, w_gate, w_up, w_down,
           g_final):
    b, s, d = x.shape
    assert g_mix.shape[0] == 1, "the final norm is fused into the single layer's expert kernel"
    assert N_CLASSES <= ROUTE_LANES
    bf = jnp.bfloat16
    kt, v = _kv_proj(mem, g_mem[0][None], w_k[0].astype(bf), w_v[0].astype(bf))

    bias = jnp.repeat(b_s[0].T, GMLP_HEAD_DIM, axis=1)
    head_of_lane = jnp.arange(GMLP_HEADS * GMLP_HEAD_DIM) // GMLP_HEAD_DIM
    hsum = (head_of_lane[:, None] == head_of_lane[None, :]).astype(bf)
    x1 = _mixer(x, g_mix[0][None], w_in[0].astype(bf), conv_w[0], g_v[0][None], w_s[0], bias,
                g_out_conv[0][None], g_out_gmlp[0][None], w_out[0].astype(bf), hsum)

    pad = ROUTE_LANES - N_GROUPS - N_EXPERTS
    w_r = jnp.concatenate([w_grp[0], w_rt[0], jnp.zeros((d, pad), jnp.float32)], axis=1).astype(bf)
    b_r = jnp.concatenate([b_grp[0], b_rt[0], jnp.zeros((pad,), jnp.float32)])[None]
    x2, slab, counts_row = _attn_route(x1, kt, v, g_xattn[0][None], w_q[0].astype(bf),
                                       w_o[0].astype(bf), g_ffn[0][None], w_r, b_r)
    out = _moe_final(x2.reshape(b * s, d), slab, counts_row, g_ffn[0][None], w_r, b_r,
                     g_final[None], w_gate[0], w_up[0], w_down[0])
    return out.reshape(b, s, d)
```

```python
import jax
import jax.numpy as jnp
from jax import lax
from jax.experimental import pallas as pl
from jax.experimental.pallas import tpu as pltpu

EPS = 1e-6
CONV_GROUP_WIDTH = 512
GMLP_HEADS = 8
GMLP_HEAD_DIM = 64
CHUNK = 128
CONV_K = 3
XA_HEADS = 4
N_GROUPS = 4
EXPERTS_PER_GROUP = 8
N_EXPERTS = N_GROUPS * EXPERTS_PER_GROUP
TOP_K = 2
ROUTE_LANES = 128
EXPERT_LANE0 = N_GROUPS

N_PAIRS = EXPERTS_PER_GROUP * (EXPERTS_PER_GROUP - 1) // 2
N_CLASSES = N_GROUPS * N_PAIRS
PAIRS = [(a, b) for a in range(EXPERTS_PER_GROUP) for b in range(a + 1, EXPERTS_PER_GROUP)]

TOKEN_TILE = 512
ROW_BLOCK = 128
COPY_TILE = 512
VMEM_LIMIT_BYTES = 56 * 1024 * 1024

_NEG = -1e30


def _rms(x, g):
    return x * lax.rsqrt(jnp.mean(x * x, axis=-1, keepdims=True) + EPS) * g


def _gelu_tanh(x):
    return 0.5 * x * (1.0 + jnp.tanh(0.7978845608028654 * (x + 0.044715 * (x * x * x))))


def _kv_kernel(mem_ref, g_ref, wk_ref, wv_ref, kt_ref, v_ref):
    m = _rms(mem_ref[0], g_ref[...]).astype(jnp.bfloat16)
    k = jnp.dot(m, wk_ref[...], preferred_element_type=jnp.float32)
    v = jnp.dot(m, wv_ref[...], preferred_element_type=jnp.float32)
    kt_ref[0] = k.T.astype(jnp.bfloat16)
    v_ref[0] = v.astype(jnp.bfloat16)


def _kv_proj(mem, g_mem, w_k, w_v):
    b, m, d = mem.shape
    const = lambda i: (0, 0)
    return pl.pallas_call(
        _kv_kernel,
        grid=(b,),
        in_specs=[
            pl.BlockSpec((1, m, d), lambda i: (i, 0, 0)),
            pl.BlockSpec((1, d), const),
            pl.BlockSpec((d, d), const),
            pl.BlockSpec((d, d), const),
        ],
        out_specs=[
            pl.BlockSpec((1, d, m), lambda i: (i, 0, 0)),
            pl.BlockSpec((1, m, d), lambda i: (i, 0, 0)),
        ],
        out_shape=[
            jax.ShapeDtypeStruct((b, d, m), jnp.bfloat16),
            jax.ShapeDtypeStruct((b, m, d), jnp.bfloat16),
        ],
        compiler_params=pltpu.CompilerParams(
            dimension_semantics=("arbitrary",), vmem_limit_bytes=VMEM_LIMIT_BYTES),
        name="kv_proj",
    )(mem, g_mem, w_k, w_v)


def _mixer_kernel(x_ref, gmix_ref, win_ref, convw_ref, gv_ref, ws_ref, bias_ref,
                  goc_ref, gog_ref, wout_ref, hsum_ref, o_ref, zbuf_ref):
    ts = x_ref.shape[1]
    w = CONV_GROUP_WIDTH
    xt = x_ref[0]
    h = _rms(xt, gmix_ref[...]).astype(jnp.bfloat16)

    def proj(i):
        return jnp.dot(h, win_ref[:, i * w:(i + 1) * w], preferred_element_type=jnp.float32)

    z = proj(1) * proj(2)

    @pl.when(pl.program_id(1) == 0)
    def _():
        zbuf_ref[0:8, :] = jnp.zeros((8, w), jnp.float32)

    @pl.when(pl.program_id(1) != 0)
    def _():
        zbuf_ref[0:8, :] = zbuf_ref[ts:ts + 8, :]

    zbuf_ref[8:8 + ts, :] = z
    zc = (convw_ref[0:1, :] * zbuf_ref[6:6 + ts, :]
          + convw_ref[1:2, :] * zbuf_ref[7:7 + ts, :]
          + convw_ref[2:3, :] * z)
    ya = _rms(proj(0) * zc, goc_ref[...]).astype(jnp.bfloat16)

    u = _gelu_tanh(proj(3))
    v = _gelu_tanh(proj(4))
    v2 = v * v
    v2_hi = v2.astype(jnp.bfloat16)
    v2_lo = (v2 - v2_hi.astype(jnp.float32)).astype(jnp.bfloat16)
    ss = (jnp.dot(v2_hi, hsum_ref[...], preferred_element_type=jnp.float32)
          + jnp.dot(v2_lo, hsum_ref[...], preferred_element_type=jnp.float32))
    vn = (v * lax.rsqrt(ss * (1.0 / GMLP_HEAD_DIM) + EPS) * gv_ref[...]).astype(jnp.bfloat16)

    row = lax.broadcasted_iota(jnp.int32, (CHUNK, CHUNK), 0)
    col = lax.broadcasted_iota(jnp.int32, (CHUNK, CHUNK), 1)
    causal = row >= col
    ws = [jnp.where(causal, ws_ref[hd], 0.0).astype(jnp.bfloat16) for hd in range(GMLP_HEADS)]
    chunks = []
    for c in range(ts // CHUNK):
        vc = vn[c * CHUNK:(c + 1) * CHUNK, :]
        heads = [jnp.dot(ws[hd], vc[:, hd * GMLP_HEAD_DIM:(hd + 1) * GMLP_HEAD_DIM],
                         preferred_element_type=jnp.float32) for hd in range(GMLP_HEADS)]
        chunks.append(jnp.concatenate(heads, axis=1) + bias_ref[...])
    s = jnp.concatenate(chunks, axis=0)
    yb = _rms(u * s, gog_ref[...]).astype(jnp.bfloat16)

    o_ref[0] = (xt
                + jnp.dot(ya, wout_ref[0:w, :], preferred_element_type=jnp.float32)
                + jnp.dot(yb, wout_ref[w:2 * w, :], preferred_element_type=jnp.float32))


def _mixer(x, g_mix, w_in, conv_w, g_v, w_s, bias, g_oc, g_og, w_out, hsum):
    b, s, d = x.shape
    ts = min(TOKEN_TILE, s)
    const2 = lambda i, j: (0, 0)
    const3 = lambda i, j: (0, 0, 0)
    return pl.pallas_call(
        _mixer_kernel,
        grid=(b, s // ts),
        in_specs=[
            pl.BlockSpec((1, ts, d), lambda i, j: (i, j, 0)),
            pl.BlockSpec(g_mix.shape, const2),
            pl.BlockSpec(w_in.shape, const2),
            pl.BlockSpec(conv_w.shape, const2),
            pl.BlockSpec(g_v.shape, const2),
            pl.BlockSpec(w_s.shape, const3),
            pl.BlockSpec(bias.shape, const2),
            pl.BlockSpec(g_oc.shape, const2),
            pl.BlockSpec(g_og.shape, const2),
            pl.BlockSpec(w_out.shape, const2),
            pl.BlockSpec(hsum.shape, const2),
        ],
        out_specs=pl.BlockSpec((1, ts, d), lambda i, j: (i, j, 0)),
        out_shape=jax.ShapeDtypeStruct((b, s, d), jnp.float32),
        scratch_shapes=[pltpu.VMEM((ts + 8, CONV_GROUP_WIDTH), jnp.float32)],
        compiler_params=pltpu.CompilerParams(
            dimension_semantics=("arbitrary", "arbitrary"), vmem_limit_bytes=VMEM_LIMIT_BYTES),
        name="mixer",
    )(x, g_mix, w_in, conv_w, g_v, w_s, bias, g_oc, g_og, w_out, hsum)


def _attn_route_kernel(x_ref, kt_ref, v_ref, gx_ref, wq_ref, wo_ref, gf_ref, wr_ref, br_ref,
                       x2_ref, slab_ref, counts_ref, carry_ref):
    ts = x_ref.shape[1]
    d = x_ref.shape[2]
    hd = d // XA_HEADS
    first = (pl.program_id(0) == 0) & (pl.program_id(1) == 0)

    @pl.when(first)
    def _():
        carry_ref[...] = jnp.zeros_like(carry_ref)

    x1 = x_ref[0]
    h2 = _rms(x1, gx_ref[...]).astype(jnp.bfloat16)
    q = jnp.dot(h2, wq_ref[...], preferred_element_type=jnp.float32).astype(jnp.bfloat16)
    heads = []
    for a in range(XA_HEADS):
        sc = jnp.dot(q[:, a * hd:(a + 1) * hd], kt_ref[0, a * hd:(a + 1) * hd, :],
                     preferred_element_type=jnp.float32) * (hd ** -0.5)
        p = jnp.exp(sc - jnp.max(sc, axis=-1, keepdims=True))
        l = jnp.sum(p, axis=-1, keepdims=True)
        o = jnp.dot(p.astype(jnp.bfloat16), v_ref[0, :, a * hd:(a + 1) * hd],
                    preferred_element_type=jnp.float32)
        heads.append((o / l).astype(jnp.bfloat16))
    o = jnp.concatenate(heads, axis=1)
    x2 = x1 + jnp.dot(o, wo_ref[...], preferred_element_type=jnp.float32)
    x2_ref[0] = x2

    h3 = _rms(x2, gf_ref[...]).astype(jnp.bfloat16)
    lg = jnp.dot(h3, wr_ref[...], preferred_element_type=jnp.float32) + br_ref[...]
    lane = lax.broadcasted_iota(jnp.int32, (ts, ROUTE_LANES), 1).astype(jnp.float32)
    big = float(ROUTE_LANES)
    is_g = lane < N_GROUPS
    gmax = jnp.max(jnp.where(is_g, lg, _NEG), axis=-1, keepdims=True)
    gidx = jnp.min(jnp.where(is_g & (lg == gmax), lane, big), axis=-1, keepdims=True)
    lo = EXPERT_LANE0 + EXPERTS_PER_GROUP * gidx
    in_g = (lane >= lo) & (lane < lo + EXPERTS_PER_GROUP)
    t1 = jnp.max(jnp.where(in_g, lg, _NEG), axis=-1, keepdims=True)
    i1 = jnp.min(jnp.where(in_g & (lg == t1), lane, big), axis=-1, keepdims=True)
    rest = in_g & (lane != i1)
    t2 = jnp.max(jnp.where(rest, lg, _NEG), axis=-1, keepdims=True)
    i2 = jnp.min(jnp.where(rest & (lg == t2), lane, big), axis=-1, keepdims=True)

    a = jnp.minimum(i1, i2) - lo
    b = jnp.maximum(i1, i2) - lo
    pair = a * (2 * EXPERTS_PER_GROUP - 1 - a) * 0.5 + (b - a - 1.0)
    cls = gidx * N_PAIRS + pair

    oh = lane == cls
    ohf = jnp.where(oh, 1.0, 0.0)
    r = lax.broadcasted_iota(jnp.int32, (ts, ts), 0)
    c = lax.broadcasted_iota(jnp.int32, (ts, ts), 1)
    lower = jnp.where(r > c, 1.0, 0.0).astype(jnp.bfloat16)
    before = jnp.dot(lower, ohf.astype(jnp.bfloat16),
                     preferred_element_type=jnp.float32) + carry_ref[...]
    rank = jnp.sum(jnp.where(oh, before, 0.0), axis=-1, keepdims=True)
    carry_ref[...] = carry_ref[...] + jnp.sum(ohf, axis=0, keepdims=True)
    counts_ref[...] = carry_ref[...]
    slab_ref[...] = jnp.where(lane == 0, cls, jnp.where(lane == 1, rank, 0.0))


def _attn_route(x1, kt, v, g_x, w_q, w_o, g_f, w_r, b_r):
    b, s, d = x1.shape
    m = v.shape[1]
    ts = min(TOKEN_TILE, s)
    nt = s // ts
    const2 = lambda i, j: (0, 0)
    return pl.pallas_call(
        _attn_route_kernel,
        grid=(b, nt),
        in_specs=[
            pl.BlockSpec((1, ts, d), lambda i, j: (i, j, 0)),
            pl.BlockSpec((1, d, m), lambda i, j: (i, 0, 0)),
            pl.BlockSpec((1, m, d), lambda i, j: (i, 0, 0)),
            pl.BlockSpec(g_x.shape, const2),
            pl.BlockSpec(w_q.shape, const2),
            pl.BlockSpec(w_o.shape, const2),
            pl.BlockSpec(g_f.shape, const2),
            pl.BlockSpec(w_r.shape, const2),
            pl.BlockSpec(b_r.shape, const2),
        ],
        out_specs=[
            pl.BlockSpec((1, ts, d), lambda i, j: (i, j, 0)),
            pl.BlockSpec((ts, ROUTE_LANES), lambda i, j: (i * nt + j, 0)),
            pl.BlockSpec((1, ROUTE_LANES), const2),
        ],
        out_shape=[
            jax.ShapeDtypeStruct((b, s, d), jnp.float32),
            jax.ShapeDtypeStruct((b * s, ROUTE_LANES), jnp.float32),
            jax.ShapeDtypeStruct((1, ROUTE_LANES), jnp.float32),
        ],
        scratch_shapes=[pltpu.VMEM((1, ROUTE_LANES), jnp.float32)],
        compiler_params=pltpu.CompilerParams(
            dimension_semantics=("arbitrary", "arbitrary"), vmem_limit_bytes=VMEM_LIMIT_BYTES),
        name="attn_route",
    )(x1, kt, v, g_x, w_q, w_o, g_f, w_r, b_r)


def _dest_kernel(slab_ref, starts_ref, o_ref):
    slab = slab_ref[...]
    lane = lax.broadcasted_iota(jnp.int32, slab.shape, 1).astype(jnp.float32)
    row = jnp.sum(jnp.where(lane == slab[:, 0:1], starts_ref[...], 0.0),
                  axis=-1, keepdims=True) + slab[:, 1:2]
    o_ref[...] = jnp.where(lane == 0, row, 0.0).astype(jnp.int32)


def _dest_rows(slab, starts_row):
    t = slab.shape[0]
    tt = min(2048, t)
    return pl.pallas_call(
        _dest_kernel,
        grid=(t // tt,),
        in_specs=[pl.BlockSpec((tt, ROUTE_LANES), lambda i: (i, 0)),
                  pl.BlockSpec((1, ROUTE_LANES), lambda i: (0, 0))],
        out_specs=pl.BlockSpec((tt, ROUTE_LANES), lambda i: (i, 0)),
        out_shape=jax.ShapeDtypeStruct((t, ROUTE_LANES), jnp.int32),
        compiler_params=pltpu.CompilerParams(dimension_semantics=("arbitrary",)),
        name="dest_rows",
    )(slab, starts_row)


def _wait_rows(src_ref, dst_ref, sem, n):
    pltpu.make_async_copy(src_ref.at[pl.ds(0, n)], dst_ref.at[pl.ds(0, n)], sem).wait()


def _dispatch_kernel(zero_block_ref, dest_ref, x_ref, xs_ref, zero_ref, sem, zsem):
    tc = x_ref.shape[0]
    nb = zero_block_ref.shape[0]

    def block_copy(j):
        start = pl.multiple_of(j * ROW_BLOCK, ROW_BLOCK)
        return pltpu.make_async_copy(zero_ref, xs_ref.at[pl.ds(start, ROW_BLOCK)], zsem)

    @pl.when(pl.program_id(0) == 0)
    def _():
        zero_ref[...] = jnp.zeros_like(zero_ref)

        @pl.loop(0, nb)
        def _(j):
            @pl.when(zero_block_ref[j] > 0)
            def _():
                block_copy(j).start()

        @pl.loop(0, nb)
        def _(j):
            @pl.when(zero_block_ref[j] > 0)
            def _():
                block_copy(j).wait()

    def issue(r, carry):
        pltpu.make_async_copy(x_ref.at[pl.ds(r, 1)],
                              xs_ref.at[pl.ds(dest_ref[r], 1)], sem).start()
        return carry

    lax.fori_loop(0, tc, issue, 0, unroll=8)
    _wait_rows(x_ref, xs_ref, sem, tc)


def _dispatch(x2, dest, zero_block):
    t, d = x2.shape
    tc = min(COPY_TILE, t)
    cap = zero_block.shape[0] * ROW_BLOCK
    grid_spec = pltpu.PrefetchScalarGridSpec(
        num_scalar_prefetch=1,
        grid=(t // tc,),
        in_specs=[
            pl.BlockSpec((tc,), lambda i, zb: (i,), memory_space=pltpu.SMEM),
            pl.BlockSpec((tc, d), lambda i, zb: (i, 0)),
        ],
        out_specs=pl.BlockSpec(memory_space=pl.ANY),
        scratch_shapes=[pltpu.VMEM((ROW_BLOCK, d), jnp.float32),
                        pltpu.SemaphoreType.DMA, pltpu.SemaphoreType.DMA],
    )
    return pl.pallas_call(
        _dispatch_kernel,
        grid_spec=grid_spec,
        out_shape=jax.ShapeDtypeStruct((cap, d), jnp.float32),
        compiler_params=pltpu.CompilerParams(
            dimension_semantics=("arbitrary",), has_side_effects=True),
        name="dispatch",
    )(zero_block, dest, x2)


def _unsort_kernel(dest_ref, ys_ref, o_ref, sem):
    tc = o_ref.shape[0]

    def issue(r, carry):
        pltpu.make_async_copy(ys_ref.at[pl.ds(dest_ref[r], 1)],
                              o_ref.at[pl.ds(r, 1)], sem).start()
        return carry

    lax.fori_loop(0, tc, issue, 0, unroll=8)
    _wait_rows(ys_ref, o_ref, sem, tc)


def _unsort(ys, dest, t):
    d = ys.shape[1]
    tc = min(COPY_TILE, t)
    return pl.pallas_call(
        _unsort_kernel,
        grid=(t // tc,),
        in_specs=[
            pl.BlockSpec((tc,), lambda i: (i,), memory_space=pltpu.SMEM),
            pl.BlockSpec(memory_space=pl.ANY),
        ],
        out_specs=pl.BlockSpec((tc, d), lambda i: (i, 0)),
        out_shape=jax.ShapeDtypeStruct((t, d), jnp.float32),
        scratch_shapes=[pltpu.SemaphoreType.DMA],
        compiler_params=pltpu.CompilerParams(dimension_semantics=("arbitrary",)),
        name="unsort",
    )(dest, ys)


def _expert_kernel(grp_ref, ea_ref, eb_ref, nact_ref,
                   xs_ref, gf_ref, wr_ref, br_ref, gfin_ref, wg_hbm, wu_hbm, wd_hbm,
                   o_ref, wgb, wub, wdb, sg, su, sd, sem):
    j = pl.program_id(0)
    g = grp_ref[j]
    new_group = (j == 0) | (g != grp_ref[jnp.maximum(j - 1, 0)])

    def stage(k, slot):
        e = g * EXPERTS_PER_GROUP + k
        return (pltpu.make_async_copy(wg_hbm.at[e], sg.at[slot], sem.at[slot, 0]),
                pltpu.make_async_copy(wu_hbm.at[e], su.at[slot], sem.at[slot, 1]),
                pltpu.make_async_copy(wd_hbm.at[e], sd.at[slot], sem.at[slot, 2]))

    @pl.when(new_group & (j < nact_ref[0]))
    def _():
        for cp in stage(0, 0):
            cp.start()

        @pl.loop(0, EXPERTS_PER_GROUP)
        def _(k):
            slot = k % 2

            @pl.when(k + 1 < EXPERTS_PER_GROUP)
            def _():
                for cp in stage(k + 1, 1 - slot):
                    cp.start()

            for cp in stage(k, slot):
                cp.wait()
            wgb[k] = sg[slot].astype(jnp.bfloat16)
            wub[k] = su[slot].astype(jnp.bfloat16)
            wdb[k] = sd[slot].astype(jnp.bfloat16)

    @pl.when(j < nact_ref[0])
    def _():
        x2 = xs_ref[...]
        rows = x2.shape[0]
        h3 = _rms(x2, gf_ref[...]).astype(jnp.bfloat16)
        lg = jnp.dot(h3, wr_ref[...], preferred_element_type=jnp.float32) + br_ref[...]
        lane = lax.broadcasted_iota(jnp.int32, (rows, ROUTE_LANES), 1)
        is_g = lane < N_GROUPS
        gmax = jnp.max(jnp.where(is_g, lg, _NEG), axis=-1, keepdims=True)
        den = jnp.sum(jnp.where(is_g, jnp.exp(jnp.where(is_g, lg, _NEG) - gmax), 0.0),
                      axis=-1, keepdims=True)

        def pick(idx):
            return jnp.sum(jnp.where(lane == idx, lg, 0.0), axis=-1, keepdims=True)

        grp_p = jnp.exp(pick(g) - gmax) / den
        lo = EXPERT_LANE0 + EXPERTS_PER_GROUP * g
        la = pick(lo + ea_ref[j])
        lb = pick(lo + eb_ref[j])
        m = jnp.maximum(la, lb)
        pa = jnp.exp(la - m)
        pb = jnp.exp(lb - m)
        gate = (grp_p * pa / (pa + pb), grp_p * pb / (pa + pb))

        y = x2
        for k, gt in zip((ea_ref[j], eb_ref[j]), gate):
            gg = jnp.dot(h3, wgb[k], preferred_element_type=jnp.float32)
            uu = jnp.dot(h3, wub[k], preferred_element_type=jnp.float32)
            act = (gg * (1.0 / (1.0 + jnp.exp(-gg))) * uu * gt).astype(jnp.bfloat16)
            y = y + jnp.dot(act, wdb[k], preferred_element_type=jnp.float32)
        o_ref[...] = _rms(y, gfin_ref[...])

    @pl.when(j >= nact_ref[0])
    def _():
        o_ref[...] = jnp.zeros_like(o_ref)


def _experts(xs, blk_grp, blk_a, blk_b, nact, g_ffn, w_r, b_r, g_final, w_gate, w_up, w_down):
    cap, d = xs.shape
    de = w_gate.shape[2]
    nb = cap // ROW_BLOCK
    pre = lambda f: (lambda j, gr, ea, eb, na: f(j, na))
    const2 = pre(lambda j, na: (0, 0))
    hbm = pl.BlockSpec(memory_space=pl.ANY)
    grid_spec = pltpu.PrefetchScalarGridSpec(
        num_scalar_prefetch=4,
        grid=(nb,),
        in_specs=[
            pl.BlockSpec((ROW_BLOCK, d), pre(lambda j, na: (jnp.minimum(j, na[0] - 1), 0))),
            pl.BlockSpec(g_ffn.shape, const2),
            pl.BlockSpec(w_r.shape, const2),
            pl.BlockSpec(b_r.shape, const2),
            pl.BlockSpec(g_final.shape, const2),
            hbm, hbm, hbm,
        ],
        out_specs=pl.BlockSpec((ROW_BLOCK, d), pre(lambda j, na: (j, 0))),
        scratch_shapes=[
            pltpu.VMEM((EXPERTS_PER_GROUP, d, de), jnp.bfloat16),
            pltpu.VMEM((EXPERTS_PER_GROUP, d, de), jnp.bfloat16),
            pltpu.VMEM((EXPERTS_PER_GROUP, de, d), jnp.bfloat16),
            pltpu.VMEM((2, d, de), jnp.float32),
            pltpu.VMEM((2, d, de), jnp.float32),
            pltpu.VMEM((2, de, d), jnp.float32),
            pltpu.SemaphoreType.DMA((2, 3)),
        ],
    )
    return pl.pallas_call(
        _expert_kernel,
        grid_spec=grid_spec,
        out_shape=jax.ShapeDtypeStruct((cap, d), jnp.float32),
        compiler_params=pltpu.CompilerParams(
            dimension_semantics=("arbitrary",), vmem_limit_bytes=VMEM_LIMIT_BYTES),
        name="experts",
    )(blk_grp, blk_a, blk_b, nact, xs, g_ffn, w_r, b_r, g_final, w_gate, w_up, w_down)


def _moe_final(x2, slab, counts_row, g_ffn, w_r, b_r, g_final, w_gate, w_up, w_down):
    t, d = x2.shape
    nb = t // ROW_BLOCK + N_CLASSES
    counts = counts_row[0, :N_CLASSES].astype(jnp.int32)
    nblk = (counts + ROW_BLOCK - 1) // ROW_BLOCK
    blk_end = jnp.cumsum(nblk)
    blk_start = blk_end - nblk
    nact = blk_end[-1]
    j = jnp.arange(nb, dtype=jnp.int32)
    blk_class = jnp.minimum(
        jnp.sum((blk_end[None, :] <= j[:, None]).astype(jnp.int32), axis=1), N_CLASSES - 1)
    blk_class = jnp.where(j < nact, blk_class, blk_class[nact - 1])
    zero_block = ((j + 1 == blk_end[blk_class]) | (j >= nact)).astype(jnp.int32)
    pair_a = jnp.array([p[0] for p in PAIRS], jnp.int32)
    pair_b = jnp.array([p[1] for p in PAIRS], jnp.int32)
    blk_grp = blk_class // N_PAIRS
    blk_a = pair_a[blk_class % N_PAIRS]
    blk_b = pair_b[blk_class % N_PAIRS]
    starts_row = jnp.zeros((1, ROUTE_LANES), jnp.float32).at[0, :N_CLASSES].set(
        (blk_start * ROW_BLOCK).astype(jnp.float32))

    dest = _dest_rows(slab, starts_row)[:, 0]
    xs = _dispatch(x2, dest, zero_block)
    ys = _experts(xs, blk_grp, blk_a, blk_b, nact[None], g_ffn, w_r, b_r, g_final,
                  w_gate, w_up, w_down)
    return _unsort(ys, dest, t)


def kernel(x, mem, g_mix, w_in, conv_w, g_v, w_s, b_s, g_out_conv, g_out_gmlp, w_out, g_xattn,
           g_mem, w_q, w_k, w_v, w_o, g_ffn, w_grp, b_grp, w_rt, b_rt, w_gate, w_up, w_down,
           g_final):
    b, s, d = x.shape
    assert g_mix.shape[0] == 1, "the final norm is fused into the single layer's expert kernel"
    assert N_CLASSES <= ROUTE_LANES
    bf = jnp.bfloat16
    kt, v = _kv_proj(mem, g_mem[0][None], w_k[0].astype(bf), w_v[0].astype(bf))

    bias = jnp.repeat(b_s[0].T, GMLP_HEAD_DIM, axis=1)
    head_of_lane = jnp.arange(GMLP_HEADS * GMLP_HEAD_DIM) // GMLP_HEAD_DIM
    hsum = (head_of_lane[:, None] == head_of_lane[None, :]).astype(bf)
    x1 = _mixer(x, g_mix[0][None], w_in[0].astype(bf), conv_w[0], g_v[0][None], w_s[0], bias,
                g_out_conv[0][None], g_out_gmlp[0][None], w_out[0].astype(bf), hsum)

    pad = ROUTE_LANES - N_GROUPS - N_EXPERTS
    w_r = jnp.concatenate([w_grp[0], w_rt[0], jnp.zeros((d, pad), jnp.float32)], axis=1).astype(bf)
    b_r = jnp.concatenate([b_grp[0], b_rt[0], jnp.zeros((pad,), jnp.float32)])[None]
    x2, slab, counts_row = _attn_route(x1, kt, v, g_xattn[0][None], w_q[0].astype(bf),
                                       w_o[0].astype(bf), g_ffn[0][None], w_r, b_r)
    out = _moe_final(x2.reshape(b * s, d), slab, counts_row, g_ffn[0][None], w_r, b_r,
                     g_final[None], w_gate[0], w_up[0], w_down[0])
    return out.reshape(b, s, d)
```

```python
import functools

import jax
import jax.numpy as jnp
from jax import lax
from jax.experimental import pallas as pl
from jax.experimental.pallas import tpu as pltpu

EPS = 1e-6
CONV_GROUP_WIDTH = 512
GMLP_HEADS = 8
GMLP_HEAD_DIM = 64
CHUNK = 128
CONV_K = 3
XA_HEADS = 4
N_GROUPS = 4
EXPERTS_PER_GROUP = 8
N_EXPERTS = N_GROUPS * EXPERTS_PER_GROUP
TOP_K = 2
ROUTE_LANES = 128
EXPERT_LANE0 = N_GROUPS

N_PAIRS = EXPERTS_PER_GROUP * (EXPERTS_PER_GROUP - 1) // 2
N_CLASSES = N_GROUPS * N_PAIRS
PAIRS = [(a, b) for a in range(EXPERTS_PER_GROUP) for b in range(a + 1, EXPERTS_PER_GROUP)]

TOKEN_TILE = 512
ROW_BLOCK = 128
BLOCKS_PER_STEP = 2
COPY_TILE = 2048
VMEM_LIMIT_BYTES = 56 * 1024 * 1024

_NEG = -1e30


def _rms(x, g):
    return x * lax.rsqrt(jnp.mean(x * x, axis=-1, keepdims=True) + EPS) * g


def _gelu_tanh(x):
    return 0.5 * x * (1.0 + jnp.tanh(0.7978845608028654 * (x + 0.044715 * (x * x * x))))


def _kv_kernel(mem_ref, g_ref, wk_ref, wv_ref, kt_ref, v_ref):
    m = _rms(mem_ref[0], g_ref[...]).astype(jnp.bfloat16)
    k = jnp.dot(m, wk_ref[...], preferred_element_type=jnp.float32)
    v = jnp.dot(m, wv_ref[...], preferred_element_type=jnp.float32)
    kt_ref[0] = k.T.astype(jnp.bfloat16)
    v_ref[0] = v.astype(jnp.bfloat16)


def _kv_proj(mem, g_mem, w_k, w_v):
    b, m, d = mem.shape
    const = lambda i: (0, 0)
    return pl.pallas_call(
        _kv_kernel,
        grid=(b,),
        in_specs=[
            pl.BlockSpec((1, m, d), lambda i: (i, 0, 0)),
            pl.BlockSpec((1, d), const),
            pl.BlockSpec((d, d), const),
            pl.BlockSpec((d, d), const),
        ],
        out_specs=[
            pl.BlockSpec((1, d, m), lambda i: (i, 0, 0)),
            pl.BlockSpec((1, m, d), lambda i: (i, 0, 0)),
        ],
        out_shape=[
            jax.ShapeDtypeStruct((b, d, m), jnp.bfloat16),
            jax.ShapeDtypeStruct((b, m, d), jnp.bfloat16),
        ],
        compiler_params=pltpu.CompilerParams(
            dimension_semantics=("arbitrary",), vmem_limit_bytes=VMEM_LIMIT_BYTES),
        name="kv_proj",
    )(mem, g_mem, w_k, w_v)


def _mixer_kernel(x_ref, gmix_ref, win_ref, convw_ref, gv_ref, ws_ref, bias_ref,
                  goc_ref, gog_ref, wout_ref, hsum_ref, o_ref, zbuf_ref):
    ts = x_ref.shape[1]
    w = CONV_GROUP_WIDTH
    xt = x_ref[0]
    h = _rms(xt, gmix_ref[...]).astype(jnp.bfloat16)

    def proj(i):
        return jnp.dot(h, win_ref[:, i * w:(i + 1) * w], preferred_element_type=jnp.float32)

    z = proj(1) * proj(2)

    @pl.when(pl.program_id(1) == 0)
    def _():
        zbuf_ref[0:8, :] = jnp.zeros((8, w), jnp.float32)

    @pl.when(pl.program_id(1) != 0)
    def _():
        zbuf_ref[0:8, :] = zbuf_ref[ts:ts + 8, :]

    zbuf_ref[8:8 + ts, :] = z
    zc = (convw_ref[0:1, :] * zbuf_ref[6:6 + ts, :]
          + convw_ref[1:2, :] * zbuf_ref[7:7 + ts, :]
          + convw_ref[2:3, :] * z)
    ya = _rms(proj(0) * zc, goc_ref[...]).astype(jnp.bfloat16)

    u = _gelu_tanh(proj(3))
    v = _gelu_tanh(proj(4))
    v2 = v * v
    v2_hi = v2.astype(jnp.bfloat16)
    v2_lo = (v2 - v2_hi.astype(jnp.float32)).astype(jnp.bfloat16)
    ss = (jnp.dot(v2_hi, hsum_ref[...], preferred_element_type=jnp.float32)
          + jnp.dot(v2_lo, hsum_ref[...], preferred_element_type=jnp.float32))
    vn = (v * lax.rsqrt(ss * (1.0 / GMLP_HEAD_DIM) + EPS) * gv_ref[...]).astype(jnp.bfloat16)

    row = lax.broadcasted_iota(jnp.int32, (CHUNK, CHUNK), 0)
    col = lax.broadcasted_iota(jnp.int32, (CHUNK, CHUNK), 1)
    causal = row >= col
    ws = [jnp.where(causal, ws_ref[hd], 0.0).astype(jnp.bfloat16) for hd in range(GMLP_HEADS)]
    chunks = []
    for c in range(ts // CHUNK):
        vc = vn[c * CHUNK:(c + 1) * CHUNK, :]
        heads = [jnp.dot(ws[hd], vc[:, hd * GMLP_HEAD_DIM:(hd + 1) * GMLP_HEAD_DIM],
                         preferred_element_type=jnp.float32) for hd in range(GMLP_HEADS)]
        chunks.append(jnp.concatenate(heads, axis=1) + bias_ref[...])
    s = jnp.concatenate(chunks, axis=0)
    yb = _rms(u * s, gog_ref[...]).astype(jnp.bfloat16)

    o_ref[0] = (xt
                + jnp.dot(ya, wout_ref[0:w, :], preferred_element_type=jnp.float32)
                + jnp.dot(yb, wout_ref[w:2 * w, :], preferred_element_type=jnp.float32))


def _mixer(x, g_mix, w_in, conv_w, g_v, w_s, bias, g_oc, g_og, w_out, hsum):
    b, s, d = x.shape
    ts = min(TOKEN_TILE, s)
    const2 = lambda i, j: (0, 0)
    const3 = lambda i, j: (0, 0, 0)
    return pl.pallas_call(
        _mixer_kernel,
        grid=(b, s // ts),
        in_specs=[
            pl.BlockSpec((1, ts, d), lambda i, j: (i, j, 0)),
            pl.BlockSpec(g_mix.shape, const2),
            pl.BlockSpec(w_in.shape, const2),
            pl.BlockSpec(conv_w.shape, const2),
            pl.BlockSpec(g_v.shape, const2),
            pl.BlockSpec(w_s.shape, const3),
            pl.BlockSpec(bias.shape, const2),
            pl.BlockSpec(g_oc.shape, const2),
            pl.BlockSpec(g_og.shape, const2),
            pl.BlockSpec(w_out.shape, const2),
            pl.BlockSpec(hsum.shape, const2),
        ],
        out_specs=pl.BlockSpec((1, ts, d), lambda i, j: (i, j, 0)),
        out_shape=jax.ShapeDtypeStruct((b, s, d), jnp.float32),
        scratch_shapes=[pltpu.VMEM((ts + 8, CONV_GROUP_WIDTH), jnp.float32)],
        compiler_params=pltpu.CompilerParams(
            dimension_semantics=("arbitrary", "arbitrary"), vmem_limit_bytes=VMEM_LIMIT_BYTES),
        name="mixer",
    )(x, g_mix, w_in, conv_w, g_v, w_s, bias, g_oc, g_og, w_out, hsum)


def _attn_route_kernel(x_ref, kt_ref, v_ref, gx_ref, wq_ref, wo_ref, gf_ref, wr_ref, br_ref,
                       x2_ref, slab_ref, counts_ref, carry_ref):
    ts = x_ref.shape[1]
    d = x_ref.shape[2]
    hd = d // XA_HEADS
    first = (pl.program_id(0) == 0) & (pl.program_id(1) == 0)

    @pl.when(first)
    def _():
        carry_ref[...] = jnp.zeros_like(carry_ref)

    x1 = x_ref[0]
    h2 = _rms(x1, gx_ref[...]).astype(jnp.bfloat16)
    q = jnp.dot(h2, wq_ref[...], preferred_element_type=jnp.float32).astype(jnp.bfloat16)
    heads = []
    for a in range(XA_HEADS):
        sc = jnp.dot(q[:, a * hd:(a + 1) * hd], kt_ref[0, a * hd:(a + 1) * hd, :],
                     preferred_element_type=jnp.float32) * (hd ** -0.5)
        p = jnp.exp(sc - jnp.max(sc, axis=-1, keepdims=True))
        l = jnp.sum(p, axis=-1, keepdims=True)
        o = jnp.dot(p.astype(jnp.bfloat16), v_ref[0, :, a * hd:(a + 1) * hd],
                    preferred_element_type=jnp.float32)
        heads.append((o / l).astype(jnp.bfloat16))
    o = jnp.concatenate(heads, axis=1)
    x2 = x1 + jnp.dot(o, wo_ref[...], preferred_element_type=jnp.float32)
    x2_ref[0] = x2

    h3 = _rms(x2, gf_ref[...]).astype(jnp.bfloat16)
    lg = jnp.dot(h3, wr_ref[...], preferred_element_type=jnp.float32) + br_ref[...]
    lane = lax.broadcasted_iota(jnp.int32, (ts, ROUTE_LANES), 1).astype(jnp.float32)
    big = float(ROUTE_LANES)
    is_g = lane < N_GROUPS
    gmax = jnp.max(jnp.where(is_g, lg, _NEG), axis=-1, keepdims=True)
    gidx = jnp.min(jnp.where(is_g & (lg == gmax), lane, big), axis=-1, keepdims=True)
    lo = EXPERT_LANE0 + EXPERTS_PER_GROUP * gidx
    in_g = (lane >= lo) & (lane < lo + EXPERTS_PER_GROUP)
    t1 = jnp.max(jnp.where(in_g, lg, _NEG), axis=-1, keepdims=True)
    i1 = jnp.min(jnp.where(in_g & (lg == t1), lane, big), axis=-1, keepdims=True)
    rest = in_g & (lane != i1)
    t2 = jnp.max(jnp.where(rest, lg, _NEG), axis=-1, keepdims=True)
    i2 = jnp.min(jnp.where(rest & (lg == t2), lane, big), axis=-1, keepdims=True)

    a = jnp.minimum(i1, i2) - lo
    b = jnp.maximum(i1, i2) - lo
    pair = a * (2 * EXPERTS_PER_GROUP - 1 - a) * 0.5 + (b - a - 1.0)
    cls = gidx * N_PAIRS + pair

    oh = lane == cls
    ohf = jnp.where(oh, 1.0, 0.0)
    r = lax.broadcasted_iota(jnp.int32, (ts, ts), 0)
    c = lax.broadcasted_iota(jnp.int32, (ts, ts), 1)
    lower = jnp.where(r > c, 1.0, 0.0).astype(jnp.bfloat16)
    before = jnp.dot(lower, ohf.astype(jnp.bfloat16),
                     preferred_element_type=jnp.float32) + carry_ref[...]
    rank = jnp.sum(jnp.where(oh, before, 0.0), axis=-1, keepdims=True)
    carry_ref[...] = carry_ref[...] + jnp.sum(ohf, axis=0, keepdims=True)
    counts_ref[...] = carry_ref[...]
    slab_ref[...] = jnp.where(lane == 0, cls, jnp.where(lane == 1, rank, 0.0))


def _attn_route(x1, kt, v, g_x, w_q, w_o, g_f, w_r, b_r):
    b, s, d = x1.shape
    m = v.shape[1]
    ts = min(TOKEN_TILE, s)
    nt = s // ts
    const2 = lambda i, j: (0, 0)
    return pl.pallas_call(
        _attn_route_kernel,
        grid=(b, nt),
        in_specs=[
            pl.BlockSpec((1, ts, d), lambda i, j: (i, j, 0)),
            pl.BlockSpec((1, d, m), lambda i, j: (i, 0, 0)),
            pl.BlockSpec((1, m, d), lambda i, j: (i, 0, 0)),
            pl.BlockSpec(g_x.shape, const2),
            pl.BlockSpec(w_q.shape, const2),
            pl.BlockSpec(w_o.shape, const2),
            pl.BlockSpec(g_f.shape, const2),
            pl.BlockSpec(w_r.shape, const2),
            pl.BlockSpec(b_r.shape, const2),
        ],
        out_specs=[
            pl.BlockSpec((1, ts, d), lambda i, j: (i, j, 0)),
            pl.BlockSpec((ts, ROUTE_LANES), lambda i, j: (i * nt + j, 0)),
            pl.BlockSpec((1, ROUTE_LANES), const2),
        ],
        out_shape=[
            jax.ShapeDtypeStruct((b, s, d), jnp.float32),
            jax.ShapeDtypeStruct((b * s, ROUTE_LANES), jnp.float32),
            jax.ShapeDtypeStruct((1, ROUTE_LANES), jnp.float32),
        ],
        scratch_shapes=[pltpu.VMEM((1, ROUTE_LANES), jnp.float32)],
        compiler_params=pltpu.CompilerParams(
            dimension_semantics=("arbitrary", "arbitrary"), vmem_limit_bytes=VMEM_LIMIT_BYTES),
        name="attn_route",
    )(x1, kt, v, g_x, w_q, w_o, g_f, w_r, b_r)


def _dest_kernel(slab_ref, starts_ref, o_ref):
    slab = slab_ref[...]
    lane = lax.broadcasted_iota(jnp.int32, slab.shape, 1).astype(jnp.float32)
    row = jnp.sum(jnp.where(lane == slab[:, 0:1], starts_ref[...], 0.0),
                  axis=-1, keepdims=True) + slab[:, 1:2]
    o_ref[...] = jnp.where(lane == 0, row, 0.0).astype(jnp.int32)


def _dest_rows(slab, starts_row):
    t = slab.shape[0]
    tt = min(2048, t)
    return pl.pallas_call(
        _dest_kernel,
        grid=(t // tt,),
        in_specs=[pl.BlockSpec((tt, ROUTE_LANES), lambda i: (i, 0)),
                  pl.BlockSpec((1, ROUTE_LANES), lambda i: (0, 0))],
        out_specs=pl.BlockSpec((tt, ROUTE_LANES), lambda i: (i, 0)),
        out_shape=jax.ShapeDtypeStruct((t, ROUTE_LANES), jnp.int32),
        compiler_params=pltpu.CompilerParams(dimension_semantics=("arbitrary",)),
        name="dest_rows",
    )(slab, starts_row)


def _wait_rows(src_ref, dst_ref, sem, n):
    pltpu.make_async_copy(src_ref.at[pl.ds(0, n)], dst_ref.at[pl.ds(0, n)], sem).wait()


def _dispatch_kernel(zero_block_ref, dest_ref, x_ref, xs_ref, zero_ref, sem, zsem):
    tc = x_ref.shape[0]
    nb = zero_block_ref.shape[0]

    def block_copy(j):
        start = pl.multiple_of(j * ROW_BLOCK, ROW_BLOCK)
        return pltpu.make_async_copy(zero_ref, xs_ref.at[pl.ds(start, ROW_BLOCK)], zsem)

    @pl.when(pl.program_id(0) == 0)
    def _():
        zero_ref[...] = jnp.zeros_like(zero_ref)

        @pl.loop(0, nb)
        def _(j):
            @pl.when(zero_block_ref[j] > 0)
            def _():
                block_copy(j).start()

        @pl.loop(0, nb)
        def _(j):
            @pl.when(zero_block_ref[j] > 0)
            def _():
                block_copy(j).wait()

    def issue(r, carry):
        pltpu.make_async_copy(x_ref.at[pl.ds(r, 1)],
                              xs_ref.at[pl.ds(dest_ref[r], 1)], sem).start()
        return carry

    lax.fori_loop(0, tc, issue, 0, unroll=8)
    _wait_rows(x_ref, xs_ref, sem, tc)


def _dispatch(x2, dest, zero_block):
    t, d = x2.shape
    tc = min(COPY_TILE, t)
    cap = zero_block.shape[0] * ROW_BLOCK
    grid_spec = pltpu.PrefetchScalarGridSpec(
        num_scalar_prefetch=1,
        grid=(t // tc,),
        in_specs=[
            pl.BlockSpec((tc,), lambda i, zb: (i,), memory_space=pltpu.SMEM),
            pl.BlockSpec((tc, d), lambda i, zb: (i, 0)),
        ],
        out_specs=pl.BlockSpec(memory_space=pl.ANY),
        scratch_shapes=[pltpu.VMEM((ROW_BLOCK, d), jnp.float32),
                        pltpu.SemaphoreType.DMA, pltpu.SemaphoreType.DMA],
    )
    return pl.pallas_call(
        _dispatch_kernel,
        grid_spec=grid_spec,
        out_shape=jax.ShapeDtypeStruct((cap, d), jnp.float32),
        compiler_params=pltpu.CompilerParams(
            dimension_semantics=("arbitrary",), has_side_effects=True),
        name="dispatch",
    )(zero_block, dest, x2)


def _unsort_kernel(dest_ref, ys_ref, o_ref, sem):
    tc = o_ref.shape[0]

    def issue(r, carry):
        pltpu.make_async_copy(ys_ref.at[pl.ds(dest_ref[r], 1)],
                              o_ref.at[pl.ds(r, 1)], sem).start()
        return carry

    lax.fori_loop(0, tc, issue, 0, unroll=8)
    _wait_rows(ys_ref, o_ref, sem, tc)


def _unsort(ys, dest, t):
    d = ys.shape[1]
    tc = min(COPY_TILE, t)
    return pl.pallas_call(
        _unsort_kernel,
        grid=(t // tc,),
        in_specs=[
            pl.BlockSpec((tc,), lambda i: (i,), memory_space=pltpu.SMEM),
            pl.BlockSpec(memory_space=pl.ANY),
        ],
        out_specs=pl.BlockSpec((tc, d), lambda i: (i, 0)),
        out_shape=jax.ShapeDtypeStruct((t, d), jnp.float32),
        scratch_shapes=[pltpu.SemaphoreType.DMA],
        compiler_params=pltpu.CompilerParams(dimension_semantics=("arbitrary",)),
        name="unsort",
    )(dest, ys)


def _expert_kernel(grp_ref, ea_ref, eb_ref, nact_ref,
                   xs_ref, gf_ref, wr_ref, br_ref, gfin_ref, wg_hbm, wu_hbm, wd_hbm,
                   o_ref, wgb, wub, wdb, sg, su, sd, sem, cnt_ref):
    s = pl.program_id(0)
    blocks = tuple(BLOCKS_PER_STEP * s + i for i in range(BLOCKS_PER_STEP))
    g = grp_ref[blocks[0]]

    def stage(e):
        slot = e % 2
        return (pltpu.make_async_copy(wg_hbm.at[e], sg.at[slot], sem.at[slot, 0]),
                pltpu.make_async_copy(wu_hbm.at[e], su.at[slot], sem.at[slot, 1]),
                pltpu.make_async_copy(wd_hbm.at[e], sd.at[slot], sem.at[slot, 2]))

    def start_next():
        @pl.when(cnt_ref[0] < N_EXPERTS)
        def _():
            for cp in stage(cnt_ref[0]):
                cp.start()
            cnt_ref[0] = cnt_ref[0] + 1

    @pl.when(s == 0)
    def _():
        cnt_ref[0] = 0
        cnt_ref[1] = 0
        start_next()
        start_next()

    active = blocks[0] < nact_ref[0]

    @pl.when(active)
    def _():
        need = g * EXPERTS_PER_GROUP + functools.reduce(
            jnp.maximum, [eb_ref[j] for j in blocks])

        def load(e, carry):
            for cp in stage(e):
                cp.wait()
            slot = e % 2
            k = e % EXPERTS_PER_GROUP
            wgb[k] = sg[slot].astype(jnp.bfloat16)
            wub[k] = su[slot].astype(jnp.bfloat16)
            wdb[k] = sd[slot].astype(jnp.bfloat16)
            cnt_ref[1] = e + 1
            start_next()
            return carry

        lax.fori_loop(cnt_ref[1], need + 1, load, 0)

        for i, j in enumerate(blocks):
            x2 = xs_ref[i * ROW_BLOCK:(i + 1) * ROW_BLOCK, :]
            h3 = _rms(x2, gf_ref[...]).astype(jnp.bfloat16)
            lg = jnp.dot(h3, wr_ref[...], preferred_element_type=jnp.float32) + br_ref[...]
            lane = lax.broadcasted_iota(jnp.int32, (ROW_BLOCK, ROUTE_LANES), 1)
            is_g = lane < N_GROUPS
            gmax = jnp.max(jnp.where(is_g, lg, _NEG), axis=-1, keepdims=True)
            den = jnp.sum(jnp.where(is_g, jnp.exp(jnp.where(is_g, lg, _NEG) - gmax), 0.0),
                          axis=-1, keepdims=True)

            def pick(idx):
                return jnp.sum(jnp.where(lane == idx, lg, 0.0), axis=-1, keepdims=True)

            grp_p = jnp.exp(pick(g) - gmax) / den
            lo = EXPERT_LANE0 + EXPERTS_PER_GROUP * g
            la = pick(lo + ea_ref[j])
            lb = pick(lo + eb_ref[j])
            m = jnp.maximum(la, lb)
            pa = jnp.exp(la - m)
            pb = jnp.exp(lb - m)
            gate = (grp_p * pa / (pa + pb), grp_p * pb / (pa + pb))

            y = x2
            for k, gt in zip((ea_ref[j], eb_ref[j]), gate):
                gg = jnp.dot(h3, wgb[k], preferred_element_type=jnp.float32)
                uu = jnp.dot(h3, wub[k], preferred_element_type=jnp.float32)
                act = (gg * (1.0 / (1.0 + jnp.exp(-gg))) * uu * gt).astype(jnp.bfloat16)
                y = y + jnp.dot(act, wdb[k], preferred_element_type=jnp.float32)
            o_ref[i * ROW_BLOCK:(i + 1) * ROW_BLOCK, :] = _rms(y, gfin_ref[...])

    @pl.when(jnp.logical_not(active))
    def _():
        o_ref[...] = jnp.zeros_like(o_ref)

    @pl.when(s == pl.num_programs(0) - 1)
    def _():
        def drain(e, carry):
            for cp in stage(e):
                cp.wait()
            return carry

        lax.fori_loop(cnt_ref[1], cnt_ref[0], drain, 0)


def _experts(xs, blk_grp, blk_a, blk_b, nact, g_ffn, w_r, b_r, g_final, w_gate, w_up, w_down):
    cap, d = xs.shape
    de = w_gate.shape[2]
    step_rows = BLOCKS_PER_STEP * ROW_BLOCK
    steps = cap // step_rows
    pre = lambda f: (lambda s, gr, ea, eb, na: f(s, na))
    const2 = pre(lambda s, na: (0, 0))
    last_step = lambda na: (na[0] - 1) // BLOCKS_PER_STEP
    hbm = pl.BlockSpec(memory_space=pl.ANY)
    grid_spec = pltpu.PrefetchScalarGridSpec(
        num_scalar_prefetch=4,
        grid=(steps,),
        in_specs=[
            pl.BlockSpec((step_rows, d), pre(lambda s, na: (jnp.minimum(s, last_step(na)), 0))),
            pl.BlockSpec(g_ffn.shape, const2),
            pl.BlockSpec(w_r.shape, const2),
            pl.BlockSpec(b_r.shape, const2),
            pl.BlockSpec(g_final.shape, const2),
            hbm, hbm, hbm,
        ],
        out_specs=pl.BlockSpec((step_rows, d), pre(lambda s, na: (s, 0))),
        scratch_shapes=[
            pltpu.VMEM((EXPERTS_PER_GROUP, d, de), jnp.bfloat16),
            pltpu.VMEM((EXPERTS_PER_GROUP, d, de), jnp.bfloat16),
            pltpu.VMEM((EXPERTS_PER_GROUP, de, d), jnp.bfloat16),
            pltpu.VMEM((2, d, de), jnp.float32),
            pltpu.VMEM((2, d, de), jnp.float32),
            pltpu.VMEM((2, de, d), jnp.float32),
            pltpu.SemaphoreType.DMA((2, 3)),
            pltpu.SMEM((2,), jnp.int32),
        ],
    )
    return pl.pallas_call(
        _expert_kernel,
        grid_spec=grid_spec,
        out_shape=jax.ShapeDtypeStruct((cap, d), jnp.float32),
        compiler_params=pltpu.CompilerParams(
            dimension_semantics=("arbitrary",), vmem_limit_bytes=VMEM_LIMIT_BYTES),
        name="experts",
    )(blk_grp, blk_a, blk_b, nact, xs, g_ffn, w_r, b_r, g_final, w_gate, w_up, w_down)


def _moe_final(x2, slab, counts_row, g_ffn, w_r, b_r, g_final, w_gate, w_up, w_down):
    t, d = x2.shape
    nb = t // ROW_BLOCK + N_CLASSES + N_GROUPS * (BLOCKS_PER_STEP - 1)
    assert nb % BLOCKS_PER_STEP == 0
    counts = counts_row[0, :N_CLASSES].astype(jnp.int32)
    nblk = (counts + ROW_BLOCK - 1) // ROW_BLOCK
    grp_blocks = jnp.sum(nblk.reshape(N_GROUPS, N_PAIRS), axis=1)
    nblk = nblk.reshape(N_GROUPS, N_PAIRS).at[:, N_PAIRS - 1].add(
        (-grp_blocks) % BLOCKS_PER_STEP).reshape(N_CLASSES)
    blk_end = jnp.cumsum(nblk)
    blk_start = blk_end - nblk
    nact = blk_end[-1]
    j = jnp.arange(nb, dtype=jnp.int32)
    blk_class = jnp.minimum(
        jnp.sum((blk_end[None, :] <= j[:, None]).astype(jnp.int32), axis=1), N_CLASSES - 1)
    blk_class = jnp.where(j < nact, blk_class, blk_class[jnp.maximum(nact - 1, 0)])
    not_full = (j - blk_start[blk_class] + 1) * ROW_BLOCK > counts[blk_class]
    zero_block = (not_full | (j >= nact)).astype(jnp.int32)
    pair_a = jnp.array([p[0] for p in PAIRS], jnp.int32)
    pair_b = jnp.array([p[1] for p in PAIRS], jnp.int32)
    blk_grp = blk_class // N_PAIRS
    blk_a = pair_a[blk_class % N_PAIRS]
    blk_b = pair_b[blk_class % N_PAIRS]
    starts_row = jnp.zeros((1, ROUTE_LANES), jnp.float32).at[0, :N_CLASSES].set(
        (blk_start * ROW_BLOCK).astype(jnp.float32))

    dest = _dest_rows(slab, starts_row)[:, 0]
    xs = _dispatch(x2, dest, zero_block)
    ys = _experts(xs, blk_grp, blk_a, blk_b, nact[None], g_ffn, w_r, b_r, g_final,
                  w_gate, w_up, w_down)
    return _unsort(ys, dest, t)


def kernel(x, mem, g_mix, w_in, conv_w, g_v, w_s, b_s, g_out_conv, g_out_gmlp, w_out, g_xattn,
           g_mem, w_q, w_k, w_v, w_o, g_ffn, w_grp, b_grp, w_rt, b_rt, w_gate, w_up, w_down,
           g_final):
    b, s, d = x.shape
    assert g_mix.shape[0] == 1, "the final norm is fused into the single layer's expert kernel"
    assert N_CLASSES <= ROUTE_LANES
    bf = jnp.bfloat16
    kt, v = _kv_proj(mem, g_mem[0][None], w_k[0].astype(bf), w_v[0].astype(bf))

    bias = jnp.repeat(b_s[0].T, GMLP_HEAD_DIM, axis=1)
    head_of_lane = jnp.arange(GMLP_HEADS * GMLP_HEAD_DIM) // GMLP_HEAD_DIM
    hsum = (head_of_lane[:, None] == head_of_lane[None, :]).astype(bf)
    x1 = _mixer(x, g_mix[0][None], w_in[0].astype(bf), conv_w[0], g_v[0][None], w_s[0], bias,
                g_out_conv[0][None], g_out_gmlp[0][None], w_out[0].astype(bf), hsum)

    pad = ROUTE_LANES - N_GROUPS - N_EXPERTS
    w_r = jnp.concatenate([w_grp[0], w_rt[0], jnp.zeros((d, pad), jnp.float32)], axis=1).astype(bf)
    b_r = jnp.concatenate([b_grp[0], b_rt[0], jnp.zeros((pad,), jnp.float32)])[None]
    x2, slab, counts_row = _attn_route(x1, kt, v, g_xattn[0][None], w_q[0].astype(bf),
                                       w_o[0].astype(bf), g_ffn[0][None], w_r, b_r)
    out = _moe_final(x2.reshape(b * s, d), slab, counts_row, g_ffn[0][None], w_r, b_r,
                     g_final[None], w_gate[0], w_up[0], w_down[0])
    return out.reshape(b, s, d)
```

```python
import functools

import jax
import jax.numpy as jnp
from jax import lax
from jax.experimental import pallas as pl
from jax.experimental.pallas import tpu as pltpu

EPS = 1e-6
CONV_GROUP_WIDTH = 512
GMLP_HEADS = 8
GMLP_HEAD_DIM = 64
CHUNK = 128
CONV_K = 3
XA_HEADS = 4
N_GROUPS = 4
EXPERTS_PER_GROUP = 8
N_EXPERTS = N_GROUPS * EXPERTS_PER_GROUP
TOP_K = 2
ROUTE_LANES = 128
EXPERT_LANE0 = N_GROUPS

N_PAIRS = EXPERTS_PER_GROUP * (EXPERTS_PER_GROUP - 1) // 2
N_CLASSES = N_GROUPS * N_PAIRS
PAIRS = [(a, b) for a in range(EXPERTS_PER_GROUP) for b in range(a + 1, EXPERTS_PER_GROUP)]

TOKEN_TILE = 512
ROW_BLOCK = 128
BLOCKS_PER_STEP = 2
COPY_TILE = 2048
VMEM_LIMIT_BYTES = 56 * 1024 * 1024

SUBLANES = 8

_NEG = -1e30


def _rms(x, g):
    return x * lax.rsqrt(jnp.mean(x * x, axis=-1, keepdims=True) + EPS) * g


def _gelu_tanh(x):
    return 0.5 * x * (1.0 + jnp.tanh(0.7978845608028654 * (x + 0.044715 * (x * x * x))))


def _kv_kernel(mem_ref, g_ref, wk_ref, wv_ref, kt_ref, v_ref):
    m = _rms(mem_ref[0], g_ref[...]).astype(jnp.bfloat16)
    k = jnp.dot(m, wk_ref[...], preferred_element_type=jnp.float32)
    v = jnp.dot(m, wv_ref[...], preferred_element_type=jnp.float32)
    kt_ref[0] = k.T.astype(jnp.bfloat16)
    v_ref[0] = v.astype(jnp.bfloat16)


def _kv_proj(mem, g_mem, w_k, w_v):
    b, m, d = mem.shape
    const = lambda i: (0, 0)
    return pl.pallas_call(
        _kv_kernel,
        grid=(b,),
        in_specs=[
            pl.BlockSpec((1, m, d), lambda i: (i, 0, 0)),
            pl.BlockSpec((1, d), const),
            pl.BlockSpec((d, d), const),
            pl.BlockSpec((d, d), const),
        ],
        out_specs=[
            pl.BlockSpec((1, d, m), lambda i: (i, 0, 0)),
            pl.BlockSpec((1, m, d), lambda i: (i, 0, 0)),
        ],
        out_shape=[
            jax.ShapeDtypeStruct((b, d, m), jnp.bfloat16),
            jax.ShapeDtypeStruct((b, m, d), jnp.bfloat16),
        ],
        compiler_params=pltpu.CompilerParams(
            dimension_semantics=("arbitrary",), vmem_limit_bytes=VMEM_LIMIT_BYTES),
        name="kv_proj",
    )(mem, g_mem, w_k, w_v)


def _mixer_kernel(x_ref, gmix_ref, win_ref, convw_ref, gv_ref, ws_ref, bias_ref,
                  goc_ref, gog_ref, wout_ref, hsum_ref, o_ref, zbuf_ref):
    ts = x_ref.shape[1]
    w = CONV_GROUP_WIDTH
    xt = x_ref[0]
    h = _rms(xt, gmix_ref[...]).astype(jnp.bfloat16)

    def proj(i):
        return jnp.dot(h, win_ref[:, i * w:(i + 1) * w], preferred_element_type=jnp.float32)

    z = proj(1) * proj(2)

    @pl.when(pl.program_id(1) == 0)
    def _():
        zbuf_ref[0:8, :] = jnp.zeros((8, w), jnp.float32)

    @pl.when(pl.program_id(1) != 0)
    def _():
        zbuf_ref[0:8, :] = zbuf_ref[ts:ts + 8, :]

    zbuf_ref[8:8 + ts, :] = z
    zc = (convw_ref[0:1, :] * zbuf_ref[6:6 + ts, :]
          + convw_ref[1:2, :] * zbuf_ref[7:7 + ts, :]
          + convw_ref[2:3, :] * z)
    ya = _rms(proj(0) * zc, goc_ref[...]).astype(jnp.bfloat16)

    u = _gelu_tanh(proj(3))
    v = _gelu_tanh(proj(4))
    v2 = v * v
    v2_hi = v2.astype(jnp.bfloat16)
    v2_lo = (v2 - v2_hi.astype(jnp.float32)).astype(jnp.bfloat16)
    ss = (jnp.dot(v2_hi, hsum_ref[...], preferred_element_type=jnp.float32)
          + jnp.dot(v2_lo, hsum_ref[...], preferred_element_type=jnp.float32))
    vn = (v * lax.rsqrt(ss * (1.0 / GMLP_HEAD_DIM) + EPS) * gv_ref[...]).astype(jnp.bfloat16)

    row = lax.broadcasted_iota(jnp.int32, (CHUNK, CHUNK), 0)
    col = lax.broadcasted_iota(jnp.int32, (CHUNK, CHUNK), 1)
    causal = row >= col
    ws = [jnp.where(causal, ws_ref[hd], 0.0).astype(jnp.bfloat16) for hd in range(GMLP_HEADS)]
    chunks = []
    for c in range(ts // CHUNK):
        vc = vn[c * CHUNK:(c + 1) * CHUNK, :]
        heads = [jnp.dot(ws[hd], vc[:, hd * GMLP_HEAD_DIM:(hd + 1) * GMLP_HEAD_DIM],
                         preferred_element_type=jnp.float32) for hd in range(GMLP_HEADS)]
        chunks.append(jnp.concatenate(heads, axis=1) + bias_ref[...])
    s = jnp.concatenate(chunks, axis=0)
    yb = _rms(u * s, gog_ref[...]).astype(jnp.bfloat16)

    o_ref[0] = (xt
                + jnp.dot(ya, wout_ref[0:w, :], preferred_element_type=jnp.float32)
                + jnp.dot(yb, wout_ref[w:2 * w, :], preferred_element_type=jnp.float32))


def _mixer(x, g_mix, w_in, conv_w, g_v, w_s, bias, g_oc, g_og, w_out, hsum):
    b, s, d = x.shape
    ts = min(TOKEN_TILE, s)
    const2 = lambda i, j: (0, 0)
    const3 = lambda i, j: (0, 0, 0)
    return pl.pallas_call(
        _mixer_kernel,
        grid=(b, s // ts),
        in_specs=[
            pl.BlockSpec((1, ts, d), lambda i, j: (i, j, 0)),
            pl.BlockSpec(g_mix.shape, const2),
            pl.BlockSpec(w_in.shape, const2),
            pl.BlockSpec(conv_w.shape, const2),
            pl.BlockSpec(g_v.shape, const2),
            pl.BlockSpec(w_s.shape, const3),
            pl.BlockSpec(bias.shape, const2),
            pl.BlockSpec(g_oc.shape, const2),
            pl.BlockSpec(g_og.shape, const2),
            pl.BlockSpec(w_out.shape, const2),
            pl.BlockSpec(hsum.shape, const2),
        ],
        out_specs=pl.BlockSpec((1, ts, d), lambda i, j: (i, j, 0)),
        out_shape=jax.ShapeDtypeStruct((b, s, d), jnp.float32),
        scratch_shapes=[pltpu.VMEM((ts + 8, CONV_GROUP_WIDTH), jnp.float32)],
        compiler_params=pltpu.CompilerParams(
            dimension_semantics=("arbitrary", "arbitrary"), vmem_limit_bytes=VMEM_LIMIT_BYTES),
        name="mixer",
    )(x, g_mix, w_in, conv_w, g_v, w_s, bias, g_oc, g_og, w_out, hsum)


def _attn_route_kernel(x_ref, kt_ref, v_ref, gx_ref, wq_ref, wo_ref, gf_ref, wr_ref, br_ref,
                       x2_ref, slab_ref, counts_ref, carry_ref):
    ts = x_ref.shape[1]
    d = x_ref.shape[2]
    hd = d // XA_HEADS
    first = (pl.program_id(0) == 0) & (pl.program_id(1) == 0)

    @pl.when(first)
    def _():
        carry_ref[...] = jnp.zeros_like(carry_ref)

    x1 = x_ref[0]
    h2 = _rms(x1, gx_ref[...]).astype(jnp.bfloat16)
    q = jnp.dot(h2, wq_ref[...], preferred_element_type=jnp.float32).astype(jnp.bfloat16)
    heads = []
    for a in range(XA_HEADS):
        sc = jnp.dot(q[:, a * hd:(a + 1) * hd], kt_ref[0, a * hd:(a + 1) * hd, :],
                     preferred_element_type=jnp.float32) * (hd ** -0.5)
        p = jnp.exp(sc - jnp.max(sc, axis=-1, keepdims=True))
        l = jnp.sum(p, axis=-1, keepdims=True)
        o = jnp.dot(p.astype(jnp.bfloat16), v_ref[0, :, a * hd:(a + 1) * hd],
                    preferred_element_type=jnp.float32)
        heads.append((o / l).astype(jnp.bfloat16))
    o = jnp.concatenate(heads, axis=1)
    x2 = x1 + jnp.dot(o, wo_ref[...], preferred_element_type=jnp.float32)
    x2_ref[0] = x2

    h3 = _rms(x2, gf_ref[...]).astype(jnp.bfloat16)
    lg = jnp.dot(h3, wr_ref[...], preferred_element_type=jnp.float32) + br_ref[...]
    lane = lax.broadcasted_iota(jnp.int32, (ts, ROUTE_LANES), 1).astype(jnp.float32)
    big = float(ROUTE_LANES)
    is_g = lane < N_GROUPS
    gmax = jnp.max(jnp.where(is_g, lg, _NEG), axis=-1, keepdims=True)
    gidx = jnp.min(jnp.where(is_g & (lg == gmax), lane, big), axis=-1, keepdims=True)
    lo = EXPERT_LANE0 + EXPERTS_PER_GROUP * gidx
    in_g = (lane >= lo) & (lane < lo + EXPERTS_PER_GROUP)
    t1 = jnp.max(jnp.where(in_g, lg, _NEG), axis=-1, keepdims=True)
    i1 = jnp.min(jnp.where(in_g & (lg == t1), lane, big), axis=-1, keepdims=True)
    rest = in_g & (lane != i1)
    t2 = jnp.max(jnp.where(rest, lg, _NEG), axis=-1, keepdims=True)
    i2 = jnp.min(jnp.where(rest & (lg == t2), lane, big), axis=-1, keepdims=True)

    a = jnp.minimum(i1, i2) - lo
    b = jnp.maximum(i1, i2) - lo
    pair = a * (2 * EXPERTS_PER_GROUP - 1 - a) * 0.5 + (b - a - 1.0)
    cls = gidx * N_PAIRS + pair

    oh = lane == cls
    ohf = jnp.where(oh, 1.0, 0.0)
    r = lax.broadcasted_iota(jnp.int32, (ts, ts), 0)
    c = lax.broadcasted_iota(jnp.int32, (ts, ts), 1)
    lower = jnp.where(r > c, 1.0, 0.0).astype(jnp.bfloat16)
    before = jnp.dot(lower, ohf.astype(jnp.bfloat16),
                     preferred_element_type=jnp.float32) + carry_ref[...]
    rank = jnp.sum(jnp.where(oh, before, 0.0), axis=-1, keepdims=True)
    carry_ref[...] = carry_ref[...] + jnp.sum(ohf, axis=0, keepdims=True)
    counts_ref[...] = carry_ref[...]
    slab_ref[...] = jnp.where(lane == 0, cls, jnp.where(lane == 1, rank, 0.0))


def _attn_route(x1, kt, v, g_x, w_q, w_o, g_f, w_r, b_r):
    b, s, d = x1.shape
    m = v.shape[1]
    ts = min(TOKEN_TILE, s)
    nt = s // ts
    const2 = lambda i, j: (0, 0)
    return pl.pallas_call(
        _attn_route_kernel,
        grid=(b, nt),
        in_specs=[
            pl.BlockSpec((1, ts, d), lambda i, j: (i, j, 0)),
            pl.BlockSpec((1, d, m), lambda i, j: (i, 0, 0)),
            pl.BlockSpec((1, m, d), lambda i, j: (i, 0, 0)),
            pl.BlockSpec(g_x.shape, const2),
            pl.BlockSpec(w_q.shape, const2),
            pl.BlockSpec(w_o.shape, const2),
            pl.BlockSpec(g_f.shape, const2),
            pl.BlockSpec(w_r.shape, const2),
            pl.BlockSpec(b_r.shape, const2),
        ],
        out_specs=[
            pl.BlockSpec((1, ts, d), lambda i, j: (i, j, 0)),
            pl.BlockSpec((ts, ROUTE_LANES), lambda i, j: (i * nt + j, 0)),
            pl.BlockSpec((1, ROUTE_LANES), const2),
        ],
        out_shape=[
            jax.ShapeDtypeStruct((b, s, d), jnp.float32),
            jax.ShapeDtypeStruct((b * s, ROUTE_LANES), jnp.float32),
            jax.ShapeDtypeStruct((1, ROUTE_LANES), jnp.float32),
        ],
        scratch_shapes=[pltpu.VMEM((1, ROUTE_LANES), jnp.float32)],
        compiler_params=pltpu.CompilerParams(
            dimension_semantics=("arbitrary", "arbitrary"), vmem_limit_bytes=VMEM_LIMIT_BYTES),
        name="attn_route",
    )(x1, kt, v, g_x, w_q, w_o, g_f, w_r, b_r)


def _dest_kernel(slab_ref, starts_ref, o_ref):
    slab = slab_ref[...]
    lane = lax.broadcasted_iota(jnp.int32, slab.shape, 1).astype(jnp.float32)
    row = jnp.sum(jnp.where(lane == slab[:, 0:1], starts_ref[...], 0.0),
                  axis=-1, keepdims=True) + slab[:, 1:2]
    o_ref[...] = jnp.where(lane == 0, row, 0.0).astype(jnp.int32)


def _dest_rows(slab, starts_row):
    t = slab.shape[0]
    tt = min(2048, t)
    return pl.pallas_call(
        _dest_kernel,
        grid=(t // tt,),
        in_specs=[pl.BlockSpec((tt, ROUTE_LANES), lambda i: (i, 0)),
                  pl.BlockSpec((1, ROUTE_LANES), lambda i: (0, 0))],
        out_specs=pl.BlockSpec((tt, ROUTE_LANES), lambda i: (i, 0)),
        out_shape=jax.ShapeDtypeStruct((t, ROUTE_LANES), jnp.int32),
        compiler_params=pltpu.CompilerParams(dimension_semantics=("arbitrary",)),
        name="dest_rows",
    )(slab, starts_row)


def _wait_rows(src_ref, dst_ref, sem, n):
    pltpu.make_async_copy(src_ref.at[pl.ds(0, n)], dst_ref.at[pl.ds(0, n)], sem).wait()


def _dispatch_kernel(zero_block_ref, dest_ref, x_ref, xs_ref, zero_ref, sem, zsem):
    tiles, sub, _ = x_ref.shape
    nb = zero_block_ref.shape[0]

    def block_copy(j):
        start = pl.multiple_of(j * ROW_BLOCK, ROW_BLOCK)
        return pltpu.make_async_copy(zero_ref, xs_ref.at[pl.ds(start, ROW_BLOCK)], zsem)

    @pl.when(pl.program_id(0) == 0)
    def _():
        zero_ref[...] = jnp.zeros_like(zero_ref)

        @pl.loop(0, nb)
        def _(j):
            @pl.when(zero_block_ref[j] > 0)
            def _():
                block_copy(j).start()

        @pl.loop(0, nb)
        def _(j):
            @pl.when(zero_block_ref[j] > 0)
            def _():
                block_copy(j).wait()

    def issue(i, carry):
        for k in range(sub):
            pltpu.make_async_copy(x_ref.at[i, pl.ds(k, 1)],
                                  xs_ref.at[pl.ds(dest_ref[i * sub + k], 1)], sem).start()
        return carry

    lax.fori_loop(0, tiles, issue, 0)
    _wait_rows(xs_ref, xs_ref, sem, tiles * sub)


def _dispatch(x2, dest, zero_block):
    t, d = x2.shape
    tc = min(COPY_TILE, t)
    cap = zero_block.shape[0] * ROW_BLOCK
    grid_spec = pltpu.PrefetchScalarGridSpec(
        num_scalar_prefetch=1,
        grid=(t // tc,),
        in_specs=[
            pl.BlockSpec((tc,), lambda i, zb: (i,), memory_space=pltpu.SMEM),
            pl.BlockSpec((tc // SUBLANES, SUBLANES, d), lambda i, zb: (i, 0, 0)),
        ],
        out_specs=pl.BlockSpec(memory_space=pl.ANY),
        scratch_shapes=[pltpu.VMEM((ROW_BLOCK, d), jnp.float32),
                        pltpu.SemaphoreType.DMA, pltpu.SemaphoreType.DMA],
    )
    return pl.pallas_call(
        _dispatch_kernel,
        grid_spec=grid_spec,
        out_shape=jax.ShapeDtypeStruct((cap, d), jnp.float32),
        compiler_params=pltpu.CompilerParams(
            dimension_semantics=("arbitrary",), has_side_effects=True),
        name="dispatch",
    )(zero_block, dest, x2.reshape(t // SUBLANES, SUBLANES, d))


def _unsort_kernel(dest_ref, ys_ref, o_ref, sem):
    tiles, sub, _ = o_ref.shape

    def issue(i, carry):
        for k in range(sub):
            pltpu.make_async_copy(ys_ref.at[pl.ds(dest_ref[i * sub + k], 1)],
                                  o_ref.at[i, pl.ds(k, 1)], sem).start()
        return carry

    lax.fori_loop(0, tiles, issue, 0)
    _wait_rows(ys_ref, ys_ref, sem, tiles * sub)


def _unsort(ys, dest, t):
    d = ys.shape[1]
    tc = min(COPY_TILE, t)
    return pl.pallas_call(
        _unsort_kernel,
        grid=(t // tc,),
        in_specs=[
            pl.BlockSpec((tc,), lambda i: (i,), memory_space=pltpu.SMEM),
            pl.BlockSpec(memory_space=pl.ANY),
        ],
        out_specs=pl.BlockSpec((tc // SUBLANES, SUBLANES, d), lambda i: (i, 0, 0)),
        out_shape=jax.ShapeDtypeStruct((t // SUBLANES, SUBLANES, d), jnp.float32),
        scratch_shapes=[pltpu.SemaphoreType.DMA],
        compiler_params=pltpu.CompilerParams(dimension_semantics=("arbitrary",)),
        name="unsort",
    )(dest, ys).reshape(t, d)


def _expert_kernel(grp_ref, ea_ref, eb_ref, nact_ref,
                   xs_ref, gf_ref, wr_ref, br_ref, gfin_ref, wg_hbm, wu_hbm, wd_hbm,
                   o_ref, wgb, wub, wdb, sg, su, sd, sem, cnt_ref):
    s = pl.program_id(0)
    blocks = tuple(BLOCKS_PER_STEP * s + i for i in range(BLOCKS_PER_STEP))
    g = grp_ref[blocks[0]]

    def stage(e):
        slot = e % 2
        return (pltpu.make_async_copy(wg_hbm.at[e], sg.at[slot], sem.at[slot, 0]),
                pltpu.make_async_copy(wu_hbm.at[e], su.at[slot], sem.at[slot, 1]),
                pltpu.make_async_copy(wd_hbm.at[e], sd.at[slot], sem.at[slot, 2]))

    def start_next():
        @pl.when(cnt_ref[0] < N_EXPERTS)
        def _():
            for cp in stage(cnt_ref[0]):
                cp.start()
            cnt_ref[0] = cnt_ref[0] + 1

    @pl.when(s == 0)
    def _():
        cnt_ref[0] = 0
        cnt_ref[1] = 0
        start_next()
        start_next()

    active = blocks[0] < nact_ref[0]

    @pl.when(active)
    def _():
        need = g * EXPERTS_PER_GROUP + functools.reduce(
            jnp.maximum, [eb_ref[j] for j in blocks])

        def load(e, carry):
            for cp in stage(e):
                cp.wait()
            slot = e % 2
            k = e % EXPERTS_PER_GROUP
            wgb[k] = sg[slot].astype(jnp.bfloat16)
            wub[k] = su[slot].astype(jnp.bfloat16)
            wdb[k] = sd[slot].astype(jnp.bfloat16)
            cnt_ref[1] = e + 1
            start_next()
            return carry

        lax.fori_loop(cnt_ref[1], need + 1, load, 0)

        lane = lax.broadcasted_iota(jnp.int32, (ROW_BLOCK, ROUTE_LANES), 1)
        is_g = lane < N_GROUPS
        lo = EXPERT_LANE0 + EXPERTS_PER_GROUP * g
        nblk = len(blocks)
        xs = [xs_ref[i * ROW_BLOCK:(i + 1) * ROW_BLOCK, :] for i in range(nblk)]
        h3 = [_rms(x, gf_ref[...]).astype(jnp.bfloat16) for x in xs]
        lg = [jnp.dot(h, wr_ref[...], preferred_element_type=jnp.float32) + br_ref[...]
              for h in h3]

        def gates(lgi, j):
            def pick(idx):
                return jnp.sum(jnp.where(lane == idx, lgi, 0.0), axis=-1, keepdims=True)

            gmax = jnp.max(jnp.where(is_g, lgi, _NEG), axis=-1, keepdims=True)
            den = jnp.sum(jnp.where(is_g, jnp.exp(jnp.where(is_g, lgi, _NEG) - gmax), 0.0),
                          axis=-1, keepdims=True)
            grp_p = jnp.exp(pick(g) - gmax) / den
            la = pick(lo + ea_ref[j])
            lb = pick(lo + eb_ref[j])
            m = jnp.maximum(la, lb)
            pa = jnp.exp(la - m)
            pb = jnp.exp(lb - m)
            return grp_p * pa / (pa + pb), grp_p * pb / (pa + pb)

        gate = [gates(lg[i], blocks[i]) for i in range(nblk)]
        ys = list(xs)
        for which, e_ref in enumerate((ea_ref, eb_ref)):
            ks = [e_ref[j] for j in blocks]
            gg = [jnp.dot(h3[i], wgb[ks[i]], preferred_element_type=jnp.float32)
                  for i in range(nblk)]
            uu = [jnp.dot(h3[i], wub[ks[i]], preferred_element_type=jnp.float32)
                  for i in range(nblk)]
            act = [(gg[i] * (1.0 / (1.0 + jnp.exp(-gg[i]))) * uu[i]
                    * gate[i][which]).astype(jnp.bfloat16) for i in range(nblk)]
            ys = [ys[i] + jnp.dot(act[i], wdb[ks[i]], preferred_element_type=jnp.float32)
                  for i in range(nblk)]
        for i in range(nblk):
            o_ref[i * ROW_BLOCK:(i + 1) * ROW_BLOCK, :] = _rms(ys[i], gfin_ref[...])

    @pl.when(jnp.logical_not(active))
    def _():
        o_ref[...] = jnp.zeros_like(o_ref)

    @pl.when(s == pl.num_programs(0) - 1)
    def _():
        def drain(e, carry):
            for cp in stage(e):
                cp.wait()
            return carry

        lax.fori_loop(cnt_ref[1], cnt_ref[0], drain, 0)


def _experts(xs, blk_grp, blk_a, blk_b, nact, g_ffn, w_r, b_r, g_final, w_gate, w_up, w_down):
    cap, d = xs.shape
    de = w_gate.shape[2]
    step_rows = BLOCKS_PER_STEP * ROW_BLOCK
    steps = cap // step_rows
    pre = lambda f: (lambda s, gr, ea, eb, na: f(s, na))
    const2 = pre(lambda s, na: (0, 0))
    last_step = lambda na: (na[0] - 1) // BLOCKS_PER_STEP
    hbm = pl.BlockSpec(memory_space=pl.ANY)
    grid_spec = pltpu.PrefetchScalarGridSpec(
        num_scalar_prefetch=4,
        grid=(steps,),
        in_specs=[
            pl.BlockSpec((step_rows, d), pre(lambda s, na: (jnp.minimum(s, last_step(na)), 0))),
            pl.BlockSpec(g_ffn.shape, const2),
            pl.BlockSpec(w_r.shape, const2),
            pl.BlockSpec(b_r.shape, const2),
            pl.BlockSpec(g_final.shape, const2),
            hbm, hbm, hbm,
        ],
        out_specs=pl.BlockSpec((step_rows, d), pre(lambda s, na: (s, 0))),
        scratch_shapes=[
            pltpu.VMEM((EXPERTS_PER_GROUP, d, de), jnp.bfloat16),
            pltpu.VMEM((EXPERTS_PER_GROUP, d, de), jnp.bfloat16),
            pltpu.VMEM((EXPERTS_PER_GROUP, de, d), jnp.bfloat16),
            pltpu.VMEM((2, d, de), jnp.float32),
            pltpu.VMEM((2, d, de), jnp.float32),
            pltpu.VMEM((2, de, d), jnp.float32),
            pltpu.SemaphoreType.DMA((2, 3)),
            pltpu.SMEM((2,), jnp.int32),
        ],
    )
    return pl.pallas_call(
        _expert_kernel,
        grid_spec=grid_spec,
        out_shape=jax.ShapeDtypeStruct((cap, d), jnp.float32),
        compiler_params=pltpu.CompilerParams(
            dimension_semantics=("arbitrary",), vmem_limit_bytes=VMEM_LIMIT_BYTES),
        name="experts",
    )(blk_grp, blk_a, blk_b, nact, xs, g_ffn, w_r, b_r, g_final, w_gate, w_up, w_down)


def _moe_final(x2, slab, counts_row, g_ffn, w_r, b_r, g_final, w_gate, w_up, w_down):
    t, d = x2.shape
    nb = t // ROW_BLOCK + N_CLASSES + N_GROUPS * (BLOCKS_PER_STEP - 1)
    assert nb % BLOCKS_PER_STEP == 0
    counts = counts_row[0, :N_CLASSES].astype(jnp.int32)
    nblk = (counts + ROW_BLOCK - 1) // ROW_BLOCK
    grp_blocks = jnp.sum(nblk.reshape(N_GROUPS, N_PAIRS), axis=1)
    nblk = nblk.reshape(N_GROUPS, N_PAIRS).at[:, N_PAIRS - 1].add(
        (-grp_blocks) % BLOCKS_PER_STEP).reshape(N_CLASSES)
    blk_end = jnp.cumsum(nblk)
    blk_start = blk_end - nblk
    nact = blk_end[-1]
    j = jnp.arange(nb, dtype=jnp.int32)
    blk_class = jnp.minimum(
        jnp.sum((blk_end[None, :] <= j[:, None]).astype(jnp.int32), axis=1), N_CLASSES - 1)
    blk_class = jnp.where(j < nact, blk_class, blk_class[jnp.maximum(nact - 1, 0)])
    not_full = (j - blk_start[blk_class] + 1) * ROW_BLOCK > counts[blk_class]
    zero_block = (not_full | (j >= nact)).astype(jnp.int32)
    pair_a = jnp.array([p[0] for p in PAIRS], jnp.int32)
    pair_b = jnp.array([p[1] for p in PAIRS], jnp.int32)
    blk_grp = blk_class // N_PAIRS
    blk_a = pair_a[blk_class % N_PAIRS]
    blk_b = pair_b[blk_class % N_PAIRS]
    starts_row = jnp.zeros((1, ROUTE_LANES), jnp.float32).at[0, :N_CLASSES].set(
        (blk_start * ROW_BLOCK).astype(jnp.float32))

    dest = _dest_rows(slab, starts_row)[:, 0]
    xs = _dispatch(x2, dest, zero_block)
    ys = _experts(xs, blk_grp, blk_a, blk_b, nact[None], g_ffn, w_r, b_r, g_final,
                  w_gate, w_up, w_down)
    return _unsort(ys, dest, t)


def kernel(x, mem, g_mix, w_in, conv_w, g_v, w_s, b_s, g_out_conv, g_out_gmlp, w_out, g_xattn,
           g_mem, w_q, w_k, w_v, w_o, g_ffn, w_grp, b_grp, w_rt, b_rt, w_gate, w_up, w_down,
           g_final):
    b, s, d = x.shape
    assert g_mix.shape[0] == 1, "the final norm is fused into the single layer's expert kernel"
    assert N_CLASSES <= ROUTE_LANES
    bf = jnp.bfloat16
    kt, v = _kv_proj(mem, g_mem[0][None], w_k[0].astype(bf), w_v[0].astype(bf))

    bias = jnp.repeat(b_s[0].T, GMLP_HEAD_DIM, axis=1)
    head_of_lane = jnp.arange(GMLP_HEADS * GMLP_HEAD_DIM) // GMLP_HEAD_DIM
    hsum = (head_of_lane[:, None] == head_of_lane[None, :]).astype(bf)
    x1 = _mixer(x, g_mix[0][None], w_in[0].astype(bf), conv_w[0], g_v[0][None], w_s[0], bias,
                g_out_conv[0][None], g_out_gmlp[0][None], w_out[0].astype(bf), hsum)

    pad = ROUTE_LANES - N_GROUPS - N_EXPERTS
    w_r = jnp.concatenate([w_grp[0], w_rt[0], jnp.zeros((d, pad), jnp.float32)], axis=1).astype(bf)
    b_r = jnp.concatenate([b_grp[0], b_rt[0], jnp.zeros((pad,), jnp.float32)])[None]
    x2, slab, counts_row = _attn_route(x1, kt, v, g_xattn[0][None], w_q[0].astype(bf),
                                       w_o[0].astype(bf), g_ffn[0][None], w_r, b_r)
    out = _moe_final(x2.reshape(b * s, d), slab, counts_row, g_ffn[0][None], w_r, b_r,
                     g_final[None], w_gate[0], w_up[0], w_down[0])
    return out.reshape(b, s, d)
```

```python
import functools

import jax
import jax.numpy as jnp
from jax import lax
from jax.experimental import pallas as pl
from jax.experimental.pallas import tpu as pltpu

EPS = 1e-6
CONV_GROUP_WIDTH = 512
GMLP_HEADS = 8
GMLP_HEAD_DIM = 64
CHUNK = 128
CONV_K = 3
XA_HEADS = 4
N_GROUPS = 4
EXPERTS_PER_GROUP = 8
N_EXPERTS = N_GROUPS * EXPERTS_PER_GROUP
TOP_K = 2
ROUTE_LANES = 128
EXPERT_LANE0 = N_GROUPS
ROUTE_EXPERT_ROW0 = 8

N_PAIRS = EXPERTS_PER_GROUP * (EXPERTS_PER_GROUP - 1) // 2
N_CLASSES = N_GROUPS * N_PAIRS
PAIRS = [(a, b) for a in range(EXPERTS_PER_GROUP) for b in range(a + 1, EXPERTS_PER_GROUP)]

TOKEN_TILE = 512
SUB_TILE = 256
ROW_BLOCK = 128
BLOCKS_PER_STEP = 2
COPY_TILE = 2048
VMEM_LIMIT_BYTES = 56 * 1024 * 1024

SUBLANES = 8

_NEG = -1e30


def _rms(x, g):
    return x * lax.rsqrt(jnp.mean(x * x, axis=-1, keepdims=True) + EPS) * g


def _gelu_tanh(x):
    return 0.5 * x * (1.0 + jnp.tanh(0.7978845608028654 * (x + 0.044715 * (x * x * x))))


def _kv_kernel(mem_ref, g_ref, wk_ref, wv_ref, kt_ref, v_ref):
    m = _rms(mem_ref[0], g_ref[...]).astype(jnp.bfloat16)
    k = jnp.dot(m, wk_ref[...], preferred_element_type=jnp.float32)
    v = jnp.dot(m, wv_ref[...], preferred_element_type=jnp.float32)
    kt_ref[0] = k.T.astype(jnp.bfloat16)
    v_ref[0] = v.astype(jnp.bfloat16)


def _kv_proj(mem, g_mem, w_k, w_v):
    b, m, d = mem.shape
    const = lambda i: (0, 0)
    return pl.pallas_call(
        _kv_kernel,
        grid=(b,),
        in_specs=[
            pl.BlockSpec((1, m, d), lambda i: (i, 0, 0)),
            pl.BlockSpec((1, d), const),
            pl.BlockSpec((d, d), const),
            pl.BlockSpec((d, d), const),
        ],
        out_specs=[
            pl.BlockSpec((1, d, m), lambda i: (i, 0, 0)),
            pl.BlockSpec((1, m, d), lambda i: (i, 0, 0)),
        ],
        out_shape=[
            jax.ShapeDtypeStruct((b, d, m), jnp.bfloat16),
            jax.ShapeDtypeStruct((b, m, d), jnp.bfloat16),
        ],
        compiler_params=pltpu.CompilerParams(
            dimension_semantics=("arbitrary",), vmem_limit_bytes=VMEM_LIMIT_BYTES),
        name="kv_proj",
    )(mem, g_mem, w_k, w_v)


def _mixer_kernel(x_ref, gmix_ref, win_ref, convw_ref, gv_ref, ws_ref, bias_ref,
                  goc_ref, gog_ref, wout_ref, hsum_ref, o_ref, zbuf_ref):
    ts = x_ref.shape[1]
    w = CONV_GROUP_WIDTH
    xt = x_ref[0]
    h = _rms(xt, gmix_ref[...]).astype(jnp.bfloat16)

    def proj(i):
        return jnp.dot(h, win_ref[:, i * w:(i + 1) * w], preferred_element_type=jnp.float32)

    z = proj(1) * proj(2)

    @pl.when(pl.program_id(1) == 0)
    def _():
        zbuf_ref[0:8, :] = jnp.zeros((8, w), jnp.float32)

    @pl.when(pl.program_id(1) != 0)
    def _():
        zbuf_ref[0:8, :] = zbuf_ref[ts:ts + 8, :]

    zbuf_ref[8:8 + ts, :] = z
    zc = (convw_ref[0:1, :] * zbuf_ref[6:6 + ts, :]
          + convw_ref[1:2, :] * zbuf_ref[7:7 + ts, :]
          + convw_ref[2:3, :] * z)
    ya = _rms(proj(0) * zc, goc_ref[...]).astype(jnp.bfloat16)

    u = _gelu_tanh(proj(3))
    v = _gelu_tanh(proj(4))
    v2 = v * v
    v2_hi = v2.astype(jnp.bfloat16)
    v2_lo = (v2 - v2_hi.astype(jnp.float32)).astype(jnp.bfloat16)
    ss = (jnp.dot(v2_hi, hsum_ref[...], preferred_element_type=jnp.float32)
          + jnp.dot(v2_lo, hsum_ref[...], preferred_element_type=jnp.float32))
    vn = (v * lax.rsqrt(ss * (1.0 / GMLP_HEAD_DIM) + EPS) * gv_ref[...]).astype(jnp.bfloat16)

    row = lax.broadcasted_iota(jnp.int32, (CHUNK, CHUNK), 0)
    col = lax.broadcasted_iota(jnp.int32, (CHUNK, CHUNK), 1)
    causal = row >= col
    ws = [jnp.where(causal, ws_ref[hd], 0.0).astype(jnp.bfloat16) for hd in range(GMLP_HEADS)]
    chunks = []
    for c in range(ts // CHUNK):
        vc = vn[c * CHUNK:(c + 1) * CHUNK, :]
        heads = [jnp.dot(ws[hd], vc[:, hd * GMLP_HEAD_DIM:(hd + 1) * GMLP_HEAD_DIM],
                         preferred_element_type=jnp.float32) for hd in range(GMLP_HEADS)]
        chunks.append(jnp.concatenate(heads, axis=1) + bias_ref[...])
    s = jnp.concatenate(chunks, axis=0)
    yb = _rms(u * s, gog_ref[...]).astype(jnp.bfloat16)

    o_ref[0] = (xt
                + jnp.dot(ya, wout_ref[0:w, :], preferred_element_type=jnp.float32)
                + jnp.dot(yb, wout_ref[w:2 * w, :], preferred_element_type=jnp.float32))


def _mixer(x, g_mix, w_in, conv_w, g_v, w_s, bias, g_oc, g_og, w_out, hsum):
    b, s, d = x.shape
    ts = min(TOKEN_TILE, s)
    const2 = lambda i, j: (0, 0)
    const3 = lambda i, j: (0, 0, 0)
    return pl.pallas_call(
        _mixer_kernel,
        grid=(b, s // ts),
        in_specs=[
            pl.BlockSpec((1, ts, d), lambda i, j: (i, j, 0)),
            pl.BlockSpec(g_mix.shape, const2),
            pl.BlockSpec(w_in.shape, const2),
            pl.BlockSpec(conv_w.shape, const2),
            pl.BlockSpec(g_v.shape, const2),
            pl.BlockSpec(w_s.shape, const3),
            pl.BlockSpec(bias.shape, const2),
            pl.BlockSpec(g_oc.shape, const2),
            pl.BlockSpec(g_og.shape, const2),
            pl.BlockSpec(w_out.shape, const2),
            pl.BlockSpec(hsum.shape, const2),
        ],
        out_specs=pl.BlockSpec((1, ts, d), lambda i, j: (i, j, 0)),
        out_shape=jax.ShapeDtypeStruct((b, s, d), jnp.float32),
        scratch_shapes=[pltpu.VMEM((ts + 8, CONV_GROUP_WIDTH), jnp.float32)],
        compiler_params=pltpu.CompilerParams(
            dimension_semantics=("arbitrary", "arbitrary"), vmem_limit_bytes=VMEM_LIMIT_BYTES),
        name="mixer",
    )(x, g_mix, w_in, conv_w, g_v, w_s, bias, g_oc, g_og, w_out, hsum)


def _attn_route_kernel(x_ref, kt_ref, v_ref, gx_ref, wq_ref, wo_ref, gf_ref, wrt_ref, brt_ref,
                       x2_ref, slab_ref, counts_ref, carry_ref):
    ts = x_ref.shape[1]
    d = x_ref.shape[2]
    hd = d // XA_HEADS
    nsub = ts // SUB_TILE
    first = (pl.program_id(0) == 0) & (pl.program_id(1) == 0)

    @pl.when(first)
    def _():
        carry_ref[...] = jnp.zeros_like(carry_ref)

    def dot(a, b):
        return jnp.dot(a, b, preferred_element_type=jnp.float32)

    subs = range(nsub)
    x1 = [x_ref[0, i * SUB_TILE:(i + 1) * SUB_TILE, :] for i in subs]
    h2 = [_rms(x, gx_ref[...]).astype(jnp.bfloat16) for x in x1]
    q = [dot(h, wq_ref[...]).astype(jnp.bfloat16) for h in h2]
    heads = [[] for _ in subs]
    for a in range(XA_HEADS):
        sc = [dot(q[i][:, a * hd:(a + 1) * hd], kt_ref[0, a * hd:(a + 1) * hd, :]) * (hd ** -0.5)
              for i in subs]
        p = [jnp.exp(s_ - jnp.max(s_, axis=-1, keepdims=True)) for s_ in sc]
        l = [jnp.sum(p_, axis=-1, keepdims=True) for p_ in p]
        o = [dot(p[i].astype(jnp.bfloat16), v_ref[0, :, a * hd:(a + 1) * hd]) for i in subs]
        for i in subs:
            heads[i].append((o[i] / l[i]).astype(jnp.bfloat16))
    x2 = [x1[i] + dot(jnp.concatenate(heads[i], axis=1), wo_ref[...]) for i in subs]
    for i in subs:
        x2_ref[0, i * SUB_TILE:(i + 1) * SUB_TILE, :] = x2[i]

    h3 = [_rms(x, gf_ref[...]).astype(jnp.bfloat16) for x in x2]
    lgs = [lax.dot_general(wrt_ref[...], h, (((1,), (1,)), ((), ())),
                           preferred_element_type=jnp.float32) + brt_ref[...] for h in h3]
    sub = lax.broadcasted_iota(jnp.int32, (EXPERTS_PER_GROUP, SUB_TILE), 0).astype(jnp.float32)
    big = float(EXPERTS_PER_GROUP)
    is_g = sub < N_GROUPS

    def classify(lg):
        glog = lg[0:EXPERTS_PER_GROUP, :]
        gmax = jnp.max(jnp.where(is_g, glog, _NEG), axis=0, keepdims=True)
        gidx = jnp.min(jnp.where(is_g & (glog == gmax), sub, big), axis=0, keepdims=True)
        el = lg[ROUTE_EXPERT_ROW0:ROUTE_EXPERT_ROW0 + EXPERTS_PER_GROUP, :]
        for grp in range(1, N_GROUPS):
            r0 = ROUTE_EXPERT_ROW0 + grp * EXPERTS_PER_GROUP
            el = jnp.where(gidx == grp, lg[r0:r0 + EXPERTS_PER_GROUP, :], el)
        t1 = jnp.max(el, axis=0, keepdims=True)
        i1 = jnp.min(jnp.where(el == t1, sub, big), axis=0, keepdims=True)
        rest = sub != i1
        t2 = jnp.max(jnp.where(rest, el, _NEG), axis=0, keepdims=True)
        i2 = jnp.min(jnp.where(rest & (el == t2), sub, big), axis=0, keepdims=True)
        a = jnp.minimum(i1, i2)
        b = jnp.maximum(i1, i2)
        pair = a * (2 * EXPERTS_PER_GROUP - 1 - a) * 0.5 + (b - a - 1.0)
        return gidx * N_PAIRS + pair

    cls = [classify(lg) for lg in lgs]

    r = lax.broadcasted_iota(jnp.int32, (SUB_TILE, SUB_TILE), 0)
    c = lax.broadcasted_iota(jnp.int32, (SUB_TILE, SUB_TILE), 1)
    earlier = jnp.where(r < c, 1.0, 0.0).astype(jnp.bfloat16)
    crow = lax.broadcasted_iota(jnp.int32, (ROUTE_LANES, SUB_TILE), 0).astype(jnp.float32)
    oh = [crow == cl for cl in cls]
    ohf = [jnp.where(o_, 1.0, 0.0) for o_ in oh]
    inside = [dot(o_.astype(jnp.bfloat16), earlier) for o_ in ohf]
    seen = carry_ref[...]
    for i in subs:
        rank = jnp.sum(jnp.where(oh[i], inside[i] + seen, 0.0), axis=0, keepdims=True)
        slab_ref[:, i * SUB_TILE:(i + 1) * SUB_TILE] = jnp.where(
            sub == 0, cls[i], jnp.where(sub == 1, rank, 0.0))
        seen = seen + jnp.sum(ohf[i], axis=1, keepdims=True)
    carry_ref[...] = seen
    counts_ref[...] = seen


def _attn_route(x1, kt, v, g_x, w_q, w_o, g_f, w_rt, b_rt):
    b, s, d = x1.shape
    m = v.shape[1]
    ts = min(TOKEN_TILE, s)
    nt = s // ts
    const2 = lambda i, j: (0, 0)
    return pl.pallas_call(
        _attn_route_kernel,
        grid=(b, nt),
        in_specs=[
            pl.BlockSpec((1, ts, d), lambda i, j: (i, j, 0)),
            pl.BlockSpec((1, d, m), lambda i, j: (i, 0, 0)),
            pl.BlockSpec((1, m, d), lambda i, j: (i, 0, 0)),
            pl.BlockSpec(g_x.shape, const2),
            pl.BlockSpec(w_q.shape, const2),
            pl.BlockSpec(w_o.shape, const2),
            pl.BlockSpec(g_f.shape, const2),
            pl.BlockSpec(w_rt.shape, const2),
            pl.BlockSpec(b_rt.shape, const2),
        ],
        out_specs=[
            pl.BlockSpec((1, ts, d), lambda i, j: (i, j, 0)),
            pl.BlockSpec((SUBLANES, ts), lambda i, j: (0, i * nt + j)),
            pl.BlockSpec((ROUTE_LANES, 1), const2),
        ],
        out_shape=[
            jax.ShapeDtypeStruct((b, s, d), jnp.float32),
            jax.ShapeDtypeStruct((SUBLANES, b * s), jnp.float32),
            jax.ShapeDtypeStruct((ROUTE_LANES, 1), jnp.float32),
        ],
        scratch_shapes=[pltpu.VMEM((ROUTE_LANES, 1), jnp.float32)],
        compiler_params=pltpu.CompilerParams(
            dimension_semantics=("arbitrary", "arbitrary"), vmem_limit_bytes=VMEM_LIMIT_BYTES),
        name="attn_route",
    )(x1, kt, v, g_x, w_q, w_o, g_f, w_rt, b_rt)


def _dest_kernel(slab_ref, starts_ref, o_ref):
    slab = slab_ref[...]
    tt = slab.shape[1]
    crow = lax.broadcasted_iota(jnp.int32, (ROUTE_LANES, tt), 0).astype(jnp.float32)
    start = jnp.sum(jnp.where(crow == slab[0:1, :], starts_ref[...], 0.0), axis=0, keepdims=True)
    sub = lax.broadcasted_iota(jnp.int32, slab.shape, 0)
    o_ref[...] = jnp.where(sub == 0, start + slab[1:2, :], 0.0).astype(jnp.int32)


def _dest_rows(slab, starts_col):
    t = slab.shape[1]
    tt = min(2048, t)
    return pl.pallas_call(
        _dest_kernel,
        grid=(t // tt,),
        in_specs=[pl.BlockSpec((SUBLANES, tt), lambda i: (0, i)),
                  pl.BlockSpec((ROUTE_LANES, 1), lambda i: (0, 0))],
        out_specs=pl.BlockSpec((SUBLANES, tt), lambda i: (0, i)),
        out_shape=jax.ShapeDtypeStruct((SUBLANES, t), jnp.int32),
        compiler_params=pltpu.CompilerParams(dimension_semantics=("arbitrary",)),
        name="dest_rows",
    )(slab, starts_col)


def _wait_rows(src_ref, dst_ref, sem, n):
    pltpu.make_async_copy(src_ref.at[pl.ds(0, n)], dst_ref.at[pl.ds(0, n)], sem).wait()


def _dispatch_kernel(zero_block_ref, dest_ref, x_ref, xs_ref, zero_ref, sem, zsem):
    tiles, sub, _ = x_ref.shape
    nb = zero_block_ref.shape[0]

    def block_copy(j):
        start = pl.multiple_of(j * ROW_BLOCK, ROW_BLOCK)
        return pltpu.make_async_copy(zero_ref, xs_ref.at[pl.ds(start, ROW_BLOCK)], zsem)

    @pl.when(pl.program_id(0) == 0)
    def _():
        zero_ref[...] = jnp.zeros_like(zero_ref)

        @pl.loop(0, nb)
        def _(j):
            @pl.when(zero_block_ref[j] > 0)
            def _():
                block_copy(j).start()

        @pl.loop(0, nb)
        def _(j):
            @pl.when(zero_block_ref[j] > 0)
            def _():
                block_copy(j).wait()

    def issue(i, carry):
        for k in range(sub):
            pltpu.make_async_copy(x_ref.at[i, pl.ds(k, 1)],
                                  xs_ref.at[pl.ds(dest_ref[i * sub + k], 1)], sem).start()
        return carry

    lax.fori_loop(0, tiles, issue, 0)
    _wait_rows(xs_ref, xs_ref, sem, tiles * sub)


def _dispatch(x2, dest, zero_block):
    t, d = x2.shape
    tc = min(COPY_TILE, t)
    cap = zero_block.shape[0] * ROW_BLOCK
    grid_spec = pltpu.PrefetchScalarGridSpec(
        num_scalar_prefetch=1,
        grid=(t // tc,),
        in_specs=[
            pl.BlockSpec((tc,), lambda i, zb: (i,), memory_space=pltpu.SMEM),
            pl.BlockSpec((tc // SUBLANES, SUBLANES, d), lambda i, zb: (i, 0, 0)),
        ],
        out_specs=pl.BlockSpec(memory_space=pl.ANY),
        scratch_shapes=[pltpu.VMEM((ROW_BLOCK, d), jnp.float32),
                        pltpu.SemaphoreType.DMA, pltpu.SemaphoreType.DMA],
    )
    return pl.pallas_call(
        _dispatch_kernel,
        grid_spec=grid_spec,
        out_shape=jax.ShapeDtypeStruct((cap, d), jnp.float32),
        compiler_params=pltpu.CompilerParams(
            dimension_semantics=("arbitrary",), has_side_effects=True),
        name="dispatch",
    )(zero_block, dest, x2.reshape(t // SUBLANES, SUBLANES, d))


def _unsort_kernel(dest_ref, ys_ref, o_ref, sem):
    tiles, sub, _ = o_ref.shape

    def issue(i, carry):
        for k in range(sub):
            pltpu.make_async_copy(ys_ref.at[pl.ds(dest_ref[i * sub + k], 1)],
                                  o_ref.at[i, pl.ds(k, 1)], sem).start()
        return carry

    lax.fori_loop(0, tiles, issue, 0)
    _wait_rows(ys_ref, ys_ref, sem, tiles * sub)


def _unsort(ys, dest, t):
    d = ys.shape[1]
    tc = min(COPY_TILE, t)
    return pl.pallas_call(
        _unsort_kernel,
        grid=(t // tc,),
        in_specs=[
            pl.BlockSpec((tc,), lambda i: (i,), memory_space=pltpu.SMEM),
            pl.BlockSpec(memory_space=pl.ANY),
        ],
        out_specs=pl.BlockSpec((tc // SUBLANES, SUBLANES, d), lambda i: (i, 0, 0)),
        out_shape=jax.ShapeDtypeStruct((t // SUBLANES, SUBLANES, d), jnp.float32),
        scratch_shapes=[pltpu.SemaphoreType.DMA],
        compiler_params=pltpu.CompilerParams(dimension_semantics=("arbitrary",)),
        name="unsort",
    )(dest, ys).reshape(t, d)


def _expert_kernel(grp_ref, ea_ref, eb_ref, nact_ref,
                   xs_ref, gf_ref, wr_ref, br_ref, gfin_ref, wg_hbm, wu_hbm, wd_hbm,
                   o_ref, wgb, wub, wdb, sg, su, sd, sem, cnt_ref):
    s = pl.program_id(0)
    blocks = tuple(BLOCKS_PER_STEP * s + i for i in range(BLOCKS_PER_STEP))
    g = grp_ref[blocks[0]]

    def stage(e):
        slot = e % 2
        return (pltpu.make_async_copy(wg_hbm.at[e], sg.at[slot], sem.at[slot, 0]),
                pltpu.make_async_copy(wu_hbm.at[e], su.at[slot], sem.at[slot, 1]),
                pltpu.make_async_copy(wd_hbm.at[e], sd.at[slot], sem.at[slot, 2]))

    def start_next():
        @pl.when(cnt_ref[0] < N_EXPERTS)
        def _():
            for cp in stage(cnt_ref[0]):
                cp.start()
            cnt_ref[0] = cnt_ref[0] + 1

    @pl.when(s == 0)
    def _():
        cnt_ref[0] = 0
        cnt_ref[1] = 0
        start_next()
        start_next()

    active = blocks[0] < nact_ref[0]

    @pl.when(active)
    def _():
        need = g * EXPERTS_PER_GROUP + functools.reduce(
            jnp.maximum, [eb_ref[j] for j in blocks])

        def load(e, carry):
            for cp in stage(e):
                cp.wait()
            slot = e % 2
            k = e % EXPERTS_PER_GROUP
            wgb[k] = sg[slot].astype(jnp.bfloat16)
            wub[k] = su[slot].astype(jnp.bfloat16)
            wdb[k] = sd[slot].astype(jnp.bfloat16)
            cnt_ref[1] = e + 1
            start_next()
            return carry

        lax.fori_loop(cnt_ref[1], need + 1, load, 0)

        lane = lax.broadcasted_iota(jnp.int32, (ROW_BLOCK, ROUTE_LANES), 1)
        is_g = lane < N_GROUPS
        lo = EXPERT_LANE0 + EXPERTS_PER_GROUP * g
        nblk = len(blocks)
        xs = [xs_ref[i * ROW_BLOCK:(i + 1) * ROW_BLOCK, :] for i in range(nblk)]
        h3 = [_rms(x, gf_ref[...]).astype(jnp.bfloat16) for x in xs]
        lg = [jnp.dot(h, wr_ref[...], preferred_element_type=jnp.float32) + br_ref[...]
              for h in h3]

        def gates(lgi, j):
            def pick(idx):
                return jnp.sum(jnp.where(lane == idx, lgi, 0.0), axis=-1, keepdims=True)

            gmax = jnp.max(jnp.where(is_g, lgi, _NEG), axis=-1, keepdims=True)
            den = jnp.sum(jnp.where(is_g, jnp.exp(jnp.where(is_g, lgi, _NEG) - gmax), 0.0),
                          axis=-1, keepdims=True)
            grp_p = jnp.exp(pick(g) - gmax) / den
            la = pick(lo + ea_ref[j])
            lb = pick(lo + eb_ref[j])
            m = jnp.maximum(la, lb)
            pa = jnp.exp(la - m)
            pb = jnp.exp(lb - m)
            return grp_p * pa / (pa + pb), grp_p * pb / (pa + pb)

        gate = [gates(lg[i], blocks[i]) for i in range(nblk)]
        ys = list(xs)
        for which, e_ref in enumerate((ea_ref, eb_ref)):
            ks = [e_ref[j] for j in blocks]
            gg = [jnp.dot(h3[i], wgb[ks[i]], preferred_element_type=jnp.float32)
                  for i in range(nblk)]
            uu = [jnp.dot(h3[i], wub[ks[i]], preferred_element_type=jnp.float32)
                  for i in range(nblk)]
            act = [(gg[i] * (1.0 / (1.0 + jnp.exp(-gg[i]))) * uu[i]
                    * gate[i][which]).astype(jnp.bfloat16) for i in range(nblk)]
            ys = [ys[i] + jnp.dot(act[i], wdb[ks[i]], preferred_element_type=jnp.float32)
                  for i in range(nblk)]
        for i in range(nblk):
            o_ref[i * ROW_BLOCK:(i + 1) * ROW_BLOCK, :] = _rms(ys[i], gfin_ref[...])

    @pl.when(jnp.logical_not(active))
    def _():
        o_ref[...] = jnp.zeros_like(o_ref)

    @pl.when(s == pl.num_programs(0) - 1)
    def _():
        def drain(e, carry):
            for cp in stage(e):
                cp.wait()
            return carry

        lax.fori_loop(cnt_ref[1], cnt_ref[0], drain, 0)


def _experts(xs, blk_grp, blk_a, blk_b, nact, g_ffn, w_r, b_r, g_final, w_gate, w_up, w_down):
    cap, d = xs.shape
    de = w_gate.shape[2]
    step_rows = BLOCKS_PER_STEP * ROW_BLOCK
    steps = cap // step_rows
    pre = lambda f: (lambda s, gr, ea, eb, na: f(s, na))
    const2 = pre(lambda s, na: (0, 0))
    last_step = lambda na: (na[0] - 1) // BLOCKS_PER_STEP
    hbm = pl.BlockSpec(memory_space=pl.ANY)
    grid_spec = pltpu.PrefetchScalarGridSpec(
        num_scalar_prefetch=4,
        grid=(steps,),
        in_specs=[
            pl.BlockSpec((step_rows, d), pre(lambda s, na: (jnp.minimum(s, last_step(na)), 0))),
            pl.BlockSpec(g_ffn.shape, const2),
            pl.BlockSpec(w_r.shape, const2),
            pl.BlockSpec(b_r.shape, const2),
            pl.BlockSpec(g_final.shape, const2),
            hbm, hbm, hbm,
        ],
        out_specs=pl.BlockSpec((step_rows, d), pre(lambda s, na: (s, 0))),
        scratch_shapes=[
            pltpu.VMEM((EXPERTS_PER_GROUP, d, de), jnp.bfloat16),
            pltpu.VMEM((EXPERTS_PER_GROUP, d, de), jnp.bfloat16),
            pltpu.VMEM((EXPERTS_PER_GROUP, de, d), jnp.bfloat16),
            pltpu.VMEM((2, d, de), jnp.float32),
            pltpu.VMEM((2, d, de), jnp.float32),
            pltpu.VMEM((2, de, d), jnp.float32),
            pltpu.SemaphoreType.DMA((2, 3)),
            pltpu.SMEM((2,), jnp.int32),
        ],
    )
    return pl.pallas_call(
        _expert_kernel,
        grid_spec=grid_spec,
        out_shape=jax.ShapeDtypeStruct((cap, d), jnp.float32),
        compiler_params=pltpu.CompilerParams(
            dimension_semantics=("arbitrary",), vmem_limit_bytes=VMEM_LIMIT_BYTES),
        name="experts",
    )(blk_grp, blk_a, blk_b, nact, xs, g_ffn, w_r, b_r, g_final, w_gate, w_up, w_down)


def _moe_final(x2, slab, counts_col, g_ffn, w_r, b_r, g_final, w_gate, w_up, w_down):
    t, d = x2.shape
    nb = t // ROW_BLOCK + N_CLASSES + N_GROUPS * (BLOCKS_PER_STEP - 1)
    assert nb % BLOCKS_PER_STEP == 0
    counts = counts_col[:N_CLASSES, 0].astype(jnp.int32)
    nblk = (counts + ROW_BLOCK - 1) // ROW_BLOCK
    grp_blocks = jnp.sum(nblk.reshape(N_GROUPS, N_PAIRS), axis=1)
    nblk = nblk.reshape(N_GROUPS, N_PAIRS).at[:, N_PAIRS - 1].add(
        (-grp_blocks) % BLOCKS_PER_STEP).reshape(N_CLASSES)
    blk_end = jnp.cumsum(nblk)
    blk_start = blk_end - nblk
    nact = blk_end[-1]
    j = jnp.arange(nb, dtype=jnp.int32)
    blk_class = jnp.minimum(
        jnp.sum((blk_end[None, :] <= j[:, None]).astype(jnp.int32), axis=1), N_CLASSES - 1)
    blk_class = jnp.where(j < nact, blk_class, blk_class[jnp.maximum(nact - 1, 0)])
    not_full = (j - blk_start[blk_class] + 1) * ROW_BLOCK > counts[blk_class]
    zero_block = (not_full | (j >= nact)).astype(jnp.int32)
    pair_a = jnp.array([p[0] for p in PAIRS], jnp.int32)
    pair_b = jnp.array([p[1] for p in PAIRS], jnp.int32)
    blk_grp = blk_class // N_PAIRS
    blk_a = pair_a[blk_class % N_PAIRS]
    blk_b = pair_b[blk_class % N_PAIRS]
    starts_col = jnp.zeros((ROUTE_LANES, 1), jnp.float32).at[:N_CLASSES, 0].set(
        (blk_start * ROW_BLOCK).astype(jnp.float32))

    dest = _dest_rows(slab, starts_col)[0]
    xs = _dispatch(x2, dest, zero_block)
    ys = _experts(xs, blk_grp, blk_a, blk_b, nact[None], g_ffn, w_r, b_r, g_final,
                  w_gate, w_up, w_down)
    return _unsort(ys, dest, t)


def kernel(x, mem, g_mix, w_in, conv_w, g_v, w_s, b_s, g_out_conv, g_out_gmlp, w_out, g_xattn,
           g_mem, w_q, w_k, w_v, w_o, g_ffn, w_grp, b_grp, w_rt, b_rt, w_gate, w_up, w_down,
           g_final):
    b, s, d = x.shape
    assert g_mix.shape[0] == 1, "the final norm is fused into the single layer's expert kernel"
    assert N_CLASSES <= ROUTE_LANES
    bf = jnp.bfloat16
    kt, v = _kv_proj(mem, g_mem[0][None], w_k[0].astype(bf), w_v[0].astype(bf))

    bias = jnp.repeat(b_s[0].T, GMLP_HEAD_DIM, axis=1)
    head_of_lane = jnp.arange(GMLP_HEADS * GMLP_HEAD_DIM) // GMLP_HEAD_DIM
    hsum = (head_of_lane[:, None] == head_of_lane[None, :]).astype(bf)
    x1 = _mixer(x, g_mix[0][None], w_in[0].astype(bf), conv_w[0], g_v[0][None], w_s[0], bias,
                g_out_conv[0][None], g_out_gmlp[0][None], w_out[0].astype(bf), hsum)

    pad = ROUTE_LANES - N_GROUPS - N_EXPERTS
    w_r = jnp.concatenate([w_grp[0], w_rt[0], jnp.zeros((d, pad), jnp.float32)], axis=1).astype(bf)
    b_r = jnp.concatenate([b_grp[0], b_rt[0], jnp.zeros((pad,), jnp.float32)])[None]
    gap = ROUTE_EXPERT_ROW0 - N_GROUPS
    tail = ROUTE_LANES - ROUTE_EXPERT_ROW0 - N_EXPERTS
    w_r_t = jnp.concatenate([w_grp[0].T, jnp.zeros((gap, d), jnp.float32), w_rt[0].T,
                             jnp.zeros((tail, d), jnp.float32)], axis=0).astype(bf)
    b_r_t = jnp.concatenate([b_grp[0], jnp.zeros((gap,), jnp.float32), b_rt[0],
                             jnp.zeros((tail,), jnp.float32)])[:, None]
    x2, slab, counts_col = _attn_route(x1, kt, v, g_xattn[0][None], w_q[0].astype(bf),
                                       w_o[0].astype(bf), g_ffn[0][None], w_r_t, b_r_t)
    out = _moe_final(x2.reshape(b * s, d), slab, counts_col, g_ffn[0][None], w_r, b_r,
                     g_final[None], w_gate[0], w_up[0], w_down[0])
    return out.reshape(b, s, d)
```

```python
import functools

import jax
import jax.numpy as jnp
from jax import lax
from jax.experimental import pallas as pl
from jax.experimental.pallas import tpu as pltpu

EPS = 1e-6
CONV_GROUP_WIDTH = 512
GMLP_HEADS = 8
GMLP_HEAD_DIM = 64
CHUNK = 128
CONV_K = 3
XA_HEADS = 4
N_GROUPS = 4
EXPERTS_PER_GROUP = 8
N_EXPERTS = N_GROUPS * EXPERTS_PER_GROUP
TOP_K = 2
ROUTE_LANES = 128
EXPERT_LANE0 = N_GROUPS
ROUTE_EXPERT_ROW0 = 8

N_PAIRS = EXPERTS_PER_GROUP * (EXPERTS_PER_GROUP - 1) // 2
N_CLASSES = N_GROUPS * N_PAIRS
PAIRS = [(a, b) for a in range(EXPERTS_PER_GROUP) for b in range(a + 1, EXPERTS_PER_GROUP)]

TOKEN_TILE = 1024
SUB_TILE = 256
ROW_BLOCK = 128
BLOCKS_PER_STEP = 2
COPY_TILE = 2048
VMEM_LIMIT_BYTES = 56 * 1024 * 1024

SUBLANES = 8

_NEG = -1e30


def _rms(x, g):
    return x * lax.rsqrt(jnp.mean(x * x, axis=-1, keepdims=True) + EPS) * g


def _gelu_tanh(x):
    return 0.5 * x * (1.0 + jnp.tanh(0.7978845608028654 * (x + 0.044715 * (x * x * x))))


def _kv_kernel(mem_ref, g_ref, wk_ref, wv_ref, kt_ref, v_ref):
    m = _rms(mem_ref[0], g_ref[...]).astype(jnp.bfloat16)
    k = jnp.dot(m, wk_ref[...], preferred_element_type=jnp.float32)
    v = jnp.dot(m, wv_ref[...], preferred_element_type=jnp.float32)
    kt_ref[0] = k.T.astype(jnp.bfloat16)
    v_ref[0] = v.astype(jnp.bfloat16)


def _kv_proj(mem, g_mem, w_k, w_v):
    b, m, d = mem.shape
    const = lambda i: (0, 0)
    return pl.pallas_call(
        _kv_kernel,
        grid=(b,),
        in_specs=[
            pl.BlockSpec((1, m, d), lambda i: (i, 0, 0)),
            pl.BlockSpec((1, d), const),
            pl.BlockSpec((d, d), const),
            pl.BlockSpec((d, d), const),
        ],
        out_specs=[
            pl.BlockSpec((1, d, m), lambda i: (i, 0, 0)),
            pl.BlockSpec((1, m, d), lambda i: (i, 0, 0)),
        ],
        out_shape=[
            jax.ShapeDtypeStruct((b, d, m), jnp.bfloat16),
            jax.ShapeDtypeStruct((b, m, d), jnp.bfloat16),
        ],
        compiler_params=pltpu.CompilerParams(
            dimension_semantics=("arbitrary",), vmem_limit_bytes=VMEM_LIMIT_BYTES),
        name="kv_proj",
    )(mem, g_mem, w_k, w_v)


def _mixer_kernel(x_ref, gmix_ref, win_ref, convw_ref, gv_ref, ws_ref, bias_ref,
                  goc_ref, gog_ref, wout_ref, o_ref, zbuf_ref):
    ts = x_ref.shape[1]
    w = CONV_GROUP_WIDTH
    hw = 2 * GMLP_HEAD_DIM
    nsub = ts // SUB_TILE

    def dot(a, b):
        return jnp.dot(a, b, preferred_element_type=jnp.float32)

    @pl.when(pl.program_id(1) == 0)
    def _():
        zbuf_ref[0:8, :] = jnp.zeros((8, w), jnp.float32)

    @pl.when(pl.program_id(1) != 0)
    def _():
        zbuf_ref[0:8, :] = zbuf_ref[ts:ts + 8, :]

    low = lax.broadcasted_iota(jnp.int32, (SUB_TILE, hw), 1) < GMLP_HEAD_DIM
    row = lax.broadcasted_iota(jnp.int32, (CHUNK, CHUNK), 0)
    colid = lax.broadcasted_iota(jnp.int32, (CHUNK, CHUNK), 1)
    causal = row >= colid
    ws = [jnp.where(causal, ws_ref[hd], 0.0).astype(jnp.bfloat16) for hd in range(GMLP_HEADS)]
    npair = SUB_TILE // CHUNK // 2
    lo = lax.broadcasted_iota(jnp.int32, (CHUNK, hw), 1) < GMLP_HEAD_DIM
    swap = lambda a: pltpu.roll(a, GMLP_HEAD_DIM, axis=1)

    def project(i):
        r0 = i * SUB_TILE
        xt = x_ref[0, r0:r0 + SUB_TILE, :]
        h = _rms(xt, gmix_ref[...]).astype(jnp.bfloat16)
        p = [dot(h, win_ref[:, k * w:(k + 1) * w]) for k in range(5)]
        z = p[1] * p[2]
        zbuf_ref[8 + r0:8 + r0 + SUB_TILE, :] = z
        return dict(xt=xt, gate_b=p[0], z=z, u=p[3], v=p[4], r0=r0)

    def gate_and_norm(st):
        r0 = st["r0"]
        zc = (convw_ref[0:1, :] * zbuf_ref[6 + r0:6 + r0 + SUB_TILE, :]
              + convw_ref[1:2, :] * zbuf_ref[7 + r0:7 + r0 + SUB_TILE, :]
              + convw_ref[2:3, :] * st["z"])
        st["ya"] = _rms(st["gate_b"] * zc, goc_ref[...]).astype(jnp.bfloat16)
        st["u"] = _gelu_tanh(st["u"])
        v = _gelu_tanh(st["v"])
        v2 = v * v
        ss_cols = []
        for k in range(GMLP_HEADS // 2):
            col = v2[:, hw * k:hw * (k + 1)]
            ss_cols.append(jnp.where(low,
                                     jnp.sum(jnp.where(low, col, 0.0), axis=1, keepdims=True),
                                     jnp.sum(jnp.where(low, 0.0, col), axis=1, keepdims=True)))
        ss = jnp.concatenate(ss_cols, axis=1)
        st["vn"] = v * lax.rsqrt(ss * (1.0 / GMLP_HEAD_DIM) + EPS) * gv_ref[...]

    def mix_positions(st):
        vn = st["vn"]
        s_cols = [[None] * (GMLP_HEADS // 2) for _ in range(2 * npair)]
        for hp in range(GMLP_HEADS // 2):
            cols = [vn[c * CHUNK:(c + 1) * CHUNK, hw * hp:hw * (hp + 1)] for c in range(2 * npair)]
            swapped = [swap(a) for a in cols]
            rhs_even = jnp.concatenate(
                [jnp.where(lo, cols[2 * p], swapped[2 * p + 1]) for p in range(npair)], axis=1)
            rhs_odd = jnp.concatenate(
                [jnp.where(lo, swapped[2 * p], cols[2 * p + 1]) for p in range(npair)], axis=1)
            out_e = dot(ws[2 * hp], rhs_even.astype(jnp.bfloat16))
            out_o = dot(ws[2 * hp + 1], rhs_odd.astype(jnp.bfloat16))
            for p in range(npair):
                e = out_e[:, hw * p:hw * (p + 1)]
                o = out_o[:, hw * p:hw * (p + 1)]
                s_cols[2 * p][hp] = jnp.where(lo, e, swap(o))
                s_cols[2 * p + 1][hp] = jnp.where(lo, swap(e), o)
        st["s"] = jnp.concatenate(
            [jnp.concatenate(c, axis=1) + bias_ref[...] for c in s_cols], axis=0)

    def output(st):
        r0 = st["r0"]
        yb = _rms(st["u"] * st["s"], gog_ref[...]).astype(jnp.bfloat16)
        o_ref[0, r0:r0 + SUB_TILE, :] = (st["xt"] + dot(st["ya"], wout_ref[0:w, :])
                                          + dot(yb, wout_ref[w:2 * w, :]))

    phases = (gate_and_norm, mix_positions, output)
    states = []
    for step in range(nsub + len(phases)):
        if step < nsub:
            states.append(project(step))
        for k, phase in enumerate(phases):
            i = step - 1 - k
            if 0 <= i < nsub:
                phase(states[i])


def _mixer(x, g_mix, w_in, conv_w, g_v, w_s, bias, g_oc, g_og, w_out):
    b, s, d = x.shape
    ts = min(TOKEN_TILE, s)
    const2 = lambda i, j: (0, 0)
    const3 = lambda i, j: (0, 0, 0)
    return pl.pallas_call(
        _mixer_kernel,
        grid=(b, s // ts),
        in_specs=[
            pl.BlockSpec((1, ts, d), lambda i, j: (i, j, 0)),
            pl.BlockSpec(g_mix.shape, const2),
            pl.BlockSpec(w_in.shape, const2),
            pl.BlockSpec(conv_w.shape, const2),
            pl.BlockSpec(g_v.shape, const2),
            pl.BlockSpec(w_s.shape, const3),
            pl.BlockSpec(bias.shape, const2),
            pl.BlockSpec(g_oc.shape, const2),
            pl.BlockSpec(g_og.shape, const2),
            pl.BlockSpec(w_out.shape, const2),
        ],
        out_specs=pl.BlockSpec((1, ts, d), lambda i, j: (i, j, 0)),
        out_shape=jax.ShapeDtypeStruct((b, s, d), jnp.float32),
        scratch_shapes=[pltpu.VMEM((ts + 8, CONV_GROUP_WIDTH), jnp.float32)],
        compiler_params=pltpu.CompilerParams(
            dimension_semantics=("arbitrary", "arbitrary"), vmem_limit_bytes=VMEM_LIMIT_BYTES),
        name="mixer",
    )(x, g_mix, w_in, conv_w, g_v, w_s, bias, g_oc, g_og, w_out)


def _attn_route_kernel(x_ref, kt_ref, v_ref, gx_ref, wq_ref, wo_ref, gf_ref, wrt_ref, brt_ref,
                       x2_ref, slab_ref, counts_ref, carry_ref):
    ts = x_ref.shape[1]
    d = x_ref.shape[2]
    hd = d // XA_HEADS
    nsub = ts // SUB_TILE
    first = (pl.program_id(0) == 0) & (pl.program_id(1) == 0)

    @pl.when(first)
    def _():
        carry_ref[...] = jnp.zeros_like(carry_ref)

    def dot(a, b):
        return jnp.dot(a, b, preferred_element_type=jnp.float32)

    subs = range(nsub)
    x1 = [x_ref[0, i * SUB_TILE:(i + 1) * SUB_TILE, :] for i in subs]
    h2 = [_rms(x, gx_ref[...]).astype(jnp.bfloat16) for x in x1]
    q = [dot(h, wq_ref[...]).astype(jnp.bfloat16) for h in h2]
    heads = [[] for _ in subs]
    for a in range(XA_HEADS):
        sc = [dot(q[i][:, a * hd:(a + 1) * hd], kt_ref[0, a * hd:(a + 1) * hd, :]) * (hd ** -0.5)
              for i in subs]
        p = [jnp.exp(s_ - jnp.max(s_, axis=-1, keepdims=True)) for s_ in sc]
        l = [jnp.sum(p_, axis=-1, keepdims=True) for p_ in p]
        o = [dot(p[i].astype(jnp.bfloat16), v_ref[0, :, a * hd:(a + 1) * hd]) for i in subs]
        for i in subs:
            heads[i].append((o[i] / l[i]).astype(jnp.bfloat16))
    x2 = [x1[i] + dot(jnp.concatenate(heads[i], axis=1), wo_ref[...]) for i in subs]
    for i in subs:
        x2_ref[0, i * SUB_TILE:(i + 1) * SUB_TILE, :] = x2[i]

    h3 = [_rms(x, gf_ref[...]).astype(jnp.bfloat16) for x in x2]
    lgs = [lax.dot_general(wrt_ref[...], h, (((1,), (1,)), ((), ())),
                           preferred_element_type=jnp.float32) + brt_ref[...] for h in h3]
    sub = lax.broadcasted_iota(jnp.int32, (EXPERTS_PER_GROUP, SUB_TILE), 0).astype(jnp.float32)
    big = float(EXPERTS_PER_GROUP)
    is_g = sub < N_GROUPS

    def classify(lg):
        glog = lg[0:EXPERTS_PER_GROUP, :]
        gmax = jnp.max(jnp.where(is_g, glog, _NEG), axis=0, keepdims=True)
        gidx = jnp.min(jnp.where(is_g & (glog == gmax), sub, big), axis=0, keepdims=True)
        el = lg[ROUTE_EXPERT_ROW0:ROUTE_EXPERT_ROW0 + EXPERTS_PER_GROUP, :]
        for grp in range(1, N_GROUPS):
            r0 = ROUTE_EXPERT_ROW0 + grp * EXPERTS_PER_GROUP
            el = jnp.where(gidx == grp, lg[r0:r0 + EXPERTS_PER_GROUP, :], el)
        t1 = jnp.max(el, axis=0, keepdims=True)
        i1 = jnp.min(jnp.where(el == t1, sub, big), axis=0, keepdims=True)
        rest = sub != i1
        t2 = jnp.max(jnp.where(rest, el, _NEG), axis=0, keepdims=True)
        i2 = jnp.min(jnp.where(rest & (el == t2), sub, big), axis=0, keepdims=True)
        a = jnp.minimum(i1, i2)
        b = jnp.maximum(i1, i2)
        pair = a * (2 * EXPERTS_PER_GROUP - 1 - a) * 0.5 + (b - a - 1.0)
        return gidx * N_PAIRS + pair

    cls = [classify(lg) for lg in lgs]

    r = lax.broadcasted_iota(jnp.int32, (SUB_TILE, SUB_TILE), 0)
    c = lax.broadcasted_iota(jnp.int32, (SUB_TILE, SUB_TILE), 1)
    earlier = jnp.where(r < c, 1.0, 0.0).astype(jnp.bfloat16)
    crow = lax.broadcasted_iota(jnp.int32, (ROUTE_LANES, SUB_TILE), 0).astype(jnp.float32)
    oh = [crow == cl for cl in cls]
    ohf = [jnp.where(o_, 1.0, 0.0) for o_ in oh]
    inside = [dot(o_.astype(jnp.bfloat16), earlier) for o_ in ohf]
    seen = carry_ref[...]
    for i in subs:
        rank = jnp.sum(jnp.where(oh[i], inside[i] + seen, 0.0), axis=0, keepdims=True)
        slab_ref[:, i * SUB_TILE:(i + 1) * SUB_TILE] = jnp.where(
            sub == 0, cls[i], jnp.where(sub == 1, rank, 0.0))
        seen = seen + jnp.sum(ohf[i], axis=1, keepdims=True)
    carry_ref[...] = seen
    counts_ref[...] = seen


def _attn_route(x1, kt, v, g_x, w_q, w_o, g_f, w_rt, b_rt):
    b, s, d = x1.shape
    m = v.shape[1]
    ts = min(TOKEN_TILE, s)
    nt = s // ts
    const2 = lambda i, j: (0, 0)
    return pl.pallas_call(
        _attn_route_kernel,
        grid=(b, nt),
        in_specs=[
            pl.BlockSpec((1, ts, d), lambda i, j: (i, j, 0)),
            pl.BlockSpec((1, d, m), lambda i, j: (i, 0, 0)),
            pl.BlockSpec((1, m, d), lambda i, j: (i, 0, 0)),
            pl.BlockSpec(g_x.shape, const2),
            pl.BlockSpec(w_q.shape, const2),
            pl.BlockSpec(w_o.shape, const2),
            pl.BlockSpec(g_f.shape, const2),
            pl.BlockSpec(w_rt.shape, const2),
            pl.BlockSpec(b_rt.shape, const2),
        ],
        out_specs=[
            pl.BlockSpec((1, ts, d), lambda i, j: (i, j, 0)),
            pl.BlockSpec((SUBLANES, ts), lambda i, j: (0, i * nt + j)),
            pl.BlockSpec((ROUTE_LANES, 1), const2),
        ],
        out_shape=[
            jax.ShapeDtypeStruct((b, s, d), jnp.float32),
            jax.ShapeDtypeStruct((SUBLANES, b * s), jnp.float32),
            jax.ShapeDtypeStruct((ROUTE_LANES, 1), jnp.float32),
        ],
        scratch_shapes=[pltpu.VMEM((ROUTE_LANES, 1), jnp.float32)],
        compiler_params=pltpu.CompilerParams(
            dimension_semantics=("arbitrary", "arbitrary"), vmem_limit_bytes=VMEM_LIMIT_BYTES),
        name="attn_route",
    )(x1, kt, v, g_x, w_q, w_o, g_f, w_rt, b_rt)


def _dest_kernel(slab_ref, starts_ref, o_ref):
    slab = slab_ref[...]
    tt = slab.shape[1]
    crow = lax.broadcasted_iota(jnp.int32, (ROUTE_LANES, tt), 0).astype(jnp.float32)
    start = jnp.sum(jnp.where(crow == slab[0:1, :], starts_ref[...], 0.0), axis=0, keepdims=True)
    sub = lax.broadcasted_iota(jnp.int32, slab.shape, 0)
    o_ref[...] = jnp.where(sub == 0, start + slab[1:2, :], 0.0).astype(jnp.int32)


def _dest_rows(slab, starts_col):
    t = slab.shape[1]
    tt = min(2048, t)
    return pl.pallas_call(
        _dest_kernel,
        grid=(t // tt,),
        in_specs=[pl.BlockSpec((SUBLANES, tt), lambda i: (0, i)),
                  pl.BlockSpec((ROUTE_LANES, 1), lambda i: (0, 0))],
        out_specs=pl.BlockSpec((SUBLANES, tt), lambda i: (0, i)),
        out_shape=jax.ShapeDtypeStruct((SUBLANES, t), jnp.int32),
        compiler_params=pltpu.CompilerParams(dimension_semantics=("arbitrary",)),
        name="dest_rows",
    )(slab, starts_col)


def _wait_rows(src_ref, dst_ref, sem, n):
    pltpu.make_async_copy(src_ref.at[pl.ds(0, n)], dst_ref.at[pl.ds(0, n)], sem).wait()


def _dispatch_kernel(zero_block_ref, dest_ref, x_ref, xs_ref, zero_ref, sem, zsem):
    tiles, sub, _ = x_ref.shape
    nb = zero_block_ref.shape[0]

    def block_copy(j):
        start = pl.multiple_of(j * ROW_BLOCK, ROW_BLOCK)
        return pltpu.make_async_copy(zero_ref, xs_ref.at[pl.ds(start, ROW_BLOCK)], zsem)

    @pl.when(pl.program_id(0) == 0)
    def _():
        zero_ref[...] = jnp.zeros_like(zero_ref)

        @pl.loop(0, nb)
        def _(j):
            @pl.when(zero_block_ref[j] > 0)
            def _():
                block_copy(j).start()

        @pl.loop(0, nb)
        def _(j):
            @pl.when(zero_block_ref[j] > 0)
            def _():
                block_copy(j).wait()

    def issue(i, carry):
        for k in range(sub):
            pltpu.make_async_copy(x_ref.at[i, pl.ds(k, 1)],
                                  xs_ref.at[pl.ds(dest_ref[i * sub + k], 1)], sem).start()
        return carry

    lax.fori_loop(0, tiles, issue, 0)
    _wait_rows(xs_ref, xs_ref, sem, tiles * sub)


def _dispatch(x2, dest, zero_block):
    t, d = x2.shape
    tc = min(COPY_TILE, t)
    cap = zero_block.shape[0] * ROW_BLOCK
    grid_spec = pltpu.PrefetchScalarGridSpec(
        num_scalar_prefetch=1,
        grid=(t // tc,),
        in_specs=[
            pl.BlockSpec((tc,), lambda i, zb: (i,), memory_space=pltpu.SMEM),
            pl.BlockSpec((tc // SUBLANES, SUBLANES, d), lambda i, zb: (i, 0, 0)),
        ],
        out_specs=pl.BlockSpec(memory_space=pl.ANY),
        scratch_shapes=[pltpu.VMEM((ROW_BLOCK, d), jnp.float32),
                        pltpu.SemaphoreType.DMA, pltpu.SemaphoreType.DMA],
    )
    return pl.pallas_call(
        _dispatch_kernel,
        grid_spec=grid_spec,
        out_shape=jax.ShapeDtypeStruct((cap, d), jnp.float32),
        compiler_params=pltpu.CompilerParams(
            dimension_semantics=("arbitrary",), has_side_effects=True),
        name="dispatch",
    )(zero_block, dest, x2.reshape(t // SUBLANES, SUBLANES, d))


def _unsort_kernel(dest_ref, ys_ref, o_ref, sem):
    tiles, sub, _ = o_ref.shape

    def issue(i, carry):
        for k in range(sub):
            pltpu.make_async_copy(ys_ref.at[pl.ds(dest_ref[i * sub + k], 1)],
                                  o_ref.at[i, pl.ds(k, 1)], sem).start()
        return carry

    lax.fori_loop(0, tiles, issue, 0)
    _wait_rows(ys_ref, ys_ref, sem, tiles * sub)


def _unsort(ys, dest, t):
    d = ys.shape[1]
    tc = min(COPY_TILE, t)
    return pl.pallas_call(
        _unsort_kernel,
        grid=(t // tc,),
        in_specs=[
            pl.BlockSpec((tc,), lambda i: (i,), memory_space=pltpu.SMEM),
            pl.BlockSpec(memory_space=pl.ANY),
        ],
        out_specs=pl.BlockSpec((tc // SUBLANES, SUBLANES, d), lambda i: (i, 0, 0)),
        out_shape=jax.ShapeDtypeStruct((t // SUBLANES, SUBLANES, d), jnp.float32),
        scratch_shapes=[pltpu.SemaphoreType.DMA],
        compiler_params=pltpu.CompilerParams(dimension_semantics=("arbitrary",)),
        name="unsort",
    )(dest, ys).reshape(t, d)


def _expert_kernel(grp_ref, ea_ref, eb_ref, nact_ref,
                   xs_ref, gf_ref, wr_ref, br_ref, gfin_ref, wg_hbm, wu_hbm, wd_hbm,
                   o_ref, wgb, wub, wdb, sg, su, sd, sem, cnt_ref):
    s = pl.program_id(0)
    blocks = tuple(BLOCKS_PER_STEP * s + i for i in range(BLOCKS_PER_STEP))
    g = grp_ref[blocks[0]]

    def stage(e):
        slot = e % 2
        return (pltpu.make_async_copy(wg_hbm.at[e], sg.at[slot], sem.at[slot, 0]),
                pltpu.make_async_copy(wu_hbm.at[e], su.at[slot], sem.at[slot, 1]),
                pltpu.make_async_copy(wd_hbm.at[e], sd.at[slot], sem.at[slot, 2]))

    def start_next():
        @pl.when(cnt_ref[0] < N_EXPERTS)
        def _():
            for cp in stage(cnt_ref[0]):
                cp.start()
            cnt_ref[0] = cnt_ref[0] + 1

    @pl.when(s == 0)
    def _():
        cnt_ref[0] = 0
        cnt_ref[1] = 0
        start_next()
        start_next()

    active = blocks[0] < nact_ref[0]

    @pl.when(active)
    def _():
        need = g * EXPERTS_PER_GROUP + functools.reduce(
            jnp.maximum, [eb_ref[j] for j in blocks])

        def load(e, carry):
            for cp in stage(e):
                cp.wait()
            slot = e % 2
            k = e % EXPERTS_PER_GROUP
            wgb[k] = sg[slot].astype(jnp.bfloat16)
            wub[k] = su[slot].astype(jnp.bfloat16)
            wdb[k] = sd[slot].astype(jnp.bfloat16)
            cnt_ref[1] = e + 1
            start_next()
            return carry

        lax.fori_loop(cnt_ref[1], need + 1, load, 0)

        lane = lax.broadcasted_iota(jnp.int32, (ROW_BLOCK, ROUTE_LANES), 1)
        is_g = lane < N_GROUPS
        lo = EXPERT_LANE0 + EXPERTS_PER_GROUP * g
        nblk = len(blocks)
        xs = [xs_ref[i * ROW_BLOCK:(i + 1) * ROW_BLOCK, :] for i in range(nblk)]
        h3 = [_rms(x, gf_ref[...]).astype(jnp.bfloat16) for x in xs]
        lg = [jnp.dot(h, wr_ref[...], preferred_element_type=jnp.float32) + br_ref[...]
              for h in h3]

        def gates(lgi, j):
            def pick(idx):
                return jnp.sum(jnp.where(lane == idx, lgi, 0.0), axis=-1, keepdims=True)

            gmax = jnp.max(jnp.where(is_g, lgi, _NEG), axis=-1, keepdims=True)
            den = jnp.sum(jnp.where(is_g, jnp.exp(jnp.where(is_g, lgi, _NEG) - gmax), 0.0),
                          axis=-1, keepdims=True)
            grp_p = jnp.exp(pick(g) - gmax) / den
            la = pick(lo + ea_ref[j])
            lb = pick(lo + eb_ref[j])
            m = jnp.maximum(la, lb)
            pa = jnp.exp(la - m)
            pb = jnp.exp(lb - m)
            return grp_p * pa / (pa + pb), grp_p * pb / (pa + pb)

        gate = [gates(lg[i], blocks[i]) for i in range(nblk)]
        ys = list(xs)
        for which, e_ref in enumerate((ea_ref, eb_ref)):
            ks = [e_ref[j] for j in blocks]
            gg = [jnp.dot(h3[i], wgb[ks[i]], preferred_element_type=jnp.float32)
                  for i in range(nblk)]
            uu = [jnp.dot(h3[i], wub[ks[i]], preferred_element_type=jnp.float32)
                  for i in range(nblk)]
            act = [(gg[i] * (1.0 / (1.0 + jnp.exp(-gg[i]))) * uu[i]
                    * gate[i][which]).astype(jnp.bfloat16) for i in range(nblk)]
            ys = [ys[i] + jnp.dot(act[i], wdb[ks[i]], preferred_element_type=jnp.float32)
                  for i in range(nblk)]
        for i in range(nblk):
            o_ref[i * ROW_BLOCK:(i + 1) * ROW_BLOCK, :] = _rms(ys[i], gfin_ref[...])

    @pl.when(jnp.logical_not(active))
    def _():
        o_ref[...] = jnp.zeros_like(o_ref)

    @pl.when(s == pl.num_programs(0) - 1)
    def _():
        def drain(e, carry):
            for cp in stage(e):
                cp.wait()
            return carry

        lax.fori_loop(cnt_ref[1], cnt_ref[0], drain, 0)


def _experts(xs, blk_grp, blk_a, blk_b, nact, g_ffn, w_r, b_r, g_final, w_gate, w_up, w_down):
    cap, d = xs.shape
    de = w_gate.shape[2]
    step_rows = BLOCKS_PER_STEP * ROW_BLOCK
    steps = cap // step_rows
    pre = lambda f: (lambda s, gr, ea, eb, na: f(s, na))
    const2 = pre(lambda s, na: (0, 0))
    last_step = lambda na: (na[0] - 1) // BLOCKS_PER_STEP
    hbm = pl.BlockSpec(memory_space=pl.ANY)
    grid_spec = pltpu.PrefetchScalarGridSpec(
        num_scalar_prefetch=4,
        grid=(steps,),
        in_specs=[
            pl.BlockSpec((step_rows, d), pre(lambda s, na: (jnp.minimum(s, last_step(na)), 0))),
            pl.BlockSpec(g_ffn.shape, const2),
            pl.BlockSpec(w_r.shape, const2),
            pl.BlockSpec(b_r.shape, const2),
            pl.BlockSpec(g_final.shape, const2),
            hbm, hbm, hbm,
        ],
        out_specs=pl.BlockSpec((step_rows, d), pre(lambda s, na: (s, 0))),
        scratch_shapes=[
            pltpu.VMEM((EXPERTS_PER_GROUP, d, de), jnp.bfloat16),
            pltpu.VMEM((EXPERTS_PER_GROUP, d, de), jnp.bfloat16),
            pltpu.VMEM((EXPERTS_PER_GROUP, de, d), jnp.bfloat16),
            pltpu.VMEM((2, d, de), jnp.float32),
            pltpu.VMEM((2, d, de), jnp.float32),
            pltpu.VMEM((2, de, d), jnp.float32),
            pltpu.SemaphoreType.DMA((2, 3)),
            pltpu.SMEM((2,), jnp.int32),
        ],
    )
    return pl.pallas_call(
        _expert_kernel,
        grid_spec=grid_spec,
        out_shape=jax.ShapeDtypeStruct((cap, d), jnp.float32),
        compiler_params=pltpu.CompilerParams(
            dimension_semantics=("arbitrary",), vmem_limit_bytes=VMEM_LIMIT_BYTES),
        name="experts",
    )(blk_grp, blk_a, blk_b, nact, xs, g_ffn, w_r, b_r, g_final, w_gate, w_up, w_down)


def _moe_final(x2, slab, counts_col, g_ffn, w_r, b_r, g_final, w_gate, w_up, w_down):
    t, d = x2.shape
    nb = t // ROW_BLOCK + N_CLASSES + N_GROUPS * (BLOCKS_PER_STEP - 1)
    assert nb % BLOCKS_PER_STEP == 0
    counts = counts_col[:N_CLASSES, 0].astype(jnp.int32)
    nblk = (counts + ROW_BLOCK - 1) // ROW_BLOCK
    grp_blocks = jnp.sum(nblk.reshape(N_GROUPS, N_PAIRS), axis=1)
    nblk = nblk.reshape(N_GROUPS, N_PAIRS).at[:, N_PAIRS - 1].add(
        (-grp_blocks) % BLOCKS_PER_STEP).reshape(N_CLASSES)
    blk_end = jnp.cumsum(nblk)
    blk_start = blk_end - nblk
    nact = blk_end[-1]
    j = jnp.arange(nb, dtype=jnp.int32)
    blk_class = jnp.minimum(
        jnp.sum((blk_end[None, :] <= j[:, None]).astype(jnp.int32), axis=1), N_CLASSES - 1)
    blk_class = jnp.where(j < nact, blk_class, blk_class[jnp.maximum(nact - 1, 0)])
    not_full = (j - blk_start[blk_class] + 1) * ROW_BLOCK > counts[blk_class]
    zero_block = (not_full | (j >= nact)).astype(jnp.int32)
    pair_a = jnp.array([p[0] for p in PAIRS], jnp.int32)
    pair_b = jnp.array([p[1] for p in PAIRS], jnp.int32)
    blk_grp = blk_class // N_PAIRS
    blk_a = pair_a[blk_class % N_PAIRS]
    blk_b = pair_b[blk_class % N_PAIRS]
    starts_col = jnp.zeros((ROUTE_LANES, 1), jnp.float32).at[:N_CLASSES, 0].set(
        (blk_start * ROW_BLOCK).astype(jnp.float32))

    dest = _dest_rows(slab, starts_col)[0]
    xs = _dispatch(x2, dest, zero_block)
    ys = _experts(xs, blk_grp, blk_a, blk_b, nact[None], g_ffn, w_r, b_r, g_final,
                  w_gate, w_up, w_down)
    return _unsort(ys, dest, t)


def kernel(x, mem, g_mix, w_in, conv_w, g_v, w_s, b_s, g_out_conv, g_out_gmlp, w_out, g_xattn,
           g_mem, w_q, w_k, w_v, w_o, g_ffn, w_grp, b_grp, w_rt, b_rt, w_gate, w_up, w_down,
           g_final):
    b, s, d = x.shape
    assert g_mix.shape[0] == 1, "the final norm is fused into the single layer's expert kernel"
    assert N_CLASSES <= ROUTE_LANES
    bf = jnp.bfloat16
    kt, v = _kv_proj(mem, g_mem[0][None], w_k[0].astype(bf), w_v[0].astype(bf))

    bias = jnp.repeat(b_s[0].T, GMLP_HEAD_DIM, axis=1)
    x1 = _mixer(x, g_mix[0][None], w_in[0].astype(bf), conv_w[0], g_v[0][None], w_s[0], bias,
                g_out_conv[0][None], g_out_gmlp[0][None], w_out[0].astype(bf))

    pad = ROUTE_LANES - N_GROUPS - N_EXPERTS
    w_r = jnp.concatenate([w_grp[0], w_rt[0], jnp.zeros((d, pad), jnp.float32)], axis=1).astype(bf)
    b_r = jnp.concatenate([b_grp[0], b_rt[0], jnp.zeros((pad,), jnp.float32)])[None]
    gap = ROUTE_EXPERT_ROW0 - N_GROUPS
    tail = ROUTE_LANES - ROUTE_EXPERT_ROW0 - N_EXPERTS
    w_r_t = jnp.concatenate([w_grp[0].T, jnp.zeros((gap, d), jnp.float32), w_rt[0].T,
                             jnp.zeros((tail, d), jnp.float32)], axis=0).astype(bf)
    b_r_t = jnp.concatenate([b_grp[0], jnp.zeros((gap,), jnp.float32), b_rt[0],
                             jnp.zeros((tail,), jnp.float32)])[:, None]
    x2, slab, counts_col = _attn_route(x1, kt, v, g_xattn[0][None], w_q[0].astype(bf),
                                       w_o[0].astype(bf), g_ffn[0][None], w_r_t, b_r_t)
    out = _moe_final(x2.reshape(b * s, d), slab, counts_col, g_ffn[0][None], w_r, b_r,
                     g_final[None], w_gate[0], w_up[0], w_down[0])
    return out.reshape(b, s, d)
```

```python
import functools

import jax
import jax.numpy as jnp
from jax import lax
from jax.experimental import pallas as pl
from jax.experimental.pallas import tpu as pltpu

EPS = 1e-6
CONV_GROUP_WIDTH = 512
GMLP_HEADS = 8
GMLP_HEAD_DIM = 64
CHUNK = 128
CONV_K = 3
XA_HEADS = 4
N_GROUPS = 4
EXPERTS_PER_GROUP = 8
N_EXPERTS = N_GROUPS * EXPERTS_PER_GROUP
TOP_K = 2
ROUTE_LANES = 128
EXPERT_LANE0 = N_GROUPS
ROUTE_EXPERT_ROW0 = 8

N_PAIRS = EXPERTS_PER_GROUP * (EXPERTS_PER_GROUP - 1) // 2
N_CLASSES = N_GROUPS * N_PAIRS
PAIRS = [(a, b) for a in range(EXPERTS_PER_GROUP) for b in range(a + 1, EXPERTS_PER_GROUP)]

TOKEN_TILE = 1024
SUB_TILE = 256
ROW_BLOCK = 128
BLOCKS_PER_STEP = 4
COPY_TILE = 2048
VMEM_LIMIT_BYTES = 56 * 1024 * 1024

SUBLANES = 8

_NEG = -1e30


def _rms(x, g):
    return x * lax.rsqrt(jnp.mean(x * x, axis=-1, keepdims=True) + EPS) * g


def _gelu_tanh(x):
    return 0.5 * x * (1.0 + jnp.tanh(0.7978845608028654 * (x + 0.044715 * (x * x * x))))


def _kv_kernel(mem_ref, g_ref, wk_ref, wv_ref, kt_ref, v_ref):
    m = _rms(mem_ref[0], g_ref[...]).astype(jnp.bfloat16)
    k = jnp.dot(m, wk_ref[...], preferred_element_type=jnp.float32)
    v = jnp.dot(m, wv_ref[...], preferred_element_type=jnp.float32)
    kt_ref[0] = k.T.astype(jnp.bfloat16)
    v_ref[0] = v.astype(jnp.bfloat16)


def _kv_proj(mem, g_mem, w_k, w_v):
    b, m, d = mem.shape
    const = lambda i: (0, 0)
    return pl.pallas_call(
        _kv_kernel,
        grid=(b,),
        in_specs=[
            pl.BlockSpec((1, m, d), lambda i: (i, 0, 0)),
            pl.BlockSpec((1, d), const),
            pl.BlockSpec((d, d), const),
            pl.BlockSpec((d, d), const),
        ],
        out_specs=[
            pl.BlockSpec((1, d, m), lambda i: (i, 0, 0)),
            pl.BlockSpec((1, m, d), lambda i: (i, 0, 0)),
        ],
        out_shape=[
            jax.ShapeDtypeStruct((b, d, m), jnp.bfloat16),
            jax.ShapeDtypeStruct((b, m, d), jnp.bfloat16),
        ],
        compiler_params=pltpu.CompilerParams(
            dimension_semantics=("arbitrary",), vmem_limit_bytes=VMEM_LIMIT_BYTES),
        name="kv_proj",
    )(mem, g_mem, w_k, w_v)


def _mixer_kernel(x_ref, gmix_ref, win_ref, convw_ref, gv_ref, ws_ref, bias_ref,
                  goc_ref, gog_ref, wout_ref, o_ref, zbuf_ref):
    ts = x_ref.shape[1]
    w = CONV_GROUP_WIDTH
    hw = 2 * GMLP_HEAD_DIM
    nsub = ts // SUB_TILE

    def dot(a, b):
        return jnp.dot(a, b, preferred_element_type=jnp.float32)

    @pl.when(pl.program_id(1) == 0)
    def _():
        zbuf_ref[0:8, :] = jnp.zeros((8, w), jnp.float32)

    @pl.when(pl.program_id(1) != 0)
    def _():
        zbuf_ref[0:8, :] = zbuf_ref[ts:ts + 8, :]

    low = lax.broadcasted_iota(jnp.int32, (SUB_TILE, hw), 1) < GMLP_HEAD_DIM
    row = lax.broadcasted_iota(jnp.int32, (CHUNK, CHUNK), 0)
    colid = lax.broadcasted_iota(jnp.int32, (CHUNK, CHUNK), 1)
    causal = row >= colid
    ws = [jnp.where(causal, ws_ref[hd], 0.0).astype(jnp.bfloat16) for hd in range(GMLP_HEADS)]
    npair = SUB_TILE // CHUNK // 2
    lo = lax.broadcasted_iota(jnp.int32, (CHUNK, hw), 1) < GMLP_HEAD_DIM
    swap = lambda a: pltpu.roll(a, GMLP_HEAD_DIM, axis=1)

    def project(i):
        r0 = i * SUB_TILE
        xt = x_ref[0, r0:r0 + SUB_TILE, :]
        h = _rms(xt, gmix_ref[...]).astype(jnp.bfloat16)
        p = [dot(h, win_ref[:, k * w:(k + 1) * w]) for k in range(5)]
        z = p[1] * p[2]
        zbuf_ref[8 + r0:8 + r0 + SUB_TILE, :] = z
        return dict(xt=xt, gate_b=p[0], z=z, u=p[3], v=p[4], r0=r0)

    def gate_and_norm(st):
        r0 = st["r0"]
        zc = (convw_ref[0:1, :] * zbuf_ref[6 + r0:6 + r0 + SUB_TILE, :]
              + convw_ref[1:2, :] * zbuf_ref[7 + r0:7 + r0 + SUB_TILE, :]
              + convw_ref[2:3, :] * st["z"])
        st["ya"] = _rms(st["gate_b"] * zc, goc_ref[...]).astype(jnp.bfloat16)
        st["u"] = _gelu_tanh(st["u"])
        v = _gelu_tanh(st["v"])
        v2 = v * v
        ss_cols = []
        for k in range(GMLP_HEADS // 2):
            col = v2[:, hw * k:hw * (k + 1)]
            ss_cols.append(jnp.where(low,
                                     jnp.sum(jnp.where(low, col, 0.0), axis=1, keepdims=True),
                                     jnp.sum(jnp.where(low, 0.0, col), axis=1, keepdims=True)))
        ss = jnp.concatenate(ss_cols, axis=1)
        st["vn"] = v * lax.rsqrt(ss * (1.0 / GMLP_HEAD_DIM) + EPS) * gv_ref[...]

    def mix_positions(st):
        vn = st["vn"]
        s_cols = [[None] * (GMLP_HEADS // 2) for _ in range(2 * npair)]
        for hp in range(GMLP_HEADS // 2):
            cols = [vn[c * CHUNK:(c + 1) * CHUNK, hw * hp:hw * (hp + 1)] for c in range(2 * npair)]
            swapped = [swap(a) for a in cols]
            rhs_even = jnp.concatenate(
                [jnp.where(lo, cols[2 * p], swapped[2 * p + 1]) for p in range(npair)], axis=1)
            rhs_odd = jnp.concatenate(
                [jnp.where(lo, swapped[2 * p], cols[2 * p + 1]) for p in range(npair)], axis=1)
            out_e = dot(ws[2 * hp], rhs_even.astype(jnp.bfloat16))
            out_o = dot(ws[2 * hp + 1], rhs_odd.astype(jnp.bfloat16))
            for p in range(npair):
                e = out_e[:, hw * p:hw * (p + 1)]
                o = out_o[:, hw * p:hw * (p + 1)]
                s_cols[2 * p][hp] = jnp.where(lo, e, swap(o))
                s_cols[2 * p + 1][hp] = jnp.where(lo, swap(e), o)
        st["s"] = jnp.concatenate(
            [jnp.concatenate(c, axis=1) + bias_ref[...] for c in s_cols], axis=0)

    def output(st):
        r0 = st["r0"]
        yb = _rms(st["u"] * st["s"], gog_ref[...]).astype(jnp.bfloat16)
        o_ref[0, r0:r0 + SUB_TILE, :] = (st["xt"] + dot(st["ya"], wout_ref[0:w, :])
                                          + dot(yb, wout_ref[w:2 * w, :]))

    phases = (gate_and_norm, mix_positions, output)
    states = []
    for step in range(nsub + len(phases)):
        if step < nsub:
            states.append(project(step))
        for k, phase in enumerate(phases):
            i = step - 1 - k
            if 0 <= i < nsub:
                phase(states[i])


def _mixer(x, g_mix, w_in, conv_w, g_v, w_s, bias, g_oc, g_og, w_out):
    b, s, d = x.shape
    ts = min(TOKEN_TILE, s)
    const2 = lambda i, j: (0, 0)
    const3 = lambda i, j: (0, 0, 0)
    return pl.pallas_call(
        _mixer_kernel,
        grid=(b, s // ts),
        in_specs=[
            pl.BlockSpec((1, ts, d), lambda i, j: (i, j, 0)),
            pl.BlockSpec(g_mix.shape, const2),
            pl.BlockSpec(w_in.shape, const2),
            pl.BlockSpec(conv_w.shape, const2),
            pl.BlockSpec(g_v.shape, const2),
            pl.BlockSpec(w_s.shape, const3),
            pl.BlockSpec(bias.shape, const2),
            pl.BlockSpec(g_oc.shape, const2),
            pl.BlockSpec(g_og.shape, const2),
            pl.BlockSpec(w_out.shape, const2),
        ],
        out_specs=pl.BlockSpec((1, ts, d), lambda i, j: (i, j, 0)),
        out_shape=jax.ShapeDtypeStruct((b, s, d), jnp.float32),
        scratch_shapes=[pltpu.VMEM((ts + 8, CONV_GROUP_WIDTH), jnp.float32)],
        compiler_params=pltpu.CompilerParams(
            dimension_semantics=("arbitrary", "arbitrary"), vmem_limit_bytes=VMEM_LIMIT_BYTES),
        name="mixer",
    )(x, g_mix, w_in, conv_w, g_v, w_s, bias, g_oc, g_og, w_out)


def _attn_route_kernel(x_ref, kt_ref, v_ref, gx_ref, wq_ref, wo_ref, gf_ref, wrt_ref, brt_ref,
                       x2_ref, slab_ref, counts_ref, carry_ref):
    ts = x_ref.shape[1]
    d = x_ref.shape[2]
    hd = d // XA_HEADS
    nsub = ts // SUB_TILE
    first = (pl.program_id(0) == 0) & (pl.program_id(1) == 0)

    @pl.when(first)
    def _():
        carry_ref[...] = jnp.zeros_like(carry_ref)

    def dot(a, b):
        return jnp.dot(a, b, preferred_element_type=jnp.float32)

    subs = range(nsub)
    x1 = [x_ref[0, i * SUB_TILE:(i + 1) * SUB_TILE, :] for i in subs]
    h2 = [_rms(x, gx_ref[...]).astype(jnp.bfloat16) for x in x1]
    q = [dot(h, wq_ref[...]).astype(jnp.bfloat16) for h in h2]
    heads = [[] for _ in subs]
    for a in range(XA_HEADS):
        sc = [dot(q[i][:, a * hd:(a + 1) * hd], kt_ref[0, a * hd:(a + 1) * hd, :]) * (hd ** -0.5)
              for i in subs]
        p = [jnp.exp(s_ - jnp.max(s_, axis=-1, keepdims=True)) for s_ in sc]
        l = [jnp.sum(p_, axis=-1, keepdims=True) for p_ in p]
        o = [dot(p[i].astype(jnp.bfloat16), v_ref[0, :, a * hd:(a + 1) * hd]) for i in subs]
        for i in subs:
            heads[i].append((o[i] / l[i]).astype(jnp.bfloat16))
    x2 = [x1[i] + dot(jnp.concatenate(heads[i], axis=1), wo_ref[...]) for i in subs]
    for i in subs:
        x2_ref[0, i * SUB_TILE:(i + 1) * SUB_TILE, :] = x2[i]

    h3 = [_rms(x, gf_ref[...]).astype(jnp.bfloat16) for x in x2]
    lgs = [lax.dot_general(wrt_ref[...], h, (((1,), (1,)), ((), ())),
                           preferred_element_type=jnp.float32) + brt_ref[...] for h in h3]
    sub = lax.broadcasted_iota(jnp.int32, (EXPERTS_PER_GROUP, SUB_TILE), 0).astype(jnp.float32)
    big = float(EXPERTS_PER_GROUP)
    is_g = sub < N_GROUPS

    def classify(lg):
        glog = lg[0:EXPERTS_PER_GROUP, :]
        gmax = jnp.max(jnp.where(is_g, glog, _NEG), axis=0, keepdims=True)
        gidx = jnp.min(jnp.where(is_g & (glog == gmax), sub, big), axis=0, keepdims=True)
        el = lg[ROUTE_EXPERT_ROW0:ROUTE_EXPERT_ROW0 + EXPERTS_PER_GROUP, :]
        for grp in range(1, N_GROUPS):
            r0 = ROUTE_EXPERT_ROW0 + grp * EXPERTS_PER_GROUP
            el = jnp.where(gidx == grp, lg[r0:r0 + EXPERTS_PER_GROUP, :], el)
        t1 = jnp.max(el, axis=0, keepdims=True)
        i1 = jnp.min(jnp.where(el == t1, sub, big), axis=0, keepdims=True)
        rest = sub != i1
        t2 = jnp.max(jnp.where(rest, el, _NEG), axis=0, keepdims=True)
        i2 = jnp.min(jnp.where(rest & (el == t2), sub, big), axis=0, keepdims=True)
        a = jnp.minimum(i1, i2)
        b = jnp.maximum(i1, i2)
        pair = a * (2 * EXPERTS_PER_GROUP - 1 - a) * 0.5 + (b - a - 1.0)
        return gidx * N_PAIRS + pair

    cls = [classify(lg) for lg in lgs]

    r = lax.broadcasted_iota(jnp.int32, (SUB_TILE, SUB_TILE), 0)
    c = lax.broadcasted_iota(jnp.int32, (SUB_TILE, SUB_TILE), 1)
    earlier = jnp.where(r < c, 1.0, 0.0).astype(jnp.bfloat16)
    crow = lax.broadcasted_iota(jnp.int32, (ROUTE_LANES, SUB_TILE), 0).astype(jnp.float32)
    oh = [crow == cl for cl in cls]
    ohf = [jnp.where(o_, 1.0, 0.0) for o_ in oh]
    inside = [dot(o_.astype(jnp.bfloat16), earlier) for o_ in ohf]
    seen = carry_ref[...]
    for i in subs:
        rank = jnp.sum(jnp.where(oh[i], inside[i] + seen, 0.0), axis=0, keepdims=True)
        slab_ref[:, i * SUB_TILE:(i + 1) * SUB_TILE] = jnp.where(
            sub == 0, cls[i], jnp.where(sub == 1, rank, 0.0))
        seen = seen + jnp.sum(ohf[i], axis=1, keepdims=True)
    carry_ref[...] = seen
    counts_ref[...] = seen


def _attn_route(x1, kt, v, g_x, w_q, w_o, g_f, w_rt, b_rt):
    b, s, d = x1.shape
    m = v.shape[1]
    ts = min(TOKEN_TILE, s)
    nt = s // ts
    const2 = lambda i, j: (0, 0)
    return pl.pallas_call(
        _attn_route_kernel,
        grid=(b, nt),
        in_specs=[
            pl.BlockSpec((1, ts, d), lambda i, j: (i, j, 0)),
            pl.BlockSpec((1, d, m), lambda i, j: (i, 0, 0)),
            pl.BlockSpec((1, m, d), lambda i, j: (i, 0, 0)),
            pl.BlockSpec(g_x.shape, const2),
            pl.BlockSpec(w_q.shape, const2),
            pl.BlockSpec(w_o.shape, const2),
            pl.BlockSpec(g_f.shape, const2),
            pl.BlockSpec(w_rt.shape, const2),
            pl.BlockSpec(b_rt.shape, const2),
        ],
        out_specs=[
            pl.BlockSpec((1, ts, d), lambda i, j: (i, j, 0)),
            pl.BlockSpec((SUBLANES, ts), lambda i, j: (0, i * nt + j)),
            pl.BlockSpec((ROUTE_LANES, 1), const2),
        ],
        out_shape=[
            jax.ShapeDtypeStruct((b, s, d), jnp.float32),
            jax.ShapeDtypeStruct((SUBLANES, b * s), jnp.float32),
            jax.ShapeDtypeStruct((ROUTE_LANES, 1), jnp.float32),
        ],
        scratch_shapes=[pltpu.VMEM((ROUTE_LANES, 1), jnp.float32)],
        compiler_params=pltpu.CompilerParams(
            dimension_semantics=("arbitrary", "arbitrary"), vmem_limit_bytes=VMEM_LIMIT_BYTES),
        name="attn_route",
    )(x1, kt, v, g_x, w_q, w_o, g_f, w_rt, b_rt)


def _dest_kernel(slab_ref, starts_ref, o_ref):
    slab = slab_ref[...]
    tt = slab.shape[1]
    crow = lax.broadcasted_iota(jnp.int32, (ROUTE_LANES, tt), 0).astype(jnp.float32)
    start = jnp.sum(jnp.where(crow == slab[0:1, :], starts_ref[...], 0.0), axis=0, keepdims=True)
    sub = lax.broadcasted_iota(jnp.int32, slab.shape, 0)
    o_ref[...] = jnp.where(sub == 0, start + slab[1:2, :], 0.0).astype(jnp.int32)


def _dest_rows(slab, starts_col):
    t = slab.shape[1]
    tt = min(2048, t)
    return pl.pallas_call(
        _dest_kernel,
        grid=(t // tt,),
        in_specs=[pl.BlockSpec((SUBLANES, tt), lambda i: (0, i)),
                  pl.BlockSpec((ROUTE_LANES, 1), lambda i: (0, 0))],
        out_specs=pl.BlockSpec((SUBLANES, tt), lambda i: (0, i)),
        out_shape=jax.ShapeDtypeStruct((SUBLANES, t), jnp.int32),
        compiler_params=pltpu.CompilerParams(dimension_semantics=("arbitrary",)),
        name="dest_rows",
    )(slab, starts_col)


def _wait_rows(src_ref, dst_ref, sem, n):
    pltpu.make_async_copy(src_ref.at[pl.ds(0, n)], dst_ref.at[pl.ds(0, n)], sem).wait()


def _dispatch_kernel(zero_block_ref, dest_ref, x_ref, xs_ref, zero_ref, sem, zsem):
    tiles, sub, _ = x_ref.shape
    nb = zero_block_ref.shape[0]

    def block_copy(j):
        start = pl.multiple_of(j * ROW_BLOCK, ROW_BLOCK)
        return pltpu.make_async_copy(zero_ref, xs_ref.at[pl.ds(start, ROW_BLOCK)], zsem)

    @pl.when(pl.program_id(0) == 0)
    def _():
        zero_ref[...] = jnp.zeros_like(zero_ref)

        @pl.loop(0, nb)
        def _(j):
            @pl.when(zero_block_ref[j] > 0)
            def _():
                block_copy(j).start()

        @pl.loop(0, nb)
        def _(j):
            @pl.when(zero_block_ref[j] > 0)
            def _():
                block_copy(j).wait()

    def issue(i, carry):
        for k in range(sub):
            pltpu.make_async_copy(x_ref.at[i, pl.ds(k, 1)],
                                  xs_ref.at[pl.ds(dest_ref[i * sub + k], 1)], sem).start()
        return carry

    lax.fori_loop(0, tiles, issue, 0)
    _wait_rows(xs_ref, xs_ref, sem, tiles * sub)


def _dispatch(x2, dest, zero_block):
    t, d = x2.shape
    tc = min(COPY_TILE, t)
    cap = zero_block.shape[0] * ROW_BLOCK
    grid_spec = pltpu.PrefetchScalarGridSpec(
        num_scalar_prefetch=1,
        grid=(t // tc,),
        in_specs=[
            pl.BlockSpec((tc,), lambda i, zb: (i,), memory_space=pltpu.SMEM),
            pl.BlockSpec((tc // SUBLANES, SUBLANES, d), lambda i, zb: (i, 0, 0)),
        ],
        out_specs=pl.BlockSpec(memory_space=pl.ANY),
        scratch_shapes=[pltpu.VMEM((ROW_BLOCK, d), jnp.float32),
                        pltpu.SemaphoreType.DMA, pltpu.SemaphoreType.DMA],
    )
    return pl.pallas_call(
        _dispatch_kernel,
        grid_spec=grid_spec,
        out_shape=jax.ShapeDtypeStruct((cap, d), jnp.float32),
        compiler_params=pltpu.CompilerParams(
            dimension_semantics=("arbitrary",), has_side_effects=True),
        name="dispatch",
    )(zero_block, dest, x2.reshape(t // SUBLANES, SUBLANES, d))


def _unsort_kernel(dest_ref, ys_ref, o_ref, sem):
    tiles, sub, _ = o_ref.shape

    def issue(i, carry):
        for k in range(sub):
            pltpu.make_async_copy(ys_ref.at[pl.ds(dest_ref[i * sub + k], 1)],
                                  o_ref.at[i, pl.ds(k, 1)], sem).start()
        return carry

    lax.fori_loop(0, tiles, issue, 0)
    _wait_rows(ys_ref, ys_ref, sem, tiles * sub)


def _unsort(ys, dest, t):
    d = ys.shape[1]
    tc = min(COPY_TILE, t)
    return pl.pallas_call(
        _unsort_kernel,
        grid=(t // tc,),
        in_specs=[
            pl.BlockSpec((tc,), lambda i: (i,), memory_space=pltpu.SMEM),
            pl.BlockSpec(memory_space=pl.ANY),
        ],
        out_specs=pl.BlockSpec((tc // SUBLANES, SUBLANES, d), lambda i: (i, 0, 0)),
        out_shape=jax.ShapeDtypeStruct((t // SUBLANES, SUBLANES, d), jnp.float32),
        scratch_shapes=[pltpu.SemaphoreType.DMA],
        compiler_params=pltpu.CompilerParams(dimension_semantics=("arbitrary",)),
        name="unsort",
    )(dest, ys).reshape(t, d)


def _expert_kernel(grp_ref, ea_ref, eb_ref, nact_ref,
                   xs_ref, gf_ref, wr_ref, br_ref, gfin_ref, wg_hbm, wu_hbm, wd_hbm,
                   o_ref, wgb, wub, wdb, sg, su, sd, sem, cnt_ref):
    s = pl.program_id(0)
    blocks = tuple(BLOCKS_PER_STEP * s + i for i in range(BLOCKS_PER_STEP))
    g = grp_ref[blocks[0]]

    def stage(e):
        slot = e % 2
        return (pltpu.make_async_copy(wg_hbm.at[e], sg.at[slot], sem.at[slot, 0]),
                pltpu.make_async_copy(wu_hbm.at[e], su.at[slot], sem.at[slot, 1]),
                pltpu.make_async_copy(wd_hbm.at[e], sd.at[slot], sem.at[slot, 2]))

    def start_next():
        @pl.when(cnt_ref[0] < N_EXPERTS)
        def _():
            for cp in stage(cnt_ref[0]):
                cp.start()
            cnt_ref[0] = cnt_ref[0] + 1

    @pl.when(s == 0)
    def _():
        cnt_ref[0] = 0
        cnt_ref[1] = 0
        start_next()
        start_next()

    active = blocks[0] < nact_ref[0]

    @pl.when(active)
    def _():
        need = g * EXPERTS_PER_GROUP + functools.reduce(
            jnp.maximum, [eb_ref[j] for j in blocks])

        def load(e, carry):
            for cp in stage(e):
                cp.wait()
            slot = e % 2
            k = e % EXPERTS_PER_GROUP
            wgb[k] = sg[slot].astype(jnp.bfloat16)
            wub[k] = su[slot].astype(jnp.bfloat16)
            wdb[k] = sd[slot].astype(jnp.bfloat16)
            cnt_ref[1] = e + 1
            start_next()
            return carry

        lax.fori_loop(cnt_ref[1], need + 1, load, 0)

        lane = lax.broadcasted_iota(jnp.int32, (ROW_BLOCK, ROUTE_LANES), 1)
        is_g = lane < N_GROUPS
        lo = EXPERT_LANE0 + EXPERTS_PER_GROUP * g
        nblk = len(blocks)
        xs = [xs_ref[i * ROW_BLOCK:(i + 1) * ROW_BLOCK, :] for i in range(nblk)]
        h3 = [_rms(x, gf_ref[...]).astype(jnp.bfloat16) for x in xs]
        lg = [jnp.dot(h, wr_ref[...], preferred_element_type=jnp.float32) + br_ref[...]
              for h in h3]

        def gates(lgi, j):
            def pick(idx):
                return jnp.sum(jnp.where(lane == idx, lgi, 0.0), axis=-1, keepdims=True)

            gmax = jnp.max(jnp.where(is_g, lgi, _NEG), axis=-1, keepdims=True)
            den = jnp.sum(jnp.where(is_g, jnp.exp(jnp.where(is_g, lgi, _NEG) - gmax), 0.0),
                          axis=-1, keepdims=True)
            grp_p = jnp.exp(pick(g) - gmax) / den
            la = pick(lo + ea_ref[j])
            lb = pick(lo + eb_ref[j])
            m = jnp.maximum(la, lb)
            pa = jnp.exp(la - m)
            pb = jnp.exp(lb - m)
            return grp_p * pa / (pa + pb), grp_p * pb / (pa + pb)

        gate = [gates(lg[i], blocks[i]) for i in range(nblk)]
        ys = list(xs)
        for which, e_ref in enumerate((ea_ref, eb_ref)):
            ks = [e_ref[j] for j in blocks]
            gg = [jnp.dot(h3[i], wgb[ks[i]], preferred_element_type=jnp.float32)
                  for i in range(nblk)]
            uu = [jnp.dot(h3[i], wub[ks[i]], preferred_element_type=jnp.float32)
                  for i in range(nblk)]
            act = [(gg[i] * (1.0 / (1.0 + jnp.exp(-gg[i]))) * uu[i]
                    * gate[i][which]).astype(jnp.bfloat16) for i in range(nblk)]
            ys = [ys[i] + jnp.dot(act[i], wdb[ks[i]], preferred_element_type=jnp.float32)
                  for i in range(nblk)]
        for i in range(nblk):
            o_ref[i * ROW_BLOCK:(i + 1) * ROW_BLOCK, :] = _rms(ys[i], gfin_ref[...])

    @pl.when(jnp.logical_not(active))
    def _():
        o_ref[...] = jnp.zeros_like(o_ref)

    @pl.when(s == pl.num_programs(0) - 1)
    def _():
        def drain(e, carry):
            for cp in stage(e):
                cp.wait()
            return carry

        lax.fori_loop(cnt_ref[1], cnt_ref[0], drain, 0)


def _experts(xs, blk_grp, blk_a, blk_b, nact, g_ffn, w_r, b_r, g_final, w_gate, w_up, w_down):
    cap, d = xs.shape
    de = w_gate.shape[2]
    step_rows = BLOCKS_PER_STEP * ROW_BLOCK
    steps = cap // step_rows
    pre = lambda f: (lambda s, gr, ea, eb, na: f(s, na))
    const2 = pre(lambda s, na: (0, 0))
    last_step = lambda na: (na[0] - 1) // BLOCKS_PER_STEP
    hbm = pl.BlockSpec(memory_space=pl.ANY)
    grid_spec = pltpu.PrefetchScalarGridSpec(
        num_scalar_prefetch=4,
        grid=(steps,),
        in_specs=[
            pl.BlockSpec((step_rows, d), pre(lambda s, na: (jnp.minimum(s, last_step(na)), 0))),
            pl.BlockSpec(g_ffn.shape, const2),
            pl.BlockSpec(w_r.shape, const2),
            pl.BlockSpec(b_r.shape, const2),
            pl.BlockSpec(g_final.shape, const2),
            hbm, hbm, hbm,
        ],
        out_specs=pl.BlockSpec((step_rows, d), pre(lambda s, na: (s, 0))),
        scratch_shapes=[
            pltpu.VMEM((EXPERTS_PER_GROUP, d, de), jnp.bfloat16),
            pltpu.VMEM((EXPERTS_PER_GROUP, d, de), jnp.bfloat16),
            pltpu.VMEM((EXPERTS_PER_GROUP, de, d), jnp.bfloat16),
            pltpu.VMEM((2, d, de), jnp.float32),
            pltpu.VMEM((2, d, de), jnp.float32),
            pltpu.VMEM((2, de, d), jnp.float32),
            pltpu.SemaphoreType.DMA((2, 3)),
            pltpu.SMEM((2,), jnp.int32),
        ],
    )
    return pl.pallas_call(
        _expert_kernel,
        grid_spec=grid_spec,
        out_shape=jax.ShapeDtypeStruct((cap, d), jnp.float32),
        compiler_params=pltpu.CompilerParams(
            dimension_semantics=("arbitrary",), vmem_limit_bytes=VMEM_LIMIT_BYTES),
        name="experts",
    )(blk_grp, blk_a, blk_b, nact, xs, g_ffn, w_r, b_r, g_final, w_gate, w_up, w_down)


def _moe_final(x2, slab, counts_col, g_ffn, w_r, b_r, g_final, w_gate, w_up, w_down):
    t, d = x2.shape
    nb = t // ROW_BLOCK + N_CLASSES + N_GROUPS * (BLOCKS_PER_STEP - 1)
    assert nb % BLOCKS_PER_STEP == 0
    counts = counts_col[:N_CLASSES, 0].astype(jnp.int32)
    nblk = (counts + ROW_BLOCK - 1) // ROW_BLOCK
    grp_blocks = jnp.sum(nblk.reshape(N_GROUPS, N_PAIRS), axis=1)
    nblk = nblk.reshape(N_GROUPS, N_PAIRS).at[:, N_PAIRS - 1].add(
        (-grp_blocks) % BLOCKS_PER_STEP).reshape(N_CLASSES)
    blk_end = jnp.cumsum(nblk)
    blk_start = blk_end - nblk
    nact = blk_end[-1]
    j = jnp.arange(nb, dtype=jnp.int32)
    blk_class = jnp.minimum(
        jnp.sum((blk_end[None, :] <= j[:, None]).astype(jnp.int32), axis=1), N_CLASSES - 1)
    blk_class = jnp.where(j < nact, blk_class, blk_class[jnp.maximum(nact - 1, 0)])
    not_full = (j - blk_start[blk_class] + 1) * ROW_BLOCK > counts[blk_class]
    zero_block = (not_full | (j >= nact)).astype(jnp.int32)
    pair_a = jnp.array([p[0] for p in PAIRS], jnp.int32)
    pair_b = jnp.array([p[1] for p in PAIRS], jnp.int32)
    blk_grp = blk_class // N_PAIRS
    blk_a = pair_a[blk_class % N_PAIRS]
    blk_b = pair_b[blk_class % N_PAIRS]
    starts_col = jnp.zeros((ROUTE_LANES, 1), jnp.float32).at[:N_CLASSES, 0].set(
        (blk_start * ROW_BLOCK).astype(jnp.float32))

    dest = _dest_rows(slab, starts_col)[0]
    xs = _dispatch(x2, dest, zero_block)
    ys = _experts(xs, blk_grp, blk_a, blk_b, nact[None], g_ffn, w_r, b_r, g_final,
                  w_gate, w_up, w_down)
    return _unsort(ys, dest, t)


def kernel(x, mem, g_mix, w_in, conv_w, g_v, w_s, b_s, g_out_conv, g_out_gmlp, w_out, g_xattn,
           g_mem, w_q, w_k, w_v, w_o, g_ffn, w_grp, b_grp, w_rt, b_rt, w_gate, w_up, w_down,
           g_final):
    b, s, d = x.shape
    assert g_mix.shape[0] == 1, "the final norm is fused into the single layer's expert kernel"
    assert N_CLASSES <= ROUTE_LANES
    bf = jnp.bfloat16
    kt, v = _kv_proj(mem, g_mem[0][None], w_k[0].astype(bf), w_v[0].astype(bf))

    bias = jnp.repeat(b_s[0].T, GMLP_HEAD_DIM, axis=1)
    x1 = _mixer(x, g_mix[0][None], w_in[0].astype(bf), conv_w[0], g_v[0][None], w_s[0], bias,
                g_out_conv[0][None], g_out_gmlp[0][None], w_out[0].astype(bf))

    pad = ROUTE_LANES - N_GROUPS - N_EXPERTS
    w_r = jnp.concatenate([w_grp[0], w_rt[0], jnp.zeros((d, pad), jnp.float32)], axis=1).astype(bf)
    b_r = jnp.concatenate([b_grp[0], b_rt[0], jnp.zeros((pad,), jnp.float32)])[None]
    gap = ROUTE_EXPERT_ROW0 - N_GROUPS
    tail = ROUTE_LANES - ROUTE_EXPERT_ROW0 - N_EXPERTS
    w_r_t = jnp.concatenate([w_grp[0].T, jnp.zeros((gap, d), jnp.float32), w_rt[0].T,
                             jnp.zeros((tail, d), jnp.float32)], axis=0).astype(bf)
    b_r_t = jnp.concatenate([b_grp[0], jnp.zeros((gap,), jnp.float32), b_rt[0],
                             jnp.zeros((tail,), jnp.float32)])[:, None]
    x2, slab, counts_col = _attn_route(x1, kt, v, g_xattn[0][None], w_q[0].astype(bf),
                                       w_o[0].astype(bf), g_ffn[0][None], w_r_t, b_r_t)
    out = _moe_final(x2.reshape(b * s, d), slab, counts_col, g_ffn[0][None], w_r, b_r,
                     g_final[None], w_gate[0], w_up[0], w_down[0])
    return out.reshape(b, s, d)
```

```python
import functools

import jax
import jax.numpy as jnp
from jax import lax
from jax.experimental import pallas as pl
from jax.experimental.pallas import tpu as pltpu

EPS = 1e-6
CONV_GROUP_WIDTH = 512
GMLP_HEADS = 8
GMLP_HEAD_DIM = 64
CHUNK = 128
CONV_K = 3
XA_HEADS = 4
N_GROUPS = 4
EXPERTS_PER_GROUP = 8
N_EXPERTS = N_GROUPS * EXPERTS_PER_GROUP
TOP_K = 2
ROUTE_LANES = 128
EXPERT_LANE0 = N_GROUPS
ROUTE_EXPERT_ROW0 = 8

N_PAIRS = EXPERTS_PER_GROUP * (EXPERTS_PER_GROUP - 1) // 2
N_CLASSES = N_GROUPS * N_PAIRS
PAIRS = [(a, b) for a in range(EXPERTS_PER_GROUP) for b in range(a + 1, EXPERTS_PER_GROUP)]

TOKEN_TILE = 1024
SUB_TILE = 256
ROW_BLOCK = 128
BLOCKS_PER_STEP = 4
COPY_TILE = 2048
VMEM_LIMIT_BYTES = 56 * 1024 * 1024

SUBLANES = 8

_NEG = -1e30


def _rms(x, g):
    return x * lax.rsqrt(jnp.mean(x * x, axis=-1, keepdims=True) + EPS) * g


def _gelu_tanh(x):
    return 0.5 * x * (1.0 + jnp.tanh(0.7978845608028654 * (x + 0.044715 * (x * x * x))))


def _kv_kernel(mem_ref, g_ref, wk_ref, wv_ref, kt_ref, v_ref):
    m = _rms(mem_ref[0], g_ref[...]).astype(jnp.bfloat16)
    k = jnp.dot(m, wk_ref[...], preferred_element_type=jnp.float32)
    v = jnp.dot(m, wv_ref[...], preferred_element_type=jnp.float32)
    kt_ref[0] = k.T.astype(jnp.bfloat16)
    v_ref[0] = v.astype(jnp.bfloat16)


def _kv_proj(mem, g_mem, w_k, w_v):
    b, m, d = mem.shape
    const = lambda i: (0, 0)
    return pl.pallas_call(
        _kv_kernel,
        grid=(b,),
        in_specs=[
            pl.BlockSpec((1, m, d), lambda i: (i, 0, 0)),
            pl.BlockSpec((1, d), const),
            pl.BlockSpec((d, d), const),
            pl.BlockSpec((d, d), const),
        ],
        out_specs=[
            pl.BlockSpec((1, d, m), lambda i: (i, 0, 0)),
            pl.BlockSpec((1, m, d), lambda i: (i, 0, 0)),
        ],
        out_shape=[
            jax.ShapeDtypeStruct((b, d, m), jnp.bfloat16),
            jax.ShapeDtypeStruct((b, m, d), jnp.bfloat16),
        ],
        compiler_params=pltpu.CompilerParams(
            dimension_semantics=("arbitrary",), vmem_limit_bytes=VMEM_LIMIT_BYTES),
        name="kv_proj",
    )(mem, g_mem, w_k, w_v)


def _mixer_kernel(x_ref, gmix_ref, win_ref, convw_ref, gv_ref, ws_ref, bias_ref,
                  goc_ref, gog_ref, wout_ref, o_ref, zbuf_ref):
    ts = x_ref.shape[1]
    w = CONV_GROUP_WIDTH
    hw = 2 * GMLP_HEAD_DIM
    nsub = ts // SUB_TILE

    def dot(a, b):
        return jnp.dot(a, b, preferred_element_type=jnp.float32)

    @pl.when(pl.program_id(1) == 0)
    def _():
        zbuf_ref[0:8, :] = jnp.zeros((8, w), jnp.float32)

    @pl.when(pl.program_id(1) != 0)
    def _():
        zbuf_ref[0:8, :] = zbuf_ref[ts:ts + 8, :]

    low = lax.broadcasted_iota(jnp.int32, (SUB_TILE, hw), 1) < GMLP_HEAD_DIM
    row = lax.broadcasted_iota(jnp.int32, (CHUNK, CHUNK), 0)
    colid = lax.broadcasted_iota(jnp.int32, (CHUNK, CHUNK), 1)
    causal = row >= colid
    ws = [jnp.where(causal, ws_ref[hd], 0.0).astype(jnp.bfloat16) for hd in range(GMLP_HEADS)]
    npair = SUB_TILE // CHUNK // 2
    lo = lax.broadcasted_iota(jnp.int32, (CHUNK, hw), 1) < GMLP_HEAD_DIM
    swap = lambda a: pltpu.roll(a, GMLP_HEAD_DIM, axis=1)

    def project(i):
        r0 = i * SUB_TILE
        xt = x_ref[0, r0:r0 + SUB_TILE, :]
        h = _rms(xt, gmix_ref[...]).astype(jnp.bfloat16)
        p = [dot(h, win_ref[:, k * w:(k + 1) * w]) for k in range(5)]
        z = p[1] * p[2]
        zbuf_ref[8 + r0:8 + r0 + SUB_TILE, :] = z
        return dict(xt=xt, gate_b=p[0], z=z, u=p[3], v=p[4], r0=r0)

    def gate_and_norm(st):
        r0 = st["r0"]
        zc = (convw_ref[0:1, :] * zbuf_ref[6 + r0:6 + r0 + SUB_TILE, :]
              + convw_ref[1:2, :] * zbuf_ref[7 + r0:7 + r0 + SUB_TILE, :]
              + convw_ref[2:3, :] * st["z"])
        st["ya"] = _rms(st["gate_b"] * zc, goc_ref[...]).astype(jnp.bfloat16)
        st["u"] = _gelu_tanh(st["u"])
        v = _gelu_tanh(st["v"])
        v2 = v * v
        ss_cols = []
        for k in range(GMLP_HEADS // 2):
            col = v2[:, hw * k:hw * (k + 1)]
            ss_cols.append(jnp.where(low,
                                     jnp.sum(jnp.where(low, col, 0.0), axis=1, keepdims=True),
                                     jnp.sum(jnp.where(low, 0.0, col), axis=1, keepdims=True)))
        ss = jnp.concatenate(ss_cols, axis=1)
        st["vn"] = v * lax.rsqrt(ss * (1.0 / GMLP_HEAD_DIM) + EPS) * gv_ref[...]

    def mix_positions(st):
        vn = st["vn"]
        s_cols = [[None] * (GMLP_HEADS // 2) for _ in range(2 * npair)]
        for hp in range(GMLP_HEADS // 2):
            cols = [vn[c * CHUNK:(c + 1) * CHUNK, hw * hp:hw * (hp + 1)] for c in range(2 * npair)]
            swapped = [swap(a) for a in cols]
            rhs_even = jnp.concatenate(
                [jnp.where(lo, cols[2 * p], swapped[2 * p + 1]) for p in range(npair)], axis=1)
            rhs_odd = jnp.concatenate(
                [jnp.where(lo, swapped[2 * p], cols[2 * p + 1]) for p in range(npair)], axis=1)
            out_e = dot(ws[2 * hp], rhs_even.astype(jnp.bfloat16))
            out_o = dot(ws[2 * hp + 1], rhs_odd.astype(jnp.bfloat16))
            for p in range(npair):
                e = out_e[:, hw * p:hw * (p + 1)]
                o = out_o[:, hw * p:hw * (p + 1)]
                s_cols[2 * p][hp] = jnp.where(lo, e, swap(o))
                s_cols[2 * p + 1][hp] = jnp.where(lo, swap(e), o)
        st["s"] = jnp.concatenate(
            [jnp.concatenate(c, axis=1) + bias_ref[...] for c in s_cols], axis=0)

    def output(st):
        r0 = st["r0"]
        yb = _rms(st["u"] * st["s"], gog_ref[...]).astype(jnp.bfloat16)
        o_ref[0, r0:r0 + SUB_TILE, :] = (st["xt"] + dot(st["ya"], wout_ref[0:w, :])
                                          + dot(yb, wout_ref[w:2 * w, :]))

    phases = (gate_and_norm, mix_positions, output)
    states = []
    for step in range(nsub + len(phases)):
        if step < nsub:
            states.append(project(step))
        for k, phase in enumerate(phases):
            i = step - 1 - k
            if 0 <= i < nsub:
                phase(states[i])


def _mixer(x, g_mix, w_in, conv_w, g_v, w_s, bias, g_oc, g_og, w_out):
    b, s, d = x.shape
    ts = min(TOKEN_TILE, s)
    const2 = lambda i, j: (0, 0)
    const3 = lambda i, j: (0, 0, 0)
    return pl.pallas_call(
        _mixer_kernel,
        grid=(b, s // ts),
        in_specs=[
            pl.BlockSpec((1, ts, d), lambda i, j: (i, j, 0)),
            pl.BlockSpec(g_mix.shape, const2),
            pl.BlockSpec(w_in.shape, const2),
            pl.BlockSpec(conv_w.shape, const2),
            pl.BlockSpec(g_v.shape, const2),
            pl.BlockSpec(w_s.shape, const3),
            pl.BlockSpec(bias.shape, const2),
            pl.BlockSpec(g_oc.shape, const2),
            pl.BlockSpec(g_og.shape, const2),
            pl.BlockSpec(w_out.shape, const2),
        ],
        out_specs=pl.BlockSpec((1, ts, d), lambda i, j: (i, j, 0)),
        out_shape=jax.ShapeDtypeStruct((b, s, d), jnp.float32),
        scratch_shapes=[pltpu.VMEM((ts + 8, CONV_GROUP_WIDTH), jnp.float32)],
        compiler_params=pltpu.CompilerParams(
            dimension_semantics=("arbitrary", "arbitrary"), vmem_limit_bytes=VMEM_LIMIT_BYTES),
        name="mixer",
    )(x, g_mix, w_in, conv_w, g_v, w_s, bias, g_oc, g_og, w_out)


def _attn_route_kernel(x_ref, kt_ref, v_ref, gx_ref, wq_ref, wo_ref, gf_ref, wrt_ref, brt_ref,
                       x2_ref, slab_ref, counts_ref, carry_ref):
    ts = x_ref.shape[1]
    d = x_ref.shape[2]
    hd = d // XA_HEADS
    nsub = ts // SUB_TILE
    first = (pl.program_id(0) == 0) & (pl.program_id(1) == 0)

    @pl.when(first)
    def _():
        carry_ref[...] = jnp.zeros_like(carry_ref)

    def dot(a, b):
        return jnp.dot(a, b, preferred_element_type=jnp.float32)

    subs = range(nsub)
    x1 = [x_ref[0, i * SUB_TILE:(i + 1) * SUB_TILE, :] for i in subs]
    h2 = [_rms(x, gx_ref[...]).astype(jnp.bfloat16) for x in x1]
    q = [dot(h, wq_ref[...]).astype(jnp.bfloat16) for h in h2]
    heads = [[] for _ in subs]
    for a in range(XA_HEADS):
        sc = [dot(q[i][:, a * hd:(a + 1) * hd], kt_ref[0, a * hd:(a + 1) * hd, :]) * (hd ** -0.5)
              for i in subs]
        p = [jnp.exp(s_ - jnp.max(s_, axis=-1, keepdims=True)) for s_ in sc]
        l = [jnp.sum(p_, axis=-1, keepdims=True) for p_ in p]
        o = [dot(p[i].astype(jnp.bfloat16), v_ref[0, :, a * hd:(a + 1) * hd]) for i in subs]
        for i in subs:
            heads[i].append((o[i] / l[i]).astype(jnp.bfloat16))
    x2 = [x1[i] + dot(jnp.concatenate(heads[i], axis=1), wo_ref[...]) for i in subs]
    for i in subs:
        x2_ref[0, i * SUB_TILE:(i + 1) * SUB_TILE, :] = x2[i]

    h3 = [_rms(x, gf_ref[...]).astype(jnp.bfloat16) for x in x2]
    lgs = [lax.dot_general(wrt_ref[...], h, (((1,), (1,)), ((), ())),
                           preferred_element_type=jnp.float32) + brt_ref[...] for h in h3]
    sub = lax.broadcasted_iota(jnp.int32, (EXPERTS_PER_GROUP, SUB_TILE), 0).astype(jnp.float32)
    big = float(EXPERTS_PER_GROUP)
    is_g = sub < N_GROUPS

    def classify(lg):
        glog = lg[0:EXPERTS_PER_GROUP, :]
        gmax = jnp.max(jnp.where(is_g, glog, _NEG), axis=0, keepdims=True)
        gidx = jnp.min(jnp.where(is_g & (glog == gmax), sub, big), axis=0, keepdims=True)
        el = lg[ROUTE_EXPERT_ROW0:ROUTE_EXPERT_ROW0 + EXPERTS_PER_GROUP, :]
        for grp in range(1, N_GROUPS):
            r0 = ROUTE_EXPERT_ROW0 + grp * EXPERTS_PER_GROUP
            el = jnp.where(gidx == grp, lg[r0:r0 + EXPERTS_PER_GROUP, :], el)
        t1 = jnp.max(el, axis=0, keepdims=True)
        i1 = jnp.min(jnp.where(el == t1, sub, big), axis=0, keepdims=True)
        rest = sub != i1
        t2 = jnp.max(jnp.where(rest, el, _NEG), axis=0, keepdims=True)
        i2 = jnp.min(jnp.where(rest & (el == t2), sub, big), axis=0, keepdims=True)
        a = jnp.minimum(i1, i2)
        b = jnp.maximum(i1, i2)
        pair = a * (2 * EXPERTS_PER_GROUP - 1 - a) * 0.5 + (b - a - 1.0)
        return gidx * N_PAIRS + pair

    cls = [classify(lg) for lg in lgs]

    r = lax.broadcasted_iota(jnp.int32, (SUB_TILE, SUB_TILE), 0)
    c = lax.broadcasted_iota(jnp.int32, (SUB_TILE, SUB_TILE), 1)
    earlier = jnp.where(r < c, 1.0, 0.0).astype(jnp.bfloat16)
    crow = lax.broadcasted_iota(jnp.int32, (ROUTE_LANES, SUB_TILE), 0).astype(jnp.float32)
    oh = [crow == cl for cl in cls]
    ohf = [jnp.where(o_, 1.0, 0.0) for o_ in oh]
    inside = [dot(o_.astype(jnp.bfloat16), earlier) for o_ in ohf]
    seen = carry_ref[...]
    for i in subs:
        rank = jnp.sum(jnp.where(oh[i], inside[i] + seen, 0.0), axis=0, keepdims=True)
        slab_ref[:, i * SUB_TILE:(i + 1) * SUB_TILE] = jnp.where(
            sub == 0, cls[i], jnp.where(sub == 1, rank, 0.0))
        seen = seen + jnp.sum(ohf[i], axis=1, keepdims=True)
    carry_ref[...] = seen
    counts_ref[...] = seen


def _attn_route(x1, kt, v, g_x, w_q, w_o, g_f, w_rt, b_rt):
    b, s, d = x1.shape
    m = v.shape[1]
    ts = min(TOKEN_TILE, s)
    nt = s // ts
    const2 = lambda i, j: (0, 0)
    return pl.pallas_call(
        _attn_route_kernel,
        grid=(b, nt),
        in_specs=[
            pl.BlockSpec((1, ts, d), lambda i, j: (i, j, 0)),
            pl.BlockSpec((1, d, m), lambda i, j: (i, 0, 0)),
            pl.BlockSpec((1, m, d), lambda i, j: (i, 0, 0)),
            pl.BlockSpec(g_x.shape, const2),
            pl.BlockSpec(w_q.shape, const2),
            pl.BlockSpec(w_o.shape, const2),
            pl.BlockSpec(g_f.shape, const2),
            pl.BlockSpec(w_rt.shape, const2),
            pl.BlockSpec(b_rt.shape, const2),
        ],
        out_specs=[
            pl.BlockSpec((1, ts, d), lambda i, j: (i, j, 0)),
            pl.BlockSpec((SUBLANES, ts), lambda i, j: (0, i * nt + j)),
            pl.BlockSpec((ROUTE_LANES, 1), const2),
        ],
        out_shape=[
            jax.ShapeDtypeStruct((b, s, d), jnp.float32),
            jax.ShapeDtypeStruct((SUBLANES, b * s), jnp.float32),
            jax.ShapeDtypeStruct((ROUTE_LANES, 1), jnp.float32),
        ],
        scratch_shapes=[pltpu.VMEM((ROUTE_LANES, 1), jnp.float32)],
        compiler_params=pltpu.CompilerParams(
            dimension_semantics=("arbitrary", "arbitrary"), vmem_limit_bytes=VMEM_LIMIT_BYTES),
        name="attn_route",
    )(x1, kt, v, g_x, w_q, w_o, g_f, w_rt, b_rt)


def _dest_kernel(slab_ref, starts_ref, o_ref):
    slab = slab_ref[...]
    tt = slab.shape[1]
    crow = lax.broadcasted_iota(jnp.int32, (ROUTE_LANES, tt), 0).astype(jnp.float32)
    start = jnp.sum(jnp.where(crow == slab[0:1, :], starts_ref[...], 0.0), axis=0, keepdims=True)
    sub = lax.broadcasted_iota(jnp.int32, slab.shape, 0)
    o_ref[...] = jnp.where(sub == 0, start + slab[1:2, :], 0.0).astype(jnp.int32)


def _dest_rows(slab, starts_col):
    t = slab.shape[1]
    tt = min(2048, t)
    return pl.pallas_call(
        _dest_kernel,
        grid=(t // tt,),
        in_specs=[pl.BlockSpec((SUBLANES, tt), lambda i: (0, i)),
                  pl.BlockSpec((ROUTE_LANES, 1), lambda i: (0, 0))],
        out_specs=pl.BlockSpec((SUBLANES, tt), lambda i: (0, i)),
        out_shape=jax.ShapeDtypeStruct((SUBLANES, t), jnp.int32),
        compiler_params=pltpu.CompilerParams(dimension_semantics=("arbitrary",)),
        name="dest_rows",
    )(slab, starts_col)


def _wait_rows(src_ref, dst_ref, sem, n):
    pltpu.make_async_copy(src_ref.at[pl.ds(0, n)], dst_ref.at[pl.ds(0, n)], sem).wait()


def _dispatch_kernel(zero_block_ref, dest_ref, x_ref, xs_ref, zero_ref, sem, zsem):
    tiles, sub, _ = x_ref.shape
    nb = zero_block_ref.shape[0]

    def block_copy(j):
        start = pl.multiple_of(j * ROW_BLOCK, ROW_BLOCK)
        return pltpu.make_async_copy(zero_ref, xs_ref.at[pl.ds(start, ROW_BLOCK)], zsem)

    @pl.when(pl.program_id(0) == 0)
    def _():
        zero_ref[...] = jnp.zeros_like(zero_ref)

        @pl.loop(0, nb)
        def _(j):
            @pl.when(zero_block_ref[j] > 0)
            def _():
                block_copy(j).start()

        @pl.loop(0, nb)
        def _(j):
            @pl.when(zero_block_ref[j] > 0)
            def _():
                block_copy(j).wait()

    def issue(i, carry):
        for k in range(sub):
            pltpu.make_async_copy(x_ref.at[i, pl.ds(k, 1)],
                                  xs_ref.at[pl.ds(dest_ref[i * sub + k], 1)], sem).start(
                                      priority=k % 2)
        return carry

    lax.fori_loop(0, tiles, issue, 0)
    _wait_rows(xs_ref, xs_ref, sem, tiles * sub)


def _dispatch(x2, dest, zero_block):
    t, d = x2.shape
    tc = min(COPY_TILE, t)
    cap = zero_block.shape[0] * ROW_BLOCK
    grid_spec = pltpu.PrefetchScalarGridSpec(
        num_scalar_prefetch=1,
        grid=(t // tc,),
        in_specs=[
            pl.BlockSpec((tc,), lambda i, zb: (i,), memory_space=pltpu.SMEM),
            pl.BlockSpec((tc // SUBLANES, SUBLANES, d), lambda i, zb: (i, 0, 0)),
        ],
        out_specs=pl.BlockSpec(memory_space=pl.ANY),
        scratch_shapes=[pltpu.VMEM((ROW_BLOCK, d), jnp.float32),
                        pltpu.SemaphoreType.DMA, pltpu.SemaphoreType.DMA],
    )
    return pl.pallas_call(
        _dispatch_kernel,
        grid_spec=grid_spec,
        out_shape=jax.ShapeDtypeStruct((cap, d), jnp.float32),
        compiler_params=pltpu.CompilerParams(
            dimension_semantics=("arbitrary",), has_side_effects=True),
        name="dispatch",
    )(zero_block, dest, x2.reshape(t // SUBLANES, SUBLANES, d))


def _unsort_kernel(dest_ref, ys_ref, o_ref, sem):
    tiles, sub, _ = o_ref.shape

    def issue(i, carry):
        for k in range(sub):
            pltpu.make_async_copy(ys_ref.at[pl.ds(dest_ref[i * sub + k], 1)],
                                  o_ref.at[i, pl.ds(k, 1)], sem).start(priority=k % 2)
        return carry

    lax.fori_loop(0, tiles, issue, 0)
    _wait_rows(ys_ref, ys_ref, sem, tiles * sub)


def _unsort(ys, dest, t):
    d = ys.shape[1]
    tc = min(COPY_TILE, t)
    return pl.pallas_call(
        _unsort_kernel,
        grid=(t // tc,),
        in_specs=[
            pl.BlockSpec((tc,), lambda i: (i,), memory_space=pltpu.SMEM),
            pl.BlockSpec(memory_space=pl.ANY),
        ],
        out_specs=pl.BlockSpec((tc // SUBLANES, SUBLANES, d), lambda i: (i, 0, 0)),
        out_shape=jax.ShapeDtypeStruct((t // SUBLANES, SUBLANES, d), jnp.float32),
        scratch_shapes=[pltpu.SemaphoreType.DMA],
        compiler_params=pltpu.CompilerParams(dimension_semantics=("arbitrary",)),
        name="unsort",
    )(dest, ys).reshape(t, d)


def _expert_kernel(grp_ref, ea_ref, eb_ref, nact_ref,
                   xs_ref, gf_ref, wr_ref, br_ref, gfin_ref, wg_hbm, wu_hbm, wd_hbm,
                   o_ref, wgb, wub, wdb, sg, su, sd, sem, cnt_ref):
    s = pl.program_id(0)
    blocks = tuple(BLOCKS_PER_STEP * s + i for i in range(BLOCKS_PER_STEP))
    g = grp_ref[blocks[0]]

    def stage(e):
        slot = e % 2
        return (pltpu.make_async_copy(wg_hbm.at[e], sg.at[slot], sem.at[slot, 0]),
                pltpu.make_async_copy(wu_hbm.at[e], su.at[slot], sem.at[slot, 1]),
                pltpu.make_async_copy(wd_hbm.at[e], sd.at[slot], sem.at[slot, 2]))

    def start_next():
        @pl.when(cnt_ref[0] < N_EXPERTS)
        def _():
            for cp in stage(cnt_ref[0]):
                cp.start(priority=1)
            cnt_ref[0] = cnt_ref[0] + 1

    @pl.when(s == 0)
    def _():
        cnt_ref[0] = 0
        cnt_ref[1] = 0
        start_next()
        start_next()

    active = blocks[0] < nact_ref[0]

    @pl.when(active)
    def _():
        need = g * EXPERTS_PER_GROUP + functools.reduce(
            jnp.maximum, [eb_ref[j] for j in blocks])

        def load(e, carry):
            for cp in stage(e):
                cp.wait()
            slot = e % 2
            k = e % EXPERTS_PER_GROUP
            wgb[k] = sg[slot].astype(jnp.bfloat16)
            wub[k] = su[slot].astype(jnp.bfloat16)
            wdb[k] = sd[slot].astype(jnp.bfloat16)
            cnt_ref[1] = e + 1
            start_next()
            return carry

        lax.fori_loop(cnt_ref[1], need + 1, load, 0)

        lane = lax.broadcasted_iota(jnp.int32, (ROW_BLOCK, ROUTE_LANES), 1)
        is_g = lane < N_GROUPS
        lo = EXPERT_LANE0 + EXPERTS_PER_GROUP * g
        nblk = len(blocks)
        xs = [xs_ref[i * ROW_BLOCK:(i + 1) * ROW_BLOCK, :] for i in range(nblk)]
        h3 = [_rms(x, gf_ref[...]).astype(jnp.bfloat16) for x in xs]
        lg = [jnp.dot(h, wr_ref[...], preferred_element_type=jnp.float32) + br_ref[...]
              for h in h3]

        def gates(lgi, j):
            def pick(idx):
                return jnp.sum(jnp.where(lane == idx, lgi, 0.0), axis=-1, keepdims=True)

            gmax = jnp.max(jnp.where(is_g, lgi, _NEG), axis=-1, keepdims=True)
            den = jnp.sum(jnp.where(is_g, jnp.exp(jnp.where(is_g, lgi, _NEG) - gmax), 0.0),
                          axis=-1, keepdims=True)
            grp_p = jnp.exp(pick(g) - gmax) / den
            la = pick(lo + ea_ref[j])
            lb = pick(lo + eb_ref[j])
            m = jnp.maximum(la, lb)
            pa = jnp.exp(la - m)
            pb = jnp.exp(lb - m)
            return grp_p * pa / (pa + pb), grp_p * pb / (pa + pb)

        gate = [gates(lg[i], blocks[i]) for i in range(nblk)]
        ys = list(xs)
        for which, e_ref in enumerate((ea_ref, eb_ref)):
            ks = [e_ref[j] for j in blocks]
            gg = [jnp.dot(h3[i], wgb[ks[i]], preferred_element_type=jnp.float32)
                  for i in range(nblk)]
            uu = [jnp.dot(h3[i], wub[ks[i]], preferred_element_type=jnp.float32)
                  for i in range(nblk)]
            act = [(gg[i] * (1.0 / (1.0 + jnp.exp(-gg[i]))) * uu[i]
                    * gate[i][which]).astype(jnp.bfloat16) for i in range(nblk)]
            ys = [ys[i] + jnp.dot(act[i], wdb[ks[i]], preferred_element_type=jnp.float32)
                  for i in range(nblk)]
        for i in range(nblk):
            o_ref[i * ROW_BLOCK:(i + 1) * ROW_BLOCK, :] = _rms(ys[i], gfin_ref[...])

    @pl.when(jnp.logical_not(active))
    def _():
        o_ref[...] = jnp.zeros_like(o_ref)

    @pl.when(s == pl.num_programs(0) - 1)
    def _():
        def drain(e, carry):
            for cp in stage(e):
                cp.wait()
            return carry

        lax.fori_loop(cnt_ref[1], cnt_ref[0], drain, 0)


def _experts(xs, blk_grp, blk_a, blk_b, nact, g_ffn, w_r, b_r, g_final, w_gate, w_up, w_down):
    cap, d = xs.shape
    de = w_gate.shape[2]
    step_rows = BLOCKS_PER_STEP * ROW_BLOCK
    steps = cap // step_rows
    pre = lambda f: (lambda s, gr, ea, eb, na: f(s, na))
    const2 = pre(lambda s, na: (0, 0))
    last_step = lambda na: (na[0] - 1) // BLOCKS_PER_STEP
    hbm = pl.BlockSpec(memory_space=pl.ANY)
    grid_spec = pltpu.PrefetchScalarGridSpec(
        num_scalar_prefetch=4,
        grid=(steps,),
        in_specs=[
            pl.BlockSpec((step_rows, d), pre(lambda s, na: (jnp.minimum(s, last_step(na)), 0))),
            pl.BlockSpec(g_ffn.shape, const2),
            pl.BlockSpec(w_r.shape, const2),
            pl.BlockSpec(b_r.shape, const2),
            pl.BlockSpec(g_final.shape, const2),
            hbm, hbm, hbm,
        ],
        out_specs=pl.BlockSpec((step_rows, d), pre(lambda s, na: (s, 0))),
        scratch_shapes=[
            pltpu.VMEM((EXPERTS_PER_GROUP, d, de), jnp.bfloat16),
            pltpu.VMEM((EXPERTS_PER_GROUP, d, de), jnp.bfloat16),
            pltpu.VMEM((EXPERTS_PER_GROUP, de, d), jnp.bfloat16),
            pltpu.VMEM((2, d, de), jnp.float32),
            pltpu.VMEM((2, d, de), jnp.float32),
            pltpu.VMEM((2, de, d), jnp.float32),
            pltpu.SemaphoreType.DMA((2, 3)),
            pltpu.SMEM((2,), jnp.int32),
        ],
    )
    return pl.pallas_call(
        _expert_kernel,
        grid_spec=grid_spec,
        out_shape=jax.ShapeDtypeStruct((cap, d), jnp.float32),
        compiler_params=pltpu.CompilerParams(
            dimension_semantics=("arbitrary",), vmem_limit_bytes=VMEM_LIMIT_BYTES),
        name="experts",
    )(blk_grp, blk_a, blk_b, nact, xs, g_ffn, w_r, b_r, g_final, w_gate, w_up, w_down)


def _moe_final(x2, slab, counts_col, g_ffn, w_r, b_r, g_final, w_gate, w_up, w_down):
    t, d = x2.shape
    nb = t // ROW_BLOCK + N_CLASSES + N_GROUPS * (BLOCKS_PER_STEP - 1)
    assert nb % BLOCKS_PER_STEP == 0
    counts = counts_col[:N_CLASSES, 0].astype(jnp.int32)
    nblk = (counts + ROW_BLOCK - 1) // ROW_BLOCK
    grp_blocks = jnp.sum(nblk.reshape(N_GROUPS, N_PAIRS), axis=1)
    nblk = nblk.reshape(N_GROUPS, N_PAIRS).at[:, N_PAIRS - 1].add(
        (-grp_blocks) % BLOCKS_PER_STEP).reshape(N_CLASSES)
    blk_end = jnp.cumsum(nblk)
    blk_start = blk_end - nblk
    nact = blk_end[-1]
    j = jnp.arange(nb, dtype=jnp.int32)
    blk_class = jnp.minimum(
        jnp.sum((blk_end[None, :] <= j[:, None]).astype(jnp.int32), axis=1), N_CLASSES - 1)
    blk_class = jnp.where(j < nact, blk_class, blk_class[jnp.maximum(nact - 1, 0)])
    not_full = (j - blk_start[blk_class] + 1) * ROW_BLOCK > counts[blk_class]
    zero_block = (not_full | (j >= nact)).astype(jnp.int32)
    pair_a = jnp.array([p[0] for p in PAIRS], jnp.int32)
    pair_b = jnp.array([p[1] for p in PAIRS], jnp.int32)
    blk_grp = blk_class // N_PAIRS
    blk_a = pair_a[blk_class % N_PAIRS]
    blk_b = pair_b[blk_class % N_PAIRS]
    starts_col = jnp.zeros((ROUTE_LANES, 1), jnp.float32).at[:N_CLASSES, 0].set(
        (blk_start * ROW_BLOCK).astype(jnp.float32))

    dest = _dest_rows(slab, starts_col)[0]
    xs = _dispatch(x2, dest, zero_block)
    ys = _experts(xs, blk_grp, blk_a, blk_b, nact[None], g_ffn, w_r, b_r, g_final,
                  w_gate, w_up, w_down)
    return _unsort(ys, dest, t)


def kernel(x, mem, g_mix, w_in, conv_w, g_v, w_s, b_s, g_out_conv, g_out_gmlp, w_out, g_xattn,
           g_mem, w_q, w_k, w_v, w_o, g_ffn, w_grp, b_grp, w_rt, b_rt, w_gate, w_up, w_down,
           g_final):
    b, s, d = x.shape
    assert g_mix.shape[0] == 1, "the final norm is fused into the single layer's expert kernel"
    assert N_CLASSES <= ROUTE_LANES
    bf = jnp.bfloat16
    kt, v = _kv_proj(mem, g_mem[0][None], w_k[0].astype(bf), w_v[0].astype(bf))

    bias = jnp.repeat(b_s[0].T, GMLP_HEAD_DIM, axis=1)
    x1 = _mixer(x, g_mix[0][None], w_in[0].astype(bf), conv_w[0], g_v[0][None], w_s[0], bias,
                g_out_conv[0][None], g_out_gmlp[0][None], w_out[0].astype(bf))

    pad = ROUTE_LANES - N_GROUPS - N_EXPERTS
    w_r = jnp.concatenate([w_grp[0], w_rt[0], jnp.zeros((d, pad), jnp.float32)], axis=1).astype(bf)
    b_r = jnp.concatenate([b_grp[0], b_rt[0], jnp.zeros((pad,), jnp.float32)])[None]
    gap = ROUTE_EXPERT_ROW0 - N_GROUPS
    tail = ROUTE_LANES - ROUTE_EXPERT_ROW0 - N_EXPERTS
    w_r_t = jnp.concatenate([w_grp[0].T, jnp.zeros((gap, d), jnp.float32), w_rt[0].T,
                             jnp.zeros((tail, d), jnp.float32)], axis=0).astype(bf)
    b_r_t = jnp.concatenate([b_grp[0], jnp.zeros((gap,), jnp.float32), b_rt[0],
                             jnp.zeros((tail,), jnp.float32)])[:, None]
    x2, slab, counts_col = _attn_route(x1, kt, v, g_xattn[0][None], w_q[0].astype(bf),
                                       w_o[0].astype(bf), g_ffn[0][None], w_r_t, b_r_t)
    out = _moe_final(x2.reshape(b * s, d), slab, counts_col, g_ffn[0][None], w_r, b_r,
                     g_final[None], w_gate[0], w_up[0], w_down[0])
    return out.reshape(b, s, d)
```

```python
import functools

import jax
import jax.numpy as jnp
from jax import lax
from jax.experimental import pallas as pl
from jax.experimental.pallas import tpu as pltpu

EPS = 1e-6
CONV_GROUP_WIDTH = 512
GMLP_HEADS = 8
GMLP_HEAD_DIM = 64
CHUNK = 128
CONV_K = 3
XA_HEADS = 4
N_GROUPS = 4
EXPERTS_PER_GROUP = 8
N_EXPERTS = N_GROUPS * EXPERTS_PER_GROUP
TOP_K = 2
ROUTE_LANES = 128
EXPERT_LANE0 = N_GROUPS
ROUTE_EXPERT_ROW0 = 8

N_PAIRS = EXPERTS_PER_GROUP * (EXPERTS_PER_GROUP - 1) // 2
N_CLASSES = N_GROUPS * N_PAIRS
PAIRS = [(a, b) for a in range(EXPERTS_PER_GROUP) for b in range(a + 1, EXPERTS_PER_GROUP)]

TOKEN_TILE = 1024
SUB_TILE = 256
ROW_BLOCK = 128
BLOCKS_PER_STEP = 4
COPY_TILE = 2048
VMEM_LIMIT_BYTES = 56 * 1024 * 1024

SUBLANES = 8

_NEG = -1e30


def _rms(x, g):
    return x * lax.rsqrt(jnp.mean(x * x, axis=-1, keepdims=True) + EPS) * g


def _gelu_tanh(x):
    return 0.5 * x * (1.0 + jnp.tanh(0.7978845608028654 * (x + 0.044715 * (x * x * x))))


def _kv_kernel(mem_ref, g_ref, wk_ref, wv_ref, kt_ref, v_ref):
    m = _rms(mem_ref[0], g_ref[...]).astype(jnp.bfloat16)
    k = jnp.dot(m, wk_ref[...], preferred_element_type=jnp.float32)
    v = jnp.dot(m, wv_ref[...], preferred_element_type=jnp.float32)
    kt_ref[0] = k.T.astype(jnp.bfloat16)
    v_ref[0] = v.astype(jnp.bfloat16)


def _kv_proj(mem, g_mem, w_k, w_v):
    b, m, d = mem.shape
    const = lambda i: (0, 0)
    return pl.pallas_call(
        _kv_kernel,
        grid=(b,),
        in_specs=[
            pl.BlockSpec((1, m, d), lambda i: (i, 0, 0)),
            pl.BlockSpec((1, d), const),
            pl.BlockSpec((d, d), const),
            pl.BlockSpec((d, d), const),
        ],
        out_specs=[
            pl.BlockSpec((1, d, m), lambda i: (i, 0, 0)),
            pl.BlockSpec((1, m, d), lambda i: (i, 0, 0)),
        ],
        out_shape=[
            jax.ShapeDtypeStruct((b, d, m), jnp.bfloat16),
            jax.ShapeDtypeStruct((b, m, d), jnp.bfloat16),
        ],
        compiler_params=pltpu.CompilerParams(
            dimension_semantics=("arbitrary",), vmem_limit_bytes=VMEM_LIMIT_BYTES),
        name="kv_proj",
    )(mem, g_mem, w_k, w_v)


def _mixer_kernel(x_ref, gmix_ref, win_ref, convw_ref, gv_ref, ws_ref, bias_ref,
                  goc_ref, gog_ref, wout_ref, o_ref, zbuf_ref):
    ts = x_ref.shape[1]
    w = CONV_GROUP_WIDTH
    hw = 2 * GMLP_HEAD_DIM
    nsub = ts // SUB_TILE

    def dot(a, b):
        return jnp.dot(a, b, preferred_element_type=jnp.float32)

    @pl.when(pl.program_id(1) == 0)
    def _():
        zbuf_ref[0:8, :] = jnp.zeros((8, w), jnp.float32)

    @pl.when(pl.program_id(1) != 0)
    def _():
        zbuf_ref[0:8, :] = zbuf_ref[ts:ts + 8, :]

    low = lax.broadcasted_iota(jnp.int32, (SUB_TILE, hw), 1) < GMLP_HEAD_DIM
    row = lax.broadcasted_iota(jnp.int32, (CHUNK, CHUNK), 0)
    colid = lax.broadcasted_iota(jnp.int32, (CHUNK, CHUNK), 1)
    causal = row >= colid
    ws = [jnp.where(causal, ws_ref[hd], 0.0).astype(jnp.bfloat16) for hd in range(GMLP_HEADS)]
    npair = SUB_TILE // CHUNK // 2
    lo = lax.broadcasted_iota(jnp.int32, (CHUNK, hw), 1) < GMLP_HEAD_DIM
    swap = lambda a: pltpu.roll(a, GMLP_HEAD_DIM, axis=1)

    def project(i):
        r0 = i * SUB_TILE
        xt = x_ref[0, r0:r0 + SUB_TILE, :]
        h = _rms(xt, gmix_ref[...]).astype(jnp.bfloat16)
        p = [dot(h, win_ref[:, k * w:(k + 1) * w]) for k in range(5)]
        z = p[1] * p[2]
        zbuf_ref[8 + r0:8 + r0 + SUB_TILE, :] = z
        return dict(xt=xt, gate_b=p[0], z=z, u=p[3], v=p[4], r0=r0)

    def gate_and_norm(st):
        r0 = st["r0"]
        zc = (convw_ref[0:1, :] * zbuf_ref[6 + r0:6 + r0 + SUB_TILE, :]
              + convw_ref[1:2, :] * zbuf_ref[7 + r0:7 + r0 + SUB_TILE, :]
              + convw_ref[2:3, :] * st["z"])
        st["ya"] = _rms(st["gate_b"] * zc, goc_ref[...]).astype(jnp.bfloat16)
        st["u"] = _gelu_tanh(st["u"])
        v = _gelu_tanh(st["v"])
        v2 = v * v
        ss_cols = []
        for k in range(GMLP_HEADS // 2):
            col = v2[:, hw * k:hw * (k + 1)]
            ss_cols.append(jnp.where(low,
                                     jnp.sum(jnp.where(low, col, 0.0), axis=1, keepdims=True),
                                     jnp.sum(jnp.where(low, 0.0, col), axis=1, keepdims=True)))
        ss = jnp.concatenate(ss_cols, axis=1)
        st["vn"] = v * lax.rsqrt(ss * (1.0 / GMLP_HEAD_DIM) + EPS) * gv_ref[...]

    def mix_positions(st):
        vn = st["vn"]
        s_cols = [[None] * (GMLP_HEADS // 2) for _ in range(2 * npair)]
        for hp in range(GMLP_HEADS // 2):
            cols = [vn[c * CHUNK:(c + 1) * CHUNK, hw * hp:hw * (hp + 1)] for c in range(2 * npair)]
            swapped = [swap(a) for a in cols]
            rhs_even = jnp.concatenate(
                [jnp.where(lo, cols[2 * p], swapped[2 * p + 1]) for p in range(npair)], axis=1)
            rhs_odd = jnp.concatenate(
                [jnp.where(lo, swapped[2 * p], cols[2 * p + 1]) for p in range(npair)], axis=1)
            out_e = dot(ws[2 * hp], rhs_even.astype(jnp.bfloat16))
            out_o = dot(ws[2 * hp + 1], rhs_odd.astype(jnp.bfloat16))
            for p in range(npair):
                e = out_e[:, hw * p:hw * (p + 1)]
                o = out_o[:, hw * p:hw * (p + 1)]
                s_cols[2 * p][hp] = jnp.where(lo, e, swap(o))
                s_cols[2 * p + 1][hp] = jnp.where(lo, swap(e), o)
        st["s"] = jnp.concatenate(
            [jnp.concatenate(c, axis=1) + bias_ref[...] for c in s_cols], axis=0)

    def output(st):
        r0 = st["r0"]
        yb = _rms(st["u"] * st["s"], gog_ref[...]).astype(jnp.bfloat16)
        o_ref[0, r0:r0 + SUB_TILE, :] = (st["xt"] + dot(st["ya"], wout_ref[0:w, :])
                                          + dot(yb, wout_ref[w:2 * w, :]))

    phases = (gate_and_norm, mix_positions, output)
    states = []
    for step in range(nsub + len(phases)):
        if step < nsub:
            states.append(project(step))
        for k, phase in enumerate(phases):
            i = step - 1 - k
            if 0 <= i < nsub:
                phase(states[i])


def _mixer(x, g_mix, w_in, conv_w, g_v, w_s, bias, g_oc, g_og, w_out):
    b, s, d = x.shape
    ts = min(TOKEN_TILE, s)
    const2 = lambda i, j: (0, 0)
    const3 = lambda i, j: (0, 0, 0)
    return pl.pallas_call(
        _mixer_kernel,
        grid=(b, s // ts),
        in_specs=[
            pl.BlockSpec((1, ts, d), lambda i, j: (i, j, 0)),
            pl.BlockSpec(g_mix.shape, const2),
            pl.BlockSpec(w_in.shape, const2),
            pl.BlockSpec(conv_w.shape, const2),
            pl.BlockSpec(g_v.shape, const2),
            pl.BlockSpec(w_s.shape, const3),
            pl.BlockSpec(bias.shape, const2),
            pl.BlockSpec(g_oc.shape, const2),
            pl.BlockSpec(g_og.shape, const2),
            pl.BlockSpec(w_out.shape, const2),
        ],
        out_specs=pl.BlockSpec((1, ts, d), lambda i, j: (i, j, 0)),
        out_shape=jax.ShapeDtypeStruct((b, s, d), jnp.float32),
        scratch_shapes=[pltpu.VMEM((ts + 8, CONV_GROUP_WIDTH), jnp.float32)],
        compiler_params=pltpu.CompilerParams(
            dimension_semantics=("arbitrary", "arbitrary"), vmem_limit_bytes=VMEM_LIMIT_BYTES),
        name="mixer",
    )(x, g_mix, w_in, conv_w, g_v, w_s, bias, g_oc, g_og, w_out)


def _attn_route_kernel(x_ref, kt_ref, v_ref, gx_ref, wq_ref, wo_ref, gf_ref, wrt_ref, brt_ref,
                       x2_ref, slab_ref, counts_ref, carry_ref):
    ts = x_ref.shape[1]
    d = x_ref.shape[2]
    hd = d // XA_HEADS
    nsub = ts // SUB_TILE
    first = (pl.program_id(0) == 0) & (pl.program_id(1) == 0)

    @pl.when(first)
    def _():
        carry_ref[...] = jnp.zeros_like(carry_ref)

    def dot(a, b):
        return jnp.dot(a, b, preferred_element_type=jnp.float32)

    subs = range(nsub)
    x1 = [x_ref[0, i * SUB_TILE:(i + 1) * SUB_TILE, :] for i in subs]
    h2 = [_rms(x, gx_ref[...]).astype(jnp.bfloat16) for x in x1]
    q = [dot(h, wq_ref[...]).astype(jnp.bfloat16) for h in h2]
    heads = [[] for _ in subs]
    for a in range(XA_HEADS):
        sc = [dot(q[i][:, a * hd:(a + 1) * hd], kt_ref[0, a * hd:(a + 1) * hd, :]) * (hd ** -0.5)
              for i in subs]
        p = [jnp.exp(s_ - jnp.max(s_, axis=-1, keepdims=True)) for s_ in sc]
        l = [jnp.sum(p_, axis=-1, keepdims=True) for p_ in p]
        o = [dot(p[i].astype(jnp.bfloat16), v_ref[0, :, a * hd:(a + 1) * hd]) for i in subs]
        for i in subs:
            heads[i].append((o[i] / l[i]).astype(jnp.bfloat16))
    x2 = [x1[i] + dot(jnp.concatenate(heads[i], axis=1), wo_ref[...]) for i in subs]
    for i in subs:
        x2_ref[0, i * SUB_TILE:(i + 1) * SUB_TILE, :] = x2[i]

    h3 = [_rms(x, gf_ref[...]).astype(jnp.bfloat16) for x in x2]
    lgs = [lax.dot_general(wrt_ref[...], h, (((1,), (1,)), ((), ())),
                           preferred_element_type=jnp.float32) + brt_ref[...] for h in h3]
    sub = lax.broadcasted_iota(jnp.int32, (EXPERTS_PER_GROUP, SUB_TILE), 0).astype(jnp.float32)
    big = float(EXPERTS_PER_GROUP)
    is_g = sub < N_GROUPS

    def classify(lg):
        glog = lg[0:EXPERTS_PER_GROUP, :]
        gmax = jnp.max(jnp.where(is_g, glog, _NEG), axis=0, keepdims=True)
        gidx = jnp.min(jnp.where(is_g & (glog == gmax), sub, big), axis=0, keepdims=True)
        el = lg[ROUTE_EXPERT_ROW0:ROUTE_EXPERT_ROW0 + EXPERTS_PER_GROUP, :]
        for grp in range(1, N_GROUPS):
            r0 = ROUTE_EXPERT_ROW0 + grp * EXPERTS_PER_GROUP
            el = jnp.where(gidx == grp, lg[r0:r0 + EXPERTS_PER_GROUP, :], el)
        t1 = jnp.max(el, axis=0, keepdims=True)
        i1 = jnp.min(jnp.where(el == t1, sub, big), axis=0, keepdims=True)
        rest = sub != i1
        t2 = jnp.max(jnp.where(rest, el, _NEG), axis=0, keepdims=True)
        i2 = jnp.min(jnp.where(rest & (el == t2), sub, big), axis=0, keepdims=True)
        a = jnp.minimum(i1, i2)
        b = jnp.maximum(i1, i2)
        pair = a * (2 * EXPERTS_PER_GROUP - 1 - a) * 0.5 + (b - a - 1.0)
        return gidx * N_PAIRS + pair

    cls = [classify(lg) for lg in lgs]

    r = lax.broadcasted_iota(jnp.int32, (SUB_TILE, SUB_TILE), 0)
    c = lax.broadcasted_iota(jnp.int32, (SUB_TILE, SUB_TILE), 1)
    earlier = jnp.where(r < c, 1.0, 0.0).astype(jnp.bfloat16)
    crow = lax.broadcasted_iota(jnp.int32, (ROUTE_LANES, SUB_TILE), 0).astype(jnp.float32)
    oh = [crow == cl for cl in cls]
    ohf = [jnp.where(o_, 1.0, 0.0) for o_ in oh]
    inside = [dot(o_.astype(jnp.bfloat16), earlier) for o_ in ohf]
    seen = carry_ref[...]
    for i in subs:
        rank = jnp.sum(jnp.where(oh[i], inside[i] + seen, 0.0), axis=0, keepdims=True)
        slab_ref[:, i * SUB_TILE:(i + 1) * SUB_TILE] = jnp.where(
            sub == 0, cls[i], jnp.where(sub == 1, rank, 0.0))
        seen = seen + jnp.sum(ohf[i], axis=1, keepdims=True)
    carry_ref[...] = seen
    counts_ref[...] = seen


def _attn_route(x1, kt, v, g_x, w_q, w_o, g_f, w_rt, b_rt):
    b, s, d = x1.shape
    m = v.shape[1]
    ts = min(TOKEN_TILE, s)
    nt = s // ts
    const2 = lambda i, j: (0, 0)
    return pl.pallas_call(
        _attn_route_kernel,
        grid=(b, nt),
        in_specs=[
            pl.BlockSpec((1, ts, d), lambda i, j: (i, j, 0)),
            pl.BlockSpec((1, d, m), lambda i, j: (i, 0, 0)),
            pl.BlockSpec((1, m, d), lambda i, j: (i, 0, 0)),
            pl.BlockSpec(g_x.shape, const2),
            pl.BlockSpec(w_q.shape, const2),
            pl.BlockSpec(w_o.shape, const2),
            pl.BlockSpec(g_f.shape, const2),
            pl.BlockSpec(w_rt.shape, const2),
            pl.BlockSpec(b_rt.shape, const2),
        ],
        out_specs=[
            pl.BlockSpec((1, ts, d), lambda i, j: (i, j, 0)),
            pl.BlockSpec((SUBLANES, ts), lambda i, j: (0, i * nt + j)),
            pl.BlockSpec((ROUTE_LANES, 1), const2),
        ],
        out_shape=[
            jax.ShapeDtypeStruct((b, s, d), jnp.float32),
            jax.ShapeDtypeStruct((SUBLANES, b * s), jnp.float32),
            jax.ShapeDtypeStruct((ROUTE_LANES, 1), jnp.float32),
        ],
        scratch_shapes=[pltpu.VMEM((ROUTE_LANES, 1), jnp.float32)],
        compiler_params=pltpu.CompilerParams(
            dimension_semantics=("arbitrary", "arbitrary"), vmem_limit_bytes=VMEM_LIMIT_BYTES),
        name="attn_route",
    )(x1, kt, v, g_x, w_q, w_o, g_f, w_rt, b_rt)


def _dest_kernel(slab_ref, starts_ref, o_ref):
    slab = slab_ref[...]
    tt = slab.shape[1]
    crow = lax.broadcasted_iota(jnp.int32, (ROUTE_LANES, tt), 0).astype(jnp.float32)
    start = jnp.sum(jnp.where(crow == slab[0:1, :], starts_ref[...], 0.0), axis=0, keepdims=True)
    sub = lax.broadcasted_iota(jnp.int32, slab.shape, 0)
    o_ref[...] = jnp.where(sub == 0, start + slab[1:2, :], 0.0).astype(jnp.int32)


def _dest_rows(slab, starts_col):
    t = slab.shape[1]
    tt = min(2048, t)
    return pl.pallas_call(
        _dest_kernel,
        grid=(t // tt,),
        in_specs=[pl.BlockSpec((SUBLANES, tt), lambda i: (0, i)),
                  pl.BlockSpec((ROUTE_LANES, 1), lambda i: (0, 0))],
        out_specs=pl.BlockSpec((SUBLANES, tt), lambda i: (0, i)),
        out_shape=jax.ShapeDtypeStruct((SUBLANES, t), jnp.int32),
        compiler_params=pltpu.CompilerParams(dimension_semantics=("arbitrary",)),
        name="dest_rows",
    )(slab, starts_col)


def _wait_rows(src_ref, dst_ref, sem, n):
    pltpu.make_async_copy(src_ref.at[pl.ds(0, n)], dst_ref.at[pl.ds(0, n)], sem).wait()


def _dispatch_kernel(zero_block_ref, dest_ref, x_ref, xs_ref, zero_ref, sem, zsem):
    tiles, sub, _ = x_ref.shape
    nb = zero_block_ref.shape[0]

    def block_copy(j):
        start = pl.multiple_of(j * ROW_BLOCK, ROW_BLOCK)
        return pltpu.make_async_copy(zero_ref, xs_ref.at[pl.ds(start, ROW_BLOCK)], zsem)

    @pl.when(pl.program_id(0) == 0)
    def _():
        zero_ref[...] = jnp.zeros_like(zero_ref)

        @pl.loop(0, nb)
        def _(j):
            @pl.when(zero_block_ref[j] > 0)
            def _():
                block_copy(j).start()

        @pl.loop(0, nb)
        def _(j):
            @pl.when(zero_block_ref[j] > 0)
            def _():
                block_copy(j).wait()

    for i in range(tiles):
        for k in range(sub):
            pltpu.make_async_copy(x_ref.at[i, pl.ds(k, 1)],
                                  xs_ref.at[pl.ds(dest_ref[i * sub + k], 1)], sem).start()
    _wait_rows(xs_ref, xs_ref, sem, tiles * sub)


def _dispatch(x2, dest, zero_block):
    t, d = x2.shape
    tc = min(COPY_TILE, t)
    cap = zero_block.shape[0] * ROW_BLOCK
    grid_spec = pltpu.PrefetchScalarGridSpec(
        num_scalar_prefetch=1,
        grid=(t // tc,),
        in_specs=[
            pl.BlockSpec((tc,), lambda i, zb: (i,), memory_space=pltpu.SMEM),
            pl.BlockSpec((tc // SUBLANES, SUBLANES, d), lambda i, zb: (i, 0, 0)),
        ],
        out_specs=pl.BlockSpec(memory_space=pl.ANY),
        scratch_shapes=[pltpu.VMEM((ROW_BLOCK, d), jnp.float32),
                        pltpu.SemaphoreType.DMA, pltpu.SemaphoreType.DMA],
    )
    return pl.pallas_call(
        _dispatch_kernel,
        grid_spec=grid_spec,
        out_shape=jax.ShapeDtypeStruct((cap, d), jnp.float32),
        compiler_params=pltpu.CompilerParams(
            dimension_semantics=("arbitrary",), has_side_effects=True),
        name="dispatch",
    )(zero_block, dest, x2.reshape(t // SUBLANES, SUBLANES, d))


def _unsort_kernel(dest_ref, ys_ref, o_ref, sem):
    tiles, sub, _ = o_ref.shape

    for i in range(tiles):
        for k in range(sub):
            pltpu.make_async_copy(ys_ref.at[pl.ds(dest_ref[i * sub + k], 1)],
                                  o_ref.at[i, pl.ds(k, 1)], sem).start()
    _wait_rows(ys_ref, ys_ref, sem, tiles * sub)


def _unsort(ys, dest, t):
    d = ys.shape[1]
    tc = min(COPY_TILE, t)
    return pl.pallas_call(
        _unsort_kernel,
        grid=(t // tc,),
        in_specs=[
            pl.BlockSpec((tc,), lambda i: (i,), memory_space=pltpu.SMEM),
            pl.BlockSpec(memory_space=pl.ANY),
        ],
        out_specs=pl.BlockSpec((tc // SUBLANES, SUBLANES, d), lambda i: (i, 0, 0)),
        out_shape=jax.ShapeDtypeStruct((t // SUBLANES, SUBLANES, d), jnp.float32),
        scratch_shapes=[pltpu.SemaphoreType.DMA],
        compiler_params=pltpu.CompilerParams(dimension_semantics=("arbitrary",)),
        name="unsort",
    )(dest, ys).reshape(t, d)


def _expert_kernel(grp_ref, ea_ref, eb_ref, nact_ref,
                   xs_ref, gf_ref, wr_ref, br_ref, gfin_ref, wg_hbm, wu_hbm, wd_hbm,
                   o_ref, wgb, wub, wdb, sg, su, sd, sem, cnt_ref):
    s = pl.program_id(0)
    blocks = tuple(BLOCKS_PER_STEP * s + i for i in range(BLOCKS_PER_STEP))
    g = grp_ref[blocks[0]]

    def stage(e):
        slot = e % 2
        return (pltpu.make_async_copy(wg_hbm.at[e], sg.at[slot], sem.at[slot, 0]),
                pltpu.make_async_copy(wu_hbm.at[e], su.at[slot], sem.at[slot, 1]),
                pltpu.make_async_copy(wd_hbm.at[e], sd.at[slot], sem.at[slot, 2]))

    def start_next():
        @pl.when(cnt_ref[0] < N_EXPERTS)
        def _():
            for cp in stage(cnt_ref[0]):
                cp.start()
            cnt_ref[0] = cnt_ref[0] + 1

    @pl.when(s == 0)
    def _():
        cnt_ref[0] = 0
        cnt_ref[1] = 0
        start_next()
        start_next()

    active = blocks[0] < nact_ref[0]

    @pl.when(active)
    def _():
        need = g * EXPERTS_PER_GROUP + functools.reduce(
            jnp.maximum, [eb_ref[j] for j in blocks])

        def load(e, carry):
            for cp in stage(e):
                cp.wait()
            slot = e % 2
            k = e % EXPERTS_PER_GROUP
            wgb[k] = sg[slot].astype(jnp.bfloat16)
            wub[k] = su[slot].astype(jnp.bfloat16)
            wdb[k] = sd[slot].astype(jnp.bfloat16)
            cnt_ref[1] = e + 1
            start_next()
            return carry

        lax.fori_loop(cnt_ref[1], need + 1, load, 0)

        lane = lax.broadcasted_iota(jnp.int32, (ROW_BLOCK, ROUTE_LANES), 1)
        is_g = lane < N_GROUPS
        lo = EXPERT_LANE0 + EXPERTS_PER_GROUP * g
        nblk = len(blocks)
        xs = [xs_ref[i * ROW_BLOCK:(i + 1) * ROW_BLOCK, :] for i in range(nblk)]
        h3 = [_rms(x, gf_ref[...]).astype(jnp.bfloat16) for x in xs]
        lg = [jnp.dot(h, wr_ref[...], preferred_element_type=jnp.float32) + br_ref[...]
              for h in h3]

        def gates(lgi, j):
            def pick(idx):
                return jnp.sum(jnp.where(lane == idx, lgi, 0.0), axis=-1, keepdims=True)

            gmax = jnp.max(jnp.where(is_g, lgi, _NEG), axis=-1, keepdims=True)
            den = jnp.sum(jnp.where(is_g, jnp.exp(jnp.where(is_g, lgi, _NEG) - gmax), 0.0),
                          axis=-1, keepdims=True)
            grp_p = jnp.exp(pick(g) - gmax) / den
            la = pick(lo + ea_ref[j])
            lb = pick(lo + eb_ref[j])
            m = jnp.maximum(la, lb)
            pa = jnp.exp(la - m)
            pb = jnp.exp(lb - m)
            return grp_p * pa / (pa + pb), grp_p * pb / (pa + pb)

        gate = [gates(lg[i], blocks[i]) for i in range(nblk)]
        ys = list(xs)
        for which, e_ref in enumerate((ea_ref, eb_ref)):
            ks = [e_ref[j] for j in blocks]
            gg = [jnp.dot(h3[i], wgb[ks[i]], preferred_element_type=jnp.float32)
                  for i in range(nblk)]
            uu = [jnp.dot(h3[i], wub[ks[i]], preferred_element_type=jnp.float32)
                  for i in range(nblk)]
            act = [(gg[i] * (1.0 / (1.0 + jnp.exp(-gg[i]))) * uu[i]
                    * gate[i][which]).astype(jnp.bfloat16) for i in range(nblk)]
            ys = [ys[i] + jnp.dot(act[i], wdb[ks[i]], preferred_element_type=jnp.float32)
                  for i in range(nblk)]
        for i in range(nblk):
            o_ref[i * ROW_BLOCK:(i + 1) * ROW_BLOCK, :] = _rms(ys[i], gfin_ref[...])

    @pl.when(jnp.logical_not(active))
    def _():
        o_ref[...] = jnp.zeros_like(o_ref)

    @pl.when(s == pl.num_programs(0) - 1)
    def _():
        def drain(e, carry):
            for cp in stage(e):
                cp.wait()
            return carry

        lax.fori_loop(cnt_ref[1], cnt_ref[0], drain, 0)


def _experts(xs, blk_grp, blk_a, blk_b, nact, g_ffn, w_r, b_r, g_final, w_gate, w_up, w_down):
    cap, d = xs.shape
    de = w_gate.shape[2]
    step_rows = BLOCKS_PER_STEP * ROW_BLOCK
    steps = cap // step_rows
    pre = lambda f: (lambda s, gr, ea, eb, na: f(s, na))
    const2 = pre(lambda s, na: (0, 0))
    last_step = lambda na: (na[0] - 1) // BLOCKS_PER_STEP
    hbm = pl.BlockSpec(memory_space=pl.ANY)
    grid_spec = pltpu.PrefetchScalarGridSpec(
        num_scalar_prefetch=4,
        grid=(steps,),
        in_specs=[
            pl.BlockSpec((step_rows, d), pre(lambda s, na: (jnp.minimum(s, last_step(na)), 0))),
            pl.BlockSpec(g_ffn.shape, const2),
            pl.BlockSpec(w_r.shape, const2),
            pl.BlockSpec(b_r.shape, const2),
            pl.BlockSpec(g_final.shape, const2),
            hbm, hbm, hbm,
        ],
        out_specs=pl.BlockSpec((step_rows, d), pre(lambda s, na: (s, 0))),
        scratch_shapes=[
            pltpu.VMEM((EXPERTS_PER_GROUP, d, de), jnp.bfloat16),
            pltpu.VMEM((EXPERTS_PER_GROUP, d, de), jnp.bfloat16),
            pltpu.VMEM((EXPERTS_PER_GROUP, de, d), jnp.bfloat16),
            pltpu.VMEM((2, d, de), jnp.float32),
            pltpu.VMEM((2, d, de), jnp.float32),
            pltpu.VMEM((2, de, d), jnp.float32),
            pltpu.SemaphoreType.DMA((2, 3)),
            pltpu.SMEM((2,), jnp.int32),
        ],
    )
    return pl.pallas_call(
        _expert_kernel,
        grid_spec=grid_spec,
        out_shape=jax.ShapeDtypeStruct((cap, d), jnp.float32),
        compiler_params=pltpu.CompilerParams(
            dimension_semantics=("arbitrary",), vmem_limit_bytes=VMEM_LIMIT_BYTES),
        name="experts",
    )(blk_grp, blk_a, blk_b, nact, xs, g_ffn, w_r, b_r, g_final, w_gate, w_up, w_down)


def _moe_final(x2, slab, counts_col, g_ffn, w_r, b_r, g_final, w_gate, w_up, w_down):
    t, d = x2.shape
    nb = t // ROW_BLOCK + N_CLASSES + N_GROUPS * (BLOCKS_PER_STEP - 1)
    assert nb % BLOCKS_PER_STEP == 0
    counts = counts_col[:N_CLASSES, 0].astype(jnp.int32)
    nblk = (counts + ROW_BLOCK - 1) // ROW_BLOCK
    grp_blocks = jnp.sum(nblk.reshape(N_GROUPS, N_PAIRS), axis=1)
    nblk = nblk.reshape(N_GROUPS, N_PAIRS).at[:, N_PAIRS - 1].add(
        (-grp_blocks) % BLOCKS_PER_STEP).reshape(N_CLASSES)
    blk_end = jnp.cumsum(nblk)
    blk_start = blk_end - nblk
    nact = blk_end[-1]
    j = jnp.arange(nb, dtype=jnp.int32)
    blk_class = jnp.minimum(
        jnp.sum((blk_end[None, :] <= j[:, None]).astype(jnp.int32), axis=1), N_CLASSES - 1)
    blk_class = jnp.where(j < nact, blk_class, blk_class[jnp.maximum(nact - 1, 0)])
    not_full = (j - blk_start[blk_class] + 1) * ROW_BLOCK > counts[blk_class]
    zero_block = (not_full | (j >= nact)).astype(jnp.int32)
    pair_a = jnp.array([p[0] for p in PAIRS], jnp.int32)
    pair_b = jnp.array([p[1] for p in PAIRS], jnp.int32)
    blk_grp = blk_class // N_PAIRS
    blk_a = pair_a[blk_class % N_PAIRS]
    blk_b = pair_b[blk_class % N_PAIRS]
    starts_col = jnp.zeros((ROUTE_LANES, 1), jnp.float32).at[:N_CLASSES, 0].set(
        (blk_start * ROW_BLOCK).astype(jnp.float32))

    dest = _dest_rows(slab, starts_col)[0]
    xs = _dispatch(x2, dest, zero_block)
    ys = _experts(xs, blk_grp, blk_a, blk_b, nact[None], g_ffn, w_r, b_r, g_final,
                  w_gate, w_up, w_down)
    return _unsort(ys, dest, t)


def kernel(x, mem, g_mix, w_in, conv_w, g_v, w_s, b_s, g_out_conv, g_out_gmlp, w_out, g_xattn,
           g_mem, w_q, w_k, w_v, w_o, g_ffn, w_grp, b_grp, w_rt, b_rt, w_gate, w_up, w_down,
           g_final):
    b, s, d = x.shape
    assert g_mix.shape[0] == 1, "the final norm is fused into the single layer's expert kernel"
    assert N_CLASSES <= ROUTE_LANES
    bf = jnp.bfloat16
    kt, v = _kv_proj(mem, g_mem[0][None], w_k[0].astype(bf), w_v[0].astype(bf))

    bias = jnp.repeat(b_s[0].T, GMLP_HEAD_DIM, axis=1)
    x1 = _mixer(x, g_mix[0][None], w_in[0].astype(bf), conv_w[0], g_v[0][None], w_s[0], bias,
                g_out_conv[0][None], g_out_gmlp[0][None], w_out[0].astype(bf))

    pad = ROUTE_LANES - N_GROUPS - N_EXPERTS
    w_r = jnp.concatenate([w_grp[0], w_rt[0], jnp.zeros((d, pad), jnp.float32)], axis=1).astype(bf)
    b_r = jnp.concatenate([b_grp[0], b_rt[0], jnp.zeros((pad,), jnp.float32)])[None]
    gap = ROUTE_EXPERT_ROW0 - N_GROUPS
    tail = ROUTE_LANES - ROUTE_EXPERT_ROW0 - N_EXPERTS
    w_r_t = jnp.concatenate([w_grp[0].T, jnp.zeros((gap, d), jnp.float32), w_rt[0].T,
                             jnp.zeros((tail, d), jnp.float32)], axis=0).astype(bf)
    b_r_t = jnp.concatenate([b_grp[0], jnp.zeros((gap,), jnp.float32), b_rt[0],
                             jnp.zeros((tail,), jnp.float32)])[:, None]
    x2, slab, counts_col = _attn_route(x1, kt, v, g_xattn[0][None], w_q[0].astype(bf),
                                       w_o[0].astype(bf), g_ffn[0][None], w_r_t, b_r_t)
    out = _moe_final(x2.reshape(b * s, d), slab, counts_col, g_ffn[0][None], w_r, b_r,
                     g_final[None], w_gate[0], w_up[0], w_down[0])
    return out.reshape(b, s, d)
```

```python
import functools

import jax
import jax.numpy as jnp
from jax import lax
from jax.experimental import pallas as pl
from jax.experimental.pallas import tpu as pltpu

EPS = 1e-6
CONV_GROUP_WIDTH = 512
GMLP_HEADS = 8
GMLP_HEAD_DIM = 64
CHUNK = 128
CONV_K = 3
XA_HEADS = 4
N_GROUPS = 4
EXPERTS_PER_GROUP = 8
N_EXPERTS = N_GROUPS * EXPERTS_PER_GROUP
TOP_K = 2
ROUTE_LANES = 128
EXPERT_LANE0 = N_GROUPS
ROUTE_EXPERT_ROW0 = 8

N_PAIRS = EXPERTS_PER_GROUP * (EXPERTS_PER_GROUP - 1) // 2
N_CLASSES = N_GROUPS * N_PAIRS
PAIRS = [(a, b) for a in range(EXPERTS_PER_GROUP) for b in range(a + 1, EXPERTS_PER_GROUP)]

TOKEN_TILE = 1024
SUB_TILE = 256
ROW_BLOCK = 128
BLOCKS_PER_STEP = 4
COPY_TILE = 2048
VMEM_LIMIT_BYTES = 56 * 1024 * 1024

SUBLANES = 8

_NEG = -1e30


def _rms(x, g):
    return x * lax.rsqrt(jnp.mean(x * x, axis=-1, keepdims=True) + EPS) * g


def _gelu_tanh(x):
    return 0.5 * x * (1.0 + jnp.tanh(0.7978845608028654 * (x + 0.044715 * (x * x * x))))


def _kv_kernel(mem_ref, g_ref, wk_ref, wv_ref, kt_ref, v_ref):
    m = _rms(mem_ref[0], g_ref[...]).astype(jnp.bfloat16)
    k = jnp.dot(m, wk_ref[...], preferred_element_type=jnp.float32)
    v = jnp.dot(m, wv_ref[...], preferred_element_type=jnp.float32)
    kt_ref[0] = k.T.astype(jnp.bfloat16)
    v_ref[0] = v.astype(jnp.bfloat16)


def _kv_proj(mem, g_mem, w_k, w_v):
    b, m, d = mem.shape
    const = lambda i: (0, 0)
    return pl.pallas_call(
        _kv_kernel,
        grid=(b,),
        in_specs=[
            pl.BlockSpec((1, m, d), lambda i: (i, 0, 0)),
            pl.BlockSpec((1, d), const),
            pl.BlockSpec((d, d), const),
            pl.BlockSpec((d, d), const),
        ],
        out_specs=[
            pl.BlockSpec((1, d, m), lambda i: (i, 0, 0)),
            pl.BlockSpec((1, m, d), lambda i: (i, 0, 0)),
        ],
        out_shape=[
            jax.ShapeDtypeStruct((b, d, m), jnp.bfloat16),
            jax.ShapeDtypeStruct((b, m, d), jnp.bfloat16),
        ],
        compiler_params=pltpu.CompilerParams(
            dimension_semantics=("arbitrary",), vmem_limit_bytes=VMEM_LIMIT_BYTES),
        name="kv_proj",
    )(mem, g_mem, w_k, w_v)


def _mixer_kernel(x_ref, gmix_ref, win_ref, convw_ref, gv_ref, ws_ref, bias_ref,
                  goc_ref, gog_ref, wout_ref, o_ref, zbuf_ref):
    ts = x_ref.shape[1]
    w = CONV_GROUP_WIDTH
    hw = 2 * GMLP_HEAD_DIM
    nsub = ts // SUB_TILE

    def dot(a, b):
        return jnp.dot(a, b, preferred_element_type=jnp.float32)

    @pl.when(pl.program_id(1) == 0)
    def _():
        zbuf_ref[0:8, :] = jnp.zeros((8, w), jnp.float32)

    @pl.when(pl.program_id(1) != 0)
    def _():
        zbuf_ref[0:8, :] = zbuf_ref[ts:ts + 8, :]

    low = lax.broadcasted_iota(jnp.int32, (SUB_TILE, hw), 1) < GMLP_HEAD_DIM
    row = lax.broadcasted_iota(jnp.int32, (CHUNK, CHUNK), 0)
    colid = lax.broadcasted_iota(jnp.int32, (CHUNK, CHUNK), 1)
    causal = row >= colid
    ws = [jnp.where(causal, ws_ref[hd], 0.0).astype(jnp.bfloat16) for hd in range(GMLP_HEADS)]
    npair = SUB_TILE // CHUNK // 2
    lo = lax.broadcasted_iota(jnp.int32, (CHUNK, hw), 1) < GMLP_HEAD_DIM
    swap = lambda a: pltpu.roll(a, GMLP_HEAD_DIM, axis=1)

    def project(i):
        r0 = i * SUB_TILE
        xt = x_ref[0, r0:r0 + SUB_TILE, :]
        h = _rms(xt, gmix_ref[...]).astype(jnp.bfloat16)
        p = [dot(h, win_ref[:, k * w:(k + 1) * w]) for k in range(5)]
        z = p[1] * p[2]
        zbuf_ref[8 + r0:8 + r0 + SUB_TILE, :] = z
        return dict(xt=xt, gate_b=p[0], z=z, u=p[3], v=p[4], r0=r0)

    def gate_and_norm(st):
        r0 = st["r0"]
        zc = (convw_ref[0:1, :] * zbuf_ref[6 + r0:6 + r0 + SUB_TILE, :]
              + convw_ref[1:2, :] * zbuf_ref[7 + r0:7 + r0 + SUB_TILE, :]
              + convw_ref[2:3, :] * st["z"])
        st["ya"] = _rms(st["gate_b"] * zc, goc_ref[...]).astype(jnp.bfloat16)
        st["u"] = _gelu_tanh(st["u"])
        v = _gelu_tanh(st["v"])
        v2 = v * v
        ss_cols = []
        for k in range(GMLP_HEADS // 2):
            col = v2[:, hw * k:hw * (k + 1)]
            ss_cols.append(jnp.where(low,
                                     jnp.sum(jnp.where(low, col, 0.0), axis=1, keepdims=True),
                                     jnp.sum(jnp.where(low, 0.0, col), axis=1, keepdims=True)))
        ss = jnp.concatenate(ss_cols, axis=1)
        st["vn"] = v * lax.rsqrt(ss * (1.0 / GMLP_HEAD_DIM) + EPS) * gv_ref[...]

    def mix_positions(st):
        vn = st["vn"]
        s_cols = [[None] * (GMLP_HEADS // 2) for _ in range(2 * npair)]
        for hp in range(GMLP_HEADS // 2):
            cols = [vn[c * CHUNK:(c + 1) * CHUNK, hw * hp:hw * (hp + 1)] for c in range(2 * npair)]
            swapped = [swap(a) for a in cols]
            rhs_even = jnp.concatenate(
                [jnp.where(lo, cols[2 * p], swapped[2 * p + 1]) for p in range(npair)], axis=1)
            rhs_odd = jnp.concatenate(
                [jnp.where(lo, swapped[2 * p], cols[2 * p + 1]) for p in range(npair)], axis=1)
            out_e = dot(ws[2 * hp], rhs_even.astype(jnp.bfloat16))
            out_o = dot(ws[2 * hp + 1], rhs_odd.astype(jnp.bfloat16))
            for p in range(npair):
                e = out_e[:, hw * p:hw * (p + 1)]
                o = out_o[:, hw * p:hw * (p + 1)]
                s_cols[2 * p][hp] = jnp.where(lo, e, swap(o))
                s_cols[2 * p + 1][hp] = jnp.where(lo, swap(e), o)
        st["s"] = jnp.concatenate(
            [jnp.concatenate(c, axis=1) + bias_ref[...] for c in s_cols], axis=0)

    def output(st):
        r0 = st["r0"]
        yb = _rms(st["u"] * st["s"], gog_ref[...]).astype(jnp.bfloat16)
        o_ref[0, r0:r0 + SUB_TILE, :] = (st["xt"] + dot(st["ya"], wout_ref[0:w, :])
                                          + dot(yb, wout_ref[w:2 * w, :]))

    phases = (gate_and_norm, mix_positions, output)
    states = []
    for step in range(nsub + len(phases)):
        if step < nsub:
            states.append(project(step))
        for k, phase in enumerate(phases):
            i = step - 1 - k
            if 0 <= i < nsub:
                phase(states[i])


def _mixer(x, g_mix, w_in, conv_w, g_v, w_s, bias, g_oc, g_og, w_out):
    b, s, d = x.shape
    ts = min(TOKEN_TILE, s)
    const2 = lambda i, j: (0, 0)
    const3 = lambda i, j: (0, 0, 0)
    return pl.pallas_call(
        _mixer_kernel,
        grid=(b, s // ts),
        in_specs=[
            pl.BlockSpec((1, ts, d), lambda i, j: (i, j, 0)),
            pl.BlockSpec(g_mix.shape, const2),
            pl.BlockSpec(w_in.shape, const2),
            pl.BlockSpec(conv_w.shape, const2),
            pl.BlockSpec(g_v.shape, const2),
            pl.BlockSpec(w_s.shape, const3),
            pl.BlockSpec(bias.shape, const2),
            pl.BlockSpec(g_oc.shape, const2),
            pl.BlockSpec(g_og.shape, const2),
            pl.BlockSpec(w_out.shape, const2),
        ],
        out_specs=pl.BlockSpec((1, ts, d), lambda i, j: (i, j, 0)),
        out_shape=jax.ShapeDtypeStruct((b, s, d), jnp.float32),
        scratch_shapes=[pltpu.VMEM((ts + 8, CONV_GROUP_WIDTH), jnp.float32)],
        compiler_params=pltpu.CompilerParams(
            dimension_semantics=("arbitrary", "arbitrary"), vmem_limit_bytes=VMEM_LIMIT_BYTES),
        name="mixer",
    )(x, g_mix, w_in, conv_w, g_v, w_s, bias, g_oc, g_og, w_out)


def _attn_route_kernel(x_ref, kt_ref, v_ref, gx_ref, wq_ref, wo_ref, gf_ref, wrt_ref, brt_ref,
                       x2_ref, slab_ref, counts_ref, carry_ref):
    ts = x_ref.shape[1]
    d = x_ref.shape[2]
    hd = d // XA_HEADS
    nsub = ts // SUB_TILE
    first = (pl.program_id(0) == 0) & (pl.program_id(1) == 0)

    @pl.when(first)
    def _():
        carry_ref[...] = jnp.zeros_like(carry_ref)

    def dot(a, b):
        return jnp.dot(a, b, preferred_element_type=jnp.float32)

    subs = range(nsub)
    x1 = [x_ref[0, i * SUB_TILE:(i + 1) * SUB_TILE, :] for i in subs]
    h2 = [_rms(x, gx_ref[...]).astype(jnp.bfloat16) for x in x1]
    q = [dot(h, wq_ref[...]).astype(jnp.bfloat16) for h in h2]
    heads = [[] for _ in subs]
    for a in range(XA_HEADS):
        sc = [dot(q[i][:, a * hd:(a + 1) * hd], kt_ref[0, a * hd:(a + 1) * hd, :]) * (hd ** -0.5)
              for i in subs]
        p = [jnp.exp(s_ - jnp.max(s_, axis=-1, keepdims=True)) for s_ in sc]
        l = [jnp.sum(p_, axis=-1, keepdims=True) for p_ in p]
        o = [dot(p[i].astype(jnp.bfloat16), v_ref[0, :, a * hd:(a + 1) * hd]) for i in subs]
        for i in subs:
            heads[i].append((o[i] / l[i]).astype(jnp.bfloat16))
    x2 = [x1[i] + dot(jnp.concatenate(heads[i], axis=1), wo_ref[...]) for i in subs]
    for i in subs:
        x2_ref[0, i * SUB_TILE:(i + 1) * SUB_TILE, :] = x2[i]

    h3 = [_rms(x, gf_ref[...]).astype(jnp.bfloat16) for x in x2]
    lgs = [lax.dot_general(wrt_ref[...], h, (((1,), (1,)), ((), ())),
                           preferred_element_type=jnp.float32) + brt_ref[...] for h in h3]
    sub = lax.broadcasted_iota(jnp.int32, (EXPERTS_PER_GROUP, SUB_TILE), 0).astype(jnp.float32)
    big = float(EXPERTS_PER_GROUP)
    is_g = sub < N_GROUPS

    def classify(lg):
        glog = lg[0:EXPERTS_PER_GROUP, :]
        gmax = jnp.max(jnp.where(is_g, glog, _NEG), axis=0, keepdims=True)
        gidx = jnp.min(jnp.where(is_g & (glog == gmax), sub, big), axis=0, keepdims=True)
        el = lg[ROUTE_EXPERT_ROW0:ROUTE_EXPERT_ROW0 + EXPERTS_PER_GROUP, :]
        for grp in range(1, N_GROUPS):
            r0 = ROUTE_EXPERT_ROW0 + grp * EXPERTS_PER_GROUP
            el = jnp.where(gidx == grp, lg[r0:r0 + EXPERTS_PER_GROUP, :], el)
        t1 = jnp.max(el, axis=0, keepdims=True)
        i1 = jnp.min(jnp.where(el == t1, sub, big), axis=0, keepdims=True)
        rest = sub != i1
        t2 = jnp.max(jnp.where(rest, el, _NEG), axis=0, keepdims=True)
        i2 = jnp.min(jnp.where(rest & (el == t2), sub, big), axis=0, keepdims=True)
        a = jnp.minimum(i1, i2)
        b = jnp.maximum(i1, i2)
        pair = a * (2 * EXPERTS_PER_GROUP - 1 - a) * 0.5 + (b - a - 1.0)
        return gidx * N_PAIRS + pair

    cls = [classify(lg) for lg in lgs]

    r = lax.broadcasted_iota(jnp.int32, (SUB_TILE, SUB_TILE), 0)
    c = lax.broadcasted_iota(jnp.int32, (SUB_TILE, SUB_TILE), 1)
    earlier = jnp.where(r < c, 1.0, 0.0).astype(jnp.bfloat16)
    crow = lax.broadcasted_iota(jnp.int32, (ROUTE_LANES, SUB_TILE), 0).astype(jnp.float32)
    oh = [crow == cl for cl in cls]
    ohf = [jnp.where(o_, 1.0, 0.0) for o_ in oh]
    inside = [dot(o_.astype(jnp.bfloat16), earlier) for o_ in ohf]
    seen = carry_ref[...]
    for i in subs:
        rank = jnp.sum(jnp.where(oh[i], inside[i] + seen, 0.0), axis=0, keepdims=True)
        slab_ref[:, i * SUB_TILE:(i + 1) * SUB_TILE] = jnp.where(
            sub == 0, cls[i], jnp.where(sub == 1, rank, 0.0))
        seen = seen + jnp.sum(ohf[i], axis=1, keepdims=True)
    carry_ref[...] = seen
    counts_ref[...] = seen


def _attn_route(x1, kt, v, g_x, w_q, w_o, g_f, w_rt, b_rt):
    b, s, d = x1.shape
    m = v.shape[1]
    ts = min(TOKEN_TILE, s)
    nt = s // ts
    const2 = lambda i, j: (0, 0)
    return pl.pallas_call(
        _attn_route_kernel,
        grid=(b, nt),
        in_specs=[
            pl.BlockSpec((1, ts, d), lambda i, j: (i, j, 0)),
            pl.BlockSpec((1, d, m), lambda i, j: (i, 0, 0)),
            pl.BlockSpec((1, m, d), lambda i, j: (i, 0, 0)),
            pl.BlockSpec(g_x.shape, const2),
            pl.BlockSpec(w_q.shape, const2),
            pl.BlockSpec(w_o.shape, const2),
            pl.BlockSpec(g_f.shape, const2),
            pl.BlockSpec(w_rt.shape, const2),
            pl.BlockSpec(b_rt.shape, const2),
        ],
        out_specs=[
            pl.BlockSpec((1, ts, d), lambda i, j: (i, j, 0)),
            pl.BlockSpec((SUBLANES, ts), lambda i, j: (0, i * nt + j)),
            pl.BlockSpec((ROUTE_LANES, 1), const2),
        ],
        out_shape=[
            jax.ShapeDtypeStruct((b, s, d), jnp.float32),
            jax.ShapeDtypeStruct((SUBLANES, b * s), jnp.float32),
            jax.ShapeDtypeStruct((ROUTE_LANES, 1), jnp.float32),
        ],
        scratch_shapes=[pltpu.VMEM((ROUTE_LANES, 1), jnp.float32)],
        compiler_params=pltpu.CompilerParams(
            dimension_semantics=("arbitrary", "arbitrary"), vmem_limit_bytes=VMEM_LIMIT_BYTES),
        name="attn_route",
    )(x1, kt, v, g_x, w_q, w_o, g_f, w_rt, b_rt)


def _dest_kernel(slab_ref, starts_ref, o_ref):
    slab = slab_ref[...]
    tt = slab.shape[1]
    crow = lax.broadcasted_iota(jnp.int32, (ROUTE_LANES, tt), 0).astype(jnp.float32)
    start = jnp.sum(jnp.where(crow == slab[0:1, :], starts_ref[...], 0.0), axis=0, keepdims=True)
    sub = lax.broadcasted_iota(jnp.int32, slab.shape, 0)
    o_ref[...] = jnp.where(sub == 0, start + slab[1:2, :], 0.0).astype(jnp.int32)


def _dest_rows(slab, starts_col):
    t = slab.shape[1]
    tt = min(2048, t)
    return pl.pallas_call(
        _dest_kernel,
        grid=(t // tt,),
        in_specs=[pl.BlockSpec((SUBLANES, tt), lambda i: (0, i)),
                  pl.BlockSpec((ROUTE_LANES, 1), lambda i: (0, 0))],
        out_specs=pl.BlockSpec((SUBLANES, tt), lambda i: (0, i)),
        out_shape=jax.ShapeDtypeStruct((SUBLANES, t), jnp.int32),
        compiler_params=pltpu.CompilerParams(dimension_semantics=("arbitrary",)),
        name="dest_rows",
    )(slab, starts_col)


def _wait_rows(src_ref, dst_ref, sem, n):
    pltpu.make_async_copy(src_ref.at[pl.ds(0, n)], dst_ref.at[pl.ds(0, n)], sem).wait()


def _dispatch_kernel(zero_block_ref, dest_ref, x_ref, xs_ref, zero_ref, sem, zsem):
    tiles, sub, _ = x_ref.shape
    nb = zero_block_ref.shape[0]

    def block_copy(j):
        start = pl.multiple_of(j * ROW_BLOCK, ROW_BLOCK)
        return pltpu.make_async_copy(zero_ref, xs_ref.at[pl.ds(start, ROW_BLOCK)], zsem)

    @pl.when(pl.program_id(0) == 0)
    def _():
        zero_ref[...] = jnp.zeros_like(zero_ref)

        @pl.loop(0, nb)
        def _(j):
            @pl.when(zero_block_ref[j] > 0)
            def _():
                block_copy(j).start()

        @pl.loop(0, nb)
        def _(j):
            @pl.when(zero_block_ref[j] > 0)
            def _():
                block_copy(j).wait()

    for i in range(tiles):
        for k in range(sub):
            pltpu.make_async_copy(x_ref.at[i, pl.ds(k, 1)],
                                  xs_ref.at[pl.ds(dest_ref[i * sub + k], 1)], sem).start(
                                      priority=k % 2)
    _wait_rows(xs_ref, xs_ref, sem, tiles * sub)


def _dispatch(x2, dest, zero_block):
    t, d = x2.shape
    tc = min(COPY_TILE, t)
    cap = zero_block.shape[0] * ROW_BLOCK
    grid_spec = pltpu.PrefetchScalarGridSpec(
        num_scalar_prefetch=1,
        grid=(t // tc,),
        in_specs=[
            pl.BlockSpec((tc,), lambda i, zb: (i,), memory_space=pltpu.SMEM),
            pl.BlockSpec((tc // SUBLANES, SUBLANES, d), lambda i, zb: (i, 0, 0)),
        ],
        out_specs=pl.BlockSpec(memory_space=pl.ANY),
        scratch_shapes=[pltpu.VMEM((ROW_BLOCK, d), jnp.float32),
                        pltpu.SemaphoreType.DMA, pltpu.SemaphoreType.DMA],
    )
    return pl.pallas_call(
        _dispatch_kernel,
        grid_spec=grid_spec,
        out_shape=jax.ShapeDtypeStruct((cap, d), jnp.float32),
        compiler_params=pltpu.CompilerParams(
            dimension_semantics=("arbitrary",), has_side_effects=True),
        name="dispatch",
    )(zero_block, dest, x2.reshape(t // SUBLANES, SUBLANES, d))


def _unsort_kernel(dest_ref, ys_ref, o_ref, sem):
    tiles, sub, _ = o_ref.shape

    for i in range(tiles):
        for k in range(sub):
            pltpu.make_async_copy(ys_ref.at[pl.ds(dest_ref[i * sub + k], 1)],
                                  o_ref.at[i, pl.ds(k, 1)], sem).start(priority=k % 2)
    _wait_rows(ys_ref, ys_ref, sem, tiles * sub)


def _unsort(ys, dest, t):
    d = ys.shape[1]
    tc = min(COPY_TILE, t)
    return pl.pallas_call(
        _unsort_kernel,
        grid=(t // tc,),
        in_specs=[
            pl.BlockSpec((tc,), lambda i: (i,), memory_space=pltpu.SMEM),
            pl.BlockSpec(memory_space=pl.ANY),
        ],
        out_specs=pl.BlockSpec((tc // SUBLANES, SUBLANES, d), lambda i: (i, 0, 0)),
        out_shape=jax.ShapeDtypeStruct((t // SUBLANES, SUBLANES, d), jnp.float32),
        scratch_shapes=[pltpu.SemaphoreType.DMA],
        compiler_params=pltpu.CompilerParams(dimension_semantics=("arbitrary",)),
        name="unsort",
    )(dest, ys).reshape(t, d)


def _expert_kernel(grp_ref, ea_ref, eb_ref, nact_ref,
                   xs_ref, gf_ref, wr_ref, br_ref, gfin_ref, wg_hbm, wu_hbm, wd_hbm,
                   o_ref, wgb, wub, wdb, sg, su, sd, sem, cnt_ref):
    s = pl.program_id(0)
    blocks = tuple(BLOCKS_PER_STEP * s + i for i in range(BLOCKS_PER_STEP))
    g = grp_ref[blocks[0]]

    def stage(e):
        slot = e % 2
        return (pltpu.make_async_copy(wg_hbm.at[e], sg.at[slot], sem.at[slot, 0]),
                pltpu.make_async_copy(wu_hbm.at[e], su.at[slot], sem.at[slot, 1]),
                pltpu.make_async_copy(wd_hbm.at[e], sd.at[slot], sem.at[slot, 2]))

    def start_next():
        @pl.when(cnt_ref[0] < N_EXPERTS)
        def _():
            for cp in stage(cnt_ref[0]):
                cp.start()
            cnt_ref[0] = cnt_ref[0] + 1

    @pl.when(s == 0)
    def _():
        cnt_ref[0] = 0
        cnt_ref[1] = 0
        start_next()
        start_next()

    active = blocks[0] < nact_ref[0]

    @pl.when(active)
    def _():
        need = g * EXPERTS_PER_GROUP + functools.reduce(
            jnp.maximum, [eb_ref[j] for j in blocks])

        def load(e, carry):
            for cp in stage(e):
                cp.wait()
            slot = e % 2
            k = e % EXPERTS_PER_GROUP
            wgb[k] = sg[slot].astype(jnp.bfloat16)
            wub[k] = su[slot].astype(jnp.bfloat16)
            wdb[k] = sd[slot].astype(jnp.bfloat16)
            cnt_ref[1] = e + 1
            start_next()
            return carry

        lax.fori_loop(cnt_ref[1], need + 1, load, 0)

        lane = lax.broadcasted_iota(jnp.int32, (ROW_BLOCK, ROUTE_LANES), 1)
        is_g = lane < N_GROUPS
        lo = EXPERT_LANE0 + EXPERTS_PER_GROUP * g
        nblk = len(blocks)
        xs = [xs_ref[i * ROW_BLOCK:(i + 1) * ROW_BLOCK, :] for i in range(nblk)]
        h3 = [_rms(x, gf_ref[...]).astype(jnp.bfloat16) for x in xs]
        lg = [jnp.dot(h, wr_ref[...], preferred_element_type=jnp.float32) + br_ref[...]
              for h in h3]

        def gates(lgi, j):
            def pick(idx):
                return jnp.sum(jnp.where(lane == idx, lgi, 0.0), axis=-1, keepdims=True)

            gmax = jnp.max(jnp.where(is_g, lgi, _NEG), axis=-1, keepdims=True)
            den = jnp.sum(jnp.where(is_g, jnp.exp(jnp.where(is_g, lgi, _NEG) - gmax), 0.0),
                          axis=-1, keepdims=True)
            grp_p = jnp.exp(pick(g) - gmax) / den
            la = pick(lo + ea_ref[j])
            lb = pick(lo + eb_ref[j])
            m = jnp.maximum(la, lb)
            pa = jnp.exp(la - m)
            pb = jnp.exp(lb - m)
            return grp_p * pa / (pa + pb), grp_p * pb / (pa + pb)

        gate = [gates(lg[i], blocks[i]) for i in range(nblk)]
        ys = list(xs)
        for which, e_ref in enumerate((ea_ref, eb_ref)):
            ks = [e_ref[j] for j in blocks]
            gg = [jnp.dot(h3[i], wgb[ks[i]], preferred_element_type=jnp.float32)
                  for i in range(nblk)]
            uu = [jnp.dot(h3[i], wub[ks[i]], preferred_element_type=jnp.float32)
                  for i in range(nblk)]
            act = [(gg[i] * (1.0 / (1.0 + jnp.exp(-gg[i]))) * uu[i]
                    * gate[i][which]).astype(jnp.bfloat16) for i in range(nblk)]
            ys = [ys[i] + jnp.dot(act[i], wdb[ks[i]], preferred_element_type=jnp.float32)
                  for i in range(nblk)]
        for i in range(nblk):
            o_ref[i * ROW_BLOCK:(i + 1) * ROW_BLOCK, :] = _rms(ys[i], gfin_ref[...])

    @pl.when(jnp.logical_not(active))
    def _():
        o_ref[...] = jnp.zeros_like(o_ref)

    @pl.when(s == pl.num_programs(0) - 1)
    def _():
        def drain(e, carry):
            for cp in stage(e):
                cp.wait()
            return carry

        lax.fori_loop(cnt_ref[1], cnt_ref[0], drain, 0)


def _experts(xs, blk_grp, blk_a, blk_b, nact, g_ffn, w_r, b_r, g_final, w_gate, w_up, w_down):
    cap, d = xs.shape
    de = w_gate.shape[2]
    step_rows = BLOCKS_PER_STEP * ROW_BLOCK
    steps = cap // step_rows
    pre = lambda f: (lambda s, gr, ea, eb, na: f(s, na))
    const2 = pre(lambda s, na: (0, 0))
    last_step = lambda na: (na[0] - 1) // BLOCKS_PER_STEP
    hbm = pl.BlockSpec(memory_space=pl.ANY)
    grid_spec = pltpu.PrefetchScalarGridSpec(
        num_scalar_prefetch=4,
        grid=(steps,),
        in_specs=[
            pl.BlockSpec((step_rows, d), pre(lambda s, na: (jnp.minimum(s, last_step(na)), 0))),
            pl.BlockSpec(g_ffn.shape, const2),
            pl.BlockSpec(w_r.shape, const2),
            pl.BlockSpec(b_r.shape, const2),
            pl.BlockSpec(g_final.shape, const2),
            hbm, hbm, hbm,
        ],
        out_specs=pl.BlockSpec((step_rows, d), pre(lambda s, na: (s, 0))),
        scratch_shapes=[
            pltpu.VMEM((EXPERTS_PER_GROUP, d, de), jnp.bfloat16),
            pltpu.VMEM((EXPERTS_PER_GROUP, d, de), jnp.bfloat16),
            pltpu.VMEM((EXPERTS_PER_GROUP, de, d), jnp.bfloat16),
            pltpu.VMEM((2, d, de), jnp.float32),
            pltpu.VMEM((2, d, de), jnp.float32),
            pltpu.VMEM((2, de, d), jnp.float32),
            pltpu.SemaphoreType.DMA((2, 3)),
            pltpu.SMEM((2,), jnp.int32),
        ],
    )
    return pl.pallas_call(
        _expert_kernel,
        grid_spec=grid_spec,
        out_shape=jax.ShapeDtypeStruct((cap, d), jnp.float32),
        compiler_params=pltpu.CompilerParams(
            dimension_semantics=("arbitrary",), vmem_limit_bytes=VMEM_LIMIT_BYTES),
        name="experts",
    )(blk_grp, blk_a, blk_b, nact, xs, g_ffn, w_r, b_r, g_final, w_gate, w_up, w_down)


def _moe_final(x2, slab, counts_col, g_ffn, w_r, b_r, g_final, w_gate, w_up, w_down):
    t, d = x2.shape
    nb = t // ROW_BLOCK + N_CLASSES + N_GROUPS * (BLOCKS_PER_STEP - 1)
    assert nb % BLOCKS_PER_STEP == 0
    counts = counts_col[:N_CLASSES, 0].astype(jnp.int32)
    nblk = (counts + ROW_BLOCK - 1) // ROW_BLOCK
    grp_blocks = jnp.sum(nblk.reshape(N_GROUPS, N_PAIRS), axis=1)
    nblk = nblk.reshape(N_GROUPS, N_PAIRS).at[:, N_PAIRS - 1].add(
        (-grp_blocks) % BLOCKS_PER_STEP).reshape(N_CLASSES)
    blk_end = jnp.cumsum(nblk)
    blk_start = blk_end - nblk
    nact = blk_end[-1]
    j = jnp.arange(nb, dtype=jnp.int32)
    blk_class = jnp.minimum(
        jnp.sum((blk_end[None, :] <= j[:, None]).astype(jnp.int32), axis=1), N_CLASSES - 1)
    blk_class = jnp.where(j < nact, blk_class, blk_class[jnp.maximum(nact - 1, 0)])
    not_full = (j - blk_start[blk_class] + 1) * ROW_BLOCK > counts[blk_class]
    zero_block = (not_full | (j >= nact)).astype(jnp.int32)
    pair_a = jnp.array([p[0] for p in PAIRS], jnp.int32)
    pair_b = jnp.array([p[1] for p in PAIRS], jnp.int32)
    blk_grp = blk_class // N_PAIRS
    blk_a = pair_a[blk_class % N_PAIRS]
    blk_b = pair_b[blk_class % N_PAIRS]
    starts_col = jnp.zeros((ROUTE_LANES, 1), jnp.float32).at[:N_CLASSES, 0].set(
        (blk_start * ROW_BLOCK).astype(jnp.float32))

    dest = _dest_rows(slab, starts_col)[0]
    xs = _dispatch(x2, dest, zero_block)
    ys = _experts(xs, blk_grp, blk_a, blk_b, nact[None], g_ffn, w_r, b_r, g_final,
                  w_gate, w_up, w_down)
    return _unsort(ys, dest, t)


def kernel(x, mem, g_mix, w_in, conv_w, g_v, w_s, b_s, g_out_conv, g_out_gmlp, w_out, g_xattn,
           g_mem, w_q, w_k, w_v, w_o, g_ffn, w_grp, b_grp, w_rt, b_rt, w_gate, w_up, w_down,
           g_final):
    b, s, d = x.shape
    assert g_mix.shape[0] == 1, "the final norm is fused into the single layer's expert kernel"
    assert N_CLASSES <= ROUTE_LANES
    bf = jnp.bfloat16
    kt, v = _kv_proj(mem, g_mem[0][None], w_k[0].astype(bf), w_v[0].astype(bf))

    bias = jnp.repeat(b_s[0].T, GMLP_HEAD_DIM, axis=1)
    x1 = _mixer(x, g_mix[0][None], w_in[0].astype(bf), conv_w[0], g_v[0][None], w_s[0], bias,
                g_out_conv[0][None], g_out_gmlp[0][None], w_out[0].astype(bf))

    pad = ROUTE_LANES - N_GROUPS - N_EXPERTS
    w_r = jnp.concatenate([w_grp[0], w_rt[0], jnp.zeros((d, pad), jnp.float32)], axis=1).astype(bf)
    b_r = jnp.concatenate([b_grp[0], b_rt[0], jnp.zeros((pad,), jnp.float32)])[None]
    gap = ROUTE_EXPERT_ROW0 - N_GROUPS
    tail = ROUTE_LANES - ROUTE_EXPERT_ROW0 - N_EXPERTS
    w_r_t = jnp.concatenate([w_grp[0].T, jnp.zeros((gap, d), jnp.float32), w_rt[0].T,
                             jnp.zeros((tail, d), jnp.float32)], axis=0).astype(bf)
    b_r_t = jnp.concatenate([b_grp[0], jnp.zeros((gap,), jnp.float32), b_rt[0],
                             jnp.zeros((tail,), jnp.float32)])[:, None]
    x2, slab, counts_col = _attn_route(x1, kt, v, g_xattn[0][None], w_q[0].astype(bf),
                                       w_o[0].astype(bf), g_ffn[0][None], w_r_t, b_r_t)
    out = _moe_final(x2.reshape(b * s, d), slab, counts_col, g_ffn[0][None], w_r, b_r,
                     g_final[None], w_gate[0], w_up[0], w_down[0])
    return out.reshape(b, s, d)
```

```python
import functools

import jax
import jax.numpy as jnp
from jax import lax
from jax.experimental import pallas as pl
from jax.experimental.pallas import tpu as pltpu
from jax.experimental.pallas import tpu_sc as plsc

EPS = 1e-6
CONV_GROUP_WIDTH = 512
GMLP_HEADS = 8
GMLP_HEAD_DIM = 64
CHUNK = 128
CONV_K = 3
XA_HEADS = 4
N_GROUPS = 4
EXPERTS_PER_GROUP = 8
N_EXPERTS = N_GROUPS * EXPERTS_PER_GROUP
TOP_K = 2
ROUTE_LANES = 128
EXPERT_LANE0 = N_GROUPS
ROUTE_EXPERT_ROW0 = 8

N_PAIRS = EXPERTS_PER_GROUP * (EXPERTS_PER_GROUP - 1) // 2
N_CLASSES = N_GROUPS * N_PAIRS
PAIRS = [(a, b) for a in range(EXPERTS_PER_GROUP) for b in range(a + 1, EXPERTS_PER_GROUP)]

TOKEN_TILE = 1024
SUB_TILE = 256
ROW_BLOCK = 128
BLOCKS_PER_STEP = 4
COPY_TILE = 2048
SC_GATHER_ROWS = 64
VMEM_LIMIT_BYTES = 56 * 1024 * 1024

SUBLANES = 8

_NEG = -1e30


def _rms(x, g):
    return x * lax.rsqrt(jnp.mean(x * x, axis=-1, keepdims=True) + EPS) * g


def _gelu_tanh(x):
    return 0.5 * x * (1.0 + jnp.tanh(0.7978845608028654 * (x + 0.044715 * (x * x * x))))


def _kv_kernel(mem_ref, g_ref, wk_ref, wv_ref, kt_ref, v_ref):
    m = _rms(mem_ref[0], g_ref[...]).astype(jnp.bfloat16)
    k = jnp.dot(m, wk_ref[...], preferred_element_type=jnp.float32)
    v = jnp.dot(m, wv_ref[...], preferred_element_type=jnp.float32)
    kt_ref[0] = k.T.astype(jnp.bfloat16)
    v_ref[0] = v.astype(jnp.bfloat16)


def _kv_proj(mem, g_mem, w_k, w_v):
    b, m, d = mem.shape
    const = lambda i: (0, 0)
    return pl.pallas_call(
        _kv_kernel,
        grid=(b,),
        in_specs=[
            pl.BlockSpec((1, m, d), lambda i: (i, 0, 0)),
            pl.BlockSpec((1, d), const),
            pl.BlockSpec((d, d), const),
            pl.BlockSpec((d, d), const),
        ],
        out_specs=[
            pl.BlockSpec((1, d, m), lambda i: (i, 0, 0)),
            pl.BlockSpec((1, m, d), lambda i: (i, 0, 0)),
        ],
        out_shape=[
            jax.ShapeDtypeStruct((b, d, m), jnp.bfloat16),
            jax.ShapeDtypeStruct((b, m, d), jnp.bfloat16),
        ],
        compiler_params=pltpu.CompilerParams(
            dimension_semantics=("arbitrary",), vmem_limit_bytes=VMEM_LIMIT_BYTES),
        name="kv_proj",
    )(mem, g_mem, w_k, w_v)


def _mixer_kernel(x_ref, gmix_ref, win_ref, convw_ref, gv_ref, ws_ref, bias_ref,
                  goc_ref, gog_ref, wout_ref, o_ref, zbuf_ref):
    ts = x_ref.shape[1]
    w = CONV_GROUP_WIDTH
    hw = 2 * GMLP_HEAD_DIM
    nsub = ts // SUB_TILE

    def dot(a, b):
        return jnp.dot(a, b, preferred_element_type=jnp.float32)

    @pl.when(pl.program_id(1) == 0)
    def _():
        zbuf_ref[0:8, :] = jnp.zeros((8, w), jnp.float32)

    @pl.when(pl.program_id(1) != 0)
    def _():
        zbuf_ref[0:8, :] = zbuf_ref[ts:ts + 8, :]

    low = lax.broadcasted_iota(jnp.int32, (SUB_TILE, hw), 1) < GMLP_HEAD_DIM
    row = lax.broadcasted_iota(jnp.int32, (CHUNK, CHUNK), 0)
    colid = lax.broadcasted_iota(jnp.int32, (CHUNK, CHUNK), 1)
    causal = row >= colid
    ws = [jnp.where(causal, ws_ref[hd], 0.0).astype(jnp.bfloat16) for hd in range(GMLP_HEADS)]
    npair = SUB_TILE // CHUNK // 2
    lo = lax.broadcasted_iota(jnp.int32, (CHUNK, hw), 1) < GMLP_HEAD_DIM
    swap = lambda a: pltpu.roll(a, GMLP_HEAD_DIM, axis=1)

    def project(i):
        r0 = i * SUB_TILE
        xt = x_ref[0, r0:r0 + SUB_TILE, :]
        h = _rms(xt, gmix_ref[...]).astype(jnp.bfloat16)
        p = [dot(h, win_ref[:, k * w:(k + 1) * w]) for k in range(5)]
        z = p[1] * p[2]
        zbuf_ref[8 + r0:8 + r0 + SUB_TILE, :] = z
        return dict(xt=xt, gate_b=p[0], z=z, u=p[3], v=p[4], r0=r0)

    def gate_and_norm(st):
        r0 = st["r0"]
        zc = (convw_ref[0:1, :] * zbuf_ref[6 + r0:6 + r0 + SUB_TILE, :]
              + convw_ref[1:2, :] * zbuf_ref[7 + r0:7 + r0 + SUB_TILE, :]
              + convw_ref[2:3, :] * st["z"])
        st["ya"] = _rms(st["gate_b"] * zc, goc_ref[...]).astype(jnp.bfloat16)
        st["u"] = _gelu_tanh(st["u"])
        v = _gelu_tanh(st["v"])
        v2 = v * v
        ss_cols = []
        for k in range(GMLP_HEADS // 2):
            col = v2[:, hw * k:hw * (k + 1)]
            ss_cols.append(jnp.where(low,
                                     jnp.sum(jnp.where(low, col, 0.0), axis=1, keepdims=True),
                                     jnp.sum(jnp.where(low, 0.0, col), axis=1, keepdims=True)))
        ss = jnp.concatenate(ss_cols, axis=1)
        st["vn"] = v * lax.rsqrt(ss * (1.0 / GMLP_HEAD_DIM) + EPS) * gv_ref[...]

    def mix_positions(st):
        vn = st["vn"]
        s_cols = [[None] * (GMLP_HEADS // 2) for _ in range(2 * npair)]
        for hp in range(GMLP_HEADS // 2):
            cols = [vn[c * CHUNK:(c + 1) * CHUNK, hw * hp:hw * (hp + 1)] for c in range(2 * npair)]
            swapped = [swap(a) for a in cols]
            rhs_even = jnp.concatenate(
                [jnp.where(lo, cols[2 * p], swapped[2 * p + 1]) for p in range(npair)], axis=1)
            rhs_odd = jnp.concatenate(
                [jnp.where(lo, swapped[2 * p], cols[2 * p + 1]) for p in range(npair)], axis=1)
            out_e = dot(ws[2 * hp], rhs_even.astype(jnp.bfloat16))
            out_o = dot(ws[2 * hp + 1], rhs_odd.astype(jnp.bfloat16))
            for p in range(npair):
                e = out_e[:, hw * p:hw * (p + 1)]
                o = out_o[:, hw * p:hw * (p + 1)]
                s_cols[2 * p][hp] = jnp.where(lo, e, swap(o))
                s_cols[2 * p + 1][hp] = jnp.where(lo, swap(e), o)
        st["s"] = jnp.concatenate(
            [jnp.concatenate(c, axis=1) + bias_ref[...] for c in s_cols], axis=0)

    def output(st):
        r0 = st["r0"]
        yb = _rms(st["u"] * st["s"], gog_ref[...]).astype(jnp.bfloat16)
        o_ref[0, r0:r0 + SUB_TILE, :] = (st["xt"] + dot(st["ya"], wout_ref[0:w, :])
                                          + dot(yb, wout_ref[w:2 * w, :]))

    phases = (gate_and_norm, mix_positions, output)
    states = []
    for step in range(nsub + len(phases)):
        if step < nsub:
            states.append(project(step))
        for k, phase in enumerate(phases):
            i = step - 1 - k
            if 0 <= i < nsub:
                phase(states[i])


def _mixer(x, g_mix, w_in, conv_w, g_v, w_s, bias, g_oc, g_og, w_out):
    b, s, d = x.shape
    ts = min(TOKEN_TILE, s)
    const2 = lambda i, j: (0, 0)
    const3 = lambda i, j: (0, 0, 0)
    return pl.pallas_call(
        _mixer_kernel,
        grid=(b, s // ts),
        in_specs=[
            pl.BlockSpec((1, ts, d), lambda i, j: (i, j, 0)),
            pl.BlockSpec(g_mix.shape, const2),
            pl.BlockSpec(w_in.shape, const2),
            pl.BlockSpec(conv_w.shape, const2),
            pl.BlockSpec(g_v.shape, const2),
            pl.BlockSpec(w_s.shape, const3),
            pl.BlockSpec(bias.shape, const2),
            pl.BlockSpec(g_oc.shape, const2),
            pl.BlockSpec(g_og.shape, const2),
            pl.BlockSpec(w_out.shape, const2),
        ],
        out_specs=pl.BlockSpec((1, ts, d), lambda i, j: (i, j, 0)),
        out_shape=jax.ShapeDtypeStruct((b, s, d), jnp.float32),
        scratch_shapes=[pltpu.VMEM((ts + 8, CONV_GROUP_WIDTH), jnp.float32)],
        compiler_params=pltpu.CompilerParams(
            dimension_semantics=("arbitrary", "arbitrary"), vmem_limit_bytes=VMEM_LIMIT_BYTES),
        name="mixer",
    )(x, g_mix, w_in, conv_w, g_v, w_s, bias, g_oc, g_og, w_out)


def _attn_route_kernel(x_ref, kt_ref, v_ref, gx_ref, wq_ref, wo_ref, gf_ref, wrt_ref, brt_ref,
                       x2_ref, slab_ref, counts_ref, carry_ref):
    ts = x_ref.shape[1]
    d = x_ref.shape[2]
    hd = d // XA_HEADS
    nsub = ts // SUB_TILE
    first = (pl.program_id(0) == 0) & (pl.program_id(1) == 0)

    @pl.when(first)
    def _():
        carry_ref[...] = jnp.zeros_like(carry_ref)

    def dot(a, b):
        return jnp.dot(a, b, preferred_element_type=jnp.float32)

    subs = range(nsub)
    x1 = [x_ref[0, i * SUB_TILE:(i + 1) * SUB_TILE, :] for i in subs]
    h2 = [_rms(x, gx_ref[...]).astype(jnp.bfloat16) for x in x1]
    q = [dot(h, wq_ref[...]).astype(jnp.bfloat16) for h in h2]
    heads = [[] for _ in subs]
    for a in range(XA_HEADS):
        sc = [dot(q[i][:, a * hd:(a + 1) * hd], kt_ref[0, a * hd:(a + 1) * hd, :]) * (hd ** -0.5)
              for i in subs]
        p = [jnp.exp(s_ - jnp.max(s_, axis=-1, keepdims=True)) for s_ in sc]
        l = [jnp.sum(p_, axis=-1, keepdims=True) for p_ in p]
        o = [dot(p[i].astype(jnp.bfloat16), v_ref[0, :, a * hd:(a + 1) * hd]) for i in subs]
        for i in subs:
            heads[i].append((o[i] / l[i]).astype(jnp.bfloat16))
    x2 = [x1[i] + dot(jnp.concatenate(heads[i], axis=1), wo_ref[...]) for i in subs]
    for i in subs:
        x2_ref[0, i * SUB_TILE:(i + 1) * SUB_TILE, :] = x2[i]

    h3 = [_rms(x, gf_ref[...]).astype(jnp.bfloat16) for x in x2]
    lgs = [lax.dot_general(wrt_ref[...], h, (((1,), (1,)), ((), ())),
                           preferred_element_type=jnp.float32) + brt_ref[...] for h in h3]
    sub = lax.broadcasted_iota(jnp.int32, (EXPERTS_PER_GROUP, SUB_TILE), 0).astype(jnp.float32)
    big = float(EXPERTS_PER_GROUP)
    is_g = sub < N_GROUPS

    def classify(lg):
        glog = lg[0:EXPERTS_PER_GROUP, :]
        gmax = jnp.max(jnp.where(is_g, glog, _NEG), axis=0, keepdims=True)
        gidx = jnp.min(jnp.where(is_g & (glog == gmax), sub, big), axis=0, keepdims=True)
        el = lg[ROUTE_EXPERT_ROW0:ROUTE_EXPERT_ROW0 + EXPERTS_PER_GROUP, :]
        for grp in range(1, N_GROUPS):
            r0 = ROUTE_EXPERT_ROW0 + grp * EXPERTS_PER_GROUP
            el = jnp.where(gidx == grp, lg[r0:r0 + EXPERTS_PER_GROUP, :], el)
        t1 = jnp.max(el, axis=0, keepdims=True)
        i1 = jnp.min(jnp.where(el == t1, sub, big), axis=0, keepdims=True)
        rest = sub != i1
        t2 = jnp.max(jnp.where(rest, el, _NEG), axis=0, keepdims=True)
        i2 = jnp.min(jnp.where(rest & (el == t2), sub, big), axis=0, keepdims=True)
        a = jnp.minimum(i1, i2)
        b = jnp.maximum(i1, i2)
        pair = a * (2 * EXPERTS_PER_GROUP - 1 - a) * 0.5 + (b - a - 1.0)
        return gidx * N_PAIRS + pair

    cls = [classify(lg) for lg in lgs]

    r = lax.broadcasted_iota(jnp.int32, (SUB_TILE, SUB_TILE), 0)
    c = lax.broadcasted_iota(jnp.int32, (SUB_TILE, SUB_TILE), 1)
    earlier = jnp.where(r < c, 1.0, 0.0).astype(jnp.bfloat16)
    crow = lax.broadcasted_iota(jnp.int32, (ROUTE_LANES, SUB_TILE), 0).astype(jnp.float32)
    oh = [crow == cl for cl in cls]
    ohf = [jnp.where(o_, 1.0, 0.0) for o_ in oh]
    inside = [dot(o_.astype(jnp.bfloat16), earlier) for o_ in ohf]
    seen = carry_ref[...]
    for i in subs:
        rank = jnp.sum(jnp.where(oh[i], inside[i] + seen, 0.0), axis=0, keepdims=True)
        slab_ref[:, i * SUB_TILE:(i + 1) * SUB_TILE] = jnp.where(
            sub == 0, cls[i], jnp.where(sub == 1, rank, 0.0))
        seen = seen + jnp.sum(ohf[i], axis=1, keepdims=True)
    carry_ref[...] = seen
    counts_ref[...] = seen


def _attn_route(x1, kt, v, g_x, w_q, w_o, g_f, w_rt, b_rt):
    b, s, d = x1.shape
    m = v.shape[1]
    ts = min(TOKEN_TILE, s)
    nt = s // ts
    const2 = lambda i, j: (0, 0)
    return pl.pallas_call(
        _attn_route_kernel,
        grid=(b, nt),
        in_specs=[
            pl.BlockSpec((1, ts, d), lambda i, j: (i, j, 0)),
            pl.BlockSpec((1, d, m), lambda i, j: (i, 0, 0)),
            pl.BlockSpec((1, m, d), lambda i, j: (i, 0, 0)),
            pl.BlockSpec(g_x.shape, const2),
            pl.BlockSpec(w_q.shape, const2),
            pl.BlockSpec(w_o.shape, const2),
            pl.BlockSpec(g_f.shape, const2),
            pl.BlockSpec(w_rt.shape, const2),
            pl.BlockSpec(b_rt.shape, const2),
        ],
        out_specs=[
            pl.BlockSpec((1, ts, d), lambda i, j: (i, j, 0)),
            pl.BlockSpec((SUBLANES, ts), lambda i, j: (0, i * nt + j)),
            pl.BlockSpec((ROUTE_LANES, 1), const2),
        ],
        out_shape=[
            jax.ShapeDtypeStruct((b, s, d), jnp.float32),
            jax.ShapeDtypeStruct((SUBLANES, b * s), jnp.float32),
            jax.ShapeDtypeStruct((ROUTE_LANES, 1), jnp.float32),
        ],
        scratch_shapes=[pltpu.VMEM((ROUTE_LANES, 1), jnp.float32)],
        compiler_params=pltpu.CompilerParams(
            dimension_semantics=("arbitrary", "arbitrary"), vmem_limit_bytes=VMEM_LIMIT_BYTES),
        name="attn_route",
    )(x1, kt, v, g_x, w_q, w_o, g_f, w_rt, b_rt)


def _dest_kernel(slab_ref, starts_ref, o_ref):
    slab = slab_ref[...]
    tt = slab.shape[1]
    crow = lax.broadcasted_iota(jnp.int32, (ROUTE_LANES, tt), 0).astype(jnp.float32)
    start = jnp.sum(jnp.where(crow == slab[0:1, :], starts_ref[...], 0.0), axis=0, keepdims=True)
    sub = lax.broadcasted_iota(jnp.int32, slab.shape, 0)
    o_ref[...] = jnp.where(sub == 0, start + slab[1:2, :], 0.0).astype(jnp.int32)


def _dest_rows(slab, starts_col):
    t = slab.shape[1]
    tt = min(2048, t)
    return pl.pallas_call(
        _dest_kernel,
        grid=(t // tt,),
        in_specs=[pl.BlockSpec((SUBLANES, tt), lambda i: (0, i)),
                  pl.BlockSpec((ROUTE_LANES, 1), lambda i: (0, 0))],
        out_specs=pl.BlockSpec((SUBLANES, tt), lambda i: (0, i)),
        out_shape=jax.ShapeDtypeStruct((SUBLANES, t), jnp.int32),
        compiler_params=pltpu.CompilerParams(dimension_semantics=("arbitrary",)),
        name="dest_rows",
    )(slab, starts_col)


def _wait_rows(src_ref, dst_ref, sem, n):
    pltpu.make_async_copy(src_ref.at[pl.ds(0, n)], dst_ref.at[pl.ds(0, n)], sem).wait()


def _dispatch_kernel(zero_block_ref, dest_ref, x_ref, xs_ref, zero_ref, sem, zsem):
    tiles, sub, _ = x_ref.shape
    nb = zero_block_ref.shape[0]

    def block_copy(j):
        start = pl.multiple_of(j * ROW_BLOCK, ROW_BLOCK)
        return pltpu.make_async_copy(zero_ref, xs_ref.at[pl.ds(start, ROW_BLOCK)], zsem)

    @pl.when(pl.program_id(0) == 0)
    def _():
        zero_ref[...] = jnp.zeros_like(zero_ref)

        @pl.loop(0, nb)
        def _(j):
            @pl.when(zero_block_ref[j] > 0)
            def _():
                block_copy(j).start()

        @pl.loop(0, nb)
        def _(j):
            @pl.when(zero_block_ref[j] > 0)
            def _():
                block_copy(j).wait()

    for i in range(tiles):
        for k in range(sub):
            pltpu.make_async_copy(x_ref.at[i, pl.ds(k, 1)],
                                  xs_ref.at[pl.ds(dest_ref[i * sub + k], 1)], sem).start(
                                      priority=k % 2)
    _wait_rows(xs_ref, xs_ref, sem, tiles * sub)


def _dispatch(x2, dest, zero_block):
    t, d = x2.shape
    tc = min(COPY_TILE, t)
    cap = zero_block.shape[0] * ROW_BLOCK
    grid_spec = pltpu.PrefetchScalarGridSpec(
        num_scalar_prefetch=1,
        grid=(t // tc,),
        in_specs=[
            pl.BlockSpec((tc,), lambda i, zb: (i,), memory_space=pltpu.SMEM),
            pl.BlockSpec((tc // SUBLANES, SUBLANES, d), lambda i, zb: (i, 0, 0)),
        ],
        out_specs=pl.BlockSpec(memory_space=pl.ANY),
        scratch_shapes=[pltpu.VMEM((ROW_BLOCK, d), jnp.float32),
                        pltpu.SemaphoreType.DMA, pltpu.SemaphoreType.DMA],
    )
    return pl.pallas_call(
        _dispatch_kernel,
        grid_spec=grid_spec,
        out_shape=jax.ShapeDtypeStruct((cap, d), jnp.float32),
        compiler_params=pltpu.CompilerParams(
            dimension_semantics=("arbitrary",), has_side_effects=True),
        name="dispatch",
    )(zero_block, dest, x2.reshape(t // SUBLANES, SUBLANES, d))


def _unsort(ys, dest, t):
    d = ys.shape[1]
    info = plsc.get_sparse_core_info()
    nc, ns = info.num_cores, info.num_subcores
    per_w = t // (nc * ns)
    ch = SC_GATHER_ROWS
    nchunk = per_w // ch
    mesh = plsc.VectorSubcoreMesh(core_axis_name="c", subcore_axis_name="s")

    @functools.partial(
        pl.kernel, mesh=mesh,
        out_type=jax.ShapeDtypeStruct((t, d), jnp.float32),
        scratch_types=[pltpu.VMEM((ch,), jnp.int32), pltpu.VMEM((ch, d), jnp.float32),
                       pltpu.SemaphoreType.DMA],
    )
    def unsort_sc(ys_hbm, dest_hbm, out_hbm, idx_v, rows_v, sem):
        wid = lax.axis_index("s") * nc + lax.axis_index("c")
        base = wid * per_w

        @pl.loop(0, nchunk)
        def _(c):
            off = base + c * ch
            pltpu.sync_copy(dest_hbm.at[pl.ds(off, ch)], idx_v)
            pltpu.async_copy(ys_hbm.at[idx_v], rows_v, sem).wait()
            pltpu.sync_copy(rows_v, out_hbm.at[pl.ds(off, ch)])

    return unsort_sc(ys, dest)


def _expert_kernel(grp_ref, ea_ref, eb_ref, nact_ref,
                   xs_ref, gf_ref, wr_ref, br_ref, gfin_ref, wg_hbm, wu_hbm, wd_hbm,
                   o_ref, wgb, wub, wdb, sg, su, sd, sem, cnt_ref):
    s = pl.program_id(0)
    blocks = tuple(BLOCKS_PER_STEP * s + i for i in range(BLOCKS_PER_STEP))
    g = grp_ref[blocks[0]]

    def stage(e):
        slot = e % 2
        return (pltpu.make_async_copy(wg_hbm.at[e], sg.at[slot], sem.at[slot, 0]),
                pltpu.make_async_copy(wu_hbm.at[e], su.at[slot], sem.at[slot, 1]),
                pltpu.make_async_copy(wd_hbm.at[e], sd.at[slot], sem.at[slot, 2]))

    def start_next():
        @pl.when(cnt_ref[0] < N_EXPERTS)
        def _():
            for cp in stage(cnt_ref[0]):
                cp.start()
            cnt_ref[0] = cnt_ref[0] + 1

    @pl.when(s == 0)
    def _():
        cnt_ref[0] = 0
        cnt_ref[1] = 0
        start_next()
        start_next()

    active = blocks[0] < nact_ref[0]

    @pl.when(active)
    def _():
        need = g * EXPERTS_PER_GROUP + functools.reduce(
            jnp.maximum, [eb_ref[j] for j in blocks])

        def load(e, carry):
            for cp in stage(e):
                cp.wait()
            slot = e % 2
            k = e % EXPERTS_PER_GROUP
            wgb[k] = sg[slot].astype(jnp.bfloat16)
            wub[k] = su[slot].astype(jnp.bfloat16)
            wdb[k] = sd[slot].astype(jnp.bfloat16)
            cnt_ref[1] = e + 1
            start_next()
            return carry

        lax.fori_loop(cnt_ref[1], need + 1, load, 0)

        lane = lax.broadcasted_iota(jnp.int32, (ROW_BLOCK, ROUTE_LANES), 1)
        is_g = lane < N_GROUPS
        lo = EXPERT_LANE0 + EXPERTS_PER_GROUP * g
        nblk = len(blocks)
        xs = [xs_ref[i * ROW_BLOCK:(i + 1) * ROW_BLOCK, :] for i in range(nblk)]
        h3 = [_rms(x, gf_ref[...]).astype(jnp.bfloat16) for x in xs]
        lg = [jnp.dot(h, wr_ref[...], preferred_element_type=jnp.float32) + br_ref[...]
              for h in h3]

        def gates(lgi, j):
            def pick(idx):
                return jnp.sum(jnp.where(lane == idx, lgi, 0.0), axis=-1, keepdims=True)

            gmax = jnp.max(jnp.where(is_g, lgi, _NEG), axis=-1, keepdims=True)
            den = jnp.sum(jnp.where(is_g, jnp.exp(jnp.where(is_g, lgi, _NEG) - gmax), 0.0),
                          axis=-1, keepdims=True)
            grp_p = jnp.exp(pick(g) - gmax) / den
            la = pick(lo + ea_ref[j])
            lb = pick(lo + eb_ref[j])
            m = jnp.maximum(la, lb)
            pa = jnp.exp(la - m)
            pb = jnp.exp(lb - m)
            return grp_p * pa / (pa + pb), grp_p * pb / (pa + pb)

        gate = [gates(lg[i], blocks[i]) for i in range(nblk)]
        ys = list(xs)
        for which, e_ref in enumerate((ea_ref, eb_ref)):
            ks = [e_ref[j] for j in blocks]
            gg = [jnp.dot(h3[i], wgb[ks[i]], preferred_element_type=jnp.float32)
                  for i in range(nblk)]
            uu = [jnp.dot(h3[i], wub[ks[i]], preferred_element_type=jnp.float32)
                  for i in range(nblk)]
            act = [(gg[i] * (1.0 / (1.0 + jnp.exp(-gg[i]))) * uu[i]
                    * gate[i][which]).astype(jnp.bfloat16) for i in range(nblk)]
            ys = [ys[i] + jnp.dot(act[i], wdb[ks[i]], preferred_element_type=jnp.float32)
                  for i in range(nblk)]
        for i in range(nblk):
            o_ref[i * ROW_BLOCK:(i + 1) * ROW_BLOCK, :] = _rms(ys[i], gfin_ref[...])

    @pl.when(jnp.logical_not(active))
    def _():
        o_ref[...] = jnp.zeros_like(o_ref)

    @pl.when(s == pl.num_programs(0) - 1)
    def _():
        def drain(e, carry):
            for cp in stage(e):
                cp.wait()
            return carry

        lax.fori_loop(cnt_ref[1], cnt_ref[0], drain, 0)


def _experts(xs, blk_grp, blk_a, blk_b, nact, g_ffn, w_r, b_r, g_final, w_gate, w_up, w_down):
    cap, d = xs.shape
    de = w_gate.shape[2]
    step_rows = BLOCKS_PER_STEP * ROW_BLOCK
    steps = cap // step_rows
    pre = lambda f: (lambda s, gr, ea, eb, na: f(s, na))
    const2 = pre(lambda s, na: (0, 0))
    last_step = lambda na: (na[0] - 1) // BLOCKS_PER_STEP
    hbm = pl.BlockSpec(memory_space=pl.ANY)
    grid_spec = pltpu.PrefetchScalarGridSpec(
        num_scalar_prefetch=4,
        grid=(steps,),
        in_specs=[
            pl.BlockSpec((step_rows, d), pre(lambda s, na: (jnp.minimum(s, last_step(na)), 0))),
            pl.BlockSpec(g_ffn.shape, const2),
            pl.BlockSpec(w_r.shape, const2),
            pl.BlockSpec(b_r.shape, const2),
            pl.BlockSpec(g_final.shape, const2),
            hbm, hbm, hbm,
        ],
        out_specs=pl.BlockSpec((step_rows, d), pre(lambda s, na: (s, 0))),
        scratch_shapes=[
            pltpu.VMEM((EXPERTS_PER_GROUP, d, de), jnp.bfloat16),
            pltpu.VMEM((EXPERTS_PER_GROUP, d, de), jnp.bfloat16),
            pltpu.VMEM((EXPERTS_PER_GROUP, de, d), jnp.bfloat16),
            pltpu.VMEM((2, d, de), jnp.float32),
            pltpu.VMEM((2, d, de), jnp.float32),
            pltpu.VMEM((2, de, d), jnp.float32),
            pltpu.SemaphoreType.DMA((2, 3)),
            pltpu.SMEM((2,), jnp.int32),
        ],
    )
    return pl.pallas_call(
        _expert_kernel,
        grid_spec=grid_spec,
        out_shape=jax.ShapeDtypeStruct((cap, d), jnp.float32),
        compiler_params=pltpu.CompilerParams(
            dimension_semantics=("arbitrary",), vmem_limit_bytes=VMEM_LIMIT_BYTES),
        name="experts",
    )(blk_grp, blk_a, blk_b, nact, xs, g_ffn, w_r, b_r, g_final, w_gate, w_up, w_down)


def _moe_final(x2, slab, counts_col, g_ffn, w_r, b_r, g_final, w_gate, w_up, w_down):
    t, d = x2.shape
    nb = t // ROW_BLOCK + N_CLASSES + N_GROUPS * (BLOCKS_PER_STEP - 1)
    assert nb % BLOCKS_PER_STEP == 0
    counts = counts_col[:N_CLASSES, 0].astype(jnp.int32)
    nblk = (counts + ROW_BLOCK - 1) // ROW_BLOCK
    grp_blocks = jnp.sum(nblk.reshape(N_GROUPS, N_PAIRS), axis=1)
    nblk = nblk.reshape(N_GROUPS, N_PAIRS).at[:, N_PAIRS - 1].add(
        (-grp_blocks) % BLOCKS_PER_STEP).reshape(N_CLASSES)
    blk_end = jnp.cumsum(nblk)
    blk_start = blk_end - nblk
    nact = blk_end[-1]
    j = jnp.arange(nb, dtype=jnp.int32)
    blk_class = jnp.minimum(
        jnp.sum((blk_end[None, :] <= j[:, None]).astype(jnp.int32), axis=1), N_CLASSES - 1)
    blk_class = jnp.where(j < nact, blk_class, blk_class[jnp.maximum(nact - 1, 0)])
    not_full = (j - blk_start[blk_class] + 1) * ROW_BLOCK > counts[blk_class]
    zero_block = (not_full | (j >= nact)).astype(jnp.int32)
    pair_a = jnp.array([p[0] for p in PAIRS], jnp.int32)
    pair_b = jnp.array([p[1] for p in PAIRS], jnp.int32)
    blk_grp = blk_class // N_PAIRS
    blk_a = pair_a[blk_class % N_PAIRS]
    blk_b = pair_b[blk_class % N_PAIRS]
    starts_col = jnp.zeros((ROUTE_LANES, 1), jnp.float32).at[:N_CLASSES, 0].set(
        (blk_start * ROW_BLOCK).astype(jnp.float32))

    dest = _dest_rows(slab, starts_col)[0]
    xs = _dispatch(x2, dest, zero_block)
    ys = _experts(xs, blk_grp, blk_a, blk_b, nact[None], g_ffn, w_r, b_r, g_final,
                  w_gate, w_up, w_down)
    return _unsort(ys, dest, t)


def kernel(x, mem, g_mix, w_in, conv_w, g_v, w_s, b_s, g_out_conv, g_out_gmlp, w_out, g_xattn,
           g_mem, w_q, w_k, w_v, w_o, g_ffn, w_grp, b_grp, w_rt, b_rt, w_gate, w_up, w_down,
           g_final):
    b, s, d = x.shape
    assert g_mix.shape[0] == 1, "the final norm is fused into the single layer's expert kernel"
    assert N_CLASSES <= ROUTE_LANES
    bf = jnp.bfloat16
    kt, v = _kv_proj(mem, g_mem[0][None], w_k[0].astype(bf), w_v[0].astype(bf))

    bias = jnp.repeat(b_s[0].T, GMLP_HEAD_DIM, axis=1)
    x1 = _mixer(x, g_mix[0][None], w_in[0].astype(bf), conv_w[0], g_v[0][None], w_s[0], bias,
                g_out_conv[0][None], g_out_gmlp[0][None], w_out[0].astype(bf))

    pad = ROUTE_LANES - N_GROUPS - N_EXPERTS
    w_r = jnp.concatenate([w_grp[0], w_rt[0], jnp.zeros((d, pad), jnp.float32)], axis=1).astype(bf)
    b_r = jnp.concatenate([b_grp[0], b_rt[0], jnp.zeros((pad,), jnp.float32)])[None]
    gap = ROUTE_EXPERT_ROW0 - N_GROUPS
    tail = ROUTE_LANES - ROUTE_EXPERT_ROW0 - N_EXPERTS
    w_r_t = jnp.concatenate([w_grp[0].T, jnp.zeros((gap, d), jnp.float32), w_rt[0].T,
                             jnp.zeros((tail, d), jnp.float32)], axis=0).astype(bf)
    b_r_t = jnp.concatenate([b_grp[0], jnp.zeros((gap,), jnp.float32), b_rt[0],
                             jnp.zeros((tail,), jnp.float32)])[:, None]
    x2, slab, counts_col = _attn_route(x1, kt, v, g_xattn[0][None], w_q[0].astype(bf),
                                       w_o[0].astype(bf), g_ffn[0][None], w_r_t, b_r_t)
    out = _moe_final(x2.reshape(b * s, d), slab, counts_col, g_ffn[0][None], w_r, b_r,
                     g_final[None], w_gate[0], w_up[0], w_down[0])
    return out.reshape(b, s, d)
```

```python
import functools

import jax
import jax.numpy as jnp
from jax import lax
from jax.experimental import pallas as pl
from jax.experimental.pallas import tpu as pltpu
from jax.experimental.pallas import tpu_sc as plsc

EPS = 1e-6
CONV_GROUP_WIDTH = 512
GMLP_HEADS = 8
GMLP_HEAD_DIM = 64
CHUNK = 128
CONV_K = 3
XA_HEADS = 4
N_GROUPS = 4
EXPERTS_PER_GROUP = 8
N_EXPERTS = N_GROUPS * EXPERTS_PER_GROUP
TOP_K = 2
ROUTE_LANES = 128
EXPERT_LANE0 = N_GROUPS
ROUTE_EXPERT_ROW0 = 8

N_PAIRS = EXPERTS_PER_GROUP * (EXPERTS_PER_GROUP - 1) // 2
N_CLASSES = N_GROUPS * N_PAIRS
PAIRS = [(a, b) for a in range(EXPERTS_PER_GROUP) for b in range(a + 1, EXPERTS_PER_GROUP)]

TOKEN_TILE = 1024
SUB_TILE = 256
ROW_BLOCK = 128
BLOCKS_PER_STEP = 4
COPY_TILE = 2048
SC_GATHER_ROWS = 32
VMEM_LIMIT_BYTES = 56 * 1024 * 1024

SUBLANES = 8

_NEG = -1e30


def _rms(x, g):
    return x * lax.rsqrt(jnp.mean(x * x, axis=-1, keepdims=True) + EPS) * g


def _gelu_tanh(x):
    return 0.5 * x * (1.0 + jnp.tanh(0.7978845608028654 * (x + 0.044715 * (x * x * x))))


def _kv_kernel(mem_ref, g_ref, wk_ref, wv_ref, kt_ref, v_ref):
    m = _rms(mem_ref[0], g_ref[...]).astype(jnp.bfloat16)
    k = jnp.dot(m, wk_ref[...], preferred_element_type=jnp.float32)
    v = jnp.dot(m, wv_ref[...], preferred_element_type=jnp.float32)
    kt_ref[0] = k.T.astype(jnp.bfloat16)
    v_ref[0] = v.astype(jnp.bfloat16)


def _kv_proj(mem, g_mem, w_k, w_v):
    b, m, d = mem.shape
    const = lambda i: (0, 0)
    return pl.pallas_call(
        _kv_kernel,
        grid=(b,),
        in_specs=[
            pl.BlockSpec((1, m, d), lambda i: (i, 0, 0)),
            pl.BlockSpec((1, d), const),
            pl.BlockSpec((d, d), const),
            pl.BlockSpec((d, d), const),
        ],
        out_specs=[
            pl.BlockSpec((1, d, m), lambda i: (i, 0, 0)),
            pl.BlockSpec((1, m, d), lambda i: (i, 0, 0)),
        ],
        out_shape=[
            jax.ShapeDtypeStruct((b, d, m), jnp.bfloat16),
            jax.ShapeDtypeStruct((b, m, d), jnp.bfloat16),
        ],
        compiler_params=pltpu.CompilerParams(
            dimension_semantics=("arbitrary",), vmem_limit_bytes=VMEM_LIMIT_BYTES),
        name="kv_proj",
    )(mem, g_mem, w_k, w_v)


def _mixer_kernel(x_ref, gmix_ref, win_ref, convw_ref, gv_ref, ws_ref, bias_ref,
                  goc_ref, gog_ref, wout_ref, o_ref, zbuf_ref):
    ts = x_ref.shape[1]
    w = CONV_GROUP_WIDTH
    hw = 2 * GMLP_HEAD_DIM
    nsub = ts // SUB_TILE

    def dot(a, b):
        return jnp.dot(a, b, preferred_element_type=jnp.float32)

    @pl.when(pl.program_id(1) == 0)
    def _():
        zbuf_ref[0:8, :] = jnp.zeros((8, w), jnp.float32)

    @pl.when(pl.program_id(1) != 0)
    def _():
        zbuf_ref[0:8, :] = zbuf_ref[ts:ts + 8, :]

    low = lax.broadcasted_iota(jnp.int32, (SUB_TILE, hw), 1) < GMLP_HEAD_DIM
    row = lax.broadcasted_iota(jnp.int32, (CHUNK, CHUNK), 0)
    colid = lax.broadcasted_iota(jnp.int32, (CHUNK, CHUNK), 1)
    causal = row >= colid
    ws = [jnp.where(causal, ws_ref[hd], 0.0).astype(jnp.bfloat16) for hd in range(GMLP_HEADS)]
    npair = SUB_TILE // CHUNK // 2
    lo = lax.broadcasted_iota(jnp.int32, (CHUNK, hw), 1) < GMLP_HEAD_DIM
    swap = lambda a: pltpu.roll(a, GMLP_HEAD_DIM, axis=1)

    def project(i):
        r0 = i * SUB_TILE
        xt = x_ref[0, r0:r0 + SUB_TILE, :]
        h = _rms(xt, gmix_ref[...]).astype(jnp.bfloat16)
        p = [dot(h, win_ref[:, k * w:(k + 1) * w]) for k in range(5)]
        z = p[1] * p[2]
        zbuf_ref[8 + r0:8 + r0 + SUB_TILE, :] = z
        return dict(xt=xt, gate_b=p[0], z=z, u=p[3], v=p[4], r0=r0)

    def gate_and_norm(st):
        r0 = st["r0"]
        zc = (convw_ref[0:1, :] * zbuf_ref[6 + r0:6 + r0 + SUB_TILE, :]
              + convw_ref[1:2, :] * zbuf_ref[7 + r0:7 + r0 + SUB_TILE, :]
              + convw_ref[2:3, :] * st["z"])
        st["ya"] = _rms(st["gate_b"] * zc, goc_ref[...]).astype(jnp.bfloat16)
        st["u"] = _gelu_tanh(st["u"])
        v = _gelu_tanh(st["v"])
        v2 = v * v
        ss_cols = []
        for k in range(GMLP_HEADS // 2):
            col = v2[:, hw * k:hw * (k + 1)]
            ss_cols.append(jnp.where(low,
                                     jnp.sum(jnp.where(low, col, 0.0), axis=1, keepdims=True),
                                     jnp.sum(jnp.where(low, 0.0, col), axis=1, keepdims=True)))
        ss = jnp.concatenate(ss_cols, axis=1)
        st["vn"] = v * lax.rsqrt(ss * (1.0 / GMLP_HEAD_DIM) + EPS) * gv_ref[...]

    def mix_positions(st):
        vn = st["vn"]
        s_cols = [[None] * (GMLP_HEADS // 2) for _ in range(2 * npair)]
        for hp in range(GMLP_HEADS // 2):
            cols = [vn[c * CHUNK:(c + 1) * CHUNK, hw * hp:hw * (hp + 1)] for c in range(2 * npair)]
            swapped = [swap(a) for a in cols]
            rhs_even = jnp.concatenate(
                [jnp.where(lo, cols[2 * p], swapped[2 * p + 1]) for p in range(npair)], axis=1)
            rhs_odd = jnp.concatenate(
                [jnp.where(lo, swapped[2 * p], cols[2 * p + 1]) for p in range(npair)], axis=1)
            out_e = dot(ws[2 * hp], rhs_even.astype(jnp.bfloat16))
            out_o = dot(ws[2 * hp + 1], rhs_odd.astype(jnp.bfloat16))
            for p in range(npair):
                e = out_e[:, hw * p:hw * (p + 1)]
                o = out_o[:, hw * p:hw * (p + 1)]
                s_cols[2 * p][hp] = jnp.where(lo, e, swap(o))
                s_cols[2 * p + 1][hp] = jnp.where(lo, swap(e), o)
        st["s"] = jnp.concatenate(
            [jnp.concatenate(c, axis=1) + bias_ref[...] for c in s_cols], axis=0)

    def output(st):
        r0 = st["r0"]
        yb = _rms(st["u"] * st["s"], gog_ref[...]).astype(jnp.bfloat16)
        o_ref[0, r0:r0 + SUB_TILE, :] = (st["xt"] + dot(st["ya"], wout_ref[0:w, :])
                                          + dot(yb, wout_ref[w:2 * w, :]))

    phases = (gate_and_norm, mix_positions, output)
    states = []
    for step in range(nsub + len(phases)):
        if step < nsub:
            states.append(project(step))
        for k, phase in enumerate(phases):
            i = step - 1 - k
            if 0 <= i < nsub:
                phase(states[i])


def _mixer(x, g_mix, w_in, conv_w, g_v, w_s, bias, g_oc, g_og, w_out):
    b, s, d = x.shape
    ts = min(TOKEN_TILE, s)
    const2 = lambda i, j: (0, 0)
    const3 = lambda i, j: (0, 0, 0)
    return pl.pallas_call(
        _mixer_kernel,
        grid=(b, s // ts),
        in_specs=[
            pl.BlockSpec((1, ts, d), lambda i, j: (i, j, 0)),
            pl.BlockSpec(g_mix.shape, const2),
            pl.BlockSpec(w_in.shape, const2),
            pl.BlockSpec(conv_w.shape, const2),
            pl.BlockSpec(g_v.shape, const2),
            pl.BlockSpec(w_s.shape, const3),
            pl.BlockSpec(bias.shape, const2),
            pl.BlockSpec(g_oc.shape, const2),
            pl.BlockSpec(g_og.shape, const2),
            pl.BlockSpec(w_out.shape, const2),
        ],
        out_specs=pl.BlockSpec((1, ts, d), lambda i, j: (i, j, 0)),
        out_shape=jax.ShapeDtypeStruct((b, s, d), jnp.float32),
        scratch_shapes=[pltpu.VMEM((ts + 8, CONV_GROUP_WIDTH), jnp.float32)],
        compiler_params=pltpu.CompilerParams(
            dimension_semantics=("arbitrary", "arbitrary"), vmem_limit_bytes=VMEM_LIMIT_BYTES),
        name="mixer",
    )(x, g_mix, w_in, conv_w, g_v, w_s, bias, g_oc, g_og, w_out)


def _attn_route_kernel(x_ref, kt_ref, v_ref, gx_ref, wq_ref, wo_ref, gf_ref, wrt_ref, brt_ref,
                       x2_ref, slab_ref, counts_ref, carry_ref):
    ts = x_ref.shape[1]
    d = x_ref.shape[2]
    hd = d // XA_HEADS
    nsub = ts // SUB_TILE
    first = (pl.program_id(0) == 0) & (pl.program_id(1) == 0)

    @pl.when(first)
    def _():
        carry_ref[...] = jnp.zeros_like(carry_ref)

    def dot(a, b):
        return jnp.dot(a, b, preferred_element_type=jnp.float32)

    subs = range(nsub)
    x1 = [x_ref[0, i * SUB_TILE:(i + 1) * SUB_TILE, :] for i in subs]
    h2 = [_rms(x, gx_ref[...]).astype(jnp.bfloat16) for x in x1]
    q = [dot(h, wq_ref[...]).astype(jnp.bfloat16) for h in h2]
    heads = [[] for _ in subs]
    for a in range(XA_HEADS):
        sc = [dot(q[i][:, a * hd:(a + 1) * hd], kt_ref[0, a * hd:(a + 1) * hd, :]) * (hd ** -0.5)
              for i in subs]
        p = [jnp.exp(s_ - jnp.max(s_, axis=-1, keepdims=True)) for s_ in sc]
        l = [jnp.sum(p_, axis=-1, keepdims=True) for p_ in p]
        o = [dot(p[i].astype(jnp.bfloat16), v_ref[0, :, a * hd:(a + 1) * hd]) for i in subs]
        for i in subs:
            heads[i].append((o[i] / l[i]).astype(jnp.bfloat16))
    x2 = [x1[i] + dot(jnp.concatenate(heads[i], axis=1), wo_ref[...]) for i in subs]
    for i in subs:
        x2_ref[0, i * SUB_TILE:(i + 1) * SUB_TILE, :] = x2[i]

    h3 = [_rms(x, gf_ref[...]).astype(jnp.bfloat16) for x in x2]
    lgs = [lax.dot_general(wrt_ref[...], h, (((1,), (1,)), ((), ())),
                           preferred_element_type=jnp.float32) + brt_ref[...] for h in h3]
    sub = lax.broadcasted_iota(jnp.int32, (EXPERTS_PER_GROUP, SUB_TILE), 0).astype(jnp.float32)
    big = float(EXPERTS_PER_GROUP)
    is_g = sub < N_GROUPS

    def classify(lg):
        glog = lg[0:EXPERTS_PER_GROUP, :]
        gmax = jnp.max(jnp.where(is_g, glog, _NEG), axis=0, keepdims=True)
        gidx = jnp.min(jnp.where(is_g & (glog == gmax), sub, big), axis=0, keepdims=True)
        el = lg[ROUTE_EXPERT_ROW0:ROUTE_EXPERT_ROW0 + EXPERTS_PER_GROUP, :]
        for grp in range(1, N_GROUPS):
            r0 = ROUTE_EXPERT_ROW0 + grp * EXPERTS_PER_GROUP
            el = jnp.where(gidx == grp, lg[r0:r0 + EXPERTS_PER_GROUP, :], el)
        t1 = jnp.max(el, axis=0, keepdims=True)
        i1 = jnp.min(jnp.where(el == t1, sub, big), axis=0, keepdims=True)
        rest = sub != i1
        t2 = jnp.max(jnp.where(rest, el, _NEG), axis=0, keepdims=True)
        i2 = jnp.min(jnp.where(rest & (el == t2), sub, big), axis=0, keepdims=True)
        a = jnp.minimum(i1, i2)
        b = jnp.maximum(i1, i2)
        pair = a * (2 * EXPERTS_PER_GROUP - 1 - a) * 0.5 + (b - a - 1.0)
        return gidx * N_PAIRS + pair

    cls = [classify(lg) for lg in lgs]

    r = lax.broadcasted_iota(jnp.int32, (SUB_TILE, SUB_TILE), 0)
    c = lax.broadcasted_iota(jnp.int32, (SUB_TILE, SUB_TILE), 1)
    earlier = jnp.where(r < c, 1.0, 0.0).astype(jnp.bfloat16)
    crow = lax.broadcasted_iota(jnp.int32, (ROUTE_LANES, SUB_TILE), 0).astype(jnp.float32)
    oh = [crow == cl for cl in cls]
    ohf = [jnp.where(o_, 1.0, 0.0) for o_ in oh]
    inside = [dot(o_.astype(jnp.bfloat16), earlier) for o_ in ohf]
    seen = carry_ref[...]
    for i in subs:
        rank = jnp.sum(jnp.where(oh[i], inside[i] + seen, 0.0), axis=0, keepdims=True)
        slab_ref[:, i * SUB_TILE:(i + 1) * SUB_TILE] = jnp.where(
            sub == 0, cls[i], jnp.where(sub == 1, rank, 0.0))
        seen = seen + jnp.sum(ohf[i], axis=1, keepdims=True)
    carry_ref[...] = seen
    counts_ref[...] = seen


def _attn_route(x1, kt, v, g_x, w_q, w_o, g_f, w_rt, b_rt):
    b, s, d = x1.shape
    m = v.shape[1]
    ts = min(TOKEN_TILE, s)
    nt = s // ts
    const2 = lambda i, j: (0, 0)
    return pl.pallas_call(
        _attn_route_kernel,
        grid=(b, nt),
        in_specs=[
            pl.BlockSpec((1, ts, d), lambda i, j: (i, j, 0)),
            pl.BlockSpec((1, d, m), lambda i, j: (i, 0, 0)),
            pl.BlockSpec((1, m, d), lambda i, j: (i, 0, 0)),
            pl.BlockSpec(g_x.shape, const2),
            pl.BlockSpec(w_q.shape, const2),
            pl.BlockSpec(w_o.shape, const2),
            pl.BlockSpec(g_f.shape, const2),
            pl.BlockSpec(w_rt.shape, const2),
            pl.BlockSpec(b_rt.shape, const2),
        ],
        out_specs=[
            pl.BlockSpec((1, ts, d), lambda i, j: (i, j, 0)),
            pl.BlockSpec((SUBLANES, ts), lambda i, j: (0, i * nt + j)),
            pl.BlockSpec((ROUTE_LANES, 1), const2),
        ],
        out_shape=[
            jax.ShapeDtypeStruct((b, s, d), jnp.float32),
            jax.ShapeDtypeStruct((SUBLANES, b * s), jnp.float32),
            jax.ShapeDtypeStruct((ROUTE_LANES, 1), jnp.float32),
        ],
        scratch_shapes=[pltpu.VMEM((ROUTE_LANES, 1), jnp.float32)],
        compiler_params=pltpu.CompilerParams(
            dimension_semantics=("arbitrary", "arbitrary"), vmem_limit_bytes=VMEM_LIMIT_BYTES),
        name="attn_route",
    )(x1, kt, v, g_x, w_q, w_o, g_f, w_rt, b_rt)


def _dest_kernel(slab_ref, starts_ref, o_ref):
    slab = slab_ref[...]
    tt = slab.shape[1]
    crow = lax.broadcasted_iota(jnp.int32, (ROUTE_LANES, tt), 0).astype(jnp.float32)
    start = jnp.sum(jnp.where(crow == slab[0:1, :], starts_ref[...], 0.0), axis=0, keepdims=True)
    sub = lax.broadcasted_iota(jnp.int32, slab.shape, 0)
    o_ref[...] = jnp.where(sub == 0, start + slab[1:2, :], 0.0).astype(jnp.int32)


def _dest_rows(slab, starts_col):
    t = slab.shape[1]
    tt = min(2048, t)
    return pl.pallas_call(
        _dest_kernel,
        grid=(t // tt,),
        in_specs=[pl.BlockSpec((SUBLANES, tt), lambda i: (0, i)),
                  pl.BlockSpec((ROUTE_LANES, 1), lambda i: (0, 0))],
        out_specs=pl.BlockSpec((SUBLANES, tt), lambda i: (0, i)),
        out_shape=jax.ShapeDtypeStruct((SUBLANES, t), jnp.int32),
        compiler_params=pltpu.CompilerParams(dimension_semantics=("arbitrary",)),
        name="dest_rows",
    )(slab, starts_col)


def _wait_rows(src_ref, dst_ref, sem, n):
    pltpu.make_async_copy(src_ref.at[pl.ds(0, n)], dst_ref.at[pl.ds(0, n)], sem).wait()


def _dispatch_kernel(zero_block_ref, dest_ref, x_ref, xs_ref, zero_ref, sem, zsem):
    tiles, sub, _ = x_ref.shape
    nb = zero_block_ref.shape[0]

    def block_copy(j):
        start = pl.multiple_of(j * ROW_BLOCK, ROW_BLOCK)
        return pltpu.make_async_copy(zero_ref, xs_ref.at[pl.ds(start, ROW_BLOCK)], zsem)

    @pl.when(pl.program_id(0) == 0)
    def _():
        zero_ref[...] = jnp.zeros_like(zero_ref)

        @pl.loop(0, nb)
        def _(j):
            @pl.when(zero_block_ref[j] > 0)
            def _():
                block_copy(j).start()

        @pl.loop(0, nb)
        def _(j):
            @pl.when(zero_block_ref[j] > 0)
            def _():
                block_copy(j).wait()

    for i in range(tiles):
        for k in range(sub):
            pltpu.make_async_copy(x_ref.at[i, pl.ds(k, 1)],
                                  xs_ref.at[pl.ds(dest_ref[i * sub + k], 1)], sem).start(
                                      priority=k % 2)
    _wait_rows(xs_ref, xs_ref, sem, tiles * sub)


def _dispatch(x2, dest, zero_block):
    t, d = x2.shape
    tc = min(COPY_TILE, t)
    cap = zero_block.shape[0] * ROW_BLOCK
    grid_spec = pltpu.PrefetchScalarGridSpec(
        num_scalar_prefetch=1,
        grid=(t // tc,),
        in_specs=[
            pl.BlockSpec((tc,), lambda i, zb: (i,), memory_space=pltpu.SMEM),
            pl.BlockSpec((tc // SUBLANES, SUBLANES, d), lambda i, zb: (i, 0, 0)),
        ],
        out_specs=pl.BlockSpec(memory_space=pl.ANY),
        scratch_shapes=[pltpu.VMEM((ROW_BLOCK, d), jnp.float32),
                        pltpu.SemaphoreType.DMA, pltpu.SemaphoreType.DMA],
    )
    return pl.pallas_call(
        _dispatch_kernel,
        grid_spec=grid_spec,
        out_shape=jax.ShapeDtypeStruct((cap, d), jnp.float32),
        compiler_params=pltpu.CompilerParams(
            dimension_semantics=("arbitrary",), has_side_effects=True),
        name="dispatch",
    )(zero_block, dest, x2.reshape(t // SUBLANES, SUBLANES, d))


def _unsort(ys, dest, t):
    d = ys.shape[1]
    info = plsc.get_sparse_core_info()
    nc, ns = info.num_cores, info.num_subcores
    per_w = t // (nc * ns)
    ch = SC_GATHER_ROWS
    nchunk = per_w // ch
    mesh = plsc.VectorSubcoreMesh(core_axis_name="c", subcore_axis_name="s")

    @functools.partial(
        pl.kernel, mesh=mesh,
        out_type=jax.ShapeDtypeStruct((t, d), jnp.float32),
        scratch_types=[pltpu.VMEM((per_w,), jnp.int32),
                       pltpu.VMEM((ch, d), jnp.float32), pltpu.VMEM((ch, d), jnp.float32),
                       pltpu.SemaphoreType.DMA, pltpu.SemaphoreType.DMA,
                       pltpu.SemaphoreType.DMA, pltpu.SemaphoreType.DMA],
    )
    def unsort_sc(ys_hbm, dest_hbm, out_hbm, idx_v, rows0, rows1, g0, g1, w0, w1):
        wid = lax.axis_index("s") * nc + lax.axis_index("c")
        base = wid * per_w
        pltpu.sync_copy(dest_hbm.at[pl.ds(base, per_w)], idx_v)
        bufs = ((rows0, g0, w0), (rows1, g1, w1))

        def gather(c, buf):
            return pltpu.make_async_copy(ys_hbm.at[idx_v.at[pl.ds(c * ch, ch)]], buf[0], buf[1])

        def write(c, buf):
            return pltpu.make_async_copy(buf[0], out_hbm.at[pl.ds(base + c * ch, ch)], buf[2])

        gather(0, bufs[0]).start()

        @pl.loop(0, nchunk, step=2)
        def _(c):
            for b in range(2):
                cur, nxt = bufs[b], bufs[1 - b]
                cc = c + b

                @pl.when(cc >= 1)
                def _():
                    write(cc - 1, nxt).wait()

                @pl.when(cc + 1 < nchunk)
                def _():
                    gather(cc + 1, nxt).start()

                gather(cc, cur).wait()
                write(cc, cur).start()

        write(nchunk - 1, bufs[(nchunk - 1) % 2]).wait()

    return unsort_sc(ys, dest)


def _expert_kernel(grp_ref, ea_ref, eb_ref, nact_ref,
                   xs_ref, gf_ref, wr_ref, br_ref, gfin_ref, wg_hbm, wu_hbm, wd_hbm,
                   o_ref, wgb, wub, wdb, sg, su, sd, sem, cnt_ref):
    s = pl.program_id(0)
    blocks = tuple(BLOCKS_PER_STEP * s + i for i in range(BLOCKS_PER_STEP))
    g = grp_ref[blocks[0]]

    def stage(e):
        slot = e % 2
        return (pltpu.make_async_copy(wg_hbm.at[e], sg.at[slot], sem.at[slot, 0]),
                pltpu.make_async_copy(wu_hbm.at[e], su.at[slot], sem.at[slot, 1]),
                pltpu.make_async_copy(wd_hbm.at[e], sd.at[slot], sem.at[slot, 2]))

    def start_next():
        @pl.when(cnt_ref[0] < N_EXPERTS)
        def _():
            for cp in stage(cnt_ref[0]):
                cp.start()
            cnt_ref[0] = cnt_ref[0] + 1

    @pl.when(s == 0)
    def _():
        cnt_ref[0] = 0
        cnt_ref[1] = 0
        start_next()
        start_next()

    active = blocks[0] < nact_ref[0]

    @pl.when(active)
    def _():
        need = g * EXPERTS_PER_GROUP + functools.reduce(
            jnp.maximum, [eb_ref[j] for j in blocks])

        def load(e, carry):
            for cp in stage(e):
                cp.wait()
            slot = e % 2
            k = e % EXPERTS_PER_GROUP
            wgb[k] = sg[slot].astype(jnp.bfloat16)
            wub[k] = su[slot].astype(jnp.bfloat16)
            wdb[k] = sd[slot].astype(jnp.bfloat16)
            cnt_ref[1] = e + 1
            start_next()
            return carry

        lax.fori_loop(cnt_ref[1], need + 1, load, 0)

        lane = lax.broadcasted_iota(jnp.int32, (ROW_BLOCK, ROUTE_LANES), 1)
        is_g = lane < N_GROUPS
        lo = EXPERT_LANE0 + EXPERTS_PER_GROUP * g
        nblk = len(blocks)
        xs = [xs_ref[i * ROW_BLOCK:(i + 1) * ROW_BLOCK, :] for i in range(nblk)]
        h3 = [_rms(x, gf_ref[...]).astype(jnp.bfloat16) for x in xs]
        lg = [jnp.dot(h, wr_ref[...], preferred_element_type=jnp.float32) + br_ref[...]
              for h in h3]

        def gates(lgi, j):
            def pick(idx):
                return jnp.sum(jnp.where(lane == idx, lgi, 0.0), axis=-1, keepdims=True)

            gmax = jnp.max(jnp.where(is_g, lgi, _NEG), axis=-1, keepdims=True)
            den = jnp.sum(jnp.where(is_g, jnp.exp(jnp.where(is_g, lgi, _NEG) - gmax), 0.0),
                          axis=-1, keepdims=True)
            grp_p = jnp.exp(pick(g) - gmax) / den
            la = pick(lo + ea_ref[j])
            lb = pick(lo + eb_ref[j])
            m = jnp.maximum(la, lb)
            pa = jnp.exp(la - m)
            pb = jnp.exp(lb - m)
            return grp_p * pa / (pa + pb), grp_p * pb / (pa + pb)

        gate = [gates(lg[i], blocks[i]) for i in range(nblk)]
        ys = list(xs)
        for which, e_ref in enumerate((ea_ref, eb_ref)):
            ks = [e_ref[j] for j in blocks]
            gg = [jnp.dot(h3[i], wgb[ks[i]], preferred_element_type=jnp.float32)
                  for i in range(nblk)]
            uu = [jnp.dot(h3[i], wub[ks[i]], preferred_element_type=jnp.float32)
                  for i in range(nblk)]
            act = [(gg[i] * (1.0 / (1.0 + jnp.exp(-gg[i]))) * uu[i]
                    * gate[i][which]).astype(jnp.bfloat16) for i in range(nblk)]
            ys = [ys[i] + jnp.dot(act[i], wdb[ks[i]], preferred_element_type=jnp.float32)
                  for i in range(nblk)]
        for i in range(nblk):
            o_ref[i * ROW_BLOCK:(i + 1) * ROW_BLOCK, :] = _rms(ys[i], gfin_ref[...])

    @pl.when(jnp.logical_not(active))
    def _():
        o_ref[...] = jnp.zeros_like(o_ref)

    @pl.when(s == pl.num_programs(0) - 1)
    def _():
        def drain(e, carry):
            for cp in stage(e):
                cp.wait()
            return carry

        lax.fori_loop(cnt_ref[1], cnt_ref[0], drain, 0)


def _experts(xs, blk_grp, blk_a, blk_b, nact, g_ffn, w_r, b_r, g_final, w_gate, w_up, w_down):
    cap, d = xs.shape
    de = w_gate.shape[2]
    step_rows = BLOCKS_PER_STEP * ROW_BLOCK
    steps = cap // step_rows
    pre = lambda f: (lambda s, gr, ea, eb, na: f(s, na))
    const2 = pre(lambda s, na: (0, 0))
    last_step = lambda na: (na[0] - 1) // BLOCKS_PER_STEP
    hbm = pl.BlockSpec(memory_space=pl.ANY)
    grid_spec = pltpu.PrefetchScalarGridSpec(
        num_scalar_prefetch=4,
        grid=(steps,),
        in_specs=[
            pl.BlockSpec((step_rows, d), pre(lambda s, na: (jnp.minimum(s, last_step(na)), 0))),
            pl.BlockSpec(g_ffn.shape, const2),
            pl.BlockSpec(w_r.shape, const2),
            pl.BlockSpec(b_r.shape, const2),
            pl.BlockSpec(g_final.shape, const2),
            hbm, hbm, hbm,
        ],
        out_specs=pl.BlockSpec((step_rows, d), pre(lambda s, na: (s, 0))),
        scratch_shapes=[
            pltpu.VMEM((EXPERTS_PER_GROUP, d, de), jnp.bfloat16),
            pltpu.VMEM((EXPERTS_PER_GROUP, d, de), jnp.bfloat16),
            pltpu.VMEM((EXPERTS_PER_GROUP, de, d), jnp.bfloat16),
            pltpu.VMEM((2, d, de), jnp.float32),
            pltpu.VMEM((2, d, de), jnp.float32),
            pltpu.VMEM((2, de, d), jnp.float32),
            pltpu.SemaphoreType.DMA((2, 3)),
            pltpu.SMEM((2,), jnp.int32),
        ],
    )
    return pl.pallas_call(
        _expert_kernel,
        grid_spec=grid_spec,
        out_shape=jax.ShapeDtypeStruct((cap, d), jnp.float32),
        compiler_params=pltpu.CompilerParams(
            dimension_semantics=("arbitrary",), vmem_limit_bytes=VMEM_LIMIT_BYTES),
        name="experts",
    )(blk_grp, blk_a, blk_b, nact, xs, g_ffn, w_r, b_r, g_final, w_gate, w_up, w_down)


def _moe_final(x2, slab, counts_col, g_ffn, w_r, b_r, g_final, w_gate, w_up, w_down):
    t, d = x2.shape
    nb = t // ROW_BLOCK + N_CLASSES + N_GROUPS * (BLOCKS_PER_STEP - 1)
    assert nb % BLOCKS_PER_STEP == 0
    counts = counts_col[:N_CLASSES, 0].astype(jnp.int32)
    nblk = (counts + ROW_BLOCK - 1) // ROW_BLOCK
    grp_blocks = jnp.sum(nblk.reshape(N_GROUPS, N_PAIRS), axis=1)
    nblk = nblk.reshape(N_GROUPS, N_PAIRS).at[:, N_PAIRS - 1].add(
        (-grp_blocks) % BLOCKS_PER_STEP).reshape(N_CLASSES)
    blk_end = jnp.cumsum(nblk)
    blk_start = blk_end - nblk
    nact = blk_end[-1]
    j = jnp.arange(nb, dtype=jnp.int32)
    blk_class = jnp.minimum(
        jnp.sum((blk_end[None, :] <= j[:, None]).astype(jnp.int32), axis=1), N_CLASSES - 1)
    blk_class = jnp.where(j < nact, blk_class, blk_class[jnp.maximum(nact - 1, 0)])
    not_full = (j - blk_start[blk_class] + 1) * ROW_BLOCK > counts[blk_class]
    zero_block = (not_full | (j >= nact)).astype(jnp.int32)
    pair_a = jnp.array([p[0] for p in PAIRS], jnp.int32)
    pair_b = jnp.array([p[1] for p in PAIRS], jnp.int32)
    blk_grp = blk_class // N_PAIRS
    blk_a = pair_a[blk_class % N_PAIRS]
    blk_b = pair_b[blk_class % N_PAIRS]
    starts_col = jnp.zeros((ROUTE_LANES, 1), jnp.float32).at[:N_CLASSES, 0].set(
        (blk_start * ROW_BLOCK).astype(jnp.float32))

    dest = _dest_rows(slab, starts_col)[0]
    xs = _dispatch(x2, dest, zero_block)
    ys = _experts(xs, blk_grp, blk_a, blk_b, nact[None], g_ffn, w_r, b_r, g_final,
                  w_gate, w_up, w_down)
    return _unsort(ys, dest, t)


def kernel(x, mem, g_mix, w_in, conv_w, g_v, w_s, b_s, g_out_conv, g_out_gmlp, w_out, g_xattn,
           g_mem, w_q, w_k, w_v, w_o, g_ffn, w_grp, b_grp, w_rt, b_rt, w_gate, w_up, w_down,
           g_final):
    b, s, d = x.shape
    assert g_mix.shape[0] == 1, "the final norm is fused into the single layer's expert kernel"
    assert N_CLASSES <= ROUTE_LANES
    bf = jnp.bfloat16
    kt, v = _kv_proj(mem, g_mem[0][None], w_k[0].astype(bf), w_v[0].astype(bf))

    bias = jnp.repeat(b_s[0].T, GMLP_HEAD_DIM, axis=1)
    x1 = _mixer(x, g_mix[0][None], w_in[0].astype(bf), conv_w[0], g_v[0][None], w_s[0], bias,
                g_out_conv[0][None], g_out_gmlp[0][None], w_out[0].astype(bf))

    pad = ROUTE_LANES - N_GROUPS - N_EXPERTS
    w_r = jnp.concatenate([w_grp[0], w_rt[0], jnp.zeros((d, pad), jnp.float32)], axis=1).astype(bf)
    b_r = jnp.concatenate([b_grp[0], b_rt[0], jnp.zeros((pad,), jnp.float32)])[None]
    gap = ROUTE_EXPERT_ROW0 - N_GROUPS
    tail = ROUTE_LANES - ROUTE_EXPERT_ROW0 - N_EXPERTS
    w_r_t = jnp.concatenate([w_grp[0].T, jnp.zeros((gap, d), jnp.float32), w_rt[0].T,
                             jnp.zeros((tail, d), jnp.float32)], axis=0).astype(bf)
    b_r_t = jnp.concatenate([b_grp[0], jnp.zeros((gap,), jnp.float32), b_rt[0],
                             jnp.zeros((tail,), jnp.float32)])[:, None]
    x2, slab, counts_col = _attn_route(x1, kt, v, g_xattn[0][None], w_q[0].astype(bf),
                                       w_o[0].astype(bf), g_ffn[0][None], w_r_t, b_r_t)
    out = _moe_final(x2.reshape(b * s, d), slab, counts_col, g_ffn[0][None], w_r, b_r,
                     g_final[None], w_gate[0], w_up[0], w_down[0])
    return out.reshape(b, s, d)
```

```python
import functools

import jax
import jax.numpy as jnp
from jax import lax
from jax.experimental import pallas as pl
from jax.experimental.pallas import tpu as pltpu
from jax.experimental.pallas import tpu_sc as plsc

EPS = 1e-6
CONV_GROUP_WIDTH = 512
GMLP_HEADS = 8
GMLP_HEAD_DIM = 64
CHUNK = 128
CONV_K = 3
XA_HEADS = 4
N_GROUPS = 4
EXPERTS_PER_GROUP = 8
N_EXPERTS = N_GROUPS * EXPERTS_PER_GROUP
TOP_K = 2
ROUTE_LANES = 128
EXPERT_LANE0 = N_GROUPS
ROUTE_EXPERT_ROW0 = 8

N_PAIRS = EXPERTS_PER_GROUP * (EXPERTS_PER_GROUP - 1) // 2
N_CLASSES = N_GROUPS * N_PAIRS
PAIRS = [(a, b) for a in range(EXPERTS_PER_GROUP) for b in range(a + 1, EXPERTS_PER_GROUP)]

TOKEN_TILE = 1024
SUB_TILE = 256
ROW_BLOCK = 128
BLOCKS_PER_STEP = 4
SC_GATHER_ROWS = 32
VMEM_LIMIT_BYTES = 56 * 1024 * 1024

SUBLANES = 8

_NEG = -1e30


def _rms(x, g):
    return x * lax.rsqrt(jnp.mean(x * x, axis=-1, keepdims=True) + EPS) * g


def _gelu_tanh(x):
    return 0.5 * x * (1.0 + jnp.tanh(0.7978845608028654 * (x + 0.044715 * (x * x * x))))


def _kv_kernel(mem_ref, g_ref, wk_ref, wv_ref, kt_ref, v_ref):
    m = _rms(mem_ref[0], g_ref[...]).astype(jnp.bfloat16)
    k = jnp.dot(m, wk_ref[...], preferred_element_type=jnp.float32)
    v = jnp.dot(m, wv_ref[...], preferred_element_type=jnp.float32)
    kt_ref[0] = k.T.astype(jnp.bfloat16)
    v_ref[0] = v.astype(jnp.bfloat16)


def _kv_proj(mem, g_mem, w_k, w_v):
    b, m, d = mem.shape
    const = lambda i: (0, 0)
    return pl.pallas_call(
        _kv_kernel,
        grid=(b,),
        in_specs=[
            pl.BlockSpec((1, m, d), lambda i: (i, 0, 0)),
            pl.BlockSpec((1, d), const),
            pl.BlockSpec((d, d), const),
            pl.BlockSpec((d, d), const),
        ],
        out_specs=[
            pl.BlockSpec((1, d, m), lambda i: (i, 0, 0)),
            pl.BlockSpec((1, m, d), lambda i: (i, 0, 0)),
        ],
        out_shape=[
            jax.ShapeDtypeStruct((b, d, m), jnp.bfloat16),
            jax.ShapeDtypeStruct((b, m, d), jnp.bfloat16),
        ],
        compiler_params=pltpu.CompilerParams(
            dimension_semantics=("arbitrary",), vmem_limit_bytes=VMEM_LIMIT_BYTES),
        name="kv_proj",
    )(mem, g_mem, w_k, w_v)


def _mixer_kernel(x_ref, gmix_ref, win_ref, convw_ref, gv_ref, ws_ref, bias_ref,
                  goc_ref, gog_ref, wout_ref, o_ref, zbuf_ref):
    ts = x_ref.shape[1]
    w = CONV_GROUP_WIDTH
    hw = 2 * GMLP_HEAD_DIM
    nsub = ts // SUB_TILE

    def dot(a, b):
        return jnp.dot(a, b, preferred_element_type=jnp.float32)

    @pl.when(pl.program_id(1) == 0)
    def _():
        zbuf_ref[0:8, :] = jnp.zeros((8, w), jnp.float32)

    @pl.when(pl.program_id(1) != 0)
    def _():
        zbuf_ref[0:8, :] = zbuf_ref[ts:ts + 8, :]

    low = lax.broadcasted_iota(jnp.int32, (SUB_TILE, hw), 1) < GMLP_HEAD_DIM
    row = lax.broadcasted_iota(jnp.int32, (CHUNK, CHUNK), 0)
    colid = lax.broadcasted_iota(jnp.int32, (CHUNK, CHUNK), 1)
    causal = row >= colid
    ws = [jnp.where(causal, ws_ref[hd], 0.0).astype(jnp.bfloat16) for hd in range(GMLP_HEADS)]
    npair = SUB_TILE // CHUNK // 2
    lo = lax.broadcasted_iota(jnp.int32, (CHUNK, hw), 1) < GMLP_HEAD_DIM
    swap = lambda a: pltpu.roll(a, GMLP_HEAD_DIM, axis=1)

    def project(i):
        r0 = i * SUB_TILE
        xt = x_ref[0, r0:r0 + SUB_TILE, :]
        h = _rms(xt, gmix_ref[...]).astype(jnp.bfloat16)
        p = [dot(h, win_ref[:, k * w:(k + 1) * w]) for k in range(5)]
        z = p[1] * p[2]
        zbuf_ref[8 + r0:8 + r0 + SUB_TILE, :] = z
        return dict(xt=xt, gate_b=p[0], z=z, u=p[3], v=p[4], r0=r0)

    def gate_and_norm(st):
        r0 = st["r0"]
        zc = (convw_ref[0:1, :] * zbuf_ref[6 + r0:6 + r0 + SUB_TILE, :]
              + convw_ref[1:2, :] * zbuf_ref[7 + r0:7 + r0 + SUB_TILE, :]
              + convw_ref[2:3, :] * st["z"])
        st["ya"] = _rms(st["gate_b"] * zc, goc_ref[...]).astype(jnp.bfloat16)
        st["u"] = _gelu_tanh(st["u"])
        v = _gelu_tanh(st["v"])
        v2 = v * v
        ss_cols = []
        for k in range(GMLP_HEADS // 2):
            col = v2[:, hw * k:hw * (k + 1)]
            ss_cols.append(jnp.where(low,
                                     jnp.sum(jnp.where(low, col, 0.0), axis=1, keepdims=True),
                                     jnp.sum(jnp.where(low, 0.0, col), axis=1, keepdims=True)))
        ss = jnp.concatenate(ss_cols, axis=1)
        st["vn"] = v * lax.rsqrt(ss * (1.0 / GMLP_HEAD_DIM) + EPS) * gv_ref[...]

    def mix_positions(st):
        vn = st["vn"]
        s_cols = [[None] * (GMLP_HEADS // 2) for _ in range(2 * npair)]
        for hp in range(GMLP_HEADS // 2):
            cols = [vn[c * CHUNK:(c + 1) * CHUNK, hw * hp:hw * (hp + 1)] for c in range(2 * npair)]
            swapped = [swap(a) for a in cols]
            rhs_even = jnp.concatenate(
                [jnp.where(lo, cols[2 * p], swapped[2 * p + 1]) for p in range(npair)], axis=1)
            rhs_odd = jnp.concatenate(
                [jnp.where(lo, swapped[2 * p], cols[2 * p + 1]) for p in range(npair)], axis=1)
            out_e = dot(ws[2 * hp], rhs_even.astype(jnp.bfloat16))
            out_o = dot(ws[2 * hp + 1], rhs_odd.astype(jnp.bfloat16))
            for p in range(npair):
                e = out_e[:, hw * p:hw * (p + 1)]
                o = out_o[:, hw * p:hw * (p + 1)]
                s_cols[2 * p][hp] = jnp.where(lo, e, swap(o))
                s_cols[2 * p + 1][hp] = jnp.where(lo, swap(e), o)
        st["s"] = jnp.concatenate(
            [jnp.concatenate(c, axis=1) + bias_ref[...] for c in s_cols], axis=0)

    def output(st):
        r0 = st["r0"]
        yb = _rms(st["u"] * st["s"], gog_ref[...]).astype(jnp.bfloat16)
        o_ref[0, r0:r0 + SUB_TILE, :] = (st["xt"] + dot(st["ya"], wout_ref[0:w, :])
                                          + dot(yb, wout_ref[w:2 * w, :]))

    phases = (gate_and_norm, mix_positions, output)
    states = []
    for step in range(nsub + len(phases)):
        if step < nsub:
            states.append(project(step))
        for k, phase in enumerate(phases):
            i = step - 1 - k
            if 0 <= i < nsub:
                phase(states[i])


def _mixer(x, g_mix, w_in, conv_w, g_v, w_s, bias, g_oc, g_og, w_out):
    b, s, d = x.shape
    ts = min(TOKEN_TILE, s)
    const2 = lambda i, j: (0, 0)
    const3 = lambda i, j: (0, 0, 0)
    return pl.pallas_call(
        _mixer_kernel,
        grid=(b, s // ts),
        in_specs=[
            pl.BlockSpec((1, ts, d), lambda i, j: (i, j, 0)),
            pl.BlockSpec(g_mix.shape, const2),
            pl.BlockSpec(w_in.shape, const2),
            pl.BlockSpec(conv_w.shape, const2),
            pl.BlockSpec(g_v.shape, const2),
            pl.BlockSpec(w_s.shape, const3),
            pl.BlockSpec(bias.shape, const2),
            pl.BlockSpec(g_oc.shape, const2),
            pl.BlockSpec(g_og.shape, const2),
            pl.BlockSpec(w_out.shape, const2),
        ],
        out_specs=pl.BlockSpec((1, ts, d), lambda i, j: (i, j, 0)),
        out_shape=jax.ShapeDtypeStruct((b, s, d), jnp.float32),
        scratch_shapes=[pltpu.VMEM((ts + 8, CONV_GROUP_WIDTH), jnp.float32)],
        compiler_params=pltpu.CompilerParams(
            dimension_semantics=("arbitrary", "arbitrary"), vmem_limit_bytes=VMEM_LIMIT_BYTES),
        name="mixer",
    )(x, g_mix, w_in, conv_w, g_v, w_s, bias, g_oc, g_og, w_out)


def _attn_route_kernel(x_ref, kt_ref, v_ref, gx_ref, wq_ref, wo_ref, gf_ref, wrt_ref, brt_ref,
                       x2_ref, slab_ref, counts_ref, carry_ref):
    ts = x_ref.shape[1]
    d = x_ref.shape[2]
    hd = d // XA_HEADS
    nsub = ts // SUB_TILE
    first = (pl.program_id(0) == 0) & (pl.program_id(1) == 0)

    @pl.when(first)
    def _():
        carry_ref[...] = jnp.zeros_like(carry_ref)

    def dot(a, b):
        return jnp.dot(a, b, preferred_element_type=jnp.float32)

    subs = range(nsub)
    x1 = [x_ref[0, i * SUB_TILE:(i + 1) * SUB_TILE, :] for i in subs]
    h2 = [_rms(x, gx_ref[...]).astype(jnp.bfloat16) for x in x1]
    q = [dot(h, wq_ref[...]).astype(jnp.bfloat16) for h in h2]
    heads = [[] for _ in subs]
    for a in range(XA_HEADS):
        sc = [dot(q[i][:, a * hd:(a + 1) * hd], kt_ref[0, a * hd:(a + 1) * hd, :]) * (hd ** -0.5)
              for i in subs]
        p = [jnp.exp(s_ - jnp.max(s_, axis=-1, keepdims=True)) for s_ in sc]
        l = [jnp.sum(p_, axis=-1, keepdims=True) for p_ in p]
        o = [dot(p[i].astype(jnp.bfloat16), v_ref[0, :, a * hd:(a + 1) * hd]) for i in subs]
        for i in subs:
            heads[i].append((o[i] / l[i]).astype(jnp.bfloat16))
    x2 = [x1[i] + dot(jnp.concatenate(heads[i], axis=1), wo_ref[...]) for i in subs]
    for i in subs:
        x2_ref[0, i * SUB_TILE:(i + 1) * SUB_TILE, :] = x2[i]

    h3 = [_rms(x, gf_ref[...]).astype(jnp.bfloat16) for x in x2]
    lgs = [lax.dot_general(wrt_ref[...], h, (((1,), (1,)), ((), ())),
                           preferred_element_type=jnp.float32) + brt_ref[...] for h in h3]
    sub = lax.broadcasted_iota(jnp.int32, (EXPERTS_PER_GROUP, SUB_TILE), 0).astype(jnp.float32)
    big = float(EXPERTS_PER_GROUP)
    is_g = sub < N_GROUPS

    def classify(lg):
        glog = lg[0:EXPERTS_PER_GROUP, :]
        gmax = jnp.max(jnp.where(is_g, glog, _NEG), axis=0, keepdims=True)
        gidx = jnp.min(jnp.where(is_g & (glog == gmax), sub, big), axis=0, keepdims=True)
        el = lg[ROUTE_EXPERT_ROW0:ROUTE_EXPERT_ROW0 + EXPERTS_PER_GROUP, :]
        for grp in range(1, N_GROUPS):
            r0 = ROUTE_EXPERT_ROW0 + grp * EXPERTS_PER_GROUP
            el = jnp.where(gidx == grp, lg[r0:r0 + EXPERTS_PER_GROUP, :], el)
        t1 = jnp.max(el, axis=0, keepdims=True)
        i1 = jnp.min(jnp.where(el == t1, sub, big), axis=0, keepdims=True)
        rest = sub != i1
        t2 = jnp.max(jnp.where(rest, el, _NEG), axis=0, keepdims=True)
        i2 = jnp.min(jnp.where(rest & (el == t2), sub, big), axis=0, keepdims=True)
        a = jnp.minimum(i1, i2)
        b = jnp.maximum(i1, i2)
        pair = a * (2 * EXPERTS_PER_GROUP - 1 - a) * 0.5 + (b - a - 1.0)
        return gidx * N_PAIRS + pair

    cls = [classify(lg) for lg in lgs]

    r = lax.broadcasted_iota(jnp.int32, (SUB_TILE, SUB_TILE), 0)
    c = lax.broadcasted_iota(jnp.int32, (SUB_TILE, SUB_TILE), 1)
    earlier = jnp.where(r < c, 1.0, 0.0).astype(jnp.bfloat16)
    crow = lax.broadcasted_iota(jnp.int32, (ROUTE_LANES, SUB_TILE), 0).astype(jnp.float32)
    oh = [crow == cl for cl in cls]
    ohf = [jnp.where(o_, 1.0, 0.0) for o_ in oh]
    inside = [dot(o_.astype(jnp.bfloat16), earlier) for o_ in ohf]
    seen = carry_ref[...]
    for i in subs:
        rank = jnp.sum(jnp.where(oh[i], inside[i] + seen, 0.0), axis=0, keepdims=True)
        slab_ref[:, i * SUB_TILE:(i + 1) * SUB_TILE] = jnp.where(
            sub == 0, cls[i], jnp.where(sub == 1, rank, 0.0))
        seen = seen + jnp.sum(ohf[i], axis=1, keepdims=True)
    carry_ref[...] = seen
    counts_ref[...] = seen


def _attn_route(x1, kt, v, g_x, w_q, w_o, g_f, w_rt, b_rt):
    b, s, d = x1.shape
    m = v.shape[1]
    ts = min(TOKEN_TILE, s)
    nt = s // ts
    const2 = lambda i, j: (0, 0)
    return pl.pallas_call(
        _attn_route_kernel,
        grid=(b, nt),
        in_specs=[
            pl.BlockSpec((1, ts, d), lambda i, j: (i, j, 0)),
            pl.BlockSpec((1, d, m), lambda i, j: (i, 0, 0)),
            pl.BlockSpec((1, m, d), lambda i, j: (i, 0, 0)),
            pl.BlockSpec(g_x.shape, const2),
            pl.BlockSpec(w_q.shape, const2),
            pl.BlockSpec(w_o.shape, const2),
            pl.BlockSpec(g_f.shape, const2),
            pl.BlockSpec(w_rt.shape, const2),
            pl.BlockSpec(b_rt.shape, const2),
        ],
        out_specs=[
            pl.BlockSpec((1, ts, d), lambda i, j: (i, j, 0)),
            pl.BlockSpec((SUBLANES, ts), lambda i, j: (0, i * nt + j)),
            pl.BlockSpec((ROUTE_LANES, 1), const2),
        ],
        out_shape=[
            jax.ShapeDtypeStruct((b, s, d), jnp.float32),
            jax.ShapeDtypeStruct((SUBLANES, b * s), jnp.float32),
            jax.ShapeDtypeStruct((ROUTE_LANES, 1), jnp.float32),
        ],
        scratch_shapes=[pltpu.VMEM((ROUTE_LANES, 1), jnp.float32)],
        compiler_params=pltpu.CompilerParams(
            dimension_semantics=("arbitrary", "arbitrary"), vmem_limit_bytes=VMEM_LIMIT_BYTES),
        name="attn_route",
    )(x1, kt, v, g_x, w_q, w_o, g_f, w_rt, b_rt)


def _dest_kernel(slab_ref, starts_ref, o_ref):
    slab = slab_ref[...]
    tt = slab.shape[1]
    crow = lax.broadcasted_iota(jnp.int32, (ROUTE_LANES, tt), 0).astype(jnp.float32)
    start = jnp.sum(jnp.where(crow == slab[0:1, :], starts_ref[...], 0.0), axis=0, keepdims=True)
    sub = lax.broadcasted_iota(jnp.int32, slab.shape, 0)
    o_ref[...] = jnp.where(sub == 0, start + slab[1:2, :], 0.0).astype(jnp.int32)


def _dest_rows(slab, starts_col):
    t = slab.shape[1]
    tt = min(2048, t)
    return pl.pallas_call(
        _dest_kernel,
        grid=(t // tt,),
        in_specs=[pl.BlockSpec((SUBLANES, tt), lambda i: (0, i)),
                  pl.BlockSpec((ROUTE_LANES, 1), lambda i: (0, 0))],
        out_specs=pl.BlockSpec((SUBLANES, tt), lambda i: (0, i)),
        out_shape=jax.ShapeDtypeStruct((SUBLANES, t), jnp.int32),
        compiler_params=pltpu.CompilerParams(dimension_semantics=("arbitrary",)),
        name="dest_rows",
    )(slab, starts_col)


def _sc_workers():
    info = plsc.get_sparse_core_info()
    return info.num_cores, info.num_subcores


def _dispatch(x2, dest, cap):
    t, d = x2.shape
    nc, ns = _sc_workers()
    per_w = t // (nc * ns)
    ch = SC_GATHER_ROWS
    nchunk = per_w // ch
    mesh = plsc.VectorSubcoreMesh(core_axis_name="c", subcore_axis_name="s")
    dma = pltpu.SemaphoreType.DMA

    @functools.partial(
        pl.kernel, mesh=mesh,
        out_type=jax.ShapeDtypeStruct((cap, d), jnp.float32),
        scratch_types=[pltpu.VMEM((ch,), jnp.int32), pltpu.VMEM((ch,), jnp.int32),
                       pltpu.VMEM((ch, d), jnp.float32), pltpu.VMEM((ch, d), jnp.float32),
                       dma, dma, dma, dma, dma, dma],
    )
    def dispatch_sc(x_hbm, dest_hbm, xs_hbm, idx0, idx1, rows0, rows1, i0, i1, r0, r1, s0, s1):
        wid = lax.axis_index("s") * nc + lax.axis_index("c")
        base = wid * per_w
        bufs = ((idx0, rows0, i0, r0, s0), (idx1, rows1, i1, r1, s1))

        def read(c, buf):
            rows = pl.ds(base + c * ch, ch)
            return (pltpu.make_async_copy(dest_hbm.at[rows], buf[0], buf[2]),
                    pltpu.make_async_copy(x_hbm.at[rows], buf[1], buf[3]))

        def scatter(buf):
            return pltpu.make_async_copy(buf[1], xs_hbm.at[buf[0]], buf[4])

        for cp in read(0, bufs[0]):
            cp.start()

        @pl.loop(0, nchunk, step=2)
        def _(c):
            for b in range(2):
                cur, nxt = bufs[b], bufs[1 - b]
                cc = c + b

                @pl.when(cc >= 1)
                def _():
                    scatter(nxt).wait()

                @pl.when(cc + 1 < nchunk)
                def _():
                    for cp in read(cc + 1, nxt):
                        cp.start()

                for cp in read(cc, cur):
                    cp.wait()
                scatter(cur).start()

        scatter(bufs[(nchunk - 1) % 2]).wait()

    return dispatch_sc(x2, dest)


def _unsort(ys, dest, t):
    d = ys.shape[1]
    nc, ns = _sc_workers()
    per_w = t // (nc * ns)
    ch = SC_GATHER_ROWS
    nchunk = per_w // ch
    mesh = plsc.VectorSubcoreMesh(core_axis_name="c", subcore_axis_name="s")

    @functools.partial(
        pl.kernel, mesh=mesh,
        out_type=jax.ShapeDtypeStruct((t, d), jnp.float32),
        scratch_types=[pltpu.VMEM((per_w,), jnp.int32),
                       pltpu.VMEM((ch, d), jnp.float32), pltpu.VMEM((ch, d), jnp.float32),
                       pltpu.SemaphoreType.DMA, pltpu.SemaphoreType.DMA,
                       pltpu.SemaphoreType.DMA, pltpu.SemaphoreType.DMA],
    )
    def unsort_sc(ys_hbm, dest_hbm, out_hbm, idx_v, rows0, rows1, g0, g1, w0, w1):
        wid = lax.axis_index("s") * nc + lax.axis_index("c")
        base = wid * per_w
        pltpu.sync_copy(dest_hbm.at[pl.ds(base, per_w)], idx_v)
        bufs = ((rows0, g0, w0), (rows1, g1, w1))

        def gather(c, buf):
            return pltpu.make_async_copy(ys_hbm.at[idx_v.at[pl.ds(c * ch, ch)]], buf[0], buf[1])

        def write(c, buf):
            return pltpu.make_async_copy(buf[0], out_hbm.at[pl.ds(base + c * ch, ch)], buf[2])

        gather(0, bufs[0]).start()

        @pl.loop(0, nchunk, step=2)
        def _(c):
            for b in range(2):
                cur, nxt = bufs[b], bufs[1 - b]
                cc = c + b

                @pl.when(cc >= 1)
                def _():
                    write(cc - 1, nxt).wait()

                @pl.when(cc + 1 < nchunk)
                def _():
                    gather(cc + 1, nxt).start()

                gather(cc, cur).wait()
                write(cc, cur).start()

        write(nchunk - 1, bufs[(nchunk - 1) % 2]).wait()

    return unsort_sc(ys, dest)


def _expert_kernel(grp_ref, ea_ref, eb_ref, nvalid_ref, nact_ref,
                   xs_ref, gf_ref, wr_ref, br_ref, gfin_ref, wg_hbm, wu_hbm, wd_hbm,
                   o_ref, wgb, wub, wdb, sg, su, sd, sem, cnt_ref):
    s = pl.program_id(0)
    blocks = tuple(BLOCKS_PER_STEP * s + i for i in range(BLOCKS_PER_STEP))
    g = grp_ref[blocks[0]]

    def stage(e):
        slot = e % 2
        return (pltpu.make_async_copy(wg_hbm.at[e], sg.at[slot], sem.at[slot, 0]),
                pltpu.make_async_copy(wu_hbm.at[e], su.at[slot], sem.at[slot, 1]),
                pltpu.make_async_copy(wd_hbm.at[e], sd.at[slot], sem.at[slot, 2]))

    def start_next():
        @pl.when(cnt_ref[0] < N_EXPERTS)
        def _():
            for cp in stage(cnt_ref[0]):
                cp.start()
            cnt_ref[0] = cnt_ref[0] + 1

    @pl.when(s == 0)
    def _():
        cnt_ref[0] = 0
        cnt_ref[1] = 0
        start_next()
        start_next()

    active = blocks[0] < nact_ref[0]

    @pl.when(active)
    def _():
        need = g * EXPERTS_PER_GROUP + functools.reduce(
            jnp.maximum, [eb_ref[j] for j in blocks])

        def load(e, carry):
            for cp in stage(e):
                cp.wait()
            slot = e % 2
            k = e % EXPERTS_PER_GROUP
            wgb[k] = sg[slot].astype(jnp.bfloat16)
            wub[k] = su[slot].astype(jnp.bfloat16)
            wdb[k] = sd[slot].astype(jnp.bfloat16)
            cnt_ref[1] = e + 1
            start_next()
            return carry

        lax.fori_loop(cnt_ref[1], need + 1, load, 0)

        lane = lax.broadcasted_iota(jnp.int32, (ROW_BLOCK, ROUTE_LANES), 1)
        is_g = lane < N_GROUPS
        lo = EXPERT_LANE0 + EXPERTS_PER_GROUP * g
        nblk = len(blocks)
        rowid = lax.broadcasted_iota(jnp.int32, (ROW_BLOCK, 1), 0)
        xs = [jnp.where(rowid < nvalid_ref[blocks[i]],
                        xs_ref[i * ROW_BLOCK:(i + 1) * ROW_BLOCK, :], 0.0) for i in range(nblk)]
        h3 = [_rms(x, gf_ref[...]).astype(jnp.bfloat16) for x in xs]
        lg = [jnp.dot(h, wr_ref[...], preferred_element_type=jnp.float32) + br_ref[...]
              for h in h3]

        def gates(lgi, j):
            def pick(idx):
                return jnp.sum(jnp.where(lane == idx, lgi, 0.0), axis=-1, keepdims=True)

            gmax = jnp.max(jnp.where(is_g, lgi, _NEG), axis=-1, keepdims=True)
            den = jnp.sum(jnp.where(is_g, jnp.exp(jnp.where(is_g, lgi, _NEG) - gmax), 0.0),
                          axis=-1, keepdims=True)
            grp_p = jnp.exp(pick(g) - gmax) / den
            la = pick(lo + ea_ref[j])
            lb = pick(lo + eb_ref[j])
            m = jnp.maximum(la, lb)
            pa = jnp.exp(la - m)
            pb = jnp.exp(lb - m)
            return grp_p * pa / (pa + pb), grp_p * pb / (pa + pb)

        gate = [gates(lg[i], blocks[i]) for i in range(nblk)]
        ys = list(xs)
        for which, e_ref in enumerate((ea_ref, eb_ref)):
            ks = [e_ref[j] for j in blocks]
            gg = [jnp.dot(h3[i], wgb[ks[i]], preferred_element_type=jnp.float32)
                  for i in range(nblk)]
            uu = [jnp.dot(h3[i], wub[ks[i]], preferred_element_type=jnp.float32)
                  for i in range(nblk)]
            act = [(gg[i] * (1.0 / (1.0 + jnp.exp(-gg[i]))) * uu[i]
                    * gate[i][which]).astype(jnp.bfloat16) for i in range(nblk)]
            ys = [ys[i] + jnp.dot(act[i], wdb[ks[i]], preferred_element_type=jnp.float32)
                  for i in range(nblk)]
        for i in range(nblk):
            o_ref[i * ROW_BLOCK:(i + 1) * ROW_BLOCK, :] = _rms(ys[i], gfin_ref[...])

    @pl.when(jnp.logical_not(active))
    def _():
        o_ref[...] = jnp.zeros_like(o_ref)

    @pl.when(s == pl.num_programs(0) - 1)
    def _():
        def drain(e, carry):
            for cp in stage(e):
                cp.wait()
            return carry

        lax.fori_loop(cnt_ref[1], cnt_ref[0], drain, 0)


def _experts(xs, blk_grp, blk_a, blk_b, nvalid, nact, g_ffn, w_r, b_r, g_final, w_gate, w_up, w_down):
    cap, d = xs.shape
    de = w_gate.shape[2]
    step_rows = BLOCKS_PER_STEP * ROW_BLOCK
    steps = cap // step_rows
    pre = lambda f: (lambda s, gr, ea, eb, nv, na: f(s, na))
    const2 = pre(lambda s, na: (0, 0))
    last_step = lambda na: (na[0] - 1) // BLOCKS_PER_STEP
    hbm = pl.BlockSpec(memory_space=pl.ANY)
    grid_spec = pltpu.PrefetchScalarGridSpec(
        num_scalar_prefetch=5,
        grid=(steps,),
        in_specs=[
            pl.BlockSpec((step_rows, d), pre(lambda s, na: (jnp.minimum(s, last_step(na)), 0))),
            pl.BlockSpec(g_ffn.shape, const2),
            pl.BlockSpec(w_r.shape, const2),
            pl.BlockSpec(b_r.shape, const2),
            pl.BlockSpec(g_final.shape, const2),
            hbm, hbm, hbm,
        ],
        out_specs=pl.BlockSpec((step_rows, d), pre(lambda s, na: (s, 0))),
        scratch_shapes=[
            pltpu.VMEM((EXPERTS_PER_GROUP, d, de), jnp.bfloat16),
            pltpu.VMEM((EXPERTS_PER_GROUP, d, de), jnp.bfloat16),
            pltpu.VMEM((EXPERTS_PER_GROUP, de, d), jnp.bfloat16),
            pltpu.VMEM((2, d, de), jnp.float32),
            pltpu.VMEM((2, d, de), jnp.float32),
            pltpu.VMEM((2, de, d), jnp.float32),
            pltpu.SemaphoreType.DMA((2, 3)),
            pltpu.SMEM((2,), jnp.int32),
        ],
    )
    return pl.pallas_call(
        _expert_kernel,
        grid_spec=grid_spec,
        out_shape=jax.ShapeDtypeStruct((cap, d), jnp.float32),
        compiler_params=pltpu.CompilerParams(
            dimension_semantics=("arbitrary",), vmem_limit_bytes=VMEM_LIMIT_BYTES),
        name="experts",
    )(blk_grp, blk_a, blk_b, nvalid, nact, xs, g_ffn, w_r, b_r, g_final, w_gate, w_up, w_down)


def _moe_final(x2, slab, counts_col, g_ffn, w_r, b_r, g_final, w_gate, w_up, w_down):
    t, d = x2.shape
    nb = t // ROW_BLOCK + N_CLASSES + N_GROUPS * (BLOCKS_PER_STEP - 1)
    assert nb % BLOCKS_PER_STEP == 0
    counts = counts_col[:N_CLASSES, 0].astype(jnp.int32)
    nblk = (counts + ROW_BLOCK - 1) // ROW_BLOCK
    grp_blocks = jnp.sum(nblk.reshape(N_GROUPS, N_PAIRS), axis=1)
    nblk = nblk.reshape(N_GROUPS, N_PAIRS).at[:, N_PAIRS - 1].add(
        (-grp_blocks) % BLOCKS_PER_STEP).reshape(N_CLASSES)
    blk_end = jnp.cumsum(nblk)
    blk_start = blk_end - nblk
    nact = blk_end[-1]
    j = jnp.arange(nb, dtype=jnp.int32)
    blk_class = jnp.minimum(
        jnp.sum((blk_end[None, :] <= j[:, None]).astype(jnp.int32), axis=1), N_CLASSES - 1)
    blk_class = jnp.where(j < nact, blk_class, blk_class[jnp.maximum(nact - 1, 0)])
    nvalid = jnp.where(j < nact, jnp.clip(
        counts[blk_class] - (j - blk_start[blk_class]) * ROW_BLOCK, 0, ROW_BLOCK), 0)
    pair_a = jnp.array([p[0] for p in PAIRS], jnp.int32)
    pair_b = jnp.array([p[1] for p in PAIRS], jnp.int32)
    blk_grp = blk_class // N_PAIRS
    blk_a = pair_a[blk_class % N_PAIRS]
    blk_b = pair_b[blk_class % N_PAIRS]
    starts_col = jnp.zeros((ROUTE_LANES, 1), jnp.float32).at[:N_CLASSES, 0].set(
        (blk_start * ROW_BLOCK).astype(jnp.float32))

    dest = _dest_rows(slab, starts_col)[0]
    xs = _dispatch(x2, dest, nb * ROW_BLOCK)
    ys = _experts(xs, blk_grp, blk_a, blk_b, nvalid, nact[None], g_ffn, w_r, b_r, g_final,
                  w_gate, w_up, w_down)
    return _unsort(ys, dest, t)


def kernel(x, mem, g_mix, w_in, conv_w, g_v, w_s, b_s, g_out_conv, g_out_gmlp, w_out, g_xattn,
           g_mem, w_q, w_k, w_v, w_o, g_ffn, w_grp, b_grp, w_rt, b_rt, w_gate, w_up, w_down,
           g_final):
    b, s, d = x.shape
    assert g_mix.shape[0] == 1, "the final norm is fused into the single layer's expert kernel"
    assert N_CLASSES <= ROUTE_LANES
    bf = jnp.bfloat16
    kt, v = _kv_proj(mem, g_mem[0][None], w_k[0].astype(bf), w_v[0].astype(bf))

    bias = jnp.repeat(b_s[0].T, GMLP_HEAD_DIM, axis=1)
    x1 = _mixer(x, g_mix[0][None], w_in[0].astype(bf), conv_w[0], g_v[0][None], w_s[0], bias,
                g_out_conv[0][None], g_out_gmlp[0][None], w_out[0].astype(bf))

    pad = ROUTE_LANES - N_GROUPS - N_EXPERTS
    w_r = jnp.concatenate([w_grp[0], w_rt[0], jnp.zeros((d, pad), jnp.float32)], axis=1).astype(bf)
    b_r = jnp.concatenate([b_grp[0], b_rt[0], jnp.zeros((pad,), jnp.float32)])[None]
    gap = ROUTE_EXPERT_ROW0 - N_GROUPS
    tail = ROUTE_LANES - ROUTE_EXPERT_ROW0 - N_EXPERTS
    w_r_t = jnp.concatenate([w_grp[0].T, jnp.zeros((gap, d), jnp.float32), w_rt[0].T,
                             jnp.zeros((tail, d), jnp.float32)], axis=0).astype(bf)
    b_r_t = jnp.concatenate([b_grp[0], jnp.zeros((gap,), jnp.float32), b_rt[0],
                             jnp.zeros((tail,), jnp.float32)])[:, None]
    x2, slab, counts_col = _attn_route(x1, kt, v, g_xattn[0][None], w_q[0].astype(bf),
                                       w_o[0].astype(bf), g_ffn[0][None], w_r_t, b_r_t)
    out = _moe_final(x2.reshape(b * s, d), slab, counts_col, g_ffn[0][None], w_r, b_r,
                     g_final[None], w_gate[0], w_up[0], w_down[0])
    return out.reshape(b, s, d)
```

```python
import functools

import jax
import jax.numpy as jnp
from jax import lax
from jax.experimental import pallas as pl
from jax.experimental.pallas import tpu as pltpu
from jax.experimental.pallas import tpu_sc as plsc

EPS = 1e-6
CONV_GROUP_WIDTH = 512
GMLP_HEADS = 8
GMLP_HEAD_DIM = 64
CHUNK = 128
XA_HEADS = 4
N_GROUPS = 4
EXPERTS_PER_GROUP = 8
N_EXPERTS = N_GROUPS * EXPERTS_PER_GROUP
ROUTE_LANES = 128
EXPERT_LANE0 = N_GROUPS
ROUTE_EXPERT_ROW0 = 8

N_PAIRS = EXPERTS_PER_GROUP * (EXPERTS_PER_GROUP - 1) // 2
N_CLASSES = N_GROUPS * N_PAIRS
PAIRS = [(a, b) for a in range(EXPERTS_PER_GROUP) for b in range(a + 1, EXPERTS_PER_GROUP)]

TOKEN_TILE = 1024
SUB_TILE = 256
ROW_BLOCK = 128
BLOCKS_PER_STEP = 4
SC_GATHER_ROWS = 32
VMEM_LIMIT_BYTES = 56 * 1024 * 1024

SUBLANES = 8

_NEG = -1e30


def _rms(x, g):
    return x * lax.rsqrt(jnp.mean(x * x, axis=-1, keepdims=True) + EPS) * g


def _gelu_tanh(x):
    return 0.5 * x * (1.0 + jnp.tanh(0.7978845608028654 * (x + 0.044715 * (x * x * x))))


def _kv_kernel(mem_ref, g_ref, wk_ref, wv_ref, kt_ref, v_ref):
    m = _rms(mem_ref[0], g_ref[...]).astype(jnp.bfloat16)
    k = jnp.dot(m, wk_ref[...], preferred_element_type=jnp.float32)
    v = jnp.dot(m, wv_ref[...], preferred_element_type=jnp.float32)
    kt_ref[0] = k.T.astype(jnp.bfloat16)
    v_ref[0] = v.astype(jnp.bfloat16)


def _kv_proj(mem, g_mem, w_k, w_v):
    b, m, d = mem.shape
    const = lambda i: (0, 0)
    return pl.pallas_call(
        _kv_kernel,
        grid=(b,),
        in_specs=[
            pl.BlockSpec((1, m, d), lambda i: (i, 0, 0)),
            pl.BlockSpec((1, d), const),
            pl.BlockSpec((d, d), const),
            pl.BlockSpec((d, d), const),
        ],
        out_specs=[
            pl.BlockSpec((1, d, m), lambda i: (i, 0, 0)),
            pl.BlockSpec((1, m, d), lambda i: (i, 0, 0)),
        ],
        out_shape=[
            jax.ShapeDtypeStruct((b, d, m), jnp.bfloat16),
            jax.ShapeDtypeStruct((b, m, d), jnp.bfloat16),
        ],
        compiler_params=pltpu.CompilerParams(
            dimension_semantics=("arbitrary",), vmem_limit_bytes=VMEM_LIMIT_BYTES),
        name="kv_proj",
    )(mem, g_mem, w_k, w_v)


def _mixer_kernel(x_ref, gmix_ref, win_ref, convw_ref, gv_ref, ws_ref, bias_ref,
                  goc_ref, gog_ref, wout_ref, o_ref, zbuf_ref):
    ts = x_ref.shape[1]
    w = CONV_GROUP_WIDTH
    hw = 2 * GMLP_HEAD_DIM
    nsub = ts // SUB_TILE

    def dot(a, b):
        return jnp.dot(a, b, preferred_element_type=jnp.float32)

    @pl.when(pl.program_id(1) == 0)
    def _():
        zbuf_ref[0:8, :] = jnp.zeros((8, w), jnp.float32)

    @pl.when(pl.program_id(1) != 0)
    def _():
        zbuf_ref[0:8, :] = zbuf_ref[ts:ts + 8, :]

    low = lax.broadcasted_iota(jnp.int32, (SUB_TILE, hw), 1) < GMLP_HEAD_DIM
    row = lax.broadcasted_iota(jnp.int32, (CHUNK, CHUNK), 0)
    colid = lax.broadcasted_iota(jnp.int32, (CHUNK, CHUNK), 1)
    causal = row >= colid
    ws = [jnp.where(causal, ws_ref[hd], 0.0).astype(jnp.bfloat16) for hd in range(GMLP_HEADS)]
    npair = SUB_TILE // CHUNK // 2
    lo = lax.broadcasted_iota(jnp.int32, (CHUNK, hw), 1) < GMLP_HEAD_DIM
    swap = lambda a: pltpu.roll(a, GMLP_HEAD_DIM, axis=1)

    def project(i):
        r0 = i * SUB_TILE
        xt = x_ref[0, r0:r0 + SUB_TILE, :]
        h = _rms(xt, gmix_ref[...]).astype(jnp.bfloat16)
        p = [dot(h, win_ref[:, k * w:(k + 1) * w]) for k in range(5)]
        z = p[1] * p[2]
        zbuf_ref[8 + r0:8 + r0 + SUB_TILE, :] = z
        return dict(xt=xt, gate_b=p[0], z=z, u=p[3], v=p[4], r0=r0)

    def gate_and_norm(st):
        r0 = st["r0"]
        zc = (convw_ref[0:1, :] * zbuf_ref[6 + r0:6 + r0 + SUB_TILE, :]
              + convw_ref[1:2, :] * zbuf_ref[7 + r0:7 + r0 + SUB_TILE, :]
              + convw_ref[2:3, :] * st["z"])
        st["ya"] = _rms(st["gate_b"] * zc, goc_ref[...]).astype(jnp.bfloat16)
        st["u"] = _gelu_tanh(st["u"])
        v = _gelu_tanh(st["v"])
        v2 = v * v
        ss_cols = []
        for k in range(GMLP_HEADS // 2):
            col = v2[:, hw * k:hw * (k + 1)]
            ss_cols.append(jnp.where(low,
                                     jnp.sum(jnp.where(low, col, 0.0), axis=1, keepdims=True),
                                     jnp.sum(jnp.where(low, 0.0, col), axis=1, keepdims=True)))
        ss = jnp.concatenate(ss_cols, axis=1)
        st["vn"] = v * lax.rsqrt(ss * (1.0 / GMLP_HEAD_DIM) + EPS) * gv_ref[...]

    def mix_positions(st):
        vn = st["vn"]
        s_cols = [[None] * (GMLP_HEADS // 2) for _ in range(2 * npair)]
        for hp in range(GMLP_HEADS // 2):
            cols = [vn[c * CHUNK:(c + 1) * CHUNK, hw * hp:hw * (hp + 1)] for c in range(2 * npair)]
            swapped = [swap(a) for a in cols]
            rhs_even = jnp.concatenate(
                [jnp.where(lo, cols[2 * p], swapped[2 * p + 1]) for p in range(npair)], axis=1)
            rhs_odd = jnp.concatenate(
                [jnp.where(lo, swapped[2 * p], cols[2 * p + 1]) for p in range(npair)], axis=1)
            out_e = dot(ws[2 * hp], rhs_even.astype(jnp.bfloat16))
            out_o = dot(ws[2 * hp + 1], rhs_odd.astype(jnp.bfloat16))
            for p in range(npair):
                e = out_e[:, hw * p:hw * (p + 1)]
                o = out_o[:, hw * p:hw * (p + 1)]
                s_cols[2 * p][hp] = jnp.where(lo, e, swap(o))
                s_cols[2 * p + 1][hp] = jnp.where(lo, swap(e), o)
        st["s"] = jnp.concatenate(
            [jnp.concatenate(c, axis=1) + bias_ref[...] for c in s_cols], axis=0)

    def output(st):
        r0 = st["r0"]
        yb = _rms(st["u"] * st["s"], gog_ref[...]).astype(jnp.bfloat16)
        o_ref[0, r0:r0 + SUB_TILE, :] = (st["xt"] + dot(st["ya"], wout_ref[0:w, :])
                                          + dot(yb, wout_ref[w:2 * w, :]))

    phases = (gate_and_norm, mix_positions, output)
    states = []
    for step in range(nsub + len(phases)):
        if step < nsub:
            states.append(project(step))
        for k, phase in enumerate(phases):
            i = step - 1 - k
            if 0 <= i < nsub:
                phase(states[i])


def _mixer(x, g_mix, w_in, conv_w, g_v, w_s, bias, g_oc, g_og, w_out):
    b, s, d = x.shape
    ts = min(TOKEN_TILE, s)
    const2 = lambda i, j: (0, 0)
    const3 = lambda i, j: (0, 0, 0)
    return pl.pallas_call(
        _mixer_kernel,
        grid=(b, s // ts),
        in_specs=[
            pl.BlockSpec((1, ts, d), lambda i, j: (i, j, 0)),
            pl.BlockSpec(g_mix.shape, const2),
            pl.BlockSpec(w_in.shape, const2),
            pl.BlockSpec(conv_w.shape, const2),
            pl.BlockSpec(g_v.shape, const2),
            pl.BlockSpec(w_s.shape, const3),
            pl.BlockSpec(bias.shape, const2),
            pl.BlockSpec(g_oc.shape, const2),
            pl.BlockSpec(g_og.shape, const2),
            pl.BlockSpec(w_out.shape, const2),
        ],
        out_specs=pl.BlockSpec((1, ts, d), lambda i, j: (i, j, 0)),
        out_shape=jax.ShapeDtypeStruct((b, s, d), jnp.float32),
        scratch_shapes=[pltpu.VMEM((ts + 8, CONV_GROUP_WIDTH), jnp.float32)],
        compiler_params=pltpu.CompilerParams(
            dimension_semantics=("arbitrary", "arbitrary"), vmem_limit_bytes=VMEM_LIMIT_BYTES),
        name="mixer",
    )(x, g_mix, w_in, conv_w, g_v, w_s, bias, g_oc, g_og, w_out)


def _attn_route_kernel(x_ref, kt_ref, v_ref, gx_ref, wq_ref, wo_ref, gf_ref, wrt_ref, brt_ref,
                       x2_ref, slab_ref, counts_ref, carry_ref):
    ts = x_ref.shape[1]
    d = x_ref.shape[2]
    hd = d // XA_HEADS
    nsub = ts // SUB_TILE
    first = (pl.program_id(0) == 0) & (pl.program_id(1) == 0)

    @pl.when(first)
    def _():
        carry_ref[...] = jnp.zeros_like(carry_ref)

    def dot(a, b):
        return jnp.dot(a, b, preferred_element_type=jnp.float32)

    subs = range(nsub)
    x1 = [x_ref[0, i * SUB_TILE:(i + 1) * SUB_TILE, :] for i in subs]
    h2 = [_rms(x, gx_ref[...]).astype(jnp.bfloat16) for x in x1]
    q = [dot(h, wq_ref[...]).astype(jnp.bfloat16) for h in h2]
    heads = [[] for _ in subs]
    for a in range(XA_HEADS):
        sc = [dot(q[i][:, a * hd:(a + 1) * hd], kt_ref[0, a * hd:(a + 1) * hd, :]) * (hd ** -0.5)
              for i in subs]
        p = [jnp.exp(s_ - jnp.max(s_, axis=-1, keepdims=True)) for s_ in sc]
        l = [jnp.sum(p_, axis=-1, keepdims=True) for p_ in p]
        o = [dot(p[i].astype(jnp.bfloat16), v_ref[0, :, a * hd:(a + 1) * hd]) for i in subs]
        for i in subs:
            heads[i].append((o[i] / l[i]).astype(jnp.bfloat16))
    x2 = [x1[i] + dot(jnp.concatenate(heads[i], axis=1), wo_ref[...]) for i in subs]
    for i in subs:
        x2_ref[0, i * SUB_TILE:(i + 1) * SUB_TILE, :] = x2[i]

    h3 = [_rms(x, gf_ref[...]).astype(jnp.bfloat16) for x in x2]
    lgs = [lax.dot_general(wrt_ref[...], h, (((1,), (1,)), ((), ())),
                           preferred_element_type=jnp.float32) + brt_ref[...] for h in h3]
    sub = lax.broadcasted_iota(jnp.int32, (EXPERTS_PER_GROUP, SUB_TILE), 0).astype(jnp.float32)
    big = float(EXPERTS_PER_GROUP)
    is_g = sub < N_GROUPS

    def classify(lg):
        glog = lg[0:EXPERTS_PER_GROUP, :]
        gmax = jnp.max(jnp.where(is_g, glog, _NEG), axis=0, keepdims=True)
        gidx = jnp.min(jnp.where(is_g & (glog == gmax), sub, big), axis=0, keepdims=True)
        el = lg[ROUTE_EXPERT_ROW0:ROUTE_EXPERT_ROW0 + EXPERTS_PER_GROUP, :]
        for grp in range(1, N_GROUPS):
            r0 = ROUTE_EXPERT_ROW0 + grp * EXPERTS_PER_GROUP
            el = jnp.where(gidx == grp, lg[r0:r0 + EXPERTS_PER_GROUP, :], el)
        t1 = jnp.max(el, axis=0, keepdims=True)
        i1 = jnp.min(jnp.where(el == t1, sub, big), axis=0, keepdims=True)
        rest = sub != i1
        t2 = jnp.max(jnp.where(rest, el, _NEG), axis=0, keepdims=True)
        i2 = jnp.min(jnp.where(rest & (el == t2), sub, big), axis=0, keepdims=True)
        a = jnp.minimum(i1, i2)
        b = jnp.maximum(i1, i2)
        pair = a * (2 * EXPERTS_PER_GROUP - 1 - a) * 0.5 + (b - a - 1.0)
        return gidx * N_PAIRS + pair

    cls = [classify(lg) for lg in lgs]

    r = lax.broadcasted_iota(jnp.int32, (SUB_TILE, SUB_TILE), 0)
    c = lax.broadcasted_iota(jnp.int32, (SUB_TILE, SUB_TILE), 1)
    earlier = jnp.where(r < c, 1.0, 0.0).astype(jnp.bfloat16)
    crow = lax.broadcasted_iota(jnp.int32, (ROUTE_LANES, SUB_TILE), 0).astype(jnp.float32)
    oh = [crow == cl for cl in cls]
    ohf = [jnp.where(o_, 1.0, 0.0) for o_ in oh]
    inside = [dot(o_.astype(jnp.bfloat16), earlier) for o_ in ohf]
    seen = carry_ref[...]
    for i in subs:
        rank = jnp.sum(jnp.where(oh[i], inside[i] + seen, 0.0), axis=0, keepdims=True)
        slab_ref[:, i * SUB_TILE:(i + 1) * SUB_TILE] = jnp.where(
            sub == 0, cls[i], jnp.where(sub == 1, rank, 0.0))
        seen = seen + jnp.sum(ohf[i], axis=1, keepdims=True)
    carry_ref[...] = seen
    counts_ref[...] = seen


def _attn_route(x1, kt, v, g_x, w_q, w_o, g_f, w_rt, b_rt):
    b, s, d = x1.shape
    m = v.shape[1]
    ts = min(TOKEN_TILE, s)
    nt = s // ts
    const2 = lambda i, j: (0, 0)
    return pl.pallas_call(
        _attn_route_kernel,
        grid=(b, nt),
        in_specs=[
            pl.BlockSpec((1, ts, d), lambda i, j: (i, j, 0)),
            pl.BlockSpec((1, d, m), lambda i, j: (i, 0, 0)),
            pl.BlockSpec((1, m, d), lambda i, j: (i, 0, 0)),
            pl.BlockSpec(g_x.shape, const2),
            pl.BlockSpec(w_q.shape, const2),
            pl.BlockSpec(w_o.shape, const2),
            pl.BlockSpec(g_f.shape, const2),
            pl.BlockSpec(w_rt.shape, const2),
            pl.BlockSpec(b_rt.shape, const2),
        ],
        out_specs=[
            pl.BlockSpec((1, ts, d), lambda i, j: (i, j, 0)),
            pl.BlockSpec((SUBLANES, ts), lambda i, j: (0, i * nt + j)),
            pl.BlockSpec((ROUTE_LANES, 1), const2),
        ],
        out_shape=[
            jax.ShapeDtypeStruct((b, s, d), jnp.float32),
            jax.ShapeDtypeStruct((SUBLANES, b * s), jnp.float32),
            jax.ShapeDtypeStruct((ROUTE_LANES, 1), jnp.float32),
        ],
        scratch_shapes=[pltpu.VMEM((ROUTE_LANES, 1), jnp.float32)],
        compiler_params=pltpu.CompilerParams(
            dimension_semantics=("arbitrary", "arbitrary"), vmem_limit_bytes=VMEM_LIMIT_BYTES),
        name="attn_route",
    )(x1, kt, v, g_x, w_q, w_o, g_f, w_rt, b_rt)


def _dest_kernel(slab_ref, starts_ref, o_ref):
    slab = slab_ref[...]
    tt = slab.shape[1]
    crow = lax.broadcasted_iota(jnp.int32, (ROUTE_LANES, tt), 0).astype(jnp.float32)
    start = jnp.sum(jnp.where(crow == slab[0:1, :], starts_ref[...], 0.0), axis=0, keepdims=True)
    sub = lax.broadcasted_iota(jnp.int32, slab.shape, 0)
    o_ref[...] = jnp.where(sub == 0, start + slab[1:2, :], 0.0).astype(jnp.int32)


def _dest_rows(slab, starts_col):
    t = slab.shape[1]
    tt = min(2048, t)
    return pl.pallas_call(
        _dest_kernel,
        grid=(t // tt,),
        in_specs=[pl.BlockSpec((SUBLANES, tt), lambda i: (0, i)),
                  pl.BlockSpec((ROUTE_LANES, 1), lambda i: (0, 0))],
        out_specs=pl.BlockSpec((SUBLANES, tt), lambda i: (0, i)),
        out_shape=jax.ShapeDtypeStruct((SUBLANES, t), jnp.int32),
        compiler_params=pltpu.CompilerParams(dimension_semantics=("arbitrary",)),
        name="dest_rows",
    )(slab, starts_col)


def _sc_split(t):
    info = plsc.get_sparse_core_info()
    workers = info.num_cores * info.num_subcores
    per_w, rem = divmod(t, workers)
    nchunk, rem2 = divmod(per_w, SC_GATHER_ROWS)
    assert rem == 0 and rem2 == 0 and nchunk % 2 == 0, (t, workers, SC_GATHER_ROWS)
    return info.num_cores, per_w, nchunk


def _dispatch(x2, dest, cap):
    t, d = x2.shape
    nc, per_w, nchunk = _sc_split(t)
    ch = SC_GATHER_ROWS
    mesh = plsc.VectorSubcoreMesh(core_axis_name="c", subcore_axis_name="s")
    dma = pltpu.SemaphoreType.DMA

    @functools.partial(
        pl.kernel, mesh=mesh,
        out_type=jax.ShapeDtypeStruct((cap, d), jnp.float32),
        scratch_types=[pltpu.VMEM((ch,), jnp.int32), pltpu.VMEM((ch,), jnp.int32),
                       pltpu.VMEM((ch, d), jnp.float32), pltpu.VMEM((ch, d), jnp.float32),
                       dma, dma, dma, dma, dma, dma],
    )
    def dispatch_sc(x_hbm, dest_hbm, xs_hbm, idx0, idx1, rows0, rows1, i0, i1, r0, r1, s0, s1):
        wid = lax.axis_index("s") * nc + lax.axis_index("c")
        base = wid * per_w
        bufs = ((idx0, rows0, i0, r0, s0), (idx1, rows1, i1, r1, s1))

        def read(c, buf):
            rows = pl.ds(base + c * ch, ch)
            return (pltpu.make_async_copy(dest_hbm.at[rows], buf[0], buf[2]),
                    pltpu.make_async_copy(x_hbm.at[rows], buf[1], buf[3]))

        def scatter(buf):
            return pltpu.make_async_copy(buf[1], xs_hbm.at[buf[0]], buf[4])

        for cp in read(0, bufs[0]):
            cp.start()

        @pl.loop(0, nchunk, step=2)
        def _(c):
            for b in range(2):
                cur, nxt = bufs[b], bufs[1 - b]
                cc = c + b

                @pl.when(cc >= 1)
                def _():
                    scatter(nxt).wait()

                @pl.when(cc + 1 < nchunk)
                def _():
                    for cp in read(cc + 1, nxt):
                        cp.start()

                for cp in read(cc, cur):
                    cp.wait()
                scatter(cur).start()

        scatter(bufs[(nchunk - 1) % 2]).wait()

    return dispatch_sc(x2, dest)


def _unsort(ys, dest, t):
    d = ys.shape[1]
    nc, per_w, nchunk = _sc_split(t)
    ch = SC_GATHER_ROWS
    mesh = plsc.VectorSubcoreMesh(core_axis_name="c", subcore_axis_name="s")

    @functools.partial(
        pl.kernel, mesh=mesh,
        out_type=jax.ShapeDtypeStruct((t, d), jnp.float32),
        scratch_types=[pltpu.VMEM((per_w,), jnp.int32),
                       pltpu.VMEM((ch, d), jnp.float32), pltpu.VMEM((ch, d), jnp.float32),
                       pltpu.SemaphoreType.DMA, pltpu.SemaphoreType.DMA,
                       pltpu.SemaphoreType.DMA, pltpu.SemaphoreType.DMA],
    )
    def unsort_sc(ys_hbm, dest_hbm, out_hbm, idx_v, rows0, rows1, g0, g1, w0, w1):
        wid = lax.axis_index("s") * nc + lax.axis_index("c")
        base = wid * per_w
        pltpu.sync_copy(dest_hbm.at[pl.ds(base, per_w)], idx_v)
        bufs = ((rows0, g0, w0), (rows1, g1, w1))

        def gather(c, buf):
            return pltpu.make_async_copy(ys_hbm.at[idx_v.at[pl.ds(c * ch, ch)]], buf[0], buf[1])

        def write(c, buf):
            return pltpu.make_async_copy(buf[0], out_hbm.at[pl.ds(base + c * ch, ch)], buf[2])

        gather(0, bufs[0]).start()

        @pl.loop(0, nchunk, step=2)
        def _(c):
            for b in range(2):
                cur, nxt = bufs[b], bufs[1 - b]
                cc = c + b

                @pl.when(cc >= 1)
                def _():
                    write(cc - 1, nxt).wait()

                @pl.when(cc + 1 < nchunk)
                def _():
                    gather(cc + 1, nxt).start()

                gather(cc, cur).wait()
                write(cc, cur).start()

        write(nchunk - 1, bufs[(nchunk - 1) % 2]).wait()

    return unsort_sc(ys, dest)


def _expert_kernel(grp_ref, ea_ref, eb_ref, nvalid_ref, nact_ref,
                   xs_ref, gf_ref, wr_ref, br_ref, gfin_ref, wg_hbm, wu_hbm, wd_hbm,
                   o_ref, wgb, wub, wdb, sg, su, sd, sem, cnt_ref):
    s = pl.program_id(0)
    blocks = tuple(BLOCKS_PER_STEP * s + i for i in range(BLOCKS_PER_STEP))
    g = grp_ref[blocks[0]]

    def stage(e):
        slot = e % 2
        return (pltpu.make_async_copy(wg_hbm.at[e], sg.at[slot], sem.at[slot, 0]),
                pltpu.make_async_copy(wu_hbm.at[e], su.at[slot], sem.at[slot, 1]),
                pltpu.make_async_copy(wd_hbm.at[e], sd.at[slot], sem.at[slot, 2]))

    def start_next():
        @pl.when(cnt_ref[0] < N_EXPERTS)
        def _():
            for cp in stage(cnt_ref[0]):
                cp.start()
            cnt_ref[0] = cnt_ref[0] + 1

    @pl.when(s == 0)
    def _():
        cnt_ref[0] = 0
        cnt_ref[1] = 0
        start_next()
        start_next()

    active = blocks[0] < nact_ref[0]

    @pl.when(active)
    def _():
        need = g * EXPERTS_PER_GROUP + functools.reduce(
            jnp.maximum, [eb_ref[j] for j in blocks])

        def load(e, carry):
            for cp in stage(e):
                cp.wait()
            slot = e % 2
            k = e % EXPERTS_PER_GROUP
            wgb[k] = sg[slot].astype(jnp.bfloat16)
            wub[k] = su[slot].astype(jnp.bfloat16)
            wdb[k] = sd[slot].astype(jnp.bfloat16)
            cnt_ref[1] = e + 1
            start_next()
            return carry

        lax.fori_loop(cnt_ref[1], need + 1, load, 0)

        lane = lax.broadcasted_iota(jnp.int32, (ROW_BLOCK, ROUTE_LANES), 1)
        is_g = lane < N_GROUPS
        lo = EXPERT_LANE0 + EXPERTS_PER_GROUP * g
        nblk = len(blocks)
        rowid = lax.broadcasted_iota(jnp.int32, (ROW_BLOCK, 1), 0)
        xs = [jnp.where(rowid < nvalid_ref[blocks[i]],
                        xs_ref[i * ROW_BLOCK:(i + 1) * ROW_BLOCK, :], 0.0) for i in range(nblk)]
        h3 = [_rms(x, gf_ref[...]).astype(jnp.bfloat16) for x in xs]
        lg = [jnp.dot(h, wr_ref[...], preferred_element_type=jnp.float32) + br_ref[...]
              for h in h3]

        def gates(lgi, j):
            def pick(idx):
                return jnp.sum(jnp.where(lane == idx, lgi, 0.0), axis=-1, keepdims=True)

            gmax = jnp.max(jnp.where(is_g, lgi, _NEG), axis=-1, keepdims=True)
            den = jnp.sum(jnp.where(is_g, jnp.exp(jnp.where(is_g, lgi, _NEG) - gmax), 0.0),
                          axis=-1, keepdims=True)
            grp_p = jnp.exp(pick(g) - gmax) / den
            la = pick(lo + ea_ref[j])
            lb = pick(lo + eb_ref[j])
            m = jnp.maximum(la, lb)
            pa = jnp.exp(la - m)
            pb = jnp.exp(lb - m)
            return grp_p * pa / (pa + pb), grp_p * pb / (pa + pb)

        gate = [gates(lg[i], blocks[i]) for i in range(nblk)]
        ys = list(xs)
        for which, e_ref in enumerate((ea_ref, eb_ref)):
            ks = [e_ref[j] for j in blocks]
            gg = [jnp.dot(h3[i], wgb[ks[i]], preferred_element_type=jnp.float32)
                  for i in range(nblk)]
            uu = [jnp.dot(h3[i], wub[ks[i]], preferred_element_type=jnp.float32)
                  for i in range(nblk)]
            act = [(gg[i] * (1.0 / (1.0 + jnp.exp(-gg[i]))) * uu[i]
                    * gate[i][which]).astype(jnp.bfloat16) for i in range(nblk)]
            ys = [ys[i] + jnp.dot(act[i], wdb[ks[i]], preferred_element_type=jnp.float32)
                  for i in range(nblk)]
        for i in range(nblk):
            o_ref[i * ROW_BLOCK:(i + 1) * ROW_BLOCK, :] = _rms(ys[i], gfin_ref[...])

    @pl.when(jnp.logical_not(active))
    def _():
        o_ref[...] = jnp.zeros_like(o_ref)

    @pl.when(s == pl.num_programs(0) - 1)
    def _():
        def drain(e, carry):
            for cp in stage(e):
                cp.wait()
            return carry

        lax.fori_loop(cnt_ref[1], cnt_ref[0], drain, 0)


def _experts(xs, blk_grp, blk_a, blk_b, nvalid, nact, g_ffn, w_r, b_r, g_final, w_gate, w_up, w_down):
    cap, d = xs.shape
    de = w_gate.shape[2]
    step_rows = BLOCKS_PER_STEP * ROW_BLOCK
    steps = cap // step_rows
    pre = lambda f: (lambda s, gr, ea, eb, nv, na: f(s, na))
    const2 = pre(lambda s, na: (0, 0))
    last_step = lambda na: (na[0] - 1) // BLOCKS_PER_STEP
    hbm = pl.BlockSpec(memory_space=pl.ANY)
    grid_spec = pltpu.PrefetchScalarGridSpec(
        num_scalar_prefetch=5,
        grid=(steps,),
        in_specs=[
            pl.BlockSpec((step_rows, d), pre(lambda s, na: (jnp.minimum(s, last_step(na)), 0))),
            pl.BlockSpec(g_ffn.shape, const2),
            pl.BlockSpec(w_r.shape, const2),
            pl.BlockSpec(b_r.shape, const2),
            pl.BlockSpec(g_final.shape, const2),
            hbm, hbm, hbm,
        ],
        out_specs=pl.BlockSpec((step_rows, d), pre(lambda s, na: (s, 0))),
        scratch_shapes=[
            pltpu.VMEM((EXPERTS_PER_GROUP, d, de), jnp.bfloat16),
            pltpu.VMEM((EXPERTS_PER_GROUP, d, de), jnp.bfloat16),
            pltpu.VMEM((EXPERTS_PER_GROUP, de, d), jnp.bfloat16),
            pltpu.VMEM((2, d, de), jnp.float32),
            pltpu.VMEM((2, d, de), jnp.float32),
            pltpu.VMEM((2, de, d), jnp.float32),
            pltpu.SemaphoreType.DMA((2, 3)),
            pltpu.SMEM((2,), jnp.int32),
        ],
    )
    return pl.pallas_call(
        _expert_kernel,
        grid_spec=grid_spec,
        out_shape=jax.ShapeDtypeStruct((cap, d), jnp.float32),
        compiler_params=pltpu.CompilerParams(
            dimension_semantics=("arbitrary",), vmem_limit_bytes=VMEM_LIMIT_BYTES),
        name="experts",
    )(blk_grp, blk_a, blk_b, nvalid, nact, xs, g_ffn, w_r, b_r, g_final, w_gate, w_up, w_down)


def _moe_final(x2, slab, counts_col, g_ffn, w_r, b_r, g_final, w_gate, w_up, w_down):
    t, d = x2.shape
    nb = t // ROW_BLOCK + N_CLASSES + N_GROUPS * (BLOCKS_PER_STEP - 1)
    assert nb % BLOCKS_PER_STEP == 0
    counts = counts_col[:N_CLASSES, 0].astype(jnp.int32)
    nblk = (counts + ROW_BLOCK - 1) // ROW_BLOCK
    grp_blocks = jnp.sum(nblk.reshape(N_GROUPS, N_PAIRS), axis=1)
    nblk = nblk.reshape(N_GROUPS, N_PAIRS).at[:, N_PAIRS - 1].add(
        (-grp_blocks) % BLOCKS_PER_STEP).reshape(N_CLASSES)
    blk_end = jnp.cumsum(nblk)
    blk_start = blk_end - nblk
    nact = blk_end[-1]
    j = jnp.arange(nb, dtype=jnp.int32)
    blk_class = jnp.minimum(
        jnp.sum((blk_end[None, :] <= j[:, None]).astype(jnp.int32), axis=1), N_CLASSES - 1)
    blk_class = jnp.where(j < nact, blk_class, blk_class[jnp.maximum(nact - 1, 0)])
    nvalid = jnp.where(j < nact, jnp.clip(
        counts[blk_class] - (j - blk_start[blk_class]) * ROW_BLOCK, 0, ROW_BLOCK), 0)
    pair_a = jnp.array([p[0] for p in PAIRS], jnp.int32)
    pair_b = jnp.array([p[1] for p in PAIRS], jnp.int32)
    blk_grp = blk_class // N_PAIRS
    blk_a = pair_a[blk_class % N_PAIRS]
    blk_b = pair_b[blk_class % N_PAIRS]
    starts_col = jnp.zeros((ROUTE_LANES, 1), jnp.float32).at[:N_CLASSES, 0].set(
        (blk_start * ROW_BLOCK).astype(jnp.float32))

    dest = _dest_rows(slab, starts_col)[0]
    xs = _dispatch(x2, dest, nb * ROW_BLOCK)
    ys = _experts(xs, blk_grp, blk_a, blk_b, nvalid, nact[None], g_ffn, w_r, b_r, g_final,
                  w_gate, w_up, w_down)
    return _unsort(ys, dest, t)


def kernel(x, mem, g_mix, w_in, conv_w, g_v, w_s, b_s, g_out_conv, g_out_gmlp, w_out, g_xattn,
           g_mem, w_q, w_k, w_v, w_o, g_ffn, w_grp, b_grp, w_rt, b_rt, w_gate, w_up, w_down,
           g_final):
    b, s, d = x.shape
    assert g_mix.shape[0] == 1, "the final norm is fused into the single layer's expert kernel"
    assert N_CLASSES <= ROUTE_LANES
    bf = jnp.bfloat16
    kt, v = _kv_proj(mem, g_mem[0][None], w_k[0].astype(bf), w_v[0].astype(bf))

    bias = jnp.repeat(b_s[0].T, GMLP_HEAD_DIM, axis=1)
    x1 = _mixer(x, g_mix[0][None], w_in[0].astype(bf), conv_w[0], g_v[0][None], w_s[0], bias,
                g_out_conv[0][None], g_out_gmlp[0][None], w_out[0].astype(bf))

    pad = ROUTE_LANES - N_GROUPS - N_EXPERTS
    w_r = jnp.concatenate([w_grp[0], w_rt[0], jnp.zeros((d, pad), jnp.float32)], axis=1).astype(bf)
    b_r = jnp.concatenate([b_grp[0], b_rt[0], jnp.zeros((pad,), jnp.float32)])[None]
    gap = ROUTE_EXPERT_ROW0 - N_GROUPS
    tail = ROUTE_LANES - ROUTE_EXPERT_ROW0 - N_EXPERTS
    w_r_t = jnp.concatenate([w_grp[0].T, jnp.zeros((gap, d), jnp.float32), w_rt[0].T,
                             jnp.zeros((tail, d), jnp.float32)], axis=0).astype(bf)
    b_r_t = jnp.concatenate([b_grp[0], jnp.zeros((gap,), jnp.float32), b_rt[0],
                             jnp.zeros((tail,), jnp.float32)])[:, None]
    x2, slab, counts_col = _attn_route(x1, kt, v, g_xattn[0][None], w_q[0].astype(bf),
                                       w_o[0].astype(bf), g_ffn[0][None], w_r_t, b_r_t)
    out = _moe_final(x2.reshape(b * s, d), slab, counts_col, g_ffn[0][None], w_r, b_r,
                     g_final[None], w_gate[0], w_up[0], w_down[0])
    return out.reshape(b, s, d)
```

```python
import functools

import jax
import jax.numpy as jnp
from jax import lax
from jax.experimental import pallas as pl
from jax.experimental.pallas import tpu as pltpu
from jax.experimental.pallas import tpu_sc as plsc

EPS = 1e-6
CONV_GROUP_WIDTH = 512
GMLP_HEADS = 8
GMLP_HEAD_DIM = 64
CHUNK = 128
XA_HEADS = 4
N_GROUPS = 4
EXPERTS_PER_GROUP = 8
N_EXPERTS = N_GROUPS * EXPERTS_PER_GROUP
ROUTE_LANES = 128
EXPERT_LANE0 = N_GROUPS
ROUTE_EXPERT_ROW0 = 8

N_PAIRS = EXPERTS_PER_GROUP * (EXPERTS_PER_GROUP - 1) // 2
N_CLASSES = N_GROUPS * N_PAIRS
PAIRS = [(a, b) for a in range(EXPERTS_PER_GROUP) for b in range(a + 1, EXPERTS_PER_GROUP)]

MIXER_TILE = 1024
ATTN_TILE = 2048
SUB_TILE = 256
ROW_BLOCK = 128
BLOCKS_PER_STEP = 4
SC_GATHER_ROWS = 32
VMEM_LIMIT_BYTES = 56 * 1024 * 1024

SUBLANES = 8

_NEG = -1e30


def _rms(x, g):
    return x * lax.rsqrt(jnp.mean(x * x, axis=-1, keepdims=True) + EPS) * g


def _gelu_tanh(x):
    return 0.5 * x * (1.0 + jnp.tanh(0.7978845608028654 * (x + 0.044715 * (x * x * x))))


def _kv_kernel(mem_ref, g_ref, wk_ref, wv_ref, kt_ref, v_ref):
    m = _rms(mem_ref[0], g_ref[...]).astype(jnp.bfloat16)
    k = jnp.dot(m, wk_ref[...], preferred_element_type=jnp.float32)
    v = jnp.dot(m, wv_ref[...], preferred_element_type=jnp.float32)
    kt_ref[0] = k.T.astype(jnp.bfloat16)
    v_ref[0] = v.astype(jnp.bfloat16)


def _kv_proj(mem, g_mem, w_k, w_v):
    b, m, d = mem.shape
    const = lambda i: (0, 0)
    return pl.pallas_call(
        _kv_kernel,
        grid=(b,),
        in_specs=[
            pl.BlockSpec((1, m, d), lambda i: (i, 0, 0)),
            pl.BlockSpec((1, d), const),
            pl.BlockSpec((d, d), const),
            pl.BlockSpec((d, d), const),
        ],
        out_specs=[
            pl.BlockSpec((1, d, m), lambda i: (i, 0, 0)),
            pl.BlockSpec((1, m, d), lambda i: (i, 0, 0)),
        ],
        out_shape=[
            jax.ShapeDtypeStruct((b, d, m), jnp.bfloat16),
            jax.ShapeDtypeStruct((b, m, d), jnp.bfloat16),
        ],
        compiler_params=pltpu.CompilerParams(
            dimension_semantics=("arbitrary",), vmem_limit_bytes=VMEM_LIMIT_BYTES),
        name="kv_proj",
    )(mem, g_mem, w_k, w_v)


def _mixer_kernel(x_ref, gmix_ref, win_ref, convw_ref, gv_ref, ws_ref, bias_ref,
                  goc_ref, gog_ref, wout_ref, o_ref, zbuf_ref):
    ts = x_ref.shape[1]
    w = CONV_GROUP_WIDTH
    hw = 2 * GMLP_HEAD_DIM
    nsub = ts // SUB_TILE

    def dot(a, b):
        return jnp.dot(a, b, preferred_element_type=jnp.float32)

    @pl.when(pl.program_id(1) == 0)
    def _():
        zbuf_ref[0:8, :] = jnp.zeros((8, w), jnp.float32)

    @pl.when(pl.program_id(1) != 0)
    def _():
        zbuf_ref[0:8, :] = zbuf_ref[ts:ts + 8, :]

    low = lax.broadcasted_iota(jnp.int32, (SUB_TILE, hw), 1) < GMLP_HEAD_DIM
    row = lax.broadcasted_iota(jnp.int32, (CHUNK, CHUNK), 0)
    colid = lax.broadcasted_iota(jnp.int32, (CHUNK, CHUNK), 1)
    causal = row >= colid
    ws = [jnp.where(causal, ws_ref[hd], 0.0).astype(jnp.bfloat16) for hd in range(GMLP_HEADS)]
    npair = SUB_TILE // CHUNK // 2
    lo = lax.broadcasted_iota(jnp.int32, (CHUNK, hw), 1) < GMLP_HEAD_DIM
    swap = lambda a: pltpu.roll(a, GMLP_HEAD_DIM, axis=1)

    def project(i):
        r0 = i * SUB_TILE
        xt = x_ref[0, r0:r0 + SUB_TILE, :]
        h = _rms(xt, gmix_ref[...]).astype(jnp.bfloat16)
        p = [dot(h, win_ref[:, k * w:(k + 1) * w]) for k in range(5)]
        z = p[1] * p[2]
        zbuf_ref[8 + r0:8 + r0 + SUB_TILE, :] = z
        return dict(xt=xt, gate_b=p[0], z=z, u=p[3], v=p[4], r0=r0)

    def gate_and_norm(st):
        r0 = st["r0"]
        zc = (convw_ref[0:1, :] * zbuf_ref[6 + r0:6 + r0 + SUB_TILE, :]
              + convw_ref[1:2, :] * zbuf_ref[7 + r0:7 + r0 + SUB_TILE, :]
              + convw_ref[2:3, :] * st["z"])
        st["ya"] = _rms(st["gate_b"] * zc, goc_ref[...]).astype(jnp.bfloat16)
        st["u"] = _gelu_tanh(st["u"])
        v = _gelu_tanh(st["v"])
        v2 = v * v
        ss_cols = []
        for k in range(GMLP_HEADS // 2):
            col = v2[:, hw * k:hw * (k + 1)]
            ss_cols.append(jnp.where(low,
                                     jnp.sum(jnp.where(low, col, 0.0), axis=1, keepdims=True),
                                     jnp.sum(jnp.where(low, 0.0, col), axis=1, keepdims=True)))
        ss = jnp.concatenate(ss_cols, axis=1)
        st["vn"] = v * lax.rsqrt(ss * (1.0 / GMLP_HEAD_DIM) + EPS) * gv_ref[...]

    def mix_positions(st):
        vn = st["vn"]
        s_cols = [[None] * (GMLP_HEADS // 2) for _ in range(2 * npair)]
        for hp in range(GMLP_HEADS // 2):
            cols = [vn[c * CHUNK:(c + 1) * CHUNK, hw * hp:hw * (hp + 1)] for c in range(2 * npair)]
            swapped = [swap(a) for a in cols]
            rhs_even = jnp.concatenate(
                [jnp.where(lo, cols[2 * p], swapped[2 * p + 1]) for p in range(npair)], axis=1)
            rhs_odd = jnp.concatenate(
                [jnp.where(lo, swapped[2 * p], cols[2 * p + 1]) for p in range(npair)], axis=1)
            out_e = dot(ws[2 * hp], rhs_even.astype(jnp.bfloat16))
            out_o = dot(ws[2 * hp + 1], rhs_odd.astype(jnp.bfloat16))
            for p in range(npair):
                e = out_e[:, hw * p:hw * (p + 1)]
                o = out_o[:, hw * p:hw * (p + 1)]
                s_cols[2 * p][hp] = jnp.where(lo, e, swap(o))
                s_cols[2 * p + 1][hp] = jnp.where(lo, swap(e), o)
        st["s"] = jnp.concatenate(
            [jnp.concatenate(c, axis=1) + bias_ref[...] for c in s_cols], axis=0)

    def output(st):
        r0 = st["r0"]
        yb = _rms(st["u"] * st["s"], gog_ref[...]).astype(jnp.bfloat16)
        o_ref[0, r0:r0 + SUB_TILE, :] = (st["xt"] + dot(st["ya"], wout_ref[0:w, :])
                                          + dot(yb, wout_ref[w:2 * w, :]))

    phases = (gate_and_norm, mix_positions, output)
    states = []
    for step in range(nsub + len(phases)):
        if step < nsub:
            states.append(project(step))
        for k, phase in enumerate(phases):
            i = step - 1 - k
            if 0 <= i < nsub:
                phase(states[i])


def _mixer(x, g_mix, w_in, conv_w, g_v, w_s, bias, g_oc, g_og, w_out):
    b, s, d = x.shape
    ts = min(MIXER_TILE, s)
    const2 = lambda i, j: (0, 0)
    const3 = lambda i, j: (0, 0, 0)
    return pl.pallas_call(
        _mixer_kernel,
        grid=(b, s // ts),
        in_specs=[
            pl.BlockSpec((1, ts, d), lambda i, j: (i, j, 0)),
            pl.BlockSpec(g_mix.shape, const2),
            pl.BlockSpec(w_in.shape, const2),
            pl.BlockSpec(conv_w.shape, const2),
            pl.BlockSpec(g_v.shape, const2),
            pl.BlockSpec(w_s.shape, const3),
            pl.BlockSpec(bias.shape, const2),
            pl.BlockSpec(g_oc.shape, const2),
            pl.BlockSpec(g_og.shape, const2),
            pl.BlockSpec(w_out.shape, const2),
        ],
        out_specs=pl.BlockSpec((1, ts, d), lambda i, j: (i, j, 0)),
        out_shape=jax.ShapeDtypeStruct((b, s, d), jnp.float32),
        scratch_shapes=[pltpu.VMEM((ts + 8, CONV_GROUP_WIDTH), jnp.float32)],
        compiler_params=pltpu.CompilerParams(
            dimension_semantics=("arbitrary", "arbitrary"), vmem_limit_bytes=VMEM_LIMIT_BYTES),
        name="mixer",
    )(x, g_mix, w_in, conv_w, g_v, w_s, bias, g_oc, g_og, w_out)


def _attn_route_kernel(x_ref, kt_ref, v_ref, gx_ref, wq_ref, wo_ref, gf_ref, wrt_ref, brt_ref,
                       x2_ref, slab_ref, counts_ref, carry_ref):
    ts = x_ref.shape[1]
    d = x_ref.shape[2]
    hd = d // XA_HEADS
    nsub = ts // SUB_TILE
    first = (pl.program_id(0) == 0) & (pl.program_id(1) == 0)

    @pl.when(first)
    def _():
        carry_ref[...] = jnp.zeros_like(carry_ref)

    def dot(a, b):
        return jnp.dot(a, b, preferred_element_type=jnp.float32)

    subs = range(nsub)
    x1 = [x_ref[0, i * SUB_TILE:(i + 1) * SUB_TILE, :] for i in subs]
    h2 = [_rms(x, gx_ref[...]).astype(jnp.bfloat16) for x in x1]
    q = [dot(h, wq_ref[...]).astype(jnp.bfloat16) for h in h2]
    heads = [[] for _ in subs]
    for a in range(XA_HEADS):
        sc = [dot(q[i][:, a * hd:(a + 1) * hd], kt_ref[0, a * hd:(a + 1) * hd, :]) * (hd ** -0.5)
              for i in subs]
        p = [jnp.exp(s_ - jnp.max(s_, axis=-1, keepdims=True)) for s_ in sc]
        l = [jnp.sum(p_, axis=-1, keepdims=True) for p_ in p]
        o = [dot(p[i].astype(jnp.bfloat16), v_ref[0, :, a * hd:(a + 1) * hd]) for i in subs]
        for i in subs:
            heads[i].append((o[i] / l[i]).astype(jnp.bfloat16))
    x2 = [x1[i] + dot(jnp.concatenate(heads[i], axis=1), wo_ref[...]) for i in subs]
    for i in subs:
        x2_ref[0, i * SUB_TILE:(i + 1) * SUB_TILE, :] = x2[i]

    h3 = [_rms(x, gf_ref[...]).astype(jnp.bfloat16) for x in x2]
    lgs = [lax.dot_general(wrt_ref[...], h, (((1,), (1,)), ((), ())),
                           preferred_element_type=jnp.float32) + brt_ref[...] for h in h3]
    sub = lax.broadcasted_iota(jnp.int32, (EXPERTS_PER_GROUP, SUB_TILE), 0).astype(jnp.float32)
    big = float(EXPERTS_PER_GROUP)
    is_g = sub < N_GROUPS

    def classify(lg):
        glog = lg[0:EXPERTS_PER_GROUP, :]
        gmax = jnp.max(jnp.where(is_g, glog, _NEG), axis=0, keepdims=True)
        gidx = jnp.min(jnp.where(is_g & (glog == gmax), sub, big), axis=0, keepdims=True)
        el = lg[ROUTE_EXPERT_ROW0:ROUTE_EXPERT_ROW0 + EXPERTS_PER_GROUP, :]
        for grp in range(1, N_GROUPS):
            r0 = ROUTE_EXPERT_ROW0 + grp * EXPERTS_PER_GROUP
            el = jnp.where(gidx == grp, lg[r0:r0 + EXPERTS_PER_GROUP, :], el)
        t1 = jnp.max(el, axis=0, keepdims=True)
        i1 = jnp.min(jnp.where(el == t1, sub, big), axis=0, keepdims=True)
        rest = sub != i1
        t2 = jnp.max(jnp.where(rest, el, _NEG), axis=0, keepdims=True)
        i2 = jnp.min(jnp.where(rest & (el == t2), sub, big), axis=0, keepdims=True)
        a = jnp.minimum(i1, i2)
        b = jnp.maximum(i1, i2)
        pair = a * (2 * EXPERTS_PER_GROUP - 1 - a) * 0.5 + (b - a - 1.0)
        return gidx * N_PAIRS + pair

    cls = [classify(lg) for lg in lgs]

    r = lax.broadcasted_iota(jnp.int32, (SUB_TILE, SUB_TILE), 0)
    c = lax.broadcasted_iota(jnp.int32, (SUB_TILE, SUB_TILE), 1)
    earlier = jnp.where(r < c, 1.0, 0.0).astype(jnp.bfloat16)
    crow = lax.broadcasted_iota(jnp.int32, (ROUTE_LANES, SUB_TILE), 0).astype(jnp.float32)
    oh = [crow == cl for cl in cls]
    ohf = [jnp.where(o_, 1.0, 0.0) for o_ in oh]
    inside = [dot(o_.astype(jnp.bfloat16), earlier) for o_ in ohf]
    seen = carry_ref[...]
    for i in subs:
        rank = jnp.sum(jnp.where(oh[i], inside[i] + seen, 0.0), axis=0, keepdims=True)
        slab_ref[:, i * SUB_TILE:(i + 1) * SUB_TILE] = jnp.where(
            sub == 0, cls[i], jnp.where(sub == 1, rank, 0.0))
        seen = seen + jnp.sum(ohf[i], axis=1, keepdims=True)
    carry_ref[...] = seen
    counts_ref[...] = seen


def _attn_route(x1, kt, v, g_x, w_q, w_o, g_f, w_rt, b_rt):
    b, s, d = x1.shape
    m = v.shape[1]
    ts = min(ATTN_TILE, s)
    nt = s // ts
    const2 = lambda i, j: (0, 0)
    return pl.pallas_call(
        _attn_route_kernel,
        grid=(b, nt),
        in_specs=[
            pl.BlockSpec((1, ts, d), lambda i, j: (i, j, 0)),
            pl.BlockSpec((1, d, m), lambda i, j: (i, 0, 0)),
            pl.BlockSpec((1, m, d), lambda i, j: (i, 0, 0)),
            pl.BlockSpec(g_x.shape, const2),
            pl.BlockSpec(w_q.shape, const2),
            pl.BlockSpec(w_o.shape, const2),
            pl.BlockSpec(g_f.shape, const2),
            pl.BlockSpec(w_rt.shape, const2),
            pl.BlockSpec(b_rt.shape, const2),
        ],
        out_specs=[
            pl.BlockSpec((1, ts, d), lambda i, j: (i, j, 0)),
            pl.BlockSpec((SUBLANES, ts), lambda i, j: (0, i * nt + j)),
            pl.BlockSpec((ROUTE_LANES, 1), const2),
        ],
        out_shape=[
            jax.ShapeDtypeStruct((b, s, d), jnp.float32),
            jax.ShapeDtypeStruct((SUBLANES, b * s), jnp.float32),
            jax.ShapeDtypeStruct((ROUTE_LANES, 1), jnp.float32),
        ],
        scratch_shapes=[pltpu.VMEM((ROUTE_LANES, 1), jnp.float32)],
        compiler_params=pltpu.CompilerParams(
            dimension_semantics=("arbitrary", "arbitrary"), vmem_limit_bytes=VMEM_LIMIT_BYTES),
        name="attn_route",
    )(x1, kt, v, g_x, w_q, w_o, g_f, w_rt, b_rt)


def _dest_kernel(slab_ref, starts_ref, o_ref):
    slab = slab_ref[...]
    tt = slab.shape[1]
    crow = lax.broadcasted_iota(jnp.int32, (ROUTE_LANES, tt), 0).astype(jnp.float32)
    start = jnp.sum(jnp.where(crow == slab[0:1, :], starts_ref[...], 0.0), axis=0, keepdims=True)
    sub = lax.broadcasted_iota(jnp.int32, slab.shape, 0)
    o_ref[...] = jnp.where(sub == 0, start + slab[1:2, :], 0.0).astype(jnp.int32)


def _dest_rows(slab, starts_col):
    t = slab.shape[1]
    tt = min(2048, t)
    return pl.pallas_call(
        _dest_kernel,
        grid=(t // tt,),
        in_specs=[pl.BlockSpec((SUBLANES, tt), lambda i: (0, i)),
                  pl.BlockSpec((ROUTE_LANES, 1), lambda i: (0, 0))],
        out_specs=pl.BlockSpec((SUBLANES, tt), lambda i: (0, i)),
        out_shape=jax.ShapeDtypeStruct((SUBLANES, t), jnp.int32),
        compiler_params=pltpu.CompilerParams(dimension_semantics=("arbitrary",)),
        name="dest_rows",
    )(slab, starts_col)


def _sc_split(t):
    info = plsc.get_sparse_core_info()
    workers = info.num_cores * info.num_subcores
    per_w, rem = divmod(t, workers)
    nchunk, rem2 = divmod(per_w, SC_GATHER_ROWS)
    assert rem == 0 and rem2 == 0 and nchunk % 2 == 0, (t, workers, SC_GATHER_ROWS)
    return info.num_cores, per_w, nchunk


def _dispatch(x2, dest, cap):
    t, d = x2.shape
    nc, per_w, nchunk = _sc_split(t)
    ch = SC_GATHER_ROWS
    mesh = plsc.VectorSubcoreMesh(core_axis_name="c", subcore_axis_name="s")
    dma = pltpu.SemaphoreType.DMA

    @functools.partial(
        pl.kernel, mesh=mesh,
        out_type=jax.ShapeDtypeStruct((cap, d), jnp.float32),
        scratch_types=[pltpu.VMEM((ch,), jnp.int32), pltpu.VMEM((ch,), jnp.int32),
                       pltpu.VMEM((ch, d), jnp.float32), pltpu.VMEM((ch, d), jnp.float32),
                       dma, dma, dma, dma, dma, dma],
    )
    def dispatch_sc(x_hbm, dest_hbm, xs_hbm, idx0, idx1, rows0, rows1, i0, i1, r0, r1, s0, s1):
        wid = lax.axis_index("s") * nc + lax.axis_index("c")
        base = wid * per_w
        bufs = ((idx0, rows0, i0, r0, s0), (idx1, rows1, i1, r1, s1))

        def read(c, buf):
            rows = pl.ds(base + c * ch, ch)
            return (pltpu.make_async_copy(dest_hbm.at[rows], buf[0], buf[2]),
                    pltpu.make_async_copy(x_hbm.at[rows], buf[1], buf[3]))

        def scatter(buf):
            return pltpu.make_async_copy(buf[1], xs_hbm.at[buf[0]], buf[4])

        for cp in read(0, bufs[0]):
            cp.start()

        @pl.loop(0, nchunk, step=2)
        def _(c):
            for b in range(2):
                cur, nxt = bufs[b], bufs[1 - b]
                cc = c + b

                @pl.when(cc >= 1)
                def _():
                    scatter(nxt).wait()

                @pl.when(cc + 1 < nchunk)
                def _():
                    for cp in read(cc + 1, nxt):
                        cp.start()

                for cp in read(cc, cur):
                    cp.wait()
                scatter(cur).start()

        scatter(bufs[(nchunk - 1) % 2]).wait()

    return dispatch_sc(x2, dest)


def _unsort(ys, dest, t):
    d = ys.shape[1]
    nc, per_w, nchunk = _sc_split(t)
    ch = SC_GATHER_ROWS
    mesh = plsc.VectorSubcoreMesh(core_axis_name="c", subcore_axis_name="s")

    @functools.partial(
        pl.kernel, mesh=mesh,
        out_type=jax.ShapeDtypeStruct((t, d), jnp.float32),
        scratch_types=[pltpu.VMEM((per_w,), jnp.int32),
                       pltpu.VMEM((ch, d), jnp.float32), pltpu.VMEM((ch, d), jnp.float32),
                       pltpu.SemaphoreType.DMA, pltpu.SemaphoreType.DMA,
                       pltpu.SemaphoreType.DMA, pltpu.SemaphoreType.DMA],
    )
    def unsort_sc(ys_hbm, dest_hbm, out_hbm, idx_v, rows0, rows1, g0, g1, w0, w1):
        wid = lax.axis_index("s") * nc + lax.axis_index("c")
        base = wid * per_w
        pltpu.sync_copy(dest_hbm.at[pl.ds(base, per_w)], idx_v)
        bufs = ((rows0, g0, w0), (rows1, g1, w1))

        def gather(c, buf):
            return pltpu.make_async_copy(ys_hbm.at[idx_v.at[pl.ds(c * ch, ch)]], buf[0], buf[1])

        def write(c, buf):
            return pltpu.make_async_copy(buf[0], out_hbm.at[pl.ds(base + c * ch, ch)], buf[2])

        gather(0, bufs[0]).start()

        @pl.loop(0, nchunk, step=2)
        def _(c):
            for b in range(2):
                cur, nxt = bufs[b], bufs[1 - b]
                cc = c + b

                @pl.when(cc >= 1)
                def _():
                    write(cc - 1, nxt).wait()

                @pl.when(cc + 1 < nchunk)
                def _():
                    gather(cc + 1, nxt).start()

                gather(cc, cur).wait()
                write(cc, cur).start()

        write(nchunk - 1, bufs[(nchunk - 1) % 2]).wait()

    return unsort_sc(ys, dest)


def _expert_kernel(grp_ref, ea_ref, eb_ref, nvalid_ref, nact_ref,
                   xs_ref, gf_ref, wr_ref, br_ref, gfin_ref, wg_hbm, wu_hbm, wd_hbm,
                   o_ref, wgb, wub, wdb, sg, su, sd, sem, cnt_ref):
    s = pl.program_id(0)
    blocks = tuple(BLOCKS_PER_STEP * s + i for i in range(BLOCKS_PER_STEP))
    g = grp_ref[blocks[0]]

    def stage(e):
        slot = e % 2
        return (pltpu.make_async_copy(wg_hbm.at[e], sg.at[slot], sem.at[slot, 0]),
                pltpu.make_async_copy(wu_hbm.at[e], su.at[slot], sem.at[slot, 1]),
                pltpu.make_async_copy(wd_hbm.at[e], sd.at[slot], sem.at[slot, 2]))

    def start_next():
        @pl.when(cnt_ref[0] < N_EXPERTS)
        def _():
            for cp in stage(cnt_ref[0]):
                cp.start()
            cnt_ref[0] = cnt_ref[0] + 1

    @pl.when(s == 0)
    def _():
        cnt_ref[0] = 0
        cnt_ref[1] = 0
        start_next()
        start_next()

    active = blocks[0] < nact_ref[0]

    @pl.when(active)
    def _():
        need = g * EXPERTS_PER_GROUP + functools.reduce(
            jnp.maximum, [eb_ref[j] for j in blocks])

        def load(e, carry):
            for cp in stage(e):
                cp.wait()
            slot = e % 2
            k = e % EXPERTS_PER_GROUP
            wgb[k] = sg[slot].astype(jnp.bfloat16)
            wub[k] = su[slot].astype(jnp.bfloat16)
            wdb[k] = sd[slot].astype(jnp.bfloat16)
            cnt_ref[1] = e + 1
            start_next()
            return carry

        lax.fori_loop(cnt_ref[1], need + 1, load, 0)

        lane = lax.broadcasted_iota(jnp.int32, (ROW_BLOCK, ROUTE_LANES), 1)
        is_g = lane < N_GROUPS
        lo = EXPERT_LANE0 + EXPERTS_PER_GROUP * g
        nblk = len(blocks)
        rowid = lax.broadcasted_iota(jnp.int32, (ROW_BLOCK, 1), 0)
        xs = [jnp.where(rowid < nvalid_ref[blocks[i]],
                        xs_ref[i * ROW_BLOCK:(i + 1) * ROW_BLOCK, :], 0.0) for i in range(nblk)]
        h3 = [_rms(x, gf_ref[...]).astype(jnp.bfloat16) for x in xs]
        lg = [jnp.dot(h, wr_ref[...], preferred_element_type=jnp.float32) + br_ref[...]
              for h in h3]

        def gates(lgi, j):
            def pick(idx):
                return jnp.sum(jnp.where(lane == idx, lgi, 0.0), axis=-1, keepdims=True)

            gmax = jnp.max(jnp.where(is_g, lgi, _NEG), axis=-1, keepdims=True)
            den = jnp.sum(jnp.where(is_g, jnp.exp(jnp.where(is_g, lgi, _NEG) - gmax), 0.0),
                          axis=-1, keepdims=True)
            grp_p = jnp.exp(pick(g) - gmax) / den
            la = pick(lo + ea_ref[j])
            lb = pick(lo + eb_ref[j])
            m = jnp.maximum(la, lb)
            pa = jnp.exp(la - m)
            pb = jnp.exp(lb - m)
            return grp_p * pa / (pa + pb), grp_p * pb / (pa + pb)

        gate = [gates(lg[i], blocks[i]) for i in range(nblk)]
        ys = list(xs)
        for which, e_ref in enumerate((ea_ref, eb_ref)):
            ks = [e_ref[j] for j in blocks]
            gg = [jnp.dot(h3[i], wgb[ks[i]], preferred_element_type=jnp.float32)
                  for i in range(nblk)]
            uu = [jnp.dot(h3[i], wub[ks[i]], preferred_element_type=jnp.float32)
                  for i in range(nblk)]
            act = [(gg[i] * (1.0 / (1.0 + jnp.exp(-gg[i]))) * uu[i]
                    * gate[i][which]).astype(jnp.bfloat16) for i in range(nblk)]
            ys = [ys[i] + jnp.dot(act[i], wdb[ks[i]], preferred_element_type=jnp.float32)
                  for i in range(nblk)]
        for i in range(nblk):
            o_ref[i * ROW_BLOCK:(i + 1) * ROW_BLOCK, :] = _rms(ys[i], gfin_ref[...])

    @pl.when(jnp.logical_not(active))
    def _():
        o_ref[...] = jnp.zeros_like(o_ref)

    @pl.when(s == pl.num_programs(0) - 1)
    def _():
        def drain(e, carry):
            for cp in stage(e):
                cp.wait()
            return carry

        lax.fori_loop(cnt_ref[1], cnt_ref[0], drain, 0)


def _experts(xs, blk_grp, blk_a, blk_b, nvalid, nact, g_ffn, w_r, b_r, g_final, w_gate, w_up, w_down):
    cap, d = xs.shape
    de = w_gate.shape[2]
    step_rows = BLOCKS_PER_STEP * ROW_BLOCK
    steps = cap // step_rows
    pre = lambda f: (lambda s, gr, ea, eb, nv, na: f(s, na))
    const2 = pre(lambda s, na: (0, 0))
    last_step = lambda na: (na[0] - 1) // BLOCKS_PER_STEP
    hbm = pl.BlockSpec(memory_space=pl.ANY)
    grid_spec = pltpu.PrefetchScalarGridSpec(
        num_scalar_prefetch=5,
        grid=(steps,),
        in_specs=[
            pl.BlockSpec((step_rows, d), pre(lambda s, na: (jnp.minimum(s, last_step(na)), 0))),
            pl.BlockSpec(g_ffn.shape, const2),
            pl.BlockSpec(w_r.shape, const2),
            pl.BlockSpec(b_r.shape, const2),
            pl.BlockSpec(g_final.shape, const2),
            hbm, hbm, hbm,
        ],
        out_specs=pl.BlockSpec((step_rows, d), pre(lambda s, na: (s, 0))),
        scratch_shapes=[
            pltpu.VMEM((EXPERTS_PER_GROUP, d, de), jnp.bfloat16),
            pltpu.VMEM((EXPERTS_PER_GROUP, d, de), jnp.bfloat16),
            pltpu.VMEM((EXPERTS_PER_GROUP, de, d), jnp.bfloat16),
            pltpu.VMEM((2, d, de), jnp.float32),
            pltpu.VMEM((2, d, de), jnp.float32),
            pltpu.VMEM((2, de, d), jnp.float32),
            pltpu.SemaphoreType.DMA((2, 3)),
            pltpu.SMEM((2,), jnp.int32),
        ],
    )
    return pl.pallas_call(
        _expert_kernel,
        grid_spec=grid_spec,
        out_shape=jax.ShapeDtypeStruct((cap, d), jnp.float32),
        compiler_params=pltpu.CompilerParams(
            dimension_semantics=("arbitrary",), vmem_limit_bytes=VMEM_LIMIT_BYTES),
        name="experts",
    )(blk_grp, blk_a, blk_b, nvalid, nact, xs, g_ffn, w_r, b_r, g_final, w_gate, w_up, w_down)


def _moe_final(x2, slab, counts_col, g_ffn, w_r, b_r, g_final, w_gate, w_up, w_down):
    t, d = x2.shape
    nb = t // ROW_BLOCK + N_CLASSES + N_GROUPS * (BLOCKS_PER_STEP - 1)
    assert nb % BLOCKS_PER_STEP == 0
    counts = counts_col[:N_CLASSES, 0].astype(jnp.int32)
    nblk = (counts + ROW_BLOCK - 1) // ROW_BLOCK
    grp_blocks = jnp.sum(nblk.reshape(N_GROUPS, N_PAIRS), axis=1)
    nblk = nblk.reshape(N_GROUPS, N_PAIRS).at[:, N_PAIRS - 1].add(
        (-grp_blocks) % BLOCKS_PER_STEP).reshape(N_CLASSES)
    blk_end = jnp.cumsum(nblk)
    blk_start = blk_end - nblk
    nact = blk_end[-1]
    j = jnp.arange(nb, dtype=jnp.int32)
    blk_class = jnp.minimum(
        jnp.sum((blk_end[None, :] <= j[:, None]).astype(jnp.int32), axis=1), N_CLASSES - 1)
    blk_class = jnp.where(j < nact, blk_class, blk_class[jnp.maximum(nact - 1, 0)])
    nvalid = jnp.where(j < nact, jnp.clip(
        counts[blk_class] - (j - blk_start[blk_class]) * ROW_BLOCK, 0, ROW_BLOCK), 0)
    pair_a = jnp.array([p[0] for p in PAIRS], jnp.int32)
    pair_b = jnp.array([p[1] for p in PAIRS], jnp.int32)
    blk_grp = blk_class // N_PAIRS
    blk_a = pair_a[blk_class % N_PAIRS]
    blk_b = pair_b[blk_class % N_PAIRS]
    starts_col = jnp.zeros((ROUTE_LANES, 1), jnp.float32).at[:N_CLASSES, 0].set(
        (blk_start * ROW_BLOCK).astype(jnp.float32))

    dest = _dest_rows(slab, starts_col)[0]
    xs = _dispatch(x2, dest, nb * ROW_BLOCK)
    ys = _experts(xs, blk_grp, blk_a, blk_b, nvalid, nact[None], g_ffn, w_r, b_r, g_final,
                  w_gate, w_up, w_down)
    return _unsort(ys, dest, t)


def kernel(x, mem, g_mix, w_in, conv_w, g_v, w_s, b_s, g_out_conv, g_out_gmlp, w_out, g_xattn,
           g_mem, w_q, w_k, w_v, w_o, g_ffn, w_grp, b_grp, w_rt, b_rt, w_gate, w_up, w_down,
           g_final):
    b, s, d = x.shape
    assert g_mix.shape[0] == 1, "the final norm is fused into the single layer's expert kernel"
    assert N_CLASSES <= ROUTE_LANES
    bf = jnp.bfloat16
    kt, v = _kv_proj(mem, g_mem[0][None], w_k[0].astype(bf), w_v[0].astype(bf))

    bias = jnp.repeat(b_s[0].T, GMLP_HEAD_DIM, axis=1)
    x1 = _mixer(x, g_mix[0][None], w_in[0].astype(bf), conv_w[0], g_v[0][None], w_s[0], bias,
                g_out_conv[0][None], g_out_gmlp[0][None], w_out[0].astype(bf))

    pad = ROUTE_LANES - N_GROUPS - N_EXPERTS
    w_r = jnp.concatenate([w_grp[0], w_rt[0], jnp.zeros((d, pad), jnp.float32)], axis=1).astype(bf)
    b_r = jnp.concatenate([b_grp[0], b_rt[0], jnp.zeros((pad,), jnp.float32)])[None]
    gap = ROUTE_EXPERT_ROW0 - N_GROUPS
    tail = ROUTE_LANES - ROUTE_EXPERT_ROW0 - N_EXPERTS
    w_r_t = jnp.concatenate([w_grp[0].T, jnp.zeros((gap, d), jnp.float32), w_rt[0].T,
                             jnp.zeros((tail, d), jnp.float32)], axis=0).astype(bf)
    b_r_t = jnp.concatenate([b_grp[0], jnp.zeros((gap,), jnp.float32), b_rt[0],
                             jnp.zeros((tail,), jnp.float32)])[:, None]
    x2, slab, counts_col = _attn_route(x1, kt, v, g_xattn[0][None], w_q[0].astype(bf),
                                       w_o[0].astype(bf), g_ffn[0][None], w_r_t, b_r_t)
    out = _moe_final(x2.reshape(b * s, d), slab, counts_col, g_ffn[0][None], w_r, b_r,
                     g_final[None], w_gate[0], w_up[0], w_down[0])
    return out.reshape(b, s, d)
```

```python
import functools

import jax
import jax.numpy as jnp
from jax import lax
from jax.experimental import pallas as pl
from jax.experimental.pallas import tpu as pltpu
from jax.experimental.pallas import tpu_sc as plsc

EPS = 1e-6
CONV_GROUP_WIDTH = 512
GMLP_HEADS = 8
GMLP_HEAD_DIM = 64
CHUNK = 128
XA_HEADS = 4
N_GROUPS = 4
EXPERTS_PER_GROUP = 8
N_EXPERTS = N_GROUPS * EXPERTS_PER_GROUP
ROUTE_LANES = 128
EXPERT_LANE0 = N_GROUPS
ROUTE_EXPERT_ROW0 = 8

N_PAIRS = EXPERTS_PER_GROUP * (EXPERTS_PER_GROUP - 1) // 2
N_CLASSES = N_GROUPS * N_PAIRS
PAIRS = [(a, b) for a in range(EXPERTS_PER_GROUP) for b in range(a + 1, EXPERTS_PER_GROUP)]

MIXER_TILE = 1024
ATTN_TILE = 2048
SUB_TILE = 256
ROW_BLOCK = 128
BLOCKS_PER_STEP = 4
SC_CHUNK_ROWS = 16
SC_BUFFERS = 4
VMEM_LIMIT_BYTES = 56 * 1024 * 1024

SUBLANES = 8

_NEG = -1e30


def _rms(x, g):
    return x * lax.rsqrt(jnp.mean(x * x, axis=-1, keepdims=True) + EPS) * g


def _gelu_tanh(x):
    return 0.5 * x * (1.0 + jnp.tanh(0.7978845608028654 * (x + 0.044715 * (x * x * x))))


def _kv_kernel(mem_ref, g_ref, wk_ref, wv_ref, kt_ref, v_ref):
    m = _rms(mem_ref[0], g_ref[...]).astype(jnp.bfloat16)
    k = jnp.dot(m, wk_ref[...], preferred_element_type=jnp.float32)
    v = jnp.dot(m, wv_ref[...], preferred_element_type=jnp.float32)
    kt_ref[0] = k.T.astype(jnp.bfloat16)
    v_ref[0] = v.astype(jnp.bfloat16)


def _kv_proj(mem, g_mem, w_k, w_v):
    b, m, d = mem.shape
    const = lambda i: (0, 0)
    return pl.pallas_call(
        _kv_kernel,
        grid=(b,),
        in_specs=[
            pl.BlockSpec((1, m, d), lambda i: (i, 0, 0)),
            pl.BlockSpec((1, d), const),
            pl.BlockSpec((d, d), const),
            pl.BlockSpec((d, d), const),
        ],
        out_specs=[
            pl.BlockSpec((1, d, m), lambda i: (i, 0, 0)),
            pl.BlockSpec((1, m, d), lambda i: (i, 0, 0)),
        ],
        out_shape=[
            jax.ShapeDtypeStruct((b, d, m), jnp.bfloat16),
            jax.ShapeDtypeStruct((b, m, d), jnp.bfloat16),
        ],
        compiler_params=pltpu.CompilerParams(
            dimension_semantics=("arbitrary",), vmem_limit_bytes=VMEM_LIMIT_BYTES),
        name="kv_proj",
    )(mem, g_mem, w_k, w_v)


def _mixer_kernel(x_ref, gmix_ref, win_ref, convw_ref, gv_ref, ws_ref, bias_ref,
                  goc_ref, gog_ref, wout_ref, o_ref, zbuf_ref):
    ts = x_ref.shape[1]
    w = CONV_GROUP_WIDTH
    hw = 2 * GMLP_HEAD_DIM
    nsub = ts // SUB_TILE

    def dot(a, b):
        return jnp.dot(a, b, preferred_element_type=jnp.float32)

    @pl.when(pl.program_id(1) == 0)
    def _():
        zbuf_ref[0:8, :] = jnp.zeros((8, w), jnp.float32)

    @pl.when(pl.program_id(1) != 0)
    def _():
        zbuf_ref[0:8, :] = zbuf_ref[ts:ts + 8, :]

    low = lax.broadcasted_iota(jnp.int32, (SUB_TILE, hw), 1) < GMLP_HEAD_DIM
    row = lax.broadcasted_iota(jnp.int32, (CHUNK, CHUNK), 0)
    colid = lax.broadcasted_iota(jnp.int32, (CHUNK, CHUNK), 1)
    causal = row >= colid
    ws = [jnp.where(causal, ws_ref[hd], 0.0).astype(jnp.bfloat16) for hd in range(GMLP_HEADS)]
    npair = SUB_TILE // CHUNK // 2
    lo = lax.broadcasted_iota(jnp.int32, (CHUNK, hw), 1) < GMLP_HEAD_DIM
    swap = lambda a: pltpu.roll(a, GMLP_HEAD_DIM, axis=1)

    def project(i):
        r0 = i * SUB_TILE
        xt = x_ref[0, r0:r0 + SUB_TILE, :]
        h = _rms(xt, gmix_ref[...]).astype(jnp.bfloat16)
        p = [dot(h, win_ref[:, k * w:(k + 1) * w]) for k in range(5)]
        z = p[1] * p[2]
        zbuf_ref[8 + r0:8 + r0 + SUB_TILE, :] = z
        return dict(xt=xt, gate_b=p[0], z=z, u=p[3], v=p[4], r0=r0)

    def gate_and_norm(st):
        r0 = st["r0"]
        zc = (convw_ref[0:1, :] * zbuf_ref[6 + r0:6 + r0 + SUB_TILE, :]
              + convw_ref[1:2, :] * zbuf_ref[7 + r0:7 + r0 + SUB_TILE, :]
              + convw_ref[2:3, :] * st["z"])
        st["ya"] = _rms(st["gate_b"] * zc, goc_ref[...]).astype(jnp.bfloat16)
        st["u"] = _gelu_tanh(st["u"])
        v = _gelu_tanh(st["v"])
        v2 = v * v
        ss_cols = []
        for k in range(GMLP_HEADS // 2):
            col = v2[:, hw * k:hw * (k + 1)]
            ss_cols.append(jnp.where(low,
                                     jnp.sum(jnp.where(low, col, 0.0), axis=1, keepdims=True),
                                     jnp.sum(jnp.where(low, 0.0, col), axis=1, keepdims=True)))
        ss = jnp.concatenate(ss_cols, axis=1)
        st["vn"] = v * lax.rsqrt(ss * (1.0 / GMLP_HEAD_DIM) + EPS) * gv_ref[...]

    def mix_positions(st):
        vn = st["vn"]
        s_cols = [[None] * (GMLP_HEADS // 2) for _ in range(2 * npair)]
        for hp in range(GMLP_HEADS // 2):
            cols = [vn[c * CHUNK:(c + 1) * CHUNK, hw * hp:hw * (hp + 1)] for c in range(2 * npair)]
            swapped = [swap(a) for a in cols]
            rhs_even = jnp.concatenate(
                [jnp.where(lo, cols[2 * p], swapped[2 * p + 1]) for p in range(npair)], axis=1)
            rhs_odd = jnp.concatenate(
                [jnp.where(lo, swapped[2 * p], cols[2 * p + 1]) for p in range(npair)], axis=1)
            out_e = dot(ws[2 * hp], rhs_even.astype(jnp.bfloat16))
            out_o = dot(ws[2 * hp + 1], rhs_odd.astype(jnp.bfloat16))
            for p in range(npair):
                e = out_e[:, hw * p:hw * (p + 1)]
                o = out_o[:, hw * p:hw * (p + 1)]
                s_cols[2 * p][hp] = jnp.where(lo, e, swap(o))
                s_cols[2 * p + 1][hp] = jnp.where(lo, swap(e), o)
        st["s"] = jnp.concatenate(
            [jnp.concatenate(c, axis=1) + bias_ref[...] for c in s_cols], axis=0)

    def output(st):
        r0 = st["r0"]
        yb = _rms(st["u"] * st["s"], gog_ref[...]).astype(jnp.bfloat16)
        o_ref[0, r0:r0 + SUB_TILE, :] = (st["xt"] + dot(st["ya"], wout_ref[0:w, :])
                                          + dot(yb, wout_ref[w:2 * w, :]))

    phases = (gate_and_norm, mix_positions, output)
    states = []
    for step in range(nsub + len(phases)):
        if step < nsub:
            states.append(project(step))
        for k, phase in enumerate(phases):
            i = step - 1 - k
            if 0 <= i < nsub:
                phase(states[i])


def _mixer(x, g_mix, w_in, conv_w, g_v, w_s, bias, g_oc, g_og, w_out):
    b, s, d = x.shape
    ts = min(MIXER_TILE, s)
    const2 = lambda i, j: (0, 0)
    const3 = lambda i, j: (0, 0, 0)
    return pl.pallas_call(
        _mixer_kernel,
        grid=(b, s // ts),
        in_specs=[
            pl.BlockSpec((1, ts, d), lambda i, j: (i, j, 0)),
            pl.BlockSpec(g_mix.shape, const2),
            pl.BlockSpec(w_in.shape, const2),
            pl.BlockSpec(conv_w.shape, const2),
            pl.BlockSpec(g_v.shape, const2),
            pl.BlockSpec(w_s.shape, const3),
            pl.BlockSpec(bias.shape, const2),
            pl.BlockSpec(g_oc.shape, const2),
            pl.BlockSpec(g_og.shape, const2),
            pl.BlockSpec(w_out.shape, const2),
        ],
        out_specs=pl.BlockSpec((1, ts, d), lambda i, j: (i, j, 0)),
        out_shape=jax.ShapeDtypeStruct((b, s, d), jnp.float32),
        scratch_shapes=[pltpu.VMEM((ts + 8, CONV_GROUP_WIDTH), jnp.float32)],
        compiler_params=pltpu.CompilerParams(
            dimension_semantics=("arbitrary", "arbitrary"), vmem_limit_bytes=VMEM_LIMIT_BYTES),
        name="mixer",
    )(x, g_mix, w_in, conv_w, g_v, w_s, bias, g_oc, g_og, w_out)


def _attn_route_kernel(x_ref, kt_ref, v_ref, gx_ref, wq_ref, wo_ref, gf_ref, wrt_ref, brt_ref,
                       x2_ref, slab_ref, counts_ref, carry_ref):
    ts = x_ref.shape[1]
    d = x_ref.shape[2]
    hd = d // XA_HEADS
    nsub = ts // SUB_TILE
    first = (pl.program_id(0) == 0) & (pl.program_id(1) == 0)

    @pl.when(first)
    def _():
        carry_ref[...] = jnp.zeros_like(carry_ref)

    def dot(a, b):
        return jnp.dot(a, b, preferred_element_type=jnp.float32)

    subs = range(nsub)
    x1 = [x_ref[0, i * SUB_TILE:(i + 1) * SUB_TILE, :] for i in subs]
    h2 = [_rms(x, gx_ref[...]).astype(jnp.bfloat16) for x in x1]
    q = [dot(h, wq_ref[...]).astype(jnp.bfloat16) for h in h2]
    heads = [[] for _ in subs]
    for a in range(XA_HEADS):
        sc = [dot(q[i][:, a * hd:(a + 1) * hd], kt_ref[0, a * hd:(a + 1) * hd, :]) * (hd ** -0.5)
              for i in subs]
        p = [jnp.exp(s_ - jnp.max(s_, axis=-1, keepdims=True)) for s_ in sc]
        l = [jnp.sum(p_, axis=-1, keepdims=True) for p_ in p]
        o = [dot(p[i].astype(jnp.bfloat16), v_ref[0, :, a * hd:(a + 1) * hd]) for i in subs]
        for i in subs:
            heads[i].append((o[i] / l[i]).astype(jnp.bfloat16))
    x2 = [x1[i] + dot(jnp.concatenate(heads[i], axis=1), wo_ref[...]) for i in subs]
    for i in subs:
        x2_ref[0, i * SUB_TILE:(i + 1) * SUB_TILE, :] = x2[i]

    h3 = [_rms(x, gf_ref[...]).astype(jnp.bfloat16) for x in x2]
    lgs = [lax.dot_general(wrt_ref[...], h, (((1,), (1,)), ((), ())),
                           preferred_element_type=jnp.float32) + brt_ref[...] for h in h3]
    sub = lax.broadcasted_iota(jnp.int32, (EXPERTS_PER_GROUP, SUB_TILE), 0).astype(jnp.float32)
    big = float(EXPERTS_PER_GROUP)
    is_g = sub < N_GROUPS

    def classify(lg):
        glog = lg[0:EXPERTS_PER_GROUP, :]
        gmax = jnp.max(jnp.where(is_g, glog, _NEG), axis=0, keepdims=True)
        gidx = jnp.min(jnp.where(is_g & (glog == gmax), sub, big), axis=0, keepdims=True)
        el = lg[ROUTE_EXPERT_ROW0:ROUTE_EXPERT_ROW0 + EXPERTS_PER_GROUP, :]
        for grp in range(1, N_GROUPS):
            r0 = ROUTE_EXPERT_ROW0 + grp * EXPERTS_PER_GROUP
            el = jnp.where(gidx == grp, lg[r0:r0 + EXPERTS_PER_GROUP, :], el)
        t1 = jnp.max(el, axis=0, keepdims=True)
        i1 = jnp.min(jnp.where(el == t1, sub, big), axis=0, keepdims=True)
        rest = sub != i1
        t2 = jnp.max(jnp.where(rest, el, _NEG), axis=0, keepdims=True)
        i2 = jnp.min(jnp.where(rest & (el == t2), sub, big), axis=0, keepdims=True)
        a = jnp.minimum(i1, i2)
        b = jnp.maximum(i1, i2)
        pair = a * (2 * EXPERTS_PER_GROUP - 1 - a) * 0.5 + (b - a - 1.0)
        return gidx * N_PAIRS + pair

    cls = [classify(lg) for lg in lgs]

    r = lax.broadcasted_iota(jnp.int32, (SUB_TILE, SUB_TILE), 0)
    c = lax.broadcasted_iota(jnp.int32, (SUB_TILE, SUB_TILE), 1)
    earlier = jnp.where(r < c, 1.0, 0.0).astype(jnp.bfloat16)
    crow = lax.broadcasted_iota(jnp.int32, (ROUTE_LANES, SUB_TILE), 0).astype(jnp.float32)
    oh = [crow == cl for cl in cls]
    ohf = [jnp.where(o_, 1.0, 0.0) for o_ in oh]
    inside = [dot(o_.astype(jnp.bfloat16), earlier) for o_ in ohf]
    seen = carry_ref[...]
    for i in subs:
        rank = jnp.sum(jnp.where(oh[i], inside[i] + seen, 0.0), axis=0, keepdims=True)
        slab_ref[:, i * SUB_TILE:(i + 1) * SUB_TILE] = jnp.where(
            sub == 0, cls[i], jnp.where(sub == 1, rank, 0.0))
        seen = seen + jnp.sum(ohf[i], axis=1, keepdims=True)
    carry_ref[...] = seen
    counts_ref[...] = seen


def _attn_route(x1, kt, v, g_x, w_q, w_o, g_f, w_rt, b_rt):
    b, s, d = x1.shape
    m = v.shape[1]
    ts = min(ATTN_TILE, s)
    nt = s // ts
    const2 = lambda i, j: (0, 0)
    return pl.pallas_call(
        _attn_route_kernel,
        grid=(b, nt),
        in_specs=[
            pl.BlockSpec((1, ts, d), lambda i, j: (i, j, 0)),
            pl.BlockSpec((1, d, m), lambda i, j: (i, 0, 0)),
            pl.BlockSpec((1, m, d), lambda i, j: (i, 0, 0)),
            pl.BlockSpec(g_x.shape, const2),
            pl.BlockSpec(w_q.shape, const2),
            pl.BlockSpec(w_o.shape, const2),
            pl.BlockSpec(g_f.shape, const2),
            pl.BlockSpec(w_rt.shape, const2),
            pl.BlockSpec(b_rt.shape, const2),
        ],
        out_specs=[
            pl.BlockSpec((1, ts, d), lambda i, j: (i, j, 0)),
            pl.BlockSpec((SUBLANES, ts), lambda i, j: (0, i * nt + j)),
            pl.BlockSpec((ROUTE_LANES, 1), const2),
        ],
        out_shape=[
            jax.ShapeDtypeStruct((b, s, d), jnp.float32),
            jax.ShapeDtypeStruct((SUBLANES, b * s), jnp.float32),
            jax.ShapeDtypeStruct((ROUTE_LANES, 1), jnp.float32),
        ],
        scratch_shapes=[pltpu.VMEM((ROUTE_LANES, 1), jnp.float32)],
        compiler_params=pltpu.CompilerParams(
            dimension_semantics=("arbitrary", "arbitrary"), vmem_limit_bytes=VMEM_LIMIT_BYTES),
        name="attn_route",
    )(x1, kt, v, g_x, w_q, w_o, g_f, w_rt, b_rt)


def _dest_kernel(slab_ref, starts_ref, o_ref):
    slab = slab_ref[...]
    tt = slab.shape[1]
    crow = lax.broadcasted_iota(jnp.int32, (ROUTE_LANES, tt), 0).astype(jnp.float32)
    start = jnp.sum(jnp.where(crow == slab[0:1, :], starts_ref[...], 0.0), axis=0, keepdims=True)
    sub = lax.broadcasted_iota(jnp.int32, slab.shape, 0)
    o_ref[...] = jnp.where(sub == 0, start + slab[1:2, :], 0.0).astype(jnp.int32)


def _dest_rows(slab, starts_col):
    t = slab.shape[1]
    tt = min(2048, t)
    return pl.pallas_call(
        _dest_kernel,
        grid=(t // tt,),
        in_specs=[pl.BlockSpec((SUBLANES, tt), lambda i: (0, i)),
                  pl.BlockSpec((ROUTE_LANES, 1), lambda i: (0, 0))],
        out_specs=pl.BlockSpec((SUBLANES, tt), lambda i: (0, i)),
        out_shape=jax.ShapeDtypeStruct((SUBLANES, t), jnp.int32),
        compiler_params=pltpu.CompilerParams(dimension_semantics=("arbitrary",)),
        name="dest_rows",
    )(slab, starts_col)


def _sc_split(t):
    info = plsc.get_sparse_core_info()
    workers = info.num_cores * info.num_subcores
    per_w, rem = divmod(t, workers)
    nchunk, rem2 = divmod(per_w, SC_CHUNK_ROWS)
    assert rem == 0 and rem2 == 0 and nchunk % SC_BUFFERS == 0, (t, workers, SC_CHUNK_ROWS)
    return info.num_cores, per_w, nchunk


def _sc_ring(nchunk, load, store):
    ahead = SC_BUFFERS - 1
    for k in range(ahead):
        for cp in load(k, k):
            cp.start()

    @pl.loop(0, nchunk, step=SC_BUFFERS)
    def _(c):
        for b in range(SC_BUFFERS):
            cc = c + b
            refill = (b + ahead) % SC_BUFFERS

            @pl.when(cc >= 1)
            def _():
                for cp in store(cc - 1, refill):
                    cp.wait()

            @pl.when(cc + ahead < nchunk)
            def _():
                for cp in load(cc + ahead, refill):
                    cp.start()

            for cp in load(cc, b):
                cp.wait()
            for cp in store(cc, b):
                cp.start()

    for cp in store(nchunk - 1, (nchunk - 1) % SC_BUFFERS):
        cp.wait()


def _dispatch(x2, dest, cap):
    t, d = x2.shape
    nc, per_w, nchunk = _sc_split(t)
    ch = SC_CHUNK_ROWS
    mesh = plsc.VectorSubcoreMesh(core_axis_name="c", subcore_axis_name="s")
    dma = pltpu.SemaphoreType.DMA

    @functools.partial(
        pl.kernel, mesh=mesh,
        out_type=jax.ShapeDtypeStruct((cap, d), jnp.float32),
        scratch_types=([pltpu.VMEM((ch,), jnp.int32)] * SC_BUFFERS
                       + [pltpu.VMEM((ch, d), jnp.float32)] * SC_BUFFERS + [dma] * (3 * SC_BUFFERS)),
    )
    def dispatch_sc(x_hbm, dest_hbm, xs_hbm, *scratch):
        n = SC_BUFFERS
        idx, rows, isem, rsem, ssem = (scratch[k * n:(k + 1) * n] for k in range(5))
        wid = lax.axis_index("s") * nc + lax.axis_index("c")
        base = wid * per_w

        def load(c, b):
            src = pl.ds(base + c * ch, ch)
            return (pltpu.make_async_copy(dest_hbm.at[src], idx[b], isem[b]),
                    pltpu.make_async_copy(x_hbm.at[src], rows[b], rsem[b]))

        def store(c, b):
            return (pltpu.make_async_copy(rows[b], xs_hbm.at[idx[b]], ssem[b]),)

        _sc_ring(nchunk, load, store)

    return dispatch_sc(x2, dest)


def _unsort(ys, dest, t):
    d = ys.shape[1]
    nc, per_w, nchunk = _sc_split(t)
    ch = SC_CHUNK_ROWS
    mesh = plsc.VectorSubcoreMesh(core_axis_name="c", subcore_axis_name="s")
    dma = pltpu.SemaphoreType.DMA

    @functools.partial(
        pl.kernel, mesh=mesh,
        out_type=jax.ShapeDtypeStruct((t, d), jnp.float32),
        scratch_types=([pltpu.VMEM((per_w,), jnp.int32)]
                       + [pltpu.VMEM((ch, d), jnp.float32)] * SC_BUFFERS + [dma] * (2 * SC_BUFFERS)),
    )
    def unsort_sc(ys_hbm, dest_hbm, out_hbm, idx_v, *scratch):
        n = SC_BUFFERS
        rows, gsem, wsem = (scratch[k * n:(k + 1) * n] for k in range(3))
        wid = lax.axis_index("s") * nc + lax.axis_index("c")
        base = wid * per_w
        pltpu.sync_copy(dest_hbm.at[pl.ds(base, per_w)], idx_v)

        def load(c, b):
            return (pltpu.make_async_copy(ys_hbm.at[idx_v.at[pl.ds(c * ch, ch)]], rows[b], gsem[b]),)

        def store(c, b):
            return (pltpu.make_async_copy(rows[b], out_hbm.at[pl.ds(base + c * ch, ch)], wsem[b]),)

        _sc_ring(nchunk, load, store)

    return unsort_sc(ys, dest)


def _expert_kernel(grp_ref, ea_ref, eb_ref, nvalid_ref, nact_ref,
                   xs_ref, gf_ref, wr_ref, br_ref, gfin_ref, wg_hbm, wu_hbm, wd_hbm,
                   o_ref, wgb, wub, wdb, sg, su, sd, sem, cnt_ref):
    s = pl.program_id(0)
    blocks = tuple(BLOCKS_PER_STEP * s + i for i in range(BLOCKS_PER_STEP))
    g = grp_ref[blocks[0]]

    def stage(e):
        slot = e % 2
        return (pltpu.make_async_copy(wg_hbm.at[e], sg.at[slot], sem.at[slot, 0]),
                pltpu.make_async_copy(wu_hbm.at[e], su.at[slot], sem.at[slot, 1]),
                pltpu.make_async_copy(wd_hbm.at[e], sd.at[slot], sem.at[slot, 2]))

    def start_next():
        @pl.when(cnt_ref[0] < N_EXPERTS)
        def _():
            for cp in stage(cnt_ref[0]):
                cp.start()
            cnt_ref[0] = cnt_ref[0] + 1

    @pl.when(s == 0)
    def _():
        cnt_ref[0] = 0
        cnt_ref[1] = 0
        start_next()
        start_next()

    active = blocks[0] < nact_ref[0]

    @pl.when(active)
    def _():
        need = g * EXPERTS_PER_GROUP + functools.reduce(
            jnp.maximum, [eb_ref[j] for j in blocks])

        def load(e, carry):
            for cp in stage(e):
                cp.wait()
            slot = e % 2
            k = e % EXPERTS_PER_GROUP
            wgb[k] = sg[slot].astype(jnp.bfloat16)
            wub[k] = su[slot].astype(jnp.bfloat16)
            wdb[k] = sd[slot].astype(jnp.bfloat16)
            cnt_ref[1] = e + 1
            start_next()
            return carry

        lax.fori_loop(cnt_ref[1], need + 1, load, 0)

        lane = lax.broadcasted_iota(jnp.int32, (ROW_BLOCK, ROUTE_LANES), 1)
        is_g = lane < N_GROUPS
        lo = EXPERT_LANE0 + EXPERTS_PER_GROUP * g
        nblk = len(blocks)
        rowid = lax.broadcasted_iota(jnp.int32, (ROW_BLOCK, 1), 0)
        xs = [jnp.where(rowid < nvalid_ref[blocks[i]],
                        xs_ref[i * ROW_BLOCK:(i + 1) * ROW_BLOCK, :], 0.0) for i in range(nblk)]
        h3 = [_rms(x, gf_ref[...]).astype(jnp.bfloat16) for x in xs]
        lg = [jnp.dot(h, wr_ref[...], preferred_element_type=jnp.float32) + br_ref[...]
              for h in h3]

        def gates(lgi, j):
            def pick(idx):
                return jnp.sum(jnp.where(lane == idx, lgi, 0.0), axis=-1, keepdims=True)

            gmax = jnp.max(jnp.where(is_g, lgi, _NEG), axis=-1, keepdims=True)
            den = jnp.sum(jnp.where(is_g, jnp.exp(jnp.where(is_g, lgi, _NEG) - gmax), 0.0),
                          axis=-1, keepdims=True)
            grp_p = jnp.exp(pick(g) - gmax) / den
            la = pick(lo + ea_ref[j])
            lb = pick(lo + eb_ref[j])
            m = jnp.maximum(la, lb)
            pa = jnp.exp(la - m)
            pb = jnp.exp(lb - m)
            return grp_p * pa / (pa + pb), grp_p * pb / (pa + pb)

        gate = [gates(lg[i], blocks[i]) for i in range(nblk)]
        ys = list(xs)
        for which, e_ref in enumerate((ea_ref, eb_ref)):
            ks = [e_ref[j] for j in blocks]
            gg = [jnp.dot(h3[i], wgb[ks[i]], preferred_element_type=jnp.float32)
                  for i in range(nblk)]
            uu = [jnp.dot(h3[i], wub[ks[i]], preferred_element_type=jnp.float32)
                  for i in range(nblk)]
            act = [(gg[i] * (1.0 / (1.0 + jnp.exp(-gg[i]))) * uu[i]
                    * gate[i][which]).astype(jnp.bfloat16) for i in range(nblk)]
            ys = [ys[i] + jnp.dot(act[i], wdb[ks[i]], preferred_element_type=jnp.float32)
                  for i in range(nblk)]
        for i in range(nblk):
            o_ref[i * ROW_BLOCK:(i + 1) * ROW_BLOCK, :] = _rms(ys[i], gfin_ref[...])

    @pl.when(jnp.logical_not(active))
    def _():
        o_ref[...] = jnp.zeros_like(o_ref)

    @pl.when(s == pl.num_programs(0) - 1)
    def _():
        def drain(e, carry):
            for cp in stage(e):
                cp.wait()
            return carry

        lax.fori_loop(cnt_ref[1], cnt_ref[0], drain, 0)


def _experts(xs, blk_grp, blk_a, blk_b, nvalid, nact, g_ffn, w_r, b_r, g_final, w_gate, w_up, w_down):
    cap, d = xs.shape
    de = w_gate.shape[2]
    step_rows = BLOCKS_PER_STEP * ROW_BLOCK
    steps = cap // step_rows
    pre = lambda f: (lambda s, gr, ea, eb, nv, na: f(s, na))
    const2 = pre(lambda s, na: (0, 0))
    last_step = lambda na: (na[0] - 1) // BLOCKS_PER_STEP
    hbm = pl.BlockSpec(memory_space=pl.ANY)
    grid_spec = pltpu.PrefetchScalarGridSpec(
        num_scalar_prefetch=5,
        grid=(steps,),
        in_specs=[
            pl.BlockSpec((step_rows, d), pre(lambda s, na: (jnp.minimum(s, last_step(na)), 0))),
            pl.BlockSpec(g_ffn.shape, const2),
            pl.BlockSpec(w_r.shape, const2),
            pl.BlockSpec(b_r.shape, const2),
            pl.BlockSpec(g_final.shape, const2),
            hbm, hbm, hbm,
        ],
        out_specs=pl.BlockSpec((step_rows, d), pre(lambda s, na: (s, 0))),
        scratch_shapes=[
            pltpu.VMEM((EXPERTS_PER_GROUP, d, de), jnp.bfloat16),
            pltpu.VMEM((EXPERTS_PER_GROUP, d, de), jnp.bfloat16),
            pltpu.VMEM((EXPERTS_PER_GROUP, de, d), jnp.bfloat16),
            pltpu.VMEM((2, d, de), jnp.float32),
            pltpu.VMEM((2, d, de), jnp.float32),
            pltpu.VMEM((2, de, d), jnp.float32),
            pltpu.SemaphoreType.DMA((2, 3)),
            pltpu.SMEM((2,), jnp.int32),
        ],
    )
    return pl.pallas_call(
        _expert_kernel,
        grid_spec=grid_spec,
        out_shape=jax.ShapeDtypeStruct((cap, d), jnp.float32),
        compiler_params=pltpu.CompilerParams(
            dimension_semantics=("arbitrary",), vmem_limit_bytes=VMEM_LIMIT_BYTES),
        name="experts",
    )(blk_grp, blk_a, blk_b, nvalid, nact, xs, g_ffn, w_r, b_r, g_final, w_gate, w_up, w_down)


def _moe_final(x2, slab, counts_col, g_ffn, w_r, b_r, g_final, w_gate, w_up, w_down):
    t, d = x2.shape
    nb = t // ROW_BLOCK + N_CLASSES + N_GROUPS * (BLOCKS_PER_STEP - 1)
    assert nb % BLOCKS_PER_STEP == 0
    counts = counts_col[:N_CLASSES, 0].astype(jnp.int32)
    nblk = (counts + ROW_BLOCK - 1) // ROW_BLOCK
    grp_blocks = jnp.sum(nblk.reshape(N_GROUPS, N_PAIRS), axis=1)
    nblk = nblk.reshape(N_GROUPS, N_PAIRS).at[:, N_PAIRS - 1].add(
        (-grp_blocks) % BLOCKS_PER_STEP).reshape(N_CLASSES)
    blk_end = jnp.cumsum(nblk)
    blk_start = blk_end - nblk
    nact = blk_end[-1]
    j = jnp.arange(nb, dtype=jnp.int32)
    blk_class = jnp.minimum(
        jnp.sum((blk_end[None, :] <= j[:, None]).astype(jnp.int32), axis=1), N_CLASSES - 1)
    blk_class = jnp.where(j < nact, blk_class, blk_class[jnp.maximum(nact - 1, 0)])
    nvalid = jnp.where(j < nact, jnp.clip(
        counts[blk_class] - (j - blk_start[blk_class]) * ROW_BLOCK, 0, ROW_BLOCK), 0)
    pair_a = jnp.array([p[0] for p in PAIRS], jnp.int32)
    pair_b = jnp.array([p[1] for p in PAIRS], jnp.int32)
    blk_grp = blk_class // N_PAIRS
    blk_a = pair_a[blk_class % N_PAIRS]
    blk_b = pair_b[blk_class % N_PAIRS]
    starts_col = jnp.zeros((ROUTE_LANES, 1), jnp.float32).at[:N_CLASSES, 0].set(
        (blk_start * ROW_BLOCK).astype(jnp.float32))

    dest = _dest_rows(slab, starts_col)[0]
    xs = _dispatch(x2, dest, nb * ROW_BLOCK)
    ys = _experts(xs, blk_grp, blk_a, blk_b, nvalid, nact[None], g_ffn, w_r, b_r, g_final,
                  w_gate, w_up, w_down)
    return _unsort(ys, dest, t)


def kernel(x, mem, g_mix, w_in, conv_w, g_v, w_s, b_s, g_out_conv, g_out_gmlp, w_out, g_xattn,
           g_mem, w_q, w_k, w_v, w_o, g_ffn, w_grp, b_grp, w_rt, b_rt, w_gate, w_up, w_down,
           g_final):
    b, s, d = x.shape
    assert g_mix.shape[0] == 1, "the final norm is fused into the single layer's expert kernel"
    assert N_CLASSES <= ROUTE_LANES
    bf = jnp.bfloat16
    kt, v = _kv_proj(mem, g_mem[0][None], w_k[0].astype(bf), w_v[0].astype(bf))

    bias = jnp.repeat(b_s[0].T, GMLP_HEAD_DIM, axis=1)
    x1 = _mixer(x, g_mix[0][None], w_in[0].astype(bf), conv_w[0], g_v[0][None], w_s[0], bias,
                g_out_conv[0][None], g_out_gmlp[0][None], w_out[0].astype(bf))

    pad = ROUTE_LANES - N_GROUPS - N_EXPERTS
    w_r = jnp.concatenate([w_grp[0], w_rt[0], jnp.zeros((d, pad), jnp.float32)], axis=1).astype(bf)
    b_r = jnp.concatenate([b_grp[0], b_rt[0], jnp.zeros((pad,), jnp.float32)])[None]
    gap = ROUTE_EXPERT_ROW0 - N_GROUPS
    tail = ROUTE_LANES - ROUTE_EXPERT_ROW0 - N_EXPERTS
    w_r_t = jnp.concatenate([w_grp[0].T, jnp.zeros((gap, d), jnp.float32), w_rt[0].T,
                             jnp.zeros((tail, d), jnp.float32)], axis=0).astype(bf)
    b_r_t = jnp.concatenate([b_grp[0], jnp.zeros((gap,), jnp.float32), b_rt[0],
                             jnp.zeros((tail,), jnp.float32)])[:, None]
    x2, slab, counts_col = _attn_route(x1, kt, v, g_xattn[0][None], w_q[0].astype(bf),
                                       w_o[0].astype(bf), g_ffn[0][None], w_r_t, b_r_t)
    out = _moe_final(x2.reshape(b * s, d), slab, counts_col, g_ffn[0][None], w_r, b_r,
                     g_final[None], w_gate[0], w_up[0], w_down[0])
    return out.reshape(b, s, d)
```

```python
import functools

import jax
import jax.numpy as jnp
from jax import lax
from jax.experimental import pallas as pl
from jax.experimental.pallas import tpu as pltpu
from jax.experimental.pallas import tpu_sc as plsc

EPS = 1e-6
CONV_GROUP_WIDTH = 512
GMLP_HEADS = 8
GMLP_HEAD_DIM = 64
CHUNK = 128
XA_HEADS = 4
N_GROUPS = 4
EXPERTS_PER_GROUP = 8
N_EXPERTS = N_GROUPS * EXPERTS_PER_GROUP
ROUTE_LANES = 128
EXPERT_LANE0 = N_GROUPS
ROUTE_EXPERT_ROW0 = 8

N_PAIRS = EXPERTS_PER_GROUP * (EXPERTS_PER_GROUP - 1) // 2
N_CLASSES = N_GROUPS * N_PAIRS
PAIRS = [(a, b) for a in range(EXPERTS_PER_GROUP) for b in range(a + 1, EXPERTS_PER_GROUP)]

MIXER_TILE = 1024
ATTN_TILE = 2048
SUB_TILE = 256
DEST_TILE = 8192
ROW_BLOCK = 128
BLOCKS_PER_STEP = 4
SC_CHUNK_ROWS = 16
SC_BUFFERS = 4
VMEM_LIMIT_BYTES = 56 * 1024 * 1024

SUBLANES = 8

_NEG = -1e30


def _rms(x, g):
    return x * lax.rsqrt(jnp.mean(x * x, axis=-1, keepdims=True) + EPS) * g


def _gelu_tanh(x):
    return 0.5 * x * (1.0 + jnp.tanh(0.7978845608028654 * (x + 0.044715 * (x * x * x))))


def _kv_kernel(mem_ref, g_ref, wk_ref, wv_ref, kt_ref, v_ref):
    m = _rms(mem_ref[0], g_ref[...]).astype(jnp.bfloat16)
    k = jnp.dot(m, wk_ref[...], preferred_element_type=jnp.float32)
    v = jnp.dot(m, wv_ref[...], preferred_element_type=jnp.float32)
    kt_ref[0] = k.T.astype(jnp.bfloat16)
    v_ref[0] = v.astype(jnp.bfloat16)


def _kv_proj(mem, g_mem, w_k, w_v):
    b, m, d = mem.shape
    const = lambda i: (0, 0)
    return pl.pallas_call(
        _kv_kernel,
        grid=(b,),
        in_specs=[
            pl.BlockSpec((1, m, d), lambda i: (i, 0, 0)),
            pl.BlockSpec((1, d), const),
            pl.BlockSpec((d, d), const),
            pl.BlockSpec((d, d), const),
        ],
        out_specs=[
            pl.BlockSpec((1, d, m), lambda i: (i, 0, 0)),
            pl.BlockSpec((1, m, d), lambda i: (i, 0, 0)),
        ],
        out_shape=[
            jax.ShapeDtypeStruct((b, d, m), jnp.bfloat16),
            jax.ShapeDtypeStruct((b, m, d), jnp.bfloat16),
        ],
        compiler_params=pltpu.CompilerParams(
            dimension_semantics=("arbitrary",), vmem_limit_bytes=VMEM_LIMIT_BYTES),
        name="kv_proj",
    )(mem, g_mem, w_k, w_v)


def _mixer_kernel(x_ref, gmix_ref, win_ref, convw_ref, gv_ref, ws_ref, bias_ref,
                  goc_ref, gog_ref, wout_ref, o_ref, zbuf_ref):
    ts = x_ref.shape[1]
    w = CONV_GROUP_WIDTH
    hw = 2 * GMLP_HEAD_DIM
    nsub = ts // SUB_TILE

    def dot(a, b):
        return jnp.dot(a, b, preferred_element_type=jnp.float32)

    @pl.when(pl.program_id(1) == 0)
    def _():
        zbuf_ref[0:8, :] = jnp.zeros((8, w), jnp.float32)

    @pl.when(pl.program_id(1) != 0)
    def _():
        zbuf_ref[0:8, :] = zbuf_ref[ts:ts + 8, :]

    low = lax.broadcasted_iota(jnp.int32, (SUB_TILE, hw), 1) < GMLP_HEAD_DIM
    row = lax.broadcasted_iota(jnp.int32, (CHUNK, CHUNK), 0)
    colid = lax.broadcasted_iota(jnp.int32, (CHUNK, CHUNK), 1)
    causal = row >= colid
    ws = [jnp.where(causal, ws_ref[hd], 0.0).astype(jnp.bfloat16) for hd in range(GMLP_HEADS)]
    npair = SUB_TILE // CHUNK // 2
    lo = lax.broadcasted_iota(jnp.int32, (CHUNK, hw), 1) < GMLP_HEAD_DIM
    swap = lambda a: pltpu.roll(a, GMLP_HEAD_DIM, axis=1)

    def project(i):
        r0 = i * SUB_TILE
        xt = x_ref[0, r0:r0 + SUB_TILE, :]
        h = _rms(xt, gmix_ref[...]).astype(jnp.bfloat16)
        p = [dot(h, win_ref[:, k * w:(k + 1) * w]) for k in range(5)]
        z = p[1] * p[2]
        zbuf_ref[8 + r0:8 + r0 + SUB_TILE, :] = z
        return dict(xt=xt, gate_b=p[0], z=z, u=p[3], v=p[4], r0=r0)

    def gate_and_norm(st):
        r0 = st["r0"]
        zc = (convw_ref[0:1, :] * zbuf_ref[6 + r0:6 + r0 + SUB_TILE, :]
              + convw_ref[1:2, :] * zbuf_ref[7 + r0:7 + r0 + SUB_TILE, :]
              + convw_ref[2:3, :] * st["z"])
        st["ya"] = _rms(st["gate_b"] * zc, goc_ref[...]).astype(jnp.bfloat16)
        st["u"] = _gelu_tanh(st["u"])
        v = _gelu_tanh(st["v"])
        v2 = v * v
        ss_cols = []
        for k in range(GMLP_HEADS // 2):
            col = v2[:, hw * k:hw * (k + 1)]
            ss_cols.append(jnp.where(low,
                                     jnp.sum(jnp.where(low, col, 0.0), axis=1, keepdims=True),
                                     jnp.sum(jnp.where(low, 0.0, col), axis=1, keepdims=True)))
        ss = jnp.concatenate(ss_cols, axis=1)
        st["vn"] = v * lax.rsqrt(ss * (1.0 / GMLP_HEAD_DIM) + EPS) * gv_ref[...]

    def mix_positions(st):
        vn = st["vn"]
        s_cols = [[None] * (GMLP_HEADS // 2) for _ in range(2 * npair)]
        for hp in range(GMLP_HEADS // 2):
            cols = [vn[c * CHUNK:(c + 1) * CHUNK, hw * hp:hw * (hp + 1)] for c in range(2 * npair)]
            swapped = [swap(a) for a in cols]
            rhs_even = jnp.concatenate(
                [jnp.where(lo, cols[2 * p], swapped[2 * p + 1]) for p in range(npair)], axis=1)
            rhs_odd = jnp.concatenate(
                [jnp.where(lo, swapped[2 * p], cols[2 * p + 1]) for p in range(npair)], axis=1)
            out_e = dot(ws[2 * hp], rhs_even.astype(jnp.bfloat16))
            out_o = dot(ws[2 * hp + 1], rhs_odd.astype(jnp.bfloat16))
            for p in range(npair):
                e = out_e[:, hw * p:hw * (p + 1)]
                o = out_o[:, hw * p:hw * (p + 1)]
                s_cols[2 * p][hp] = jnp.where(lo, e, swap(o))
                s_cols[2 * p + 1][hp] = jnp.where(lo, swap(e), o)
        st["s"] = jnp.concatenate(
            [jnp.concatenate(c, axis=1) + bias_ref[...] for c in s_cols], axis=0)

    def output(st):
        r0 = st["r0"]
        yb = _rms(st["u"] * st["s"], gog_ref[...]).astype(jnp.bfloat16)
        o_ref[0, r0:r0 + SUB_TILE, :] = (st["xt"] + dot(st["ya"], wout_ref[0:w, :])
                                          + dot(yb, wout_ref[w:2 * w, :]))

    phases = (gate_and_norm, mix_positions, output)
    states = []
    for step in range(nsub + len(phases)):
        if step < nsub:
            states.append(project(step))
        for k, phase in enumerate(phases):
            i = step - 1 - k
            if 0 <= i < nsub:
                phase(states[i])


def _mixer(x, g_mix, w_in, conv_w, g_v, w_s, bias, g_oc, g_og, w_out):
    b, s, d = x.shape
    ts = min(MIXER_TILE, s)
    const2 = lambda i, j: (0, 0)
    const3 = lambda i, j: (0, 0, 0)
    return pl.pallas_call(
        _mixer_kernel,
        grid=(b, s // ts),
        in_specs=[
            pl.BlockSpec((1, ts, d), lambda i, j: (i, j, 0)),
            pl.BlockSpec(g_mix.shape, const2),
            pl.BlockSpec(w_in.shape, const2),
            pl.BlockSpec(conv_w.shape, const2),
            pl.BlockSpec(g_v.shape, const2),
            pl.BlockSpec(w_s.shape, const3),
            pl.BlockSpec(bias.shape, const2),
            pl.BlockSpec(g_oc.shape, const2),
            pl.BlockSpec(g_og.shape, const2),
            pl.BlockSpec(w_out.shape, const2),
        ],
        out_specs=pl.BlockSpec((1, ts, d), lambda i, j: (i, j, 0)),
        out_shape=jax.ShapeDtypeStruct((b, s, d), jnp.float32),
        scratch_shapes=[pltpu.VMEM((ts + 8, CONV_GROUP_WIDTH), jnp.float32)],
        compiler_params=pltpu.CompilerParams(
            dimension_semantics=("arbitrary", "arbitrary"), vmem_limit_bytes=VMEM_LIMIT_BYTES),
        name="mixer",
    )(x, g_mix, w_in, conv_w, g_v, w_s, bias, g_oc, g_og, w_out)


def _attn_route_kernel(x_ref, kt_ref, v_ref, gx_ref, wq_ref, wo_ref, gf_ref, wrt_ref, brt_ref,
                       x2_ref, slab_ref, counts_ref, carry_ref):
    ts = x_ref.shape[1]
    d = x_ref.shape[2]
    hd = d // XA_HEADS
    nsub = ts // SUB_TILE
    first = (pl.program_id(0) == 0) & (pl.program_id(1) == 0)

    @pl.when(first)
    def _():
        carry_ref[...] = jnp.zeros_like(carry_ref)

    def dot(a, b):
        return jnp.dot(a, b, preferred_element_type=jnp.float32)

    subs = range(nsub)
    x1 = [x_ref[0, i * SUB_TILE:(i + 1) * SUB_TILE, :] for i in subs]
    h2 = [_rms(x, gx_ref[...]).astype(jnp.bfloat16) for x in x1]
    q = [dot(h, wq_ref[...]).astype(jnp.bfloat16) for h in h2]
    heads = [[] for _ in subs]
    for a in range(XA_HEADS):
        sc = [dot(q[i][:, a * hd:(a + 1) * hd], kt_ref[0, a * hd:(a + 1) * hd, :]) * (hd ** -0.5)
              for i in subs]
        p = [jnp.exp(s_ - jnp.max(s_, axis=-1, keepdims=True)) for s_ in sc]
        l = [jnp.sum(p_, axis=-1, keepdims=True) for p_ in p]
        o = [dot(p[i].astype(jnp.bfloat16), v_ref[0, :, a * hd:(a + 1) * hd]) for i in subs]
        for i in subs:
            heads[i].append((o[i] / l[i]).astype(jnp.bfloat16))
    x2 = [x1[i] + dot(jnp.concatenate(heads[i], axis=1), wo_ref[...]) for i in subs]
    for i in subs:
        x2_ref[0, i * SUB_TILE:(i + 1) * SUB_TILE, :] = x2[i]

    h3 = [_rms(x, gf_ref[...]).astype(jnp.bfloat16) for x in x2]
    lgs = [lax.dot_general(wrt_ref[...], h, (((1,), (1,)), ((), ())),
                           preferred_element_type=jnp.float32) + brt_ref[...] for h in h3]
    sub = lax.broadcasted_iota(jnp.int32, (EXPERTS_PER_GROUP, SUB_TILE), 0).astype(jnp.float32)
    big = float(EXPERTS_PER_GROUP)
    is_g = sub < N_GROUPS

    def classify(lg):
        glog = lg[0:EXPERTS_PER_GROUP, :]
        gmax = jnp.max(jnp.where(is_g, glog, _NEG), axis=0, keepdims=True)
        gidx = jnp.min(jnp.where(is_g & (glog == gmax), sub, big), axis=0, keepdims=True)
        el = lg[ROUTE_EXPERT_ROW0:ROUTE_EXPERT_ROW0 + EXPERTS_PER_GROUP, :]
        for grp in range(1, N_GROUPS):
            r0 = ROUTE_EXPERT_ROW0 + grp * EXPERTS_PER_GROUP
            el = jnp.where(gidx == grp, lg[r0:r0 + EXPERTS_PER_GROUP, :], el)
        t1 = jnp.max(el, axis=0, keepdims=True)
        i1 = jnp.min(jnp.where(el == t1, sub, big), axis=0, keepdims=True)
        rest = sub != i1
        t2 = jnp.max(jnp.where(rest, el, _NEG), axis=0, keepdims=True)
        i2 = jnp.min(jnp.where(rest & (el == t2), sub, big), axis=0, keepdims=True)
        a = jnp.minimum(i1, i2)
        b = jnp.maximum(i1, i2)
        pair = a * (2 * EXPERTS_PER_GROUP - 1 - a) * 0.5 + (b - a - 1.0)
        return gidx * N_PAIRS + pair

    cls = [classify(lg) for lg in lgs]

    r = lax.broadcasted_iota(jnp.int32, (SUB_TILE, SUB_TILE), 0)
    c = lax.broadcasted_iota(jnp.int32, (SUB_TILE, SUB_TILE), 1)
    earlier = jnp.where(r < c, 1.0, 0.0).astype(jnp.bfloat16)
    crow = lax.broadcasted_iota(jnp.int32, (ROUTE_LANES, SUB_TILE), 0).astype(jnp.float32)
    oh = [crow == cl for cl in cls]
    ohf = [jnp.where(o_, 1.0, 0.0) for o_ in oh]
    inside = [dot(o_.astype(jnp.bfloat16), earlier) for o_ in ohf]
    seen = carry_ref[...]
    for i in subs:
        rank = jnp.sum(jnp.where(oh[i], inside[i] + seen, 0.0), axis=0, keepdims=True)
        slab_ref[:, i * SUB_TILE:(i + 1) * SUB_TILE] = jnp.where(
            sub == 0, cls[i], jnp.where(sub == 1, rank, 0.0))
        seen = seen + jnp.sum(ohf[i], axis=1, keepdims=True)
    carry_ref[...] = seen
    counts_ref[...] = seen


def _attn_route(x1, kt, v, g_x, w_q, w_o, g_f, w_rt, b_rt):
    b, s, d = x1.shape
    m = v.shape[1]
    ts = min(ATTN_TILE, s)
    nt = s // ts
    const2 = lambda i, j: (0, 0)
    return pl.pallas_call(
        _attn_route_kernel,
        grid=(b, nt),
        in_specs=[
            pl.BlockSpec((1, ts, d), lambda i, j: (i, j, 0)),
            pl.BlockSpec((1, d, m), lambda i, j: (i, 0, 0)),
            pl.BlockSpec((1, m, d), lambda i, j: (i, 0, 0)),
            pl.BlockSpec(g_x.shape, const2),
            pl.BlockSpec(w_q.shape, const2),
            pl.BlockSpec(w_o.shape, const2),
            pl.BlockSpec(g_f.shape, const2),
            pl.BlockSpec(w_rt.shape, const2),
            pl.BlockSpec(b_rt.shape, const2),
        ],
        out_specs=[
            pl.BlockSpec((1, ts, d), lambda i, j: (i, j, 0)),
            pl.BlockSpec((SUBLANES, ts), lambda i, j: (0, i * nt + j)),
            pl.BlockSpec((ROUTE_LANES, 1), const2),
        ],
        out_shape=[
            jax.ShapeDtypeStruct((b, s, d), jnp.float32),
            jax.ShapeDtypeStruct((SUBLANES, b * s), jnp.float32),
            jax.ShapeDtypeStruct((ROUTE_LANES, 1), jnp.float32),
        ],
        scratch_shapes=[pltpu.VMEM((ROUTE_LANES, 1), jnp.float32)],
        compiler_params=pltpu.CompilerParams(
            dimension_semantics=("arbitrary", "arbitrary"), vmem_limit_bytes=VMEM_LIMIT_BYTES),
        name="attn_route",
    )(x1, kt, v, g_x, w_q, w_o, g_f, w_rt, b_rt)


def _dest_kernel(slab_ref, starts_ref, o_ref):
    slab = slab_ref[...]
    tt = slab.shape[1]
    crow = lax.broadcasted_iota(jnp.int32, (ROUTE_LANES, tt), 0).astype(jnp.float32)
    start = jnp.sum(jnp.where(crow == slab[0:1, :], starts_ref[...], 0.0), axis=0, keepdims=True)
    sub = lax.broadcasted_iota(jnp.int32, slab.shape, 0)
    o_ref[...] = jnp.where(sub == 0, start + slab[1:2, :], 0.0).astype(jnp.int32)


def _dest_rows(slab, starts_col):
    t = slab.shape[1]
    tt = min(DEST_TILE, t)
    return pl.pallas_call(
        _dest_kernel,
        grid=(t // tt,),
        in_specs=[pl.BlockSpec((SUBLANES, tt), lambda i: (0, i)),
                  pl.BlockSpec((ROUTE_LANES, 1), lambda i: (0, 0))],
        out_specs=pl.BlockSpec((SUBLANES, tt), lambda i: (0, i)),
        out_shape=jax.ShapeDtypeStruct((SUBLANES, t), jnp.int32),
        compiler_params=pltpu.CompilerParams(dimension_semantics=("arbitrary",)),
        name="dest_rows",
    )(slab, starts_col)


def _sc_split(t):
    info = plsc.get_sparse_core_info()
    workers = info.num_cores * info.num_subcores
    per_w, rem = divmod(t, workers)
    nchunk, rem2 = divmod(per_w, SC_CHUNK_ROWS)
    assert rem == 0 and rem2 == 0 and nchunk % SC_BUFFERS == 0, (t, workers, SC_CHUNK_ROWS)
    return info.num_cores, per_w, nchunk


def _sc_ring(nchunk, load, store):
    ahead = SC_BUFFERS - 1
    for k in range(ahead):
        for cp in load(k, k):
            cp.start()

    @pl.loop(0, nchunk, step=SC_BUFFERS)
    def _(c):
        for b in range(SC_BUFFERS):
            cc = c + b
            refill = (b + ahead) % SC_BUFFERS

            @pl.when(cc >= 1)
            def _():
                for cp in store(cc - 1, refill):
                    cp.wait()

            @pl.when(cc + ahead < nchunk)
            def _():
                for cp in load(cc + ahead, refill):
                    cp.start()

            for cp in load(cc, b):
                cp.wait()
            for cp in store(cc, b):
                cp.start()

    for cp in store(nchunk - 1, (nchunk - 1) % SC_BUFFERS):
        cp.wait()


def _dispatch(x2, dest, cap):
    t, d = x2.shape
    nc, per_w, nchunk = _sc_split(t)
    ch = SC_CHUNK_ROWS
    mesh = plsc.VectorSubcoreMesh(core_axis_name="c", subcore_axis_name="s")
    dma = pltpu.SemaphoreType.DMA

    @functools.partial(
        pl.kernel, mesh=mesh,
        out_type=jax.ShapeDtypeStruct((cap, d), jnp.float32),
        scratch_types=([pltpu.VMEM((ch,), jnp.int32)] * SC_BUFFERS
                       + [pltpu.VMEM((ch, d), jnp.float32)] * SC_BUFFERS + [dma] * (3 * SC_BUFFERS)),
    )
    def dispatch_sc(x_hbm, dest_hbm, xs_hbm, *scratch):
        n = SC_BUFFERS
        idx, rows, isem, rsem, ssem = (scratch[k * n:(k + 1) * n] for k in range(5))
        wid = lax.axis_index("s") * nc + lax.axis_index("c")
        base = wid * per_w

        def load(c, b):
            src = pl.ds(base + c * ch, ch)
            return (pltpu.make_async_copy(dest_hbm.at[src], idx[b], isem[b]),
                    pltpu.make_async_copy(x_hbm.at[src], rows[b], rsem[b]))

        def store(c, b):
            return (pltpu.make_async_copy(rows[b], xs_hbm.at[idx[b]], ssem[b]),)

        _sc_ring(nchunk, load, store)

    return dispatch_sc(x2, dest)


def _unsort(ys, dest, t):
    d = ys.shape[1]
    nc, per_w, nchunk = _sc_split(t)
    ch = SC_CHUNK_ROWS
    mesh = plsc.VectorSubcoreMesh(core_axis_name="c", subcore_axis_name="s")
    dma = pltpu.SemaphoreType.DMA

    @functools.partial(
        pl.kernel, mesh=mesh,
        out_type=jax.ShapeDtypeStruct((t, d), jnp.float32),
        scratch_types=([pltpu.VMEM((per_w,), jnp.int32)]
                       + [pltpu.VMEM((ch, d), jnp.float32)] * SC_BUFFERS + [dma] * (2 * SC_BUFFERS)),
    )
    def unsort_sc(ys_hbm, dest_hbm, out_hbm, idx_v, *scratch):
        n = SC_BUFFERS
        rows, gsem, wsem = (scratch[k * n:(k + 1) * n] for k in range(3))
        wid = lax.axis_index("s") * nc + lax.axis_index("c")
        base = wid * per_w
        pltpu.sync_copy(dest_hbm.at[pl.ds(base, per_w)], idx_v)

        def load(c, b):
            return (pltpu.make_async_copy(ys_hbm.at[idx_v.at[pl.ds(c * ch, ch)]], rows[b], gsem[b]),)

        def store(c, b):
            return (pltpu.make_async_copy(rows[b], out_hbm.at[pl.ds(base + c * ch, ch)], wsem[b]),)

        _sc_ring(nchunk, load, store)

    return unsort_sc(ys, dest)


def _expert_kernel(grp_ref, ea_ref, eb_ref, nvalid_ref, nact_ref,
                   xs_ref, gf_ref, wr_ref, br_ref, gfin_ref, wg_hbm, wu_hbm, wd_hbm,
                   o_ref, wgb, wub, wdb, sg, su, sd, sem, cnt_ref):
    s = pl.program_id(0)
    blocks = tuple(BLOCKS_PER_STEP * s + i for i in range(BLOCKS_PER_STEP))
    g = grp_ref[blocks[0]]

    def stage(e):
        slot = e % 2
        return (pltpu.make_async_copy(wg_hbm.at[e], sg.at[slot], sem.at[slot, 0]),
                pltpu.make_async_copy(wu_hbm.at[e], su.at[slot], sem.at[slot, 1]),
                pltpu.make_async_copy(wd_hbm.at[e], sd.at[slot], sem.at[slot, 2]))

    def start_next():
        @pl.when(cnt_ref[0] < N_EXPERTS)
        def _():
            for cp in stage(cnt_ref[0]):
                cp.start()
            cnt_ref[0] = cnt_ref[0] + 1

    @pl.when(s == 0)
    def _():
        cnt_ref[0] = 0
        cnt_ref[1] = 0
        start_next()
        start_next()

    active = blocks[0] < nact_ref[0]

    @pl.when(active)
    def _():
        need = g * EXPERTS_PER_GROUP + functools.reduce(
            jnp.maximum, [eb_ref[j] for j in blocks])

        def load(e, carry):
            for cp in stage(e):
                cp.wait()
            slot = e % 2
            k = e % EXPERTS_PER_GROUP
            wgb[k] = sg[slot].astype(jnp.bfloat16)
            wub[k] = su[slot].astype(jnp.bfloat16)
            wdb[k] = sd[slot].astype(jnp.bfloat16)
            cnt_ref[1] = e + 1
            start_next()
            return carry

        lax.fori_loop(cnt_ref[1], need + 1, load, 0)

        lane = lax.broadcasted_iota(jnp.int32, (ROW_BLOCK, ROUTE_LANES), 1)
        is_g = lane < N_GROUPS
        lo = EXPERT_LANE0 + EXPERTS_PER_GROUP * g
        nblk = len(blocks)
        rowid = lax.broadcasted_iota(jnp.int32, (ROW_BLOCK, 1), 0)
        xs = [jnp.where(rowid < nvalid_ref[blocks[i]],
                        xs_ref[i * ROW_BLOCK:(i + 1) * ROW_BLOCK, :], 0.0) for i in range(nblk)]
        h3 = [_rms(x, gf_ref[...]).astype(jnp.bfloat16) for x in xs]
        lg = [jnp.dot(h, wr_ref[...], preferred_element_type=jnp.float32) + br_ref[...]
              for h in h3]

        def gates(lgi, j):
            def pick(idx):
                return jnp.sum(jnp.where(lane == idx, lgi, 0.0), axis=-1, keepdims=True)

            gmax = jnp.max(jnp.where(is_g, lgi, _NEG), axis=-1, keepdims=True)
            den = jnp.sum(jnp.where(is_g, jnp.exp(jnp.where(is_g, lgi, _NEG) - gmax), 0.0),
                          axis=-1, keepdims=True)
            grp_p = jnp.exp(pick(g) - gmax) / den
            la = pick(lo + ea_ref[j])
            lb = pick(lo + eb_ref[j])
            m = jnp.maximum(la, lb)
            pa = jnp.exp(la - m)
            pb = jnp.exp(lb - m)
            return grp_p * pa / (pa + pb), grp_p * pb / (pa + pb)

        gate = [gates(lg[i], blocks[i]) for i in range(nblk)]
        ys = list(xs)
        for which, e_ref in enumerate((ea_ref, eb_ref)):
            ks = [e_ref[j] for j in blocks]
            gg = [jnp.dot(h3[i], wgb[ks[i]], preferred_element_type=jnp.float32)
                  for i in range(nblk)]
            uu = [jnp.dot(h3[i], wub[ks[i]], preferred_element_type=jnp.float32)
                  for i in range(nblk)]
            act = [(gg[i] * (1.0 / (1.0 + jnp.exp(-gg[i]))) * uu[i]
                    * gate[i][which]).astype(jnp.bfloat16) for i in range(nblk)]
            ys = [ys[i] + jnp.dot(act[i], wdb[ks[i]], preferred_element_type=jnp.float32)
                  for i in range(nblk)]
        for i in range(nblk):
            o_ref[i * ROW_BLOCK:(i + 1) * ROW_BLOCK, :] = _rms(ys[i], gfin_ref[...])

    @pl.when(jnp.logical_not(active))
    def _():
        o_ref[...] = jnp.zeros_like(o_ref)

    @pl.when(s == pl.num_programs(0) - 1)
    def _():
        def drain(e, carry):
            for cp in stage(e):
                cp.wait()
            return carry

        lax.fori_loop(cnt_ref[1], cnt_ref[0], drain, 0)


def _experts(xs, blk_grp, blk_a, blk_b, nvalid, nact, g_ffn, w_r, b_r, g_final, w_gate, w_up, w_down):
    cap, d = xs.shape
    de = w_gate.shape[2]
    step_rows = BLOCKS_PER_STEP * ROW_BLOCK
    steps = cap // step_rows
    pre = lambda f: (lambda s, gr, ea, eb, nv, na: f(s, na))
    const2 = pre(lambda s, na: (0, 0))
    last_step = lambda na: (na[0] - 1) // BLOCKS_PER_STEP
    hbm = pl.BlockSpec(memory_space=pl.ANY)
    grid_spec = pltpu.PrefetchScalarGridSpec(
        num_scalar_prefetch=5,
        grid=(steps,),
        in_specs=[
            pl.BlockSpec((step_rows, d), pre(lambda s, na: (jnp.minimum(s, last_step(na)), 0))),
            pl.BlockSpec(g_ffn.shape, const2),
            pl.BlockSpec(w_r.shape, const2),
            pl.BlockSpec(b_r.shape, const2),
            pl.BlockSpec(g_final.shape, const2),
            hbm, hbm, hbm,
        ],
        out_specs=pl.BlockSpec((step_rows, d), pre(lambda s, na: (s, 0))),
        scratch_shapes=[
            pltpu.VMEM((EXPERTS_PER_GROUP, d, de), jnp.bfloat16),
            pltpu.VMEM((EXPERTS_PER_GROUP, d, de), jnp.bfloat16),
            pltpu.VMEM((EXPERTS_PER_GROUP, de, d), jnp.bfloat16),
            pltpu.VMEM((2, d, de), jnp.float32),
            pltpu.VMEM((2, d, de), jnp.float32),
            pltpu.VMEM((2, de, d), jnp.float32),
            pltpu.SemaphoreType.DMA((2, 3)),
            pltpu.SMEM((2,), jnp.int32),
        ],
    )
    return pl.pallas_call(
        _expert_kernel,
        grid_spec=grid_spec,
        out_shape=jax.ShapeDtypeStruct((cap, d), jnp.float32),
        compiler_params=pltpu.CompilerParams(
            dimension_semantics=("arbitrary",), vmem_limit_bytes=VMEM_LIMIT_BYTES),
        name="experts",
    )(blk_grp, blk_a, blk_b, nvalid, nact, xs, g_ffn, w_r, b_r, g_final, w_gate, w_up, w_down)


def _moe_final(x2, slab, counts_col, g_ffn, w_r, b_r, g_final, w_gate, w_up, w_down):
    t, d = x2.shape
    nb = t // ROW_BLOCK + N_CLASSES + N_GROUPS * (BLOCKS_PER_STEP - 1)
    assert nb % BLOCKS_PER_STEP == 0
    counts = counts_col[:N_CLASSES, 0].astype(jnp.int32)
    nblk = (counts + ROW_BLOCK - 1) // ROW_BLOCK
    grp_blocks = jnp.sum(nblk.reshape(N_GROUPS, N_PAIRS), axis=1)
    nblk = nblk.reshape(N_GROUPS, N_PAIRS).at[:, N_PAIRS - 1].add(
        (-grp_blocks) % BLOCKS_PER_STEP).reshape(N_CLASSES)
    blk_end = jnp.cumsum(nblk)
    blk_start = blk_end - nblk
    nact = blk_end[-1]
    j = jnp.arange(nb, dtype=jnp.int32)
    blk_class = jnp.minimum(
        jnp.sum((blk_end[None, :] <= j[:, None]).astype(jnp.int32), axis=1), N_CLASSES - 1)
    blk_class = jnp.where(j < nact, blk_class, blk_class[jnp.maximum(nact - 1, 0)])
    nvalid = jnp.where(j < nact, jnp.clip(
        counts[blk_class] - (j - blk_start[blk_class]) * ROW_BLOCK, 0, ROW_BLOCK), 0)
    pair_a = jnp.array([p[0] for p in PAIRS], jnp.int32)
    pair_b = jnp.array([p[1] for p in PAIRS], jnp.int32)
    blk_grp = blk_class // N_PAIRS
    blk_a = pair_a[blk_class % N_PAIRS]
    blk_b = pair_b[blk_class % N_PAIRS]
    starts_col = jnp.zeros((ROUTE_LANES, 1), jnp.float32).at[:N_CLASSES, 0].set(
        (blk_start * ROW_BLOCK).astype(jnp.float32))

    dest = _dest_rows(slab, starts_col)[0]
    xs = _dispatch(x2, dest, nb * ROW_BLOCK)
    ys = _experts(xs, blk_grp, blk_a, blk_b, nvalid, nact[None], g_ffn, w_r, b_r, g_final,
                  w_gate, w_up, w_down)
    return _unsort(ys, dest, t)


def kernel(x, mem, g_mix, w_in, conv_w, g_v, w_s, b_s, g_out_conv, g_out_gmlp, w_out, g_xattn,
           g_mem, w_q, w_k, w_v, w_o, g_ffn, w_grp, b_grp, w_rt, b_rt, w_gate, w_up, w_down,
           g_final):
    b, s, d = x.shape
    assert g_mix.shape[0] == 1, "the final norm is fused into the single layer's expert kernel"
    assert N_CLASSES <= ROUTE_LANES
    bf = jnp.bfloat16
    kt, v = _kv_proj(mem, g_mem[0][None], w_k[0].astype(bf), w_v[0].astype(bf))

    bias = jnp.repeat(b_s[0].T, GMLP_HEAD_DIM, axis=1)
    x1 = _mixer(x, g_mix[0][None], w_in[0].astype(bf), conv_w[0], g_v[0][None], w_s[0], bias,
                g_out_conv[0][None], g_out_gmlp[0][None], w_out[0].astype(bf))

    pad = ROUTE_LANES - N_GROUPS - N_EXPERTS
    w_r = jnp.concatenate([w_grp[0], w_rt[0], jnp.zeros((d, pad), jnp.float32)], axis=1).astype(bf)
    b_r = jnp.concatenate([b_grp[0], b_rt[0], jnp.zeros((pad,), jnp.float32)])[None]
    gap = ROUTE_EXPERT_ROW0 - N_GROUPS
    tail = ROUTE_LANES - ROUTE_EXPERT_ROW0 - N_EXPERTS
    w_r_t = jnp.concatenate([w_grp[0].T, jnp.zeros((gap, d), jnp.float32), w_rt[0].T,
                             jnp.zeros((tail, d), jnp.float32)], axis=0).astype(bf)
    b_r_t = jnp.concatenate([b_grp[0], jnp.zeros((gap,), jnp.float32), b_rt[0],
                             jnp.zeros((tail,), jnp.float32)])[:, None]
    x2, slab, counts_col = _attn_route(x1, kt, v, g_xattn[0][None], w_q[0].astype(bf),
                                       w_o[0].astype(bf), g_ffn[0][None], w_r_t, b_r_t)
    out = _moe_final(x2.reshape(b * s, d), slab, counts_col, g_ffn[0][None], w_r, b_r,
                     g_final[None], w_gate[0], w_up[0], w_down[0])
    return out.reshape(b, s, d)
```

```python
import functools

import jax
import jax.numpy as jnp
from jax import lax
from jax.experimental import pallas as pl
from jax.experimental.pallas import tpu as pltpu
from jax.experimental.pallas import tpu_sc as plsc

EPS = 1e-6
CONV_GROUP_WIDTH = 512
GMLP_HEADS = 8
GMLP_HEAD_DIM = 64
CHUNK = 128
XA_HEADS = 4
N_GROUPS = 4
EXPERTS_PER_GROUP = 8
N_EXPERTS = N_GROUPS * EXPERTS_PER_GROUP
ROUTE_LANES = 128
EXPERT_LANE0 = N_GROUPS
ROUTE_EXPERT_ROW0 = 8

N_PAIRS = EXPERTS_PER_GROUP * (EXPERTS_PER_GROUP - 1) // 2
N_CLASSES = N_GROUPS * N_PAIRS
PAIRS = [(a, b) for a in range(EXPERTS_PER_GROUP) for b in range(a + 1, EXPERTS_PER_GROUP)]

MIXER_TILE = 1024
ATTN_TILE = 2048
SUB_TILE = 256
DEST_TILE = 8192
ROW_BLOCK = 128
BLOCKS_PER_STEP = 4
SC_CHUNK_ROWS = 16
SC_BUFFERS = 4
VMEM_LIMIT_BYTES = 56 * 1024 * 1024

SUBLANES = 8

_NEG = -1e30


def _rms(x, g):
    return x * lax.rsqrt(jnp.mean(x * x, axis=-1, keepdims=True) + EPS) * g


def _gelu_tanh(x):
    return 0.5 * x * (1.0 + jnp.tanh(0.7978845608028654 * (x + 0.044715 * (x * x * x))))


def _kv_kernel(mem_ref, g_ref, wk_ref, wv_ref, kt_ref, v_ref):
    m = _rms(mem_ref[0], g_ref[...]).astype(jnp.bfloat16)
    k = jnp.dot(m, wk_ref[...], preferred_element_type=jnp.float32)
    v = jnp.dot(m, wv_ref[...], preferred_element_type=jnp.float32)
    kt_ref[0] = k.T.astype(jnp.bfloat16)
    v_ref[0] = v.astype(jnp.bfloat16)


def _kv_proj(mem, g_mem, w_k, w_v):
    b, m, d = mem.shape
    const = lambda i: (0, 0)
    return pl.pallas_call(
        _kv_kernel,
        grid=(b,),
        in_specs=[
            pl.BlockSpec((1, m, d), lambda i: (i, 0, 0)),
            pl.BlockSpec((1, d), const),
            pl.BlockSpec((d, d), const),
            pl.BlockSpec((d, d), const),
        ],
        out_specs=[
            pl.BlockSpec((1, d, m), lambda i: (i, 0, 0)),
            pl.BlockSpec((1, m, d), lambda i: (i, 0, 0)),
        ],
        out_shape=[
            jax.ShapeDtypeStruct((b, d, m), jnp.bfloat16),
            jax.ShapeDtypeStruct((b, m, d), jnp.bfloat16),
        ],
        compiler_params=pltpu.CompilerParams(
            dimension_semantics=("arbitrary",), vmem_limit_bytes=VMEM_LIMIT_BYTES),
        name="kv_proj",
    )(mem, g_mem, w_k, w_v)


def _mixer_kernel(x_ref, gmix_ref, win_ref, convw_ref, gv_ref, ws_ref, bias_ref,
                  goc_ref, gog_ref, wout_ref, o_ref, zbuf_ref):
    ts = x_ref.shape[1]
    w = CONV_GROUP_WIDTH
    hw = 2 * GMLP_HEAD_DIM
    nsub = ts // SUB_TILE

    def dot(a, b):
        return jnp.dot(a, b, preferred_element_type=jnp.float32)

    @pl.when(pl.program_id(1) == 0)
    def _():
        zbuf_ref[0:8, :] = jnp.zeros((8, w), jnp.float32)

    @pl.when(pl.program_id(1) != 0)
    def _():
        zbuf_ref[0:8, :] = zbuf_ref[ts:ts + 8, :]

    low = lax.broadcasted_iota(jnp.int32, (SUB_TILE, hw), 1) < GMLP_HEAD_DIM
    row = lax.broadcasted_iota(jnp.int32, (CHUNK, CHUNK), 0)
    colid = lax.broadcasted_iota(jnp.int32, (CHUNK, CHUNK), 1)
    causal = row >= colid
    ws = [jnp.where(causal, ws_ref[hd], 0.0).astype(jnp.bfloat16) for hd in range(GMLP_HEADS)]
    npair = SUB_TILE // CHUNK // 2
    lo = lax.broadcasted_iota(jnp.int32, (CHUNK, hw), 1) < GMLP_HEAD_DIM
    swap = lambda a: pltpu.roll(a, GMLP_HEAD_DIM, axis=1)

    def project(i):
        r0 = i * SUB_TILE
        xt = x_ref[0, r0:r0 + SUB_TILE, :]
        h = _rms(xt, gmix_ref[...]).astype(jnp.bfloat16)
        p = [dot(h, win_ref[:, k * w:(k + 1) * w]) for k in range(5)]
        z = p[1] * p[2]
        zbuf_ref[8 + r0:8 + r0 + SUB_TILE, :] = z
        return dict(xt=xt, gate_b=p[0], z=z, u=p[3], v=p[4], r0=r0)

    def gate_and_norm(st):
        r0 = st["r0"]
        zc = (convw_ref[0:1, :] * zbuf_ref[6 + r0:6 + r0 + SUB_TILE, :]
              + convw_ref[1:2, :] * zbuf_ref[7 + r0:7 + r0 + SUB_TILE, :]
              + convw_ref[2:3, :] * st["z"])
        st["ya"] = _rms(st["gate_b"] * zc, goc_ref[...]).astype(jnp.bfloat16)
        st["u"] = _gelu_tanh(st["u"])
        v = _gelu_tanh(st["v"])
        v2 = v * v
        ss_cols = []
        for k in range(GMLP_HEADS // 2):
            col = v2[:, hw * k:hw * (k + 1)]
            ss_cols.append(jnp.where(low,
                                     jnp.sum(jnp.where(low, col, 0.0), axis=1, keepdims=True),
                                     jnp.sum(jnp.where(low, 0.0, col), axis=1, keepdims=True)))
        ss = jnp.concatenate(ss_cols, axis=1)
        st["vn"] = v * lax.rsqrt(ss * (1.0 / GMLP_HEAD_DIM) + EPS) * gv_ref[...]

    def mix_positions(st):
        vn = st["vn"]
        s_cols = [[None] * (GMLP_HEADS // 2) for _ in range(2 * npair)]
        for hp in range(GMLP_HEADS // 2):
            cols = [vn[c * CHUNK:(c + 1) * CHUNK, hw * hp:hw * (hp + 1)] for c in range(2 * npair)]
            swapped = [swap(a) for a in cols]
            rhs_even = jnp.concatenate(
                [jnp.where(lo, cols[2 * p], swapped[2 * p + 1]) for p in range(npair)], axis=1)
            rhs_odd = jnp.concatenate(
                [jnp.where(lo, swapped[2 * p], cols[2 * p + 1]) for p in range(npair)], axis=1)
            out_e = dot(ws[2 * hp], rhs_even.astype(jnp.bfloat16))
            out_o = dot(ws[2 * hp + 1], rhs_odd.astype(jnp.bfloat16))
            for p in range(npair):
                e = out_e[:, hw * p:hw * (p + 1)]
                o = out_o[:, hw * p:hw * (p + 1)]
                s_cols[2 * p][hp] = jnp.where(lo, e, swap(o))
                s_cols[2 * p + 1][hp] = jnp.where(lo, swap(e), o)
        st["s"] = jnp.concatenate(
            [jnp.concatenate(c, axis=1) + bias_ref[...] for c in s_cols], axis=0)

    def output(st):
        r0 = st["r0"]
        yb = _rms(st["u"] * st["s"], gog_ref[...]).astype(jnp.bfloat16)
        o_ref[0, r0:r0 + SUB_TILE, :] = (st["xt"] + dot(st["ya"], wout_ref[0:w, :])
                                          + dot(yb, wout_ref[w:2 * w, :]))

    phases = (gate_and_norm, mix_positions, output)
    states = []
    for step in range(nsub + len(phases)):
        if step < nsub:
            states.append(project(step))
        for k, phase in enumerate(phases):
            i = step - 1 - k
            if 0 <= i < nsub:
                phase(states[i])


def _mixer(x, g_mix, w_in, conv_w, g_v, w_s, bias, g_oc, g_og, w_out):
    b, s, d = x.shape
    ts = min(MIXER_TILE, s)
    const2 = lambda i, j: (0, 0)
    const3 = lambda i, j: (0, 0, 0)
    return pl.pallas_call(
        _mixer_kernel,
        grid=(b, s // ts),
        in_specs=[
            pl.BlockSpec((1, ts, d), lambda i, j: (i, j, 0)),
            pl.BlockSpec(g_mix.shape, const2),
            pl.BlockSpec(w_in.shape, const2),
            pl.BlockSpec(conv_w.shape, const2),
            pl.BlockSpec(g_v.shape, const2),
            pl.BlockSpec(w_s.shape, const3),
            pl.BlockSpec(bias.shape, const2),
            pl.BlockSpec(g_oc.shape, const2),
            pl.BlockSpec(g_og.shape, const2),
            pl.BlockSpec(w_out.shape, const2),
        ],
        out_specs=pl.BlockSpec((1, ts, d), lambda i, j: (i, j, 0)),
        out_shape=jax.ShapeDtypeStruct((b, s, d), jnp.float32),
        scratch_shapes=[pltpu.VMEM((ts + 8, CONV_GROUP_WIDTH), jnp.float32)],
        compiler_params=pltpu.CompilerParams(
            dimension_semantics=("arbitrary", "arbitrary"), vmem_limit_bytes=VMEM_LIMIT_BYTES),
        name="mixer",
    )(x, g_mix, w_in, conv_w, g_v, w_s, bias, g_oc, g_og, w_out)


def _attn_route_kernel(x_ref, kt_ref, v_ref, gx_ref, wq_ref, wo_ref, gf_ref, wrt_ref, brt_ref,
                       x2_ref, slab_ref, counts_ref, carry_ref):
    ts = x_ref.shape[1]
    d = x_ref.shape[2]
    hd = d // XA_HEADS
    nsub = ts // SUB_TILE
    first = (pl.program_id(0) == 0) & (pl.program_id(1) == 0)

    @pl.when(first)
    def _():
        carry_ref[...] = jnp.zeros_like(carry_ref)

    def dot(a, b):
        return jnp.dot(a, b, preferred_element_type=jnp.float32)

    subs = range(nsub)
    x1 = [x_ref[0, i * SUB_TILE:(i + 1) * SUB_TILE, :] for i in subs]
    h2 = [_rms(x, gx_ref[...]).astype(jnp.bfloat16) for x in x1]
    q = [dot(h, wq_ref[...]).astype(jnp.bfloat16) for h in h2]
    heads = [[] for _ in subs]
    for a in range(XA_HEADS):
        sc = [dot(q[i][:, a * hd:(a + 1) * hd], kt_ref[0, a * hd:(a + 1) * hd, :]) * (hd ** -0.5)
              for i in subs]
        p = [jnp.exp(s_ - jnp.max(s_, axis=-1, keepdims=True)) for s_ in sc]
        l = [jnp.sum(p_, axis=-1, keepdims=True) for p_ in p]
        o = [dot(p[i].astype(jnp.bfloat16), v_ref[0, :, a * hd:(a + 1) * hd]) for i in subs]
        for i in subs:
            heads[i].append((o[i] / l[i]).astype(jnp.bfloat16))
    x2 = [x1[i] + dot(jnp.concatenate(heads[i], axis=1), wo_ref[...]) for i in subs]
    for i in subs:
        x2_ref[0, i * SUB_TILE:(i + 1) * SUB_TILE, :] = x2[i]

    h3 = [_rms(x, gf_ref[...]).astype(jnp.bfloat16) for x in x2]
    lgs = [lax.dot_general(wrt_ref[...], h, (((1,), (1,)), ((), ())),
                           preferred_element_type=jnp.float32) + brt_ref[...] for h in h3]
    sub = lax.broadcasted_iota(jnp.int32, (EXPERTS_PER_GROUP, SUB_TILE), 0).astype(jnp.float32)
    big = float(EXPERTS_PER_GROUP)
    is_g = sub < N_GROUPS

    def classify(lg):
        glog = lg[0:EXPERTS_PER_GROUP, :]
        gmax = jnp.max(jnp.where(is_g, glog, _NEG), axis=0, keepdims=True)
        gidx = jnp.min(jnp.where(is_g & (glog == gmax), sub, big), axis=0, keepdims=True)
        el = lg[ROUTE_EXPERT_ROW0:ROUTE_EXPERT_ROW0 + EXPERTS_PER_GROUP, :]
        for grp in range(1, N_GROUPS):
            r0 = ROUTE_EXPERT_ROW0 + grp * EXPERTS_PER_GROUP
            el = jnp.where(gidx == grp, lg[r0:r0 + EXPERTS_PER_GROUP, :], el)
        t1 = jnp.max(el, axis=0, keepdims=True)
        i1 = jnp.min(jnp.where(el == t1, sub, big), axis=0, keepdims=True)
        rest = sub != i1
        t2 = jnp.max(jnp.where(rest, el, _NEG), axis=0, keepdims=True)
        i2 = jnp.min(jnp.where(rest & (el == t2), sub, big), axis=0, keepdims=True)
        a = jnp.minimum(i1, i2)
        b = jnp.maximum(i1, i2)
        pair = a * (2 * EXPERTS_PER_GROUP - 1 - a) * 0.5 + (b - a - 1.0)
        return gidx * N_PAIRS + pair

    cls = [classify(lg) for lg in lgs]

    r = lax.broadcasted_iota(jnp.int32, (SUB_TILE, SUB_TILE), 0)
    c = lax.broadcasted_iota(jnp.int32, (SUB_TILE, SUB_TILE), 1)
    earlier = jnp.where(r < c, 1.0, 0.0).astype(jnp.bfloat16)
    crow = lax.broadcasted_iota(jnp.int32, (ROUTE_LANES, SUB_TILE), 0).astype(jnp.float32)
    oh = [crow == cl for cl in cls]
    ohf = [jnp.where(o_, 1.0, 0.0) for o_ in oh]
    inside = [dot(o_.astype(jnp.bfloat16), earlier) for o_ in ohf]
    seen = carry_ref[...]
    for i in subs:
        rank = jnp.sum(jnp.where(oh[i], inside[i] + seen, 0.0), axis=0, keepdims=True)
        slab_ref[:, i * SUB_TILE:(i + 1) * SUB_TILE] = jnp.where(
            sub == 0, cls[i], jnp.where(sub == 1, rank, 0.0))
        seen = seen + jnp.sum(ohf[i], axis=1, keepdims=True)
    carry_ref[...] = seen
    counts_ref[...] = seen


def _attn_route(x1, kt, v, g_x, w_q, w_o, g_f, w_rt, b_rt):
    b, s, d = x1.shape
    m = v.shape[1]
    ts = min(ATTN_TILE, s)
    nt = s // ts
    const2 = lambda i, j: (0, 0)
    return pl.pallas_call(
        _attn_route_kernel,
        grid=(b, nt),
        in_specs=[
            pl.BlockSpec((1, ts, d), lambda i, j: (i, j, 0)),
            pl.BlockSpec((1, d, m), lambda i, j: (i, 0, 0)),
            pl.BlockSpec((1, m, d), lambda i, j: (i, 0, 0)),
            pl.BlockSpec(g_x.shape, const2),
            pl.BlockSpec(w_q.shape, const2),
            pl.BlockSpec(w_o.shape, const2),
            pl.BlockSpec(g_f.shape, const2),
            pl.BlockSpec(w_rt.shape, const2),
            pl.BlockSpec(b_rt.shape, const2),
        ],
        out_specs=[
            pl.BlockSpec((1, ts, d), lambda i, j: (i, j, 0)),
            pl.BlockSpec((SUBLANES, ts), lambda i, j: (0, i * nt + j)),
            pl.BlockSpec((ROUTE_LANES, 1), const2),
        ],
        out_shape=[
            jax.ShapeDtypeStruct((b, s, d), jnp.float32),
            jax.ShapeDtypeStruct((SUBLANES, b * s), jnp.float32),
            jax.ShapeDtypeStruct((ROUTE_LANES, 1), jnp.float32),
        ],
        scratch_shapes=[pltpu.VMEM((ROUTE_LANES, 1), jnp.float32)],
        compiler_params=pltpu.CompilerParams(
            dimension_semantics=("arbitrary", "arbitrary"), vmem_limit_bytes=VMEM_LIMIT_BYTES),
        name="attn_route",
    )(x1, kt, v, g_x, w_q, w_o, g_f, w_rt, b_rt)


def _dest_kernel(slab_ref, starts_ref, o_ref):
    slab = slab_ref[...]
    tt = slab.shape[1]
    crow = lax.broadcasted_iota(jnp.int32, (ROUTE_LANES, tt), 0).astype(jnp.float32)
    start = jnp.sum(jnp.where(crow == slab[0:1, :], starts_ref[...], 0.0), axis=0, keepdims=True)
    sub = lax.broadcasted_iota(jnp.int32, slab.shape, 0)
    o_ref[...] = jnp.where(sub == 0, start + slab[1:2, :], 0.0).astype(jnp.int32)


def _dest_rows(slab, starts_col):
    t = slab.shape[1]
    tt = min(DEST_TILE, t)
    return pl.pallas_call(
        _dest_kernel,
        grid=(t // tt,),
        in_specs=[pl.BlockSpec((SUBLANES, tt), lambda i: (0, i)),
                  pl.BlockSpec((ROUTE_LANES, 1), lambda i: (0, 0))],
        out_specs=pl.BlockSpec((SUBLANES, tt), lambda i: (0, i)),
        out_shape=jax.ShapeDtypeStruct((SUBLANES, t), jnp.int32),
        compiler_params=pltpu.CompilerParams(dimension_semantics=("arbitrary",)),
        name="dest_rows",
    )(slab, starts_col)


def _sc_split(t):
    info = plsc.get_sparse_core_info()
    workers = info.num_cores * info.num_subcores
    per_w, rem = divmod(t, workers)
    nchunk, rem2 = divmod(per_w, SC_CHUNK_ROWS)
    assert rem == 0 and rem2 == 0 and nchunk % SC_BUFFERS == 0, (t, workers, SC_CHUNK_ROWS)
    return info.num_cores, per_w, nchunk


def _sc_ring(nchunk, load, store):
    ahead = SC_BUFFERS - 1
    for k in range(ahead):
        for cp in load(k, k):
            cp.start()

    @pl.loop(0, nchunk, step=SC_BUFFERS)
    def _(c):
        for b in range(SC_BUFFERS):
            cc = c + b
            refill = (b + ahead) % SC_BUFFERS

            @pl.when(cc >= 1)
            def _():
                for cp in store(cc - 1, refill):
                    cp.wait()

            @pl.when(cc + ahead < nchunk)
            def _():
                for cp in load(cc + ahead, refill):
                    cp.start()

            for cp in load(cc, b):
                cp.wait()
            for cp in store(cc, b):
                cp.start()

    for cp in store(nchunk - 1, (nchunk - 1) % SC_BUFFERS):
        cp.wait()


def _dispatch(x2, dest, cap):
    t, d = x2.shape
    nc, per_w, nchunk = _sc_split(t)
    ch = SC_CHUNK_ROWS
    mesh = plsc.VectorSubcoreMesh(core_axis_name="c", subcore_axis_name="s")
    dma = pltpu.SemaphoreType.DMA

    @functools.partial(
        pl.kernel, mesh=mesh,
        out_type=jax.ShapeDtypeStruct((cap, d), jnp.float32),
        scratch_types=([pltpu.VMEM((ch,), jnp.int32)] * SC_BUFFERS
                       + [pltpu.VMEM((ch, d), jnp.float32)] * SC_BUFFERS + [dma] * (3 * SC_BUFFERS)),
    )
    def dispatch_sc(x_hbm, dest_hbm, xs_hbm, *scratch):
        n = SC_BUFFERS
        idx, rows, isem, rsem, ssem = (scratch[k * n:(k + 1) * n] for k in range(5))
        wid = lax.axis_index("s") * nc + lax.axis_index("c")
        base = wid * per_w

        def load(c, b):
            src = pl.ds(base + c * ch, ch)
            return (pltpu.make_async_copy(dest_hbm.at[src], idx[b], isem[b]),
                    pltpu.make_async_copy(x_hbm.at[src], rows[b], rsem[b]))

        def store(c, b):
            return (pltpu.make_async_copy(rows[b], xs_hbm.at[idx[b]], ssem[b]),)

        _sc_ring(nchunk, load, store)

    return dispatch_sc(x2, dest)


def _unsort(ys, dest, t):
    d = ys.shape[1]
    nc, per_w, nchunk = _sc_split(t)
    ch = SC_CHUNK_ROWS
    mesh = plsc.VectorSubcoreMesh(core_axis_name="c", subcore_axis_name="s")
    dma = pltpu.SemaphoreType.DMA

    @functools.partial(
        pl.kernel, mesh=mesh,
        out_type=jax.ShapeDtypeStruct((t, d), jnp.float32),
        scratch_types=([pltpu.VMEM((per_w,), jnp.int32)]
                       + [pltpu.VMEM((ch, d), jnp.float32)] * SC_BUFFERS + [dma] * (2 * SC_BUFFERS)),
    )
    def unsort_sc(ys_hbm, dest_hbm, out_hbm, idx_v, *scratch):
        n = SC_BUFFERS
        rows, gsem, wsem = (scratch[k * n:(k + 1) * n] for k in range(3))
        wid = lax.axis_index("s") * nc + lax.axis_index("c")
        base = wid * per_w
        pltpu.sync_copy(dest_hbm.at[pl.ds(base, per_w)], idx_v)

        def load(c, b):
            return (pltpu.make_async_copy(ys_hbm.at[idx_v.at[pl.ds(c * ch, ch)]], rows[b], gsem[b]),)

        def store(c, b):
            return (pltpu.make_async_copy(rows[b], out_hbm.at[pl.ds(base + c * ch, ch)], wsem[b]),)

        _sc_ring(nchunk, load, store)

    return unsort_sc(ys, dest)


def _expert_kernel(grp_ref, ea_ref, eb_ref, nvalid_ref, nact_ref,
                   xs_ref, gf_ref, wr_ref, br_ref, gfin_ref, wg_hbm, wu_hbm, wd_hbm,
                   o_ref, wgb, wub, wdb, sg, su, sd, sem, cnt_ref):
    s = pl.program_id(0)
    blocks = tuple(BLOCKS_PER_STEP * s + i for i in range(BLOCKS_PER_STEP))
    g = grp_ref[blocks[0]]

    def stage(e):
        slot = e % 2
        return (pltpu.make_async_copy(wg_hbm.at[e], sg.at[slot], sem.at[slot, 0]),
                pltpu.make_async_copy(wu_hbm.at[e], su.at[slot], sem.at[slot, 1]),
                pltpu.make_async_copy(wd_hbm.at[e], sd.at[slot], sem.at[slot, 2]))

    def start_next():
        @pl.when(cnt_ref[0] < N_EXPERTS)
        def _():
            for cp in stage(cnt_ref[0]):
                cp.start()
            cnt_ref[0] = cnt_ref[0] + 1

    @pl.when(s == 0)
    def _():
        cnt_ref[0] = 0
        cnt_ref[1] = 0
        start_next()
        start_next()

    active = blocks[0] < nact_ref[0]

    @pl.when(active)
    def _():
        need = g * EXPERTS_PER_GROUP + functools.reduce(
            jnp.maximum, [eb_ref[j] for j in blocks])

        def load(e, carry):
            for cp in stage(e):
                cp.wait()
            slot = e % 2
            k = e % EXPERTS_PER_GROUP
            wgb[k] = sg[slot].astype(jnp.bfloat16)
            wub[k] = su[slot].astype(jnp.bfloat16)
            wdb[k] = sd[slot].astype(jnp.bfloat16)
            cnt_ref[1] = e + 1
            start_next()
            return carry

        lax.fori_loop(cnt_ref[1], need + 1, load, 0)

        lane = lax.broadcasted_iota(jnp.int32, (ROW_BLOCK, ROUTE_LANES), 1)
        is_g = lane < N_GROUPS
        lo = EXPERT_LANE0 + EXPERTS_PER_GROUP * g
        nblk = len(blocks)
        rowid = lax.broadcasted_iota(jnp.int32, (ROW_BLOCK, 1), 0)
        xs = [jnp.where(rowid < nvalid_ref[blocks[i]],
                        xs_ref[i * ROW_BLOCK:(i + 1) * ROW_BLOCK, :], 0.0) for i in range(nblk)]
        h3 = [_rms(x, gf_ref[...]).astype(jnp.bfloat16) for x in xs]
        lg = [jnp.dot(h, wr_ref[...], preferred_element_type=jnp.float32) + br_ref[...]
              for h in h3]

        def gates(lgi, j):
            def pick(idx):
                return jnp.sum(jnp.where(lane == idx, lgi, 0.0), axis=-1, keepdims=True)

            gmax = jnp.max(jnp.where(is_g, lgi, _NEG), axis=-1, keepdims=True)
            den = jnp.sum(jnp.where(is_g, jnp.exp(jnp.where(is_g, lgi, _NEG) - gmax), 0.0),
                          axis=-1, keepdims=True)
            grp_p = jnp.exp(pick(g) - gmax) / den
            la = pick(lo + ea_ref[j])
            lb = pick(lo + eb_ref[j])
            m = jnp.maximum(la, lb)
            pa = jnp.exp(la - m)
            pb = jnp.exp(lb - m)
            return grp_p * pa / (pa + pb), grp_p * pb / (pa + pb)

        gate = [gates(lg[i], blocks[i]) for i in range(nblk)]
        ys = list(xs)
        for which, e_ref in enumerate((ea_ref, eb_ref)):
            ks = [e_ref[j] for j in blocks]
            gg = [jnp.dot(h3[i], wgb[ks[i]], preferred_element_type=jnp.float32)
                  for i in range(nblk)]
            uu = [jnp.dot(h3[i], wub[ks[i]], preferred_element_type=jnp.float32)
                  for i in range(nblk)]
            act = [(gg[i] * (1.0 / (1.0 + jnp.exp(-gg[i]))) * uu[i]
                    * gate[i][which]).astype(jnp.bfloat16) for i in range(nblk)]
            ys = [ys[i] + jnp.dot(act[i], wdb[ks[i]], preferred_element_type=jnp.float32)
                  for i in range(nblk)]
        for i in range(nblk):
            o_ref[i * ROW_BLOCK:(i + 1) * ROW_BLOCK, :] = _rms(ys[i], gfin_ref[...])

    @pl.when(s == pl.num_programs(0) - 1)
    def _():
        def drain(e, carry):
            for cp in stage(e):
                cp.wait()
            return carry

        lax.fori_loop(cnt_ref[1], cnt_ref[0], drain, 0)


def _experts(xs, blk_grp, blk_a, blk_b, nvalid, nact, g_ffn, w_r, b_r, g_final, w_gate, w_up, w_down):
    cap, d = xs.shape
    de = w_gate.shape[2]
    step_rows = BLOCKS_PER_STEP * ROW_BLOCK
    steps = cap // step_rows
    pre = lambda f: (lambda s, gr, ea, eb, nv, na: f(s, na))
    const2 = pre(lambda s, na: (0, 0))
    last_step = lambda na: (na[0] - 1) // BLOCKS_PER_STEP
    hbm = pl.BlockSpec(memory_space=pl.ANY)
    active_step = pre(lambda s, na: (jnp.minimum(s, last_step(na)), 0))
    grid_spec = pltpu.PrefetchScalarGridSpec(
        num_scalar_prefetch=5,
        grid=(steps,),
        in_specs=[
            pl.BlockSpec((step_rows, d), active_step),
            pl.BlockSpec(g_ffn.shape, const2),
            pl.BlockSpec(w_r.shape, const2),
            pl.BlockSpec(b_r.shape, const2),
            pl.BlockSpec(g_final.shape, const2),
            hbm, hbm, hbm,
        ],
        out_specs=pl.BlockSpec((step_rows, d), active_step),
        scratch_shapes=[
            pltpu.VMEM((EXPERTS_PER_GROUP, d, de), jnp.bfloat16),
            pltpu.VMEM((EXPERTS_PER_GROUP, d, de), jnp.bfloat16),
            pltpu.VMEM((EXPERTS_PER_GROUP, de, d), jnp.bfloat16),
            pltpu.VMEM((2, d, de), jnp.float32),
            pltpu.VMEM((2, d, de), jnp.float32),
            pltpu.VMEM((2, de, d), jnp.float32),
            pltpu.SemaphoreType.DMA((2, 3)),
            pltpu.SMEM((2,), jnp.int32),
        ],
    )
    return pl.pallas_call(
        _expert_kernel,
        grid_spec=grid_spec,
        out_shape=jax.ShapeDtypeStruct((cap, d), jnp.float32),
        compiler_params=pltpu.CompilerParams(
            dimension_semantics=("arbitrary",), vmem_limit_bytes=VMEM_LIMIT_BYTES),
        name="experts",
    )(blk_grp, blk_a, blk_b, nvalid, nact, xs, g_ffn, w_r, b_r, g_final, w_gate, w_up, w_down)


def _moe_final(x2, slab, counts_col, g_ffn, w_r, b_r, g_final, w_gate, w_up, w_down):
    t, d = x2.shape
    nb = t // ROW_BLOCK + N_CLASSES + N_GROUPS * (BLOCKS_PER_STEP - 1)
    assert nb % BLOCKS_PER_STEP == 0
    counts = counts_col[:N_CLASSES, 0].astype(jnp.int32)
    nblk = (counts + ROW_BLOCK - 1) // ROW_BLOCK
    grp_blocks = jnp.sum(nblk.reshape(N_GROUPS, N_PAIRS), axis=1)
    nblk = nblk.reshape(N_GROUPS, N_PAIRS).at[:, N_PAIRS - 1].add(
        (-grp_blocks) % BLOCKS_PER_STEP).reshape(N_CLASSES)
    blk_end = jnp.cumsum(nblk)
    blk_start = blk_end - nblk
    nact = blk_end[-1]
    j = jnp.arange(nb, dtype=jnp.int32)
    blk_class = jnp.minimum(
        jnp.sum((blk_end[None, :] <= j[:, None]).astype(jnp.int32), axis=1), N_CLASSES - 1)
    blk_class = jnp.where(j < nact, blk_class, blk_class[jnp.maximum(nact - 1, 0)])
    nvalid = jnp.where(j < nact, jnp.clip(
        counts[blk_class] - (j - blk_start[blk_class]) * ROW_BLOCK, 0, ROW_BLOCK), 0)
    pair_a = jnp.array([p[0] for p in PAIRS], jnp.int32)
    pair_b = jnp.array([p[1] for p in PAIRS], jnp.int32)
    blk_grp = blk_class // N_PAIRS
    blk_a = pair_a[blk_class % N_PAIRS]
    blk_b = pair_b[blk_class % N_PAIRS]
    starts_col = jnp.zeros((ROUTE_LANES, 1), jnp.float32).at[:N_CLASSES, 0].set(
        (blk_start * ROW_BLOCK).astype(jnp.float32))

    dest = _dest_rows(slab, starts_col)[0]
    xs = _dispatch(x2, dest, nb * ROW_BLOCK)
    ys = _experts(xs, blk_grp, blk_a, blk_b, nvalid, nact[None], g_ffn, w_r, b_r, g_final,
                  w_gate, w_up, w_down)
    return _unsort(ys, dest, t)


def kernel(x, mem, g_mix, w_in, conv_w, g_v, w_s, b_s, g_out_conv, g_out_gmlp, w_out, g_xattn,
           g_mem, w_q, w_k, w_v, w_o, g_ffn, w_grp, b_grp, w_rt, b_rt, w_gate, w_up, w_down,
           g_final):
    b, s, d = x.shape
    assert g_mix.shape[0] == 1, "the final norm is fused into the single layer's expert kernel"
    assert N_CLASSES <= ROUTE_LANES
    bf = jnp.bfloat16
    kt, v = _kv_proj(mem, g_mem[0][None], w_k[0].astype(bf), w_v[0].astype(bf))

    bias = jnp.repeat(b_s[0].T, GMLP_HEAD_DIM, axis=1)
    x1 = _mixer(x, g_mix[0][None], w_in[0].astype(bf), conv_w[0], g_v[0][None], w_s[0], bias,
                g_out_conv[0][None], g_out_gmlp[0][None], w_out[0].astype(bf))

    pad = ROUTE_LANES - N_GROUPS - N_EXPERTS
    w_r = jnp.concatenate([w_grp[0], w_rt[0], jnp.zeros((d, pad), jnp.float32)], axis=1).astype(bf)
    b_r = jnp.concatenate([b_grp[0], b_rt[0], jnp.zeros((pad,), jnp.float32)])[None]
    gap = ROUTE_EXPERT_ROW0 - N_GROUPS
    tail = ROUTE_LANES - ROUTE_EXPERT_ROW0 - N_EXPERTS
    w_r_t = jnp.concatenate([w_grp[0].T, jnp.zeros((gap, d), jnp.float32), w_rt[0].T,
                             jnp.zeros((tail, d), jnp.float32)], axis=0).astype(bf)
    b_r_t = jnp.concatenate([b_grp[0], jnp.zeros((gap,), jnp.float32), b_rt[0],
                             jnp.zeros((tail,), jnp.float32)])[:, None]
    x2, slab, counts_col = _attn_route(x1, kt, v, g_xattn[0][None], w_q[0].astype(bf),
                                       w_o[0].astype(bf), g_ffn[0][None], w_r_t, b_r_t)
    out = _moe_final(x2.reshape(b * s, d), slab, counts_col, g_ffn[0][None], w_r, b_r,
                     g_final[None], w_gate[0], w_up[0], w_down[0])
    return out.reshape(b, s, d)
```

```python
import functools

import jax
import jax.numpy as jnp
from jax import lax
from jax.experimental import pallas as pl
from jax.experimental.pallas import tpu as pltpu
from jax.experimental.pallas import tpu_sc as plsc

EPS = 1e-6
CONV_GROUP_WIDTH = 512
GMLP_HEADS = 8
GMLP_HEAD_DIM = 64
CHUNK = 128
XA_HEADS = 4
N_GROUPS = 4
EXPERTS_PER_GROUP = 8
N_EXPERTS = N_GROUPS * EXPERTS_PER_GROUP
ROUTE_LANES = 128
EXPERT_LANE0 = N_GROUPS
ROUTE_EXPERT_ROW0 = 8

N_PAIRS = EXPERTS_PER_GROUP * (EXPERTS_PER_GROUP - 1) // 2
N_CLASSES = N_GROUPS * N_PAIRS
PAIRS = [(a, b) for a in range(EXPERTS_PER_GROUP) for b in range(a + 1, EXPERTS_PER_GROUP)]

MIXER_TILE = 1024
ATTN_TILE = 2048
SUB_TILE = 256
DEST_TILE = 8192
ROW_BLOCK = 128
BLOCKS_PER_STEP = 4
WEIGHT_STAGES = 3
SC_CHUNK_ROWS = 16
SC_BUFFERS = 4
VMEM_LIMIT_BYTES = 56 * 1024 * 1024
EXPERT_VMEM_LIMIT_BYTES = 60 * 1024 * 1024

SUBLANES = 8

_NEG = -1e30


def _rms(x, g):
    return x * lax.rsqrt(jnp.mean(x * x, axis=-1, keepdims=True) + EPS) * g


def _gelu_tanh(x):
    return 0.5 * x * (1.0 + jnp.tanh(0.7978845608028654 * (x + 0.044715 * (x * x * x))))


def _kv_kernel(mem_ref, g_ref, wk_ref, wv_ref, kt_ref, v_ref):
    m = _rms(mem_ref[0], g_ref[...]).astype(jnp.bfloat16)
    k = jnp.dot(m, wk_ref[...], preferred_element_type=jnp.float32)
    v = jnp.dot(m, wv_ref[...], preferred_element_type=jnp.float32)
    kt_ref[0] = k.T.astype(jnp.bfloat16)
    v_ref[0] = v.astype(jnp.bfloat16)


def _kv_proj(mem, g_mem, w_k, w_v):
    b, m, d = mem.shape
    const = lambda i: (0, 0)
    return pl.pallas_call(
        _kv_kernel,
        grid=(b,),
        in_specs=[
            pl.BlockSpec((1, m, d), lambda i: (i, 0, 0)),
            pl.BlockSpec((1, d), const),
            pl.BlockSpec((d, d), const),
            pl.BlockSpec((d, d), const),
        ],
        out_specs=[
            pl.BlockSpec((1, d, m), lambda i: (i, 0, 0)),
            pl.BlockSpec((1, m, d), lambda i: (i, 0, 0)),
        ],
        out_shape=[
            jax.ShapeDtypeStruct((b, d, m), jnp.bfloat16),
            jax.ShapeDtypeStruct((b, m, d), jnp.bfloat16),
        ],
        compiler_params=pltpu.CompilerParams(
            dimension_semantics=("arbitrary",), vmem_limit_bytes=VMEM_LIMIT_BYTES),
        name="kv_proj",
    )(mem, g_mem, w_k, w_v)


def _mixer_kernel(x_ref, gmix_ref, win_ref, convw_ref, gv_ref, ws_ref, bias_ref,
                  goc_ref, gog_ref, wout_ref, o_ref, zbuf_ref):
    ts = x_ref.shape[1]
    w = CONV_GROUP_WIDTH
    hw = 2 * GMLP_HEAD_DIM
    nsub = ts // SUB_TILE

    def dot(a, b):
        return jnp.dot(a, b, preferred_element_type=jnp.float32)

    @pl.when(pl.program_id(1) == 0)
    def _():
        zbuf_ref[0:8, :] = jnp.zeros((8, w), jnp.float32)

    @pl.when(pl.program_id(1) != 0)
    def _():
        zbuf_ref[0:8, :] = zbuf_ref[ts:ts + 8, :]

    low = lax.broadcasted_iota(jnp.int32, (SUB_TILE, hw), 1) < GMLP_HEAD_DIM
    row = lax.broadcasted_iota(jnp.int32, (CHUNK, CHUNK), 0)
    colid = lax.broadcasted_iota(jnp.int32, (CHUNK, CHUNK), 1)
    causal = row >= colid
    ws = [jnp.where(causal, ws_ref[hd], 0.0).astype(jnp.bfloat16) for hd in range(GMLP_HEADS)]
    npair = SUB_TILE // CHUNK // 2
    lo = lax.broadcasted_iota(jnp.int32, (CHUNK, hw), 1) < GMLP_HEAD_DIM
    swap = lambda a: pltpu.roll(a, GMLP_HEAD_DIM, axis=1)

    def project(i):
        r0 = i * SUB_TILE
        xt = x_ref[0, r0:r0 + SUB_TILE, :]
        h = _rms(xt, gmix_ref[...]).astype(jnp.bfloat16)
        p = [dot(h, win_ref[:, k * w:(k + 1) * w]) for k in range(5)]
        z = p[1] * p[2]
        zbuf_ref[8 + r0:8 + r0 + SUB_TILE, :] = z
        return dict(xt=xt, gate_b=p[0], z=z, u=p[3], v=p[4], r0=r0)

    def gate_and_norm(st):
        r0 = st["r0"]
        zc = (convw_ref[0:1, :] * zbuf_ref[6 + r0:6 + r0 + SUB_TILE, :]
              + convw_ref[1:2, :] * zbuf_ref[7 + r0:7 + r0 + SUB_TILE, :]
              + convw_ref[2:3, :] * st["z"])
        st["ya"] = _rms(st["gate_b"] * zc, goc_ref[...]).astype(jnp.bfloat16)
        st["u"] = _gelu_tanh(st["u"])
        v = _gelu_tanh(st["v"])
        v2 = v * v
        ss_cols = []
        for k in range(GMLP_HEADS // 2):
            col = v2[:, hw * k:hw * (k + 1)]
            ss_cols.append(jnp.where(low,
                                     jnp.sum(jnp.where(low, col, 0.0), axis=1, keepdims=True),
                                     jnp.sum(jnp.where(low, 0.0, col), axis=1, keepdims=True)))
        ss = jnp.concatenate(ss_cols, axis=1)
        st["vn"] = v * lax.rsqrt(ss * (1.0 / GMLP_HEAD_DIM) + EPS) * gv_ref[...]

    def mix_positions(st):
        vn = st["vn"]
        s_cols = [[None] * (GMLP_HEADS // 2) for _ in range(2 * npair)]
        for hp in range(GMLP_HEADS // 2):
            cols = [vn[c * CHUNK:(c + 1) * CHUNK, hw * hp:hw * (hp + 1)] for c in range(2 * npair)]
            swapped = [swap(a) for a in cols]
            rhs_even = jnp.concatenate(
                [jnp.where(lo, cols[2 * p], swapped[2 * p + 1]) for p in range(npair)], axis=1)
            rhs_odd = jnp.concatenate(
                [jnp.where(lo, swapped[2 * p], cols[2 * p + 1]) for p in range(npair)], axis=1)
            out_e = dot(ws[2 * hp], rhs_even.astype(jnp.bfloat16))
            out_o = dot(ws[2 * hp + 1], rhs_odd.astype(jnp.bfloat16))
            for p in range(npair):
                e = out_e[:, hw * p:hw * (p + 1)]
                o = out_o[:, hw * p:hw * (p + 1)]
                s_cols[2 * p][hp] = jnp.where(lo, e, swap(o))
                s_cols[2 * p + 1][hp] = jnp.where(lo, swap(e), o)
        st["s"] = jnp.concatenate(
            [jnp.concatenate(c, axis=1) + bias_ref[...] for c in s_cols], axis=0)

    def output(st):
        r0 = st["r0"]
        yb = _rms(st["u"] * st["s"], gog_ref[...]).astype(jnp.bfloat16)
        o_ref[0, r0:r0 + SUB_TILE, :] = (st["xt"] + dot(st["ya"], wout_ref[0:w, :])
                                          + dot(yb, wout_ref[w:2 * w, :]))

    phases = (gate_and_norm, mix_positions, output)
    states = []
    for step in range(nsub + len(phases)):
        if step < nsub:
            states.append(project(step))
        for k, phase in enumerate(phases):
            i = step - 1 - k
            if 0 <= i < nsub:
                phase(states[i])


def _mixer(x, g_mix, w_in, conv_w, g_v, w_s, bias, g_oc, g_og, w_out):
    b, s, d = x.shape
    ts = min(MIXER_TILE, s)
    const2 = lambda i, j: (0, 0)
    const3 = lambda i, j: (0, 0, 0)
    return pl.pallas_call(
        _mixer_kernel,
        grid=(b, s // ts),
        in_specs=[
            pl.BlockSpec((1, ts, d), lambda i, j: (i, j, 0)),
            pl.BlockSpec(g_mix.shape, const2),
            pl.BlockSpec(w_in.shape, const2),
            pl.BlockSpec(conv_w.shape, const2),
            pl.BlockSpec(g_v.shape, const2),
            pl.BlockSpec(w_s.shape, const3),
            pl.BlockSpec(bias.shape, const2),
            pl.BlockSpec(g_oc.shape, const2),
            pl.BlockSpec(g_og.shape, const2),
            pl.BlockSpec(w_out.shape, const2),
        ],
        out_specs=pl.BlockSpec((1, ts, d), lambda i, j: (i, j, 0)),
        out_shape=jax.ShapeDtypeStruct((b, s, d), jnp.float32),
        scratch_shapes=[pltpu.VMEM((ts + 8, CONV_GROUP_WIDTH), jnp.float32)],
        compiler_params=pltpu.CompilerParams(
            dimension_semantics=("arbitrary", "arbitrary"), vmem_limit_bytes=VMEM_LIMIT_BYTES),
        name="mixer",
    )(x, g_mix, w_in, conv_w, g_v, w_s, bias, g_oc, g_og, w_out)


def _attn_route_kernel(x_ref, kt_ref, v_ref, gx_ref, wq_ref, wo_ref, gf_ref, wrt_ref, brt_ref,
                       x2_ref, slab_ref, counts_ref, carry_ref):
    ts = x_ref.shape[1]
    d = x_ref.shape[2]
    hd = d // XA_HEADS
    nsub = ts // SUB_TILE
    first = (pl.program_id(0) == 0) & (pl.program_id(1) == 0)

    @pl.when(first)
    def _():
        carry_ref[...] = jnp.zeros_like(carry_ref)

    def dot(a, b):
        return jnp.dot(a, b, preferred_element_type=jnp.float32)

    subs = range(nsub)
    x1 = [x_ref[0, i * SUB_TILE:(i + 1) * SUB_TILE, :] for i in subs]
    h2 = [_rms(x, gx_ref[...]).astype(jnp.bfloat16) for x in x1]
    q = [dot(h, wq_ref[...]).astype(jnp.bfloat16) for h in h2]
    heads = [[] for _ in subs]
    for a in range(XA_HEADS):
        sc = [dot(q[i][:, a * hd:(a + 1) * hd], kt_ref[0, a * hd:(a + 1) * hd, :]) * (hd ** -0.5)
              for i in subs]
        p = [jnp.exp(s_ - jnp.max(s_, axis=-1, keepdims=True)) for s_ in sc]
        l = [jnp.sum(p_, axis=-1, keepdims=True) for p_ in p]
        o = [dot(p[i].astype(jnp.bfloat16), v_ref[0, :, a * hd:(a + 1) * hd]) for i in subs]
        for i in subs:
            heads[i].append((o[i] / l[i]).astype(jnp.bfloat16))
    x2 = [x1[i] + dot(jnp.concatenate(heads[i], axis=1), wo_ref[...]) for i in subs]
    for i in subs:
        x2_ref[0, i * SUB_TILE:(i + 1) * SUB_TILE, :] = x2[i]

    h3 = [_rms(x, gf_ref[...]).astype(jnp.bfloat16) for x in x2]
    lgs = [lax.dot_general(wrt_ref[...], h, (((1,), (1,)), ((), ())),
                           preferred_element_type=jnp.float32) + brt_ref[...] for h in h3]
    sub = lax.broadcasted_iota(jnp.int32, (EXPERTS_PER_GROUP, SUB_TILE), 0).astype(jnp.float32)
    big = float(EXPERTS_PER_GROUP)
    is_g = sub < N_GROUPS

    def classify(lg):
        glog = lg[0:EXPERTS_PER_GROUP, :]
        gmax = jnp.max(jnp.where(is_g, glog, _NEG), axis=0, keepdims=True)
        gidx = jnp.min(jnp.where(is_g & (glog == gmax), sub, big), axis=0, keepdims=True)
        el = lg[ROUTE_EXPERT_ROW0:ROUTE_EXPERT_ROW0 + EXPERTS_PER_GROUP, :]
        for grp in range(1, N_GROUPS):
            r0 = ROUTE_EXPERT_ROW0 + grp * EXPERTS_PER_GROUP
            el = jnp.where(gidx == grp, lg[r0:r0 + EXPERTS_PER_GROUP, :], el)
        t1 = jnp.max(el, axis=0, keepdims=True)
        i1 = jnp.min(jnp.where(el == t1, sub, big), axis=0, keepdims=True)
        rest = sub != i1
        t2 = jnp.max(jnp.where(rest, el, _NEG), axis=0, keepdims=True)
        i2 = jnp.min(jnp.where(rest & (el == t2), sub, big), axis=0, keepdims=True)
        a = jnp.minimum(i1, i2)
        b = jnp.maximum(i1, i2)
        pair = a * (2 * EXPERTS_PER_GROUP - 1 - a) * 0.5 + (b - a - 1.0)
        return gidx * N_PAIRS + pair

    cls = [classify(lg) for lg in lgs]

    r = lax.broadcasted_iota(jnp.int32, (SUB_TILE, SUB_TILE), 0)
    c = lax.broadcasted_iota(jnp.int32, (SUB_TILE, SUB_TILE), 1)
    earlier = jnp.where(r < c, 1.0, 0.0).astype(jnp.bfloat16)
    crow = lax.broadcasted_iota(jnp.int32, (ROUTE_LANES, SUB_TILE), 0).astype(jnp.float32)
    oh = [crow == cl for cl in cls]
    ohf = [jnp.where(o_, 1.0, 0.0) for o_ in oh]
    inside = [dot(o_.astype(jnp.bfloat16), earlier) for o_ in ohf]
    seen = carry_ref[...]
    for i in subs:
        rank = jnp.sum(jnp.where(oh[i], inside[i] + seen, 0.0), axis=0, keepdims=True)
        slab_ref[:, i * SUB_TILE:(i + 1) * SUB_TILE] = jnp.where(
            sub == 0, cls[i], jnp.where(sub == 1, rank, 0.0))
        seen = seen + jnp.sum(ohf[i], axis=1, keepdims=True)
    carry_ref[...] = seen
    counts_ref[...] = seen


def _attn_route(x1, kt, v, g_x, w_q, w_o, g_f, w_rt, b_rt):
    b, s, d = x1.shape
    m = v.shape[1]
    ts = min(ATTN_TILE, s)
    nt = s // ts
    const2 = lambda i, j: (0, 0)
    return pl.pallas_call(
        _attn_route_kernel,
        grid=(b, nt),
        in_specs=[
            pl.BlockSpec((1, ts, d), lambda i, j: (i, j, 0)),
            pl.BlockSpec((1, d, m), lambda i, j: (i, 0, 0)),
            pl.BlockSpec((1, m, d), lambda i, j: (i, 0, 0)),
            pl.BlockSpec(g_x.shape, const2),
            pl.BlockSpec(w_q.shape, const2),
            pl.BlockSpec(w_o.shape, const2),
            pl.BlockSpec(g_f.shape, const2),
            pl.BlockSpec(w_rt.shape, const2),
            pl.BlockSpec(b_rt.shape, const2),
        ],
        out_specs=[
            pl.BlockSpec((1, ts, d), lambda i, j: (i, j, 0)),
            pl.BlockSpec((SUBLANES, ts), lambda i, j: (0, i * nt + j)),
            pl.BlockSpec((ROUTE_LANES, 1), const2),
        ],
        out_shape=[
            jax.ShapeDtypeStruct((b, s, d), jnp.float32),
            jax.ShapeDtypeStruct((SUBLANES, b * s), jnp.float32),
            jax.ShapeDtypeStruct((ROUTE_LANES, 1), jnp.float32),
        ],
        scratch_shapes=[pltpu.VMEM((ROUTE_LANES, 1), jnp.float32)],
        compiler_params=pltpu.CompilerParams(
            dimension_semantics=("arbitrary", "arbitrary"), vmem_limit_bytes=VMEM_LIMIT_BYTES),
        name="attn_route",
    )(x1, kt, v, g_x, w_q, w_o, g_f, w_rt, b_rt)


def _dest_kernel(slab_ref, starts_ref, o_ref):
    slab = slab_ref[...]
    tt = slab.shape[1]
    crow = lax.broadcasted_iota(jnp.int32, (ROUTE_LANES, tt), 0).astype(jnp.float32)
    start = jnp.sum(jnp.where(crow == slab[0:1, :], starts_ref[...], 0.0), axis=0, keepdims=True)
    sub = lax.broadcasted_iota(jnp.int32, slab.shape, 0)
    o_ref[...] = jnp.where(sub == 0, start + slab[1:2, :], 0.0).astype(jnp.int32)


def _dest_rows(slab, starts_col):
    t = slab.shape[1]
    tt = min(DEST_TILE, t)
    return pl.pallas_call(
        _dest_kernel,
        grid=(t // tt,),
        in_specs=[pl.BlockSpec((SUBLANES, tt), lambda i: (0, i)),
                  pl.BlockSpec((ROUTE_LANES, 1), lambda i: (0, 0))],
        out_specs=pl.BlockSpec((SUBLANES, tt), lambda i: (0, i)),
        out_shape=jax.ShapeDtypeStruct((SUBLANES, t), jnp.int32),
        compiler_params=pltpu.CompilerParams(dimension_semantics=("arbitrary",)),
        name="dest_rows",
    )(slab, starts_col)


def _sc_split(t):
    info = plsc.get_sparse_core_info()
    workers = info.num_cores * info.num_subcores
    per_w, rem = divmod(t, workers)
    nchunk, rem2 = divmod(per_w, SC_CHUNK_ROWS)
    assert rem == 0 and rem2 == 0 and nchunk % SC_BUFFERS == 0, (t, workers, SC_CHUNK_ROWS)
    return info.num_cores, per_w, nchunk


def _sc_ring(nchunk, load, store):
    ahead = SC_BUFFERS - 1
    for k in range(ahead):
        for cp in load(k, k):
            cp.start()

    @pl.loop(0, nchunk, step=SC_BUFFERS)
    def _(c):
        for b in range(SC_BUFFERS):
            cc = c + b
            refill = (b + ahead) % SC_BUFFERS

            @pl.when(cc >= 1)
            def _():
                for cp in store(cc - 1, refill):
                    cp.wait()

            @pl.when(cc + ahead < nchunk)
            def _():
                for cp in load(cc + ahead, refill):
                    cp.start()

            for cp in load(cc, b):
                cp.wait()
            for cp in store(cc, b):
                cp.start()

    for cp in store(nchunk - 1, (nchunk - 1) % SC_BUFFERS):
        cp.wait()


def _dispatch(x2, dest, cap):
    t, d = x2.shape
    nc, per_w, nchunk = _sc_split(t)
    ch = SC_CHUNK_ROWS
    mesh = plsc.VectorSubcoreMesh(core_axis_name="c", subcore_axis_name="s")
    dma = pltpu.SemaphoreType.DMA

    @functools.partial(
        pl.kernel, mesh=mesh,
        out_type=jax.ShapeDtypeStruct((cap, d), jnp.float32),
        scratch_types=([pltpu.VMEM((ch,), jnp.int32)] * SC_BUFFERS
                       + [pltpu.VMEM((ch, d), jnp.float32)] * SC_BUFFERS + [dma] * (3 * SC_BUFFERS)),
    )
    def dispatch_sc(x_hbm, dest_hbm, xs_hbm, *scratch):
        n = SC_BUFFERS
        idx, rows, isem, rsem, ssem = (scratch[k * n:(k + 1) * n] for k in range(5))
        wid = lax.axis_index("s") * nc + lax.axis_index("c")
        base = wid * per_w

        def load(c, b):
            src = pl.ds(base + c * ch, ch)
            return (pltpu.make_async_copy(dest_hbm.at[src], idx[b], isem[b]),
                    pltpu.make_async_copy(x_hbm.at[src], rows[b], rsem[b]))

        def store(c, b):
            return (pltpu.make_async_copy(rows[b], xs_hbm.at[idx[b]], ssem[b]),)

        _sc_ring(nchunk, load, store)

    return dispatch_sc(x2, dest)


def _unsort(ys, dest, t):
    d = ys.shape[1]
    nc, per_w, nchunk = _sc_split(t)
    ch = SC_CHUNK_ROWS
    mesh = plsc.VectorSubcoreMesh(core_axis_name="c", subcore_axis_name="s")
    dma = pltpu.SemaphoreType.DMA

    @functools.partial(
        pl.kernel, mesh=mesh,
        out_type=jax.ShapeDtypeStruct((t, d), jnp.float32),
        scratch_types=([pltpu.VMEM((per_w,), jnp.int32)]
                       + [pltpu.VMEM((ch, d), jnp.float32)] * SC_BUFFERS + [dma] * (2 * SC_BUFFERS)),
    )
    def unsort_sc(ys_hbm, dest_hbm, out_hbm, idx_v, *scratch):
        n = SC_BUFFERS
        rows, gsem, wsem = (scratch[k * n:(k + 1) * n] for k in range(3))
        wid = lax.axis_index("s") * nc + lax.axis_index("c")
        base = wid * per_w
        pltpu.sync_copy(dest_hbm.at[pl.ds(base, per_w)], idx_v)

        def load(c, b):
            return (pltpu.make_async_copy(ys_hbm.at[idx_v.at[pl.ds(c * ch, ch)]], rows[b], gsem[b]),)

        def store(c, b):
            return (pltpu.make_async_copy(rows[b], out_hbm.at[pl.ds(base + c * ch, ch)], wsem[b]),)

        _sc_ring(nchunk, load, store)

    return unsort_sc(ys, dest)


def _expert_kernel(grp_ref, ea_ref, eb_ref, nvalid_ref, nact_ref,
                   xs_ref, gf_ref, wr_ref, br_ref, gfin_ref, wg_hbm, wu_hbm, wd_hbm,
                   o_ref, wgb, wub, wdb, sg, su, sd, sem, cnt_ref):
    s = pl.program_id(0)
    blocks = tuple(BLOCKS_PER_STEP * s + i for i in range(BLOCKS_PER_STEP))
    g = grp_ref[blocks[0]]

    def stage(e):
        slot = e % WEIGHT_STAGES
        return (pltpu.make_async_copy(wg_hbm.at[e], sg.at[slot], sem.at[slot, 0]),
                pltpu.make_async_copy(wu_hbm.at[e], su.at[slot], sem.at[slot, 1]),
                pltpu.make_async_copy(wd_hbm.at[e], sd.at[slot], sem.at[slot, 2]))

    def start_next():
        @pl.when(cnt_ref[0] < N_EXPERTS)
        def _():
            for cp in stage(cnt_ref[0]):
                cp.start()
            cnt_ref[0] = cnt_ref[0] + 1

    @pl.when(s == 0)
    def _():
        cnt_ref[0] = 0
        cnt_ref[1] = 0
        for _ in range(WEIGHT_STAGES):
            start_next()

    active = blocks[0] < nact_ref[0]

    @pl.when(active)
    def _():
        need = g * EXPERTS_PER_GROUP + functools.reduce(
            jnp.maximum, [eb_ref[j] for j in blocks])

        def load(e, carry):
            for cp in stage(e):
                cp.wait()
            slot = e % WEIGHT_STAGES
            k = e % EXPERTS_PER_GROUP
            wgb[k] = sg[slot].astype(jnp.bfloat16)
            wub[k] = su[slot].astype(jnp.bfloat16)
            wdb[k] = sd[slot].astype(jnp.bfloat16)
            cnt_ref[1] = e + 1
            start_next()
            return carry

        lax.fori_loop(cnt_ref[1], need + 1, load, 0)

        lane = lax.broadcasted_iota(jnp.int32, (ROW_BLOCK, ROUTE_LANES), 1)
        is_g = lane < N_GROUPS
        lo = EXPERT_LANE0 + EXPERTS_PER_GROUP * g
        nblk = len(blocks)
        rowid = lax.broadcasted_iota(jnp.int32, (ROW_BLOCK, 1), 0)
        xs = [jnp.where(rowid < nvalid_ref[blocks[i]],
                        xs_ref[i * ROW_BLOCK:(i + 1) * ROW_BLOCK, :], 0.0) for i in range(nblk)]
        h3 = [_rms(x, gf_ref[...]).astype(jnp.bfloat16) for x in xs]
        lg = [jnp.dot(h, wr_ref[...], preferred_element_type=jnp.float32) + br_ref[...]
              for h in h3]

        def gates(lgi, j):
            def pick(idx):
                return jnp.sum(jnp.where(lane == idx, lgi, 0.0), axis=-1, keepdims=True)

            gmax = jnp.max(jnp.where(is_g, lgi, _NEG), axis=-1, keepdims=True)
            den = jnp.sum(jnp.where(is_g, jnp.exp(jnp.where(is_g, lgi, _NEG) - gmax), 0.0),
                          axis=-1, keepdims=True)
            grp_p = jnp.exp(pick(g) - gmax) / den
            la = pick(lo + ea_ref[j])
            lb = pick(lo + eb_ref[j])
            m = jnp.maximum(la, lb)
            pa = jnp.exp(la - m)
            pb = jnp.exp(lb - m)
            return grp_p * pa / (pa + pb), grp_p * pb / (pa + pb)

        gate = [gates(lg[i], blocks[i]) for i in range(nblk)]
        ys = list(xs)
        for which, e_ref in enumerate((ea_ref, eb_ref)):
            ks = [e_ref[j] for j in blocks]
            gg = [jnp.dot(h3[i], wgb[ks[i]], preferred_element_type=jnp.float32)
                  for i in range(nblk)]
            uu = [jnp.dot(h3[i], wub[ks[i]], preferred_element_type=jnp.float32)
                  for i in range(nblk)]
            act = [(gg[i] * (1.0 / (1.0 + jnp.exp(-gg[i]))) * uu[i]
                    * gate[i][which]).astype(jnp.bfloat16) for i in range(nblk)]
            ys = [ys[i] + jnp.dot(act[i], wdb[ks[i]], preferred_element_type=jnp.float32)
                  for i in range(nblk)]
        for i in range(nblk):
            o_ref[i * ROW_BLOCK:(i + 1) * ROW_BLOCK, :] = _rms(ys[i], gfin_ref[...])

    @pl.when(s == pl.num_programs(0) - 1)
    def _():
        def drain(e, carry):
            for cp in stage(e):
                cp.wait()
            return carry

        lax.fori_loop(cnt_ref[1], cnt_ref[0], drain, 0)


def _experts(xs, blk_grp, blk_a, blk_b, nvalid, nact, g_ffn, w_r, b_r, g_final, w_gate, w_up, w_down):
    cap, d = xs.shape
    de = w_gate.shape[2]
    step_rows = BLOCKS_PER_STEP * ROW_BLOCK
    steps = cap // step_rows
    pre = lambda f: (lambda s, gr, ea, eb, nv, na: f(s, na))
    const2 = pre(lambda s, na: (0, 0))
    last_step = lambda na: (na[0] - 1) // BLOCKS_PER_STEP
    hbm = pl.BlockSpec(memory_space=pl.ANY)
    active_step = pre(lambda s, na: (jnp.minimum(s, last_step(na)), 0))
    grid_spec = pltpu.PrefetchScalarGridSpec(
        num_scalar_prefetch=5,
        grid=(steps,),
        in_specs=[
            pl.BlockSpec((step_rows, d), active_step),
            pl.BlockSpec(g_ffn.shape, const2),
            pl.BlockSpec(w_r.shape, const2),
            pl.BlockSpec(b_r.shape, const2),
            pl.BlockSpec(g_final.shape, const2),
            hbm, hbm, hbm,
        ],
        out_specs=pl.BlockSpec((step_rows, d), active_step),
        scratch_shapes=[
            pltpu.VMEM((EXPERTS_PER_GROUP, d, de), jnp.bfloat16),
            pltpu.VMEM((EXPERTS_PER_GROUP, d, de), jnp.bfloat16),
            pltpu.VMEM((EXPERTS_PER_GROUP, de, d), jnp.bfloat16),
            pltpu.VMEM((WEIGHT_STAGES, d, de), jnp.float32),
            pltpu.VMEM((WEIGHT_STAGES, d, de), jnp.float32),
            pltpu.VMEM((WEIGHT_STAGES, de, d), jnp.float32),
            pltpu.SemaphoreType.DMA((WEIGHT_STAGES, 3)),
            pltpu.SMEM((2,), jnp.int32),
        ],
    )
    return pl.pallas_call(
        _expert_kernel,
        grid_spec=grid_spec,
        out_shape=jax.ShapeDtypeStruct((cap, d), jnp.float32),
        compiler_params=pltpu.CompilerParams(
            dimension_semantics=("arbitrary",), vmem_limit_bytes=EXPERT_VMEM_LIMIT_BYTES),
        name="experts",
    )(blk_grp, blk_a, blk_b, nvalid, nact, xs, g_ffn, w_r, b_r, g_final, w_gate, w_up, w_down)


def _moe_final(x2, slab, counts_col, g_ffn, w_r, b_r, g_final, w_gate, w_up, w_down):
    t, d = x2.shape
    nb = t // ROW_BLOCK + N_CLASSES + N_GROUPS * (BLOCKS_PER_STEP - 1)
    assert nb % BLOCKS_PER_STEP == 0
    counts = counts_col[:N_CLASSES, 0].astype(jnp.int32)
    nblk = (counts + ROW_BLOCK - 1) // ROW_BLOCK
    grp_blocks = jnp.sum(nblk.reshape(N_GROUPS, N_PAIRS), axis=1)
    nblk = nblk.reshape(N_GROUPS, N_PAIRS).at[:, N_PAIRS - 1].add(
        (-grp_blocks) % BLOCKS_PER_STEP).reshape(N_CLASSES)
    blk_end = jnp.cumsum(nblk)
    blk_start = blk_end - nblk
    nact = blk_end[-1]
    j = jnp.arange(nb, dtype=jnp.int32)
    blk_class = jnp.minimum(
        jnp.sum((blk_end[None, :] <= j[:, None]).astype(jnp.int32), axis=1), N_CLASSES - 1)
    blk_class = jnp.where(j < nact, blk_class, blk_class[jnp.maximum(nact - 1, 0)])
    nvalid = jnp.where(j < nact, jnp.clip(
        counts[blk_class] - (j - blk_start[blk_class]) * ROW_BLOCK, 0, ROW_BLOCK), 0)
    pair_a = jnp.array([p[0] for p in PAIRS], jnp.int32)
    pair_b = jnp.array([p[1] for p in PAIRS], jnp.int32)
    blk_grp = blk_class // N_PAIRS
    blk_a = pair_a[blk_class % N_PAIRS]
    blk_b = pair_b[blk_class % N_PAIRS]
    starts_col = jnp.zeros((ROUTE_LANES, 1), jnp.float32).at[:N_CLASSES, 0].set(
        (blk_start * ROW_BLOCK).astype(jnp.float32))

    dest = _dest_rows(slab, starts_col)[0]
    xs = _dispatch(x2, dest, nb * ROW_BLOCK)
    ys = _experts(xs, blk_grp, blk_a, blk_b, nvalid, nact[None], g_ffn, w_r, b_r, g_final,
                  w_gate, w_up, w_down)
    return _unsort(ys, dest, t)


def kernel(x, mem, g_mix, w_in, conv_w, g_v, w_s, b_s, g_out_conv, g_out_gmlp, w_out, g_xattn,
           g_mem, w_q, w_k, w_v, w_o, g_ffn, w_grp, b_grp, w_rt, b_rt, w_gate, w_up, w_down,
           g_final):
    b, s, d = x.shape
    assert g_mix.shape[0] == 1, "the final norm is fused into the single layer's expert kernel"
    assert N_CLASSES <= ROUTE_LANES
    bf = jnp.bfloat16
    kt, v = _kv_proj(mem, g_mem[0][None], w_k[0].astype(bf), w_v[0].astype(bf))

    bias = jnp.repeat(b_s[0].T, GMLP_HEAD_DIM, axis=1)
    x1 = _mixer(x, g_mix[0][None], w_in[0].astype(bf), conv_w[0], g_v[0][None], w_s[0], bias,
                g_out_conv[0][None], g_out_gmlp[0][None], w_out[0].astype(bf))

    pad = ROUTE_LANES - N_GROUPS - N_EXPERTS
    w_r = jnp.concatenate([w_grp[0], w_rt[0], jnp.zeros((d, pad), jnp.float32)], axis=1).astype(bf)
    b_r = jnp.concatenate([b_grp[0], b_rt[0], jnp.zeros((pad,), jnp.float32)])[None]
    gap = ROUTE_EXPERT_ROW0 - N_GROUPS
    tail = ROUTE_LANES - ROUTE_EXPERT_ROW0 - N_EXPERTS
    w_r_t = jnp.concatenate([w_grp[0].T, jnp.zeros((gap, d), jnp.float32), w_rt[0].T,
                             jnp.zeros((tail, d), jnp.float32)], axis=0).astype(bf)
    b_r_t = jnp.concatenate([b_grp[0], jnp.zeros((gap,), jnp.float32), b_rt[0],
                             jnp.zeros((tail,), jnp.float32)])[:, None]
    x2, slab, counts_col = _attn_route(x1, kt, v, g_xattn[0][None], w_q[0].astype(bf),
                                       w_o[0].astype(bf), g_ffn[0][None], w_r_t, b_r_t)
    out = _moe_final(x2.reshape(b * s, d), slab, counts_col, g_ffn[0][None], w_r, b_r,
                     g_final[None], w_gate[0], w_up[0], w_down[0])
    return out.reshape(b, s, d)
```

```python
import functools

import jax
import jax.numpy as jnp
from jax import lax
from jax.experimental import pallas as pl
from jax.experimental.pallas import tpu as pltpu
from jax.experimental.pallas import tpu_sc as plsc

EPS = 1e-6
CONV_GROUP_WIDTH = 512
GMLP_HEADS = 8
GMLP_HEAD_DIM = 64
CHUNK = 128
XA_HEADS = 4
N_GROUPS = 4
EXPERTS_PER_GROUP = 8
N_EXPERTS = N_GROUPS * EXPERTS_PER_GROUP
ROUTE_LANES = 128
EXPERT_LANE0 = N_GROUPS
ROUTE_EXPERT_ROW0 = 8

N_PAIRS = EXPERTS_PER_GROUP * (EXPERTS_PER_GROUP - 1) // 2
N_CLASSES = N_GROUPS * N_PAIRS
PAIRS = [(a, b) for a in range(EXPERTS_PER_GROUP) for b in range(a + 1, EXPERTS_PER_GROUP)]

MIXER_TILE = 1024
SUB_TILE = 256
DEST_TILE = 8192
ROW_BLOCK = 128
BLOCKS_PER_STEP = 4
WEIGHT_STAGES = 3
SC_CHUNK_ROWS = 16
SC_BUFFERS = 4
VMEM_LIMIT_BYTES = 56 * 1024 * 1024
EXPERT_VMEM_LIMIT_BYTES = 60 * 1024 * 1024

SUBLANES = 8

_NEG = -1e30


def _rms(x, g):
    return x * lax.rsqrt(jnp.mean(x * x, axis=-1, keepdims=True) + EPS) * g


def _gelu_tanh(x):
    return 0.5 * x * (1.0 + jnp.tanh(0.7978845608028654 * (x + 0.044715 * (x * x * x))))


def _kv_kernel(mem_ref, g_ref, wk_ref, wv_ref, kt_ref, v_ref):
    m = _rms(mem_ref[0], g_ref[...]).astype(jnp.bfloat16)
    k = jnp.dot(m, wk_ref[...], preferred_element_type=jnp.float32)
    v = jnp.dot(m, wv_ref[...], preferred_element_type=jnp.float32)
    kt_ref[0] = k.T.astype(jnp.bfloat16)
    v_ref[0] = v.astype(jnp.bfloat16)


def _kv_proj(mem, g_mem, w_k, w_v):
    b, m, d = mem.shape
    const = lambda i: (0, 0)
    return pl.pallas_call(
        _kv_kernel,
        grid=(b,),
        in_specs=[
            pl.BlockSpec((1, m, d), lambda i: (i, 0, 0)),
            pl.BlockSpec((1, d), const),
            pl.BlockSpec((d, d), const),
            pl.BlockSpec((d, d), const),
        ],
        out_specs=[
            pl.BlockSpec((1, d, m), lambda i: (i, 0, 0)),
            pl.BlockSpec((1, m, d), lambda i: (i, 0, 0)),
        ],
        out_shape=[
            jax.ShapeDtypeStruct((b, d, m), jnp.bfloat16),
            jax.ShapeDtypeStruct((b, m, d), jnp.bfloat16),
        ],
        compiler_params=pltpu.CompilerParams(
            dimension_semantics=("arbitrary",), vmem_limit_bytes=VMEM_LIMIT_BYTES),
        name="kv_proj",
    )(mem, g_mem, w_k, w_v)


def _mix_attn_kernel(x_ref, gmix_ref, win_ref, convw_ref, gv_ref, ws_ref, bias_ref,
                     goc_ref, gog_ref, wout_ref, kt_ref, v_ref, gx_ref, wq_ref, wo_ref, gf_ref,
                     wrt_ref, brt_ref, x2_ref, slab_ref, counts_ref, zbuf_ref, carry_ref):
    ts = x_ref.shape[1]
    w = CONV_GROUP_WIDTH
    hw = 2 * GMLP_HEAD_DIM
    nsub = ts // SUB_TILE

    def dot(a, b):
        return jnp.dot(a, b, preferred_element_type=jnp.float32)

    @pl.when(pl.program_id(1) == 0)
    def _():
        zbuf_ref[0:8, :] = jnp.zeros((8, w), jnp.float32)

    @pl.when(pl.program_id(1) != 0)
    def _():
        zbuf_ref[0:8, :] = zbuf_ref[ts:ts + 8, :]

    low = lax.broadcasted_iota(jnp.int32, (SUB_TILE, hw), 1) < GMLP_HEAD_DIM
    row = lax.broadcasted_iota(jnp.int32, (CHUNK, CHUNK), 0)
    colid = lax.broadcasted_iota(jnp.int32, (CHUNK, CHUNK), 1)
    causal = row >= colid
    ws = [jnp.where(causal, ws_ref[hd], 0.0).astype(jnp.bfloat16) for hd in range(GMLP_HEADS)]
    npair = SUB_TILE // CHUNK // 2
    lo = lax.broadcasted_iota(jnp.int32, (CHUNK, hw), 1) < GMLP_HEAD_DIM
    swap = lambda a: pltpu.roll(a, GMLP_HEAD_DIM, axis=1)

    def project(i):
        r0 = i * SUB_TILE
        xt = x_ref[0, r0:r0 + SUB_TILE, :]
        h = _rms(xt, gmix_ref[...]).astype(jnp.bfloat16)
        p = [dot(h, win_ref[:, k * w:(k + 1) * w]) for k in range(5)]
        z = p[1] * p[2]
        zbuf_ref[8 + r0:8 + r0 + SUB_TILE, :] = z
        return dict(xt=xt, gate_b=p[0], z=z, u=p[3], v=p[4], r0=r0)

    def gate_and_norm(st):
        r0 = st["r0"]
        zc = (convw_ref[0:1, :] * zbuf_ref[6 + r0:6 + r0 + SUB_TILE, :]
              + convw_ref[1:2, :] * zbuf_ref[7 + r0:7 + r0 + SUB_TILE, :]
              + convw_ref[2:3, :] * st["z"])
        st["ya"] = _rms(st["gate_b"] * zc, goc_ref[...]).astype(jnp.bfloat16)
        st["u"] = _gelu_tanh(st["u"])
        v = _gelu_tanh(st["v"])
        v2 = v * v
        ss_cols = []
        for k in range(GMLP_HEADS // 2):
            col = v2[:, hw * k:hw * (k + 1)]
            ss_cols.append(jnp.where(low,
                                     jnp.sum(jnp.where(low, col, 0.0), axis=1, keepdims=True),
                                     jnp.sum(jnp.where(low, 0.0, col), axis=1, keepdims=True)))
        ss = jnp.concatenate(ss_cols, axis=1)
        st["vn"] = v * lax.rsqrt(ss * (1.0 / GMLP_HEAD_DIM) + EPS) * gv_ref[...]

    def mix_positions(st):
        vn = st["vn"]
        s_cols = [[None] * (GMLP_HEADS // 2) for _ in range(2 * npair)]
        for hp in range(GMLP_HEADS // 2):
            cols = [vn[c * CHUNK:(c + 1) * CHUNK, hw * hp:hw * (hp + 1)] for c in range(2 * npair)]
            swapped = [swap(a) for a in cols]
            rhs_even = jnp.concatenate(
                [jnp.where(lo, cols[2 * p], swapped[2 * p + 1]) for p in range(npair)], axis=1)
            rhs_odd = jnp.concatenate(
                [jnp.where(lo, swapped[2 * p], cols[2 * p + 1]) for p in range(npair)], axis=1)
            out_e = dot(ws[2 * hp], rhs_even.astype(jnp.bfloat16))
            out_o = dot(ws[2 * hp + 1], rhs_odd.astype(jnp.bfloat16))
            for p in range(npair):
                e = out_e[:, hw * p:hw * (p + 1)]
                o = out_o[:, hw * p:hw * (p + 1)]
                s_cols[2 * p][hp] = jnp.where(lo, e, swap(o))
                s_cols[2 * p + 1][hp] = jnp.where(lo, swap(e), o)
        st["s"] = jnp.concatenate(
            [jnp.concatenate(c, axis=1) + bias_ref[...] for c in s_cols], axis=0)

    def output(st):
        r0 = st["r0"]
        yb = _rms(st["u"] * st["s"], gog_ref[...]).astype(jnp.bfloat16)
        st["x1"] = (st["xt"] + dot(st["ya"], wout_ref[0:w, :])
                    + dot(yb, wout_ref[w:2 * w, :]))

    phases = (gate_and_norm, mix_positions, output)
    states = []
    for step in range(nsub + len(phases)):
        if step < nsub:
            states.append(project(step))
        for k, phase in enumerate(phases):
            i = step - 1 - k
            if 0 <= i < nsub:
                phase(states[i])

    d = x_ref.shape[2]
    hd = d // XA_HEADS
    nsub = ts // SUB_TILE
    first = (pl.program_id(0) == 0) & (pl.program_id(1) == 0)

    @pl.when(first)
    def _():
        carry_ref[...] = jnp.zeros_like(carry_ref)

    def dot(a, b):
        return jnp.dot(a, b, preferred_element_type=jnp.float32)

    subs = range(nsub)
    x1 = [st["x1"] for st in states]
    h2 = [_rms(x, gx_ref[...]).astype(jnp.bfloat16) for x in x1]
    q = [dot(h, wq_ref[...]).astype(jnp.bfloat16) for h in h2]
    heads = [[] for _ in subs]
    for a in range(XA_HEADS):
        sc = [dot(q[i][:, a * hd:(a + 1) * hd], kt_ref[0, a * hd:(a + 1) * hd, :]) * (hd ** -0.5)
              for i in subs]
        p = [jnp.exp(s_ - jnp.max(s_, axis=-1, keepdims=True)) for s_ in sc]
        l = [jnp.sum(p_, axis=-1, keepdims=True) for p_ in p]
        o = [dot(p[i].astype(jnp.bfloat16), v_ref[0, :, a * hd:(a + 1) * hd]) for i in subs]
        for i in subs:
            heads[i].append((o[i] / l[i]).astype(jnp.bfloat16))
    x2 = [x1[i] + dot(jnp.concatenate(heads[i], axis=1), wo_ref[...]) for i in subs]
    for i in subs:
        x2_ref[0, i * SUB_TILE:(i + 1) * SUB_TILE, :] = x2[i]

    h3 = [_rms(x, gf_ref[...]).astype(jnp.bfloat16) for x in x2]
    lgs = [lax.dot_general(wrt_ref[...], h, (((1,), (1,)), ((), ())),
                           preferred_element_type=jnp.float32) + brt_ref[...] for h in h3]
    sub = lax.broadcasted_iota(jnp.int32, (EXPERTS_PER_GROUP, SUB_TILE), 0).astype(jnp.float32)
    big = float(EXPERTS_PER_GROUP)
    is_g = sub < N_GROUPS

    def classify(lg):
        glog = lg[0:EXPERTS_PER_GROUP, :]
        gmax = jnp.max(jnp.where(is_g, glog, _NEG), axis=0, keepdims=True)
        gidx = jnp.min(jnp.where(is_g & (glog == gmax), sub, big), axis=0, keepdims=True)
        el = lg[ROUTE_EXPERT_ROW0:ROUTE_EXPERT_ROW0 + EXPERTS_PER_GROUP, :]
        for grp in range(1, N_GROUPS):
            r0 = ROUTE_EXPERT_ROW0 + grp * EXPERTS_PER_GROUP
            el = jnp.where(gidx == grp, lg[r0:r0 + EXPERTS_PER_GROUP, :], el)
        t1 = jnp.max(el, axis=0, keepdims=True)
        i1 = jnp.min(jnp.where(el == t1, sub, big), axis=0, keepdims=True)
        rest = sub != i1
        t2 = jnp.max(jnp.where(rest, el, _NEG), axis=0, keepdims=True)
        i2 = jnp.min(jnp.where(rest & (el == t2), sub, big), axis=0, keepdims=True)
        a = jnp.minimum(i1, i2)
        b = jnp.maximum(i1, i2)
        pair = a * (2 * EXPERTS_PER_GROUP - 1 - a) * 0.5 + (b - a - 1.0)
        return gidx * N_PAIRS + pair

    cls = [classify(lg) for lg in lgs]

    r = lax.broadcasted_iota(jnp.int32, (SUB_TILE, SUB_TILE), 0)
    c = lax.broadcasted_iota(jnp.int32, (SUB_TILE, SUB_TILE), 1)
    earlier = jnp.where(r < c, 1.0, 0.0).astype(jnp.bfloat16)
    crow = lax.broadcasted_iota(jnp.int32, (ROUTE_LANES, SUB_TILE), 0).astype(jnp.float32)
    oh = [crow == cl for cl in cls]
    ohf = [jnp.where(o_, 1.0, 0.0) for o_ in oh]
    inside = [dot(o_.astype(jnp.bfloat16), earlier) for o_ in ohf]
    seen = carry_ref[...]
    for i in subs:
        rank = jnp.sum(jnp.where(oh[i], inside[i] + seen, 0.0), axis=0, keepdims=True)
        slab_ref[:, i * SUB_TILE:(i + 1) * SUB_TILE] = jnp.where(
            sub == 0, cls[i], jnp.where(sub == 1, rank, 0.0))
        seen = seen + jnp.sum(ohf[i], axis=1, keepdims=True)
    carry_ref[...] = seen
    counts_ref[...] = seen


def _mix_attn(x, g_mix, w_in, conv_w, g_v, w_s, bias, g_oc, g_og, w_out,
              kt, v, g_x, w_q, w_o, g_f, w_rt, b_rt):
    b, s, d = x.shape
    m = v.shape[1]
    ts = min(MIXER_TILE, s)
    nt = s // ts
    const2 = lambda i, j: (0, 0)
    const3 = lambda i, j: (0, 0, 0)
    full = lambda a: pl.BlockSpec(a.shape, const2 if a.ndim == 2 else const3)
    return pl.pallas_call(
        _mix_attn_kernel,
        grid=(b, nt),
        in_specs=[
            pl.BlockSpec((1, ts, d), lambda i, j: (i, j, 0)),
            full(g_mix), full(w_in), full(conv_w), full(g_v), full(w_s), full(bias), full(g_oc),
            full(g_og), full(w_out),
            pl.BlockSpec((1, d, m), lambda i, j: (i, 0, 0)),
            pl.BlockSpec((1, m, d), lambda i, j: (i, 0, 0)),
            full(g_x), full(w_q), full(w_o), full(g_f), full(w_rt), full(b_rt),
        ],
        out_specs=[
            pl.BlockSpec((1, ts, d), lambda i, j: (i, j, 0)),
            pl.BlockSpec((SUBLANES, ts), lambda i, j: (0, i * nt + j)),
            pl.BlockSpec((ROUTE_LANES, 1), const2),
        ],
        out_shape=[
            jax.ShapeDtypeStruct((b, s, d), jnp.float32),
            jax.ShapeDtypeStruct((SUBLANES, b * s), jnp.float32),
            jax.ShapeDtypeStruct((ROUTE_LANES, 1), jnp.float32),
        ],
        scratch_shapes=[pltpu.VMEM((ts + 8, CONV_GROUP_WIDTH), jnp.float32),
                        pltpu.VMEM((ROUTE_LANES, 1), jnp.float32)],
        compiler_params=pltpu.CompilerParams(
            dimension_semantics=("arbitrary", "arbitrary"), vmem_limit_bytes=EXPERT_VMEM_LIMIT_BYTES),
        name="mix_attn",
    )(x, g_mix, w_in, conv_w, g_v, w_s, bias, g_oc, g_og, w_out, kt, v, g_x, w_q, w_o, g_f, w_rt, b_rt)


def _dest_kernel(slab_ref, starts_ref, o_ref):
    slab = slab_ref[...]
    tt = slab.shape[1]
    crow = lax.broadcasted_iota(jnp.int32, (ROUTE_LANES, tt), 0).astype(jnp.float32)
    start = jnp.sum(jnp.where(crow == slab[0:1, :], starts_ref[...], 0.0), axis=0, keepdims=True)
    sub = lax.broadcasted_iota(jnp.int32, slab.shape, 0)
    o_ref[...] = jnp.where(sub == 0, start + slab[1:2, :], 0.0).astype(jnp.int32)


def _dest_rows(slab, starts_col):
    t = slab.shape[1]
    tt = min(DEST_TILE, t)
    return pl.pallas_call(
        _dest_kernel,
        grid=(t // tt,),
        in_specs=[pl.BlockSpec((SUBLANES, tt), lambda i: (0, i)),
                  pl.BlockSpec((ROUTE_LANES, 1), lambda i: (0, 0))],
        out_specs=pl.BlockSpec((SUBLANES, tt), lambda i: (0, i)),
        out_shape=jax.ShapeDtypeStruct((SUBLANES, t), jnp.int32),
        compiler_params=pltpu.CompilerParams(dimension_semantics=("arbitrary",)),
        name="dest_rows",
    )(slab, starts_col)


def _sc_split(t):
    info = plsc.get_sparse_core_info()
    workers = info.num_cores * info.num_subcores
    per_w, rem = divmod(t, workers)
    nchunk, rem2 = divmod(per_w, SC_CHUNK_ROWS)
    assert rem == 0 and rem2 == 0 and nchunk % SC_BUFFERS == 0, (t, workers, SC_CHUNK_ROWS)
    return info.num_cores, per_w, nchunk


def _sc_ring(nchunk, load, store):
    ahead = SC_BUFFERS - 1
    for k in range(ahead):
        for cp in load(k, k):
            cp.start()

    @pl.loop(0, nchunk, step=SC_BUFFERS)
    def _(c):
        for b in range(SC_BUFFERS):
            cc = c + b
            refill = (b + ahead) % SC_BUFFERS

            @pl.when(cc >= 1)
            def _():
                for cp in store(cc - 1, refill):
                    cp.wait()

            @pl.when(cc + ahead < nchunk)
            def _():
                for cp in load(cc + ahead, refill):
                    cp.start()

            for cp in load(cc, b):
                cp.wait()
            for cp in store(cc, b):
                cp.start()

    for cp in store(nchunk - 1, (nchunk - 1) % SC_BUFFERS):
        cp.wait()


def _dispatch(x2, dest, cap):
    t, d = x2.shape
    nc, per_w, nchunk = _sc_split(t)
    ch = SC_CHUNK_ROWS
    mesh = plsc.VectorSubcoreMesh(core_axis_name="c", subcore_axis_name="s")
    dma = pltpu.SemaphoreType.DMA

    @functools.partial(
        pl.kernel, mesh=mesh,
        out_type=jax.ShapeDtypeStruct((cap, d), jnp.float32),
        scratch_types=([pltpu.VMEM((ch,), jnp.int32)] * SC_BUFFERS
                       + [pltpu.VMEM((ch, d), jnp.float32)] * SC_BUFFERS + [dma] * (3 * SC_BUFFERS)),
    )
    def dispatch_sc(x_hbm, dest_hbm, xs_hbm, *scratch):
        n = SC_BUFFERS
        idx, rows, isem, rsem, ssem = (scratch[k * n:(k + 1) * n] for k in range(5))
        wid = lax.axis_index("s") * nc + lax.axis_index("c")
        base = wid * per_w

        def load(c, b):
            src = pl.ds(base + c * ch, ch)
            return (pltpu.make_async_copy(dest_hbm.at[src], idx[b], isem[b]),
                    pltpu.make_async_copy(x_hbm.at[src], rows[b], rsem[b]))

        def store(c, b):
            return (pltpu.make_async_copy(rows[b], xs_hbm.at[idx[b]], ssem[b]),)

        _sc_ring(nchunk, load, store)

    return dispatch_sc(x2, dest)


def _unsort(ys, dest, t):
    d = ys.shape[1]
    nc, per_w, nchunk = _sc_split(t)
    ch = SC_CHUNK_ROWS
    mesh = plsc.VectorSubcoreMesh(core_axis_name="c", subcore_axis_name="s")
    dma = pltpu.SemaphoreType.DMA

    @functools.partial(
        pl.kernel, mesh=mesh,
        out_type=jax.ShapeDtypeStruct((t, d), jnp.float32),
        scratch_types=([pltpu.VMEM((per_w,), jnp.int32)]
                       + [pltpu.VMEM((ch, d), jnp.float32)] * SC_BUFFERS + [dma] * (2 * SC_BUFFERS)),
    )
    def unsort_sc(ys_hbm, dest_hbm, out_hbm, idx_v, *scratch):
        n = SC_BUFFERS
        rows, gsem, wsem = (scratch[k * n:(k + 1) * n] for k in range(3))
        wid = lax.axis_index("s") * nc + lax.axis_index("c")
        base = wid * per_w
        pltpu.sync_copy(dest_hbm.at[pl.ds(base, per_w)], idx_v)

        def load(c, b):
            return (pltpu.make_async_copy(ys_hbm.at[idx_v.at[pl.ds(c * ch, ch)]], rows[b], gsem[b]),)

        def store(c, b):
            return (pltpu.make_async_copy(rows[b], out_hbm.at[pl.ds(base + c * ch, ch)], wsem[b]),)

        _sc_ring(nchunk, load, store)

    return unsort_sc(ys, dest)


def _expert_kernel(grp_ref, ea_ref, eb_ref, nvalid_ref, nact_ref,
                   xs_ref, gf_ref, wr_ref, br_ref, gfin_ref, wg_hbm, wu_hbm, wd_hbm,
                   o_ref, wgb, wub, wdb, sg, su, sd, sem, cnt_ref):
    s = pl.program_id(0)
    blocks = tuple(BLOCKS_PER_STEP * s + i for i in range(BLOCKS_PER_STEP))
    g = grp_ref[blocks[0]]

    def stage(e):
        slot = e % WEIGHT_STAGES
        return (pltpu.make_async_copy(wg_hbm.at[e], sg.at[slot], sem.at[slot, 0]),
                pltpu.make_async_copy(wu_hbm.at[e], su.at[slot], sem.at[slot, 1]),
                pltpu.make_async_copy(wd_hbm.at[e], sd.at[slot], sem.at[slot, 2]))

    def start_next():
        @pl.when(cnt_ref[0] < N_EXPERTS)
        def _():
            for cp in stage(cnt_ref[0]):
                cp.start()
            cnt_ref[0] = cnt_ref[0] + 1

    @pl.when(s == 0)
    def _():
        cnt_ref[0] = 0
        cnt_ref[1] = 0
        for _ in range(WEIGHT_STAGES):
            start_next()

    active = blocks[0] < nact_ref[0]

    @pl.when(active)
    def _():
        need = g * EXPERTS_PER_GROUP + functools.reduce(
            jnp.maximum, [eb_ref[j] for j in blocks])

        def load(e, carry):
            for cp in stage(e):
                cp.wait()
            slot = e % WEIGHT_STAGES
            k = e % EXPERTS_PER_GROUP
            wgb[k] = sg[slot].astype(jnp.bfloat16)
            wub[k] = su[slot].astype(jnp.bfloat16)
            wdb[k] = sd[slot].astype(jnp.bfloat16)
            cnt_ref[1] = e + 1
            start_next()
            return carry

        lax.fori_loop(cnt_ref[1], need + 1, load, 0)

        lane = lax.broadcasted_iota(jnp.int32, (ROW_BLOCK, ROUTE_LANES), 1)
        is_g = lane < N_GROUPS
        lo = EXPERT_LANE0 + EXPERTS_PER_GROUP * g
        nblk = len(blocks)
        rowid = lax.broadcasted_iota(jnp.int32, (ROW_BLOCK, 1), 0)
        xs = [jnp.where(rowid < nvalid_ref[blocks[i]],
                        xs_ref[i * ROW_BLOCK:(i + 1) * ROW_BLOCK, :], 0.0) for i in range(nblk)]
        h3 = [_rms(x, gf_ref[...]).astype(jnp.bfloat16) for x in xs]
        lg = [jnp.dot(h, wr_ref[...], preferred_element_type=jnp.float32) + br_ref[...]
              for h in h3]

        def gates(lgi, j):
            def pick(idx):
                return jnp.sum(jnp.where(lane == idx, lgi, 0.0), axis=-1, keepdims=True)

            gmax = jnp.max(jnp.where(is_g, lgi, _NEG), axis=-1, keepdims=True)
            den = jnp.sum(jnp.where(is_g, jnp.exp(jnp.where(is_g, lgi, _NEG) - gmax), 0.0),
                          axis=-1, keepdims=True)
            grp_p = jnp.exp(pick(g) - gmax) / den
            la = pick(lo + ea_ref[j])
            lb = pick(lo + eb_ref[j])
            m = jnp.maximum(la, lb)
            pa = jnp.exp(la - m)
            pb = jnp.exp(lb - m)
            return grp_p * pa / (pa + pb), grp_p * pb / (pa + pb)

        gate = [gates(lg[i], blocks[i]) for i in range(nblk)]
        ys = list(xs)
        for which, e_ref in enumerate((ea_ref, eb_ref)):
            ks = [e_ref[j] for j in blocks]
            gg = [jnp.dot(h3[i], wgb[ks[i]], preferred_element_type=jnp.float32)
                  for i in range(nblk)]
            uu = [jnp.dot(h3[i], wub[ks[i]], preferred_element_type=jnp.float32)
                  for i in range(nblk)]
            act = [(gg[i] * (0.5 + 0.5 * jnp.tanh(0.5 * gg[i])) * uu[i]
                    * gate[i][which]).astype(jnp.bfloat16) for i in range(nblk)]
            ys = [ys[i] + jnp.dot(act[i], wdb[ks[i]], preferred_element_type=jnp.float32)
                  for i in range(nblk)]
        for i in range(nblk):
            o_ref[i * ROW_BLOCK:(i + 1) * ROW_BLOCK, :] = _rms(ys[i], gfin_ref[...])

    @pl.when(s == pl.num_programs(0) - 1)
    def _():
        def drain(e, carry):
            for cp in stage(e):
                cp.wait()
            return carry

        lax.fori_loop(cnt_ref[1], cnt_ref[0], drain, 0)


def _experts(xs, blk_grp, blk_a, blk_b, nvalid, nact, g_ffn, w_r, b_r, g_final, w_gate, w_up, w_down):
    cap, d = xs.shape
    de = w_gate.shape[2]
    step_rows = BLOCKS_PER_STEP * ROW_BLOCK
    steps = cap // step_rows
    pre = lambda f: (lambda s, gr, ea, eb, nv, na: f(s, na))
    const2 = pre(lambda s, na: (0, 0))
    last_step = lambda na: (na[0] - 1) // BLOCKS_PER_STEP
    hbm = pl.BlockSpec(memory_space=pl.ANY)
    active_step = pre(lambda s, na: (jnp.minimum(s, last_step(na)), 0))
    grid_spec = pltpu.PrefetchScalarGridSpec(
        num_scalar_prefetch=5,
        grid=(steps,),
        in_specs=[
            pl.BlockSpec((step_rows, d), active_step),
            pl.BlockSpec(g_ffn.shape, const2),
            pl.BlockSpec(w_r.shape, const2),
            pl.BlockSpec(b_r.shape, const2),
            pl.BlockSpec(g_final.shape, const2),
            hbm, hbm, hbm,
        ],
        out_specs=pl.BlockSpec((step_rows, d), active_step),
        scratch_shapes=[
            pltpu.VMEM((EXPERTS_PER_GROUP, d, de), jnp.bfloat16),
            pltpu.VMEM((EXPERTS_PER_GROUP, d, de), jnp.bfloat16),
            pltpu.VMEM((EXPERTS_PER_GROUP, de, d), jnp.bfloat16),
            pltpu.VMEM((WEIGHT_STAGES, d, de), jnp.float32),
            pltpu.VMEM((WEIGHT_STAGES, d, de), jnp.float32),
            pltpu.VMEM((WEIGHT_STAGES, de, d), jnp.float32),
            pltpu.SemaphoreType.DMA((WEIGHT_STAGES, 3)),
            pltpu.SMEM((2,), jnp.int32),
        ],
    )
    return pl.pallas_call(
        _expert_kernel,
        grid_spec=grid_spec,
        out_shape=jax.ShapeDtypeStruct((cap, d), jnp.float32),
        compiler_params=pltpu.CompilerParams(
            dimension_semantics=("arbitrary",), vmem_limit_bytes=EXPERT_VMEM_LIMIT_BYTES),
        name="experts",
    )(blk_grp, blk_a, blk_b, nvalid, nact, xs, g_ffn, w_r, b_r, g_final, w_gate, w_up, w_down)


def _moe_final(x2, slab, counts_col, g_ffn, w_r, b_r, g_final, w_gate, w_up, w_down):
    t, d = x2.shape
    nb = t // ROW_BLOCK + N_CLASSES + N_GROUPS * (BLOCKS_PER_STEP - 1)
    assert nb % BLOCKS_PER_STEP == 0
    counts = counts_col[:N_CLASSES, 0].astype(jnp.int32)
    nblk = (counts + ROW_BLOCK - 1) // ROW_BLOCK
    grp_blocks = jnp.sum(nblk.reshape(N_GROUPS, N_PAIRS), axis=1)
    nblk = nblk.reshape(N_GROUPS, N_PAIRS).at[:, N_PAIRS - 1].add(
        (-grp_blocks) % BLOCKS_PER_STEP).reshape(N_CLASSES)
    blk_end = jnp.cumsum(nblk)
    blk_start = blk_end - nblk
    nact = blk_end[-1]
    j = jnp.arange(nb, dtype=jnp.int32)
    blk_class = jnp.minimum(
        jnp.sum((blk_end[None, :] <= j[:, None]).astype(jnp.int32), axis=1), N_CLASSES - 1)
    blk_class = jnp.where(j < nact, blk_class, blk_class[jnp.maximum(nact - 1, 0)])
    nvalid = jnp.where(j < nact, jnp.clip(
        counts[blk_class] - (j - blk_start[blk_class]) * ROW_BLOCK, 0, ROW_BLOCK), 0)
    pair_a = jnp.array([p[0] for p in PAIRS], jnp.int32)
    pair_b = jnp.array([p[1] for p in PAIRS], jnp.int32)
    blk_grp = blk_class // N_PAIRS
    blk_a = pair_a[blk_class % N_PAIRS]
    blk_b = pair_b[blk_class % N_PAIRS]
    starts_col = jnp.zeros((ROUTE_LANES, 1), jnp.float32).at[:N_CLASSES, 0].set(
        (blk_start * ROW_BLOCK).astype(jnp.float32))

    dest = _dest_rows(slab, starts_col)[0]
    xs = _dispatch(x2, dest, nb * ROW_BLOCK)
    ys = _experts(xs, blk_grp, blk_a, blk_b, nvalid, nact[None], g_ffn, w_r, b_r, g_final,
                  w_gate, w_up, w_down)
    return _unsort(ys, dest, t)


def kernel(x, mem, g_mix, w_in, conv_w, g_v, w_s, b_s, g_out_conv, g_out_gmlp, w_out, g_xattn,
           g_mem, w_q, w_k, w_v, w_o, g_ffn, w_grp, b_grp, w_rt, b_rt, w_gate, w_up, w_down,
           g_final):
    b, s, d = x.shape
    assert g_mix.shape[0] == 1, "the final norm is fused into the single layer's expert kernel"
    assert N_CLASSES <= ROUTE_LANES
    bf = jnp.bfloat16
    kt, v = _kv_proj(mem, g_mem[0][None], w_k[0].astype(bf), w_v[0].astype(bf))

    bias = jnp.repeat(b_s[0].T, GMLP_HEAD_DIM, axis=1)
    pad = ROUTE_LANES - N_GROUPS - N_EXPERTS
    w_r = jnp.concatenate([w_grp[0], w_rt[0], jnp.zeros((d, pad), jnp.float32)], axis=1).astype(bf)
    b_r = jnp.concatenate([b_grp[0], b_rt[0], jnp.zeros((pad,), jnp.float32)])[None]
    gap = ROUTE_EXPERT_ROW0 - N_GROUPS
    tail = ROUTE_LANES - ROUTE_EXPERT_ROW0 - N_EXPERTS
    w_r_t = jnp.concatenate([w_grp[0].T, jnp.zeros((gap, d), jnp.float32), w_rt[0].T,
                             jnp.zeros((tail, d), jnp.float32)], axis=0).astype(bf)
    b_r_t = jnp.concatenate([b_grp[0], jnp.zeros((gap,), jnp.float32), b_rt[0],
                             jnp.zeros((tail,), jnp.float32)])[:, None]
    x2, slab, counts_col = _mix_attn(
        x, g_mix[0][None], w_in[0].astype(bf), conv_w[0], g_v[0][None], w_s[0], bias,
        g_out_conv[0][None], g_out_gmlp[0][None], w_out[0].astype(bf),
        kt, v, g_xattn[0][None], w_q[0].astype(bf), w_o[0].astype(bf), g_ffn[0][None], w_r_t, b_r_t)
    out = _moe_final(x2.reshape(b * s, d), slab, counts_col, g_ffn[0][None], w_r, b_r,
                     g_final[None], w_gate[0], w_up[0], w_down[0])
    return out.reshape(b, s, d)
```

```python
import functools

import jax
import jax.numpy as jnp
from jax import lax
from jax.experimental import pallas as pl
from jax.experimental.pallas import tpu as pltpu
from jax.experimental.pallas import tpu_sc as plsc

EPS = 1e-6
CONV_GROUP_WIDTH = 512
GMLP_HEADS = 8
GMLP_HEAD_DIM = 64
CHUNK = 128
XA_HEADS = 4
N_GROUPS = 4
EXPERTS_PER_GROUP = 8
N_EXPERTS = N_GROUPS * EXPERTS_PER_GROUP
ROUTE_LANES = 128
EXPERT_LANE0 = N_GROUPS
ROUTE_EXPERT_ROW0 = 8

N_PAIRS = EXPERTS_PER_GROUP * (EXPERTS_PER_GROUP - 1) // 2
N_CLASSES = N_GROUPS * N_PAIRS
PAIRS = [(a, b) for a in range(EXPERTS_PER_GROUP) for b in range(a + 1, EXPERTS_PER_GROUP)]

MIXER_TILE = 1024
ATTN_TILE = 2048
SUB_TILE = 256
DEST_TILE = 8192
ROW_BLOCK = 128
BLOCKS_PER_STEP = 4
WEIGHT_STAGES = 3
SC_CHUNK_ROWS = 16
SC_BUFFERS = 4
VMEM_LIMIT_BYTES = 56 * 1024 * 1024
EXPERT_VMEM_LIMIT_BYTES = 60 * 1024 * 1024

SUBLANES = 8

_NEG = -1e30


def _rms(x, g):
    return x * lax.rsqrt(jnp.mean(x * x, axis=-1, keepdims=True) + EPS) * g


def _gelu_tanh(x):
    return 0.5 * x * (1.0 + jnp.tanh(0.7978845608028654 * (x + 0.044715 * (x * x * x))))


def _kv_kernel(mem_ref, g_ref, wk_ref, wv_ref, kt_ref, v_ref):
    m = _rms(mem_ref[0], g_ref[...]).astype(jnp.bfloat16)
    k = jnp.dot(m, wk_ref[...], preferred_element_type=jnp.float32)
    v = jnp.dot(m, wv_ref[...], preferred_element_type=jnp.float32)
    kt_ref[0] = k.T.astype(jnp.bfloat16)
    v_ref[0] = v.astype(jnp.bfloat16)


def _kv_proj(mem, g_mem, w_k, w_v):
    b, m, d = mem.shape
    const = lambda i: (0, 0)
    return pl.pallas_call(
        _kv_kernel,
        grid=(b,),
        in_specs=[
            pl.BlockSpec((1, m, d), lambda i: (i, 0, 0)),
            pl.BlockSpec((1, d), const),
            pl.BlockSpec((d, d), const),
            pl.BlockSpec((d, d), const),
        ],
        out_specs=[
            pl.BlockSpec((1, d, m), lambda i: (i, 0, 0)),
            pl.BlockSpec((1, m, d), lambda i: (i, 0, 0)),
        ],
        out_shape=[
            jax.ShapeDtypeStruct((b, d, m), jnp.bfloat16),
            jax.ShapeDtypeStruct((b, m, d), jnp.bfloat16),
        ],
        compiler_params=pltpu.CompilerParams(
            dimension_semantics=("arbitrary",), vmem_limit_bytes=VMEM_LIMIT_BYTES),
        name="kv_proj",
    )(mem, g_mem, w_k, w_v)


def _mixer_kernel(x_ref, gmix_ref, win_ref, convw_ref, gv_ref, ws_ref, bias_ref,
                  goc_ref, gog_ref, wout_ref, o_ref, zbuf_ref):
    ts = x_ref.shape[1]
    w = CONV_GROUP_WIDTH
    hw = 2 * GMLP_HEAD_DIM
    nsub = ts // SUB_TILE

    def dot(a, b):
        return jnp.dot(a, b, preferred_element_type=jnp.float32)

    @pl.when(pl.program_id(1) == 0)
    def _():
        zbuf_ref[0:8, :] = jnp.zeros((8, w), jnp.float32)

    @pl.when(pl.program_id(1) != 0)
    def _():
        zbuf_ref[0:8, :] = zbuf_ref[ts:ts + 8, :]

    low = lax.broadcasted_iota(jnp.int32, (SUB_TILE, hw), 1) < GMLP_HEAD_DIM
    row = lax.broadcasted_iota(jnp.int32, (CHUNK, CHUNK), 0)
    colid = lax.broadcasted_iota(jnp.int32, (CHUNK, CHUNK), 1)
    causal = row >= colid
    ws = [jnp.where(causal, ws_ref[hd], 0.0).astype(jnp.bfloat16) for hd in range(GMLP_HEADS)]
    npair = SUB_TILE // CHUNK // 2
    lo = lax.broadcasted_iota(jnp.int32, (CHUNK, hw), 1) < GMLP_HEAD_DIM
    swap = lambda a: pltpu.roll(a, GMLP_HEAD_DIM, axis=1)

    def project(i):
        r0 = i * SUB_TILE
        xt = x_ref[0, r0:r0 + SUB_TILE, :]
        h = _rms(xt, gmix_ref[...]).astype(jnp.bfloat16)
        p = [dot(h, win_ref[:, k * w:(k + 1) * w]) for k in range(5)]
        z = p[1] * p[2]
        zbuf_ref[8 + r0:8 + r0 + SUB_TILE, :] = z
        return dict(xt=xt, gate_b=p[0], z=z, u=p[3], v=p[4], r0=r0)

    def gate_and_norm(st):
        r0 = st["r0"]
        zc = (convw_ref[0:1, :] * zbuf_ref[6 + r0:6 + r0 + SUB_TILE, :]
              + convw_ref[1:2, :] * zbuf_ref[7 + r0:7 + r0 + SUB_TILE, :]
              + convw_ref[2:3, :] * st["z"])
        st["ya"] = _rms(st["gate_b"] * zc, goc_ref[...]).astype(jnp.bfloat16)
        st["u"] = _gelu_tanh(st["u"])
        v = _gelu_tanh(st["v"])
        v2 = v * v
        ss_cols = []
        for k in range(GMLP_HEADS // 2):
            col = v2[:, hw * k:hw * (k + 1)]
            ss_cols.append(jnp.where(low,
                                     jnp.sum(jnp.where(low, col, 0.0), axis=1, keepdims=True),
                                     jnp.sum(jnp.where(low, 0.0, col), axis=1, keepdims=True)))
        ss = jnp.concatenate(ss_cols, axis=1)
        st["vn"] = v * lax.rsqrt(ss * (1.0 / GMLP_HEAD_DIM) + EPS) * gv_ref[...]

    def mix_positions(st):
        vn = st["vn"]
        s_cols = [[None] * (GMLP_HEADS // 2) for _ in range(2 * npair)]
        for hp in range(GMLP_HEADS // 2):
            cols = [vn[c * CHUNK:(c + 1) * CHUNK, hw * hp:hw * (hp + 1)] for c in range(2 * npair)]
            swapped = [swap(a) for a in cols]
            rhs_even = jnp.concatenate(
                [jnp.where(lo, cols[2 * p], swapped[2 * p + 1]) for p in range(npair)], axis=1)
            rhs_odd = jnp.concatenate(
                [jnp.where(lo, swapped[2 * p], cols[2 * p + 1]) for p in range(npair)], axis=1)
            out_e = dot(ws[2 * hp], rhs_even.astype(jnp.bfloat16))
            out_o = dot(ws[2 * hp + 1], rhs_odd.astype(jnp.bfloat16))
            for p in range(npair):
                e = out_e[:, hw * p:hw * (p + 1)]
                o = out_o[:, hw * p:hw * (p + 1)]
                s_cols[2 * p][hp] = jnp.where(lo, e, swap(o))
                s_cols[2 * p + 1][hp] = jnp.where(lo, swap(e), o)
        st["s"] = jnp.concatenate(
            [jnp.concatenate(c, axis=1) + bias_ref[...] for c in s_cols], axis=0)

    def output(st):
        r0 = st["r0"]
        yb = _rms(st["u"] * st["s"], gog_ref[...]).astype(jnp.bfloat16)
        o_ref[0, r0:r0 + SUB_TILE, :] = (st["xt"] + dot(st["ya"], wout_ref[0:w, :])
                                          + dot(yb, wout_ref[w:2 * w, :]))

    phases = (gate_and_norm, mix_positions, output)
    states = []
    for step in range(nsub + len(phases)):
        if step < nsub:
            states.append(project(step))
        for k, phase in enumerate(phases):
            i = step - 1 - k
            if 0 <= i < nsub:
                phase(states[i])


def _mixer(x, g_mix, w_in, conv_w, g_v, w_s, bias, g_oc, g_og, w_out):
    b, s, d = x.shape
    ts = min(MIXER_TILE, s)
    const2 = lambda i, j: (0, 0)
    const3 = lambda i, j: (0, 0, 0)
    return pl.pallas_call(
        _mixer_kernel,
        grid=(b, s // ts),
        in_specs=[
            pl.BlockSpec((1, ts, d), lambda i, j: (i, j, 0)),
            pl.BlockSpec(g_mix.shape, const2),
            pl.BlockSpec(w_in.shape, const2),
            pl.BlockSpec(conv_w.shape, const2),
            pl.BlockSpec(g_v.shape, const2),
            pl.BlockSpec(w_s.shape, const3),
            pl.BlockSpec(bias.shape, const2),
            pl.BlockSpec(g_oc.shape, const2),
            pl.BlockSpec(g_og.shape, const2),
            pl.BlockSpec(w_out.shape, const2),
        ],
        out_specs=pl.BlockSpec((1, ts, d), lambda i, j: (i, j, 0)),
        out_shape=jax.ShapeDtypeStruct((b, s, d), jnp.float32),
        scratch_shapes=[pltpu.VMEM((ts + 8, CONV_GROUP_WIDTH), jnp.float32)],
        compiler_params=pltpu.CompilerParams(
            dimension_semantics=("arbitrary", "arbitrary"), vmem_limit_bytes=VMEM_LIMIT_BYTES),
        name="mixer",
    )(x, g_mix, w_in, conv_w, g_v, w_s, bias, g_oc, g_og, w_out)


def _attn_route_kernel(x_ref, kt_ref, v_ref, gx_ref, wq_ref, wo_ref, gf_ref, wrt_ref, brt_ref,
                       x2_ref, slab_ref, counts_ref, carry_ref):
    ts = x_ref.shape[1]
    d = x_ref.shape[2]
    hd = d // XA_HEADS
    nsub = ts // SUB_TILE
    first = (pl.program_id(0) == 0) & (pl.program_id(1) == 0)

    @pl.when(first)
    def _():
        carry_ref[...] = jnp.zeros_like(carry_ref)

    def dot(a, b):
        return jnp.dot(a, b, preferred_element_type=jnp.float32)

    subs = range(nsub)
    x1 = [x_ref[0, i * SUB_TILE:(i + 1) * SUB_TILE, :] for i in subs]
    h2 = [_rms(x, gx_ref[...]).astype(jnp.bfloat16) for x in x1]
    q = [dot(h, wq_ref[...]).astype(jnp.bfloat16) for h in h2]
    heads = [[] for _ in subs]
    for a in range(XA_HEADS):
        sc = [dot(q[i][:, a * hd:(a + 1) * hd], kt_ref[0, a * hd:(a + 1) * hd, :]) * (hd ** -0.5)
              for i in subs]
        p = [jnp.exp(s_ - jnp.max(s_, axis=-1, keepdims=True)) for s_ in sc]
        l = [jnp.sum(p_, axis=-1, keepdims=True) for p_ in p]
        o = [dot(p[i].astype(jnp.bfloat16), v_ref[0, :, a * hd:(a + 1) * hd]) for i in subs]
        for i in subs:
            heads[i].append((o[i] / l[i]).astype(jnp.bfloat16))
    x2 = [x1[i] + dot(jnp.concatenate(heads[i], axis=1), wo_ref[...]) for i in subs]
    for i in subs:
        x2_ref[0, i * SUB_TILE:(i + 1) * SUB_TILE, :] = x2[i]

    h3 = [_rms(x, gf_ref[...]).astype(jnp.bfloat16) for x in x2]
    lgs = [lax.dot_general(wrt_ref[...], h, (((1,), (1,)), ((), ())),
                           preferred_element_type=jnp.float32) + brt_ref[...] for h in h3]
    sub = lax.broadcasted_iota(jnp.int32, (EXPERTS_PER_GROUP, SUB_TILE), 0).astype(jnp.float32)
    big = float(EXPERTS_PER_GROUP)
    is_g = sub < N_GROUPS

    def classify(lg):
        glog = lg[0:EXPERTS_PER_GROUP, :]
        gmax = jnp.max(jnp.where(is_g, glog, _NEG), axis=0, keepdims=True)
        gidx = jnp.min(jnp.where(is_g & (glog == gmax), sub, big), axis=0, keepdims=True)
        el = lg[ROUTE_EXPERT_ROW0:ROUTE_EXPERT_ROW0 + EXPERTS_PER_GROUP, :]
        for grp in range(1, N_GROUPS):
            r0 = ROUTE_EXPERT_ROW0 + grp * EXPERTS_PER_GROUP
            el = jnp.where(gidx == grp, lg[r0:r0 + EXPERTS_PER_GROUP, :], el)
        t1 = jnp.max(el, axis=0, keepdims=True)
        i1 = jnp.min(jnp.where(el == t1, sub, big), axis=0, keepdims=True)
        rest = sub != i1
        t2 = jnp.max(jnp.where(rest, el, _NEG), axis=0, keepdims=True)
        i2 = jnp.min(jnp.where(rest & (el == t2), sub, big), axis=0, keepdims=True)
        a = jnp.minimum(i1, i2)
        b = jnp.maximum(i1, i2)
        pair = a * (2 * EXPERTS_PER_GROUP - 1 - a) * 0.5 + (b - a - 1.0)
        return gidx * N_PAIRS + pair

    cls = [classify(lg) for lg in lgs]

    r = lax.broadcasted_iota(jnp.int32, (SUB_TILE, SUB_TILE), 0)
    c = lax.broadcasted_iota(jnp.int32, (SUB_TILE, SUB_TILE), 1)
    earlier = jnp.where(r < c, 1.0, 0.0).astype(jnp.bfloat16)
    crow = lax.broadcasted_iota(jnp.int32, (ROUTE_LANES, SUB_TILE), 0).astype(jnp.float32)
    oh = [crow == cl for cl in cls]
    ohf = [jnp.where(o_, 1.0, 0.0) for o_ in oh]
    inside = [dot(o_.astype(jnp.bfloat16), earlier) for o_ in ohf]
    seen = carry_ref[...]
    for i in subs:
        rank = jnp.sum(jnp.where(oh[i], inside[i] + seen, 0.0), axis=0, keepdims=True)
        slab_ref[:, i * SUB_TILE:(i + 1) * SUB_TILE] = jnp.where(
            sub == 0, cls[i], jnp.where(sub == 1, rank, 0.0))
        seen = seen + jnp.sum(ohf[i], axis=1, keepdims=True)
    carry_ref[...] = seen
    counts_ref[...] = seen


def _attn_route(x1, kt, v, g_x, w_q, w_o, g_f, w_rt, b_rt):
    b, s, d = x1.shape
    m = v.shape[1]
    ts = min(ATTN_TILE, s)
    nt = s // ts
    const2 = lambda i, j: (0, 0)
    return pl.pallas_call(
        _attn_route_kernel,
        grid=(b, nt),
        in_specs=[
            pl.BlockSpec((1, ts, d), lambda i, j: (i, j, 0)),
            pl.BlockSpec((1, d, m), lambda i, j: (i, 0, 0)),
            pl.BlockSpec((1, m, d), lambda i, j: (i, 0, 0)),
            pl.BlockSpec(g_x.shape, const2),
            pl.BlockSpec(w_q.shape, const2),
            pl.BlockSpec(w_o.shape, const2),
            pl.BlockSpec(g_f.shape, const2),
            pl.BlockSpec(w_rt.shape, const2),
            pl.BlockSpec(b_rt.shape, const2),
        ],
        out_specs=[
            pl.BlockSpec((1, ts, d), lambda i, j: (i, j, 0)),
            pl.BlockSpec((SUBLANES, ts), lambda i, j: (0, i * nt + j)),
            pl.BlockSpec((ROUTE_LANES, 1), const2),
        ],
        out_shape=[
            jax.ShapeDtypeStruct((b, s, d), jnp.float32),
            jax.ShapeDtypeStruct((SUBLANES, b * s), jnp.float32),
            jax.ShapeDtypeStruct((ROUTE_LANES, 1), jnp.float32),
        ],
        scratch_shapes=[pltpu.VMEM((ROUTE_LANES, 1), jnp.float32)],
        compiler_params=pltpu.CompilerParams(
            dimension_semantics=("arbitrary", "arbitrary"), vmem_limit_bytes=VMEM_LIMIT_BYTES),
        name="attn_route",
    )(x1, kt, v, g_x, w_q, w_o, g_f, w_rt, b_rt)


def _dest_kernel(slab_ref, starts_ref, o_ref):
    slab = slab_ref[...]
    tt = slab.shape[1]
    crow = lax.broadcasted_iota(jnp.int32, (ROUTE_LANES, tt), 0).astype(jnp.float32)
    start = jnp.sum(jnp.where(crow == slab[0:1, :], starts_ref[...], 0.0), axis=0, keepdims=True)
    sub = lax.broadcasted_iota(jnp.int32, slab.shape, 0)
    o_ref[...] = jnp.where(sub == 0, start + slab[1:2, :], 0.0).astype(jnp.int32)


def _dest_rows(slab, starts_col):
    t = slab.shape[1]
    tt = min(DEST_TILE, t)
    return pl.pallas_call(
        _dest_kernel,
        grid=(t // tt,),
        in_specs=[pl.BlockSpec((SUBLANES, tt), lambda i: (0, i)),
                  pl.BlockSpec((ROUTE_LANES, 1), lambda i: (0, 0))],
        out_specs=pl.BlockSpec((SUBLANES, tt), lambda i: (0, i)),
        out_shape=jax.ShapeDtypeStruct((SUBLANES, t), jnp.int32),
        compiler_params=pltpu.CompilerParams(dimension_semantics=("arbitrary",)),
        name="dest_rows",
    )(slab, starts_col)


def _sc_split(t):
    info = plsc.get_sparse_core_info()
    workers = info.num_cores * info.num_subcores
    per_w, rem = divmod(t, workers)
    nchunk, rem2 = divmod(per_w, SC_CHUNK_ROWS)
    assert rem == 0 and rem2 == 0 and nchunk % SC_BUFFERS == 0, (t, workers, SC_CHUNK_ROWS)
    return info.num_cores, per_w, nchunk


def _sc_ring(nchunk, load, store):
    ahead = SC_BUFFERS - 1
    for k in range(ahead):
        for cp in load(k, k):
            cp.start()

    @pl.loop(0, nchunk, step=SC_BUFFERS)
    def _(c):
        for b in range(SC_BUFFERS):
            cc = c + b
            refill = (b + ahead) % SC_BUFFERS

            @pl.when(cc >= 1)
            def _():
                for cp in store(cc - 1, refill):
                    cp.wait()

            @pl.when(cc + ahead < nchunk)
            def _():
                for cp in load(cc + ahead, refill):
                    cp.start()

            for cp in load(cc, b):
                cp.wait()
            for cp in store(cc, b):
                cp.start()

    for cp in store(nchunk - 1, (nchunk - 1) % SC_BUFFERS):
        cp.wait()


def _dispatch(x2, dest, cap):
    t, d = x2.shape
    nc, per_w, nchunk = _sc_split(t)
    ch = SC_CHUNK_ROWS
    mesh = plsc.VectorSubcoreMesh(core_axis_name="c", subcore_axis_name="s")
    dma = pltpu.SemaphoreType.DMA

    @functools.partial(
        pl.kernel, mesh=mesh,
        out_type=jax.ShapeDtypeStruct((cap, d), jnp.float32),
        scratch_types=([pltpu.VMEM((ch,), jnp.int32)] * SC_BUFFERS
                       + [pltpu.VMEM((ch, d), jnp.float32)] * SC_BUFFERS + [dma] * (3 * SC_BUFFERS)),
    )
    def dispatch_sc(x_hbm, dest_hbm, xs_hbm, *scratch):
        n = SC_BUFFERS
        idx, rows, isem, rsem, ssem = (scratch[k * n:(k + 1) * n] for k in range(5))
        wid = lax.axis_index("s") * nc + lax.axis_index("c")
        base = wid * per_w

        def load(c, b):
            src = pl.ds(base + c * ch, ch)
            return (pltpu.make_async_copy(dest_hbm.at[src], idx[b], isem[b]),
                    pltpu.make_async_copy(x_hbm.at[src], rows[b], rsem[b]))

        def store(c, b):
            return (pltpu.make_async_copy(rows[b], xs_hbm.at[idx[b]], ssem[b]),)

        _sc_ring(nchunk, load, store)

    return dispatch_sc(x2, dest)


def _unsort(ys, dest, t):
    d = ys.shape[1]
    nc, per_w, nchunk = _sc_split(t)
    ch = SC_CHUNK_ROWS
    mesh = plsc.VectorSubcoreMesh(core_axis_name="c", subcore_axis_name="s")
    dma = pltpu.SemaphoreType.DMA

    @functools.partial(
        pl.kernel, mesh=mesh,
        out_type=jax.ShapeDtypeStruct((t, d), jnp.float32),
        scratch_types=([pltpu.VMEM((per_w,), jnp.int32)]
                       + [pltpu.VMEM((ch, d), jnp.float32)] * SC_BUFFERS + [dma] * (2 * SC_BUFFERS)),
    )
    def unsort_sc(ys_hbm, dest_hbm, out_hbm, idx_v, *scratch):
        n = SC_BUFFERS
        rows, gsem, wsem = (scratch[k * n:(k + 1) * n] for k in range(3))
        wid = lax.axis_index("s") * nc + lax.axis_index("c")
        base = wid * per_w
        pltpu.sync_copy(dest_hbm.at[pl.ds(base, per_w)], idx_v)

        def load(c, b):
            return (pltpu.make_async_copy(ys_hbm.at[idx_v.at[pl.ds(c * ch, ch)]], rows[b], gsem[b]),)

        def store(c, b):
            return (pltpu.make_async_copy(rows[b], out_hbm.at[pl.ds(base + c * ch, ch)], wsem[b]),)

        _sc_ring(nchunk, load, store)

    return unsort_sc(ys, dest)


def _expert_kernel(grp_ref, ea_ref, eb_ref, nvalid_ref, nact_ref,
                   xs_ref, gf_ref, wr_ref, br_ref, gfin_ref, wg_hbm, wu_hbm, wd_hbm,
                   o_ref, wgb, wub, wdb, sg, su, sd, sem, cnt_ref):
    s = pl.program_id(0)
    blocks = tuple(BLOCKS_PER_STEP * s + i for i in range(BLOCKS_PER_STEP))
    g = grp_ref[blocks[0]]

    def stage(e):
        slot = e % WEIGHT_STAGES
        return (pltpu.make_async_copy(wg_hbm.at[e], sg.at[slot], sem.at[slot, 0]),
                pltpu.make_async_copy(wu_hbm.at[e], su.at[slot], sem.at[slot, 1]),
                pltpu.make_async_copy(wd_hbm.at[e], sd.at[slot], sem.at[slot, 2]))

    def start_next():
        @pl.when(cnt_ref[0] < N_EXPERTS)
        def _():
            for cp in stage(cnt_ref[0]):
                cp.start()
            cnt_ref[0] = cnt_ref[0] + 1

    @pl.when(s == 0)
    def _():
        cnt_ref[0] = 0
        cnt_ref[1] = 0
        for _ in range(WEIGHT_STAGES):
            start_next()

    active = blocks[0] < nact_ref[0]

    @pl.when(active)
    def _():
        need = g * EXPERTS_PER_GROUP + functools.reduce(
            jnp.maximum, [eb_ref[j] for j in blocks])

        def load(e, carry):
            for cp in stage(e):
                cp.wait()
            slot = e % WEIGHT_STAGES
            k = e % EXPERTS_PER_GROUP
            wgb[k] = sg[slot].astype(jnp.bfloat16)
            wub[k] = su[slot].astype(jnp.bfloat16)
            wdb[k] = sd[slot].astype(jnp.bfloat16)
            cnt_ref[1] = e + 1
            start_next()
            return carry

        free = jnp.minimum((g + 1) * EXPERTS_PER_GROUP + ea_ref[blocks[0]], N_EXPERTS)
        upto = jnp.maximum(need + 1, jnp.minimum(cnt_ref[1] + 1, free))
        lax.fori_loop(cnt_ref[1], upto, load, 0)

        lane = lax.broadcasted_iota(jnp.int32, (ROW_BLOCK, ROUTE_LANES), 1)
        is_g = lane < N_GROUPS
        lo = EXPERT_LANE0 + EXPERTS_PER_GROUP * g
        nblk = len(blocks)
        rowid = lax.broadcasted_iota(jnp.int32, (ROW_BLOCK, 1), 0)
        xs = [jnp.where(rowid < nvalid_ref[blocks[i]],
                        xs_ref[i * ROW_BLOCK:(i + 1) * ROW_BLOCK, :], 0.0) for i in range(nblk)]
        h3 = [_rms(x, gf_ref[...]).astype(jnp.bfloat16) for x in xs]
        lg = [jnp.dot(h, wr_ref[...], preferred_element_type=jnp.float32) + br_ref[...]
              for h in h3]

        def gates(lgi, j):
            def pick(idx):
                return jnp.sum(jnp.where(lane == idx, lgi, 0.0), axis=-1, keepdims=True)

            gmax = jnp.max(jnp.where(is_g, lgi, _NEG), axis=-1, keepdims=True)
            den = jnp.sum(jnp.where(is_g, jnp.exp(jnp.where(is_g, lgi, _NEG) - gmax), 0.0),
                          axis=-1, keepdims=True)
            grp_p = jnp.exp(pick(g) - gmax) / den
            la = pick(lo + ea_ref[j])
            lb = pick(lo + eb_ref[j])
            m = jnp.maximum(la, lb)
            pa = jnp.exp(la - m)
            pb = jnp.exp(lb - m)
            return grp_p * pa / (pa + pb), grp_p * pb / (pa + pb)

        gate = [gates(lg[i], blocks[i]) for i in range(nblk)]
        ys = list(xs)
        for which, e_ref in enumerate((ea_ref, eb_ref)):
            ks = [e_ref[j] for j in blocks]
            gg = [jnp.dot(h3[i], wgb[ks[i]], preferred_element_type=jnp.float32)
                  for i in range(nblk)]
            uu = [jnp.dot(h3[i], wub[ks[i]], preferred_element_type=jnp.float32)
                  for i in range(nblk)]
            act = [(gg[i] * (0.5 + 0.5 * jnp.tanh(0.5 * gg[i])) * uu[i]
                    * gate[i][which]).astype(jnp.bfloat16) for i in range(nblk)]
            ys = [ys[i] + jnp.dot(act[i], wdb[ks[i]], preferred_element_type=jnp.float32)
                  for i in range(nblk)]
        for i in range(nblk):
            o_ref[i * ROW_BLOCK:(i + 1) * ROW_BLOCK, :] = _rms(ys[i], gfin_ref[...])

    @pl.when(s == pl.num_programs(0) - 1)
    def _():
        def drain(e, carry):
            for cp in stage(e):
                cp.wait()
            return carry

        lax.fori_loop(cnt_ref[1], cnt_ref[0], drain, 0)


def _experts(xs, blk_grp, blk_a, blk_b, nvalid, nact, g_ffn, w_r, b_r, g_final, w_gate, w_up, w_down):
    cap, d = xs.shape
    de = w_gate.shape[2]
    step_rows = BLOCKS_PER_STEP * ROW_BLOCK
    steps = cap // step_rows
    pre = lambda f: (lambda s, gr, ea, eb, nv, na: f(s, na))
    const2 = pre(lambda s, na: (0, 0))
    last_step = lambda na: (na[0] - 1) // BLOCKS_PER_STEP
    hbm = pl.BlockSpec(memory_space=pl.ANY)
    active_step = pre(lambda s, na: (jnp.minimum(s, last_step(na)), 0))
    grid_spec = pltpu.PrefetchScalarGridSpec(
        num_scalar_prefetch=5,
        grid=(steps,),
        in_specs=[
            pl.BlockSpec((step_rows, d), active_step),
            pl.BlockSpec(g_ffn.shape, const2),
            pl.BlockSpec(w_r.shape, const2),
            pl.BlockSpec(b_r.shape, const2),
            pl.BlockSpec(g_final.shape, const2),
            hbm, hbm, hbm,
        ],
        out_specs=pl.BlockSpec((step_rows, d), active_step),
        scratch_shapes=[
            pltpu.VMEM((EXPERTS_PER_GROUP, d, de), jnp.bfloat16),
            pltpu.VMEM((EXPERTS_PER_GROUP, d, de), jnp.bfloat16),
            pltpu.VMEM((EXPERTS_PER_GROUP, de, d), jnp.bfloat16),
            pltpu.VMEM((WEIGHT_STAGES, d, de), jnp.float32),
            pltpu.VMEM((WEIGHT_STAGES, d, de), jnp.float32),
            pltpu.VMEM((WEIGHT_STAGES, de, d), jnp.float32),
            pltpu.SemaphoreType.DMA((WEIGHT_STAGES, 3)),
            pltpu.SMEM((2,), jnp.int32),
        ],
    )
    return pl.pallas_call(
        _expert_kernel,
        grid_spec=grid_spec,
        out_shape=jax.ShapeDtypeStruct((cap, d), jnp.float32),
        compiler_params=pltpu.CompilerParams(
            dimension_semantics=("arbitrary",), vmem_limit_bytes=EXPERT_VMEM_LIMIT_BYTES),
        name="experts",
    )(blk_grp, blk_a, blk_b, nvalid, nact, xs, g_ffn, w_r, b_r, g_final, w_gate, w_up, w_down)


def _moe_final(x2, slab, counts_col, g_ffn, w_r, b_r, g_final, w_gate, w_up, w_down):
    t, d = x2.shape
    nb = t // ROW_BLOCK + N_CLASSES + N_GROUPS * (BLOCKS_PER_STEP - 1)
    assert nb % BLOCKS_PER_STEP == 0
    counts = counts_col[:N_CLASSES, 0].astype(jnp.int32)
    nblk = (counts + ROW_BLOCK - 1) // ROW_BLOCK
    grp_blocks = jnp.sum(nblk.reshape(N_GROUPS, N_PAIRS), axis=1)
    nblk = nblk.reshape(N_GROUPS, N_PAIRS).at[:, N_PAIRS - 1].add(
        (-grp_blocks) % BLOCKS_PER_STEP).reshape(N_CLASSES)
    blk_end = jnp.cumsum(nblk)
    blk_start = blk_end - nblk
    nact = blk_end[-1]
    j = jnp.arange(nb, dtype=jnp.int32)
    blk_class = jnp.minimum(
        jnp.sum((blk_end[None, :] <= j[:, None]).astype(jnp.int32), axis=1), N_CLASSES - 1)
    blk_class = jnp.where(j < nact, blk_class, blk_class[jnp.maximum(nact - 1, 0)])
    nvalid = jnp.where(j < nact, jnp.clip(
        counts[blk_class] - (j - blk_start[blk_class]) * ROW_BLOCK, 0, ROW_BLOCK), 0)
    pair_a = jnp.array([p[0] for p in PAIRS], jnp.int32)
    pair_b = jnp.array([p[1] for p in PAIRS], jnp.int32)
    blk_grp = blk_class // N_PAIRS
    blk_a = pair_a[blk_class % N_PAIRS]
    blk_b = pair_b[blk_class % N_PAIRS]
    starts_col = jnp.zeros((ROUTE_LANES, 1), jnp.float32).at[:N_CLASSES, 0].set(
        (blk_start * ROW_BLOCK).astype(jnp.float32))

    dest = _dest_rows(slab, starts_col)[0]
    xs = _dispatch(x2, dest, nb * ROW_BLOCK)
    ys = _experts(xs, blk_grp, blk_a, blk_b, nvalid, nact[None], g_ffn, w_r, b_r, g_final,
                  w_gate, w_up, w_down)
    return _unsort(ys, dest, t)


def kernel(x, mem, g_mix, w_in, conv_w, g_v, w_s, b_s, g_out_conv, g_out_gmlp, w_out, g_xattn,
           g_mem, w_q, w_k, w_v, w_o, g_ffn, w_grp, b_grp, w_rt, b_rt, w_gate, w_up, w_down,
           g_final):
    b, s, d = x.shape
    assert g_mix.shape[0] == 1, "the final norm is fused into the single layer's expert kernel"
    assert N_CLASSES <= ROUTE_LANES
    bf = jnp.bfloat16
    kt, v = _kv_proj(mem, g_mem[0][None], w_k[0].astype(bf), w_v[0].astype(bf))

    bias = jnp.repeat(b_s[0].T, GMLP_HEAD_DIM, axis=1)
    x1 = _mixer(x, g_mix[0][None], w_in[0].astype(bf), conv_w[0], g_v[0][None], w_s[0], bias,
                g_out_conv[0][None], g_out_gmlp[0][None], w_out[0].astype(bf))

    pad = ROUTE_LANES - N_GROUPS - N_EXPERTS
    w_r = jnp.concatenate([w_grp[0], w_rt[0], jnp.zeros((d, pad), jnp.float32)], axis=1).astype(bf)
    b_r = jnp.concatenate([b_grp[0], b_rt[0], jnp.zeros((pad,), jnp.float32)])[None]
    gap = ROUTE_EXPERT_ROW0 - N_GROUPS
    tail = ROUTE_LANES - ROUTE_EXPERT_ROW0 - N_EXPERTS
    w_r_t = jnp.concatenate([w_grp[0].T, jnp.zeros((gap, d), jnp.float32), w_rt[0].T,
                             jnp.zeros((tail, d), jnp.float32)], axis=0).astype(bf)
    b_r_t = jnp.concatenate([b_grp[0], jnp.zeros((gap,), jnp.float32), b_rt[0],
                             jnp.zeros((tail,), jnp.float32)])[:, None]
    x2, slab, counts_col = _attn_route(x1, kt, v, g_xattn[0][None], w_q[0].astype(bf),
                                       w_o[0].astype(bf), g_ffn[0][None], w_r_t, b_r_t)
    out = _moe_final(x2.reshape(b * s, d), slab, counts_col, g_ffn[0][None], w_r, b_r,
                     g_final[None], w_gate[0], w_up[0], w_down[0])
    return out.reshape(b, s, d)
```

```python
import functools

import jax
import jax.numpy as jnp
from jax import lax
from jax.experimental import pallas as pl
from jax.experimental.pallas import tpu as pltpu
from jax.experimental.pallas import tpu_sc as plsc

EPS = 1e-6
CONV_GROUP_WIDTH = 512
GMLP_HEADS = 8
GMLP_HEAD_DIM = 64
CHUNK = 128
XA_HEADS = 4
N_GROUPS = 4
EXPERTS_PER_GROUP = 8
N_EXPERTS = N_GROUPS * EXPERTS_PER_GROUP
ROUTE_LANES = 128
EXPERT_LANE0 = N_GROUPS
ROUTE_EXPERT_ROW0 = 8

N_PAIRS = EXPERTS_PER_GROUP * (EXPERTS_PER_GROUP - 1) // 2
N_CLASSES = N_GROUPS * N_PAIRS
PAIRS = [(a, b) for a in range(EXPERTS_PER_GROUP) for b in range(a + 1, EXPERTS_PER_GROUP)]

MIXER_TILE = 1024
ATTN_TILE = 2048
SUB_TILE = 256
DEST_TILE = 8192
ROW_BLOCK = 128
BLOCKS_PER_STEP = 4
WEIGHT_STAGES = 3
SC_CHUNK_ROWS = 16
SC_BUFFERS = 4
VMEM_LIMIT_BYTES = 56 * 1024 * 1024
EXPERT_VMEM_LIMIT_BYTES = 60 * 1024 * 1024

SUBLANES = 8
WEIGHT_CHUNK_COLS = 512

_NEG = -1e30


def _rms(x, g):
    return x * lax.rsqrt(jnp.mean(x * x, axis=-1, keepdims=True) + EPS) * g


def _gelu_tanh(x):
    return 0.5 * x * (1.0 + jnp.tanh(0.7978845608028654 * (x + 0.044715 * (x * x * x))))


def _weight_scratch(*shapes):
    return ([pltpu.VMEM(sh, jnp.bfloat16) for sh in shapes]
            + [pltpu.VMEM((shapes[0][0], WEIGHT_CHUNK_COLS), jnp.float32), pltpu.SemaphoreType.DMA])


def _fetch_bf16(w_hbm, dst_ref, stage_ref, sem):
    cols = stage_ref.shape[1]
    for k in range(w_hbm.shape[1] // cols):
        cp = pltpu.make_async_copy(w_hbm.at[:, k * cols:(k + 1) * cols], stage_ref, sem)
        cp.start()
        cp.wait()
        dst_ref[:, k * cols:(k + 1) * cols] = stage_ref[...].astype(jnp.bfloat16)


def _kv_kernel(mem_ref, g_ref, wk_hbm, wv_hbm, kt_ref, v_ref, wk_ref, wv_ref, stage_ref, sem):
    @pl.when(pl.program_id(0) == 0)
    def _():
        _fetch_bf16(wk_hbm, wk_ref, stage_ref, sem)
        _fetch_bf16(wv_hbm, wv_ref, stage_ref, sem)

    m = _rms(mem_ref[0], g_ref[...]).astype(jnp.bfloat16)
    k = jnp.dot(m, wk_ref[...], preferred_element_type=jnp.float32)
    v = jnp.dot(m, wv_ref[...], preferred_element_type=jnp.float32)
    kt_ref[0] = k.T.astype(jnp.bfloat16)
    v_ref[0] = v.astype(jnp.bfloat16)


def _kv_proj(mem, g_mem, w_k, w_v):
    b, m, d = mem.shape
    const = lambda i: (0, 0)
    return pl.pallas_call(
        _kv_kernel,
        grid=(b,),
        in_specs=[
            pl.BlockSpec((1, m, d), lambda i: (i, 0, 0)),
            pl.BlockSpec((1, d), const),
            pl.BlockSpec(memory_space=pl.ANY),
            pl.BlockSpec(memory_space=pl.ANY),
        ],
        scratch_shapes=_weight_scratch((d, d), (d, d)),
        out_specs=[
            pl.BlockSpec((1, d, m), lambda i: (i, 0, 0)),
            pl.BlockSpec((1, m, d), lambda i: (i, 0, 0)),
        ],
        out_shape=[
            jax.ShapeDtypeStruct((b, d, m), jnp.bfloat16),
            jax.ShapeDtypeStruct((b, m, d), jnp.bfloat16),
        ],
        compiler_params=pltpu.CompilerParams(
            dimension_semantics=("arbitrary",), vmem_limit_bytes=VMEM_LIMIT_BYTES),
        name="kv_proj",
    )(mem, g_mem, w_k, w_v)


def _mixer_kernel(x_ref, gmix_ref, win_hbm, convw_ref, gv_ref, ws_ref, bias_ref,
                  goc_ref, gog_ref, wout_hbm, o_ref, zbuf_ref, win_ref, wout_ref, stage_ref, sem):
    @pl.when((pl.program_id(0) == 0) & (pl.program_id(1) == 0))
    def _():
        _fetch_bf16(win_hbm, win_ref, stage_ref, sem)
        _fetch_bf16(wout_hbm, wout_ref, stage_ref, sem)

    ts = x_ref.shape[1]
    w = CONV_GROUP_WIDTH
    hw = 2 * GMLP_HEAD_DIM
    nsub = ts // SUB_TILE

    def dot(a, b):
        return jnp.dot(a, b, preferred_element_type=jnp.float32)

    @pl.when(pl.program_id(1) == 0)
    def _():
        zbuf_ref[0:8, :] = jnp.zeros((8, w), jnp.float32)

    @pl.when(pl.program_id(1) != 0)
    def _():
        zbuf_ref[0:8, :] = zbuf_ref[ts:ts + 8, :]

    low = lax.broadcasted_iota(jnp.int32, (SUB_TILE, hw), 1) < GMLP_HEAD_DIM
    row = lax.broadcasted_iota(jnp.int32, (CHUNK, CHUNK), 0)
    colid = lax.broadcasted_iota(jnp.int32, (CHUNK, CHUNK), 1)
    causal = row >= colid
    ws = [jnp.where(causal, ws_ref[hd], 0.0).astype(jnp.bfloat16) for hd in range(GMLP_HEADS)]
    npair = SUB_TILE // CHUNK // 2
    lo = lax.broadcasted_iota(jnp.int32, (CHUNK, hw), 1) < GMLP_HEAD_DIM
    swap = lambda a: pltpu.roll(a, GMLP_HEAD_DIM, axis=1)

    def project(i):
        r0 = i * SUB_TILE
        xt = x_ref[0, r0:r0 + SUB_TILE, :]
        h = _rms(xt, gmix_ref[...]).astype(jnp.bfloat16)
        p = [dot(h, win_ref[:, k * w:(k + 1) * w]) for k in range(5)]
        z = p[1] * p[2]
        zbuf_ref[8 + r0:8 + r0 + SUB_TILE, :] = z
        return dict(xt=xt, gate_b=p[0], z=z, u=p[3], v=p[4], r0=r0)

    def gate_and_norm(st):
        r0 = st["r0"]
        zc = (convw_ref[0:1, :] * zbuf_ref[6 + r0:6 + r0 + SUB_TILE, :]
              + convw_ref[1:2, :] * zbuf_ref[7 + r0:7 + r0 + SUB_TILE, :]
              + convw_ref[2:3, :] * st["z"])
        st["ya"] = _rms(st["gate_b"] * zc, goc_ref[...]).astype(jnp.bfloat16)
        st["u"] = _gelu_tanh(st["u"])
        v = _gelu_tanh(st["v"])
        v2 = v * v
        ss_cols = []
        for k in range(GMLP_HEADS // 2):
            col = v2[:, hw * k:hw * (k + 1)]
            ss_cols.append(jnp.where(low,
                                     jnp.sum(jnp.where(low, col, 0.0), axis=1, keepdims=True),
                                     jnp.sum(jnp.where(low, 0.0, col), axis=1, keepdims=True)))
        ss = jnp.concatenate(ss_cols, axis=1)
        st["vn"] = v * lax.rsqrt(ss * (1.0 / GMLP_HEAD_DIM) + EPS) * gv_ref[...]

    def mix_positions(st):
        vn = st["vn"]
        s_cols = [[None] * (GMLP_HEADS // 2) for _ in range(2 * npair)]
        for hp in range(GMLP_HEADS // 2):
            cols = [vn[c * CHUNK:(c + 1) * CHUNK, hw * hp:hw * (hp + 1)] for c in range(2 * npair)]
            swapped = [swap(a) for a in cols]
            rhs_even = jnp.concatenate(
                [jnp.where(lo, cols[2 * p], swapped[2 * p + 1]) for p in range(npair)], axis=1)
            rhs_odd = jnp.concatenate(
                [jnp.where(lo, swapped[2 * p], cols[2 * p + 1]) for p in range(npair)], axis=1)
            out_e = dot(ws[2 * hp], rhs_even.astype(jnp.bfloat16))
            out_o = dot(ws[2 * hp + 1], rhs_odd.astype(jnp.bfloat16))
            for p in range(npair):
                e = out_e[:, hw * p:hw * (p + 1)]
                o = out_o[:, hw * p:hw * (p + 1)]
                s_cols[2 * p][hp] = jnp.where(lo, e, swap(o))
                s_cols[2 * p + 1][hp] = jnp.where(lo, swap(e), o)
        st["s"] = jnp.concatenate(
            [jnp.concatenate(c, axis=1) + bias_ref[...] for c in s_cols], axis=0)

    def output(st):
        r0 = st["r0"]
        yb = _rms(st["u"] * st["s"], gog_ref[...]).astype(jnp.bfloat16)
        o_ref[0, r0:r0 + SUB_TILE, :] = (st["xt"] + dot(st["ya"], wout_ref[0:w, :])
                                          + dot(yb, wout_ref[w:2 * w, :]))

    phases = (gate_and_norm, mix_positions, output)
    states = []
    for step in range(nsub + len(phases)):
        if step < nsub:
            states.append(project(step))
        for k, phase in enumerate(phases):
            i = step - 1 - k
            if 0 <= i < nsub:
                phase(states[i])


def _mixer(x, g_mix, w_in, conv_w, g_v, w_s, bias, g_oc, g_og, w_out):
    b, s, d = x.shape
    ts = min(MIXER_TILE, s)
    const2 = lambda i, j: (0, 0)
    const3 = lambda i, j: (0, 0, 0)
    return pl.pallas_call(
        _mixer_kernel,
        grid=(b, s // ts),
        in_specs=[
            pl.BlockSpec((1, ts, d), lambda i, j: (i, j, 0)),
            pl.BlockSpec(g_mix.shape, const2),
            pl.BlockSpec(memory_space=pl.ANY),
            pl.BlockSpec(conv_w.shape, const2),
            pl.BlockSpec(g_v.shape, const2),
            pl.BlockSpec(w_s.shape, const3),
            pl.BlockSpec(bias.shape, const2),
            pl.BlockSpec(g_oc.shape, const2),
            pl.BlockSpec(g_og.shape, const2),
            pl.BlockSpec(memory_space=pl.ANY),
        ],
        out_specs=pl.BlockSpec((1, ts, d), lambda i, j: (i, j, 0)),
        out_shape=jax.ShapeDtypeStruct((b, s, d), jnp.float32),
        scratch_shapes=([pltpu.VMEM((ts + 8, CONV_GROUP_WIDTH), jnp.float32)]
                        + _weight_scratch(w_in.shape, w_out.shape)),
        compiler_params=pltpu.CompilerParams(
            dimension_semantics=("arbitrary", "arbitrary"), vmem_limit_bytes=VMEM_LIMIT_BYTES),
        name="mixer",
    )(x, g_mix, w_in, conv_w, g_v, w_s, bias, g_oc, g_og, w_out)


def _attn_route_kernel(x_ref, kt_ref, v_ref, gx_ref, wq_hbm, wo_hbm, gf_ref, wrt_ref, brt_ref,
                       x2_ref, slab_ref, counts_ref, carry_ref, wq_ref, wo_ref, stage_ref, sem):
    @pl.when((pl.program_id(0) == 0) & (pl.program_id(1) == 0))
    def _():
        _fetch_bf16(wq_hbm, wq_ref, stage_ref, sem)
        _fetch_bf16(wo_hbm, wo_ref, stage_ref, sem)

    ts = x_ref.shape[1]
    d = x_ref.shape[2]
    hd = d // XA_HEADS
    nsub = ts // SUB_TILE
    first = (pl.program_id(0) == 0) & (pl.program_id(1) == 0)

    @pl.when(first)
    def _():
        carry_ref[...] = jnp.zeros_like(carry_ref)

    def dot(a, b):
        return jnp.dot(a, b, preferred_element_type=jnp.float32)

    subs = range(nsub)
    x1 = [x_ref[0, i * SUB_TILE:(i + 1) * SUB_TILE, :] for i in subs]
    h2 = [_rms(x, gx_ref[...]).astype(jnp.bfloat16) for x in x1]
    q = [dot(h, wq_ref[...]).astype(jnp.bfloat16) for h in h2]
    heads = [[] for _ in subs]
    for a in range(XA_HEADS):
        sc = [dot(q[i][:, a * hd:(a + 1) * hd], kt_ref[0, a * hd:(a + 1) * hd, :]) * (hd ** -0.5)
              for i in subs]
        p = [jnp.exp(s_ - jnp.max(s_, axis=-1, keepdims=True)) for s_ in sc]
        l = [jnp.sum(p_, axis=-1, keepdims=True) for p_ in p]
        o = [dot(p[i].astype(jnp.bfloat16), v_ref[0, :, a * hd:(a + 1) * hd]) for i in subs]
        for i in subs:
            heads[i].append((o[i] / l[i]).astype(jnp.bfloat16))
    x2 = [x1[i] + dot(jnp.concatenate(heads[i], axis=1), wo_ref[...]) for i in subs]
    for i in subs:
        x2_ref[0, i * SUB_TILE:(i + 1) * SUB_TILE, :] = x2[i]

    h3 = [_rms(x, gf_ref[...]).astype(jnp.bfloat16) for x in x2]
    lgs = [lax.dot_general(wrt_ref[...], h, (((1,), (1,)), ((), ())),
                           preferred_element_type=jnp.float32) + brt_ref[...] for h in h3]
    sub = lax.broadcasted_iota(jnp.int32, (EXPERTS_PER_GROUP, SUB_TILE), 0).astype(jnp.float32)
    big = float(EXPERTS_PER_GROUP)
    is_g = sub < N_GROUPS

    def classify(lg):
        glog = lg[0:EXPERTS_PER_GROUP, :]
        gmax = jnp.max(jnp.where(is_g, glog, _NEG), axis=0, keepdims=True)
        gidx = jnp.min(jnp.where(is_g & (glog == gmax), sub, big), axis=0, keepdims=True)
        el = lg[ROUTE_EXPERT_ROW0:ROUTE_EXPERT_ROW0 + EXPERTS_PER_GROUP, :]
        for grp in range(1, N_GROUPS):
            r0 = ROUTE_EXPERT_ROW0 + grp * EXPERTS_PER_GROUP
            el = jnp.where(gidx == grp, lg[r0:r0 + EXPERTS_PER_GROUP, :], el)
        t1 = jnp.max(el, axis=0, keepdims=True)
        i1 = jnp.min(jnp.where(el == t1, sub, big), axis=0, keepdims=True)
        rest = sub != i1
        t2 = jnp.max(jnp.where(rest, el, _NEG), axis=0, keepdims=True)
        i2 = jnp.min(jnp.where(rest & (el == t2), sub, big), axis=0, keepdims=True)
        a = jnp.minimum(i1, i2)
        b = jnp.maximum(i1, i2)
        pair = a * (2 * EXPERTS_PER_GROUP - 1 - a) * 0.5 + (b - a - 1.0)
        return gidx * N_PAIRS + pair

    cls = [classify(lg) for lg in lgs]

    r = lax.broadcasted_iota(jnp.int32, (SUB_TILE, SUB_TILE), 0)
    c = lax.broadcasted_iota(jnp.int32, (SUB_TILE, SUB_TILE), 1)
    earlier = jnp.where(r < c, 1.0, 0.0).astype(jnp.bfloat16)
    crow = lax.broadcasted_iota(jnp.int32, (ROUTE_LANES, SUB_TILE), 0).astype(jnp.float32)
    oh = [crow == cl for cl in cls]
    ohf = [jnp.where(o_, 1.0, 0.0) for o_ in oh]
    inside = [dot(o_.astype(jnp.bfloat16), earlier) for o_ in ohf]
    seen = carry_ref[...]
    for i in subs:
        rank = jnp.sum(jnp.where(oh[i], inside[i] + seen, 0.0), axis=0, keepdims=True)
        slab_ref[:, i * SUB_TILE:(i + 1) * SUB_TILE] = jnp.where(
            sub == 0, cls[i], jnp.where(sub == 1, rank, 0.0))
        seen = seen + jnp.sum(ohf[i], axis=1, keepdims=True)
    carry_ref[...] = seen
    counts_ref[...] = seen


def _attn_route(x1, kt, v, g_x, w_q, w_o, g_f, w_rt, b_rt):
    b, s, d = x1.shape
    m = v.shape[1]
    ts = min(ATTN_TILE, s)
    nt = s // ts
    const2 = lambda i, j: (0, 0)
    return pl.pallas_call(
        _attn_route_kernel,
        grid=(b, nt),
        in_specs=[
            pl.BlockSpec((1, ts, d), lambda i, j: (i, j, 0)),
            pl.BlockSpec((1, d, m), lambda i, j: (i, 0, 0)),
            pl.BlockSpec((1, m, d), lambda i, j: (i, 0, 0)),
            pl.BlockSpec(g_x.shape, const2),
            pl.BlockSpec(memory_space=pl.ANY),
            pl.BlockSpec(memory_space=pl.ANY),
            pl.BlockSpec(g_f.shape, const2),
            pl.BlockSpec(w_rt.shape, const2),
            pl.BlockSpec(b_rt.shape, const2),
        ],
        out_specs=[
            pl.BlockSpec((1, ts, d), lambda i, j: (i, j, 0)),
            pl.BlockSpec((SUBLANES, ts), lambda i, j: (0, i * nt + j)),
            pl.BlockSpec((ROUTE_LANES, 1), const2),
        ],
        out_shape=[
            jax.ShapeDtypeStruct((b, s, d), jnp.float32),
            jax.ShapeDtypeStruct((SUBLANES, b * s), jnp.float32),
            jax.ShapeDtypeStruct((ROUTE_LANES, 1), jnp.float32),
        ],
        scratch_shapes=([pltpu.VMEM((ROUTE_LANES, 1), jnp.float32)]
                        + _weight_scratch(w_q.shape, w_o.shape)),
        compiler_params=pltpu.CompilerParams(
            dimension_semantics=("arbitrary", "arbitrary"), vmem_limit_bytes=VMEM_LIMIT_BYTES),
        name="attn_route",
    )(x1, kt, v, g_x, w_q, w_o, g_f, w_rt, b_rt)


def _dest_kernel(slab_ref, starts_ref, o_ref):
    slab = slab_ref[...]
    tt = slab.shape[1]
    crow = lax.broadcasted_iota(jnp.int32, (ROUTE_LANES, tt), 0).astype(jnp.float32)
    start = jnp.sum(jnp.where(crow == slab[0:1, :], starts_ref[...], 0.0), axis=0, keepdims=True)
    sub = lax.broadcasted_iota(jnp.int32, slab.shape, 0)
    o_ref[...] = jnp.where(sub == 0, start + slab[1:2, :], 0.0).astype(jnp.int32)


def _dest_rows(slab, starts_col):
    t = slab.shape[1]
    tt = min(DEST_TILE, t)
    return pl.pallas_call(
        _dest_kernel,
        grid=(t // tt,),
        in_specs=[pl.BlockSpec((SUBLANES, tt), lambda i: (0, i)),
                  pl.BlockSpec((ROUTE_LANES, 1), lambda i: (0, 0))],
        out_specs=pl.BlockSpec((SUBLANES, tt), lambda i: (0, i)),
        out_shape=jax.ShapeDtypeStruct((SUBLANES, t), jnp.int32),
        compiler_params=pltpu.CompilerParams(dimension_semantics=("arbitrary",)),
        name="dest_rows",
    )(slab, starts_col)


def _sc_split(t):
    info = plsc.get_sparse_core_info()
    workers = info.num_cores * info.num_subcores
    per_w, rem = divmod(t, workers)
    nchunk, rem2 = divmod(per_w, SC_CHUNK_ROWS)
    assert rem == 0 and rem2 == 0 and nchunk % SC_BUFFERS == 0, (t, workers, SC_CHUNK_ROWS)
    return info.num_cores, per_w, nchunk


def _sc_ring(nchunk, load, store):
    ahead = SC_BUFFERS - 1
    for k in range(ahead):
        for cp in load(k, k):
            cp.start()

    @pl.loop(0, nchunk, step=SC_BUFFERS)
    def _(c):
        for b in range(SC_BUFFERS):
            cc = c + b
            refill = (b + ahead) % SC_BUFFERS

            @pl.when(cc >= 1)
            def _():
                for cp in store(cc - 1, refill):
                    cp.wait()

            @pl.when(cc + ahead < nchunk)
            def _():
                for cp in load(cc + ahead, refill):
                    cp.start()

            for cp in load(cc, b):
                cp.wait()
            for cp in store(cc, b):
                cp.start()

    for cp in store(nchunk - 1, (nchunk - 1) % SC_BUFFERS):
        cp.wait()


def _dispatch(x2, dest, cap):
    t, d = x2.shape
    nc, per_w, nchunk = _sc_split(t)
    ch = SC_CHUNK_ROWS
    mesh = plsc.VectorSubcoreMesh(core_axis_name="c", subcore_axis_name="s")
    dma = pltpu.SemaphoreType.DMA

    @functools.partial(
        pl.kernel, mesh=mesh,
        out_type=jax.ShapeDtypeStruct((cap, d), jnp.float32),
        scratch_types=([pltpu.VMEM((ch,), jnp.int32)] * SC_BUFFERS
                       + [pltpu.VMEM((ch, d), jnp.float32)] * SC_BUFFERS + [dma] * (3 * SC_BUFFERS)),
    )
    def dispatch_sc(x_hbm, dest_hbm, xs_hbm, *scratch):
        n = SC_BUFFERS
        idx, rows, isem, rsem, ssem = (scratch[k * n:(k + 1) * n] for k in range(5))
        wid = lax.axis_index("s") * nc + lax.axis_index("c")
        base = wid * per_w

        def load(c, b):
            src = pl.ds(base + c * ch, ch)
            return (pltpu.make_async_copy(dest_hbm.at[src], idx[b], isem[b]),
                    pltpu.make_async_copy(x_hbm.at[src], rows[b], rsem[b]))

        def store(c, b):
            return (pltpu.make_async_copy(rows[b], xs_hbm.at[idx[b]], ssem[b]),)

        _sc_ring(nchunk, load, store)

    return dispatch_sc(x2, dest)


def _unsort(ys, dest, t):
    d = ys.shape[1]
    nc, per_w, nchunk = _sc_split(t)
    ch = SC_CHUNK_ROWS
    mesh = plsc.VectorSubcoreMesh(core_axis_name="c", subcore_axis_name="s")
    dma = pltpu.SemaphoreType.DMA

    @functools.partial(
        pl.kernel, mesh=mesh,
        out_type=jax.ShapeDtypeStruct((t, d), jnp.float32),
        scratch_types=([pltpu.VMEM((per_w,), jnp.int32)]
                       + [pltpu.VMEM((ch, d), jnp.float32)] * SC_BUFFERS + [dma] * (2 * SC_BUFFERS)),
    )
    def unsort_sc(ys_hbm, dest_hbm, out_hbm, idx_v, *scratch):
        n = SC_BUFFERS
        rows, gsem, wsem = (scratch[k * n:(k + 1) * n] for k in range(3))
        wid = lax.axis_index("s") * nc + lax.axis_index("c")
        base = wid * per_w
        pltpu.sync_copy(dest_hbm.at[pl.ds(base, per_w)], idx_v)

        def load(c, b):
            return (pltpu.make_async_copy(ys_hbm.at[idx_v.at[pl.ds(c * ch, ch)]], rows[b], gsem[b]),)

        def store(c, b):
            return (pltpu.make_async_copy(rows[b], out_hbm.at[pl.ds(base + c * ch, ch)], wsem[b]),)

        _sc_ring(nchunk, load, store)

    return unsort_sc(ys, dest)


def _expert_kernel(grp_ref, ea_ref, eb_ref, nvalid_ref, nact_ref,
                   xs_ref, gf_ref, wr_ref, br_ref, gfin_ref, wg_hbm, wu_hbm, wd_hbm,
                   o_ref, wgb, wub, wdb, sg, su, sd, sem, cnt_ref):
    s = pl.program_id(0)
    blocks = tuple(BLOCKS_PER_STEP * s + i for i in range(BLOCKS_PER_STEP))
    g = grp_ref[blocks[0]]

    def stage(e):
        slot = e % WEIGHT_STAGES
        return (pltpu.make_async_copy(wg_hbm.at[e], sg.at[slot], sem.at[slot, 0]),
                pltpu.make_async_copy(wu_hbm.at[e], su.at[slot], sem.at[slot, 1]),
                pltpu.make_async_copy(wd_hbm.at[e], sd.at[slot], sem.at[slot, 2]))

    def start_next():
        @pl.when(cnt_ref[0] < N_EXPERTS)
        def _():
            for cp in stage(cnt_ref[0]):
                cp.start()
            cnt_ref[0] = cnt_ref[0] + 1

    @pl.when(s == 0)
    def _():
        cnt_ref[0] = 0
        cnt_ref[1] = 0
        for _ in range(WEIGHT_STAGES):
            start_next()

    active = blocks[0] < nact_ref[0]

    @pl.when(active)
    def _():
        need = g * EXPERTS_PER_GROUP + functools.reduce(
            jnp.maximum, [eb_ref[j] for j in blocks])

        def load(e, carry):
            for cp in stage(e):
                cp.wait()
            slot = e % WEIGHT_STAGES
            k = e % EXPERTS_PER_GROUP
            wgb[k] = sg[slot].astype(jnp.bfloat16)
            wub[k] = su[slot].astype(jnp.bfloat16)
            wdb[k] = sd[slot].astype(jnp.bfloat16)
            cnt_ref[1] = e + 1
            start_next()
            return carry

        lax.fori_loop(cnt_ref[1], need + 1, load, 0)

        lane = lax.broadcasted_iota(jnp.int32, (ROW_BLOCK, ROUTE_LANES), 1)
        is_g = lane < N_GROUPS
        lo = EXPERT_LANE0 + EXPERTS_PER_GROUP * g
        nblk = len(blocks)
        rowid = lax.broadcasted_iota(jnp.int32, (ROW_BLOCK, 1), 0)
        xs = [jnp.where(rowid < nvalid_ref[blocks[i]],
                        xs_ref[i * ROW_BLOCK:(i + 1) * ROW_BLOCK, :], 0.0) for i in range(nblk)]
        h3 = [_rms(x, gf_ref[...]).astype(jnp.bfloat16) for x in xs]
        lg = [jnp.dot(h, wr_ref[...], preferred_element_type=jnp.float32) + br_ref[...]
              for h in h3]

        def gates(lgi, j):
            def pick(idx):
                return jnp.sum(jnp.where(lane == idx, lgi, 0.0), axis=-1, keepdims=True)

            gmax = jnp.max(jnp.where(is_g, lgi, _NEG), axis=-1, keepdims=True)
            den = jnp.sum(jnp.where(is_g, jnp.exp(jnp.where(is_g, lgi, _NEG) - gmax), 0.0),
                          axis=-1, keepdims=True)
            grp_p = jnp.exp(pick(g) - gmax) / den
            la = pick(lo + ea_ref[j])
            lb = pick(lo + eb_ref[j])
            m = jnp.maximum(la, lb)
            pa = jnp.exp(la - m)
            pb = jnp.exp(lb - m)
            return grp_p * pa / (pa + pb), grp_p * pb / (pa + pb)

        gate = [gates(lg[i], blocks[i]) for i in range(nblk)]
        ys = list(xs)
        for which, e_ref in enumerate((ea_ref, eb_ref)):
            ks = [e_ref[j] for j in blocks]
            gg = [jnp.dot(h3[i], wgb[ks[i]], preferred_element_type=jnp.float32)
                  for i in range(nblk)]
            uu = [jnp.dot(h3[i], wub[ks[i]], preferred_element_type=jnp.float32)
                  for i in range(nblk)]
            act = [(gg[i] * (0.5 + 0.5 * jnp.tanh(0.5 * gg[i])) * uu[i]
                    * gate[i][which]).astype(jnp.bfloat16) for i in range(nblk)]
            ys = [ys[i] + jnp.dot(act[i], wdb[ks[i]], preferred_element_type=jnp.float32)
                  for i in range(nblk)]
        for i in range(nblk):
            o_ref[i * ROW_BLOCK:(i + 1) * ROW_BLOCK, :] = _rms(ys[i], gfin_ref[...])

    @pl.when(s == pl.num_programs(0) - 1)
    def _():
        def drain(e, carry):
            for cp in stage(e):
                cp.wait()
            return carry

        lax.fori_loop(cnt_ref[1], cnt_ref[0], drain, 0)


def _experts(xs, blk_grp, blk_a, blk_b, nvalid, nact, g_ffn, w_r, b_r, g_final, w_gate, w_up, w_down):
    cap, d = xs.shape
    de = w_gate.shape[2]
    step_rows = BLOCKS_PER_STEP * ROW_BLOCK
    steps = cap // step_rows
    pre = lambda f: (lambda s, gr, ea, eb, nv, na: f(s, na))
    const2 = pre(lambda s, na: (0, 0))
    last_step = lambda na: (na[0] - 1) // BLOCKS_PER_STEP
    hbm = pl.BlockSpec(memory_space=pl.ANY)
    active_step = pre(lambda s, na: (jnp.minimum(s, last_step(na)), 0))
    grid_spec = pltpu.PrefetchScalarGridSpec(
        num_scalar_prefetch=5,
        grid=(steps,),
        in_specs=[
            pl.BlockSpec((step_rows, d), active_step),
            pl.BlockSpec(g_ffn.shape, const2),
            pl.BlockSpec(w_r.shape, const2),
            pl.BlockSpec(b_r.shape, const2),
            pl.BlockSpec(g_final.shape, const2),
            hbm, hbm, hbm,
        ],
        out_specs=pl.BlockSpec((step_rows, d), active_step),
        scratch_shapes=[
            pltpu.VMEM((EXPERTS_PER_GROUP, d, de), jnp.bfloat16),
            pltpu.VMEM((EXPERTS_PER_GROUP, d, de), jnp.bfloat16),
            pltpu.VMEM((EXPERTS_PER_GROUP, de, d), jnp.bfloat16),
            pltpu.VMEM((WEIGHT_STAGES, d, de), jnp.float32),
            pltpu.VMEM((WEIGHT_STAGES, d, de), jnp.float32),
            pltpu.VMEM((WEIGHT_STAGES, de, d), jnp.float32),
            pltpu.SemaphoreType.DMA((WEIGHT_STAGES, 3)),
            pltpu.SMEM((2,), jnp.int32),
        ],
    )
    return pl.pallas_call(
        _expert_kernel,
        grid_spec=grid_spec,
        out_shape=jax.ShapeDtypeStruct((cap, d), jnp.float32),
        compiler_params=pltpu.CompilerParams(
            dimension_semantics=("arbitrary",), vmem_limit_bytes=EXPERT_VMEM_LIMIT_BYTES),
        name="experts",
    )(blk_grp, blk_a, blk_b, nvalid, nact, xs, g_ffn, w_r, b_r, g_final, w_gate, w_up, w_down)


def _moe_final(x2, slab, counts_col, g_ffn, w_r, b_r, g_final, w_gate, w_up, w_down):
    t, d = x2.shape
    nb = t // ROW_BLOCK + N_CLASSES + N_GROUPS * (BLOCKS_PER_STEP - 1)
    assert nb % BLOCKS_PER_STEP == 0
    counts = counts_col[:N_CLASSES, 0].astype(jnp.int32)
    nblk = (counts + ROW_BLOCK - 1) // ROW_BLOCK
    grp_blocks = jnp.sum(nblk.reshape(N_GROUPS, N_PAIRS), axis=1)
    nblk = nblk.reshape(N_GROUPS, N_PAIRS).at[:, N_PAIRS - 1].add(
        (-grp_blocks) % BLOCKS_PER_STEP).reshape(N_CLASSES)
    blk_end = jnp.cumsum(nblk)
    blk_start = blk_end - nblk
    nact = blk_end[-1]
    j = jnp.arange(nb, dtype=jnp.int32)
    blk_class = jnp.minimum(
        jnp.sum((blk_end[None, :] <= j[:, None]).astype(jnp.int32), axis=1), N_CLASSES - 1)
    blk_class = jnp.where(j < nact, blk_class, blk_class[jnp.maximum(nact - 1, 0)])
    nvalid = jnp.where(j < nact, jnp.clip(
        counts[blk_class] - (j - blk_start[blk_class]) * ROW_BLOCK, 0, ROW_BLOCK), 0)
    pair_a = jnp.array([p[0] for p in PAIRS], jnp.int32)
    pair_b = jnp.array([p[1] for p in PAIRS], jnp.int32)
    blk_grp = blk_class // N_PAIRS
    blk_a = pair_a[blk_class % N_PAIRS]
    blk_b = pair_b[blk_class % N_PAIRS]
    starts_col = jnp.zeros((ROUTE_LANES, 1), jnp.float32).at[:N_CLASSES, 0].set(
        (blk_start * ROW_BLOCK).astype(jnp.float32))

    dest = _dest_rows(slab, starts_col)[0]
    xs = _dispatch(x2, dest, nb * ROW_BLOCK)
    ys = _experts(xs, blk_grp, blk_a, blk_b, nvalid, nact[None], g_ffn, w_r, b_r, g_final,
                  w_gate, w_up, w_down)
    return _unsort(ys, dest, t)


def kernel(x, mem, g_mix, w_in, conv_w, g_v, w_s, b_s, g_out_conv, g_out_gmlp, w_out, g_xattn,
           g_mem, w_q, w_k, w_v, w_o, g_ffn, w_grp, b_grp, w_rt, b_rt, w_gate, w_up, w_down,
           g_final):
    b, s, d = x.shape
    assert g_mix.shape[0] == 1, "the final norm is fused into the single layer's expert kernel"
    assert N_CLASSES <= ROUTE_LANES
    bf = jnp.bfloat16
    kt, v = _kv_proj(mem, g_mem[0][None], w_k[0], w_v[0])

    bias = jnp.repeat(b_s[0].T, GMLP_HEAD_DIM, axis=1)
    x1 = _mixer(x, g_mix[0][None], w_in[0], conv_w[0], g_v[0][None], w_s[0], bias,
                g_out_conv[0][None], g_out_gmlp[0][None], w_out[0])

    pad = ROUTE_LANES - N_GROUPS - N_EXPERTS
    w_r = jnp.concatenate([w_grp[0], w_rt[0], jnp.zeros((d, pad), jnp.float32)], axis=1).astype(bf)
    b_r = jnp.concatenate([b_grp[0], b_rt[0], jnp.zeros((pad,), jnp.float32)])[None]
    gap = ROUTE_EXPERT_ROW0 - N_GROUPS
    tail = ROUTE_LANES - ROUTE_EXPERT_ROW0 - N_EXPERTS
    w_r_t = jnp.concatenate([w_grp[0].T, jnp.zeros((gap, d), jnp.float32), w_rt[0].T,
                             jnp.zeros((tail, d), jnp.float32)], axis=0).astype(bf)
    b_r_t = jnp.concatenate([b_grp[0], jnp.zeros((gap,), jnp.float32), b_rt[0],
                             jnp.zeros((tail,), jnp.float32)])[:, None]
    x2, slab, counts_col = _attn_route(x1, kt, v, g_xattn[0][None], w_q[0], w_o[0],
                                       g_ffn[0][None], w_r_t, b_r_t)
    out = _moe_final(x2.reshape(b * s, d), slab, counts_col, g_ffn[0][None], w_r, b_r,
                     g_final[None], w_gate[0], w_up[0], w_down[0])
    return out.reshape(b, s, d)
```

```python
import functools

import jax
import jax.numpy as jnp
from jax import lax
from jax.experimental import pallas as pl
from jax.experimental.pallas import tpu as pltpu
from jax.experimental.pallas import tpu_sc as plsc

EPS = 1e-6
CONV_GROUP_WIDTH = 512
GMLP_HEADS = 8
GMLP_HEAD_DIM = 64
CHUNK = 128
XA_HEADS = 4
N_GROUPS = 4
EXPERTS_PER_GROUP = 8
N_EXPERTS = N_GROUPS * EXPERTS_PER_GROUP
ROUTE_LANES = 128
EXPERT_LANE0 = N_GROUPS
ROUTE_EXPERT_ROW0 = 8

N_PAIRS = EXPERTS_PER_GROUP * (EXPERTS_PER_GROUP - 1) // 2
N_CLASSES = N_GROUPS * N_PAIRS
PAIRS = [(a, b) for a in range(EXPERTS_PER_GROUP) for b in range(a + 1, EXPERTS_PER_GROUP)]

KV_BATCHES_PER_STEP = 8
MIXER_TILE = 1024
ATTN_TILE = 2048
SUB_TILE = 256
DEST_TILE = 8192
ROW_BLOCK = 128
BLOCKS_PER_STEP = 4
WEIGHT_STAGES = 3
SC_CHUNK_ROWS = 16
SC_BUFFERS = 4
VMEM_LIMIT_BYTES = 56 * 1024 * 1024
EXPERT_VMEM_LIMIT_BYTES = 60 * 1024 * 1024

SUBLANES = 8

_NEG = -1e30


def _rms(x, g):
    return x * lax.rsqrt(jnp.mean(x * x, axis=-1, keepdims=True) + EPS) * g


def _gelu_tanh(x):
    return 0.5 * x * (1.0 + jnp.tanh(0.7978845608028654 * (x + 0.044715 * (x * x * x))))


def _kv_kernel(mem_ref, g_ref, wk_ref, wv_ref, kt_ref, v_ref):
    nb, rows, d = mem_ref.shape
    m = _rms(mem_ref[...].reshape(nb * rows, d), g_ref[...]).astype(jnp.bfloat16)
    k = jnp.dot(m, wk_ref[...], preferred_element_type=jnp.float32)
    v = jnp.dot(m, wv_ref[...], preferred_element_type=jnp.float32)
    for i in range(nb):
        kt_ref[i] = k[i * rows:(i + 1) * rows, :].T.astype(jnp.bfloat16)
        v_ref[i] = v[i * rows:(i + 1) * rows, :].astype(jnp.bfloat16)


def _kv_proj(mem, g_mem, w_k, w_v):
    b, m, d = mem.shape
    nb = KV_BATCHES_PER_STEP if b % KV_BATCHES_PER_STEP == 0 else 1
    const = lambda i: (0, 0)
    return pl.pallas_call(
        _kv_kernel,
        grid=(b // nb,),
        in_specs=[
            pl.BlockSpec((nb, m, d), lambda i: (i, 0, 0)),
            pl.BlockSpec((1, d), const),
            pl.BlockSpec((d, d), const),
            pl.BlockSpec((d, d), const),
        ],
        out_specs=[
            pl.BlockSpec((nb, d, m), lambda i: (i, 0, 0)),
            pl.BlockSpec((nb, m, d), lambda i: (i, 0, 0)),
        ],
        out_shape=[
            jax.ShapeDtypeStruct((b, d, m), jnp.bfloat16),
            jax.ShapeDtypeStruct((b, m, d), jnp.bfloat16),
        ],
        compiler_params=pltpu.CompilerParams(
            dimension_semantics=("arbitrary",), vmem_limit_bytes=VMEM_LIMIT_BYTES),
        name="kv_proj",
    )(mem, g_mem, w_k, w_v)


def _mixer_kernel(x_ref, gmix_ref, win_ref, convw_ref, gv_ref, ws_ref, bias_ref,
                  goc_ref, gog_ref, wout_ref, o_ref, zbuf_ref):
    ts = x_ref.shape[1]
    w = CONV_GROUP_WIDTH
    hw = 2 * GMLP_HEAD_DIM
    nsub = ts // SUB_TILE

    def dot(a, b):
        return jnp.dot(a, b, preferred_element_type=jnp.float32)

    @pl.when(pl.program_id(1) == 0)
    def _():
        zbuf_ref[0:8, :] = jnp.zeros((8, w), jnp.float32)

    @pl.when(pl.program_id(1) != 0)
    def _():
        zbuf_ref[0:8, :] = zbuf_ref[ts:ts + 8, :]

    low = lax.broadcasted_iota(jnp.int32, (SUB_TILE, hw), 1) < GMLP_HEAD_DIM
    row = lax.broadcasted_iota(jnp.int32, (CHUNK, CHUNK), 0)
    colid = lax.broadcasted_iota(jnp.int32, (CHUNK, CHUNK), 1)
    causal = row >= colid
    ws = [jnp.where(causal, ws_ref[hd], 0.0).astype(jnp.bfloat16) for hd in range(GMLP_HEADS)]
    npair = SUB_TILE // CHUNK // 2
    lo = lax.broadcasted_iota(jnp.int32, (CHUNK, hw), 1) < GMLP_HEAD_DIM
    swap = lambda a: pltpu.roll(a, GMLP_HEAD_DIM, axis=1)

    def project(i):
        r0 = i * SUB_TILE
        xt = x_ref[0, r0:r0 + SUB_TILE, :]
        h = _rms(xt, gmix_ref[...]).astype(jnp.bfloat16)
        p = [dot(h, win_ref[:, k * w:(k + 1) * w]) for k in range(5)]
        z = p[1] * p[2]
        zbuf_ref[8 + r0:8 + r0 + SUB_TILE, :] = z
        return dict(xt=xt, gate_b=p[0], z=z, u=p[3], v=p[4], r0=r0)

    def gate_and_norm(st):
        r0 = st["r0"]
        zc = (convw_ref[0:1, :] * zbuf_ref[6 + r0:6 + r0 + SUB_TILE, :]
              + convw_ref[1:2, :] * zbuf_ref[7 + r0:7 + r0 + SUB_TILE, :]
              + convw_ref[2:3, :] * st["z"])
        st["ya"] = _rms(st["gate_b"] * zc, goc_ref[...]).astype(jnp.bfloat16)
        st["u"] = _gelu_tanh(st["u"])
        v = _gelu_tanh(st["v"])
        v2 = v * v
        ss_cols = []
        for k in range(GMLP_HEADS // 2):
            col = v2[:, hw * k:hw * (k + 1)]
            ss_cols.append(jnp.where(low,
                                     jnp.sum(jnp.where(low, col, 0.0), axis=1, keepdims=True),
                                     jnp.sum(jnp.where(low, 0.0, col), axis=1, keepdims=True)))
        ss = jnp.concatenate(ss_cols, axis=1)
        st["vn"] = v * lax.rsqrt(ss * (1.0 / GMLP_HEAD_DIM) + EPS) * gv_ref[...]

    def mix_positions(st):
        vn = st["vn"]
        s_cols = [[None] * (GMLP_HEADS // 2) for _ in range(2 * npair)]
        for hp in range(GMLP_HEADS // 2):
            cols = [vn[c * CHUNK:(c + 1) * CHUNK, hw * hp:hw * (hp + 1)] for c in range(2 * npair)]
            swapped = [swap(a) for a in cols]
            rhs_even = jnp.concatenate(
                [jnp.where(lo, cols[2 * p], swapped[2 * p + 1]) for p in range(npair)], axis=1)
            rhs_odd = jnp.concatenate(
                [jnp.where(lo, swapped[2 * p], cols[2 * p + 1]) for p in range(npair)], axis=1)
            out_e = dot(ws[2 * hp], rhs_even.astype(jnp.bfloat16))
            out_o = dot(ws[2 * hp + 1], rhs_odd.astype(jnp.bfloat16))
            for p in range(npair):
                e = out_e[:, hw * p:hw * (p + 1)]
                o = out_o[:, hw * p:hw * (p + 1)]
                s_cols[2 * p][hp] = jnp.where(lo, e, swap(o))
                s_cols[2 * p + 1][hp] = jnp.where(lo, swap(e), o)
        st["s"] = jnp.concatenate(
            [jnp.concatenate(c, axis=1) + bias_ref[...] for c in s_cols], axis=0)

    def output(st):
        r0 = st["r0"]
        yb = _rms(st["u"] * st["s"], gog_ref[...]).astype(jnp.bfloat16)
        o_ref[0, r0:r0 + SUB_TILE, :] = (st["xt"] + dot(st["ya"], wout_ref[0:w, :])
                                          + dot(yb, wout_ref[w:2 * w, :]))

    phases = (gate_and_norm, mix_positions, output)
    states = []
    for step in range(nsub + len(phases)):
        if step < nsub:
            states.append(project(step))
        for k, phase in enumerate(phases):
            i = step - 1 - k
            if 0 <= i < nsub:
                phase(states[i])


def _mixer(x, g_mix, w_in, conv_w, g_v, w_s, bias, g_oc, g_og, w_out):
    b, s, d = x.shape
    ts = min(MIXER_TILE, s)
    const2 = lambda i, j: (0, 0)
    const3 = lambda i, j: (0, 0, 0)
    return pl.pallas_call(
        _mixer_kernel,
        grid=(b, s // ts),
        in_specs=[
            pl.BlockSpec((1, ts, d), lambda i, j: (i, j, 0)),
            pl.BlockSpec(g_mix.shape, const2),
            pl.BlockSpec(w_in.shape, const2),
            pl.BlockSpec(conv_w.shape, const2),
            pl.BlockSpec(g_v.shape, const2),
            pl.BlockSpec(w_s.shape, const3),
            pl.BlockSpec(bias.shape, const2),
            pl.BlockSpec(g_oc.shape, const2),
            pl.BlockSpec(g_og.shape, const2),
            pl.BlockSpec(w_out.shape, const2),
        ],
        out_specs=pl.BlockSpec((1, ts, d), lambda i, j: (i, j, 0)),
        out_shape=jax.ShapeDtypeStruct((b, s, d), jnp.float32),
        scratch_shapes=[pltpu.VMEM((ts + 8, CONV_GROUP_WIDTH), jnp.float32)],
        compiler_params=pltpu.CompilerParams(
            dimension_semantics=("arbitrary", "arbitrary"), vmem_limit_bytes=VMEM_LIMIT_BYTES),
        name="mixer",
    )(x, g_mix, w_in, conv_w, g_v, w_s, bias, g_oc, g_og, w_out)


def _attn_route_kernel(x_ref, kt_ref, v_ref, gx_ref, wq_ref, wo_ref, gf_ref, wrt_ref, brt_ref,
                       x2_ref, slab_ref, counts_ref, carry_ref):
    ts = x_ref.shape[1]
    d = x_ref.shape[2]
    hd = d // XA_HEADS
    nsub = ts // SUB_TILE
    first = (pl.program_id(0) == 0) & (pl.program_id(1) == 0)

    @pl.when(first)
    def _():
        carry_ref[...] = jnp.zeros_like(carry_ref)

    def dot(a, b):
        return jnp.dot(a, b, preferred_element_type=jnp.float32)

    subs = range(nsub)
    x1 = [x_ref[0, i * SUB_TILE:(i + 1) * SUB_TILE, :] for i in subs]
    h2 = [_rms(x, gx_ref[...]).astype(jnp.bfloat16) for x in x1]
    q = [dot(h, wq_ref[...]).astype(jnp.bfloat16) for h in h2]
    heads = [[] for _ in subs]
    for a in range(XA_HEADS):
        sc = [dot(q[i][:, a * hd:(a + 1) * hd], kt_ref[0, a * hd:(a + 1) * hd, :]) * (hd ** -0.5)
              for i in subs]
        p = [jnp.exp(s_ - jnp.max(s_, axis=-1, keepdims=True)) for s_ in sc]
        l = [jnp.sum(p_, axis=-1, keepdims=True) for p_ in p]
        o = [dot(p[i].astype(jnp.bfloat16), v_ref[0, :, a * hd:(a + 1) * hd]) for i in subs]
        for i in subs:
            heads[i].append((o[i] / l[i]).astype(jnp.bfloat16))
    x2 = [x1[i] + dot(jnp.concatenate(heads[i], axis=1), wo_ref[...]) for i in subs]
    for i in subs:
        x2_ref[0, i * SUB_TILE:(i + 1) * SUB_TILE, :] = x2[i]

    h3 = [_rms(x, gf_ref[...]).astype(jnp.bfloat16) for x in x2]
    lgs = [lax.dot_general(wrt_ref[...], h, (((1,), (1,)), ((), ())),
                           preferred_element_type=jnp.float32) + brt_ref[...] for h in h3]
    sub = lax.broadcasted_iota(jnp.int32, (EXPERTS_PER_GROUP, SUB_TILE), 0).astype(jnp.float32)
    big = float(EXPERTS_PER_GROUP)
    is_g = sub < N_GROUPS

    def classify(lg):
        glog = lg[0:EXPERTS_PER_GROUP, :]
        gmax = jnp.max(jnp.where(is_g, glog, _NEG), axis=0, keepdims=True)
        gidx = jnp.min(jnp.where(is_g & (glog == gmax), sub, big), axis=0, keepdims=True)
        el = lg[ROUTE_EXPERT_ROW0:ROUTE_EXPERT_ROW0 + EXPERTS_PER_GROUP, :]
        for grp in range(1, N_GROUPS):
            r0 = ROUTE_EXPERT_ROW0 + grp * EXPERTS_PER_GROUP
            el = jnp.where(gidx == grp, lg[r0:r0 + EXPERTS_PER_GROUP, :], el)
        t1 = jnp.max(el, axis=0, keepdims=True)
        i1 = jnp.min(jnp.where(el == t1, sub, big), axis=0, keepdims=True)
        rest = sub != i1
        t2 = jnp.max(jnp.where(rest, el, _NEG), axis=0, keepdims=True)
        i2 = jnp.min(jnp.where(rest & (el == t2), sub, big), axis=0, keepdims=True)
        a = jnp.minimum(i1, i2)
        b = jnp.maximum(i1, i2)
        pair = a * (2 * EXPERTS_PER_GROUP - 1 - a) * 0.5 + (b - a - 1.0)
        return gidx * N_PAIRS + pair

    cls = [classify(lg) for lg in lgs]

    r = lax.broadcasted_iota(jnp.int32, (SUB_TILE, SUB_TILE), 0)
    c = lax.broadcasted_iota(jnp.int32, (SUB_TILE, SUB_TILE), 1)
    earlier = jnp.where(r < c, 1.0, 0.0).astype(jnp.bfloat16)
    crow = lax.broadcasted_iota(jnp.int32, (ROUTE_LANES, SUB_TILE), 0).astype(jnp.float32)
    oh = [crow == cl for cl in cls]
    ohf = [jnp.where(o_, 1.0, 0.0) for o_ in oh]
    inside = [dot(o_.astype(jnp.bfloat16), earlier) for o_ in ohf]
    seen = carry_ref[...]
    for i in subs:
        rank = jnp.sum(jnp.where(oh[i], inside[i] + seen, 0.0), axis=0, keepdims=True)
        slab_ref[:, i * SUB_TILE:(i + 1) * SUB_TILE] = jnp.where(
            sub == 0, cls[i], jnp.where(sub == 1, rank, 0.0))
        seen = seen + jnp.sum(ohf[i], axis=1, keepdims=True)
    carry_ref[...] = seen
    counts_ref[...] = seen


def _attn_route(x1, kt, v, g_x, w_q, w_o, g_f, w_rt, b_rt):
    b, s, d = x1.shape
    m = v.shape[1]
    ts = min(ATTN_TILE, s)
    nt = s // ts
    const2 = lambda i, j: (0, 0)
    return pl.pallas_call(
        _attn_route_kernel,
        grid=(b, nt),
        in_specs=[
            pl.BlockSpec((1, ts, d), lambda i, j: (i, j, 0)),
            pl.BlockSpec((1, d, m), lambda i, j: (i, 0, 0)),
            pl.BlockSpec((1, m, d), lambda i, j: (i, 0, 0)),
            pl.BlockSpec(g_x.shape, const2),
            pl.BlockSpec(w_q.shape, const2),
            pl.BlockSpec(w_o.shape, const2),
            pl.BlockSpec(g_f.shape, const2),
            pl.BlockSpec(w_rt.shape, const2),
            pl.BlockSpec(b_rt.shape, const2),
        ],
        out_specs=[
            pl.BlockSpec((1, ts, d), lambda i, j: (i, j, 0)),
            pl.BlockSpec((SUBLANES, ts), lambda i, j: (0, i * nt + j)),
            pl.BlockSpec((ROUTE_LANES, 1), const2),
        ],
        out_shape=[
            jax.ShapeDtypeStruct((b, s, d), jnp.float32),
            jax.ShapeDtypeStruct((SUBLANES, b * s), jnp.float32),
            jax.ShapeDtypeStruct((ROUTE_LANES, 1), jnp.float32),
        ],
        scratch_shapes=[pltpu.VMEM((ROUTE_LANES, 1), jnp.float32)],
        compiler_params=pltpu.CompilerParams(
            dimension_semantics=("arbitrary", "arbitrary"), vmem_limit_bytes=VMEM_LIMIT_BYTES),
        name="attn_route",
    )(x1, kt, v, g_x, w_q, w_o, g_f, w_rt, b_rt)


def _dest_kernel(slab_ref, starts_ref, o_ref):
    slab = slab_ref[...]
    tt = slab.shape[1]
    crow = lax.broadcasted_iota(jnp.int32, (ROUTE_LANES, tt), 0).astype(jnp.float32)
    start = jnp.sum(jnp.where(crow == slab[0:1, :], starts_ref[...], 0.0), axis=0, keepdims=True)
    sub = lax.broadcasted_iota(jnp.int32, slab.shape, 0)
    o_ref[...] = jnp.where(sub == 0, start + slab[1:2, :], 0.0).astype(jnp.int32)


def _dest_rows(slab, starts_col):
    t = slab.shape[1]
    tt = min(DEST_TILE, t)
    return pl.pallas_call(
        _dest_kernel,
        grid=(t // tt,),
        in_specs=[pl.BlockSpec((SUBLANES, tt), lambda i: (0, i)),
                  pl.BlockSpec((ROUTE_LANES, 1), lambda i: (0, 0))],
        out_specs=pl.BlockSpec((SUBLANES, tt), lambda i: (0, i)),
        out_shape=jax.ShapeDtypeStruct((SUBLANES, t), jnp.int32),
        compiler_params=pltpu.CompilerParams(dimension_semantics=("arbitrary",)),
        name="dest_rows",
    )(slab, starts_col)


def _sc_split(t):
    info = plsc.get_sparse_core_info()
    workers = info.num_cores * info.num_subcores
    per_w, rem = divmod(t, workers)
    nchunk, rem2 = divmod(per_w, SC_CHUNK_ROWS)
    assert rem == 0 and rem2 == 0 and nchunk % SC_BUFFERS == 0, (t, workers, SC_CHUNK_ROWS)
    return info.num_cores, per_w, nchunk


def _sc_ring(nchunk, load, store):
    ahead = SC_BUFFERS - 1
    for k in range(ahead):
        for cp in load(k, k):
            cp.start()

    @pl.loop(0, nchunk, step=SC_BUFFERS)
    def _(c):
        for b in range(SC_BUFFERS):
            cc = c + b
            refill = (b + ahead) % SC_BUFFERS

            @pl.when(cc >= 1)
            def _():
                for cp in store(cc - 1, refill):
                    cp.wait()

            @pl.when(cc + ahead < nchunk)
            def _():
                for cp in load(cc + ahead, refill):
                    cp.start()

            for cp in load(cc, b):
                cp.wait()
            for cp in store(cc, b):
                cp.start()

    for cp in store(nchunk - 1, (nchunk - 1) % SC_BUFFERS):
        cp.wait()


def _dispatch(x2, dest, cap):
    t, d = x2.shape
    nc, per_w, nchunk = _sc_split(t)
    ch = SC_CHUNK_ROWS
    mesh = plsc.VectorSubcoreMesh(core_axis_name="c", subcore_axis_name="s")
    dma = pltpu.SemaphoreType.DMA

    @functools.partial(
        pl.kernel, mesh=mesh,
        out_type=jax.ShapeDtypeStruct((cap, d), jnp.float32),
        scratch_types=([pltpu.VMEM((ch,), jnp.int32)] * SC_BUFFERS
                       + [pltpu.VMEM((ch, d), jnp.float32)] * SC_BUFFERS + [dma] * (3 * SC_BUFFERS)),
    )
    def dispatch_sc(x_hbm, dest_hbm, xs_hbm, *scratch):
        n = SC_BUFFERS
        idx, rows, isem, rsem, ssem = (scratch[k * n:(k + 1) * n] for k in range(5))
        wid = lax.axis_index("s") * nc + lax.axis_index("c")
        base = wid * per_w

        def load(c, b):
            src = pl.ds(base + c * ch, ch)
            return (pltpu.make_async_copy(dest_hbm.at[src], idx[b], isem[b]),
                    pltpu.make_async_copy(x_hbm.at[src], rows[b], rsem[b]))

        def store(c, b):
            return (pltpu.make_async_copy(rows[b], xs_hbm.at[idx[b]], ssem[b]),)

        _sc_ring(nchunk, load, store)

    return dispatch_sc(x2, dest)


def _unsort(ys, dest, t):
    d = ys.shape[1]
    nc, per_w, nchunk = _sc_split(t)
    ch = SC_CHUNK_ROWS
    mesh = plsc.VectorSubcoreMesh(core_axis_name="c", subcore_axis_name="s")
    dma = pltpu.SemaphoreType.DMA

    @functools.partial(
        pl.kernel, mesh=mesh,
        out_type=jax.ShapeDtypeStruct((t, d), jnp.float32),
        scratch_types=([pltpu.VMEM((per_w,), jnp.int32)]
                       + [pltpu.VMEM((ch, d), jnp.float32)] * SC_BUFFERS + [dma] * (2 * SC_BUFFERS)),
    )
    def unsort_sc(ys_hbm, dest_hbm, out_hbm, idx_v, *scratch):
        n = SC_BUFFERS
        rows, gsem, wsem = (scratch[k * n:(k + 1) * n] for k in range(3))
        wid = lax.axis_index("s") * nc + lax.axis_index("c")
        base = wid * per_w
        pltpu.sync_copy(dest_hbm.at[pl.ds(base, per_w)], idx_v)

        def load(c, b):
            return (pltpu.make_async_copy(ys_hbm.at[idx_v.at[pl.ds(c * ch, ch)]], rows[b], gsem[b]),)

        def store(c, b):
            return (pltpu.make_async_copy(rows[b], out_hbm.at[pl.ds(base + c * ch, ch)], wsem[b]),)

        _sc_ring(nchunk, load, store)

    return unsort_sc(ys, dest)


def _expert_kernel(grp_ref, ea_ref, eb_ref, nvalid_ref, nact_ref,
                   xs_ref, gf_ref, wr_ref, br_ref, gfin_ref, wg_hbm, wu_hbm, wd_hbm,
                   o_ref, wgb, wub, wdb, sg, su, sd, sem, cnt_ref):
    s = pl.program_id(0)
    blocks = tuple(BLOCKS_PER_STEP * s + i for i in range(BLOCKS_PER_STEP))
    g = grp_ref[blocks[0]]

    def stage(e):
        slot = e % WEIGHT_STAGES
        return (pltpu.make_async_copy(wg_hbm.at[e], sg.at[slot], sem.at[slot, 0]),
                pltpu.make_async_copy(wu_hbm.at[e], su.at[slot], sem.at[slot, 1]),
                pltpu.make_async_copy(wd_hbm.at[e], sd.at[slot], sem.at[slot, 2]))

    def start_next():
        @pl.when(cnt_ref[0] < N_EXPERTS)
        def _():
            for cp in stage(cnt_ref[0]):
                cp.start()
            cnt_ref[0] = cnt_ref[0] + 1

    @pl.when(s == 0)
    def _():
        cnt_ref[0] = 0
        cnt_ref[1] = 0
        for _ in range(WEIGHT_STAGES):
            start_next()

    active = blocks[0] < nact_ref[0]

    @pl.when(active)
    def _():
        need = g * EXPERTS_PER_GROUP + functools.reduce(
            jnp.maximum, [eb_ref[j] for j in blocks])

        def load(e, carry):
            for cp in stage(e):
                cp.wait()
            slot = e % WEIGHT_STAGES
            k = e % EXPERTS_PER_GROUP
            wgb[k] = sg[slot].astype(jnp.bfloat16)
            wub[k] = su[slot].astype(jnp.bfloat16)
            wdb[k] = sd[slot].astype(jnp.bfloat16)
            cnt_ref[1] = e + 1
            start_next()
            return carry

        lax.fori_loop(cnt_ref[1], need + 1, load, 0)

        lane = lax.broadcasted_iota(jnp.int32, (ROW_BLOCK, ROUTE_LANES), 1)
        is_g = lane < N_GROUPS
        lo = EXPERT_LANE0 + EXPERTS_PER_GROUP * g
        nblk = len(blocks)
        rowid = lax.broadcasted_iota(jnp.int32, (ROW_BLOCK, 1), 0)
        xs = [jnp.where(rowid < nvalid_ref[blocks[i]],
                        xs_ref[i * ROW_BLOCK:(i + 1) * ROW_BLOCK, :], 0.0) for i in range(nblk)]
        h3 = [_rms(x, gf_ref[...]).astype(jnp.bfloat16) for x in xs]
        lg = [jnp.dot(h, wr_ref[...], preferred_element_type=jnp.float32) + br_ref[...]
              for h in h3]

        def gates(lgi, j):
            def pick(idx):
                return jnp.sum(jnp.where(lane == idx, lgi, 0.0), axis=-1, keepdims=True)

            gmax = jnp.max(jnp.where(is_g, lgi, _NEG), axis=-1, keepdims=True)
            den = jnp.sum(jnp.where(is_g, jnp.exp(jnp.where(is_g, lgi, _NEG) - gmax), 0.0),
                          axis=-1, keepdims=True)
            grp_p = jnp.exp(pick(g) - gmax) / den
            la = pick(lo + ea_ref[j])
            lb = pick(lo + eb_ref[j])
            m = jnp.maximum(la, lb)
            pa = jnp.exp(la - m)
            pb = jnp.exp(lb - m)
            return grp_p * pa / (pa + pb), grp_p * pb / (pa + pb)

        gate = [gates(lg[i], blocks[i]) for i in range(nblk)]
        ys = list(xs)
        for which, e_ref in enumerate((ea_ref, eb_ref)):
            ks = [e_ref[j] for j in blocks]
            gg = [jnp.dot(h3[i], wgb[ks[i]], preferred_element_type=jnp.float32)
                  for i in range(nblk)]
            uu = [jnp.dot(h3[i], wub[ks[i]], preferred_element_type=jnp.float32)
                  for i in range(nblk)]
            act = [(gg[i] * (0.5 + 0.5 * jnp.tanh(0.5 * gg[i])) * uu[i]
                    * gate[i][which]).astype(jnp.bfloat16) for i in range(nblk)]
            ys = [ys[i] + jnp.dot(act[i], wdb[ks[i]], preferred_element_type=jnp.float32)
                  for i in range(nblk)]
        for i in range(nblk):
            o_ref[i * ROW_BLOCK:(i + 1) * ROW_BLOCK, :] = _rms(ys[i], gfin_ref[...])

    @pl.when(s == pl.num_programs(0) - 1)
    def _():
        def drain(e, carry):
            for cp in stage(e):
                cp.wait()
            return carry

        lax.fori_loop(cnt_ref[1], cnt_ref[0], drain, 0)


def _experts(xs, blk_grp, blk_a, blk_b, nvalid, nact, g_ffn, w_r, b_r, g_final, w_gate, w_up, w_down):
    cap, d = xs.shape
    de = w_gate.shape[2]
    step_rows = BLOCKS_PER_STEP * ROW_BLOCK
    steps = cap // step_rows
    pre = lambda f: (lambda s, gr, ea, eb, nv, na: f(s, na))
    const2 = pre(lambda s, na: (0, 0))
    last_step = lambda na: (na[0] - 1) // BLOCKS_PER_STEP
    hbm = pl.BlockSpec(memory_space=pl.ANY)
    active_step = pre(lambda s, na: (jnp.minimum(s, last_step(na)), 0))
    grid_spec = pltpu.PrefetchScalarGridSpec(
        num_scalar_prefetch=5,
        grid=(steps,),
        in_specs=[
            pl.BlockSpec((step_rows, d), active_step),
            pl.BlockSpec(g_ffn.shape, const2),
            pl.BlockSpec(w_r.shape, const2),
            pl.BlockSpec(b_r.shape, const2),
            pl.BlockSpec(g_final.shape, const2),
            hbm, hbm, hbm,
        ],
        out_specs=pl.BlockSpec((step_rows, d), active_step),
        scratch_shapes=[
            pltpu.VMEM((EXPERTS_PER_GROUP, d, de), jnp.bfloat16),
            pltpu.VMEM((EXPERTS_PER_GROUP, d, de), jnp.bfloat16),
            pltpu.VMEM((EXPERTS_PER_GROUP, de, d), jnp.bfloat16),
            pltpu.VMEM((WEIGHT_STAGES, d, de), jnp.float32),
            pltpu.VMEM((WEIGHT_STAGES, d, de), jnp.float32),
            pltpu.VMEM((WEIGHT_STAGES, de, d), jnp.float32),
            pltpu.SemaphoreType.DMA((WEIGHT_STAGES, 3)),
            pltpu.SMEM((2,), jnp.int32),
        ],
    )
    return pl.pallas_call(
        _expert_kernel,
        grid_spec=grid_spec,
        out_shape=jax.ShapeDtypeStruct((cap, d), jnp.float32),
        compiler_params=pltpu.CompilerParams(
            dimension_semantics=("arbitrary",), vmem_limit_bytes=EXPERT_VMEM_LIMIT_BYTES),
        name="experts",
    )(blk_grp, blk_a, blk_b, nvalid, nact, xs, g_ffn, w_r, b_r, g_final, w_gate, w_up, w_down)


def _moe_final(x2, slab, counts_col, g_ffn, w_r, b_r, g_final, w_gate, w_up, w_down):
    t, d = x2.shape
    nb = t // ROW_BLOCK + N_CLASSES + N_GROUPS * (BLOCKS_PER_STEP - 1)
    assert nb % BLOCKS_PER_STEP == 0
    counts = counts_col[:N_CLASSES, 0].astype(jnp.int32)
    nblk = (counts + ROW_BLOCK - 1) // ROW_BLOCK
    grp_blocks = jnp.sum(nblk.reshape(N_GROUPS, N_PAIRS), axis=1)
    nblk = nblk.reshape(N_GROUPS, N_PAIRS).at[:, N_PAIRS - 1].add(
        (-grp_blocks) % BLOCKS_PER_STEP).reshape(N_CLASSES)
    blk_end = jnp.cumsum(nblk)
    blk_start = blk_end - nblk
    nact = blk_end[-1]
    j = jnp.arange(nb, dtype=jnp.int32)
    blk_class = jnp.minimum(
        jnp.sum((blk_end[None, :] <= j[:, None]).astype(jnp.int32), axis=1), N_CLASSES - 1)
    blk_class = jnp.where(j < nact, blk_class, blk_class[jnp.maximum(nact - 1, 0)])
    nvalid = jnp.where(j < nact, jnp.clip(
        counts[blk_class] - (j - blk_start[blk_class]) * ROW_BLOCK, 0, ROW_BLOCK), 0)
    pair_a = jnp.array([p[0] for p in PAIRS], jnp.int32)
    pair_b = jnp.array([p[1] for p in PAIRS], jnp.int32)
    blk_grp = blk_class // N_PAIRS
    blk_a = pair_a[blk_class % N_PAIRS]
    blk_b = pair_b[blk_class % N_PAIRS]
    starts_col = jnp.zeros((ROUTE_LANES, 1), jnp.float32).at[:N_CLASSES, 0].set(
        (blk_start * ROW_BLOCK).astype(jnp.float32))

    dest = _dest_rows(slab, starts_col)[0]
    xs = _dispatch(x2, dest, nb * ROW_BLOCK)
    ys = _experts(xs, blk_grp, blk_a, blk_b, nvalid, nact[None], g_ffn, w_r, b_r, g_final,
                  w_gate, w_up, w_down)
    return _unsort(ys, dest, t)


def kernel(x, mem, g_mix, w_in, conv_w, g_v, w_s, b_s, g_out_conv, g_out_gmlp, w_out, g_xattn,
           g_mem, w_q, w_k, w_v, w_o, g_ffn, w_grp, b_grp, w_rt, b_rt, w_gate, w_up, w_down,
           g_final):
    b, s, d = x.shape
    assert g_mix.shape[0] == 1, "the final norm is fused into the single layer's expert kernel"
    assert N_CLASSES <= ROUTE_LANES
    bf = jnp.bfloat16
    kt, v = _kv_proj(mem, g_mem[0][None], w_k[0].astype(bf), w_v[0].astype(bf))

    bias = jnp.repeat(b_s[0].T, GMLP_HEAD_DIM, axis=1)
    x1 = _mixer(x, g_mix[0][None], w_in[0].astype(bf), conv_w[0], g_v[0][None], w_s[0], bias,
                g_out_conv[0][None], g_out_gmlp[0][None], w_out[0].astype(bf))

    pad = ROUTE_LANES - N_GROUPS - N_EXPERTS
    w_r = jnp.concatenate([w_grp[0], w_rt[0], jnp.zeros((d, pad), jnp.float32)], axis=1).astype(bf)
    b_r = jnp.concatenate([b_grp[0], b_rt[0], jnp.zeros((pad,), jnp.float32)])[None]
    gap = ROUTE_EXPERT_ROW0 - N_GROUPS
    tail = ROUTE_LANES - ROUTE_EXPERT_ROW0 - N_EXPERTS
    w_r_t = jnp.concatenate([w_grp[0].T, jnp.zeros((gap, d), jnp.float32), w_rt[0].T,
                             jnp.zeros((tail, d), jnp.float32)], axis=0).astype(bf)
    b_r_t = jnp.concatenate([b_grp[0], jnp.zeros((gap,), jnp.float32), b_rt[0],
                             jnp.zeros((tail,), jnp.float32)])[:, None]
    x2, slab, counts_col = _attn_route(x1, kt, v, g_xattn[0][None], w_q[0].astype(bf),
                                       w_o[0].astype(bf), g_ffn[0][None], w_r_t, b_r_t)
    out = _moe_final(x2.reshape(b * s, d), slab, counts_col, g_ffn[0][None], w_r, b_r,
                     g_final[None], w_gate[0], w_up[0], w_down[0])
    return out.reshape(b, s, d)
```

```python
import functools

import jax
import jax.numpy as jnp
from jax import lax
from jax.experimental import pallas as pl
from jax.experimental.pallas import tpu as pltpu
from jax.experimental.pallas import tpu_sc as plsc

EPS = 1e-6
CONV_GROUP_WIDTH = 512
GMLP_HEADS = 8
GMLP_HEAD_DIM = 64
CHUNK = 128
XA_HEADS = 4
N_GROUPS = 4
EXPERTS_PER_GROUP = 8
N_EXPERTS = N_GROUPS * EXPERTS_PER_GROUP
ROUTE_LANES = 128
EXPERT_LANE0 = N_GROUPS
ROUTE_EXPERT_ROW0 = 8

N_PAIRS = EXPERTS_PER_GROUP * (EXPERTS_PER_GROUP - 1) // 2
N_CLASSES = N_GROUPS * N_PAIRS
PAIRS = [(a, b) for a in range(EXPERTS_PER_GROUP) for b in range(a + 1, EXPERTS_PER_GROUP)]

MIXER_TILE = 1024
ATTN_TILE = 2048
SUB_TILE = 256
DEST_TILE = 8192
ROW_BLOCK = 128
BLOCKS_PER_STEP = 4
WEIGHT_STAGES = 3
SC_CHUNK_ROWS = 16
SC_BUFFERS = 4
VMEM_LIMIT_BYTES = 56 * 1024 * 1024
EXPERT_VMEM_LIMIT_BYTES = 60 * 1024 * 1024

SUBLANES = 8

_NEG = -1e30


def _rms(x, g):
    return x * lax.rsqrt(jnp.mean(x * x, axis=-1, keepdims=True) + EPS) * g


def _gelu_tanh(x):
    return 0.5 * x * (1.0 + jnp.tanh(0.7978845608028654 * (x + 0.044715 * (x * x * x))))


def _kv_kernel(mem_ref, g_ref, wk_ref, wv_ref, kt_ref, v_ref):
    m = _rms(mem_ref[0], g_ref[...]).astype(jnp.bfloat16)
    k = jnp.dot(m, wk_ref[...], preferred_element_type=jnp.float32)
    v = jnp.dot(m, wv_ref[...], preferred_element_type=jnp.float32)
    kt_ref[0] = k.T.astype(jnp.bfloat16)
    v_ref[0] = v.astype(jnp.bfloat16)


def _kv_proj(mem, g_mem, w_k, w_v):
    b, m, d = mem.shape
    const = lambda i: (0, 0)
    return pl.pallas_call(
        _kv_kernel,
        grid=(b,),
        in_specs=[
            pl.BlockSpec((1, m, d), lambda i: (i, 0, 0)),
            pl.BlockSpec((1, d), const),
            pl.BlockSpec((d, d), const),
            pl.BlockSpec((d, d), const),
        ],
        out_specs=[
            pl.BlockSpec((1, d, m), lambda i: (i, 0, 0)),
            pl.BlockSpec((1, m, d), lambda i: (i, 0, 0)),
        ],
        out_shape=[
            jax.ShapeDtypeStruct((b, d, m), jnp.bfloat16),
            jax.ShapeDtypeStruct((b, m, d), jnp.bfloat16),
        ],
        compiler_params=pltpu.CompilerParams(
            dimension_semantics=("arbitrary",), vmem_limit_bytes=VMEM_LIMIT_BYTES),
        name="kv_proj",
    )(mem, g_mem, w_k, w_v)


def _mixer_kernel(x_ref, gmix_ref, win_ref, convw_ref, gv_ref, ws_ref, bias_ref,
                  goc_ref, gog_ref, wout_ref, o_ref, zbuf_ref):
    ts = x_ref.shape[1]
    w = CONV_GROUP_WIDTH
    hw = 2 * GMLP_HEAD_DIM
    nsub = ts // SUB_TILE

    def dot(a, b):
        return jnp.dot(a, b, preferred_element_type=jnp.float32)

    @pl.when(pl.program_id(1) == 0)
    def _():
        zbuf_ref[0:8, :] = jnp.zeros((8, w), jnp.float32)

    @pl.when(pl.program_id(1) != 0)
    def _():
        zbuf_ref[0:8, :] = zbuf_ref[ts:ts + 8, :]

    low = lax.broadcasted_iota(jnp.int32, (SUB_TILE, hw), 1) < GMLP_HEAD_DIM
    row = lax.broadcasted_iota(jnp.int32, (CHUNK, CHUNK), 0)
    colid = lax.broadcasted_iota(jnp.int32, (CHUNK, CHUNK), 1)
    causal = row >= colid
    ws = [jnp.where(causal, ws_ref[hd], 0.0).astype(jnp.bfloat16) for hd in range(GMLP_HEADS)]
    npair = SUB_TILE // CHUNK // 2
    lo = lax.broadcasted_iota(jnp.int32, (CHUNK, hw), 1) < GMLP_HEAD_DIM
    swap = lambda a: pltpu.roll(a, GMLP_HEAD_DIM, axis=1)

    def project(i):
        r0 = i * SUB_TILE
        xt = x_ref[0, r0:r0 + SUB_TILE, :]
        h = _rms(xt, gmix_ref[...]).astype(jnp.bfloat16)
        p = [dot(h, win_ref[:, k * w:(k + 1) * w]) for k in range(5)]
        z = p[1] * p[2]
        zbuf_ref[8 + r0:8 + r0 + SUB_TILE, :] = z
        return dict(xt=xt, gate_b=p[0], z=z, u=p[3], v=p[4], r0=r0)

    def gate_and_norm(st):
        r0 = st["r0"]
        zc = (convw_ref[0:1, :] * zbuf_ref[6 + r0:6 + r0 + SUB_TILE, :]
              + convw_ref[1:2, :] * zbuf_ref[7 + r0:7 + r0 + SUB_TILE, :]
              + convw_ref[2:3, :] * st["z"])
        st["ya"] = _rms(st["gate_b"] * zc, goc_ref[...]).astype(jnp.bfloat16)
        st["u"] = _gelu_tanh(st["u"])
        v = _gelu_tanh(st["v"])
        v2 = v * v
        ss_cols = []
        for k in range(GMLP_HEADS // 2):
            col = v2[:, hw * k:hw * (k + 1)]
            ss_cols.append(jnp.where(low,
                                     jnp.sum(jnp.where(low, col, 0.0), axis=1, keepdims=True),
                                     jnp.sum(jnp.where(low, 0.0, col), axis=1, keepdims=True)))
        ss = jnp.concatenate(ss_cols, axis=1)
        st["vn"] = v * lax.rsqrt(ss * (1.0 / GMLP_HEAD_DIM) + EPS) * gv_ref[...]

    def mix_positions(st):
        vn = st["vn"]
        s_cols = [[None] * (GMLP_HEADS // 2) for _ in range(2 * npair)]
        for hp in range(GMLP_HEADS // 2):
            cols = [vn[c * CHUNK:(c + 1) * CHUNK, hw * hp:hw * (hp + 1)] for c in range(2 * npair)]
            swapped = [swap(a) for a in cols]
            rhs_even = jnp.concatenate(
                [jnp.where(lo, cols[2 * p], swapped[2 * p + 1]) for p in range(npair)], axis=1)
            rhs_odd = jnp.concatenate(
                [jnp.where(lo, swapped[2 * p], cols[2 * p + 1]) for p in range(npair)], axis=1)
            out_e = dot(ws[2 * hp], rhs_even.astype(jnp.bfloat16))
            out_o = dot(ws[2 * hp + 1], rhs_odd.astype(jnp.bfloat16))
            for p in range(npair):
                e = out_e[:, hw * p:hw * (p + 1)]
                o = out_o[:, hw * p:hw * (p + 1)]
                s_cols[2 * p][hp] = jnp.where(lo, e, swap(o))
                s_cols[2 * p + 1][hp] = jnp.where(lo, swap(e), o)
        st["s"] = jnp.concatenate(
            [jnp.concatenate(c, axis=1) + bias_ref[...] for c in s_cols], axis=0)

    def output(st):
        r0 = st["r0"]
        yb = _rms(st["u"] * st["s"], gog_ref[...]).astype(jnp.bfloat16)
        o_ref[0, r0:r0 + SUB_TILE, :] = (st["xt"] + dot(st["ya"], wout_ref[0:w, :])
                                          + dot(yb, wout_ref[w:2 * w, :]))

    phases = (gate_and_norm, mix_positions, output)
    states = []
    for step in range(nsub + len(phases)):
        if step < nsub:
            states.append(project(step))
        for k, phase in enumerate(phases):
            i = step - 1 - k
            if 0 <= i < nsub:
                phase(states[i])


def _mixer(x, g_mix, w_in, conv_w, g_v, w_s, bias, g_oc, g_og, w_out):
    b, s, d = x.shape
    ts = min(MIXER_TILE, s)
    const2 = lambda i, j: (0, 0)
    const3 = lambda i, j: (0, 0, 0)
    return pl.pallas_call(
        _mixer_kernel,
        grid=(b, s // ts),
        in_specs=[
            pl.BlockSpec((1, ts, d), lambda i, j: (i, j, 0)),
            pl.BlockSpec(g_mix.shape, const2),
            pl.BlockSpec(w_in.shape, const2),
            pl.BlockSpec(conv_w.shape, const2),
            pl.BlockSpec(g_v.shape, const2),
            pl.BlockSpec(w_s.shape, const3),
            pl.BlockSpec(bias.shape, const2),
            pl.BlockSpec(g_oc.shape, const2),
            pl.BlockSpec(g_og.shape, const2),
            pl.BlockSpec(w_out.shape, const2),
        ],
        out_specs=pl.BlockSpec((1, ts, d), lambda i, j: (i, j, 0)),
        out_shape=jax.ShapeDtypeStruct((b, s, d), jnp.float32),
        scratch_shapes=[pltpu.VMEM((ts + 8, CONV_GROUP_WIDTH), jnp.float32)],
        compiler_params=pltpu.CompilerParams(
            dimension_semantics=("arbitrary", "arbitrary"), vmem_limit_bytes=VMEM_LIMIT_BYTES),
        name="mixer",
    )(x, g_mix, w_in, conv_w, g_v, w_s, bias, g_oc, g_og, w_out)


def _attn_route_kernel(x_ref, kt_ref, v_ref, gx_ref, wq_ref, wo_ref, gf_ref, wrt_ref, brt_ref,
                       x2_ref, slab_ref, counts_ref, carry_ref):
    ts = x_ref.shape[1]
    d = x_ref.shape[2]
    hd = d // XA_HEADS
    nsub = ts // SUB_TILE
    first = (pl.program_id(0) == 0) & (pl.program_id(1) == 0)

    @pl.when(first)
    def _():
        carry_ref[...] = jnp.zeros_like(carry_ref)

    def dot(a, b):
        return jnp.dot(a, b, preferred_element_type=jnp.float32)

    subs = range(nsub)
    x1 = [x_ref[0, i * SUB_TILE:(i + 1) * SUB_TILE, :] for i in subs]
    h2 = [_rms(x, gx_ref[...]).astype(jnp.bfloat16) for x in x1]
    q = [dot(h, wq_ref[...]).astype(jnp.bfloat16) for h in h2]
    heads = [[] for _ in subs]
    for a in range(XA_HEADS):
        sc = [dot(q[i][:, a * hd:(a + 1) * hd], kt_ref[0, a * hd:(a + 1) * hd, :]) * (hd ** -0.5)
              for i in subs]
        p = [jnp.exp(s_ - jnp.max(s_, axis=-1, keepdims=True)) for s_ in sc]
        l = [jnp.sum(p_, axis=-1, keepdims=True) for p_ in p]
        o = [dot(p[i].astype(jnp.bfloat16), v_ref[0, :, a * hd:(a + 1) * hd]) for i in subs]
        for i in subs:
            heads[i].append((o[i] / l[i]).astype(jnp.bfloat16))
    x2 = [x1[i] + dot(jnp.concatenate(heads[i], axis=1), wo_ref[...]) for i in subs]
    for i in subs:
        x2_ref[0, i * SUB_TILE:(i + 1) * SUB_TILE, :] = x2[i]

    h3 = [_rms(x, gf_ref[...]).astype(jnp.bfloat16) for x in x2]
    lgs = [lax.dot_general(wrt_ref[...], h, (((1,), (1,)), ((), ())),
                           preferred_element_type=jnp.float32) + brt_ref[...] for h in h3]
    sub = lax.broadcasted_iota(jnp.int32, (EXPERTS_PER_GROUP, SUB_TILE), 0).astype(jnp.float32)
    big = float(EXPERTS_PER_GROUP)
    is_g = sub < N_GROUPS

    def classify(lg):
        glog = lg[0:EXPERTS_PER_GROUP, :]
        gmax = jnp.max(jnp.where(is_g, glog, _NEG), axis=0, keepdims=True)
        gidx = jnp.min(jnp.where(is_g & (glog == gmax), sub, big), axis=0, keepdims=True)
        el = lg[ROUTE_EXPERT_ROW0:ROUTE_EXPERT_ROW0 + EXPERTS_PER_GROUP, :]
        for grp in range(1, N_GROUPS):
            r0 = ROUTE_EXPERT_ROW0 + grp * EXPERTS_PER_GROUP
            el = jnp.where(gidx == grp, lg[r0:r0 + EXPERTS_PER_GROUP, :], el)
        t1 = jnp.max(el, axis=0, keepdims=True)
        i1 = jnp.min(jnp.where(el == t1, sub, big), axis=0, keepdims=True)
        rest = sub != i1
        t2 = jnp.max(jnp.where(rest, el, _NEG), axis=0, keepdims=True)
        i2 = jnp.min(jnp.where(rest & (el == t2), sub, big), axis=0, keepdims=True)
        a = jnp.minimum(i1, i2)
        b = jnp.maximum(i1, i2)
        pair = a * (2 * EXPERTS_PER_GROUP - 1 - a) * 0.5 + (b - a - 1.0)
        return gidx * N_PAIRS + pair

    cls = [classify(lg) for lg in lgs]

    r = lax.broadcasted_iota(jnp.int32, (SUB_TILE, SUB_TILE), 0)
    c = lax.broadcasted_iota(jnp.int32, (SUB_TILE, SUB_TILE), 1)
    earlier = jnp.where(r < c, 1.0, 0.0).astype(jnp.bfloat16)
    crow = lax.broadcasted_iota(jnp.int32, (ROUTE_LANES, SUB_TILE), 0).astype(jnp.float32)
    oh = [crow == cl for cl in cls]
    ohf = [jnp.where(o_, 1.0, 0.0) for o_ in oh]
    inside = [dot(o_.astype(jnp.bfloat16), earlier) for o_ in ohf]
    seen = carry_ref[...]
    for i in subs:
        rank = jnp.sum(jnp.where(oh[i], inside[i] + seen, 0.0), axis=0, keepdims=True)
        slab_ref[:, i * SUB_TILE:(i + 1) * SUB_TILE] = jnp.where(
            sub == 0, cls[i], jnp.where(sub == 1, rank, 0.0))
        seen = seen + jnp.sum(ohf[i], axis=1, keepdims=True)
    carry_ref[...] = seen
    counts_ref[...] = seen


def _attn_route(x1, kt, v, g_x, w_q, w_o, g_f, w_rt, b_rt):
    b, s, d = x1.shape
    m = v.shape[1]
    ts = min(ATTN_TILE, s)
    nt = s // ts
    const2 = lambda i, j: (0, 0)
    return pl.pallas_call(
        _attn_route_kernel,
        grid=(b, nt),
        in_specs=[
            pl.BlockSpec((1, ts, d), lambda i, j: (i, j, 0)),
            pl.BlockSpec((1, d, m), lambda i, j: (i, 0, 0)),
            pl.BlockSpec((1, m, d), lambda i, j: (i, 0, 0)),
            pl.BlockSpec(g_x.shape, const2),
            pl.BlockSpec(w_q.shape, const2),
            pl.BlockSpec(w_o.shape, const2),
            pl.BlockSpec(g_f.shape, const2),
            pl.BlockSpec(w_rt.shape, const2),
            pl.BlockSpec(b_rt.shape, const2),
        ],
        out_specs=[
            pl.BlockSpec((1, ts, d), lambda i, j: (i, j, 0)),
            pl.BlockSpec((SUBLANES, ts), lambda i, j: (0, i * nt + j)),
            pl.BlockSpec((ROUTE_LANES, 1), const2),
        ],
        out_shape=[
            jax.ShapeDtypeStruct((b, s, d), jnp.float32),
            jax.ShapeDtypeStruct((SUBLANES, b * s), jnp.float32),
            jax.ShapeDtypeStruct((ROUTE_LANES, 1), jnp.float32),
        ],
        scratch_shapes=[pltpu.VMEM((ROUTE_LANES, 1), jnp.float32)],
        compiler_params=pltpu.CompilerParams(
            dimension_semantics=("arbitrary", "arbitrary"), vmem_limit_bytes=VMEM_LIMIT_BYTES),
        name="attn_route",
    )(x1, kt, v, g_x, w_q, w_o, g_f, w_rt, b_rt)


def _dest_kernel(slab_ref, starts_ref, o_ref):
    slab = slab_ref[...]
    tt = slab.shape[1]
    crow = lax.broadcasted_iota(jnp.int32, (ROUTE_LANES, tt), 0).astype(jnp.float32)
    start = jnp.sum(jnp.where(crow == slab[0:1, :], starts_ref[...], 0.0), axis=0, keepdims=True)
    sub = lax.broadcasted_iota(jnp.int32, slab.shape, 0)
    o_ref[...] = jnp.where(sub == 0, start + slab[1:2, :], 0.0).astype(jnp.int32)


def _dest_rows(slab, starts_col):
    t = slab.shape[1]
    tt = min(DEST_TILE, t)
    return pl.pallas_call(
        _dest_kernel,
        grid=(t // tt,),
        in_specs=[pl.BlockSpec((SUBLANES, tt), lambda i: (0, i)),
                  pl.BlockSpec((ROUTE_LANES, 1), lambda i: (0, 0))],
        out_specs=pl.BlockSpec((SUBLANES, tt), lambda i: (0, i)),
        out_shape=jax.ShapeDtypeStruct((SUBLANES, t), jnp.int32),
        compiler_params=pltpu.CompilerParams(dimension_semantics=("arbitrary",)),
        name="dest_rows",
    )(slab, starts_col)


def _sc_split(t):
    info = plsc.get_sparse_core_info()
    workers = info.num_cores * info.num_subcores
    per_w, rem = divmod(t, workers)
    nchunk, rem2 = divmod(per_w, SC_CHUNK_ROWS)
    assert rem == 0 and rem2 == 0 and nchunk % SC_BUFFERS == 0, (t, workers, SC_CHUNK_ROWS)
    return info.num_cores, per_w, nchunk


def _sc_ring(nchunk, load, store, compute=None):
    ahead = SC_BUFFERS - 1
    for k in range(ahead):
        for cp in load(k, k):
            cp.start()

    @pl.loop(0, nchunk, step=SC_BUFFERS)
    def _(c):
        for b in range(SC_BUFFERS):
            cc = c + b
            refill = (b + ahead) % SC_BUFFERS

            @pl.when(cc >= 1)
            def _():
                for cp in store(cc - 1, refill):
                    cp.wait()

            @pl.when(cc + ahead < nchunk)
            def _():
                for cp in load(cc + ahead, refill):
                    cp.start()

            for cp in load(cc, b):
                cp.wait()
            if compute is not None:
                compute(cc, b)
            for cp in store(cc, b):
                cp.start()

    for cp in store(nchunk - 1, (nchunk - 1) % SC_BUFFERS):
        cp.wait()


def _dispatch(x2, slab, starts, cap):
    t, d = x2.shape
    nc, per_w, nchunk = _sc_split(t)
    ch = SC_CHUNK_ROWS
    mesh = plsc.VectorSubcoreMesh(core_axis_name="c", subcore_axis_name="s")
    dma = pltpu.SemaphoreType.DMA
    n = SC_BUFFERS

    @functools.partial(
        pl.kernel, mesh=mesh,
        out_type=(jax.ShapeDtypeStruct((cap, d), jnp.float32), jax.ShapeDtypeStruct((t,), jnp.int32)),
        compiler_params=pltpu.CompilerParams(needs_layout_passes=False),
        scratch_types=([pltpu.VMEM((ROUTE_LANES,), jnp.float32)]
                       + [pltpu.VMEM((ch,), jnp.float32)] * (2 * n) + [pltpu.VMEM((ch,), jnp.int32)] * n
                       + [pltpu.VMEM((ch, d), jnp.float32)] * n + [dma] * (5 * n)),
    )
    def dispatch_sc(x_hbm, slab_hbm, starts_hbm, xs_hbm, dest_hbm, starts_v, *scratch):
        cls, rank, idx, rows = (scratch[k * n:(k + 1) * n] for k in range(4))
        csem, ksem, rsem, ssem, dsem = (scratch[(4 + k) * n:(5 + k) * n] for k in range(5))
        wid = lax.axis_index("s") * nc + lax.axis_index("c")
        base = wid * per_w
        pltpu.sync_copy(starts_hbm, starts_v)

        def load(c, b):
            src = pl.ds(base + c * ch, ch)
            return (pltpu.make_async_copy(slab_hbm.at[0, src], cls[b], csem[b]),
                    pltpu.make_async_copy(slab_hbm.at[1, src], rank[b], ksem[b]),
                    pltpu.make_async_copy(x_hbm.at[src], rows[b], rsem[b]))

        def compute(c, b):
            start = plsc.load_gather(starts_v, [cls[b][...].astype(jnp.int32)])
            idx[b][...] = (start + rank[b][...]).astype(jnp.int32)

        def store(c, b):
            return (pltpu.make_async_copy(rows[b], xs_hbm.at[idx[b]], ssem[b]),
                    pltpu.make_async_copy(idx[b], dest_hbm.at[pl.ds(base + c * ch, ch)], dsem[b]))

        _sc_ring(nchunk, load, store, compute)

    return dispatch_sc(x2, slab, starts)


def _unsort(ys, dest, t):
    d = ys.shape[1]
    nc, per_w, nchunk = _sc_split(t)
    ch = SC_CHUNK_ROWS
    mesh = plsc.VectorSubcoreMesh(core_axis_name="c", subcore_axis_name="s")
    dma = pltpu.SemaphoreType.DMA

    @functools.partial(
        pl.kernel, mesh=mesh,
        out_type=jax.ShapeDtypeStruct((t, d), jnp.float32),
        scratch_types=([pltpu.VMEM((per_w,), jnp.int32)]
                       + [pltpu.VMEM((ch, d), jnp.float32)] * SC_BUFFERS + [dma] * (2 * SC_BUFFERS)),
    )
    def unsort_sc(ys_hbm, dest_hbm, out_hbm, idx_v, *scratch):
        n = SC_BUFFERS
        rows, gsem, wsem = (scratch[k * n:(k + 1) * n] for k in range(3))
        wid = lax.axis_index("s") * nc + lax.axis_index("c")
        base = wid * per_w
        pltpu.sync_copy(dest_hbm.at[pl.ds(base, per_w)], idx_v)

        def load(c, b):
            return (pltpu.make_async_copy(ys_hbm.at[idx_v.at[pl.ds(c * ch, ch)]], rows[b], gsem[b]),)

        def store(c, b):
            return (pltpu.make_async_copy(rows[b], out_hbm.at[pl.ds(base + c * ch, ch)], wsem[b]),)

        _sc_ring(nchunk, load, store)

    return unsort_sc(ys, dest)


def _expert_kernel(grp_ref, ea_ref, eb_ref, nvalid_ref, nact_ref,
                   xs_ref, gf_ref, wr_ref, br_ref, gfin_ref, wg_hbm, wu_hbm, wd_hbm,
                   o_ref, wgb, wub, wdb, sg, su, sd, sem, cnt_ref):
    s = pl.program_id(0)
    blocks = tuple(BLOCKS_PER_STEP * s + i for i in range(BLOCKS_PER_STEP))
    g = grp_ref[blocks[0]]

    def stage(e):
        slot = e % WEIGHT_STAGES
        return (pltpu.make_async_copy(wg_hbm.at[e], sg.at[slot], sem.at[slot, 0]),
                pltpu.make_async_copy(wu_hbm.at[e], su.at[slot], sem.at[slot, 1]),
                pltpu.make_async_copy(wd_hbm.at[e], sd.at[slot], sem.at[slot, 2]))

    def start_next():
        @pl.when(cnt_ref[0] < N_EXPERTS)
        def _():
            for cp in stage(cnt_ref[0]):
                cp.start()
            cnt_ref[0] = cnt_ref[0] + 1

    @pl.when(s == 0)
    def _():
        cnt_ref[0] = 0
        cnt_ref[1] = 0
        for _ in range(WEIGHT_STAGES):
            start_next()

    active = blocks[0] < nact_ref[0]

    @pl.when(active)
    def _():
        need = g * EXPERTS_PER_GROUP + functools.reduce(
            jnp.maximum, [eb_ref[j] for j in blocks])

        def load(e, carry):
            for cp in stage(e):
                cp.wait()
            slot = e % WEIGHT_STAGES
            k = e % EXPERTS_PER_GROUP
            wgb[k] = sg[slot].astype(jnp.bfloat16)
            wub[k] = su[slot].astype(jnp.bfloat16)
            wdb[k] = sd[slot].astype(jnp.bfloat16)
            cnt_ref[1] = e + 1
            start_next()
            return carry

        lax.fori_loop(cnt_ref[1], need + 1, load, 0)

        lane = lax.broadcasted_iota(jnp.int32, (ROW_BLOCK, ROUTE_LANES), 1)
        is_g = lane < N_GROUPS
        lo = EXPERT_LANE0 + EXPERTS_PER_GROUP * g
        nblk = len(blocks)
        rowid = lax.broadcasted_iota(jnp.int32, (ROW_BLOCK, 1), 0)
        xs = [jnp.where(rowid < nvalid_ref[blocks[i]],
                        xs_ref[i * ROW_BLOCK:(i + 1) * ROW_BLOCK, :], 0.0) for i in range(nblk)]
        h3 = [_rms(x, gf_ref[...]).astype(jnp.bfloat16) for x in xs]
        lg = [jnp.dot(h, wr_ref[...], preferred_element_type=jnp.float32) + br_ref[...]
              for h in h3]

        def gates(lgi, j):
            def pick(idx):
                return jnp.sum(jnp.where(lane == idx, lgi, 0.0), axis=-1, keepdims=True)

            gmax = jnp.max(jnp.where(is_g, lgi, _NEG), axis=-1, keepdims=True)
            den = jnp.sum(jnp.where(is_g, jnp.exp(jnp.where(is_g, lgi, _NEG) - gmax), 0.0),
                          axis=-1, keepdims=True)
            grp_p = jnp.exp(pick(g) - gmax) / den
            la = pick(lo + ea_ref[j])
            lb = pick(lo + eb_ref[j])
            m = jnp.maximum(la, lb)
            pa = jnp.exp(la - m)
            pb = jnp.exp(lb - m)
            return grp_p * pa / (pa + pb), grp_p * pb / (pa + pb)

        gate = [gates(lg[i], blocks[i]) for i in range(nblk)]
        ys = list(xs)
        for which, e_ref in enumerate((ea_ref, eb_ref)):
            ks = [e_ref[j] for j in blocks]
            gg = [jnp.dot(h3[i], wgb[ks[i]], preferred_element_type=jnp.float32)
                  for i in range(nblk)]
            uu = [jnp.dot(h3[i], wub[ks[i]], preferred_element_type=jnp.float32)
                  for i in range(nblk)]
            act = [(gg[i] * (0.5 + 0.5 * jnp.tanh(0.5 * gg[i])) * uu[i]
                    * gate[i][which]).astype(jnp.bfloat16) for i in range(nblk)]
            ys = [ys[i] + jnp.dot(act[i], wdb[ks[i]], preferred_element_type=jnp.float32)
                  for i in range(nblk)]
        for i in range(nblk):
            o_ref[i * ROW_BLOCK:(i + 1) * ROW_BLOCK, :] = _rms(ys[i], gfin_ref[...])

    @pl.when(s == pl.num_programs(0) - 1)
    def _():
        def drain(e, carry):
            for cp in stage(e):
                cp.wait()
            return carry

        lax.fori_loop(cnt_ref[1], cnt_ref[0], drain, 0)


def _experts(xs, blk_grp, blk_a, blk_b, nvalid, nact, g_ffn, w_r, b_r, g_final, w_gate, w_up, w_down):
    cap, d = xs.shape
    de = w_gate.shape[2]
    step_rows = BLOCKS_PER_STEP * ROW_BLOCK
    steps = cap // step_rows
    pre = lambda f: (lambda s, gr, ea, eb, nv, na: f(s, na))
    const2 = pre(lambda s, na: (0, 0))
    last_step = lambda na: (na[0] - 1) // BLOCKS_PER_STEP
    hbm = pl.BlockSpec(memory_space=pl.ANY)
    active_step = pre(lambda s, na: (jnp.minimum(s, last_step(na)), 0))
    grid_spec = pltpu.PrefetchScalarGridSpec(
        num_scalar_prefetch=5,
        grid=(steps,),
        in_specs=[
            pl.BlockSpec((step_rows, d), active_step),
            pl.BlockSpec(g_ffn.shape, const2),
            pl.BlockSpec(w_r.shape, const2),
            pl.BlockSpec(b_r.shape, const2),
            pl.BlockSpec(g_final.shape, const2),
            hbm, hbm, hbm,
        ],
        out_specs=pl.BlockSpec((step_rows, d), active_step),
        scratch_shapes=[
            pltpu.VMEM((EXPERTS_PER_GROUP, d, de), jnp.bfloat16),
            pltpu.VMEM((EXPERTS_PER_GROUP, d, de), jnp.bfloat16),
            pltpu.VMEM((EXPERTS_PER_GROUP, de, d), jnp.bfloat16),
            pltpu.VMEM((WEIGHT_STAGES, d, de), jnp.float32),
            pltpu.VMEM((WEIGHT_STAGES, d, de), jnp.float32),
            pltpu.VMEM((WEIGHT_STAGES, de, d), jnp.float32),
            pltpu.SemaphoreType.DMA((WEIGHT_STAGES, 3)),
            pltpu.SMEM((2,), jnp.int32),
        ],
    )
    return pl.pallas_call(
        _expert_kernel,
        grid_spec=grid_spec,
        out_shape=jax.ShapeDtypeStruct((cap, d), jnp.float32),
        compiler_params=pltpu.CompilerParams(
            dimension_semantics=("arbitrary",), vmem_limit_bytes=EXPERT_VMEM_LIMIT_BYTES),
        name="experts",
    )(blk_grp, blk_a, blk_b, nvalid, nact, xs, g_ffn, w_r, b_r, g_final, w_gate, w_up, w_down)


def _moe_final(x2, slab, counts_col, g_ffn, w_r, b_r, g_final, w_gate, w_up, w_down):
    t, d = x2.shape
    nb = t // ROW_BLOCK + N_CLASSES + N_GROUPS * (BLOCKS_PER_STEP - 1)
    assert nb % BLOCKS_PER_STEP == 0
    counts = counts_col[:N_CLASSES, 0].astype(jnp.int32)
    nblk = (counts + ROW_BLOCK - 1) // ROW_BLOCK
    grp_blocks = jnp.sum(nblk.reshape(N_GROUPS, N_PAIRS), axis=1)
    nblk = nblk.reshape(N_GROUPS, N_PAIRS).at[:, N_PAIRS - 1].add(
        (-grp_blocks) % BLOCKS_PER_STEP).reshape(N_CLASSES)
    blk_end = jnp.cumsum(nblk)
    blk_start = blk_end - nblk
    nact = blk_end[-1]
    j = jnp.arange(nb, dtype=jnp.int32)
    blk_class = jnp.minimum(
        jnp.sum((blk_end[None, :] <= j[:, None]).astype(jnp.int32), axis=1), N_CLASSES - 1)
    blk_class = jnp.where(j < nact, blk_class, blk_class[jnp.maximum(nact - 1, 0)])
    nvalid = jnp.where(j < nact, jnp.clip(
        counts[blk_class] - (j - blk_start[blk_class]) * ROW_BLOCK, 0, ROW_BLOCK), 0)
    pair_a = jnp.array([p[0] for p in PAIRS], jnp.int32)
    pair_b = jnp.array([p[1] for p in PAIRS], jnp.int32)
    blk_grp = blk_class // N_PAIRS
    blk_a = pair_a[blk_class % N_PAIRS]
    blk_b = pair_b[blk_class % N_PAIRS]
    starts_col = jnp.zeros((ROUTE_LANES, 1), jnp.float32).at[:N_CLASSES, 0].set(
        (blk_start * ROW_BLOCK).astype(jnp.float32))

    xs, dest = _dispatch(x2, slab, starts_col[:, 0], nb * ROW_BLOCK)
    ys = _experts(xs, blk_grp, blk_a, blk_b, nvalid, nact[None], g_ffn, w_r, b_r, g_final,
                  w_gate, w_up, w_down)
    return _unsort(ys, dest, t)


def kernel(x, mem, g_mix, w_in, conv_w, g_v, w_s, b_s, g_out_conv, g_out_gmlp, w_out, g_xattn,
           g_mem, w_q, w_k, w_v, w_o, g_ffn, w_grp, b_grp, w_rt, b_rt, w_gate, w_up, w_down,
           g_final):
    b, s, d = x.shape
    assert g_mix.shape[0] == 1, "the final norm is fused into the single layer's expert kernel"
    assert N_CLASSES <= ROUTE_LANES
    bf = jnp.bfloat16
    kt, v = _kv_proj(mem, g_mem[0][None], w_k[0].astype(bf), w_v[0].astype(bf))

    bias = jnp.repeat(b_s[0].T, GMLP_HEAD_DIM, axis=1)
    x1 = _mixer(x, g_mix[0][None], w_in[0].astype(bf), conv_w[0], g_v[0][None], w_s[0], bias,
                g_out_conv[0][None], g_out_gmlp[0][None], w_out[0].astype(bf))

    pad = ROUTE_LANES - N_GROUPS - N_EXPERTS
    w_r = jnp.concatenate([w_grp[0], w_rt[0], jnp.zeros((d, pad), jnp.float32)], axis=1).astype(bf)
    b_r = jnp.concatenate([b_grp[0], b_rt[0], jnp.zeros((pad,), jnp.float32)])[None]
    gap = ROUTE_EXPERT_ROW0 - N_GROUPS
    tail = ROUTE_LANES - ROUTE_EXPERT_ROW0 - N_EXPERTS
    w_r_t = jnp.concatenate([w_grp[0].T, jnp.zeros((gap, d), jnp.float32), w_rt[0].T,
                             jnp.zeros((tail, d), jnp.float32)], axis=0).astype(bf)
    b_r_t = jnp.concatenate([b_grp[0], jnp.zeros((gap,), jnp.float32), b_rt[0],
                             jnp.zeros((tail,), jnp.float32)])[:, None]
    x2, slab, counts_col = _attn_route(x1, kt, v, g_xattn[0][None], w_q[0].astype(bf),
                                       w_o[0].astype(bf), g_ffn[0][None], w_r_t, b_r_t)
    out = _moe_final(x2.reshape(b * s, d), slab, counts_col, g_ffn[0][None], w_r, b_r,
                     g_final[None], w_gate[0], w_up[0], w_down[0])
    return out.reshape(b, s, d)
```

```python
import functools

import jax
import jax.numpy as jnp
from jax import lax
from jax.experimental import pallas as pl
from jax.experimental.pallas import tpu as pltpu
from jax.experimental.pallas import tpu_sc as plsc

EPS = 1e-6
CONV_GROUP_WIDTH = 512
GMLP_HEADS = 8
GMLP_HEAD_DIM = 64
CHUNK = 128
XA_HEADS = 4
N_GROUPS = 4
EXPERTS_PER_GROUP = 8
N_EXPERTS = N_GROUPS * EXPERTS_PER_GROUP
ROUTE_LANES = 128
EXPERT_LANE0 = N_GROUPS
ROUTE_EXPERT_ROW0 = 8

N_PAIRS = EXPERTS_PER_GROUP * (EXPERTS_PER_GROUP - 1) // 2
N_CLASSES = N_GROUPS * N_PAIRS
PAIRS = [(a, b) for a in range(EXPERTS_PER_GROUP) for b in range(a + 1, EXPERTS_PER_GROUP)]

MIXER_TILE = 1024
ATTN_TILE = 2048
SUB_TILE = 256
ROW_BLOCK = 128
BLOCKS_PER_STEP = 4
WEIGHT_STAGES = 3
SC_CHUNK_ROWS = 16
SC_BUFFERS = 4
VMEM_LIMIT_BYTES = 56 * 1024 * 1024
EXPERT_VMEM_LIMIT_BYTES = 60 * 1024 * 1024

SUBLANES = 8

_NEG = -1e30


def _rms(x, g):
    return x * lax.rsqrt(jnp.mean(x * x, axis=-1, keepdims=True) + EPS) * g


def _gelu_tanh(x):
    return 0.5 * x * (1.0 + jnp.tanh(0.7978845608028654 * (x + 0.044715 * (x * x * x))))


def _kv_kernel(mem_ref, g_ref, wk_ref, wv_ref, kt_ref, v_ref):
    m = _rms(mem_ref[0], g_ref[...]).astype(jnp.bfloat16)
    k = jnp.dot(m, wk_ref[...], preferred_element_type=jnp.float32)
    v = jnp.dot(m, wv_ref[...], preferred_element_type=jnp.float32)
    kt_ref[0] = k.T.astype(jnp.bfloat16)
    v_ref[0] = v.astype(jnp.bfloat16)


def _kv_proj(mem, g_mem, w_k, w_v):
    b, m, d = mem.shape
    const = lambda i: (0, 0)
    return pl.pallas_call(
        _kv_kernel,
        grid=(b,),
        in_specs=[
            pl.BlockSpec((1, m, d), lambda i: (i, 0, 0)),
            pl.BlockSpec((1, d), const),
            pl.BlockSpec((d, d), const),
            pl.BlockSpec((d, d), const),
        ],
        out_specs=[
            pl.BlockSpec((1, d, m), lambda i: (i, 0, 0)),
            pl.BlockSpec((1, m, d), lambda i: (i, 0, 0)),
        ],
        out_shape=[
            jax.ShapeDtypeStruct((b, d, m), jnp.bfloat16),
            jax.ShapeDtypeStruct((b, m, d), jnp.bfloat16),
        ],
        compiler_params=pltpu.CompilerParams(
            dimension_semantics=("arbitrary",), vmem_limit_bytes=VMEM_LIMIT_BYTES),
        name="kv_proj",
    )(mem, g_mem, w_k, w_v)


def _mixer_kernel(x_ref, gmix_ref, win_ref, convw_ref, gv_ref, ws_ref, bias_ref,
                  goc_ref, gog_ref, wout_ref, o_ref, zbuf_ref):
    ts = x_ref.shape[1]
    w = CONV_GROUP_WIDTH
    hw = 2 * GMLP_HEAD_DIM
    nsub = ts // SUB_TILE

    def dot(a, b):
        return jnp.dot(a, b, preferred_element_type=jnp.float32)

    @pl.when(pl.program_id(1) == 0)
    def _():
        zbuf_ref[0:8, :] = jnp.zeros((8, w), jnp.float32)

    @pl.when(pl.program_id(1) != 0)
    def _():
        zbuf_ref[0:8, :] = zbuf_ref[ts:ts + 8, :]

    low = lax.broadcasted_iota(jnp.int32, (SUB_TILE, hw), 1) < GMLP_HEAD_DIM
    row = lax.broadcasted_iota(jnp.int32, (CHUNK, CHUNK), 0)
    colid = lax.broadcasted_iota(jnp.int32, (CHUNK, CHUNK), 1)
    causal = row >= colid
    ws = [jnp.where(causal, ws_ref[hd], 0.0).astype(jnp.bfloat16) for hd in range(GMLP_HEADS)]
    npair = SUB_TILE // CHUNK // 2
    lo = lax.broadcasted_iota(jnp.int32, (CHUNK, hw), 1) < GMLP_HEAD_DIM
    swap = lambda a: pltpu.roll(a, GMLP_HEAD_DIM, axis=1)

    def project(i):
        r0 = i * SUB_TILE
        xt = x_ref[0, r0:r0 + SUB_TILE, :]
        h = _rms(xt, gmix_ref[...]).astype(jnp.bfloat16)
        p = [dot(h, win_ref[:, k * w:(k + 1) * w]) for k in range(5)]
        z = p[1] * p[2]
        zbuf_ref[8 + r0:8 + r0 + SUB_TILE, :] = z
        return dict(xt=xt, gate_b=p[0], z=z, u=p[3], v=p[4], r0=r0)

    def gate_and_norm(st):
        r0 = st["r0"]
        zc = (convw_ref[0:1, :] * zbuf_ref[6 + r0:6 + r0 + SUB_TILE, :]
              + convw_ref[1:2, :] * zbuf_ref[7 + r0:7 + r0 + SUB_TILE, :]
              + convw_ref[2:3, :] * st["z"])
        st["ya"] = _rms(st["gate_b"] * zc, goc_ref[...]).astype(jnp.bfloat16)
        st["u"] = _gelu_tanh(st["u"])
        v = _gelu_tanh(st["v"])
        v2 = v * v
        ss_cols = []
        for k in range(GMLP_HEADS // 2):
            col = v2[:, hw * k:hw * (k + 1)]
            ss_cols.append(jnp.where(low,
                                     jnp.sum(jnp.where(low, col, 0.0), axis=1, keepdims=True),
                                     jnp.sum(jnp.where(low, 0.0, col), axis=1, keepdims=True)))
        ss = jnp.concatenate(ss_cols, axis=1)
        st["vn"] = v * lax.rsqrt(ss * (1.0 / GMLP_HEAD_DIM) + EPS) * gv_ref[...]

    def mix_positions(st):
        vn = st["vn"]
        s_cols = [[None] * (GMLP_HEADS // 2) for _ in range(2 * npair)]
        for hp in range(GMLP_HEADS // 2):
            cols = [vn[c * CHUNK:(c + 1) * CHUNK, hw * hp:hw * (hp + 1)] for c in range(2 * npair)]
            swapped = [swap(a) for a in cols]
            rhs_even = jnp.concatenate(
                [jnp.where(lo, cols[2 * p], swapped[2 * p + 1]) for p in range(npair)], axis=1)
            rhs_odd = jnp.concatenate(
                [jnp.where(lo, swapped[2 * p], cols[2 * p + 1]) for p in range(npair)], axis=1)
            out_e = dot(ws[2 * hp], rhs_even.astype(jnp.bfloat16))
            out_o = dot(ws[2 * hp + 1], rhs_odd.astype(jnp.bfloat16))
            for p in range(npair):
                e = out_e[:, hw * p:hw * (p + 1)]
                o = out_o[:, hw * p:hw * (p + 1)]
                s_cols[2 * p][hp] = jnp.where(lo, e, swap(o))
                s_cols[2 * p + 1][hp] = jnp.where(lo, swap(e), o)
        st["s"] = jnp.concatenate(
            [jnp.concatenate(c, axis=1) + bias_ref[...] for c in s_cols], axis=0)

    def output(st):
        r0 = st["r0"]
        yb = _rms(st["u"] * st["s"], gog_ref[...]).astype(jnp.bfloat16)
        o_ref[0, r0:r0 + SUB_TILE, :] = (st["xt"] + dot(st["ya"], wout_ref[0:w, :])
                                          + dot(yb, wout_ref[w:2 * w, :]))

    phases = (gate_and_norm, mix_positions, output)
    states = []
    for step in range(nsub + len(phases)):
        if step < nsub:
            states.append(project(step))
        for k, phase in enumerate(phases):
            i = step - 1 - k
            if 0 <= i < nsub:
                phase(states[i])


def _mixer(x, g_mix, w_in, conv_w, g_v, w_s, bias, g_oc, g_og, w_out):
    b, s, d = x.shape
    ts = min(MIXER_TILE, s)
    const2 = lambda i, j: (0, 0)
    const3 = lambda i, j: (0, 0, 0)
    return pl.pallas_call(
        _mixer_kernel,
        grid=(b, s // ts),
        in_specs=[
            pl.BlockSpec((1, ts, d), lambda i, j: (i, j, 0)),
            pl.BlockSpec(g_mix.shape, const2),
            pl.BlockSpec(w_in.shape, const2),
            pl.BlockSpec(conv_w.shape, const2),
            pl.BlockSpec(g_v.shape, const2),
            pl.BlockSpec(w_s.shape, const3),
            pl.BlockSpec(bias.shape, const2),
            pl.BlockSpec(g_oc.shape, const2),
            pl.BlockSpec(g_og.shape, const2),
            pl.BlockSpec(w_out.shape, const2),
        ],
        out_specs=pl.BlockSpec((1, ts, d), lambda i, j: (i, j, 0)),
        out_shape=jax.ShapeDtypeStruct((b, s, d), jnp.float32),
        scratch_shapes=[pltpu.VMEM((ts + 8, CONV_GROUP_WIDTH), jnp.float32)],
        compiler_params=pltpu.CompilerParams(
            dimension_semantics=("arbitrary", "arbitrary"), vmem_limit_bytes=VMEM_LIMIT_BYTES),
        name="mixer",
    )(x, g_mix, w_in, conv_w, g_v, w_s, bias, g_oc, g_og, w_out)


def _attn_route_kernel(x_ref, kt_ref, v_ref, gx_ref, wq_ref, wo_ref, gf_ref, wrt_ref, brt_ref,
                       x2_ref, slab_ref, counts_ref, carry_ref):
    ts = x_ref.shape[1]
    d = x_ref.shape[2]
    hd = d // XA_HEADS
    nsub = ts // SUB_TILE
    first = (pl.program_id(0) == 0) & (pl.program_id(1) == 0)

    @pl.when(first)
    def _():
        carry_ref[...] = jnp.zeros_like(carry_ref)

    def dot(a, b):
        return jnp.dot(a, b, preferred_element_type=jnp.float32)

    subs = range(nsub)
    x1 = [x_ref[0, i * SUB_TILE:(i + 1) * SUB_TILE, :] for i in subs]
    h2 = [_rms(x, gx_ref[...]).astype(jnp.bfloat16) for x in x1]
    q = [dot(h, wq_ref[...]).astype(jnp.bfloat16) for h in h2]
    heads = [[] for _ in subs]
    for a in range(XA_HEADS):
        sc = [dot(q[i][:, a * hd:(a + 1) * hd], kt_ref[0, a * hd:(a + 1) * hd, :]) * (hd ** -0.5)
              for i in subs]
        p = [jnp.exp(s_ - jnp.max(s_, axis=-1, keepdims=True)) for s_ in sc]
        l = [jnp.sum(p_, axis=-1, keepdims=True) for p_ in p]
        o = [dot(p[i].astype(jnp.bfloat16), v_ref[0, :, a * hd:(a + 1) * hd]) for i in subs]
        for i in subs:
            heads[i].append((o[i] / l[i]).astype(jnp.bfloat16))
    x2 = [x1[i] + dot(jnp.concatenate(heads[i], axis=1), wo_ref[...]) for i in subs]
    for i in subs:
        x2_ref[0, i * SUB_TILE:(i + 1) * SUB_TILE, :] = x2[i]

    h3 = [_rms(x, gf_ref[...]).astype(jnp.bfloat16) for x in x2]
    lgs = [lax.dot_general(wrt_ref[...], h, (((1,), (1,)), ((), ())),
                           preferred_element_type=jnp.float32) + brt_ref[...] for h in h3]
    sub = lax.broadcasted_iota(jnp.int32, (EXPERTS_PER_GROUP, SUB_TILE), 0).astype(jnp.float32)
    big = float(EXPERTS_PER_GROUP)
    is_g = sub < N_GROUPS

    def classify(lg):
        glog = lg[0:EXPERTS_PER_GROUP, :]
        gmax = jnp.max(jnp.where(is_g, glog, _NEG), axis=0, keepdims=True)
        gidx = jnp.min(jnp.where(is_g & (glog == gmax), sub, big), axis=0, keepdims=True)
        el = lg[ROUTE_EXPERT_ROW0:ROUTE_EXPERT_ROW0 + EXPERTS_PER_GROUP, :]
        for grp in range(1, N_GROUPS):
            r0 = ROUTE_EXPERT_ROW0 + grp * EXPERTS_PER_GROUP
            el = jnp.where(gidx == grp, lg[r0:r0 + EXPERTS_PER_GROUP, :], el)
        t1 = jnp.max(el, axis=0, keepdims=True)
        i1 = jnp.min(jnp.where(el == t1, sub, big), axis=0, keepdims=True)
        rest = sub != i1
        t2 = jnp.max(jnp.where(rest, el, _NEG), axis=0, keepdims=True)
        i2 = jnp.min(jnp.where(rest & (el == t2), sub, big), axis=0, keepdims=True)
        a = jnp.minimum(i1, i2)
        b = jnp.maximum(i1, i2)
        pair = a * (2 * EXPERTS_PER_GROUP - 1 - a) * 0.5 + (b - a - 1.0)
        return gidx * N_PAIRS + pair

    cls = [classify(lg) for lg in lgs]

    r = lax.broadcasted_iota(jnp.int32, (SUB_TILE, SUB_TILE), 0)
    c = lax.broadcasted_iota(jnp.int32, (SUB_TILE, SUB_TILE), 1)
    earlier = jnp.where(r < c, 1.0, 0.0).astype(jnp.bfloat16)
    crow = lax.broadcasted_iota(jnp.int32, (ROUTE_LANES, SUB_TILE), 0).astype(jnp.float32)
    oh = [crow == cl for cl in cls]
    ohf = [jnp.where(o_, 1.0, 0.0) for o_ in oh]
    inside = [dot(o_.astype(jnp.bfloat16), earlier) for o_ in ohf]
    seen = carry_ref[...]
    for i in subs:
        rank = jnp.sum(jnp.where(oh[i], inside[i] + seen, 0.0), axis=0, keepdims=True)
        slab_ref[:, i * SUB_TILE:(i + 1) * SUB_TILE] = jnp.where(
            sub == 0, cls[i], jnp.where(sub == 1, rank, 0.0))
        seen = seen + jnp.sum(ohf[i], axis=1, keepdims=True)
    carry_ref[...] = seen
    counts_ref[...] = seen


def _attn_route(x1, kt, v, g_x, w_q, w_o, g_f, w_rt, b_rt):
    b, s, d = x1.shape
    m = v.shape[1]
    ts = min(ATTN_TILE, s)
    nt = s // ts
    const2 = lambda i, j: (0, 0)
    return pl.pallas_call(
        _attn_route_kernel,
        grid=(b, nt),
        in_specs=[
            pl.BlockSpec((1, ts, d), lambda i, j: (i, j, 0)),
            pl.BlockSpec((1, d, m), lambda i, j: (i, 0, 0)),
            pl.BlockSpec((1, m, d), lambda i, j: (i, 0, 0)),
            pl.BlockSpec(g_x.shape, const2),
            pl.BlockSpec(w_q.shape, const2),
            pl.BlockSpec(w_o.shape, const2),
            pl.BlockSpec(g_f.shape, const2),
            pl.BlockSpec(w_rt.shape, const2),
            pl.BlockSpec(b_rt.shape, const2),
        ],
        out_specs=[
            pl.BlockSpec((1, ts, d), lambda i, j: (i, j, 0)),
            pl.BlockSpec((SUBLANES, ts), lambda i, j: (0, i * nt + j)),
            pl.BlockSpec((ROUTE_LANES, 1), const2),
        ],
        out_shape=[
            jax.ShapeDtypeStruct((b, s, d), jnp.float32),
            jax.ShapeDtypeStruct((SUBLANES, b * s), jnp.float32),
            jax.ShapeDtypeStruct((ROUTE_LANES, 1), jnp.float32),
        ],
        scratch_shapes=[pltpu.VMEM((ROUTE_LANES, 1), jnp.float32)],
        compiler_params=pltpu.CompilerParams(
            dimension_semantics=("arbitrary", "arbitrary"), vmem_limit_bytes=VMEM_LIMIT_BYTES),
        name="attn_route",
    )(x1, kt, v, g_x, w_q, w_o, g_f, w_rt, b_rt)


def _sc_split(t):
    info = plsc.get_sparse_core_info()
    workers = info.num_cores * info.num_subcores
    per_w, rem = divmod(t, workers)
    nchunk, rem2 = divmod(per_w, SC_CHUNK_ROWS)
    assert rem == 0 and rem2 == 0 and nchunk % SC_BUFFERS == 0, (t, workers, SC_CHUNK_ROWS)
    return info.num_cores, per_w, nchunk


def _sc_ring(nchunk, load, store, compute=None):
    ahead = SC_BUFFERS - 1
    for k in range(ahead):
        for cp in load(k, k):
            cp.start()

    @pl.loop(0, nchunk, step=SC_BUFFERS)
    def _(c):
        for b in range(SC_BUFFERS):
            cc = c + b
            refill = (b + ahead) % SC_BUFFERS

            @pl.when(cc >= 1)
            def _():
                for cp in store(cc - 1, refill):
                    cp.wait()

            @pl.when(cc + ahead < nchunk)
            def _():
                for cp in load(cc + ahead, refill):
                    cp.start()

            for cp in load(cc, b):
                cp.wait()
            if compute is not None:
                compute(cc, b)
            for cp in store(cc, b):
                cp.start()

    for cp in store(nchunk - 1, (nchunk - 1) % SC_BUFFERS):
        cp.wait()


def _dispatch(x2, slab, starts, cap):
    t, d = x2.shape
    nc, per_w, nchunk = _sc_split(t)
    ch = SC_CHUNK_ROWS
    mesh = plsc.VectorSubcoreMesh(core_axis_name="c", subcore_axis_name="s")
    dma = pltpu.SemaphoreType.DMA
    n = SC_BUFFERS

    @functools.partial(
        pl.kernel, mesh=mesh,
        out_type=(jax.ShapeDtypeStruct((cap, d), jnp.float32), jax.ShapeDtypeStruct((t,), jnp.int32)),
        compiler_params=pltpu.CompilerParams(needs_layout_passes=False),
        scratch_types=([pltpu.VMEM((ROUTE_LANES,), jnp.float32)]
                       + [pltpu.VMEM((ch,), jnp.float32)] * (2 * n) + [pltpu.VMEM((ch,), jnp.int32)] * n
                       + [pltpu.VMEM((ch, d), jnp.float32)] * n + [dma] * (5 * n)),
    )
    def dispatch_sc(x_hbm, slab_hbm, starts_hbm, xs_hbm, dest_hbm, starts_v, *scratch):
        cls, rank, idx, rows = (scratch[k * n:(k + 1) * n] for k in range(4))
        csem, ksem, rsem, ssem, dsem = (scratch[(4 + k) * n:(5 + k) * n] for k in range(5))
        wid = lax.axis_index("s") * nc + lax.axis_index("c")
        base = wid * per_w
        pltpu.sync_copy(starts_hbm, starts_v)

        def load(c, b):
            src = pl.ds(base + c * ch, ch)
            return (pltpu.make_async_copy(slab_hbm.at[0, src], cls[b], csem[b]),
                    pltpu.make_async_copy(slab_hbm.at[1, src], rank[b], ksem[b]),
                    pltpu.make_async_copy(x_hbm.at[src], rows[b], rsem[b]))

        def compute(c, b):
            start = plsc.load_gather(starts_v, [cls[b][...].astype(jnp.int32)])
            idx[b][...] = (start + rank[b][...]).astype(jnp.int32)

        def store(c, b):
            return (pltpu.make_async_copy(rows[b], xs_hbm.at[idx[b]], ssem[b]),
                    pltpu.make_async_copy(idx[b], dest_hbm.at[pl.ds(base + c * ch, ch)], dsem[b]))

        _sc_ring(nchunk, load, store, compute)

    return dispatch_sc(x2, slab, starts)


def _unsort(ys, dest, t):
    d = ys.shape[1]
    nc, per_w, nchunk = _sc_split(t)
    ch = SC_CHUNK_ROWS
    mesh = plsc.VectorSubcoreMesh(core_axis_name="c", subcore_axis_name="s")
    dma = pltpu.SemaphoreType.DMA

    @functools.partial(
        pl.kernel, mesh=mesh,
        out_type=jax.ShapeDtypeStruct((t, d), jnp.float32),
        scratch_types=([pltpu.VMEM((per_w,), jnp.int32)]
                       + [pltpu.VMEM((ch, d), jnp.float32)] * SC_BUFFERS + [dma] * (2 * SC_BUFFERS)),
    )
    def unsort_sc(ys_hbm, dest_hbm, out_hbm, idx_v, *scratch):
        n = SC_BUFFERS
        rows, gsem, wsem = (scratch[k * n:(k + 1) * n] for k in range(3))
        wid = lax.axis_index("s") * nc + lax.axis_index("c")
        base = wid * per_w
        pltpu.sync_copy(dest_hbm.at[pl.ds(base, per_w)], idx_v)

        def load(c, b):
            return (pltpu.make_async_copy(ys_hbm.at[idx_v.at[pl.ds(c * ch, ch)]], rows[b], gsem[b]),)

        def store(c, b):
            return (pltpu.make_async_copy(rows[b], out_hbm.at[pl.ds(base + c * ch, ch)], wsem[b]),)

        _sc_ring(nchunk, load, store)

    return unsort_sc(ys, dest)


def _expert_kernel(grp_ref, ea_ref, eb_ref, nvalid_ref, nact_ref,
                   xs_ref, gf_ref, wr_ref, br_ref, gfin_ref, wg_hbm, wu_hbm, wd_hbm,
                   o_ref, wgb, wub, wdb, sg, su, sd, sem, cnt_ref):
    s = pl.program_id(0)
    blocks = tuple(BLOCKS_PER_STEP * s + i for i in range(BLOCKS_PER_STEP))
    g = grp_ref[blocks[0]]

    def stage(e):
        slot = e % WEIGHT_STAGES
        return (pltpu.make_async_copy(wg_hbm.at[e], sg.at[slot], sem.at[slot, 0]),
                pltpu.make_async_copy(wu_hbm.at[e], su.at[slot], sem.at[slot, 1]),
                pltpu.make_async_copy(wd_hbm.at[e], sd.at[slot], sem.at[slot, 2]))

    def start_next():
        @pl.when(cnt_ref[0] < N_EXPERTS)
        def _():
            for cp in stage(cnt_ref[0]):
                cp.start()
            cnt_ref[0] = cnt_ref[0] + 1

    @pl.when(s == 0)
    def _():
        cnt_ref[0] = 0
        cnt_ref[1] = 0
        for _ in range(WEIGHT_STAGES):
            start_next()

    active = blocks[0] < nact_ref[0]

    @pl.when(active)
    def _():
        need = g * EXPERTS_PER_GROUP + functools.reduce(
            jnp.maximum, [eb_ref[j] for j in blocks])

        def load(e, carry):
            for cp in stage(e):
                cp.wait()
            slot = e % WEIGHT_STAGES
            k = e % EXPERTS_PER_GROUP
            wgb[k] = sg[slot].astype(jnp.bfloat16)
            wub[k] = su[slot].astype(jnp.bfloat16)
            wdb[k] = sd[slot].astype(jnp.bfloat16)
            cnt_ref[1] = e + 1
            start_next()
            return carry

        lax.fori_loop(cnt_ref[1], need + 1, load, 0)

        lane = lax.broadcasted_iota(jnp.int32, (ROW_BLOCK, ROUTE_LANES), 1)
        is_g = lane < N_GROUPS
        lo = EXPERT_LANE0 + EXPERTS_PER_GROUP * g
        nblk = len(blocks)
        rowid = lax.broadcasted_iota(jnp.int32, (ROW_BLOCK, 1), 0)
        xs = [jnp.where(rowid < nvalid_ref[blocks[i]],
                        xs_ref[i * ROW_BLOCK:(i + 1) * ROW_BLOCK, :], 0.0) for i in range(nblk)]
        h3 = [_rms(x, gf_ref[...]).astype(jnp.bfloat16) for x in xs]
        lg = [jnp.dot(h, wr_ref[...], preferred_element_type=jnp.float32) + br_ref[...]
              for h in h3]

        def gates(lgi, j):
            def pick(idx):
                return jnp.sum(jnp.where(lane == idx, lgi, 0.0), axis=-1, keepdims=True)

            gmax = jnp.max(jnp.where(is_g, lgi, _NEG), axis=-1, keepdims=True)
            den = jnp.sum(jnp.where(is_g, jnp.exp(jnp.where(is_g, lgi, _NEG) - gmax), 0.0),
                          axis=-1, keepdims=True)
            grp_p = jnp.exp(pick(g) - gmax) / den
            la = pick(lo + ea_ref[j])
            lb = pick(lo + eb_ref[j])
            m = jnp.maximum(la, lb)
            pa = jnp.exp(la - m)
            pb = jnp.exp(lb - m)
            return grp_p * pa / (pa + pb), grp_p * pb / (pa + pb)

        gate = [gates(lg[i], blocks[i]) for i in range(nblk)]
        ys = list(xs)
        for which, e_ref in enumerate((ea_ref, eb_ref)):
            ks = [e_ref[j] for j in blocks]
            gg = [jnp.dot(h3[i], wgb[ks[i]], preferred_element_type=jnp.float32)
                  for i in range(nblk)]
            uu = [jnp.dot(h3[i], wub[ks[i]], preferred_element_type=jnp.float32)
                  for i in range(nblk)]
            act = [(gg[i] * (0.5 + 0.5 * jnp.tanh(0.5 * gg[i])) * uu[i]
                    * gate[i][which]).astype(jnp.bfloat16) for i in range(nblk)]
            ys = [ys[i] + jnp.dot(act[i], wdb[ks[i]], preferred_element_type=jnp.float32)
                  for i in range(nblk)]
        for i in range(nblk):
            o_ref[i * ROW_BLOCK:(i + 1) * ROW_BLOCK, :] = _rms(ys[i], gfin_ref[...])

    @pl.when(s == pl.num_programs(0) - 1)
    def _():
        def drain(e, carry):
            for cp in stage(e):
                cp.wait()
            return carry

        lax.fori_loop(cnt_ref[1], cnt_ref[0], drain, 0)


def _experts(xs, blk_grp, blk_a, blk_b, nvalid, nact, g_ffn, w_r, b_r, g_final, w_gate, w_up, w_down):
    cap, d = xs.shape
    de = w_gate.shape[2]
    step_rows = BLOCKS_PER_STEP * ROW_BLOCK
    steps = cap // step_rows
    pre = lambda f: (lambda s, gr, ea, eb, nv, na: f(s, na))
    const2 = pre(lambda s, na: (0, 0))
    last_step = lambda na: (na[0] - 1) // BLOCKS_PER_STEP
    hbm = pl.BlockSpec(memory_space=pl.ANY)
    active_step = pre(lambda s, na: (jnp.minimum(s, last_step(na)), 0))
    grid_spec = pltpu.PrefetchScalarGridSpec(
        num_scalar_prefetch=5,
        grid=(steps,),
        in_specs=[
            pl.BlockSpec((step_rows, d), active_step),
            pl.BlockSpec(g_ffn.shape, const2),
            pl.BlockSpec(w_r.shape, const2),
            pl.BlockSpec(b_r.shape, const2),
            pl.BlockSpec(g_final.shape, const2),
            hbm, hbm, hbm,
        ],
        out_specs=pl.BlockSpec((step_rows, d), active_step),
        scratch_shapes=[
            pltpu.VMEM((EXPERTS_PER_GROUP, d, de), jnp.bfloat16),
            pltpu.VMEM((EXPERTS_PER_GROUP, d, de), jnp.bfloat16),
            pltpu.VMEM((EXPERTS_PER_GROUP, de, d), jnp.bfloat16),
            pltpu.VMEM((WEIGHT_STAGES, d, de), jnp.float32),
            pltpu.VMEM((WEIGHT_STAGES, d, de), jnp.float32),
            pltpu.VMEM((WEIGHT_STAGES, de, d), jnp.float32),
            pltpu.SemaphoreType.DMA((WEIGHT_STAGES, 3)),
            pltpu.SMEM((2,), jnp.int32),
        ],
    )
    return pl.pallas_call(
        _expert_kernel,
        grid_spec=grid_spec,
        out_shape=jax.ShapeDtypeStruct((cap, d), jnp.float32),
        compiler_params=pltpu.CompilerParams(
            dimension_semantics=("arbitrary",), vmem_limit_bytes=EXPERT_VMEM_LIMIT_BYTES),
        name="experts",
    )(blk_grp, blk_a, blk_b, nvalid, nact, xs, g_ffn, w_r, b_r, g_final, w_gate, w_up, w_down)


def _moe_final(x2, slab, counts_col, g_ffn, w_r, b_r, g_final, w_gate, w_up, w_down):
    t, d = x2.shape
    nb = t // ROW_BLOCK + N_CLASSES + N_GROUPS * (BLOCKS_PER_STEP - 1)
    assert nb % BLOCKS_PER_STEP == 0
    counts = counts_col[:N_CLASSES, 0].astype(jnp.int32)
    nblk = (counts + ROW_BLOCK - 1) // ROW_BLOCK
    grp_blocks = jnp.sum(nblk.reshape(N_GROUPS, N_PAIRS), axis=1)
    nblk = nblk.reshape(N_GROUPS, N_PAIRS).at[:, N_PAIRS - 1].add(
        (-grp_blocks) % BLOCKS_PER_STEP).reshape(N_CLASSES)
    blk_end = jnp.cumsum(nblk)
    blk_start = blk_end - nblk
    nact = blk_end[-1]
    j = jnp.arange(nb, dtype=jnp.int32)
    blk_class = jnp.minimum(
        jnp.sum((blk_end[None, :] <= j[:, None]).astype(jnp.int32), axis=1), N_CLASSES - 1)
    blk_class = jnp.where(j < nact, blk_class, blk_class[jnp.maximum(nact - 1, 0)])
    nvalid = jnp.where(j < nact, jnp.clip(
        counts[blk_class] - (j - blk_start[blk_class]) * ROW_BLOCK, 0, ROW_BLOCK), 0)
    pair_a = jnp.array([p[0] for p in PAIRS], jnp.int32)
    pair_b = jnp.array([p[1] for p in PAIRS], jnp.int32)
    blk_grp = blk_class // N_PAIRS
    blk_a = pair_a[blk_class % N_PAIRS]
    blk_b = pair_b[blk_class % N_PAIRS]
    starts_col = jnp.zeros((ROUTE_LANES, 1), jnp.float32).at[:N_CLASSES, 0].set(
        (blk_start * ROW_BLOCK).astype(jnp.float32))

    xs, dest = _dispatch(x2, slab, starts_col[:, 0], nb * ROW_BLOCK)
    ys = _experts(xs, blk_grp, blk_a, blk_b, nvalid, nact[None], g_ffn, w_r, b_r, g_final,
                  w_gate, w_up, w_down)
    return _unsort(ys, dest, t)


def kernel(x, mem, g_mix, w_in, conv_w, g_v, w_s, b_s, g_out_conv, g_out_gmlp, w_out, g_xattn,
           g_mem, w_q, w_k, w_v, w_o, g_ffn, w_grp, b_grp, w_rt, b_rt, w_gate, w_up, w_down,
           g_final):
    b, s, d = x.shape
    assert g_mix.shape[0] == 1, "the final norm is fused into the single layer's expert kernel"
    assert N_CLASSES <= ROUTE_LANES
    bf = jnp.bfloat16
    kt, v = _kv_proj(mem, g_mem[0][None], w_k[0].astype(bf), w_v[0].astype(bf))

    bias = jnp.repeat(b_s[0].T, GMLP_HEAD_DIM, axis=1)
    x1 = _mixer(x, g_mix[0][None], w_in[0].astype(bf), conv_w[0], g_v[0][None], w_s[0], bias,
                g_out_conv[0][None], g_out_gmlp[0][None], w_out[0].astype(bf))

    pad = ROUTE_LANES - N_GROUPS - N_EXPERTS
    w_r = jnp.concatenate([w_grp[0], w_rt[0], jnp.zeros((d, pad), jnp.float32)], axis=1).astype(bf)
    b_r = jnp.concatenate([b_grp[0], b_rt[0], jnp.zeros((pad,), jnp.float32)])[None]
    gap = ROUTE_EXPERT_ROW0 - N_GROUPS
    tail = ROUTE_LANES - ROUTE_EXPERT_ROW0 - N_EXPERTS
    w_r_t = jnp.concatenate([w_grp[0].T, jnp.zeros((gap, d), jnp.float32), w_rt[0].T,
                             jnp.zeros((tail, d), jnp.float32)], axis=0).astype(bf)
    b_r_t = jnp.concatenate([b_grp[0], jnp.zeros((gap,), jnp.float32), b_rt[0],
                             jnp.zeros((tail,), jnp.float32)])[:, None]
    x2, slab, counts_col = _attn_route(x1, kt, v, g_xattn[0][None], w_q[0].astype(bf),
                                       w_o[0].astype(bf), g_ffn[0][None], w_r_t, b_r_t)
    out = _moe_final(x2.reshape(b * s, d), slab, counts_col, g_ffn[0][None], w_r, b_r,
                     g_final[None], w_gate[0], w_up[0], w_down[0])
    return out.reshape(b, s, d)
```

```python
import functools

import jax
import jax.numpy as jnp
from jax import lax
from jax.experimental import pallas as pl
from jax.experimental.pallas import tpu as pltpu
from jax.experimental.pallas import tpu_sc as plsc

EPS = 1e-6
CONV_GROUP_WIDTH = 512
GMLP_HEADS = 8
GMLP_HEAD_DIM = 64
CHUNK = 128
XA_HEADS = 4
N_GROUPS = 4
EXPERTS_PER_GROUP = 8
N_EXPERTS = N_GROUPS * EXPERTS_PER_GROUP
ROUTE_LANES = 128
EXPERT_LANE0 = N_GROUPS
ROUTE_EXPERT_ROW0 = 8

N_PAIRS = EXPERTS_PER_GROUP * (EXPERTS_PER_GROUP - 1) // 2
N_CLASSES = N_GROUPS * N_PAIRS
PAIRS = [(a, b) for a in range(EXPERTS_PER_GROUP) for b in range(a + 1, EXPERTS_PER_GROUP)]

MIXER_TILE = 1024
ATTN_TILE = 2048
SUB_TILE = 256
ROW_BLOCK = 128
BLOCKS_PER_STEP = 4
WEIGHT_STAGES = 3
SC_CHUNK_ROWS = 16
SC_BUFFERS = 4
VMEM_LIMIT_BYTES = 56 * 1024 * 1024
EXPERT_VMEM_LIMIT_BYTES = 60 * 1024 * 1024

SUBLANES = 8

_NEG = -1e30


def _rms(x, g):
    return x * lax.rsqrt(jnp.mean(x * x, axis=-1, keepdims=True) + EPS) * g


def _gelu_tanh(x):
    return 0.5 * x * (1.0 + jnp.tanh(0.7978845608028654 * (x + 0.044715 * (x * x * x))))


def _kv_kernel(mem_ref, g_ref, wk_ref, wv_ref, kt_ref, v_ref):
    m = _rms(mem_ref[0], g_ref[...]).astype(jnp.bfloat16)
    k = jnp.dot(m, wk_ref[...], preferred_element_type=jnp.float32)
    v = jnp.dot(m, wv_ref[...], preferred_element_type=jnp.float32)
    kt_ref[0] = k.T.astype(jnp.bfloat16)
    v_ref[0] = v.astype(jnp.bfloat16)


def _kv_proj(mem, g_mem, w_k, w_v):
    b, m, d = mem.shape
    const = lambda i: (0, 0)
    return pl.pallas_call(
        _kv_kernel,
        grid=(b,),
        in_specs=[
            pl.BlockSpec((1, m, d), lambda i: (i, 0, 0)),
            pl.BlockSpec((1, d), const),
            pl.BlockSpec((d, d), const),
            pl.BlockSpec((d, d), const),
        ],
        out_specs=[
            pl.BlockSpec((1, d, m), lambda i: (i, 0, 0)),
            pl.BlockSpec((1, m, d), lambda i: (i, 0, 0)),
        ],
        out_shape=[
            jax.ShapeDtypeStruct((b, d, m), jnp.bfloat16),
            jax.ShapeDtypeStruct((b, m, d), jnp.bfloat16),
        ],
        compiler_params=pltpu.CompilerParams(
            dimension_semantics=("arbitrary",), vmem_limit_bytes=VMEM_LIMIT_BYTES),
        name="kv_proj",
    )(mem, g_mem, w_k, w_v)


def _mixer_kernel(x_ref, gmix_ref, win_ref, convw_ref, gv_ref, ws_ref, bias_ref,
                  goc_ref, gog_ref, wout_ref, o_ref, zbuf_ref):
    ts = x_ref.shape[1]
    w = CONV_GROUP_WIDTH
    hw = 2 * GMLP_HEAD_DIM
    nsub = ts // SUB_TILE

    def dot(a, b):
        return jnp.dot(a, b, preferred_element_type=jnp.float32)

    @pl.when(pl.program_id(1) == 0)
    def _():
        zbuf_ref[0:8, :] = jnp.zeros((8, w), jnp.float32)

    @pl.when(pl.program_id(1) != 0)
    def _():
        zbuf_ref[0:8, :] = zbuf_ref[ts:ts + 8, :]

    low = lax.broadcasted_iota(jnp.int32, (SUB_TILE, hw), 1) < GMLP_HEAD_DIM
    row = lax.broadcasted_iota(jnp.int32, (CHUNK, CHUNK), 0)
    colid = lax.broadcasted_iota(jnp.int32, (CHUNK, CHUNK), 1)
    causal = row >= colid
    ws = [jnp.where(causal, ws_ref[hd], 0.0).astype(jnp.bfloat16) for hd in range(GMLP_HEADS)]
    npair = SUB_TILE // CHUNK // 2
    lo = lax.broadcasted_iota(jnp.int32, (CHUNK, hw), 1) < GMLP_HEAD_DIM
    swap = lambda a: pltpu.roll(a, GMLP_HEAD_DIM, axis=1)

    def project(i):
        r0 = i * SUB_TILE
        xt = x_ref[0, r0:r0 + SUB_TILE, :]
        h = _rms(xt, gmix_ref[...]).astype(jnp.bfloat16)
        p = [dot(h, win_ref[:, k * w:(k + 1) * w]) for k in range(5)]
        z = p[1] * p[2]
        zbuf_ref[8 + r0:8 + r0 + SUB_TILE, :] = z
        return dict(xt=xt, gate_b=p[0], z=z, u=p[3], v=p[4], r0=r0)

    def gate_and_norm(st):
        r0 = st["r0"]
        zc = (convw_ref[0:1, :] * zbuf_ref[6 + r0:6 + r0 + SUB_TILE, :]
              + convw_ref[1:2, :] * zbuf_ref[7 + r0:7 + r0 + SUB_TILE, :]
              + convw_ref[2:3, :] * st["z"])
        st["ya"] = _rms(st["gate_b"] * zc, goc_ref[...]).astype(jnp.bfloat16)
        st["u"] = _gelu_tanh(st["u"])
        v = _gelu_tanh(st["v"])
        v2 = v * v
        ss_cols = []
        for k in range(GMLP_HEADS // 2):
            col = v2[:, hw * k:hw * (k + 1)]
            ss_cols.append(jnp.where(low,
                                     jnp.sum(jnp.where(low, col, 0.0), axis=1, keepdims=True),
                                     jnp.sum(jnp.where(low, 0.0, col), axis=1, keepdims=True)))
        ss = jnp.concatenate(ss_cols, axis=1)
        st["vn"] = v * lax.rsqrt(ss * (1.0 / GMLP_HEAD_DIM) + EPS) * gv_ref[...]

    def mix_positions(st):
        vn = st["vn"]
        s_cols = [[None] * (GMLP_HEADS // 2) for _ in range(2 * npair)]
        for hp in range(GMLP_HEADS // 2):
            cols = [vn[c * CHUNK:(c + 1) * CHUNK, hw * hp:hw * (hp + 1)] for c in range(2 * npair)]
            swapped = [swap(a) for a in cols]
            rhs_even = jnp.concatenate(
                [jnp.where(lo, cols[2 * p], swapped[2 * p + 1]) for p in range(npair)], axis=1)
            rhs_odd = jnp.concatenate(
                [jnp.where(lo, swapped[2 * p], cols[2 * p + 1]) for p in range(npair)], axis=1)
            out_e = dot(ws[2 * hp], rhs_even.astype(jnp.bfloat16))
            out_o = dot(ws[2 * hp + 1], rhs_odd.astype(jnp.bfloat16))
            for p in range(npair):
                e = out_e[:, hw * p:hw * (p + 1)]
                o = out_o[:, hw * p:hw * (p + 1)]
                s_cols[2 * p][hp] = jnp.where(lo, e, swap(o))
                s_cols[2 * p + 1][hp] = jnp.where(lo, swap(e), o)
        st["s"] = jnp.concatenate(
            [jnp.concatenate(c, axis=1) + bias_ref[...] for c in s_cols], axis=0)

    def output(st):
        r0 = st["r0"]
        yb = _rms(st["u"] * st["s"], gog_ref[...]).astype(jnp.bfloat16)
        o_ref[0, r0:r0 + SUB_TILE, :] = (st["xt"] + dot(st["ya"], wout_ref[0:w, :])
                                          + dot(yb, wout_ref[w:2 * w, :]))

    phases = (gate_and_norm, mix_positions, output)
    states = []
    for step in range(nsub + len(phases)):
        if step < nsub:
            states.append(project(step))
        for k, phase in enumerate(phases):
            i = step - 1 - k
            if 0 <= i < nsub:
                phase(states[i])


def _mixer(x, g_mix, w_in, conv_w, g_v, w_s, bias, g_oc, g_og, w_out):
    b, s, d = x.shape
    ts = min(MIXER_TILE, s)
    const2 = lambda i, j: (0, 0)
    const3 = lambda i, j: (0, 0, 0)
    return pl.pallas_call(
        _mixer_kernel,
        grid=(b, s // ts),
        in_specs=[
            pl.BlockSpec((1, ts, d), lambda i, j: (i, j, 0)),
            pl.BlockSpec(g_mix.shape, const2),
            pl.BlockSpec(w_in.shape, const2),
            pl.BlockSpec(conv_w.shape, const2),
            pl.BlockSpec(g_v.shape, const2),
            pl.BlockSpec(w_s.shape, const3),
            pl.BlockSpec(bias.shape, const2),
            pl.BlockSpec(g_oc.shape, const2),
            pl.BlockSpec(g_og.shape, const2),
            pl.BlockSpec(w_out.shape, const2),
        ],
        out_specs=pl.BlockSpec((1, ts, d), lambda i, j: (i, j, 0)),
        out_shape=jax.ShapeDtypeStruct((b, s, d), jnp.float32),
        scratch_shapes=[pltpu.VMEM((ts + 8, CONV_GROUP_WIDTH), jnp.float32)],
        compiler_params=pltpu.CompilerParams(
            dimension_semantics=("arbitrary", "arbitrary"), vmem_limit_bytes=VMEM_LIMIT_BYTES),
        name="mixer",
    )(x, g_mix, w_in, conv_w, g_v, w_s, bias, g_oc, g_og, w_out)


def _attn_route_kernel(x_ref, kt_ref, v_ref, gx_ref, wq_ref, wo_ref, gf_ref, wrt_ref, brt_ref,
                       x2_ref, slab_ref, counts_ref, carry_ref):
    ts = x_ref.shape[1]
    d = x_ref.shape[2]
    hd = d // XA_HEADS
    nsub = ts // SUB_TILE
    first = (pl.program_id(0) == 0) & (pl.program_id(1) == 0)

    @pl.when(first)
    def _():
        carry_ref[...] = jnp.zeros_like(carry_ref)

    def dot(a, b):
        return jnp.dot(a, b, preferred_element_type=jnp.float32)

    subs = range(nsub)
    x1 = [x_ref[0, i * SUB_TILE:(i + 1) * SUB_TILE, :] for i in subs]
    h2 = [_rms(x, gx_ref[...]).astype(jnp.bfloat16) for x in x1]
    q = [dot(h, wq_ref[...]).astype(jnp.bfloat16) for h in h2]
    heads = [[] for _ in subs]
    for a in range(XA_HEADS):
        sc = [dot(q[i][:, a * hd:(a + 1) * hd], kt_ref[0, a * hd:(a + 1) * hd, :]) * (hd ** -0.5)
              for i in subs]
        p = [jnp.exp(s_ - jnp.max(s_, axis=-1, keepdims=True)) for s_ in sc]
        l = [jnp.sum(p_, axis=-1, keepdims=True) for p_ in p]
        o = [dot(p[i].astype(jnp.bfloat16), v_ref[0, :, a * hd:(a + 1) * hd]) for i in subs]
        for i in subs:
            heads[i].append((o[i] / l[i]).astype(jnp.bfloat16))
    x2 = [x1[i] + dot(jnp.concatenate(heads[i], axis=1), wo_ref[...]) for i in subs]
    for i in subs:
        x2_ref[0, i * SUB_TILE:(i + 1) * SUB_TILE, :] = x2[i]

    h3 = [_rms(x, gf_ref[...]).astype(jnp.bfloat16) for x in x2]
    lgs = [lax.dot_general(wrt_ref[...], h, (((1,), (1,)), ((), ())),
                           preferred_element_type=jnp.float32) + brt_ref[...] for h in h3]
    sub = lax.broadcasted_iota(jnp.int32, (EXPERTS_PER_GROUP, SUB_TILE), 0).astype(jnp.float32)
    big = float(EXPERTS_PER_GROUP)
    is_g = sub < N_GROUPS

    def classify(lg):
        glog = lg[0:EXPERTS_PER_GROUP, :]
        gmax = jnp.max(jnp.where(is_g, glog, _NEG), axis=0, keepdims=True)
        gidx = jnp.min(jnp.where(is_g & (glog == gmax), sub, big), axis=0, keepdims=True)
        el = lg[ROUTE_EXPERT_ROW0:ROUTE_EXPERT_ROW0 + EXPERTS_PER_GROUP, :]
        for grp in range(1, N_GROUPS):
            r0 = ROUTE_EXPERT_ROW0 + grp * EXPERTS_PER_GROUP
            el = jnp.where(gidx == grp, lg[r0:r0 + EXPERTS_PER_GROUP, :], el)
        t1 = jnp.max(el, axis=0, keepdims=True)
        i1 = jnp.min(jnp.where(el == t1, sub, big), axis=0, keepdims=True)
        rest = sub != i1
        t2 = jnp.max(jnp.where(rest, el, _NEG), axis=0, keepdims=True)
        i2 = jnp.min(jnp.where(rest & (el == t2), sub, big), axis=0, keepdims=True)
        a = jnp.minimum(i1, i2)
        b = jnp.maximum(i1, i2)
        pair = a * (2 * EXPERTS_PER_GROUP - 1 - a) * 0.5 + (b - a - 1.0)
        return gidx * N_PAIRS + pair

    cls = [classify(lg) for lg in lgs]

    r = lax.broadcasted_iota(jnp.int32, (SUB_TILE, SUB_TILE), 0)
    c = lax.broadcasted_iota(jnp.int32, (SUB_TILE, SUB_TILE), 1)
    earlier = jnp.where(r < c, 1.0, 0.0).astype(jnp.bfloat16)
    crow = lax.broadcasted_iota(jnp.int32, (ROUTE_LANES, SUB_TILE), 0).astype(jnp.float32)
    oh = [crow == cl for cl in cls]
    ohf = [jnp.where(o_, 1.0, 0.0) for o_ in oh]
    inside = [dot(o_.astype(jnp.bfloat16), earlier) for o_ in ohf]
    seen = carry_ref[...]
    for i in subs:
        rank = jnp.sum(jnp.where(oh[i], inside[i] + seen, 0.0), axis=0, keepdims=True)
        slab_ref[:, i * SUB_TILE:(i + 1) * SUB_TILE] = jnp.where(
            sub == 0, cls[i], jnp.where(sub == 1, rank, 0.0))
        seen = seen + jnp.sum(ohf[i], axis=1, keepdims=True)
    carry_ref[...] = seen
    counts_ref[...] = seen


def _attn_route(x1, kt, v, g_x, w_q, w_o, g_f, w_rt, b_rt):
    b, s, d = x1.shape
    m = v.shape[1]
    ts = min(ATTN_TILE, s)
    nt = s // ts
    const2 = lambda i, j: (0, 0)
    return pl.pallas_call(
        _attn_route_kernel,
        grid=(b, nt),
        in_specs=[
            pl.BlockSpec((1, ts, d), lambda i, j: (i, j, 0)),
            pl.BlockSpec((1, d, m), lambda i, j: (i, 0, 0)),
            pl.BlockSpec((1, m, d), lambda i, j: (i, 0, 0)),
            pl.BlockSpec(g_x.shape, const2),
            pl.BlockSpec(w_q.shape, const2),
            pl.BlockSpec(w_o.shape, const2),
            pl.BlockSpec(g_f.shape, const2),
            pl.BlockSpec(w_rt.shape, const2),
            pl.BlockSpec(b_rt.shape, const2),
        ],
        out_specs=[
            pl.BlockSpec((1, ts, d), lambda i, j: (i, j, 0)),
            pl.BlockSpec((SUBLANES, ts), lambda i, j: (0, i * nt + j)),
            pl.BlockSpec((ROUTE_LANES, 1), const2),
        ],
        out_shape=[
            jax.ShapeDtypeStruct((b, s, d), jnp.float32),
            jax.ShapeDtypeStruct((SUBLANES, b * s), jnp.float32),
            jax.ShapeDtypeStruct((ROUTE_LANES, 1), jnp.float32),
        ],
        scratch_shapes=[pltpu.VMEM((ROUTE_LANES, 1), jnp.float32)],
        compiler_params=pltpu.CompilerParams(
            dimension_semantics=("arbitrary", "arbitrary"), vmem_limit_bytes=VMEM_LIMIT_BYTES),
        name="attn_route",
    )(x1, kt, v, g_x, w_q, w_o, g_f, w_rt, b_rt)


def _sc_split(t):
    info = plsc.get_sparse_core_info()
    workers = info.num_cores * info.num_subcores
    per_w, rem = divmod(t, workers)
    nchunk, rem2 = divmod(per_w, SC_CHUNK_ROWS)
    assert rem == 0 and rem2 == 0 and nchunk % SC_BUFFERS == 0, (t, workers, SC_CHUNK_ROWS)
    return info.num_cores, per_w, nchunk


def _sc_ring(nchunk, load, store, compute=None):
    ahead = SC_BUFFERS - 1
    for k in range(ahead):
        for cp in load(k, k):
            cp.start()

    @pl.loop(0, nchunk, step=SC_BUFFERS)
    def _(c):
        for b in range(SC_BUFFERS):
            cc = c + b
            refill = (b + ahead) % SC_BUFFERS

            @pl.when(cc >= 1)
            def _():
                for cp in store(cc - 1, refill):
                    cp.wait()

            @pl.when(cc + ahead < nchunk)
            def _():
                for cp in load(cc + ahead, refill):
                    cp.start()

            for cp in load(cc, b):
                cp.wait()
            if compute is not None:
                compute(cc, b)
            for cp in store(cc, b):
                cp.start()

    for cp in store(nchunk - 1, (nchunk - 1) % SC_BUFFERS):
        cp.wait()


def _dispatch(x2, slab, starts, cap):
    t, d = x2.shape
    nc, per_w, nchunk = _sc_split(t)
    ch = SC_CHUNK_ROWS
    mesh = plsc.VectorSubcoreMesh(core_axis_name="c", subcore_axis_name="s")
    dma = pltpu.SemaphoreType.DMA
    n = SC_BUFFERS

    @functools.partial(
        pl.kernel, mesh=mesh,
        out_type=(jax.ShapeDtypeStruct((cap, d), jnp.float32), jax.ShapeDtypeStruct((t,), jnp.int32)),
        compiler_params=pltpu.CompilerParams(needs_layout_passes=False),
        scratch_types=([pltpu.VMEM((ROUTE_LANES,), jnp.float32)]
                       + [pltpu.VMEM((ch,), jnp.float32)] * (2 * n) + [pltpu.VMEM((ch,), jnp.int32)] * n
                       + [pltpu.VMEM((ch, d), jnp.float32)] * n + [dma] * (5 * n)),
    )
    def dispatch_sc(x_hbm, slab_hbm, starts_hbm, xs_hbm, dest_hbm, starts_v, *scratch):
        cls, rank, idx, rows = (scratch[k * n:(k + 1) * n] for k in range(4))
        csem, ksem, rsem, ssem, dsem = (scratch[(4 + k) * n:(5 + k) * n] for k in range(5))
        wid = lax.axis_index("s") * nc + lax.axis_index("c")
        base = wid * per_w
        pltpu.sync_copy(starts_hbm, starts_v)

        def chunk_rows(c):
            return pl.ds((c * (t // per_w) + wid) * ch, ch)

        def load(c, b):
            src = chunk_rows(c)
            return (pltpu.make_async_copy(slab_hbm.at[0, src], cls[b], csem[b]),
                    pltpu.make_async_copy(slab_hbm.at[1, src], rank[b], ksem[b]),
                    pltpu.make_async_copy(x_hbm.at[src], rows[b], rsem[b]))

        def compute(c, b):
            start = plsc.load_gather(starts_v, [cls[b][...].astype(jnp.int32)])
            idx[b][...] = (start + rank[b][...]).astype(jnp.int32)

        def store(c, b):
            return (pltpu.make_async_copy(rows[b], xs_hbm.at[idx[b]], ssem[b]),
                    pltpu.make_async_copy(idx[b], dest_hbm.at[chunk_rows(c)], dsem[b]))

        _sc_ring(nchunk, load, store, compute)

    return dispatch_sc(x2, slab, starts)


def _unsort(ys, dest, t):
    d = ys.shape[1]
    nc, per_w, nchunk = _sc_split(t)
    ch = SC_CHUNK_ROWS
    mesh = plsc.VectorSubcoreMesh(core_axis_name="c", subcore_axis_name="s")
    dma = pltpu.SemaphoreType.DMA

    @functools.partial(
        pl.kernel, mesh=mesh,
        out_type=jax.ShapeDtypeStruct((t, d), jnp.float32),
        scratch_types=([pltpu.VMEM((per_w,), jnp.int32)]
                       + [pltpu.VMEM((ch, d), jnp.float32)] * SC_BUFFERS + [dma] * (2 * SC_BUFFERS)),
    )
    def unsort_sc(ys_hbm, dest_hbm, out_hbm, idx_v, *scratch):
        n = SC_BUFFERS
        rows, gsem, wsem = (scratch[k * n:(k + 1) * n] for k in range(3))
        wid = lax.axis_index("s") * nc + lax.axis_index("c")
        base = wid * per_w
        pltpu.sync_copy(dest_hbm.at[pl.ds(base, per_w)], idx_v)

        def load(c, b):
            return (pltpu.make_async_copy(ys_hbm.at[idx_v.at[pl.ds(c * ch, ch)]], rows[b], gsem[b]),)

        def store(c, b):
            return (pltpu.make_async_copy(rows[b], out_hbm.at[pl.ds(base + c * ch, ch)], wsem[b]),)

        _sc_ring(nchunk, load, store)

    return unsort_sc(ys, dest)


def _expert_kernel(grp_ref, ea_ref, eb_ref, nvalid_ref, nact_ref,
                   xs_ref, gf_ref, wr_ref, br_ref, gfin_ref, wg_hbm, wu_hbm, wd_hbm,
                   o_ref, wgb, wub, wdb, sg, su, sd, sem, cnt_ref):
    s = pl.program_id(0)
    blocks = tuple(BLOCKS_PER_STEP * s + i for i in range(BLOCKS_PER_STEP))
    g = grp_ref[blocks[0]]

    def stage(e):
        slot = e % WEIGHT_STAGES
        return (pltpu.make_async_copy(wg_hbm.at[e], sg.at[slot], sem.at[slot, 0]),
                pltpu.make_async_copy(wu_hbm.at[e], su.at[slot], sem.at[slot, 1]),
                pltpu.make_async_copy(wd_hbm.at[e], sd.at[slot], sem.at[slot, 2]))

    def start_next():
        @pl.when(cnt_ref[0] < N_EXPERTS)
        def _():
            for cp in stage(cnt_ref[0]):
                cp.start()
            cnt_ref[0] = cnt_ref[0] + 1

    @pl.when(s == 0)
    def _():
        cnt_ref[0] = 0
        cnt_ref[1] = 0
        for _ in range(WEIGHT_STAGES):
            start_next()

    active = blocks[0] < nact_ref[0]

    @pl.when(active)
    def _():
        need = g * EXPERTS_PER_GROUP + functools.reduce(
            jnp.maximum, [eb_ref[j] for j in blocks])

        def load(e, carry):
            for cp in stage(e):
                cp.wait()
            slot = e % WEIGHT_STAGES
            k = e % EXPERTS_PER_GROUP
            wgb[k] = sg[slot].astype(jnp.bfloat16)
            wub[k] = su[slot].astype(jnp.bfloat16)
            wdb[k] = sd[slot].astype(jnp.bfloat16)
            cnt_ref[1] = e + 1
            start_next()
            return carry

        lax.fori_loop(cnt_ref[1], need + 1, load, 0)

        lane = lax.broadcasted_iota(jnp.int32, (ROW_BLOCK, ROUTE_LANES), 1)
        is_g = lane < N_GROUPS
        lo = EXPERT_LANE0 + EXPERTS_PER_GROUP * g
        nblk = len(blocks)
        rowid = lax.broadcasted_iota(jnp.int32, (ROW_BLOCK, 1), 0)
        xs = [jnp.where(rowid < nvalid_ref[blocks[i]],
                        xs_ref[i * ROW_BLOCK:(i + 1) * ROW_BLOCK, :], 0.0) for i in range(nblk)]
        h3 = [_rms(x, gf_ref[...]).astype(jnp.bfloat16) for x in xs]
        lg = [jnp.dot(h, wr_ref[...], preferred_element_type=jnp.float32) + br_ref[...]
              for h in h3]

        def gates(lgi, j):
            def pick(idx):
                return jnp.sum(jnp.where(lane == idx, lgi, 0.0), axis=-1, keepdims=True)

            gmax = jnp.max(jnp.where(is_g, lgi, _NEG), axis=-1, keepdims=True)
            den = jnp.sum(jnp.where(is_g, jnp.exp(jnp.where(is_g, lgi, _NEG) - gmax), 0.0),
                          axis=-1, keepdims=True)
            grp_p = jnp.exp(pick(g) - gmax) / den
            la = pick(lo + ea_ref[j])
            lb = pick(lo + eb_ref[j])
            m = jnp.maximum(la, lb)
            pa = jnp.exp(la - m)
            pb = jnp.exp(lb - m)
            return grp_p * pa / (pa + pb), grp_p * pb / (pa + pb)

        gate = [gates(lg[i], blocks[i]) for i in range(nblk)]
        ys = list(xs)
        for which, e_ref in enumerate((ea_ref, eb_ref)):
            ks = [e_ref[j] for j in blocks]
            gg = [jnp.dot(h3[i], wgb[ks[i]], preferred_element_type=jnp.float32)
                  for i in range(nblk)]
            uu = [jnp.dot(h3[i], wub[ks[i]], preferred_element_type=jnp.float32)
                  for i in range(nblk)]
            act = [(gg[i] * (0.5 + 0.5 * jnp.tanh(0.5 * gg[i])) * uu[i]
                    * gate[i][which]).astype(jnp.bfloat16) for i in range(nblk)]
            ys = [ys[i] + jnp.dot(act[i], wdb[ks[i]], preferred_element_type=jnp.float32)
                  for i in range(nblk)]
        for i in range(nblk):
            o_ref[i * ROW_BLOCK:(i + 1) * ROW_BLOCK, :] = _rms(ys[i], gfin_ref[...])

    @pl.when(s == pl.num_programs(0) - 1)
    def _():
        def drain(e, carry):
            for cp in stage(e):
                cp.wait()
            return carry

        lax.fori_loop(cnt_ref[1], cnt_ref[0], drain, 0)


def _experts(xs, blk_grp, blk_a, blk_b, nvalid, nact, g_ffn, w_r, b_r, g_final, w_gate, w_up, w_down):
    cap, d = xs.shape
    de = w_gate.shape[2]
    step_rows = BLOCKS_PER_STEP * ROW_BLOCK
    steps = cap // step_rows
    pre = lambda f: (lambda s, gr, ea, eb, nv, na: f(s, na))
    const2 = pre(lambda s, na: (0, 0))
    last_step = lambda na: (na[0] - 1) // BLOCKS_PER_STEP
    hbm = pl.BlockSpec(memory_space=pl.ANY)
    active_step = pre(lambda s, na: (jnp.minimum(s, last_step(na)), 0))
    grid_spec = pltpu.PrefetchScalarGridSpec(
        num_scalar_prefetch=5,
        grid=(steps,),
        in_specs=[
            pl.BlockSpec((step_rows, d), active_step),
            pl.BlockSpec(g_ffn.shape, const2),
            pl.BlockSpec(w_r.shape, const2),
            pl.BlockSpec(b_r.shape, const2),
            pl.BlockSpec(g_final.shape, const2),
            hbm, hbm, hbm,
        ],
        out_specs=pl.BlockSpec((step_rows, d), active_step),
        scratch_shapes=[
            pltpu.VMEM((EXPERTS_PER_GROUP, d, de), jnp.bfloat16),
            pltpu.VMEM((EXPERTS_PER_GROUP, d, de), jnp.bfloat16),
            pltpu.VMEM((EXPERTS_PER_GROUP, de, d), jnp.bfloat16),
            pltpu.VMEM((WEIGHT_STAGES, d, de), jnp.float32),
            pltpu.VMEM((WEIGHT_STAGES, d, de), jnp.float32),
            pltpu.VMEM((WEIGHT_STAGES, de, d), jnp.float32),
            pltpu.SemaphoreType.DMA((WEIGHT_STAGES, 3)),
            pltpu.SMEM((2,), jnp.int32),
        ],
    )
    return pl.pallas_call(
        _expert_kernel,
        grid_spec=grid_spec,
        out_shape=jax.ShapeDtypeStruct((cap, d), jnp.float32),
        compiler_params=pltpu.CompilerParams(
            dimension_semantics=("arbitrary",), vmem_limit_bytes=EXPERT_VMEM_LIMIT_BYTES),
        name="experts",
    )(blk_grp, blk_a, blk_b, nvalid, nact, xs, g_ffn, w_r, b_r, g_final, w_gate, w_up, w_down)


def _moe_final(x2, slab, counts_col, g_ffn, w_r, b_r, g_final, w_gate, w_up, w_down):
    t, d = x2.shape
    nb = t // ROW_BLOCK + N_CLASSES + N_GROUPS * (BLOCKS_PER_STEP - 1)
    assert nb % BLOCKS_PER_STEP == 0
    counts = counts_col[:N_CLASSES, 0].astype(jnp.int32)
    nblk = (counts + ROW_BLOCK - 1) // ROW_BLOCK
    grp_blocks = jnp.sum(nblk.reshape(N_GROUPS, N_PAIRS), axis=1)
    nblk = nblk.reshape(N_GROUPS, N_PAIRS).at[:, N_PAIRS - 1].add(
        (-grp_blocks) % BLOCKS_PER_STEP).reshape(N_CLASSES)
    blk_end = jnp.cumsum(nblk)
    blk_start = blk_end - nblk
    nact = blk_end[-1]
    j = jnp.arange(nb, dtype=jnp.int32)
    blk_class = jnp.minimum(
        jnp.sum((blk_end[None, :] <= j[:, None]).astype(jnp.int32), axis=1), N_CLASSES - 1)
    blk_class = jnp.where(j < nact, blk_class, blk_class[jnp.maximum(nact - 1, 0)])
    nvalid = jnp.where(j < nact, jnp.clip(
        counts[blk_class] - (j - blk_start[blk_class]) * ROW_BLOCK, 0, ROW_BLOCK), 0)
    pair_a = jnp.array([p[0] for p in PAIRS], jnp.int32)
    pair_b = jnp.array([p[1] for p in PAIRS], jnp.int32)
    blk_grp = blk_class // N_PAIRS
    blk_a = pair_a[blk_class % N_PAIRS]
    blk_b = pair_b[blk_class % N_PAIRS]
    starts_col = jnp.zeros((ROUTE_LANES, 1), jnp.float32).at[:N_CLASSES, 0].set(
        (blk_start * ROW_BLOCK).astype(jnp.float32))

    xs, dest = _dispatch(x2, slab, starts_col[:, 0], nb * ROW_BLOCK)
    ys = _experts(xs, blk_grp, blk_a, blk_b, nvalid, nact[None], g_ffn, w_r, b_r, g_final,
                  w_gate, w_up, w_down)
    return _unsort(ys, dest, t)


def kernel(x, mem, g_mix, w_in, conv_w, g_v, w_s, b_s, g_out_conv, g_out_gmlp, w_out, g_xattn,
           g_mem, w_q, w_k, w_v, w_o, g_ffn, w_grp, b_grp, w_rt, b_rt, w_gate, w_up, w_down,
           g_final):
    b, s, d = x.shape
    assert g_mix.shape[0] == 1, "the final norm is fused into the single layer's expert kernel"
    assert N_CLASSES <= ROUTE_LANES
    bf = jnp.bfloat16
    kt, v = _kv_proj(mem, g_mem[0][None], w_k[0].astype(bf), w_v[0].astype(bf))

    bias = jnp.repeat(b_s[0].T, GMLP_HEAD_DIM, axis=1)
    x1 = _mixer(x, g_mix[0][None], w_in[0].astype(bf), conv_w[0], g_v[0][None], w_s[0], bias,
                g_out_conv[0][None], g_out_gmlp[0][None], w_out[0].astype(bf))

    pad = ROUTE_LANES - N_GROUPS - N_EXPERTS
    w_r = jnp.concatenate([w_grp[0], w_rt[0], jnp.zeros((d, pad), jnp.float32)], axis=1).astype(bf)
    b_r = jnp.concatenate([b_grp[0], b_rt[0], jnp.zeros((pad,), jnp.float32)])[None]
    gap = ROUTE_EXPERT_ROW0 - N_GROUPS
    tail = ROUTE_LANES - ROUTE_EXPERT_ROW0 - N_EXPERTS
    w_r_t = jnp.concatenate([w_grp[0].T, jnp.zeros((gap, d), jnp.float32), w_rt[0].T,
                             jnp.zeros((tail, d), jnp.float32)], axis=0).astype(bf)
    b_r_t = jnp.concatenate([b_grp[0], jnp.zeros((gap,), jnp.float32), b_rt[0],
                             jnp.zeros((tail,), jnp.float32)])[:, None]
    x2, slab, counts_col = _attn_route(x1, kt, v, g_xattn[0][None], w_q[0].astype(bf),
                                       w_o[0].astype(bf), g_ffn[0][None], w_r_t, b_r_t)
    out = _moe_final(x2.reshape(b * s, d), slab, counts_col, g_ffn[0][None], w_r, b_r,
                     g_final[None], w_gate[0], w_up[0], w_down[0])
    return out.reshape(b, s, d)
```

```python
import functools

import jax
import jax.numpy as jnp
from jax import lax
from jax.experimental import pallas as pl
from jax.experimental.pallas import tpu as pltpu
from jax.experimental.pallas import tpu_sc as plsc

EPS = 1e-6
CONV_GROUP_WIDTH = 512
GMLP_HEADS = 8
GMLP_HEAD_DIM = 64
CHUNK = 128
XA_HEADS = 4
N_GROUPS = 4
EXPERTS_PER_GROUP = 8
N_EXPERTS = N_GROUPS * EXPERTS_PER_GROUP
ROUTE_LANES = 128
EXPERT_LANE0 = N_GROUPS
ROUTE_EXPERT_ROW0 = 8

N_PAIRS = EXPERTS_PER_GROUP * (EXPERTS_PER_GROUP - 1) // 2
N_CLASSES = N_GROUPS * N_PAIRS
PAIRS = [(a, b) for a in range(EXPERTS_PER_GROUP) for b in range(a + 1, EXPERTS_PER_GROUP)]

MIXER_TILE = 1024
ATTN_TILE = 2048
SUB_TILE = 256
ROW_BLOCK = 128
BLOCKS_PER_STEP = 4
WEIGHT_STAGES = 3
SC_CHUNK_ROWS = 16
SC_BUFFERS = 4
VMEM_LIMIT_BYTES = 56 * 1024 * 1024
EXPERT_VMEM_LIMIT_BYTES = 60 * 1024 * 1024

SUBLANES = 8

_NEG = -1e30


def _rms(x, g):
    return x * lax.rsqrt(jnp.mean(x * x, axis=-1, keepdims=True) + EPS) * g


def _gelu_tanh(x):
    return 0.5 * x * (1.0 + jnp.tanh(0.7978845608028654 * (x + 0.044715 * (x * x * x))))


def _kv_kernel(mem_ref, g_ref, wk_ref, wv_ref, kt_ref, v_ref):
    m = _rms(mem_ref[0], g_ref[...]).astype(jnp.bfloat16)
    k = jnp.dot(m, wk_ref[...], preferred_element_type=jnp.float32)
    v = jnp.dot(m, wv_ref[...], preferred_element_type=jnp.float32)
    kt_ref[0] = k.T.astype(jnp.bfloat16)
    v_ref[0] = v.astype(jnp.bfloat16)


def _kv_proj(mem, g_mem, w_k, w_v):
    b, m, d = mem.shape
    const = lambda i: (0, 0)
    return pl.pallas_call(
        _kv_kernel,
        grid=(b,),
        in_specs=[
            pl.BlockSpec((1, m, d), lambda i: (i, 0, 0)),
            pl.BlockSpec((1, d), const),
            pl.BlockSpec((d, d), const),
            pl.BlockSpec((d, d), const),
        ],
        out_specs=[
            pl.BlockSpec((1, d, m), lambda i: (i, 0, 0)),
            pl.BlockSpec((1, m, d), lambda i: (i, 0, 0)),
        ],
        out_shape=[
            jax.ShapeDtypeStruct((b, d, m), jnp.bfloat16),
            jax.ShapeDtypeStruct((b, m, d), jnp.bfloat16),
        ],
        compiler_params=pltpu.CompilerParams(
            dimension_semantics=("arbitrary",), vmem_limit_bytes=VMEM_LIMIT_BYTES),
        name="kv_proj",
    )(mem, g_mem, w_k, w_v)


def _mixer_kernel(x_ref, gmix_ref, win_ref, convw_ref, gv_ref, ws_ref, bias_ref,
                  goc_ref, gog_ref, wout_ref, o_ref, zbuf_ref):
    ts = x_ref.shape[1]
    w = CONV_GROUP_WIDTH
    hw = 2 * GMLP_HEAD_DIM
    nsub = ts // SUB_TILE

    def dot(a, b):
        return jnp.dot(a, b, preferred_element_type=jnp.float32)

    @pl.when(pl.program_id(1) == 0)
    def _():
        zbuf_ref[0:8, :] = jnp.zeros((8, w), jnp.float32)

    @pl.when(pl.program_id(1) != 0)
    def _():
        zbuf_ref[0:8, :] = zbuf_ref[ts:ts + 8, :]

    low = lax.broadcasted_iota(jnp.int32, (SUB_TILE, hw), 1) < GMLP_HEAD_DIM
    row = lax.broadcasted_iota(jnp.int32, (CHUNK, CHUNK), 0)
    colid = lax.broadcasted_iota(jnp.int32, (CHUNK, CHUNK), 1)
    causal = row >= colid
    ws = [jnp.where(causal, ws_ref[hd], 0.0).astype(jnp.bfloat16) for hd in range(GMLP_HEADS)]
    npair = SUB_TILE // CHUNK // 2
    lo = lax.broadcasted_iota(jnp.int32, (CHUNK, hw), 1) < GMLP_HEAD_DIM
    swap = lambda a: pltpu.roll(a, GMLP_HEAD_DIM, axis=1)

    def project(i):
        r0 = i * SUB_TILE
        xt = x_ref[0, r0:r0 + SUB_TILE, :]
        h = _rms(xt, gmix_ref[...]).astype(jnp.bfloat16)
        p = [dot(h, win_ref[:, k * w:(k + 1) * w]) for k in range(5)]
        z = p[1] * p[2]
        zbuf_ref[8 + r0:8 + r0 + SUB_TILE, :] = z
        return dict(xt=xt, gate_b=p[0], z=z, u=p[3], v=p[4], r0=r0)

    def gate_and_norm(st):
        r0 = st["r0"]
        zc = (convw_ref[0:1, :] * zbuf_ref[6 + r0:6 + r0 + SUB_TILE, :]
              + convw_ref[1:2, :] * zbuf_ref[7 + r0:7 + r0 + SUB_TILE, :]
              + convw_ref[2:3, :] * st["z"])
        st["ya"] = _rms(st["gate_b"] * zc, goc_ref[...]).astype(jnp.bfloat16)
        st["u"] = _gelu_tanh(st["u"])
        v = _gelu_tanh(st["v"])
        v2 = v * v
        ss_cols = []
        for k in range(GMLP_HEADS // 2):
            col = v2[:, hw * k:hw * (k + 1)]
            ss_cols.append(jnp.where(low,
                                     jnp.sum(jnp.where(low, col, 0.0), axis=1, keepdims=True),
                                     jnp.sum(jnp.where(low, 0.0, col), axis=1, keepdims=True)))
        ss = jnp.concatenate(ss_cols, axis=1)
        st["vn"] = v * lax.rsqrt(ss * (1.0 / GMLP_HEAD_DIM) + EPS) * gv_ref[...]

    def mix_positions(st):
        vn = st["vn"]
        s_cols = [[None] * (GMLP_HEADS // 2) for _ in range(2 * npair)]
        for hp in range(GMLP_HEADS // 2):
            cols = [vn[c * CHUNK:(c + 1) * CHUNK, hw * hp:hw * (hp + 1)] for c in range(2 * npair)]
            swapped = [swap(a) for a in cols]
            rhs_even = jnp.concatenate(
                [jnp.where(lo, cols[2 * p], swapped[2 * p + 1]) for p in range(npair)], axis=1)
            rhs_odd = jnp.concatenate(
                [jnp.where(lo, swapped[2 * p], cols[2 * p + 1]) for p in range(npair)], axis=1)
            out_e = dot(ws[2 * hp], rhs_even.astype(jnp.bfloat16))
            out_o = dot(ws[2 * hp + 1], rhs_odd.astype(jnp.bfloat16))
            for p in range(npair):
                e = out_e[:, hw * p:hw * (p + 1)]
                o = out_o[:, hw * p:hw * (p + 1)]
                s_cols[2 * p][hp] = jnp.where(lo, e, swap(o))
                s_cols[2 * p + 1][hp] = jnp.where(lo, swap(e), o)
        st["s"] = jnp.concatenate(
            [jnp.concatenate(c, axis=1) + bias_ref[...] for c in s_cols], axis=0)

    def output(st):
        r0 = st["r0"]
        yb = _rms(st["u"] * st["s"], gog_ref[...]).astype(jnp.bfloat16)
        o_ref[0, r0:r0 + SUB_TILE, :] = (st["xt"] + dot(st["ya"], wout_ref[0:w, :])
                                          + dot(yb, wout_ref[w:2 * w, :]))

    phases = (gate_and_norm, mix_positions, output)
    states = []
    for step in range(nsub + len(phases)):
        if step < nsub:
            states.append(project(step))
        for k, phase in enumerate(phases):
            i = step - 1 - k
            if 0 <= i < nsub:
                phase(states[i])


def _mixer(x, g_mix, w_in, conv_w, g_v, w_s, bias, g_oc, g_og, w_out):
    b, s, d = x.shape
    ts = min(MIXER_TILE, s)
    const2 = lambda i, j: (0, 0)
    const3 = lambda i, j: (0, 0, 0)
    return pl.pallas_call(
        _mixer_kernel,
        grid=(b, s // ts),
        in_specs=[
            pl.BlockSpec((1, ts, d), lambda i, j: (i, j, 0)),
            pl.BlockSpec(g_mix.shape, const2),
            pl.BlockSpec(w_in.shape, const2),
            pl.BlockSpec(conv_w.shape, const2),
            pl.BlockSpec(g_v.shape, const2),
            pl.BlockSpec(w_s.shape, const3),
            pl.BlockSpec(bias.shape, const2),
            pl.BlockSpec(g_oc.shape, const2),
            pl.BlockSpec(g_og.shape, const2),
            pl.BlockSpec(w_out.shape, const2),
        ],
        out_specs=pl.BlockSpec((1, ts, d), lambda i, j: (i, j, 0)),
        out_shape=jax.ShapeDtypeStruct((b, s, d), jnp.float32),
        scratch_shapes=[pltpu.VMEM((ts + 8, CONV_GROUP_WIDTH), jnp.float32)],
        compiler_params=pltpu.CompilerParams(
            dimension_semantics=("arbitrary", "arbitrary"), vmem_limit_bytes=VMEM_LIMIT_BYTES),
        name="mixer",
    )(x, g_mix, w_in, conv_w, g_v, w_s, bias, g_oc, g_og, w_out)


def _attn_route_kernel(x_ref, kt_ref, v_ref, gx_ref, wq_ref, wo_ref, gf_ref, wrt_ref, brt_ref,
                       x2_ref, slab_ref, counts_ref, carry_ref):
    ts = x_ref.shape[1]
    d = x_ref.shape[2]
    hd = d // XA_HEADS
    nsub = ts // SUB_TILE
    first = (pl.program_id(0) == 0) & (pl.program_id(1) == 0)

    @pl.when(first)
    def _():
        carry_ref[...] = jnp.zeros_like(carry_ref)

    def dot(a, b):
        return jnp.dot(a, b, preferred_element_type=jnp.float32)

    subs = range(nsub)
    x1 = [x_ref[0, i * SUB_TILE:(i + 1) * SUB_TILE, :] for i in subs]
    h2 = [_rms(x, gx_ref[...]).astype(jnp.bfloat16) for x in x1]
    q = [dot(h, wq_ref[...]).astype(jnp.bfloat16) for h in h2]
    heads = [[] for _ in subs]
    for a in range(XA_HEADS):
        sc = [dot(q[i][:, a * hd:(a + 1) * hd], kt_ref[0, a * hd:(a + 1) * hd, :]) * (hd ** -0.5)
              for i in subs]
        p = [jnp.exp(s_ - jnp.max(s_, axis=-1, keepdims=True)) for s_ in sc]
        l = [jnp.sum(p_, axis=-1, keepdims=True) for p_ in p]
        o = [dot(p[i].astype(jnp.bfloat16), v_ref[0, :, a * hd:(a + 1) * hd]) for i in subs]
        for i in subs:
            heads[i].append((o[i] / l[i]).astype(jnp.bfloat16))
    x2 = [x1[i] + dot(jnp.concatenate(heads[i], axis=1), wo_ref[...]) for i in subs]
    for i in subs:
        x2_ref[0, i * SUB_TILE:(i + 1) * SUB_TILE, :] = x2[i]

    h3 = [_rms(x, gf_ref[...]).astype(jnp.bfloat16) for x in x2]
    lgs = [lax.dot_general(wrt_ref[...], h, (((1,), (1,)), ((), ())),
                           preferred_element_type=jnp.float32) + brt_ref[...] for h in h3]
    sub = lax.broadcasted_iota(jnp.int32, (EXPERTS_PER_GROUP, SUB_TILE), 0).astype(jnp.float32)
    big = float(EXPERTS_PER_GROUP)
    is_g = sub < N_GROUPS

    def classify(lg):
        glog = lg[0:EXPERTS_PER_GROUP, :]
        gmax = jnp.max(jnp.where(is_g, glog, _NEG), axis=0, keepdims=True)
        gidx = jnp.min(jnp.where(is_g & (glog == gmax), sub, big), axis=0, keepdims=True)
        el = lg[ROUTE_EXPERT_ROW0:ROUTE_EXPERT_ROW0 + EXPERTS_PER_GROUP, :]
        for grp in range(1, N_GROUPS):
            r0 = ROUTE_EXPERT_ROW0 + grp * EXPERTS_PER_GROUP
            el = jnp.where(gidx == grp, lg[r0:r0 + EXPERTS_PER_GROUP, :], el)
        t1 = jnp.max(el, axis=0, keepdims=True)
        i1 = jnp.min(jnp.where(el == t1, sub, big), axis=0, keepdims=True)
        rest = sub != i1
        t2 = jnp.max(jnp.where(rest, el, _NEG), axis=0, keepdims=True)
        i2 = jnp.min(jnp.where(rest & (el == t2), sub, big), axis=0, keepdims=True)
        a = jnp.minimum(i1, i2)
        b = jnp.maximum(i1, i2)
        pair = a * (2 * EXPERTS_PER_GROUP - 1 - a) * 0.5 + (b - a - 1.0)
        return gidx * N_PAIRS + pair

    cls = [classify(lg) for lg in lgs]

    r = lax.broadcasted_iota(jnp.int32, (SUB_TILE, SUB_TILE), 0)
    c = lax.broadcasted_iota(jnp.int32, (SUB_TILE, SUB_TILE), 1)
    earlier = jnp.where(r < c, 1.0, 0.0).astype(jnp.bfloat16)
    crow = lax.broadcasted_iota(jnp.int32, (ROUTE_LANES, SUB_TILE), 0).astype(jnp.float32)
    oh = [crow == cl for cl in cls]
    ohf = [jnp.where(o_, 1.0, 0.0) for o_ in oh]
    inside = [dot(o_.astype(jnp.bfloat16), earlier) for o_ in ohf]
    seen = carry_ref[...]
    for i in subs:
        rank = jnp.sum(jnp.where(oh[i], inside[i] + seen, 0.0), axis=0, keepdims=True)
        slab_ref[:, i * SUB_TILE:(i + 1) * SUB_TILE] = jnp.where(
            sub == 0, cls[i], jnp.where(sub == 1, rank, 0.0))
        seen = seen + jnp.sum(ohf[i], axis=1, keepdims=True)
    carry_ref[...] = seen
    counts_ref[...] = seen


def _attn_route(x1, kt, v, g_x, w_q, w_o, g_f, w_rt, b_rt):
    b, s, d = x1.shape
    m = v.shape[1]
    ts = min(ATTN_TILE, s)
    nt = s // ts
    const2 = lambda i, j: (0, 0)
    return pl.pallas_call(
        _attn_route_kernel,
        grid=(b, nt),
        in_specs=[
            pl.BlockSpec((1, ts, d), lambda i, j: (i, j, 0)),
            pl.BlockSpec((1, d, m), lambda i, j: (i, 0, 0)),
            pl.BlockSpec((1, m, d), lambda i, j: (i, 0, 0)),
            pl.BlockSpec(g_x.shape, const2),
            pl.BlockSpec(w_q.shape, const2),
            pl.BlockSpec(w_o.shape, const2),
            pl.BlockSpec(g_f.shape, const2),
            pl.BlockSpec(w_rt.shape, const2),
            pl.BlockSpec(b_rt.shape, const2),
        ],
        out_specs=[
            pl.BlockSpec((1, ts, d), lambda i, j: (i, j, 0)),
            pl.BlockSpec((SUBLANES, ts), lambda i, j: (0, i * nt + j)),
            pl.BlockSpec((ROUTE_LANES, 1), const2),
        ],
        out_shape=[
            jax.ShapeDtypeStruct((b, s, d), jnp.float32),
            jax.ShapeDtypeStruct((SUBLANES, b * s), jnp.float32),
            jax.ShapeDtypeStruct((ROUTE_LANES, 1), jnp.float32),
        ],
        scratch_shapes=[pltpu.VMEM((ROUTE_LANES, 1), jnp.float32)],
        compiler_params=pltpu.CompilerParams(
            dimension_semantics=("arbitrary", "arbitrary"), vmem_limit_bytes=VMEM_LIMIT_BYTES),
        name="attn_route",
    )(x1, kt, v, g_x, w_q, w_o, g_f, w_rt, b_rt)


def _sc_split(t):
    info = plsc.get_sparse_core_info()
    workers = info.num_cores * info.num_subcores
    per_w, rem = divmod(t, workers)
    nchunk, rem2 = divmod(per_w, SC_CHUNK_ROWS)
    assert rem == 0 and rem2 == 0 and nchunk % SC_BUFFERS == 0, (t, workers, SC_CHUNK_ROWS)
    return info.num_cores, per_w, nchunk


def _sc_ring(nchunk, load, store, compute=None):
    ahead = SC_BUFFERS - 1
    for k in range(ahead):
        for cp in load(k, k):
            cp.start()

    @pl.loop(0, nchunk, step=SC_BUFFERS)
    def _(c):
        for b in range(SC_BUFFERS):
            cc = c + b
            refill = (b + ahead) % SC_BUFFERS

            @pl.when(cc >= 1)
            def _():
                for cp in store(cc - 1, refill):
                    cp.wait()

            @pl.when(cc + ahead < nchunk)
            def _():
                for cp in load(cc + ahead, refill):
                    cp.start()

            for cp in load(cc, b):
                cp.wait()
            if compute is not None:
                compute(cc, b)
            for cp in store(cc, b):
                cp.start()

    for cp in store(nchunk - 1, (nchunk - 1) % SC_BUFFERS):
        cp.wait()


def _dispatch(x2, slab, starts, cap):
    t, d = x2.shape
    nc, per_w, nchunk = _sc_split(t)
    ch = SC_CHUNK_ROWS
    mesh = plsc.VectorSubcoreMesh(core_axis_name="c", subcore_axis_name="s")
    dma = pltpu.SemaphoreType.DMA
    n = SC_BUFFERS

    @functools.partial(
        pl.kernel, mesh=mesh,
        out_type=(jax.ShapeDtypeStruct((cap, d), jnp.float32), jax.ShapeDtypeStruct((t,), jnp.int32)),
        compiler_params=pltpu.CompilerParams(needs_layout_passes=False),
        scratch_types=([pltpu.VMEM((ROUTE_LANES,), jnp.float32)]
                       + [pltpu.VMEM((ch,), jnp.float32)] * (2 * n) + [pltpu.VMEM((ch,), jnp.int32)] * n
                       + [pltpu.VMEM((ch, d), jnp.float32)] * n + [dma] * (5 * n)),
    )
    def dispatch_sc(x_hbm, slab_hbm, starts_hbm, xs_hbm, dest_hbm, starts_v, *scratch):
        cls, rank, idx, rows = (scratch[k * n:(k + 1) * n] for k in range(4))
        csem, ksem, rsem, ssem, dsem = (scratch[(4 + k) * n:(5 + k) * n] for k in range(5))
        wid = lax.axis_index("s") * nc + lax.axis_index("c")
        base = wid * per_w
        pltpu.sync_copy(starts_hbm, starts_v)

        def chunk_rows(c):
            return pl.ds((c * (t // per_w) + wid) * ch, ch)

        def load(c, b):
            src = chunk_rows(c)
            return (pltpu.make_async_copy(slab_hbm.at[0, src], cls[b], csem[b]),
                    pltpu.make_async_copy(slab_hbm.at[1, src], rank[b], ksem[b]),
                    pltpu.make_async_copy(x_hbm.at[src], rows[b], rsem[b]))

        def compute(c, b):
            start = plsc.load_gather(starts_v, [cls[b][...].astype(jnp.int32)])
            idx[b][...] = (start + rank[b][...]).astype(jnp.int32)

        def store(c, b):
            return (pltpu.make_async_copy(rows[b], xs_hbm.at[idx[b]], ssem[b]),
                    pltpu.make_async_copy(idx[b], dest_hbm.at[chunk_rows(c)], dsem[b]))

        _sc_ring(nchunk, load, store, compute)

    return dispatch_sc(x2, slab, starts)


def _unsort(ys, dest, t):
    d = ys.shape[1]
    nc, per_w, nchunk = _sc_split(t)
    workers = t // per_w
    ch = SC_CHUNK_ROWS
    mesh = plsc.VectorSubcoreMesh(core_axis_name="c", subcore_axis_name="s")
    dma = pltpu.SemaphoreType.DMA

    @functools.partial(
        pl.kernel, mesh=mesh,
        out_type=jax.ShapeDtypeStruct((t, d), jnp.float32),
        scratch_types=([pltpu.VMEM((nchunk, ch), jnp.int32)]
                       + [pltpu.VMEM((ch, d), jnp.float32)] * SC_BUFFERS + [dma] * (2 * SC_BUFFERS)),
    )
    def unsort_sc(ys_hbm, dest_hbm, out_hbm, idx_v, *scratch):
        n = SC_BUFFERS
        rows, gsem, wsem = (scratch[k * n:(k + 1) * n] for k in range(3))
        wid = lax.axis_index("s") * nc + lax.axis_index("c")
        pltpu.sync_copy(dest_hbm.at[:, wid, :], idx_v)

        def load(c, b):
            return (pltpu.make_async_copy(ys_hbm.at[idx_v.at[c]], rows[b], gsem[b]),)

        def store(c, b):
            return (pltpu.make_async_copy(rows[b], out_hbm.at[pl.ds((c * workers + wid) * ch, ch)], wsem[b]),)

        _sc_ring(nchunk, load, store)

    return unsort_sc(ys, dest.reshape(nchunk, workers, ch))


def _expert_kernel(grp_ref, ea_ref, eb_ref, nvalid_ref, nact_ref,
                   xs_ref, gf_ref, wr_ref, br_ref, gfin_ref, wg_hbm, wu_hbm, wd_hbm,
                   o_ref, wgb, wub, wdb, sg, su, sd, sem, cnt_ref):
    s = pl.program_id(0)
    blocks = tuple(BLOCKS_PER_STEP * s + i for i in range(BLOCKS_PER_STEP))
    g = grp_ref[blocks[0]]

    def stage(e):
        slot = e % WEIGHT_STAGES
        return (pltpu.make_async_copy(wg_hbm.at[e], sg.at[slot], sem.at[slot, 0]),
                pltpu.make_async_copy(wu_hbm.at[e], su.at[slot], sem.at[slot, 1]),
                pltpu.make_async_copy(wd_hbm.at[e], sd.at[slot], sem.at[slot, 2]))

    def start_next():
        @pl.when(cnt_ref[0] < N_EXPERTS)
        def _():
            for cp in stage(cnt_ref[0]):
                cp.start()
            cnt_ref[0] = cnt_ref[0] + 1

    @pl.when(s == 0)
    def _():
        cnt_ref[0] = 0
        cnt_ref[1] = 0
        for _ in range(WEIGHT_STAGES):
            start_next()

    active = blocks[0] < nact_ref[0]

    @pl.when(active)
    def _():
        need = g * EXPERTS_PER_GROUP + functools.reduce(
            jnp.maximum, [eb_ref[j] for j in blocks])

        def load(e, carry):
            for cp in stage(e):
                cp.wait()
            slot = e % WEIGHT_STAGES
            k = e % EXPERTS_PER_GROUP
            wgb[k] = sg[slot].astype(jnp.bfloat16)
            wub[k] = su[slot].astype(jnp.bfloat16)
            wdb[k] = sd[slot].astype(jnp.bfloat16)
            cnt_ref[1] = e + 1
            start_next()
            return carry

        lax.fori_loop(cnt_ref[1], need + 1, load, 0)

        lane = lax.broadcasted_iota(jnp.int32, (ROW_BLOCK, ROUTE_LANES), 1)
        is_g = lane < N_GROUPS
        lo = EXPERT_LANE0 + EXPERTS_PER_GROUP * g
        nblk = len(blocks)
        rowid = lax.broadcasted_iota(jnp.int32, (ROW_BLOCK, 1), 0)
        xs = [jnp.where(rowid < nvalid_ref[blocks[i]],
                        xs_ref[i * ROW_BLOCK:(i + 1) * ROW_BLOCK, :], 0.0) for i in range(nblk)]
        h3 = [_rms(x, gf_ref[...]).astype(jnp.bfloat16) for x in xs]
        lg = [jnp.dot(h, wr_ref[...], preferred_element_type=jnp.float32) + br_ref[...]
              for h in h3]

        def gates(lgi, j):
            def pick(idx):
                return jnp.sum(jnp.where(lane == idx, lgi, 0.0), axis=-1, keepdims=True)

            gmax = jnp.max(jnp.where(is_g, lgi, _NEG), axis=-1, keepdims=True)
            den = jnp.sum(jnp.where(is_g, jnp.exp(jnp.where(is_g, lgi, _NEG) - gmax), 0.0),
                          axis=-1, keepdims=True)
            grp_p = jnp.exp(pick(g) - gmax) / den
            la = pick(lo + ea_ref[j])
            lb = pick(lo + eb_ref[j])
            m = jnp.maximum(la, lb)
            pa = jnp.exp(la - m)
            pb = jnp.exp(lb - m)
            return grp_p * pa / (pa + pb), grp_p * pb / (pa + pb)

        gate = [gates(lg[i], blocks[i]) for i in range(nblk)]
        ys = list(xs)
        for which, e_ref in enumerate((ea_ref, eb_ref)):
            ks = [e_ref[j] for j in blocks]
            gg = [jnp.dot(h3[i], wgb[ks[i]], preferred_element_type=jnp.float32)
                  for i in range(nblk)]
            uu = [jnp.dot(h3[i], wub[ks[i]], preferred_element_type=jnp.float32)
                  for i in range(nblk)]
            act = [(gg[i] * (0.5 + 0.5 * jnp.tanh(0.5 * gg[i])) * uu[i]
                    * gate[i][which]).astype(jnp.bfloat16) for i in range(nblk)]
            ys = [ys[i] + jnp.dot(act[i], wdb[ks[i]], preferred_element_type=jnp.float32)
                  for i in range(nblk)]
        for i in range(nblk):
            o_ref[i * ROW_BLOCK:(i + 1) * ROW_BLOCK, :] = _rms(ys[i], gfin_ref[...])

    @pl.when(s == pl.num_programs(0) - 1)
    def _():
        def drain(e, carry):
            for cp in stage(e):
                cp.wait()
            return carry

        lax.fori_loop(cnt_ref[1], cnt_ref[0], drain, 0)


def _experts(xs, blk_grp, blk_a, blk_b, nvalid, nact, g_ffn, w_r, b_r, g_final, w_gate, w_up, w_down):
    cap, d = xs.shape
    de = w_gate.shape[2]
    step_rows = BLOCKS_PER_STEP * ROW_BLOCK
    steps = cap // step_rows
    pre = lambda f: (lambda s, gr, ea, eb, nv, na: f(s, na))
    const2 = pre(lambda s, na: (0, 0))
    last_step = lambda na: (na[0] - 1) // BLOCKS_PER_STEP
    hbm = pl.BlockSpec(memory_space=pl.ANY)
    active_step = pre(lambda s, na: (jnp.minimum(s, last_step(na)), 0))
    grid_spec = pltpu.PrefetchScalarGridSpec(
        num_scalar_prefetch=5,
        grid=(steps,),
        in_specs=[
            pl.BlockSpec((step_rows, d), active_step),
            pl.BlockSpec(g_ffn.shape, const2),
            pl.BlockSpec(w_r.shape, const2),
            pl.BlockSpec(b_r.shape, const2),
            pl.BlockSpec(g_final.shape, const2),
            hbm, hbm, hbm,
        ],
        out_specs=pl.BlockSpec((step_rows, d), active_step),
        scratch_shapes=[
            pltpu.VMEM((EXPERTS_PER_GROUP, d, de), jnp.bfloat16),
            pltpu.VMEM((EXPERTS_PER_GROUP, d, de), jnp.bfloat16),
            pltpu.VMEM((EXPERTS_PER_GROUP, de, d), jnp.bfloat16),
            pltpu.VMEM((WEIGHT_STAGES, d, de), jnp.float32),
            pltpu.VMEM((WEIGHT_STAGES, d, de), jnp.float32),
            pltpu.VMEM((WEIGHT_STAGES, de, d), jnp.float32),
            pltpu.SemaphoreType.DMA((WEIGHT_STAGES, 3)),
            pltpu.SMEM((2,), jnp.int32),
        ],
    )
    return pl.pallas_call(
        _expert_kernel,
        grid_spec=grid_spec,
        out_shape=jax.ShapeDtypeStruct((cap, d), jnp.float32),
        compiler_params=pltpu.CompilerParams(
            dimension_semantics=("arbitrary",), vmem_limit_bytes=EXPERT_VMEM_LIMIT_BYTES),
        name="experts",
    )(blk_grp, blk_a, blk_b, nvalid, nact, xs, g_ffn, w_r, b_r, g_final, w_gate, w_up, w_down)


def _moe_final(x2, slab, counts_col, g_ffn, w_r, b_r, g_final, w_gate, w_up, w_down):
    t, d = x2.shape
    nb = t // ROW_BLOCK + N_CLASSES + N_GROUPS * (BLOCKS_PER_STEP - 1)
    assert nb % BLOCKS_PER_STEP == 0
    counts = counts_col[:N_CLASSES, 0].astype(jnp.int32)
    nblk = (counts + ROW_BLOCK - 1) // ROW_BLOCK
    grp_blocks = jnp.sum(nblk.reshape(N_GROUPS, N_PAIRS), axis=1)
    nblk = nblk.reshape(N_GROUPS, N_PAIRS).at[:, N_PAIRS - 1].add(
        (-grp_blocks) % BLOCKS_PER_STEP).reshape(N_CLASSES)
    blk_end = jnp.cumsum(nblk)
    blk_start = blk_end - nblk
    nact = blk_end[-1]
    j = jnp.arange(nb, dtype=jnp.int32)
    blk_class = jnp.minimum(
        jnp.sum((blk_end[None, :] <= j[:, None]).astype(jnp.int32), axis=1), N_CLASSES - 1)
    blk_class = jnp.where(j < nact, blk_class, blk_class[jnp.maximum(nact - 1, 0)])
    nvalid = jnp.where(j < nact, jnp.clip(
        counts[blk_class] - (j - blk_start[blk_class]) * ROW_BLOCK, 0, ROW_BLOCK), 0)
    pair_a = jnp.array([p[0] for p in PAIRS], jnp.int32)
    pair_b = jnp.array([p[1] for p in PAIRS], jnp.int32)
    blk_grp = blk_class // N_PAIRS
    blk_a = pair_a[blk_class % N_PAIRS]
    blk_b = pair_b[blk_class % N_PAIRS]
    starts_col = jnp.zeros((ROUTE_LANES, 1), jnp.float32).at[:N_CLASSES, 0].set(
        (blk_start * ROW_BLOCK).astype(jnp.float32))

    xs, dest = _dispatch(x2, slab, starts_col[:, 0], nb * ROW_BLOCK)
    ys = _experts(xs, blk_grp, blk_a, blk_b, nvalid, nact[None], g_ffn, w_r, b_r, g_final,
                  w_gate, w_up, w_down)
    return _unsort(ys, dest, t)


def kernel(x, mem, g_mix, w_in, conv_w, g_v, w_s, b_s, g_out_conv, g_out_gmlp, w_out, g_xattn,
           g_mem, w_q, w_k, w_v, w_o, g_ffn, w_grp, b_grp, w_rt, b_rt, w_gate, w_up, w_down,
           g_final):
    b, s, d = x.shape
    assert g_mix.shape[0] == 1, "the final norm is fused into the single layer's expert kernel"
    assert N_CLASSES <= ROUTE_LANES
    bf = jnp.bfloat16
    kt, v = _kv_proj(mem, g_mem[0][None], w_k[0].astype(bf), w_v[0].astype(bf))

    bias = jnp.repeat(b_s[0].T, GMLP_HEAD_DIM, axis=1)
    x1 = _mixer(x, g_mix[0][None], w_in[0].astype(bf), conv_w[0], g_v[0][None], w_s[0], bias,
                g_out_conv[0][None], g_out_gmlp[0][None], w_out[0].astype(bf))

    pad = ROUTE_LANES - N_GROUPS - N_EXPERTS
    w_r = jnp.concatenate([w_grp[0], w_rt[0], jnp.zeros((d, pad), jnp.float32)], axis=1).astype(bf)
    b_r = jnp.concatenate([b_grp[0], b_rt[0], jnp.zeros((pad,), jnp.float32)])[None]
    gap = ROUTE_EXPERT_ROW0 - N_GROUPS
    tail = ROUTE_LANES - ROUTE_EXPERT_ROW0 - N_EXPERTS
    w_r_t = jnp.concatenate([w_grp[0].T, jnp.zeros((gap, d), jnp.float32), w_rt[0].T,
                             jnp.zeros((tail, d), jnp.float32)], axis=0).astype(bf)
    b_r_t = jnp.concatenate([b_grp[0], jnp.zeros((gap,), jnp.float32), b_rt[0],
                             jnp.zeros((tail,), jnp.float32)])[:, None]
    x2, slab, counts_col = _attn_route(x1, kt, v, g_xattn[0][None], w_q[0].astype(bf),
                                       w_o[0].astype(bf), g_ffn[0][None], w_r_t, b_r_t)
    out = _moe_final(x2.reshape(b * s, d), slab, counts_col, g_ffn[0][None], w_r, b_r,
                     g_final[None], w_gate[0], w_up[0], w_down[0])
    return out.reshape(b, s, d)
```
